```python
import math
import jax, jax.numpy as jnp
from jax import lax
import numpy as np

D_MODEL = 1024
BATCH = 2
SEQ = 8192
DEPTH = 2

HEAD_DIM = 64
Q_BLOCK = 128
NSA_HEADS = 8
NSA_KV_HEADS = 2
CMP_LEN = 32
CMP_STRIDE = 16
CMP_HIDDEN = 256
SLC_LEN = 64
SLC_TOPN = 16
NSA_WINDOW = 512
FORCE_SCORE = 1e4
SB_HEADS = 4
SWA_HEADS = 4
SWA_KV_HEADS = 2
SWA_WINDOW = 128
NUM_BUCKETS = 32
MAX_DISTANCE = 128
N_BIAS_HEADS = NSA_HEADS + SWA_HEADS
D_FF = 2816
NORM_EPS = 1e-6
NEG_INF = -1e30
N_BRANCHES = 3

SPLIT_SIZES = (
    NSA_HEADS * HEAD_DIM,
    NSA_KV_HEADS * HEAD_DIM, NSA_KV_HEADS * HEAD_DIM,
    NSA_KV_HEADS * HEAD_DIM, NSA_KV_HEADS * HEAD_DIM,
    NSA_KV_HEADS * HEAD_DIM, NSA_KV_HEADS * HEAD_DIM,
    3 * NSA_HEADS,
    SB_HEADS * HEAD_DIM, SB_HEADS * HEAD_DIM, SB_HEADS * HEAD_DIM,
    SWA_HEADS * HEAD_DIM, SWA_KV_HEADS * HEAD_DIM, SWA_KV_HEADS * HEAD_DIM,
    N_BRANCHES * D_MODEL,
)
D_IN = sum(SPLIT_SIZES)
SPLIT_POINTS = tuple(int(p) for p in np.cumsum(SPLIT_SIZES)[:-1])

kernel_name = "hybrid_nsa_stickbreak_swasink_macaron"


def rms_norm(x, w):
    xf = x.astype(jnp.float32)
    y = xf * lax.rsqrt(jnp.mean(xf * xf, axis=-1, keepdims=True) + NORM_EPS)
    return (y * w.astype(jnp.float32)).astype(x.dtype)


def swiglu(x, w_gate, w_up, w_down):
    return (jax.nn.silu(x @ w_gate) * (x @ w_up)) @ w_down


def t5_bucket(dist):
    max_exact = NUM_BUCKETS // 2
    d = jnp.maximum(dist, 0)
    df = jnp.maximum(d, 1).astype(jnp.float32)
    large = max_exact + (jnp.log(df / max_exact) / math.log(MAX_DISTANCE / max_exact)
                         * (NUM_BUCKETS - max_exact)).astype(jnp.int32)
    large = jnp.minimum(large, NUM_BUCKETS - 1)
    return jnp.where(d < max_exact, d, large)


def masked_softmax(logits, mask):
    p = jax.nn.softmax(jnp.where(mask, logits, NEG_INF), axis=-1)
    return p * mask


def nsa_attention(q, k_cmp, v_cmp, k_slc, v_slc, k_win, v_win, gate_logits,
                  q_norm, k_norm, cmp_pos, cmp_k_w1, cmp_k_w2, cmp_v_w1, cmp_v_w2, bias_table):
    B, S = q.shape[0], q.shape[1]
    G, R, Dh = NSA_KV_HEADS, NSA_HEADS // NSA_KV_HEADS, HEAD_DIM
    scale = Dh ** -0.5
    qg = rms_norm(q, q_norm).reshape(B, S, G, R, Dh).transpose(0, 2, 3, 1, 4)
    gates = jax.nn.sigmoid(gate_logits.astype(jnp.float32)).reshape(B, S, 3, G, R)

    n_cmp = (S - CMP_LEN) // CMP_STRIDE + 1
    cmp_idx = np.arange(n_cmp)[:, None] * CMP_STRIDE + np.arange(CMP_LEN)[None, :]
    cmp_end = jnp.asarray(cmp_idx[:, -1], jnp.int32)

    def compress(t, w1, w2):
        blocks = t[:, cmp_idx] + cmp_pos[:, None, :]
        blocks = blocks.transpose(0, 3, 1, 2, 4).reshape(B, G, n_cmp, CMP_LEN * Dh)
        return jax.nn.silu(blocks @ w1) @ w2

    kc = rms_norm(compress(k_cmp, cmp_k_w1, cmp_k_w2), k_norm)
    vc = compress(v_cmp, cmp_v_w1, cmp_v_w2)

    n_slc = S // SLC_LEN
    slc_start = np.arange(n_slc) * SLC_LEN
    overlap = jnp.asarray((cmp_idx[:, :1] < slc_start[None, :] + SLC_LEN)
                          & (cmp_idx[:, -1:] >= slc_start[None, :]), jnp.float32)
    top_n = min(SLC_TOPN, n_slc)
    ks = rms_norm(k_slc, k_norm).transpose(0, 2, 1, 3).reshape(B, G, n_slc, SLC_LEN, Dh)
    vs = v_slc.transpose(0, 2, 1, 3).reshape(B, G, n_slc, SLC_LEN, Dh)

    pad = ((0, 0), (0, 0), (NSA_WINDOW, 0), (0, 0))
    kw = jnp.pad(rms_norm(k_win, k_norm).transpose(0, 2, 1, 3), pad)
    vw = jnp.pad(v_win.transpose(0, 2, 1, 3), pad)

    head_bias = bias_table[:, :NSA_HEADS].T.reshape(G, R, NUM_BUCKETS).astype(jnp.float32)
    b_idx = jnp.arange(B)[:, None, None, None]
    g_idx = jnp.arange(G)[None, :, None, None]
    g5 = jnp.arange(G)[None, :, None, None, None]
    r5 = jnp.arange(R)[None, None, :, None, None]
    slc_ids = jnp.arange(n_slc)[None, :]

    def block(i):
        q0 = i * Q_BLOCK
        t = q0 + jnp.arange(Q_BLOCK)
        qb = lax.dynamic_slice_in_dim(qg, q0, Q_BLOCK, axis=3)

        dist_c = t[:, None] - cmp_end[None, :]
        s_c = (jnp.einsum('bgrqd,bgnd->bgrqn', qb, kc).astype(jnp.float32) * scale
               + head_bias[:, :, t5_bucket(dist_c)])
        p_c = masked_softmax(s_c, dist_c >= 0)
        o_c = jnp.einsum('bgrqn,bgnd->bgrqd', p_c.astype(vc.dtype), vc)

        imp = jnp.einsum('bgrqn,nj->bgqj', p_c, overlap)
        cur = (t // SLC_LEN)[:, None]
        forced = (slc_ids == 0) | (slc_ids == cur) | (slc_ids == cur - 1)
        score = jnp.where(forced, FORCE_SCORE, jnp.where(slc_ids <= cur, imp, -1.0))
        _, sel = lax.top_k(score, top_n)
        kg = ks[b_idx, g_idx, sel].reshape(B, G, Q_BLOCK, top_n * SLC_LEN, Dh)
        vg = vs[b_idx, g_idx, sel].reshape(B, G, Q_BLOCK, top_n * SLC_LEN, Dh)
        pos = (sel[..., None] * SLC_LEN + jnp.arange(SLC_LEN)).reshape(B, G, Q_BLOCK, top_n * SLC_LEN)
        dist_s = t[None, None, :, None] - pos
        s_s = (jnp.einsum('bgrqd,bgqkd->bgrqk', qb, kg).astype(jnp.float32) * scale
               + head_bias[g5, r5, t5_bucket(dist_s)[:, :, None]])
        p_s = masked_softmax(s_s, (dist_s >= 0)[:, :, None])
        o_s = jnp.einsum('bgrqk,bgqkd->bgrqd', p_s.astype(vg.dtype), vg)

        kb = lax.dynamic_slice_in_dim(kw, q0, Q_BLOCK + NSA_WINDOW, axis=2)
        vb = lax.dynamic_slice_in_dim(vw, q0, Q_BLOCK + NSA_WINDOW, axis=2)
        key_pos = q0 - NSA_WINDOW + jnp.arange(Q_BLOCK + NSA_WINDOW)
        dist_w = t[:, None] - key_pos[None, :]
        mask_w = (dist_w >= 0) & (dist_w < NSA_WINDOW) & (key_pos[None, :] >= 0)
        s_w = (jnp.einsum('bgrqd,bgkd->bgrqk', qb, kb).astype(jnp.float32) * scale
               + head_bias[:, :, t5_bucket(dist_w)])
        p_w = masked_softmax(s_w, mask_w)
        o_w = jnp.einsum('bgrqk,bgkd->bgrqd', p_w.astype(vb.dtype), vb)

        g = lax.dynamic_slice_in_dim(gates, q0, Q_BLOCK, axis=1).transpose(2, 0, 3, 4, 1)[..., None]
        return (g[0] * o_c + g[1] * o_s + g[2] * o_w).astype(q.dtype)

    o = lax.map(block, jnp.arange(S // Q_BLOCK))
    return o.transpose(1, 0, 4, 2, 3, 5).reshape(B, S, NSA_HEADS * Dh)


def stick_breaking_attention(q, k, v):
    B, S = q.shape[0], q.shape[1]
    scale = HEAD_DIM ** -0.5
    qh = q.transpose(0, 2, 1, 3)
    kh = k.transpose(0, 2, 1, 3)
    vh = v.transpose(0, 2, 1, 3)
    key_pos = jnp.arange(S)

    def block(i):
        q0 = i * Q_BLOCK
        t = q0 + jnp.arange(Q_BLOCK)
        qb = lax.dynamic_slice_in_dim(qh, q0, Q_BLOCK, axis=2)
        z = jnp.einsum('bhqd,bhkd->bhqk', qb, kh).astype(jnp.float32) * scale
        mask = key_pos[None, :] < t[:, None]
        log_fail = jnp.where(mask, jax.nn.log_sigmoid(-z), 0.0)
        after = lax.cumsum(log_fail, axis=3, reverse=True) - log_fail
        w = jnp.where(mask, jnp.exp(jax.nn.log_sigmoid(z) + after), 0.0)
        return jnp.einsum('bhqk,bhkd->bhqd', w.astype(vh.dtype), vh)

    o = lax.map(block, jnp.arange(S // Q_BLOCK))
    return o.transpose(1, 0, 3, 2, 4).reshape(B, S, SB_HEADS * HEAD_DIM)


def swa_sink_attention(q, k, v, q_norm, k_norm, sinks, bias_table):
    B, S = q.shape[0], q.shape[1]
    G, R, Dh = SWA_KV_HEADS, SWA_HEADS // SWA_KV_HEADS, HEAD_DIM
    scale = Dh ** -0.5
    qg = rms_norm(q, q_norm).reshape(B, S, G, R, Dh).transpose(0, 2, 3, 1, 4)
    pad = ((0, 0), (0, 0), (SWA_WINDOW, 0), (0, 0))
    kp = jnp.pad(rms_norm(k, k_norm).transpose(0, 2, 1, 3), pad)
    vp = jnp.pad(v.transpose(0, 2, 1, 3), pad)
    head_bias = bias_table[:, NSA_HEADS:].T.reshape(G, R, NUM_BUCKETS).astype(jnp.float32)
    sink = sinks.astype(jnp.float32).reshape(1, G, R, 1, 1)

    def block(i):
        q0 = i * Q_BLOCK
        t = q0 + jnp.arange(Q_BLOCK)
        qb = lax.dynamic_slice_in_dim(qg, q0, Q_BLOCK, axis=3)
        kb = lax.dynamic_slice_in_dim(kp, q0, Q_BLOCK + SWA_WINDOW, axis=2)
        vb = lax.dynamic_slice_in_dim(vp, q0, Q_BLOCK + SWA_WINDOW, axis=2)
        key_pos = q0 - SWA_WINDOW + jnp.arange(Q_BLOCK + SWA_WINDOW)
        dist = t[:, None] - key_pos[None, :]
        mask = (dist >= 0) & (dist < SWA_WINDOW) & (key_pos[None, :] >= 0)
        s = (jnp.einsum('bgrqd,bgkd->bgrqk', qb, kb).astype(jnp.float32) * scale
             + head_bias[:, :, t5_bucket(dist)])
        s = jnp.where(mask, s, NEG_INF)
        m = jnp.maximum(jnp.max(s, axis=-1, keepdims=True), sink)
        e = jnp.exp(s - m)
        p = e / (jnp.sum(e, axis=-1, keepdims=True) + jnp.exp(sink - m))
        return jnp.einsum('bgrqk,bgkd->bgrqd', p.astype(vb.dtype), vb)

    o = lax.map(block, jnp.arange(S // Q_BLOCK))
    return o.transpose(1, 0, 4, 2, 3, 5).reshape(B, S, SWA_HEADS * Dh)


def setup_inputs(seed: int = 0) -> dict:
    key = jax.random.key(seed)
    ks = jax.random.split(key, 32)
    L, D, F = DEPTH, D_MODEL, D_FF

    def w(k, shape, fan_in):
        return jax.random.normal(k, shape, jnp.float32) * fan_in ** -0.5

    def gain(k, shape):
        return 1.0 + 0.02 * jax.random.normal(k, shape, jnp.float32)

    return {
        "x": jax.random.normal(ks[0], (BATCH, SEQ, D), jnp.float32),
        "rel_bias": 0.5 * jax.random.normal(ks[1], (NUM_BUCKETS, N_BIAS_HEADS), jnp.float32),
        "ffn1_norm": gain(ks[2], (L, D)),
        "ffn1_w_gate": w(ks[3], (L, D, F), D),
        "ffn1_w_up": w(ks[4], (L, D, F), D),
        "ffn1_w_down": w(ks[5], (L, F, D), F),
        "mix_norm": gain(ks[6], (L, D)),
        "w_in": w(ks[7], (L, D, D_IN), D),
        "nsa_q_norm": gain(ks[8], (L, HEAD_DIM)),
        "nsa_k_norm": gain(ks[9], (L, HEAD_DIM)),
        "nsa_cmp_pos": 0.5 * jax.random.normal(ks[10], (L, CMP_LEN, HEAD_DIM), jnp.float32),
        "nsa_cmp_k_w1": w(ks[11], (L, CMP_LEN * HEAD_DIM, CMP_HIDDEN), CMP_LEN * HEAD_DIM),
        "nsa_cmp_k_w2": w(ks[12], (L, CMP_HIDDEN, HEAD_DIM), CMP_HIDDEN),
        "nsa_cmp_v_w1": w(ks[13], (L, CMP_LEN * HEAD_DIM, CMP_HIDDEN), CMP_LEN * HEAD_DIM),
        "nsa_cmp_v_w2": w(ks[14], (L, CMP_HIDDEN, HEAD_DIM), CMP_HIDDEN),
        "swa_q_norm": gain(ks[15], (L, HEAD_DIM)),
        "swa_k_norm": gain(ks[16], (L, HEAD_DIM)),
        "swa_sinks": jax.random.normal(ks[17], (L, SWA_HEADS), jnp.float32),
        "w_up_nsa": w(ks[18], (L, NSA_HEADS * HEAD_DIM, D), NSA_HEADS * HEAD_DIM),
        "w_up_sb": w(ks[19], (L, SB_HEADS * HEAD_DIM, D), SB_HEADS * HEAD_DIM),
        "w_up_swa": w(ks[20], (L, SWA_HEADS * HEAD_DIM, D), SWA_HEADS * HEAD_DIM),
        "w_out": w(ks[21], (L, D, D), D),
        "ffn2_norm": gain(ks[22], (L, D)),
        "ffn2_w_gate": w(ks[23], (L, D, F), D),
        "ffn2_w_up": w(ks[24], (L, D, F), D),
        "ffn2_w_down": w(ks[25], (L, F, D), F),
    }


def reference(x, rel_bias, ffn1_norm, ffn1_w_gate, ffn1_w_up, ffn1_w_down, mix_norm, w_in,
              nsa_q_norm, nsa_k_norm, nsa_cmp_pos, nsa_cmp_k_w1, nsa_cmp_k_w2, nsa_cmp_v_w1,
              nsa_cmp_v_w2, swa_q_norm, swa_k_norm, swa_sinks, w_up_nsa, w_up_sb, w_up_swa, w_out,
              ffn2_norm, ffn2_w_gate, ffn2_w_up, ffn2_w_down):
    B, S = x.shape[0], x.shape[1]

    def heads(t, n):
        return t.reshape(B, S, n, HEAD_DIM)

    for l in range(DEPTH):
        h = rms_norm(x, ffn1_norm[l])
        x = x + 0.5 * swiglu(h, ffn1_w_gate[l], ffn1_w_up[l], ffn1_w_down[l])

        h = rms_norm(x, mix_norm[l])
        (nq, nkc, nvc, nks, nvs, nkw, nvw, ngate, sq, sk, sv, wq, wk, wv, bgate) = jnp.split(
            h @ w_in[l], SPLIT_POINTS, axis=-1)
        y_nsa = nsa_attention(heads(nq, NSA_HEADS), heads(nkc, NSA_KV_HEADS), heads(nvc, NSA_KV_HEADS),
                              heads(nks, NSA_KV_HEADS), heads(nvs, NSA_KV_HEADS),
                              heads(nkw, NSA_KV_HEADS), heads(nvw, NSA_KV_HEADS), ngate,
                              nsa_q_norm[l], nsa_k_norm[l], nsa_cmp_pos[l], nsa_cmp_k_w1[l],
                              nsa_cmp_k_w2[l], nsa_cmp_v_w1[l], nsa_cmp_v_w2[l], rel_bias)
        y_sb = stick_breaking_attention(heads(sq, SB_HEADS), heads(sk, SB_HEADS), heads(sv, SB_HEADS))
        y_swa = swa_sink_attention(heads(wq, SWA_HEADS), heads(wk, SWA_KV_HEADS), heads(wv, SWA_KV_HEADS),
                                   swa_q_norm[l], swa_k_norm[l], swa_sinks[l], rel_bias)

        g = jax.nn.sigmoid(bgate.astype(jnp.float32)).reshape(B, S, N_BRANCHES, D_MODEL).astype(x.dtype)
        merged = (g[:, :, 0] * (y_nsa @ w_up_nsa[l]) + g[:, :, 1] * (y_sb @ w_up_sb[l])
                  + g[:, :, 2] * (y_swa @ w_up_swa[l]))
        x = x + merged @ w_out[l]

        h = rms_norm(x, ffn2_norm[l])
        x = x + 0.5 * swiglu(h, ffn2_w_gate[l], ffn2_w_up[l], ffn2_w_down[l])
    return x
```

```python
import functools
import math

import numpy as np
import jax
import jax.numpy as jnp
from jax import lax
from jax.experimental import pallas as pl
from jax.experimental.pallas import tpu as pltpu

D_MODEL = 1024
HEAD_DIM = 64
Q_BLOCK = 128
NSA_HEADS = 8
NSA_KV_HEADS = 2
NSA_REP = NSA_HEADS // NSA_KV_HEADS
CMP_LEN = 32
CMP_STRIDE = 16
CMP_HIDDEN = 256
SLC_LEN = 64
SLC_TOPN = 16
NSA_WINDOW = 512
FORCE_SCORE = 1e4
SB_HEADS = 4
SWA_HEADS = 4
SWA_KV_HEADS = 2
SWA_REP = SWA_HEADS // SWA_KV_HEADS
SWA_WINDOW = 128
NUM_BUCKETS = 32
MAX_DISTANCE = 128
D_FF = 2816
NORM_EPS = 1e-6
NEG_INF = -1e30
N_BRANCHES = 3
N_GATE = 3 * NSA_HEADS

LANES = 128
VMEM_LIMIT = 56 * 1024 * 1024
BF16 = jnp.bfloat16
F32 = jnp.float32

_sizes = (NSA_HEADS * HEAD_DIM,) + (NSA_KV_HEADS * HEAD_DIM,) * 6 + (N_GATE,) \
    + (SB_HEADS * HEAD_DIM,) * 3 + (SWA_HEADS * HEAD_DIM, SWA_KV_HEADS * HEAD_DIM,
                                    SWA_KV_HEADS * HEAD_DIM, N_BRANCHES * D_MODEL)
_offs = np.concatenate([[0], np.cumsum(_sizes)])
(OFF_NQ, OFF_NKC, OFF_NVC, OFF_NKS, OFF_NVS, OFF_NKW, OFF_NVW, OFF_NGATE,
 OFF_SQ, OFF_SK, OFF_SV, OFF_WQ, OFF_WK, OFF_WV, OFF_BGATE) = (int(o) for o in _offs[:-1])

ROW_HEADS = ([("nkc", OFF_NKC + 64 * g) for g in range(2)] + [("nvc", OFF_NVC + 64 * g) for g in range(2)]
             + [("nks", OFF_NKS + 64 * g) for g in range(2)] + [("nkw", OFF_NKW + 64 * g) for g in range(2)]
             + [("sk", OFF_SK + 64 * h) for h in range(4)] + [("wk", OFF_WK + 64 * g) for g in range(2)])
ROW_KC, ROW_VC, ROW_KS, ROW_KW, ROW_SK, ROW_WK = 0, 2, 4, 6, 8, 12
COL_HEADS = ([("nq", OFF_NQ + 64 * h) for h in range(8)] + [("nvs", OFF_NVS + 64 * g) for g in range(2)]
             + [("nvw", OFF_NVW + 64 * g) for g in range(2)] + [("sq", OFF_SQ + 64 * h) for h in range(4)]
             + [("sv", OFF_SV + 64 * h) for h in range(4)] + [("wq", OFF_WQ + 64 * h) for h in range(4)]
             + [("wv", OFF_WV + 64 * g) for g in range(2)])
COL_NQ, COL_VS, COL_VW, COL_SQ, COL_SV, COL_WQ, COL_WV = 0, 8, 10, 12, 16, 20, 24


def _dot(a, b):
    return jnp.dot(a, b, preferred_element_type=F32)


def _rms_rows(x, w):
    ms = jnp.mean(x * x, axis=-1, keepdims=True)
    return x * lax.rsqrt(ms + NORM_EPS) * w


def _cparams(sem):
    return pltpu.CompilerParams(dimension_semantics=sem, vmem_limit_bytes=VMEM_LIMIT)


FFN_TM = 512
FFN_TF = 1408


def _ffn_kernel(x_ref, nw_ref, wg_ref, wu_ref, wd_ref, o_ref, h_ref, acc_ref):
    f = pl.program_id(1)

    @pl.when(f == 0)
    def _():
        h_ref[...] = _rms_rows(x_ref[...], nw_ref[...]).astype(BF16)
        acc_ref[...] = jnp.zeros_like(acc_ref)

    h = h_ref[...]
    g = _dot(h, wg_ref[...])
    u = _dot(h, wu_ref[...])
    a = (g * jax.nn.sigmoid(g)) * u
    acc_ref[...] += _dot(a.astype(BF16), wd_ref[...])

    @pl.when(f == pl.num_programs(1) - 1)
    def _():
        o_ref[...] = x_ref[...] + 0.5 * acc_ref[...]


def _ffn(x2, nw, wg, wu, wd):
    T = x2.shape[0]
    grid = (T // FFN_TM, D_FF // FFN_TF)
    return pl.pallas_call(
        _ffn_kernel,
        grid=grid,
        in_specs=[
            pl.BlockSpec((FFN_TM, D_MODEL), lambda i, f: (i, 0)),
            pl.BlockSpec((1, D_MODEL), lambda i, f: (0, 0)),
            pl.BlockSpec((D_MODEL, FFN_TF), lambda i, f: (0, f)),
            pl.BlockSpec((D_MODEL, FFN_TF), lambda i, f: (0, f)),
            pl.BlockSpec((FFN_TF, D_MODEL), lambda i, f: (f, 0)),
        ],
        out_specs=pl.BlockSpec((FFN_TM, D_MODEL), lambda i, f: (i, 0)),
        out_shape=jax.ShapeDtypeStruct((T, D_MODEL), F32),
        scratch_shapes=[pltpu.VMEM((FFN_TM, D_MODEL), BF16), pltpu.VMEM((FFN_TM, D_MODEL), F32)],
        compiler_params=_cparams(("parallel", "arbitrary")),
        name="ffn",
    )(x2, nw.reshape(1, D_MODEL), wg.astype(BF16), wu.astype(BF16), wd.astype(BF16))


PROJ_TM = 512


def _proj_kernel(x_ref, nw_ref, w_ref, flag_ref, gain_ref, o_ref, h_ref, *, transposed):
    j = pl.program_id(1)

    @pl.when(j == 0)
    def _():
        h_ref[...] = _rms_rows(x_ref[...], nw_ref[...]).astype(BF16)

    z = _dot(h_ref[...], w_ref[...])
    zz = z * z
    lane = lax.broadcasted_iota(jnp.int32, z.shape, 1)
    s_all = jnp.sum(zz, axis=-1, keepdims=True)
    s_lo = jnp.sum(jnp.where(lane < HEAD_DIM, zz, 0.0), axis=-1, keepdims=True)
    ms = jnp.where(lane < HEAD_DIM, s_lo, s_all - s_lo) * (1.0 / HEAD_DIM)
    inv = jnp.where(flag_ref[...] > 0.0, lax.rsqrt(ms + NORM_EPS), 1.0)
    out = z * inv * gain_ref[...]
    if transposed:
        out_t = out.T
        for c in range(PROJ_TM // LANES):
            o_ref[0, c] = out_t[:HEAD_DIM, c * LANES:(c + 1) * LANES].astype(o_ref.dtype)
            o_ref[1, c] = out_t[HEAD_DIM:, c * LANES:(c + 1) * LANES].astype(o_ref.dtype)
    else:
        o_ref[0] = out[:, :HEAD_DIM].astype(o_ref.dtype)
        o_ref[1] = out[:, HEAD_DIM:].astype(o_ref.dtype)


def _proj(x2, nw, w, flag, gain, transposed):
    T = x2.shape[0]
    n_heads = w.shape[1] // HEAD_DIM
    n_tiles = n_heads // 2
    if transposed:
        out_shape = jax.ShapeDtypeStruct((n_heads, T // LANES, HEAD_DIM, LANES), BF16)
        out_spec = pl.BlockSpec((2, PROJ_TM // LANES, HEAD_DIM, LANES), lambda i, j: (j, i, 0, 0))
    else:
        out_shape = jax.ShapeDtypeStruct((n_heads, T, HEAD_DIM), BF16)
        out_spec = pl.BlockSpec((2, PROJ_TM, HEAD_DIM), lambda i, j: (j, i, 0))
    return pl.pallas_call(
        functools.partial(_proj_kernel, transposed=transposed),
        grid=(T // PROJ_TM, n_tiles),
        in_specs=[
            pl.BlockSpec((PROJ_TM, D_MODEL), lambda i, j: (i, 0)),
            pl.BlockSpec((1, D_MODEL), lambda i, j: (0, 0)),
            pl.BlockSpec((D_MODEL, LANES), lambda i, j: (0, j)),
            pl.BlockSpec((1, LANES), lambda i, j: (0, j)),
            pl.BlockSpec((1, LANES), lambda i, j: (0, j)),
        ],
        out_specs=out_spec,
        out_shape=out_shape,
        scratch_shapes=[pltpu.VMEM((PROJ_TM, D_MODEL), BF16)],
        compiler_params=_cparams(("parallel", "arbitrary")),
        name="proj_t" if transposed else "proj_r",
    )(x2, nw.reshape(1, D_MODEL), w, flag, gain)


def _gate_kernel(x_ref, nw_ref, w_ref, o_ref):
    h = _rms_rows(x_ref[...], nw_ref[...]).astype(BF16)
    z = jax.nn.sigmoid(_dot(h, w_ref[...]))
    o_ref[...] = z.T[:o_ref.shape[0]]


def _nsa_gates(x2, nw, w_gate):
    T = x2.shape[0]
    w = jnp.pad(w_gate, ((0, 0), (0, LANES - N_GATE))).astype(BF16)
    return pl.pallas_call(
        _gate_kernel,
        grid=(T // PROJ_TM,),
        in_specs=[
            pl.BlockSpec((PROJ_TM, D_MODEL), lambda i: (i, 0)),
            pl.BlockSpec((1, D_MODEL), lambda i: (0, 0)),
            pl.BlockSpec((D_MODEL, LANES), lambda i: (0, 0)),
        ],
        out_specs=pl.BlockSpec((32, PROJ_TM), lambda i: (0, i)),
        out_shape=jax.ShapeDtypeStruct((32, T), F32),
        compiler_params=_cparams(("parallel",)),
        name="nsa_gates",
    )(x2, nw.reshape(1, D_MODEL), w)


def _compress_kernel(r_ref, w1_ref, w2_ref, pos_ref, kn_ref, row_ref, col_ref):
    kind = pl.program_id(0) // NSA_KV_HEADS
    r = r_ref[0, 0]
    nc = r.shape[0]
    half = CMP_STRIDE * HEAD_DIM
    w1 = w1_ref[0]
    p_lo = _dot(r, w1[:half])
    p_hi = _dot(r, w1[half:])
    p_pos = _dot(jnp.broadcast_to(pos_ref[...], (8, CMP_LEN * HEAD_DIM)).astype(BF16), w1)[:1]
    hdn = p_lo + pltpu.roll(p_hi, nc - 1, 0) + p_pos
    hdn = hdn * jax.nn.sigmoid(hdn)
    out = _dot(hdn.astype(BF16), w2_ref[0])
    ms = jnp.sum(out * out, axis=-1, keepdims=True) * (1.0 / HEAD_DIM)
    normed = out * lax.rsqrt(ms + NORM_EPS) * kn_ref[...]
    out = jnp.where(kind == 0, normed, out)
    row_ref[0, 0] = out[:, :HEAD_DIM].astype(row_ref.dtype)
    col_ref[0, 0] = out.T[:HEAD_DIM].astype(col_ref.dtype)


def _compress(rows4, w1, w2, pos, k_norm):
    _, B, NC, _ = rows4.shape
    kn = jnp.pad(k_norm.reshape(1, HEAD_DIM), ((0, 0), (0, LANES - HEAD_DIM)))
    w2p = jnp.pad(w2, ((0, 0), (0, 0), (0, LANES - HEAD_DIM))).astype(BF16)
    return pl.pallas_call(
        _compress_kernel,
        grid=(4, B),
        in_specs=[
            pl.BlockSpec((1, 1, NC, CMP_STRIDE * HEAD_DIM), lambda h, b: (h, b, 0, 0)),
            pl.BlockSpec((1, CMP_LEN * HEAD_DIM, CMP_HIDDEN), lambda h, b: (h // NSA_KV_HEADS, 0, 0)),
            pl.BlockSpec((1, CMP_HIDDEN, LANES), lambda h, b: (h // NSA_KV_HEADS, 0, 0)),
            pl.BlockSpec((1, CMP_LEN * HEAD_DIM), lambda h, b: (0, 0)),
            pl.BlockSpec((1, LANES), lambda h, b: (0, 0)),
        ],
        out_specs=[
            pl.BlockSpec((1, 1, NC, HEAD_DIM), lambda h, b: (h, b, 0, 0)),
            pl.BlockSpec((1, 1, HEAD_DIM, NC), lambda h, b: (h, b, 0, 0)),
        ],
        out_shape=[jax.ShapeDtypeStruct((4, B, NC, HEAD_DIM), BF16),
                   jax.ShapeDtypeStruct((4, B, HEAD_DIM, NC), BF16)],
        compiler_params=_cparams(("parallel", "parallel")),
        name="nsa_compress",
    )(rows4, w1.astype(BF16), w2p, pos.reshape(1, CMP_LEN * HEAD_DIM), kn)


def _softmax_tile_update(s, v_t, m, l, acc):
    m_new = jnp.maximum(m, jnp.max(s, axis=0, keepdims=True))
    alpha = jnp.exp(m - m_new)
    p = jnp.exp(s - m_new)
    l_new = alpha * l + jnp.sum(p, axis=0, keepdims=True)
    acc_new = alpha * acc + _dot(v_t, p.astype(BF16))
    return m_new, l_new, acc_new


def _pairs_to_rows(o_t):
    n = o_t.shape[1] // LANES
    outs = []
    for p in range(n // 2):
        a = o_t[:, (2 * p) * LANES:(2 * p + 1) * LANES]
        b = o_t[:, (2 * p + 1) * LANES:(2 * p + 2) * LANES]
        outs.append(jnp.concatenate([a, b], axis=0).T)
    return outs[0] if len(outs) == 1 else jnp.concatenate(outs, axis=1)


def _nsa_kernel(q_ref, kc_ref, vc_ref, ks_ref, vs_ref, kw_ref, vw_ref, gate_ref,
                cbias_ref, nbias_ref, ov_ref, o_ref, neg_ref, *, nb):
    i = pl.program_id(2)
    R = NSA_REP
    N = R * Q_BLOCK
    q_t = jnp.concatenate([q_ref[r, 0] for r in range(R)], axis=1)

    nc = kc_ref.shape[2]
    cb = cbias_ref[0, pl.ds(pl.multiple_of(8 * (nb - 1) - 8 * i, 8), nc), :]
    s_c = _dot(kc_ref[0, 0], q_t) + cb
    m_c = jnp.max(s_c, axis=0, keepdims=True)
    e_c = jnp.where(s_c > 0.5 * NEG_INF, jnp.exp(s_c - m_c), 0.0)
    l_c = jnp.sum(e_c, axis=0, keepdims=True)
    p_c = e_c * jnp.where(l_c > 0.0, 1.0 / l_c, 0.0)
    o_c = _dot(vc_ref[0, 0], p_c.astype(BF16))

    p_sum = p_c[:, 0:Q_BLOCK]
    for r in range(1, R):
        p_sum = p_sum + p_c[:, r * Q_BLOCK:(r + 1) * Q_BLOCK]
    p_hi = p_sum.astype(BF16)
    p_lo = (p_sum - p_hi.astype(F32)).astype(BF16)
    imp = _dot(ov_ref[...], p_hi) + _dot(ov_ref[...], p_lo)
    n_slc = imp.shape[0]
    jj = lax.broadcasted_iota(jnp.int32, imp.shape, 0)
    qq = lax.broadcasted_iota(jnp.int32, imp.shape, 1)
    cur = 2 * i + jnp.where(qq >= SLC_LEN, 1, 0)
    forced = (jj == 0) | (jj == cur) | (jj == cur - 1)
    score = jnp.where(forced, FORCE_SCORE, jnp.where(jj <= cur, imp, -1.0))
    sel = jnp.zeros(imp.shape, F32)
    for _ in range(min(SLC_TOPN, n_slc)):
        best = jnp.max(score, axis=0, keepdims=True)
        first = jnp.min(jnp.where(score == best, jj, n_slc), axis=0, keepdims=True)
        hit = jj == first
        sel = jnp.where(hit, 1.0, sel)
        score = jnp.where(hit, -2.0, score)
    neg_ref[...] = jnp.where(sel > 0.0, 0.0, NEG_INF)

    def sel_mask(J):
        a = jnp.broadcast_to(neg_ref[pl.ds(2 * J, 1), :], (SLC_LEN, Q_BLOCK))
        b = jnp.broadcast_to(neg_ref[pl.ds(2 * J + 1, 1), :], (SLC_LEN, Q_BLOCK))
        m1 = jnp.concatenate([a, b], axis=0)
        return jnp.concatenate([m1] * R, axis=1)

    def k_tile(ref, J):
        return ref[0, pl.ds(pl.multiple_of(J * Q_BLOCK, Q_BLOCK), Q_BLOCK), :]

    stats0 = (jnp.full((1, N), NEG_INF, F32), jnp.zeros((1, N), F32), jnp.zeros((HEAD_DIM, N), F32))
    s0 = _dot(k_tile(ks_ref, i), q_t) + nbias_ref[0, 0] + sel_mask(i)
    st = _softmax_tile_update(s0, vs_ref[0, i], *stats0)

    def near_s(st):
        s1 = _dot(k_tile(ks_ref, i - 1), q_t) + nbias_ref[0, 1] + sel_mask(i - 1)
        return _softmax_tile_update(s1, vs_ref[0, i - 1], *st)

    st = lax.cond(i >= 1, near_s, lambda st: st, st)

    def far_s(t, st):
        J = i - 2 - t
        s = _dot(k_tile(ks_ref, J), q_t) + sel_mask(J)
        return _softmax_tile_update(s, vs_ref[0, J], *st)

    m_s, l_s, a_s = lax.fori_loop(0, jnp.maximum(i - 1, 0), far_s, st)
    o_s = a_s * (1.0 / l_s)

    w0 = _dot(k_tile(kw_ref, i), q_t) + nbias_ref[0, 0]
    st = _softmax_tile_update(w0, vw_ref[0, i], *stats0)

    def near_w(st):
        s1 = _dot(k_tile(kw_ref, i - 1), q_t) + nbias_ref[0, 1]
        return _softmax_tile_update(s1, vw_ref[0, i - 1], *st)

    st = lax.cond(i >= 1, near_w, lambda st: st, st)
    n_full = NSA_WINDOW // Q_BLOCK

    def mid_w(t, st):
        J = i - 2 - t
        return _softmax_tile_update(_dot(k_tile(kw_ref, J), q_t), vw_ref[0, J], *st)

    st = lax.fori_loop(0, jnp.clip(i - 1, 0, n_full - 2), mid_w, st)

    def edge_w(st):
        s = _dot(k_tile(kw_ref, i - n_full), q_t) + nbias_ref[0, 2]
        return _softmax_tile_update(s, vw_ref[0, i - n_full], *st)

    m_w, l_w, a_w = lax.cond(i >= n_full, edge_w, lambda st: st, st)
    o_w = a_w * (1.0 / l_w)

    def gate_row(br):
        return jnp.concatenate([gate_ref[br, 0, r:r + 1, :] for r in range(R)], axis=1)

    o_t = gate_row(0) * o_c + gate_row(1) * o_s + gate_row(2) * o_w
    o_ref[0] = _pairs_to_rows(o_t).astype(o_ref.dtype)


def _nsa_attention(colmat, rowmat, cmp_rows, cmp_cols, gates, cbias, nbias, ov_t, B, S):
    nb = S // Q_BLOCK
    nc = S // CMP_STRIDE
    n_slc = S // SLC_LEN
    G, R = NSA_KV_HEADS, NSA_REP
    kern = functools.partial(_nsa_kernel, nb=nb)
    return pl.pallas_call(
        kern,
        grid=(B, G, nb),
        in_specs=[
            pl.BlockSpec((R, 1, HEAD_DIM, LANES), lambda b, g, i: (g, b * nb + i, 0, 0)),
            pl.BlockSpec((1, 1, nc, HEAD_DIM), lambda b, g, i: (g, b, 0, 0)),
            pl.BlockSpec((1, 1, HEAD_DIM, nc), lambda b, g, i: (NSA_KV_HEADS + g, b, 0, 0)),
            pl.BlockSpec((1, S, HEAD_DIM), lambda b, g, i: (ROW_KS + g, b, 0)),
            pl.BlockSpec((1, nb, HEAD_DIM, LANES), lambda b, g, i: (COL_VS + g, b, 0, 0)),
            pl.BlockSpec((1, S, HEAD_DIM), lambda b, g, i: (ROW_KW + g, b, 0)),
            pl.BlockSpec((1, nb, HEAD_DIM, LANES), lambda b, g, i: (COL_VW + g, b, 0, 0)),
            pl.BlockSpec((N_BRANCHES, 1, R, LANES), lambda b, g, i: (0, g, 0, b * nb + i)),
            pl.BlockSpec((1, cbias.shape[1], R * Q_BLOCK), lambda b, g, i: (g, 0, 0)),
            pl.BlockSpec((1, 3, Q_BLOCK, R * Q_BLOCK), lambda b, g, i: (g, 0, 0, 0)),
            pl.BlockSpec((n_slc, nc), lambda b, g, i: (0, 0)),
        ],
        out_specs=pl.BlockSpec((1, Q_BLOCK, R * HEAD_DIM), lambda b, g, i: (b, i, g)),
        out_shape=jax.ShapeDtypeStruct((B, S, NSA_HEADS * HEAD_DIM), BF16),
        scratch_shapes=[pltpu.VMEM((n_slc, Q_BLOCK), F32)],
        compiler_params=_cparams(("parallel", "parallel", "arbitrary")),
        name="nsa_attention",
    )(colmat, cmp_rows, cmp_cols, rowmat, colmat, rowmat, colmat, gates, cbias, nbias, ov_t)


def _sb_kernel(q_ref, k_ref, v_ref, tri_ref, o_ref):
    i = pl.program_id(2)
    tri = tri_ref[...]
    kk = lax.broadcasted_iota(jnp.int32, (Q_BLOCK, Q_BLOCK), 0)
    qq = lax.broadcasted_iota(jnp.int32, (Q_BLOCK, Q_BLOCK), 1)
    diag_mask = kk < qq
    outs = []
    for h in range(2):
        q_t = q_ref[h, 0]

        def tile(J, carry, acc, mask):
            k = k_ref[h, pl.ds(pl.multiple_of(J * Q_BLOCK, Q_BLOCK), Q_BLOCK), :]
            z = _dot(k, q_t)
            sp = jnp.maximum(z, 0.0) + jnp.log1p(jnp.exp(-jnp.abs(z)))
            lf = -sp if mask is None else jnp.where(mask, -sp, 0.0)
            lf_hi = lf.astype(BF16)
            lf_lo = (lf - lf_hi.astype(F32)).astype(BF16)
            after = _dot(tri, lf_hi) + _dot(tri, lf_lo) + carry
            w = jnp.exp(z - sp + after)
            if mask is not None:
                w = jnp.where(mask, w, 0.0)
            acc = acc + _dot(v_ref[h, J], w.astype(BF16))
            carry = carry + jnp.sum(lf, axis=0, keepdims=True)
            return carry, acc

        carry, acc = tile(i, jnp.zeros((1, Q_BLOCK), F32), jnp.zeros((HEAD_DIM, Q_BLOCK), F32), diag_mask)

        def body(t, st):
            return tile(i - 1 - t, st[0], st[1], None)

        carry, acc = lax.fori_loop(0, i, body, (carry, acc))
        outs.append(acc)
    o_ref[0] = jnp.concatenate(outs, axis=0).T.astype(o_ref.dtype)


def _sb_attention(colmat, rowmat, tri, B, S):
    nb = S // Q_BLOCK
    return pl.pallas_call(
        _sb_kernel,
        grid=(B, SB_HEADS // 2, nb),
        in_specs=[
            pl.BlockSpec((2, 1, HEAD_DIM, LANES), lambda b, hp, i: (COL_SQ // 2 + hp, b * nb + i, 0, 0)),
            pl.BlockSpec((2, S, HEAD_DIM), lambda b, hp, i: (ROW_SK // 2 + hp, b, 0)),
            pl.BlockSpec((2, nb, HEAD_DIM, LANES), lambda b, hp, i: (COL_SV // 2 + hp, b, 0, 0)),
            pl.BlockSpec((Q_BLOCK, Q_BLOCK), lambda b, hp, i: (0, 0)),
        ],
        out_specs=pl.BlockSpec((1, Q_BLOCK, 2 * HEAD_DIM), lambda b, hp, i: (b, i, hp)),
        out_shape=jax.ShapeDtypeStruct((B, S, SB_HEADS * HEAD_DIM), BF16),
        compiler_params=_cparams(("parallel", "parallel", "arbitrary")),
        name="sb_attention",
    )(colmat, rowmat, colmat, tri)


def _swa_kernel(q_ref, k_ref, v_ref, bias_ref, sink_ref, o_ref):
    i = pl.program_id(2)
    R = SWA_REP
    q_t = jnp.concatenate([q_ref[r, 0] for r in range(R)], axis=1)
    jp = jnp.maximum(i - 1, 0)
    k0 = k_ref[0, pl.ds(pl.multiple_of(i * Q_BLOCK, Q_BLOCK), Q_BLOCK), :]
    k1 = k_ref[0, pl.ds(pl.multiple_of(jp * Q_BLOCK, Q_BLOCK), Q_BLOCK), :]
    s0 = _dot(k0, q_t) + bias_ref[0, 0]
    s1 = _dot(k1, q_t) + bias_ref[0, 1] + jnp.where(i >= 1, 0.0, NEG_INF)
    sink = sink_ref[0]
    m = jnp.maximum(jnp.maximum(jnp.max(s0, axis=0, keepdims=True), jnp.max(s1, axis=0, keepdims=True)), sink)
    e0 = jnp.exp(s0 - m)
    e1 = jnp.exp(s1 - m)
    den = jnp.sum(e0, axis=0, keepdims=True) + jnp.sum(e1, axis=0, keepdims=True) + jnp.exp(sink - m)
    inv = 1.0 / den
    o_t = _dot(v_ref[0, i], (e0 * inv).astype(BF16)) + _dot(v_ref[0, jp], (e1 * inv).astype(BF16))
    o_ref[0] = _pairs_to_rows(o_t).astype(o_ref.dtype)


def _swa_attention(colmat, rowmat, bias, sinks, B, S):
    nb = S // Q_BLOCK
    G, R = SWA_KV_HEADS, SWA_REP
    return pl.pallas_call(
        _swa_kernel,
        grid=(B, G, nb),
        in_specs=[
            pl.BlockSpec((R, 1, HEAD_DIM, LANES), lambda b, g, i: (COL_WQ // R + g, b * nb + i, 0, 0)),
            pl.BlockSpec((1, S, HEAD_DIM), lambda b, g, i: (ROW_WK + g, b, 0)),
            pl.BlockSpec((1, nb, HEAD_DIM, LANES), lambda b, g, i: (COL_WV + g, b, 0, 0)),
            pl.BlockSpec((1, 2, Q_BLOCK, R * Q_BLOCK), lambda b, g, i: (g, 0, 0, 0)),
            pl.BlockSpec((1, 1, R * Q_BLOCK), lambda b, g, i: (g, 0, 0)),
        ],
        out_specs=pl.BlockSpec((1, Q_BLOCK, R * HEAD_DIM), lambda b, g, i: (b, i, g)),
        out_shape=jax.ShapeDtypeStruct((B, S, SWA_HEADS * HEAD_DIM), BF16),
        compiler_params=_cparams(("parallel", "parallel", "arbitrary")),
        name="swa_attention",
    )(colmat, rowmat, colmat, bias, sinks)


MERGE_TM = 512


def _merge_kernel(x_ref, nw_ref, wg_ref, ya_ref, yb_ref, yc_ref, ua_ref, ub_ref, uc_ref, wo_ref, o_ref):
    x = x_ref[...]
    h = _rms_rows(x, nw_ref[...]).astype(BF16)
    merged = None
    for br, (y_ref, u_ref) in enumerate(((ya_ref, ua_ref), (yb_ref, ub_ref), (yc_ref, uc_ref))):
        g = jax.nn.sigmoid(_dot(h, wg_ref[:, br * D_MODEL:(br + 1) * D_MODEL]))
        t = g * _dot(y_ref[...], u_ref[...])
        merged = t if merged is None else merged + t
    o_ref[...] = x + _dot(merged.astype(BF16), wo_ref[...])


def _merge(x2, nw, w_bgate, y_nsa, y_sb, y_swa, u_nsa, u_sb, u_swa, w_out):
    T = x2.shape[0]
    full = lambda shape: pl.BlockSpec(shape, lambda i: (0, 0))
    rows = lambda n: pl.BlockSpec((MERGE_TM, n), lambda i: (i, 0))
    return pl.pallas_call(
        _merge_kernel,
        grid=(T // MERGE_TM,),
        in_specs=[rows(D_MODEL), full((1, D_MODEL)), full((D_MODEL, N_BRANCHES * D_MODEL)),
                  rows(y_nsa.shape[1]), rows(y_sb.shape[1]), rows(y_swa.shape[1]),
                  full(u_nsa.shape), full(u_sb.shape), full(u_swa.shape), full((D_MODEL, D_MODEL))],
        out_specs=rows(D_MODEL),
        out_shape=jax.ShapeDtypeStruct((T, D_MODEL), F32),
        compiler_params=_cparams(("parallel",)),
        name="merge",
    )(x2, nw.reshape(1, D_MODEL), w_bgate.astype(BF16), y_nsa, y_sb, y_swa,
      u_nsa.astype(BF16), u_sb.astype(BF16), u_swa.astype(BF16), w_out.astype(BF16))


def _t5_bucket(dist):
    max_exact = NUM_BUCKETS // 2
    d = jnp.maximum(dist, 0)
    df = jnp.maximum(d, 1).astype(F32)
    large = max_exact + (jnp.log(df / max_exact) / math.log(MAX_DISTANCE / max_exact)
                         * (NUM_BUCKETS - max_exact)).astype(jnp.int32)
    large = jnp.minimum(large, NUM_BUCKETS - 1)
    return jnp.where(d < max_exact, d, large)


def _bias_tables(rel_bias, nb):
    by_dist = rel_bias[_t5_bucket(jnp.arange(MAX_DISTANCE)), :].T.astype(F32)
    far = rel_bias[NUM_BUCKETS - 1, :].astype(F32)

    def lookup(dist, heads, shift):
        inside = by_dist[heads][:, jnp.clip(dist, 0, MAX_DISTANCE - 1)]
        val = jnp.where(dist[None] >= MAX_DISTANCE, far[heads][:, None, None], inside)
        if shift:
            val = val - far[heads][:, None, None]
        return jnp.where(dist[None] < 0, NEG_INF, val)

    def lanes(t, G, R):
        K, Q = t.shape[1:]
        return t.reshape(G, R, K, Q).transpose(0, 2, 1, 3).reshape(G, K, R * Q)

    k = jnp.arange(Q_BLOCK)[:, None]
    q = jnp.arange(Q_BLOCK)[None, :]
    nsa_heads = jnp.arange(NSA_HEADS)
    swa_heads = NSA_HEADS + jnp.arange(SWA_HEADS)
    d0, d1 = q - k, q - k + Q_BLOCK
    edge = jnp.where(k > q, 0.0, NEG_INF)[None] * jnp.ones((NSA_HEADS, 1, 1), F32)
    nbias = jnp.stack([lanes(lookup(d0, nsa_heads, True), NSA_KV_HEADS, NSA_REP),
                       lanes(lookup(d1, nsa_heads, True), NSA_KV_HEADS, NSA_REP),
                       lanes(edge, NSA_KV_HEADS, NSA_REP)], axis=1)
    rho = jnp.arange(16 * nb - 8)[:, None]
    dc = q - CMP_STRIDE * (rho - 8 * (nb - 1)) - (CMP_LEN - 1)
    cbias = lanes(lookup(dc, nsa_heads, True), NSA_KV_HEADS, NSA_REP)
    s0 = lookup(d0, swa_heads, False)
    s1 = jnp.where((k > q)[None], lookup(d1, swa_heads, False), NEG_INF)
    sbias = jnp.stack([lanes(s0, SWA_KV_HEADS, SWA_REP), lanes(s1, SWA_KV_HEADS, SWA_REP)], axis=1)
    return nbias, cbias, sbias


def _overlap_t(S):
    nc, n_slc = S // CMP_STRIDE, S // SLC_LEN
    n = np.arange(nc)[None, :]
    j = np.arange(n_slc)[:, None]
    ov = (n * CMP_STRIDE < (j + 1) * SLC_LEN) & (n * CMP_STRIDE + CMP_LEN - 1 >= j * SLC_LEN) & (n < nc - 1)
    return jnp.asarray(ov, BF16)


def _head_cols(w, heads):
    return jnp.concatenate([w[:, off:off + HEAD_DIM] for _, off in heads], axis=1)


def _head_vectors(heads, norm_of, scaled):
    flags, gains = [], []
    ones = jnp.ones((HEAD_DIM,), F32)
    for name, _ in heads:
        w = norm_of.get(name)
        flags.append(ones if w is not None else 0.0 * ones)
        g = w.astype(F32) if w is not None else ones
        gains.append(g * (HEAD_DIM ** -0.5) if name in scaled else g)
    return jnp.concatenate(flags).reshape(1, -1), jnp.concatenate(gains).reshape(1, -1)


def kernel(x, rel_bias, ffn1_norm, ffn1_w_gate, ffn1_w_up, ffn1_w_down, mix_norm, w_in, nsa_q_norm, nsa_k_norm, nsa_cmp_pos, nsa_cmp_k_w1, nsa_cmp_k_w2, nsa_cmp_v_w1, nsa_cmp_v_w2, swa_q_norm, swa_k_norm, swa_sinks, w_up_nsa, w_up_sb, w_up_swa, w_out, ffn2_norm, ffn2_w_gate, ffn2_w_up, ffn2_w_down):
    B, S, D = x.shape
    T = B * S
    nb = S // Q_BLOCK
    nc = S // CMP_STRIDE
    depth = w_in.shape[0]
    nbias, cbias, sbias = _bias_tables(rel_bias, nb)
    ov_t = _overlap_t(S)
    tri = jnp.asarray(np.triu(np.ones((Q_BLOCK, Q_BLOCK), np.float32), 1), BF16)
    scaled = ("nq", "sq", "wq")

    x2 = x.reshape(T, D)
    for l in range(depth):
        x2 = _ffn(x2, ffn1_norm[l], ffn1_w_gate[l], ffn1_w_up[l], ffn1_w_down[l])

        norm_of = {"nq": nsa_q_norm[l], "nks": nsa_k_norm[l], "nkw": nsa_k_norm[l],
                   "wq": swa_q_norm[l], "wk": swa_k_norm[l]}
        rflag, rgain = _head_vectors(ROW_HEADS, norm_of, scaled)
        cflag, cgain = _head_vectors(COL_HEADS, norm_of, scaled)
        rowmat = _proj(x2, mix_norm[l], _head_cols(w_in[l], ROW_HEADS).astype(BF16), rflag, rgain, False)
        colmat = _proj(x2, mix_norm[l], _head_cols(w_in[l], COL_HEADS).astype(BF16), cflag, cgain, True)
        gates = _nsa_gates(x2, mix_norm[l], w_in[l][:, OFF_NGATE:OFF_NGATE + N_GATE])
        gates = gates[:N_GATE].reshape(N_BRANCHES, NSA_KV_HEADS, NSA_REP, T)

        rows4 = rowmat[ROW_KC:ROW_KC + 4].reshape(4, B, nc, CMP_STRIDE * HEAD_DIM)
        cmp_rows, cmp_cols = _compress(rows4, jnp.stack([nsa_cmp_k_w1[l], nsa_cmp_v_w1[l]]),
                                       jnp.stack([nsa_cmp_k_w2[l], nsa_cmp_v_w2[l]]),
                                       nsa_cmp_pos[l], nsa_k_norm[l])

        y_nsa = _nsa_attention(colmat, rowmat, cmp_rows, cmp_cols, gates, cbias, nbias, ov_t, B, S)
        y_sb = _sb_attention(colmat, rowmat, tri, B, S)
        sinks = jnp.repeat(swa_sinks[l].astype(F32).reshape(SWA_KV_HEADS, 1, SWA_REP), Q_BLOCK, axis=2)
        y_swa = _swa_attention(colmat, rowmat, sbias, sinks, B, S)

        x2 = _merge(x2, mix_norm[l], w_in[l][:, OFF_BGATE:], y_nsa.reshape(T, -1), y_sb.reshape(T, -1),
                    y_swa.reshape(T, -1), w_up_nsa[l], w_up_sb[l], w_up_swa[l], w_out[l])
        x2 = _ffn(x2, ffn2_norm[l], ffn2_w_gate[l], ffn2_w_up[l], ffn2_w_down[l])
    return x2.reshape(B, S, D)
```

```python
import functools
import math

import numpy as np
import jax
import jax.numpy as jnp
from jax import lax
from jax.experimental import pallas as pl
from jax.experimental.pallas import tpu as pltpu

D_MODEL = 1024
HEAD_DIM = 64
Q_BLOCK = 128
NSA_HEADS = 8
NSA_KV_HEADS = 2
NSA_REP = NSA_HEADS // NSA_KV_HEADS
CMP_LEN = 32
CMP_STRIDE = 16
CMP_HIDDEN = 256
SLC_LEN = 64
SLC_TOPN = 16
NSA_WINDOW = 512
FORCE_SCORE = 1e4
SB_HEADS = 4
SWA_HEADS = 4
SWA_KV_HEADS = 2
SWA_REP = SWA_HEADS // SWA_KV_HEADS
SWA_WINDOW = 128
NUM_BUCKETS = 32
MAX_DISTANCE = 128
D_FF = 2816
NORM_EPS = 1e-6
NEG_INF = -1e30
N_BRANCHES = 3
N_GATE = 3 * NSA_HEADS

LANES = 128
VMEM_LIMIT = 56 * 1024 * 1024
BF16 = jnp.bfloat16
F32 = jnp.float32

_sizes = (NSA_HEADS * HEAD_DIM,) + (NSA_KV_HEADS * HEAD_DIM,) * 6 + (N_GATE,) \
    + (SB_HEADS * HEAD_DIM,) * 3 + (SWA_HEADS * HEAD_DIM, SWA_KV_HEADS * HEAD_DIM,
                                    SWA_KV_HEADS * HEAD_DIM, N_BRANCHES * D_MODEL)
_offs = np.concatenate([[0], np.cumsum(_sizes)])
(OFF_NQ, OFF_NKC, OFF_NVC, OFF_NKS, OFF_NVS, OFF_NKW, OFF_NVW, OFF_NGATE,
 OFF_SQ, OFF_SK, OFF_SV, OFF_WQ, OFF_WK, OFF_WV, OFF_BGATE) = (int(o) for o in _offs[:-1])

ROW_HEADS = ([("nkc", OFF_NKC + 64 * g) for g in range(2)] + [("nvc", OFF_NVC + 64 * g) for g in range(2)]
             + [("nks", OFF_NKS + 64 * g) for g in range(2)] + [("nkw", OFF_NKW + 64 * g) for g in range(2)]
             + [("sq", OFF_SQ + 64 * h) for h in range(4)] + [("sv", OFF_SV + 64 * h) for h in range(4)]
             + [("wk", OFF_WK + 64 * g) for g in range(2)])
ROW_KC, ROW_VC, ROW_KS, ROW_KW, ROW_SQ, ROW_SV, ROW_WK = 0, 2, 4, 6, 8, 12, 16
COL_HEADS = ([("nq", OFF_NQ + 64 * h) for h in range(8)] + [("nvs", OFF_NVS + 64 * g) for g in range(2)]
             + [("nvw", OFF_NVW + 64 * g) for g in range(2)] + [("sk", OFF_SK + 64 * h) for h in range(4)]
             + [("wq", OFF_WQ + 64 * h) for h in range(4)] + [("wv", OFF_WV + 64 * g) for g in range(2)])
COL_NQ, COL_VS, COL_VW, COL_SK, COL_WQ, COL_WV = 0, 8, 10, 12, 16, 20


def _dot(a, b):
    return jnp.dot(a, b, preferred_element_type=F32)


def _rms_rows(x, w):
    ms = jnp.mean(x * x, axis=-1, keepdims=True)
    return x * lax.rsqrt(ms + NORM_EPS) * w


def _cparams(sem):
    return pltpu.CompilerParams(dimension_semantics=sem, vmem_limit_bytes=VMEM_LIMIT)


FFN_TM = 512
FFN_TF = 1408


def _ffn_kernel(x_ref, nw_ref, wg_ref, wu_ref, wd_ref, o_ref, h_ref, acc_ref):
    f = pl.program_id(1)

    @pl.when(f == 0)
    def _():
        h_ref[...] = _rms_rows(x_ref[...], nw_ref[...]).astype(BF16)
        acc_ref[...] = jnp.zeros_like(acc_ref)

    h = h_ref[...]
    g = _dot(h, wg_ref[...])
    u = _dot(h, wu_ref[...])
    a = (g * jax.nn.sigmoid(g)) * u
    acc_ref[...] += _dot(a.astype(BF16), wd_ref[...])

    @pl.when(f == pl.num_programs(1) - 1)
    def _():
        o_ref[...] = x_ref[...] + 0.5 * acc_ref[...]


def _ffn(x2, nw, wg, wu, wd):
    T = x2.shape[0]
    grid = (T // FFN_TM, D_FF // FFN_TF)
    return pl.pallas_call(
        _ffn_kernel,
        grid=grid,
        in_specs=[
            pl.BlockSpec((FFN_TM, D_MODEL), lambda i, f: (i, 0)),
            pl.BlockSpec((1, D_MODEL), lambda i, f: (0, 0)),
            pl.BlockSpec((D_MODEL, FFN_TF), lambda i, f: (0, f)),
            pl.BlockSpec((D_MODEL, FFN_TF), lambda i, f: (0, f)),
            pl.BlockSpec((FFN_TF, D_MODEL), lambda i, f: (f, 0)),
        ],
        out_specs=pl.BlockSpec((FFN_TM, D_MODEL), lambda i, f: (i, 0)),
        out_shape=jax.ShapeDtypeStruct((T, D_MODEL), F32),
        scratch_shapes=[pltpu.VMEM((FFN_TM, D_MODEL), BF16), pltpu.VMEM((FFN_TM, D_MODEL), F32)],
        compiler_params=_cparams(("parallel", "arbitrary")),
        name="ffn",
    )(x2, nw.reshape(1, D_MODEL), wg.astype(BF16), wu.astype(BF16), wd.astype(BF16))


PROJ_TM = 512


def _proj_kernel(x_ref, nw_ref, w_ref, flag_ref, gain_ref, o_ref, h_ref, *, transposed):
    j = pl.program_id(1)

    @pl.when(j == 0)
    def _():
        h_ref[...] = _rms_rows(x_ref[...], nw_ref[...]).astype(BF16)

    z = _dot(h_ref[...], w_ref[...])
    zz = z * z
    lane = lax.broadcasted_iota(jnp.int32, z.shape, 1)
    s_all = jnp.sum(zz, axis=-1, keepdims=True)
    s_lo = jnp.sum(jnp.where(lane < HEAD_DIM, zz, 0.0), axis=-1, keepdims=True)
    ms = jnp.where(lane < HEAD_DIM, s_lo, s_all - s_lo) * (1.0 / HEAD_DIM)
    inv = jnp.where(flag_ref[...] > 0.0, lax.rsqrt(ms + NORM_EPS), 1.0)
    out = z * inv * gain_ref[...]
    if transposed:
        out_t = out.T
        for c in range(PROJ_TM // LANES):
            o_ref[0, c] = out_t[:HEAD_DIM, c * LANES:(c + 1) * LANES].astype(o_ref.dtype)
            o_ref[1, c] = out_t[HEAD_DIM:, c * LANES:(c + 1) * LANES].astype(o_ref.dtype)
    else:
        o_ref[0] = out[:, :HEAD_DIM].astype(o_ref.dtype)
        o_ref[1] = out[:, HEAD_DIM:].astype(o_ref.dtype)


def _proj(x2, nw, w, flag, gain, transposed):
    T = x2.shape[0]
    n_heads = w.shape[1] // HEAD_DIM
    n_tiles = n_heads // 2
    if transposed:
        out_shape = jax.ShapeDtypeStruct((n_heads, T // LANES, HEAD_DIM, LANES), BF16)
        out_spec = pl.BlockSpec((2, PROJ_TM // LANES, HEAD_DIM, LANES), lambda i, j: (j, i, 0, 0))
    else:
        out_shape = jax.ShapeDtypeStruct((n_heads, T, HEAD_DIM), BF16)
        out_spec = pl.BlockSpec((2, PROJ_TM, HEAD_DIM), lambda i, j: (j, i, 0))
    return pl.pallas_call(
        functools.partial(_proj_kernel, transposed=transposed),
        grid=(T // PROJ_TM, n_tiles),
        in_specs=[
            pl.BlockSpec((PROJ_TM, D_MODEL), lambda i, j: (i, 0)),
            pl.BlockSpec((1, D_MODEL), lambda i, j: (0, 0)),
            pl.BlockSpec((D_MODEL, LANES), lambda i, j: (0, j)),
            pl.BlockSpec((1, LANES), lambda i, j: (0, j)),
            pl.BlockSpec((1, LANES), lambda i, j: (0, j)),
        ],
        out_specs=out_spec,
        out_shape=out_shape,
        scratch_shapes=[pltpu.VMEM((PROJ_TM, D_MODEL), BF16)],
        compiler_params=_cparams(("parallel", "arbitrary")),
        name="proj_t" if transposed else "proj_r",
    )(x2, nw.reshape(1, D_MODEL), w, flag, gain)


def _gate_kernel(x_ref, nw_ref, w_ref, o_ref):
    h = _rms_rows(x_ref[...], nw_ref[...]).astype(BF16)
    z = jax.nn.sigmoid(_dot(h, w_ref[...]))
    o_ref[...] = z.T[:o_ref.shape[0]]


def _nsa_gates(x2, nw, w_gate):
    T = x2.shape[0]
    w = jnp.pad(w_gate, ((0, 0), (0, LANES - N_GATE))).astype(BF16)
    return pl.pallas_call(
        _gate_kernel,
        grid=(T // PROJ_TM,),
        in_specs=[
            pl.BlockSpec((PROJ_TM, D_MODEL), lambda i: (i, 0)),
            pl.BlockSpec((1, D_MODEL), lambda i: (0, 0)),
            pl.BlockSpec((D_MODEL, LANES), lambda i: (0, 0)),
        ],
        out_specs=pl.BlockSpec((32, PROJ_TM), lambda i: (0, i)),
        out_shape=jax.ShapeDtypeStruct((32, T), F32),
        compiler_params=_cparams(("parallel",)),
        name="nsa_gates",
    )(x2, nw.reshape(1, D_MODEL), w)


def _compress_kernel(r_ref, w1_ref, w2_ref, pos_ref, kn_ref, row_ref, col_ref):
    kind = pl.program_id(0) // NSA_KV_HEADS
    r = r_ref[0, 0]
    nc = r.shape[0]
    half = CMP_STRIDE * HEAD_DIM
    w1 = w1_ref[0]
    p_lo = _dot(r, w1[:half])
    p_hi = _dot(r, w1[half:])
    p_pos = _dot(jnp.broadcast_to(pos_ref[...], (8, CMP_LEN * HEAD_DIM)).astype(BF16), w1)[:1]
    hdn = p_lo + pltpu.roll(p_hi, nc - 1, 0) + p_pos
    hdn = hdn * jax.nn.sigmoid(hdn)
    out = _dot(hdn.astype(BF16), w2_ref[0])
    ms = jnp.sum(out * out, axis=-1, keepdims=True) * (1.0 / HEAD_DIM)
    normed = out * lax.rsqrt(ms + NORM_EPS) * kn_ref[...]
    out = jnp.where(kind == 0, normed, out)
    row_ref[0, 0] = out[:, :HEAD_DIM].astype(row_ref.dtype)
    col_ref[0, 0] = out.T[:HEAD_DIM].astype(col_ref.dtype)


def _compress(rows4, w1, w2, pos, k_norm):
    _, B, NC, _ = rows4.shape
    kn = jnp.pad(k_norm.reshape(1, HEAD_DIM), ((0, 0), (0, LANES - HEAD_DIM)))
    w2p = jnp.pad(w2, ((0, 0), (0, 0), (0, LANES - HEAD_DIM))).astype(BF16)
    return pl.pallas_call(
        _compress_kernel,
        grid=(4, B),
        in_specs=[
            pl.BlockSpec((1, 1, NC, CMP_STRIDE * HEAD_DIM), lambda h, b: (h, b, 0, 0)),
            pl.BlockSpec((1, CMP_LEN * HEAD_DIM, CMP_HIDDEN), lambda h, b: (h // NSA_KV_HEADS, 0, 0)),
            pl.BlockSpec((1, CMP_HIDDEN, LANES), lambda h, b: (h // NSA_KV_HEADS, 0, 0)),
            pl.BlockSpec((1, CMP_LEN * HEAD_DIM), lambda h, b: (0, 0)),
            pl.BlockSpec((1, LANES), lambda h, b: (0, 0)),
        ],
        out_specs=[
            pl.BlockSpec((1, 1, NC, HEAD_DIM), lambda h, b: (h, b, 0, 0)),
            pl.BlockSpec((1, 1, HEAD_DIM, NC), lambda h, b: (h, b, 0, 0)),
        ],
        out_shape=[jax.ShapeDtypeStruct((4, B, NC, HEAD_DIM), BF16),
                   jax.ShapeDtypeStruct((4, B, HEAD_DIM, NC), BF16)],
        compiler_params=_cparams(("parallel", "parallel")),
        name="nsa_compress",
    )(rows4, w1.astype(BF16), w2p, pos.reshape(1, CMP_LEN * HEAD_DIM), kn)


def _softmax_tile_update(s, v_t, m, l, acc):
    m_new = jnp.maximum(m, jnp.max(s, axis=0, keepdims=True))
    alpha = jnp.exp(m - m_new)
    p = jnp.exp(s - m_new)
    l_new = alpha * l + jnp.sum(p, axis=0, keepdims=True)
    acc_new = alpha * acc + _dot(v_t, p.astype(BF16))
    return m_new, l_new, acc_new


def _pairs_to_rows(o_t):
    n = o_t.shape[1] // LANES
    outs = []
    for p in range(n // 2):
        a = o_t[:, (2 * p) * LANES:(2 * p + 1) * LANES]
        b = o_t[:, (2 * p + 1) * LANES:(2 * p + 2) * LANES]
        outs.append(jnp.concatenate([a, b], axis=0).T)
    return outs[0] if len(outs) == 1 else jnp.concatenate(outs, axis=1)


FAR_TILES = 4


def _nsa_kernel(q_ref, kc_ref, vc_ref, ks_ref, vs_ref, kw_ref, vw_ref, gate_ref,
                cbias_ref, nbias_ref, ov_ref, o_ref, neg_ref, *, nb):
    i = pl.program_id(2)
    R = NSA_REP
    N = R * Q_BLOCK
    q_t = jnp.concatenate([q_ref[r, 0] for r in range(R)], axis=1)

    nc = kc_ref.shape[2]
    cb = cbias_ref[0, pl.ds(pl.multiple_of(8 * (nb - 1) - 8 * i, 8), nc), :]
    s_c = _dot(kc_ref[0, 0], q_t) + cb
    m_c = jnp.max(s_c, axis=0, keepdims=True)
    e_c = jnp.where(s_c > 0.5 * NEG_INF, jnp.exp(s_c - m_c), 0.0)
    l_c = jnp.sum(e_c, axis=0, keepdims=True)
    p_c = e_c * jnp.where(l_c > 0.0, 1.0 / l_c, 0.0)
    o_c = _dot(vc_ref[0, 0], p_c.astype(BF16))

    p_sum = p_c[:, 0:Q_BLOCK]
    for r in range(1, R):
        p_sum = p_sum + p_c[:, r * Q_BLOCK:(r + 1) * Q_BLOCK]
    p_hi = p_sum.astype(BF16)
    p_lo = (p_sum - p_hi.astype(F32)).astype(BF16)
    imp = _dot(ov_ref[...], p_hi) + _dot(ov_ref[...], p_lo)
    n_slc = imp.shape[0]
    jj = lax.broadcasted_iota(jnp.int32, imp.shape, 0)
    qq = lax.broadcasted_iota(jnp.int32, imp.shape, 1)
    cur = 2 * i + jnp.where(qq >= SLC_LEN, 1, 0)
    forced = (jj == 0) | (jj == cur) | (jj == cur - 1)
    score = jnp.where(forced, FORCE_SCORE, jnp.where(jj <= cur, imp, -1.0))
    sel = jnp.zeros(imp.shape, F32)
    for _ in range(min(SLC_TOPN, n_slc)):
        best = jnp.max(score, axis=0, keepdims=True)
        first = jnp.min(jnp.where(score == best, jj, n_slc), axis=0, keepdims=True)
        hit = jj == first
        sel = jnp.where(hit, 1.0, sel)
        score = jnp.where(hit, -2.0, score)
    neg_ref[...] = jnp.where(sel > 0.0, 0.0, NEG_INF)

    def sel_mask(J):
        a = jnp.broadcast_to(neg_ref[pl.ds(2 * J, 1), :], (SLC_LEN, Q_BLOCK))
        b = jnp.broadcast_to(neg_ref[pl.ds(2 * J + 1, 1), :], (SLC_LEN, Q_BLOCK))
        m1 = jnp.concatenate([a, b], axis=0)
        return jnp.concatenate([m1] * R, axis=1)

    def k_tile(ref, J):
        return ref[0, pl.ds(pl.multiple_of(J * Q_BLOCK, Q_BLOCK), Q_BLOCK), :]

    stats0 = (jnp.full((1, N), NEG_INF, F32), jnp.zeros((1, N), F32), jnp.zeros((HEAD_DIM, N), F32))
    s0 = _dot(k_tile(ks_ref, i), q_t) + nbias_ref[0, 0] + sel_mask(i)
    st = _softmax_tile_update(s0, vs_ref[0, i], *stats0)

    def near_s(st):
        s1 = _dot(k_tile(ks_ref, i - 1), q_t) + nbias_ref[0, 1] + sel_mask(i - 1)
        return _softmax_tile_update(s1, vs_ref[0, i - 1], *st)

    st = lax.cond(i >= 1, near_s, lambda st: st, st)

    n_far = jnp.maximum(i - 1, 0)
    chunk_rows = FAR_TILES * Q_BLOCK

    def far_chunk(J0, st, n_valid):
        k = ks_ref[0, pl.ds(pl.multiple_of(J0 * Q_BLOCK, Q_BLOCK), chunk_rows), :]
        parts = [jnp.broadcast_to(neg_ref[pl.ds(2 * J0 + u, 1), :], (SLC_LEN, Q_BLOCK)) for u in range(2 * FAR_TILES)]
        mask = jnp.concatenate(parts, axis=0)
        if n_valid is not None:
            row = lax.broadcasted_iota(jnp.int32, mask.shape, 0)
            mask = jnp.where(row < n_valid * Q_BLOCK, mask, NEG_INF)
        s = _dot(k, q_t) + jnp.concatenate([mask] * R, axis=1)
        v_t = jnp.concatenate([vs_ref[0, J0 + u] for u in range(FAR_TILES)], axis=1)
        return _softmax_tile_update(s, v_t, *st)

    n_chunks = n_far // FAR_TILES
    st = lax.fori_loop(0, n_chunks, lambda t, st: far_chunk(n_far - FAR_TILES * (t + 1), st, None), st)
    rem = n_far - FAR_TILES * n_chunks
    st = lax.cond(rem > 0, lambda st: far_chunk(0, st, rem), lambda st: st, st)
    m_s, l_s, a_s = st
    o_s = a_s * (1.0 / l_s)

    w0 = _dot(k_tile(kw_ref, i), q_t) + nbias_ref[0, 0]
    st = _softmax_tile_update(w0, vw_ref[0, i], *stats0)

    def near_w(st):
        s1 = _dot(k_tile(kw_ref, i - 1), q_t) + nbias_ref[0, 1]
        return _softmax_tile_update(s1, vw_ref[0, i - 1], *st)

    st = lax.cond(i >= 1, near_w, lambda st: st, st)
    n_full = NSA_WINDOW // Q_BLOCK

    def mid_w(t, st):
        J = i - 2 - t
        return _softmax_tile_update(_dot(k_tile(kw_ref, J), q_t), vw_ref[0, J], *st)

    st = lax.fori_loop(0, jnp.clip(i - 1, 0, n_full - 2), mid_w, st)

    def edge_w(st):
        s = _dot(k_tile(kw_ref, i - n_full), q_t) + nbias_ref[0, 2]
        return _softmax_tile_update(s, vw_ref[0, i - n_full], *st)

    m_w, l_w, a_w = lax.cond(i >= n_full, edge_w, lambda st: st, st)
    o_w = a_w * (1.0 / l_w)

    def gate_row(br):
        return jnp.concatenate([gate_ref[br, 0, r:r + 1, :] for r in range(R)], axis=1)

    o_t = gate_row(0) * o_c + gate_row(1) * o_s + gate_row(2) * o_w
    o_ref[0] = _pairs_to_rows(o_t).astype(o_ref.dtype)


def _nsa_attention(colmat, rowmat, cmp_rows, cmp_cols, gates, cbias, nbias, ov_t, B, S):
    nb = S // Q_BLOCK
    nc = S // CMP_STRIDE
    n_slc = S // SLC_LEN
    G, R = NSA_KV_HEADS, NSA_REP
    kern = functools.partial(_nsa_kernel, nb=nb)
    return pl.pallas_call(
        kern,
        grid=(B, G, nb),
        in_specs=[
            pl.BlockSpec((R, 1, HEAD_DIM, LANES), lambda b, g, i: (g, b * nb + i, 0, 0)),
            pl.BlockSpec((1, 1, nc, HEAD_DIM), lambda b, g, i: (g, b, 0, 0)),
            pl.BlockSpec((1, 1, HEAD_DIM, nc), lambda b, g, i: (NSA_KV_HEADS + g, b, 0, 0)),
            pl.BlockSpec((1, S, HEAD_DIM), lambda b, g, i: (ROW_KS + g, b, 0)),
            pl.BlockSpec((1, nb, HEAD_DIM, LANES), lambda b, g, i: (COL_VS + g, b, 0, 0)),
            pl.BlockSpec((1, S, HEAD_DIM), lambda b, g, i: (ROW_KW + g, b, 0)),
            pl.BlockSpec((1, nb, HEAD_DIM, LANES), lambda b, g, i: (COL_VW + g, b, 0, 0)),
            pl.BlockSpec((N_BRANCHES, 1, R, LANES), lambda b, g, i: (0, g, 0, b * nb + i)),
            pl.BlockSpec((1, cbias.shape[1], R * Q_BLOCK), lambda b, g, i: (g, 0, 0)),
            pl.BlockSpec((1, 3, Q_BLOCK, R * Q_BLOCK), lambda b, g, i: (g, 0, 0, 0)),
            pl.BlockSpec((n_slc, nc), lambda b, g, i: (0, 0)),
        ],
        out_specs=pl.BlockSpec((1, Q_BLOCK, R * HEAD_DIM), lambda b, g, i: (b, i, g)),
        out_shape=jax.ShapeDtypeStruct((B, S, NSA_HEADS * HEAD_DIM), BF16),
        scratch_shapes=[pltpu.VMEM((n_slc, Q_BLOCK), F32)],
        compiler_params=_cparams(("parallel", "parallel", "arbitrary")),
        name="nsa_attention",
    )(colmat, cmp_rows, cmp_cols, rowmat, colmat, rowmat, colmat, gates, cbias, nbias, ov_t)


SB_QB = 512
SB_KT = 256


def _sb_kernel(q_ref, k_ref, v_ref, tri_ref, o_ref):
    i = pl.program_id(1)
    H = SB_HEADS
    tri = tri_ref[...]
    rows = lax.broadcasted_iota(jnp.int32, (SB_QB, SB_KT), 0)
    cols = lax.broadcasted_iota(jnp.int32, (SB_QB, SB_KT), 1)
    key_minus_query = cols - rows
    per_q = SB_QB // SB_KT

    def step(J, st, masked):
        out = []
        for h in range(H):
            carry, acc = st[h]
            k_t = jnp.concatenate([k_ref[h, (SB_KT // LANES) * J + c] for c in range(SB_KT // LANES)], axis=1)
            z = _dot(q_ref[h], k_t)
            sp = jnp.maximum(z, 0.0) + jnp.log(1.0 + jnp.exp(-jnp.abs(z)))
            if masked:
                valid = key_minus_query < i * SB_QB - J * SB_KT
                lf = jnp.where(valid, -sp, 0.0)
            else:
                lf = -sp
            lf_hi = lf.astype(BF16)
            lf_lo = (lf - lf_hi.astype(F32)).astype(BF16)
            after = _dot(lf_hi, tri) + _dot(lf_lo, tri) + carry
            w = jnp.exp(z - sp + after)
            if masked:
                w = jnp.where(valid, w, 0.0)
            v = v_ref[h, pl.ds(pl.multiple_of(J * SB_KT, SB_KT), SB_KT), :]
            acc = acc + _dot(w.astype(BF16), v)
            carry = carry + jnp.sum(lf, axis=1, keepdims=True)
            out.append((carry, acc))
        return tuple(out)

    st = tuple((jnp.zeros((SB_QB, 1), F32), jnp.zeros((SB_QB, HEAD_DIM), F32)) for _ in range(H))
    for d in range(per_q):
        st = step(per_q * i + per_q - 1 - d, st, True)
    st = lax.fori_loop(0, per_q * i, lambda t, st: step(per_q * i - 1 - t, st, False), st)
    o_ref[0] = jnp.concatenate([st[h][1] for h in range(H)], axis=1).astype(o_ref.dtype)


def _sb_attention(colmat, rowmat, tri, B, S):
    nq = S // SB_QB
    nb = S // LANES
    H = SB_HEADS
    return pl.pallas_call(
        _sb_kernel,
        grid=(B, nq),
        in_specs=[
            pl.BlockSpec((H, SB_QB, HEAD_DIM), lambda b, i: (ROW_SQ // H, b * nq + i, 0)),
            pl.BlockSpec((H, nb, HEAD_DIM, LANES), lambda b, i: (COL_SK // H, b, 0, 0)),
            pl.BlockSpec((H, S, HEAD_DIM), lambda b, i: (ROW_SV // H, b, 0)),
            pl.BlockSpec((SB_KT, SB_KT), lambda b, i: (0, 0)),
        ],
        out_specs=pl.BlockSpec((1, SB_QB, H * HEAD_DIM), lambda b, i: (b, i, 0)),
        out_shape=jax.ShapeDtypeStruct((B, S, H * HEAD_DIM), BF16),
        compiler_params=_cparams(("parallel", "arbitrary")),
        name="sb_attention",
    )(rowmat, colmat, rowmat, tri)


def _swa_kernel(q_ref, k_ref, v_ref, bias_ref, sink_ref, o_ref):
    i = pl.program_id(2)
    R = SWA_REP
    q_t = jnp.concatenate([q_ref[r, 0] for r in range(R)], axis=1)
    jp = jnp.maximum(i - 1, 0)
    k0 = k_ref[0, pl.ds(pl.multiple_of(i * Q_BLOCK, Q_BLOCK), Q_BLOCK), :]
    k1 = k_ref[0, pl.ds(pl.multiple_of(jp * Q_BLOCK, Q_BLOCK), Q_BLOCK), :]
    s0 = _dot(k0, q_t) + bias_ref[0, 0]
    s1 = _dot(k1, q_t) + bias_ref[0, 1] + jnp.where(i >= 1, 0.0, NEG_INF)
    sink = sink_ref[0]
    m = jnp.maximum(jnp.maximum(jnp.max(s0, axis=0, keepdims=True), jnp.max(s1, axis=0, keepdims=True)), sink)
    e0 = jnp.exp(s0 - m)
    e1 = jnp.exp(s1 - m)
    den = jnp.sum(e0, axis=0, keepdims=True) + jnp.sum(e1, axis=0, keepdims=True) + jnp.exp(sink - m)
    inv = 1.0 / den
    o_t = _dot(v_ref[0, i], (e0 * inv).astype(BF16)) + _dot(v_ref[0, jp], (e1 * inv).astype(BF16))
    o_ref[0] = _pairs_to_rows(o_t).astype(o_ref.dtype)


def _swa_attention(colmat, rowmat, bias, sinks, B, S):
    nb = S // Q_BLOCK
    G, R = SWA_KV_HEADS, SWA_REP
    return pl.pallas_call(
        _swa_kernel,
        grid=(B, G, nb),
        in_specs=[
            pl.BlockSpec((R, 1, HEAD_DIM, LANES), lambda b, g, i: (COL_WQ // R + g, b * nb + i, 0, 0)),
            pl.BlockSpec((1, S, HEAD_DIM), lambda b, g, i: (ROW_WK + g, b, 0)),
            pl.BlockSpec((1, nb, HEAD_DIM, LANES), lambda b, g, i: (COL_WV + g, b, 0, 0)),
            pl.BlockSpec((1, 2, Q_BLOCK, R * Q_BLOCK), lambda b, g, i: (g, 0, 0, 0)),
            pl.BlockSpec((1, 1, R * Q_BLOCK), lambda b, g, i: (g, 0, 0)),
        ],
        out_specs=pl.BlockSpec((1, Q_BLOCK, R * HEAD_DIM), lambda b, g, i: (b, i, g)),
        out_shape=jax.ShapeDtypeStruct((B, S, SWA_HEADS * HEAD_DIM), BF16),
        compiler_params=_cparams(("parallel", "parallel", "arbitrary")),
        name="swa_attention",
    )(colmat, rowmat, colmat, bias, sinks)


MERGE_TM = 512


def _merge_kernel(x_ref, nw_ref, wg_ref, ya_ref, yb_ref, yc_ref, ua_ref, ub_ref, uc_ref, wo_ref, o_ref):
    x = x_ref[...]
    h = _rms_rows(x, nw_ref[...]).astype(BF16)
    merged = None
    for br, (y_ref, u_ref) in enumerate(((ya_ref, ua_ref), (yb_ref, ub_ref), (yc_ref, uc_ref))):
        g = jax.nn.sigmoid(_dot(h, wg_ref[:, br * D_MODEL:(br + 1) * D_MODEL]))
        t = g * _dot(y_ref[...], u_ref[...])
        merged = t if merged is None else merged + t
    o_ref[...] = x + _dot(merged.astype(BF16), wo_ref[...])


def _merge(x2, nw, w_bgate, y_nsa, y_sb, y_swa, u_nsa, u_sb, u_swa, w_out):
    T = x2.shape[0]
    full = lambda shape: pl.BlockSpec(shape, lambda i: (0, 0))
    rows = lambda n: pl.BlockSpec((MERGE_TM, n), lambda i: (i, 0))
    return pl.pallas_call(
        _merge_kernel,
        grid=(T // MERGE_TM,),
        in_specs=[rows(D_MODEL), full((1, D_MODEL)), full((D_MODEL, N_BRANCHES * D_MODEL)),
                  rows(y_nsa.shape[1]), rows(y_sb.shape[1]), rows(y_swa.shape[1]),
                  full(u_nsa.shape), full(u_sb.shape), full(u_swa.shape), full((D_MODEL, D_MODEL))],
        out_specs=rows(D_MODEL),
        out_shape=jax.ShapeDtypeStruct((T, D_MODEL), F32),
        compiler_params=_cparams(("parallel",)),
        name="merge",
    )(x2, nw.reshape(1, D_MODEL), w_bgate.astype(BF16), y_nsa, y_sb, y_swa,
      u_nsa.astype(BF16), u_sb.astype(BF16), u_swa.astype(BF16), w_out.astype(BF16))


def _t5_bucket(dist):
    max_exact = NUM_BUCKETS // 2
    d = jnp.maximum(dist, 0)
    df = jnp.maximum(d, 1).astype(F32)
    large = max_exact + (jnp.log(df / max_exact) / math.log(MAX_DISTANCE / max_exact)
                         * (NUM_BUCKETS - max_exact)).astype(jnp.int32)
    large = jnp.minimum(large, NUM_BUCKETS - 1)
    return jnp.where(d < max_exact, d, large)


def _bias_tables(rel_bias, nb):
    by_dist = rel_bias[_t5_bucket(jnp.arange(MAX_DISTANCE)), :].T.astype(F32)
    far = rel_bias[NUM_BUCKETS - 1, :].astype(F32)

    def lookup(dist, heads, shift):
        inside = by_dist[heads][:, jnp.clip(dist, 0, MAX_DISTANCE - 1)]
        val = jnp.where(dist[None] >= MAX_DISTANCE, far[heads][:, None, None], inside)
        if shift:
            val = val - far[heads][:, None, None]
        return jnp.where(dist[None] < 0, NEG_INF, val)

    def lanes(t, G, R):
        K, Q = t.shape[1:]
        return t.reshape(G, R, K, Q).transpose(0, 2, 1, 3).reshape(G, K, R * Q)

    k = jnp.arange(Q_BLOCK)[:, None]
    q = jnp.arange(Q_BLOCK)[None, :]
    nsa_heads = jnp.arange(NSA_HEADS)
    swa_heads = NSA_HEADS + jnp.arange(SWA_HEADS)
    d0, d1 = q - k, q - k + Q_BLOCK
    edge = jnp.where(k > q, 0.0, NEG_INF)[None] * jnp.ones((NSA_HEADS, 1, 1), F32)
    nbias = jnp.stack([lanes(lookup(d0, nsa_heads, True), NSA_KV_HEADS, NSA_REP),
                       lanes(lookup(d1, nsa_heads, True), NSA_KV_HEADS, NSA_REP),
                       lanes(edge, NSA_KV_HEADS, NSA_REP)], axis=1)
    rho = jnp.arange(16 * nb - 8)[:, None]
    dc = q - CMP_STRIDE * (rho - 8 * (nb - 1)) - (CMP_LEN - 1)
    cbias = lanes(lookup(dc, nsa_heads, True), NSA_KV_HEADS, NSA_REP)
    s0 = lookup(d0, swa_heads, False)
    s1 = jnp.where((k > q)[None], lookup(d1, swa_heads, False), NEG_INF)
    sbias = jnp.stack([lanes(s0, SWA_KV_HEADS, SWA_REP), lanes(s1, SWA_KV_HEADS, SWA_REP)], axis=1)
    return nbias, cbias, sbias


def _overlap_t(S):
    nc, n_slc = S // CMP_STRIDE, S // SLC_LEN
    n = np.arange(nc)[None, :]
    j = np.arange(n_slc)[:, None]
    ov = (n * CMP_STRIDE < (j + 1) * SLC_LEN) & (n * CMP_STRIDE + CMP_LEN - 1 >= j * SLC_LEN) & (n < nc - 1)
    return jnp.asarray(ov, BF16)


def _head_cols(w, heads):
    return jnp.concatenate([w[:, off:off + HEAD_DIM] for _, off in heads], axis=1)


def _head_vectors(heads, norm_of, scaled):
    flags, gains = [], []
    ones = jnp.ones((HEAD_DIM,), F32)
    for name, _ in heads:
        w = norm_of.get(name)
        flags.append(ones if w is not None else 0.0 * ones)
        g = w.astype(F32) if w is not None else ones
        gains.append(g * (HEAD_DIM ** -0.5) if name in scaled else g)
    return jnp.concatenate(flags).reshape(1, -1), jnp.concatenate(gains).reshape(1, -1)


def kernel(x, rel_bias, ffn1_norm, ffn1_w_gate, ffn1_w_up, ffn1_w_down, mix_norm, w_in, nsa_q_norm, nsa_k_norm, nsa_cmp_pos, nsa_cmp_k_w1, nsa_cmp_k_w2, nsa_cmp_v_w1, nsa_cmp_v_w2, swa_q_norm, swa_k_norm, swa_sinks, w_up_nsa, w_up_sb, w_up_swa, w_out, ffn2_norm, ffn2_w_gate, ffn2_w_up, ffn2_w_down):
    B, S, D = x.shape
    T = B * S
    nb = S // Q_BLOCK
    nc = S // CMP_STRIDE
    depth = w_in.shape[0]
    nbias, cbias, sbias = _bias_tables(rel_bias, nb)
    ov_t = _overlap_t(S)
    tri = jnp.asarray(np.tril(np.ones((SB_KT, SB_KT), np.float32), -1), BF16)
    scaled = ("nq", "sq", "wq")

    x2 = x.reshape(T, D)
    for l in range(depth):
        x2 = _ffn(x2, ffn1_norm[l], ffn1_w_gate[l], ffn1_w_up[l], ffn1_w_down[l])

        norm_of = {"nq": nsa_q_norm[l], "nks": nsa_k_norm[l], "nkw": nsa_k_norm[l],
                   "wq": swa_q_norm[l], "wk": swa_k_norm[l]}
        rflag, rgain = _head_vectors(ROW_HEADS, norm_of, scaled)
        cflag, cgain = _head_vectors(COL_HEADS, norm_of, scaled)
        rowmat = _proj(x2, mix_norm[l], _head_cols(w_in[l], ROW_HEADS).astype(BF16), rflag, rgain, False)
        colmat = _proj(x2, mix_norm[l], _head_cols(w_in[l], COL_HEADS).astype(BF16), cflag, cgain, True)
        gates = _nsa_gates(x2, mix_norm[l], w_in[l][:, OFF_NGATE:OFF_NGATE + N_GATE])
        gates = gates[:N_GATE].reshape(N_BRANCHES, NSA_KV_HEADS, NSA_REP, T)

        rows4 = rowmat[ROW_KC:ROW_KC + 4].reshape(4, B, nc, CMP_STRIDE * HEAD_DIM)
        cmp_rows, cmp_cols = _compress(rows4, jnp.stack([nsa_cmp_k_w1[l], nsa_cmp_v_w1[l]]),
                                       jnp.stack([nsa_cmp_k_w2[l], nsa_cmp_v_w2[l]]),
                                       nsa_cmp_pos[l], nsa_k_norm[l])

        y_nsa = _nsa_attention(colmat, rowmat, cmp_rows, cmp_cols, gates, cbias, nbias, ov_t, B, S)
        y_sb = _sb_attention(colmat, rowmat, tri, B, S)
        sinks = jnp.repeat(swa_sinks[l].astype(F32).reshape(SWA_KV_HEADS, 1, SWA_REP), Q_BLOCK, axis=2)
        y_swa = _swa_attention(colmat, rowmat, sbias, sinks, B, S)

        x2 = _merge(x2, mix_norm[l], w_in[l][:, OFF_BGATE:], y_nsa.reshape(T, -1), y_sb.reshape(T, -1),
                    y_swa.reshape(T, -1), w_up_nsa[l], w_up_sb[l], w_up_swa[l], w_out[l])
        x2 = _ffn(x2, ffn2_norm[l], ffn2_w_gate[l], ffn2_w_up[l], ffn2_w_down[l])
    return x2.reshape(B, S, D)
```

```python
import functools
import math

import numpy as np
import jax
import jax.numpy as jnp
from jax import lax
from jax.experimental import pallas as pl
from jax.experimental.pallas import tpu as pltpu

D_MODEL = 1024
HEAD_DIM = 64
Q_BLOCK = 128
NSA_HEADS = 8
NSA_KV_HEADS = 2
NSA_REP = NSA_HEADS // NSA_KV_HEADS
CMP_LEN = 32
CMP_STRIDE = 16
CMP_HIDDEN = 256
SLC_LEN = 64
SLC_TOPN = 16
NSA_WINDOW = 512
FORCE_SCORE = 1e4
SB_HEADS = 4
SWA_HEADS = 4
SWA_KV_HEADS = 2
SWA_REP = SWA_HEADS // SWA_KV_HEADS
SWA_WINDOW = 128
NUM_BUCKETS = 32
MAX_DISTANCE = 128
D_FF = 2816
NORM_EPS = 1e-6
NEG_INF = -1e30
N_BRANCHES = 3
N_GATE = 3 * NSA_HEADS

LANES = 128
VMEM_LIMIT = 56 * 1024 * 1024
BF16 = jnp.bfloat16
F32 = jnp.float32

_sizes = (NSA_HEADS * HEAD_DIM,) + (NSA_KV_HEADS * HEAD_DIM,) * 6 + (N_GATE,) \
    + (SB_HEADS * HEAD_DIM,) * 3 + (SWA_HEADS * HEAD_DIM, SWA_KV_HEADS * HEAD_DIM,
                                    SWA_KV_HEADS * HEAD_DIM, N_BRANCHES * D_MODEL)
_offs = np.concatenate([[0], np.cumsum(_sizes)])
(OFF_NQ, OFF_NKC, OFF_NVC, OFF_NKS, OFF_NVS, OFF_NKW, OFF_NVW, OFF_NGATE,
 OFF_SQ, OFF_SK, OFF_SV, OFF_WQ, OFF_WK, OFF_WV, OFF_BGATE) = (int(o) for o in _offs[:-1])

ROW_HEADS = ([("nkc", OFF_NKC + 64 * g) for g in range(2)] + [("nvc", OFF_NVC + 64 * g) for g in range(2)]
             + [("nks", OFF_NKS + 64 * g) for g in range(2)] + [("nkw", OFF_NKW + 64 * g) for g in range(2)]
             + [("sq", OFF_SQ + 64 * h) for h in range(4)] + [("sv", OFF_SV + 64 * h) for h in range(4)]
             + [("wk", OFF_WK + 64 * g) for g in range(2)])
ROW_KC, ROW_VC, ROW_KS, ROW_KW, ROW_SQ, ROW_SV, ROW_WK = 0, 2, 4, 6, 8, 12, 16
COL_HEADS = ([("nq", OFF_NQ + 64 * h) for h in range(8)] + [("nvs", OFF_NVS + 64 * g) for g in range(2)]
             + [("nvw", OFF_NVW + 64 * g) for g in range(2)] + [("sk", OFF_SK + 64 * h) for h in range(4)]
             + [("wq", OFF_WQ + 64 * h) for h in range(4)] + [("wv", OFF_WV + 64 * g) for g in range(2)])
COL_NQ, COL_VS, COL_VW, COL_SK, COL_WQ, COL_WV = 0, 8, 10, 12, 16, 20


def _dot(a, b):
    return jnp.dot(a, b, preferred_element_type=F32)


def _rms_rows(x, w):
    ms = jnp.mean(x * x, axis=-1, keepdims=True)
    return x * lax.rsqrt(ms + NORM_EPS) * w


def _cparams(sem):
    return pltpu.CompilerParams(dimension_semantics=sem, vmem_limit_bytes=VMEM_LIMIT)


FFN_TM = 512
FFN_TF = 1408


def _ffn_kernel(x_ref, nw_ref, wg_ref, wu_ref, wd_ref, o_ref, h_ref, acc_ref):
    f = pl.program_id(1)

    @pl.when(f == 0)
    def _():
        h_ref[...] = _rms_rows(x_ref[...], nw_ref[...]).astype(BF16)
        acc_ref[...] = jnp.zeros_like(acc_ref)

    h = h_ref[...]
    g = _dot(h, wg_ref[...])
    u = _dot(h, wu_ref[...])
    a = (g * jax.nn.sigmoid(g)) * u
    acc_ref[...] += _dot(a.astype(BF16), wd_ref[...])

    @pl.when(f == pl.num_programs(1) - 1)
    def _():
        o_ref[...] = x_ref[...] + 0.5 * acc_ref[...]


def _ffn(x2, nw, wg, wu, wd):
    T = x2.shape[0]
    grid = (T // FFN_TM, D_FF // FFN_TF)
    return pl.pallas_call(
        _ffn_kernel,
        grid=grid,
        in_specs=[
            pl.BlockSpec((FFN_TM, D_MODEL), lambda i, f: (i, 0)),
            pl.BlockSpec((1, D_MODEL), lambda i, f: (0, 0)),
            pl.BlockSpec((D_MODEL, FFN_TF), lambda i, f: (0, f)),
            pl.BlockSpec((D_MODEL, FFN_TF), lambda i, f: (0, f)),
            pl.BlockSpec((FFN_TF, D_MODEL), lambda i, f: (f, 0)),
        ],
        out_specs=pl.BlockSpec((FFN_TM, D_MODEL), lambda i, f: (i, 0)),
        out_shape=jax.ShapeDtypeStruct((T, D_MODEL), F32),
        scratch_shapes=[pltpu.VMEM((FFN_TM, D_MODEL), BF16), pltpu.VMEM((FFN_TM, D_MODEL), F32)],
        compiler_params=_cparams(("parallel", "arbitrary")),
        name="ffn",
    )(x2, nw.reshape(1, D_MODEL), wg.astype(BF16), wu.astype(BF16), wd.astype(BF16))


PROJ_TM = 512


def _proj_kernel(x_ref, nw_ref, w_ref, flag_ref, gain_ref, o_ref, h_ref, *, transposed):
    j = pl.program_id(1)

    @pl.when(j == 0)
    def _():
        h_ref[...] = _rms_rows(x_ref[...], nw_ref[...]).astype(BF16)

    z = _dot(h_ref[...], w_ref[...])
    zz = z * z
    lane = lax.broadcasted_iota(jnp.int32, z.shape, 1)
    s_all = jnp.sum(zz, axis=-1, keepdims=True)
    s_lo = jnp.sum(jnp.where(lane < HEAD_DIM, zz, 0.0), axis=-1, keepdims=True)
    ms = jnp.where(lane < HEAD_DIM, s_lo, s_all - s_lo) * (1.0 / HEAD_DIM)
    inv = jnp.where(flag_ref[...] > 0.0, lax.rsqrt(ms + NORM_EPS), 1.0)
    out = z * inv * gain_ref[...]
    if transposed:
        out_t = out.T
        for c in range(PROJ_TM // LANES):
            o_ref[0, c] = out_t[:HEAD_DIM, c * LANES:(c + 1) * LANES].astype(o_ref.dtype)
            o_ref[1, c] = out_t[HEAD_DIM:, c * LANES:(c + 1) * LANES].astype(o_ref.dtype)
    else:
        o_ref[0] = out[:, :HEAD_DIM].astype(o_ref.dtype)
        o_ref[1] = out[:, HEAD_DIM:].astype(o_ref.dtype)


def _proj(x2, nw, w, flag, gain, transposed):
    T = x2.shape[0]
    n_heads = w.shape[1] // HEAD_DIM
    n_tiles = n_heads // 2
    if transposed:
        out_shape = jax.ShapeDtypeStruct((n_heads, T // LANES, HEAD_DIM, LANES), BF16)
        out_spec = pl.BlockSpec((2, PROJ_TM // LANES, HEAD_DIM, LANES), lambda i, j: (j, i, 0, 0))
    else:
        out_shape = jax.ShapeDtypeStruct((n_heads, T, HEAD_DIM), BF16)
        out_spec = pl.BlockSpec((2, PROJ_TM, HEAD_DIM), lambda i, j: (j, i, 0))
    return pl.pallas_call(
        functools.partial(_proj_kernel, transposed=transposed),
        grid=(T // PROJ_TM, n_tiles),
        in_specs=[
            pl.BlockSpec((PROJ_TM, D_MODEL), lambda i, j: (i, 0)),
            pl.BlockSpec((1, D_MODEL), lambda i, j: (0, 0)),
            pl.BlockSpec((D_MODEL, LANES), lambda i, j: (0, j)),
            pl.BlockSpec((1, LANES), lambda i, j: (0, j)),
            pl.BlockSpec((1, LANES), lambda i, j: (0, j)),
        ],
        out_specs=out_spec,
        out_shape=out_shape,
        scratch_shapes=[pltpu.VMEM((PROJ_TM, D_MODEL), BF16)],
        compiler_params=_cparams(("parallel", "arbitrary")),
        name="proj_t" if transposed else "proj_r",
    )(x2, nw.reshape(1, D_MODEL), w, flag, gain)


def _gate_kernel(x_ref, nw_ref, w_ref, o_ref):
    h = _rms_rows(x_ref[...], nw_ref[...]).astype(BF16)
    z = jax.nn.sigmoid(_dot(h, w_ref[...]))
    o_ref[...] = z.T[:o_ref.shape[0]]


def _nsa_gates(x2, nw, w_gate):
    T = x2.shape[0]
    w = jnp.pad(w_gate, ((0, 0), (0, LANES - N_GATE))).astype(BF16)
    return pl.pallas_call(
        _gate_kernel,
        grid=(T // PROJ_TM,),
        in_specs=[
            pl.BlockSpec((PROJ_TM, D_MODEL), lambda i: (i, 0)),
            pl.BlockSpec((1, D_MODEL), lambda i: (0, 0)),
            pl.BlockSpec((D_MODEL, LANES), lambda i: (0, 0)),
        ],
        out_specs=pl.BlockSpec((32, PROJ_TM), lambda i: (0, i)),
        out_shape=jax.ShapeDtypeStruct((32, T), F32),
        compiler_params=_cparams(("parallel",)),
        name="nsa_gates",
    )(x2, nw.reshape(1, D_MODEL), w)


def _compress_kernel(r_ref, w1_ref, w2_ref, pos_ref, kn_ref, row_ref, col_ref):
    kind = pl.program_id(0) // NSA_KV_HEADS
    r = r_ref[0, 0]
    nc = r.shape[0]
    half = CMP_STRIDE * HEAD_DIM
    w1 = w1_ref[0]
    p_lo = _dot(r, w1[:half])
    p_hi = _dot(r, w1[half:])
    p_pos = _dot(jnp.broadcast_to(pos_ref[...], (8, CMP_LEN * HEAD_DIM)).astype(BF16), w1)[:1]
    hdn = p_lo + pltpu.roll(p_hi, nc - 1, 0) + p_pos
    hdn = hdn * jax.nn.sigmoid(hdn)
    out = _dot(hdn.astype(BF16), w2_ref[0])
    ms = jnp.sum(out * out, axis=-1, keepdims=True) * (1.0 / HEAD_DIM)
    normed = out * lax.rsqrt(ms + NORM_EPS) * kn_ref[...]
    out = jnp.where(kind == 0, normed, out)
    row_ref[0, 0] = out[:, :HEAD_DIM].astype(row_ref.dtype)
    col_ref[0, 0] = out.T[:HEAD_DIM].astype(col_ref.dtype)


def _compress(rows4, w1, w2, pos, k_norm):
    _, B, NC, _ = rows4.shape
    kn = jnp.pad(k_norm.reshape(1, HEAD_DIM), ((0, 0), (0, LANES - HEAD_DIM)))
    w2p = jnp.pad(w2, ((0, 0), (0, 0), (0, LANES - HEAD_DIM))).astype(BF16)
    return pl.pallas_call(
        _compress_kernel,
        grid=(4, B),
        in_specs=[
            pl.BlockSpec((1, 1, NC, CMP_STRIDE * HEAD_DIM), lambda h, b: (h, b, 0, 0)),
            pl.BlockSpec((1, CMP_LEN * HEAD_DIM, CMP_HIDDEN), lambda h, b: (h // NSA_KV_HEADS, 0, 0)),
            pl.BlockSpec((1, CMP_HIDDEN, LANES), lambda h, b: (h // NSA_KV_HEADS, 0, 0)),
            pl.BlockSpec((1, CMP_LEN * HEAD_DIM), lambda h, b: (0, 0)),
            pl.BlockSpec((1, LANES), lambda h, b: (0, 0)),
        ],
        out_specs=[
            pl.BlockSpec((1, 1, NC, HEAD_DIM), lambda h, b: (h, b, 0, 0)),
            pl.BlockSpec((1, 1, HEAD_DIM, NC), lambda h, b: (h, b, 0, 0)),
        ],
        out_shape=[jax.ShapeDtypeStruct((4, B, NC, HEAD_DIM), BF16),
                   jax.ShapeDtypeStruct((4, B, HEAD_DIM, NC), BF16)],
        compiler_params=_cparams(("parallel", "parallel")),
        name="nsa_compress",
    )(rows4, w1.astype(BF16), w2p, pos.reshape(1, CMP_LEN * HEAD_DIM), kn)


def _softmax_tile_update(s, v_t, m, l, acc):
    m_new = jnp.maximum(m, jnp.max(s, axis=0, keepdims=True))
    alpha = jnp.exp2(m - m_new)
    p = jnp.exp2(s - m_new)
    l_new = alpha * l + jnp.sum(p, axis=0, keepdims=True)
    acc_new = alpha * acc + _dot(v_t, p.astype(BF16))
    return m_new, l_new, acc_new


def _pairs_to_rows(o_t):
    n = o_t.shape[1] // LANES
    outs = []
    for p in range(n // 2):
        a = o_t[:, (2 * p) * LANES:(2 * p + 1) * LANES]
        b = o_t[:, (2 * p + 1) * LANES:(2 * p + 2) * LANES]
        outs.append(jnp.concatenate([a, b], axis=0).T)
    return outs[0] if len(outs) == 1 else jnp.concatenate(outs, axis=1)


FAR_TILES = 4
LOG2E = math.log2(math.e)
NSA_TINY = 2.0 ** -100
BOUND_SLACK = 1.05


def _nsa_kernel(q_ref, kc_ref, vc_ref, ks_ref, vs_ref, kw_ref, vw_ref, gate_ref,
                cbias_ref, sbias_ref, wbias_ref, ov_ref, msel_ref, o_ref, neg_ref, *, nb):
    i = pl.program_id(2)
    R = NSA_REP
    N = R * Q_BLOCK
    nc = kc_ref.shape[2]
    q_t = jnp.concatenate([q_ref[r, 0] for r in range(R)], axis=1)
    n_win = NSA_WINDOW // Q_BLOCK

    def k_rows(ref, J0, n_tiles):
        return ref[0, pl.ds(pl.multiple_of(J0 * Q_BLOCK, Q_BLOCK), n_tiles * Q_BLOCK), :]

    def v_cols(ref, J0, n_tiles):
        return jnp.concatenate([ref[0, J0 + u] for u in range(n_tiles)], axis=1)

    def sel_mask(J0, n_tiles):
        parts = [jnp.broadcast_to(neg_ref[pl.ds(2 * J0 + u, 1), :], (SLC_LEN, Q_BLOCK)) for u in range(2 * n_tiles)]
        return jnp.concatenate([jnp.concatenate(parts, axis=0)] * R, axis=1)

    def compressed(fast):
        cb = cbias_ref[0, pl.ds(pl.multiple_of(8 * (nb - 1) - 8 * i, 8), nc), :]
        s = _dot(kc_ref[0, 0], q_t) + cb
        if fast:
            e = jnp.exp2(s)
        else:
            e = jnp.where(s > 0.5 * NEG_INF, jnp.exp2(s - jnp.max(s, axis=0, keepdims=True)), 0.0)
        l = jnp.sum(e, axis=0, keepdims=True)
        p = e * jnp.where(l > 0.0, 1.0 / l, 0.0)
        return p, l, _dot(vc_ref[0, 0], p.astype(BF16))

    def select_blocks(p_c):
        p_sum = p_c[:, 0:Q_BLOCK]
        for r in range(1, R):
            p_sum = p_sum + p_c[:, r * Q_BLOCK:(r + 1) * Q_BLOCK]
        p_hi = p_sum.astype(BF16)
        p_lo = (p_sum - p_hi.astype(F32)).astype(BF16)
        imp = _dot(ov_ref[...], p_hi) + _dot(ov_ref[...], p_lo)
        n_slc = imp.shape[0]
        jj = lax.broadcasted_iota(jnp.int32, imp.shape, 0)
        qq = lax.broadcasted_iota(jnp.int32, imp.shape, 1)
        cur = 2 * i + jnp.where(qq >= SLC_LEN, 1, 0)
        forced = (jj == 0) | (jj == cur) | (jj == cur - 1)
        score = jnp.where(forced, FORCE_SCORE, jnp.where(jj <= cur, imp, -1.0))
        sel = jnp.zeros(imp.shape, F32)
        for _ in range(min(SLC_TOPN, n_slc)):
            best = jnp.max(score, axis=0, keepdims=True)
            first = jnp.min(jnp.where(score == best, jj, n_slc), axis=0, keepdims=True)
            hit = jj == first
            sel = jnp.where(hit, 1.0, sel)
            score = jnp.where(hit, -2.0, score)
        neg_ref[...] = jnp.where(sel > 0.0, msel_ref[0], NEG_INF)

    def add_chunk(st, s, v_t):
        e = jnp.exp2(s)
        return st[0] + jnp.sum(e, axis=0, keepdims=True), st[1] + _dot(v_t, e.astype(BF16))

    zero = (jnp.zeros((1, N), F32), jnp.zeros((HEAD_DIM, N), F32))

    def selected_fast():
        J0 = jnp.maximum(i - 1, 0)
        bias = sbias_ref[0, pl.ds(pl.multiple_of((2 - (i - J0)) * Q_BLOCK, Q_BLOCK), 2 * Q_BLOCK), :]
        st = add_chunk(zero, _dot(k_rows(ks_ref, J0, 2), q_t) + bias + sel_mask(J0, 2), v_cols(vs_ref, J0, 2))
        n_far = jnp.maximum(i - 1, 0)

        def far_chunk(J0, st, n_valid):
            mask = sel_mask(J0, FAR_TILES)
            if n_valid is not None:
                row = lax.broadcasted_iota(jnp.int32, mask.shape, 0)
                mask = jnp.where(row < n_valid * Q_BLOCK, mask, NEG_INF)
            return add_chunk(st, _dot(k_rows(ks_ref, J0, FAR_TILES), q_t) + mask, v_cols(vs_ref, J0, FAR_TILES))

        n_chunks = n_far // FAR_TILES
        st = lax.fori_loop(0, n_chunks, lambda t, st: far_chunk(n_far - FAR_TILES * (t + 1), st, None), st)
        rem = n_far - FAR_TILES * n_chunks
        return lax.cond(rem > 0, lambda st: far_chunk(0, st, rem), lambda st: st, st)

    def window_fast():
        J0 = jnp.maximum(i - n_win, 0)
        bias = wbias_ref[0, pl.ds(pl.multiple_of((n_win - (i - J0)) * Q_BLOCK, Q_BLOCK), (n_win + 1) * Q_BLOCK), :]
        return add_chunk(zero, _dot(k_rows(kw_ref, J0, n_win + 1), q_t) + bias, v_cols(vw_ref, J0, n_win + 1))

    def sweep_exact(k_ref, v_ref, bias_ref, far_dist, n_back, with_sel):
        def body(t, st):
            J = i - t
            off = pl.multiple_of((far_dist - jnp.minimum(t, far_dist)) * Q_BLOCK, Q_BLOCK)
            s = _dot(k_rows(k_ref, J, 1), q_t) + bias_ref[0, pl.ds(off, Q_BLOCK), :]
            if with_sel:
                s = s + sel_mask(J, 1)
            return _softmax_tile_update(s, v_ref[0, J], *st)

        init = (jnp.full((1, N), NEG_INF, F32),) + zero
        _, l, acc = lax.fori_loop(0, jnp.minimum(i, n_back) + 1, body, init)
        return l, acc

    p_c, l_c, o_c = compressed(True)
    select_blocks(p_c)
    l_s, a_s = selected_fast()
    l_w, a_w = window_fast()
    lane_q = lax.broadcasted_iota(jnp.int32, (1, N), 1) % Q_BLOCK
    has_cmp = (i > 0) | (lane_q >= CMP_LEN - 1)
    l_min = jnp.minimum(jnp.minimum(l_s, l_w), jnp.where(has_cmp, l_c, 1.0))

    def exact_path():
        p_x, _, o_x = compressed(False)
        select_blocks(p_x)
        l_sx, a_sx = sweep_exact(ks_ref, vs_ref, sbias_ref, 2, nb, True)
        l_wx, a_wx = sweep_exact(kw_ref, vw_ref, wbias_ref, n_win, n_win, False)
        return o_x, a_sx * (1.0 / l_sx), a_wx * (1.0 / l_wx)

    o_c, o_s, o_w = lax.cond(jnp.min(l_min) > NSA_TINY,
                             lambda: (o_c, a_s * (1.0 / l_s), a_w * (1.0 / l_w)), exact_path)

    def gate_row(br):
        return jnp.concatenate([gate_ref[br, 0, r:r + 1, :] for r in range(R)], axis=1)

    o_t = gate_row(0) * o_c + gate_row(1) * o_s + gate_row(2) * o_w
    o_ref[0] = _pairs_to_rows(o_t).astype(o_ref.dtype)


def _nsa_attention(colmat, rowmat, cmp_rows, cmp_cols, gates, cbias, sbias, wbias, ov_t, msel, B, S):
    nb = S // Q_BLOCK
    nc = S // CMP_STRIDE
    n_slc = S // SLC_LEN
    G, R = NSA_KV_HEADS, NSA_REP
    kern = functools.partial(_nsa_kernel, nb=nb)
    return pl.pallas_call(
        kern,
        grid=(B, G, nb),
        in_specs=[
            pl.BlockSpec((R, 1, HEAD_DIM, LANES), lambda b, g, i: (g, b * nb + i, 0, 0)),
            pl.BlockSpec((1, 1, nc, HEAD_DIM), lambda b, g, i: (g, b, 0, 0)),
            pl.BlockSpec((1, 1, HEAD_DIM, nc), lambda b, g, i: (NSA_KV_HEADS + g, b, 0, 0)),
            pl.BlockSpec((1, S, HEAD_DIM), lambda b, g, i: (ROW_KS + g, b, 0)),
            pl.BlockSpec((1, nb, HEAD_DIM, LANES), lambda b, g, i: (COL_VS + g, b, 0, 0)),
            pl.BlockSpec((1, S, HEAD_DIM), lambda b, g, i: (ROW_KW + g, b, 0)),
            pl.BlockSpec((1, nb, HEAD_DIM, LANES), lambda b, g, i: (COL_VW + g, b, 0, 0)),
            pl.BlockSpec((N_BRANCHES, 1, R, LANES), lambda b, g, i: (0, g, 0, b * nb + i)),
            pl.BlockSpec((1, cbias.shape[1], R * Q_BLOCK), lambda b, g, i: (g, 0, 0)),
            pl.BlockSpec((1, sbias.shape[1], R * Q_BLOCK), lambda b, g, i: (g, 0, 0)),
            pl.BlockSpec((1, wbias.shape[1], R * Q_BLOCK), lambda b, g, i: (g, 0, 0)),
            pl.BlockSpec((n_slc, nc), lambda b, g, i: (0, 0)),
            pl.BlockSpec((1, 1, Q_BLOCK), lambda b, g, i: (g, 0, 0)),
        ],
        out_specs=pl.BlockSpec((1, Q_BLOCK, R * HEAD_DIM), lambda b, g, i: (b, i, g)),
        out_shape=jax.ShapeDtypeStruct((B, S, NSA_HEADS * HEAD_DIM), BF16),
        scratch_shapes=[pltpu.VMEM((n_slc, Q_BLOCK), F32)],
        compiler_params=_cparams(("parallel", "parallel", "arbitrary")),
        name="nsa_attention",
    )(colmat, cmp_rows, cmp_cols, rowmat, colmat, rowmat, colmat, gates, cbias, sbias, wbias, ov_t, msel)


SB_QB = 512
SB_KT = 256
SB_ZERO_LOG = -110.0


def _sb_kernel(q_ref, k_ref, v_ref, tri_ref, o_ref):
    i = pl.program_id(1)
    H = SB_HEADS
    tri = tri_ref[...]
    rows = lax.broadcasted_iota(jnp.int32, (SB_QB, SB_KT), 0)
    cols = lax.broadcasted_iota(jnp.int32, (SB_QB, SB_KT), 1)
    key_minus_query = cols - rows
    per_q = SB_QB // SB_KT

    def step(J, st, masked):
        out = []
        for h in range(H):
            carry, acc = st[h]
            k_t = jnp.concatenate([k_ref[h, (SB_KT // LANES) * J + c] for c in range(SB_KT // LANES)], axis=1)
            z = _dot(q_ref[h], k_t)
            sp = jnp.maximum(z, 0.0) + jnp.log(1.0 + jnp.exp(-jnp.abs(z)))
            if masked:
                valid = key_minus_query < i * SB_QB - J * SB_KT
                lf = jnp.where(valid, -sp, 0.0)
            else:
                lf = -sp
            lf_hi = lf.astype(BF16)
            lf_lo = (lf - lf_hi.astype(F32)).astype(BF16)
            after = _dot(lf_hi, tri) + _dot(lf_lo, tri) + carry
            w = jnp.exp(z - sp + after)
            if masked:
                w = jnp.where(valid, w, 0.0)
            v = v_ref[h, pl.ds(pl.multiple_of(J * SB_KT, SB_KT), SB_KT), :]
            acc = acc + _dot(w.astype(BF16), v)
            carry = carry + jnp.sum(lf, axis=1, keepdims=True)
            out.append((carry, acc))
        return tuple(out)

    st = tuple((jnp.zeros((SB_QB, 1), F32), jnp.zeros((SB_QB, HEAD_DIM), F32)) for _ in range(H))
    for d in range(per_q):
        st = step(per_q * i + per_q - 1 - d, st, True)

    def alive(st):
        worst = st[0][0]
        for h in range(1, H):
            worst = jnp.maximum(worst, st[h][0])
        return jnp.max(worst) > SB_ZERO_LOG

    def body(c):
        J, _, st = c
        st = step(J, st, False)
        return J - 1, alive(st), st

    _, _, st = lax.while_loop(lambda c: (c[0] >= 0) & c[1], body, (per_q * i - 1, alive(st), st))
    o_ref[0] = jnp.concatenate([st[h][1] for h in range(H)], axis=1).astype(o_ref.dtype)


def _sb_attention(colmat, rowmat, tri, B, S):
    nq = S // SB_QB
    nb = S // LANES
    H = SB_HEADS
    return pl.pallas_call(
        _sb_kernel,
        grid=(B, nq),
        in_specs=[
            pl.BlockSpec((H, SB_QB, HEAD_DIM), lambda b, i: (ROW_SQ // H, b * nq + i, 0)),
            pl.BlockSpec((H, nb, HEAD_DIM, LANES), lambda b, i: (COL_SK // H, b, 0, 0)),
            pl.BlockSpec((H, S, HEAD_DIM), lambda b, i: (ROW_SV // H, b, 0)),
            pl.BlockSpec((SB_KT, SB_KT), lambda b, i: (0, 0)),
        ],
        out_specs=pl.BlockSpec((1, SB_QB, H * HEAD_DIM), lambda b, i: (b, i, 0)),
        out_shape=jax.ShapeDtypeStruct((B, S, H * HEAD_DIM), BF16),
        compiler_params=_cparams(("parallel", "arbitrary")),
        name="sb_attention",
    )(rowmat, colmat, rowmat, tri)


def _swa_kernel(q_ref, k_ref, v_ref, bias_ref, sink_ref, o_ref):
    i = pl.program_id(2)
    R = SWA_REP
    q_t = jnp.concatenate([q_ref[r, 0] for r in range(R)], axis=1)
    jp = jnp.maximum(i - 1, 0)
    k0 = k_ref[0, pl.ds(pl.multiple_of(i * Q_BLOCK, Q_BLOCK), Q_BLOCK), :]
    k1 = k_ref[0, pl.ds(pl.multiple_of(jp * Q_BLOCK, Q_BLOCK), Q_BLOCK), :]
    s0 = _dot(k0, q_t) + bias_ref[0, 0]
    s1 = _dot(k1, q_t) + bias_ref[0, 1] + jnp.where(i >= 1, 0.0, NEG_INF)
    sink = sink_ref[0]
    m = jnp.maximum(jnp.maximum(jnp.max(s0, axis=0, keepdims=True), jnp.max(s1, axis=0, keepdims=True)), sink)
    e0 = jnp.exp(s0 - m)
    e1 = jnp.exp(s1 - m)
    den = jnp.sum(e0, axis=0, keepdims=True) + jnp.sum(e1, axis=0, keepdims=True) + jnp.exp(sink - m)
    inv = 1.0 / den
    o_t = _dot(v_ref[0, i], (e0 * inv).astype(BF16)) + _dot(v_ref[0, jp], (e1 * inv).astype(BF16))
    o_ref[0] = _pairs_to_rows(o_t).astype(o_ref.dtype)


def _swa_attention(colmat, rowmat, bias, sinks, B, S):
    nb = S // Q_BLOCK
    G, R = SWA_KV_HEADS, SWA_REP
    return pl.pallas_call(
        _swa_kernel,
        grid=(B, G, nb),
        in_specs=[
            pl.BlockSpec((R, 1, HEAD_DIM, LANES), lambda b, g, i: (COL_WQ // R + g, b * nb + i, 0, 0)),
            pl.BlockSpec((1, S, HEAD_DIM), lambda b, g, i: (ROW_WK + g, b, 0)),
            pl.BlockSpec((1, nb, HEAD_DIM, LANES), lambda b, g, i: (COL_WV + g, b, 0, 0)),
            pl.BlockSpec((1, 2, Q_BLOCK, R * Q_BLOCK), lambda b, g, i: (g, 0, 0, 0)),
            pl.BlockSpec((1, 1, R * Q_BLOCK), lambda b, g, i: (g, 0, 0)),
        ],
        out_specs=pl.BlockSpec((1, Q_BLOCK, R * HEAD_DIM), lambda b, g, i: (b, i, g)),
        out_shape=jax.ShapeDtypeStruct((B, S, SWA_HEADS * HEAD_DIM), BF16),
        compiler_params=_cparams(("parallel", "parallel", "arbitrary")),
        name="swa_attention",
    )(colmat, rowmat, colmat, bias, sinks)


MERGE_TM = 512


def _merge_kernel(x_ref, nw_ref, wg_ref, ya_ref, yb_ref, yc_ref, ua_ref, ub_ref, uc_ref, wo_ref, o_ref):
    x = x_ref[...]
    h = _rms_rows(x, nw_ref[...]).astype(BF16)
    merged = None
    for br, (y_ref, u_ref) in enumerate(((ya_ref, ua_ref), (yb_ref, ub_ref), (yc_ref, uc_ref))):
        g = jax.nn.sigmoid(_dot(h, wg_ref[:, br * D_MODEL:(br + 1) * D_MODEL]))
        t = g * _dot(y_ref[...], u_ref[...])
        merged = t if merged is None else merged + t
    o_ref[...] = x + _dot(merged.astype(BF16), wo_ref[...])


def _merge(x2, nw, w_bgate, y_nsa, y_sb, y_swa, u_nsa, u_sb, u_swa, w_out):
    T = x2.shape[0]
    full = lambda shape: pl.BlockSpec(shape, lambda i: (0, 0))
    rows = lambda n: pl.BlockSpec((MERGE_TM, n), lambda i: (i, 0))
    return pl.pallas_call(
        _merge_kernel,
        grid=(T // MERGE_TM,),
        in_specs=[rows(D_MODEL), full((1, D_MODEL)), full((D_MODEL, N_BRANCHES * D_MODEL)),
                  rows(y_nsa.shape[1]), rows(y_sb.shape[1]), rows(y_swa.shape[1]),
                  full(u_nsa.shape), full(u_sb.shape), full(u_swa.shape), full((D_MODEL, D_MODEL))],
        out_specs=rows(D_MODEL),
        out_shape=jax.ShapeDtypeStruct((T, D_MODEL), F32),
        compiler_params=_cparams(("parallel",)),
        name="merge",
    )(x2, nw.reshape(1, D_MODEL), w_bgate.astype(BF16), y_nsa, y_sb, y_swa,
      u_nsa.astype(BF16), u_sb.astype(BF16), u_swa.astype(BF16), w_out.astype(BF16))


def _t5_bucket(dist):
    max_exact = NUM_BUCKETS // 2
    d = jnp.maximum(dist, 0)
    df = jnp.maximum(d, 1).astype(F32)
    large = max_exact + (jnp.log(df / max_exact) / math.log(MAX_DISTANCE / max_exact)
                         * (NUM_BUCKETS - max_exact)).astype(jnp.int32)
    large = jnp.minimum(large, NUM_BUCKETS - 1)
    return jnp.where(d < max_exact, d, large)


def _toeplitz(ext, delta):
    H = ext.shape[0]
    n = Q_BLOCK
    c = delta + MAX_DISTANCE
    w = ext[:, c - (n - 1):c + n]
    flat = jnp.pad(jnp.broadcast_to(w[:, None, :], (H, n, 2 * n - 1)), ((0, 0), (0, 0), (0, 1))).reshape(H, 2 * n * n)
    return flat[:, n - 1:n - 1 + n * (2 * n - 1)].reshape(H, n, 2 * n - 1)[:, :, :n]


def _lanes(t, G, R):
    K, Q = t.shape[1:]
    return t.reshape(G, R, K, Q).transpose(0, 2, 1, 3).reshape(G, K, R * Q)


def _bias_tables(rel_bias, nb, qk_bound):
    n = Q_BLOCK
    G, R = NSA_KV_HEADS, NSA_REP
    by_dist = rel_bias[_t5_bucket(jnp.arange(MAX_DISTANCE)), :].T.astype(F32)
    far = rel_bias[NUM_BUCKETS - 1, :].astype(F32)
    neg = jnp.full((1, n), NEG_INF, F32)
    k = jnp.arange(n)[:, None]
    q = jnp.arange(n)[None, :]
    above = (k > q)[None]

    rel = (by_dist[:NSA_HEADS] - far[:NSA_HEADS, None]) * LOG2E
    m_g = qk_bound + jnp.maximum(jnp.max(rel.reshape(G, -1), axis=1), 0.0)
    m_h = jnp.repeat(m_g, R)[:, None, None]
    ext = jnp.concatenate([jnp.broadcast_to(neg, (NSA_HEADS, n)), rel, jnp.zeros((NSA_HEADS, n), F32)], axis=1)
    t0, t1 = _toeplitz(ext, 0), _toeplitz(ext, n)
    zeros = jnp.zeros_like(t0)
    masked = jnp.full_like(t0, NEG_INF)
    sbias = _lanes(jnp.concatenate([zeros, t1, t0, masked], axis=1), G, R)
    n_win = NSA_WINDOW // n
    edge = jnp.where(above, zeros, NEG_INF)
    tiles = [edge] + [zeros] * (n_win - 2) + [t1, t0]
    wbias = _lanes(jnp.concatenate([t - m_h for t in tiles] + [masked] * n_win, axis=1), G, R)
    off = 8 * (nb - 1)
    near = jnp.stack([ext[:, 97 - CMP_STRIDE * m:97 - CMP_STRIDE * m + n] for m in range(-9, 7)], axis=1)
    n_rows = 16 * nb - 8
    cb = jnp.concatenate([jnp.zeros((NSA_HEADS, off - 9, n), F32), near,
                          jnp.full((NSA_HEADS, n_rows - off - 7, n), NEG_INF, F32)], axis=1) - m_h
    cbias = _lanes(cb, G, R)
    msel = jnp.broadcast_to(-m_g[:, None, None], (G, 1, n))

    hs = slice(NSA_HEADS, NSA_HEADS + SWA_HEADS)
    ext_w = jnp.concatenate([jnp.broadcast_to(neg, (SWA_HEADS, n)), by_dist[hs],
                             jnp.broadcast_to(far[hs, None], (SWA_HEADS, n))], axis=1)
    s0 = _toeplitz(ext_w, 0)
    s1 = jnp.where(above, _toeplitz(ext_w, n), NEG_INF)
    swa_bias = jnp.stack([_lanes(s0, SWA_KV_HEADS, SWA_REP), _lanes(s1, SWA_KV_HEADS, SWA_REP)], axis=1)
    return cbias, sbias, wbias, msel, swa_bias


def _overlap_t(S):
    nc, n_slc = S // CMP_STRIDE, S // SLC_LEN
    n = np.arange(nc)[None, :]
    j = np.arange(n_slc)[:, None]
    ov = (n * CMP_STRIDE < (j + 1) * SLC_LEN) & (n * CMP_STRIDE + CMP_LEN - 1 >= j * SLC_LEN) & (n < nc - 1)
    return jnp.asarray(ov, BF16)


def _head_cols(w, heads):
    return jnp.concatenate([w[:, off:off + HEAD_DIM] for _, off in heads], axis=1)


def _head_vectors(heads, norm_of, scale_of):
    flags, gains = [], []
    ones = jnp.ones((HEAD_DIM,), F32)
    for name, _ in heads:
        w = norm_of.get(name)
        flags.append(ones if w is not None else 0.0 * ones)
        g = w.astype(F32) if w is not None else ones
        gains.append(g * scale_of.get(name, 1.0))
    return jnp.concatenate(flags).reshape(1, -1), jnp.concatenate(gains).reshape(1, -1)


def kernel(x, rel_bias, ffn1_norm, ffn1_w_gate, ffn1_w_up, ffn1_w_down, mix_norm, w_in, nsa_q_norm, nsa_k_norm, nsa_cmp_pos, nsa_cmp_k_w1, nsa_cmp_k_w2, nsa_cmp_v_w1, nsa_cmp_v_w2, swa_q_norm, swa_k_norm, swa_sinks, w_up_nsa, w_up_sb, w_up_swa, w_out, ffn2_norm, ffn2_w_gate, ffn2_w_up, ffn2_w_down):
    B, S, D = x.shape
    T = B * S
    nb = S // Q_BLOCK
    nc = S // CMP_STRIDE
    depth = w_in.shape[0]
    scale = HEAD_DIM ** -0.5
    scale_of = {"nq": scale * LOG2E, "sq": scale, "wq": scale}
    qk_bound = (BOUND_SLACK * LOG2E * HEAD_DIM ** 0.5
                * jnp.max(jnp.abs(nsa_q_norm.astype(F32))) * jnp.max(jnp.abs(nsa_k_norm.astype(F32))))
    cbias, sbias, wbias, msel, swa_bias = _bias_tables(rel_bias, nb, qk_bound)
    ov_t = _overlap_t(S)
    tri = jnp.asarray(np.tril(np.ones((SB_KT, SB_KT), np.float32), -1), BF16)

    x2 = x.reshape(T, D)
    for l in range(depth):
        x2 = _ffn(x2, ffn1_norm[l], ffn1_w_gate[l], ffn1_w_up[l], ffn1_w_down[l])

        norm_of = {"nq": nsa_q_norm[l], "nks": nsa_k_norm[l], "nkw": nsa_k_norm[l],
                   "wq": swa_q_norm[l], "wk": swa_k_norm[l]}
        rflag, rgain = _head_vectors(ROW_HEADS, norm_of, scale_of)
        cflag, cgain = _head_vectors(COL_HEADS, norm_of, scale_of)
        rowmat = _proj(x2, mix_norm[l], _head_cols(w_in[l], ROW_HEADS).astype(BF16), rflag, rgain, False)
        colmat = _proj(x2, mix_norm[l], _head_cols(w_in[l], COL_HEADS).astype(BF16), cflag, cgain, True)
        gates = _nsa_gates(x2, mix_norm[l], w_in[l][:, OFF_NGATE:OFF_NGATE + N_GATE])
        gates = gates[:N_GATE].reshape(N_BRANCHES, NSA_KV_HEADS, NSA_REP, T)

        rows4 = rowmat[ROW_KC:ROW_KC + 4].reshape(4, B, nc, CMP_STRIDE * HEAD_DIM)
        cmp_rows, cmp_cols = _compress(rows4, jnp.stack([nsa_cmp_k_w1[l], nsa_cmp_v_w1[l]]),
                                       jnp.stack([nsa_cmp_k_w2[l], nsa_cmp_v_w2[l]]),
                                       nsa_cmp_pos[l], nsa_k_norm[l])

        y_nsa = _nsa_attention(colmat, rowmat, cmp_rows, cmp_cols, gates, cbias, sbias, wbias, ov_t, msel, B, S)
        y_sb = _sb_attention(colmat, rowmat, tri, B, S)
        sinks = jnp.repeat(swa_sinks[l].astype(F32).reshape(SWA_KV_HEADS, 1, SWA_REP), Q_BLOCK, axis=2)
        y_swa = _swa_attention(colmat, rowmat, swa_bias, sinks, B, S)

        x2 = _merge(x2, mix_norm[l], w_in[l][:, OFF_BGATE:], y_nsa.reshape(T, -1), y_sb.reshape(T, -1),
                    y_swa.reshape(T, -1), w_up_nsa[l], w_up_sb[l], w_up_swa[l], w_out[l])
        x2 = _ffn(x2, ffn2_norm[l], ffn2_w_gate[l], ffn2_w_up[l], ffn2_w_down[l])
    return x2.reshape(B, S, D)
```

```python
import functools
import math

import numpy as np
import jax
import jax.numpy as jnp
from jax import lax
from jax.experimental import pallas as pl
from jax.experimental.pallas import tpu as pltpu

D_MODEL = 1024
HEAD_DIM = 64
Q_BLOCK = 128
NSA_HEADS = 8
NSA_KV_HEADS = 2
NSA_REP = NSA_HEADS // NSA_KV_HEADS
CMP_LEN = 32
CMP_STRIDE = 16
CMP_HIDDEN = 256
SLC_LEN = 64
SLC_TOPN = 16
NSA_WINDOW = 512
FORCE_SCORE = 1e4
SB_HEADS = 4
SWA_HEADS = 4
SWA_KV_HEADS = 2
SWA_REP = SWA_HEADS // SWA_KV_HEADS
SWA_WINDOW = 128
NUM_BUCKETS = 32
MAX_DISTANCE = 128
D_FF = 2816
NORM_EPS = 1e-6
NEG_INF = -1e30
N_BRANCHES = 3
N_GATE = 3 * NSA_HEADS

LANES = 128
VMEM_LIMIT = 56 * 1024 * 1024
BF16 = jnp.bfloat16
F32 = jnp.float32

_sizes = (NSA_HEADS * HEAD_DIM,) + (NSA_KV_HEADS * HEAD_DIM,) * 6 + (N_GATE,) \
    + (SB_HEADS * HEAD_DIM,) * 3 + (SWA_HEADS * HEAD_DIM, SWA_KV_HEADS * HEAD_DIM,
                                    SWA_KV_HEADS * HEAD_DIM, N_BRANCHES * D_MODEL)
_offs = np.concatenate([[0], np.cumsum(_sizes)])
(OFF_NQ, OFF_NKC, OFF_NVC, OFF_NKS, OFF_NVS, OFF_NKW, OFF_NVW, OFF_NGATE,
 OFF_SQ, OFF_SK, OFF_SV, OFF_WQ, OFF_WK, OFF_WV, OFF_BGATE) = (int(o) for o in _offs[:-1])

ROW_HEADS = ([("nkc", OFF_NKC + 64 * g) for g in range(2)] + [("nvc", OFF_NVC + 64 * g) for g in range(2)]
             + [("nks", OFF_NKS + 64 * g) for g in range(2)] + [("nkw", OFF_NKW + 64 * g) for g in range(2)]
             + [("sq", OFF_SQ + 64 * h) for h in range(4)] + [("sv", OFF_SV + 64 * h) for h in range(4)]
             + [("wk", OFF_WK + 64 * g) for g in range(2)])
ROW_KC, ROW_VC, ROW_KS, ROW_KW, ROW_SQ, ROW_SV, ROW_WK = 0, 2, 4, 6, 8, 12, 16
COL_HEADS = ([("nq", OFF_NQ + 64 * h) for h in range(8)] + [("nvs", OFF_NVS + 64 * g) for g in range(2)]
             + [("nvw", OFF_NVW + 64 * g) for g in range(2)] + [("sk", OFF_SK + 64 * h) for h in range(4)]
             + [("wq", OFF_WQ + 64 * h) for h in range(4)] + [("wv", OFF_WV + 64 * g) for g in range(2)])
COL_NQ, COL_VS, COL_VW, COL_SK, COL_WQ, COL_WV = 0, 8, 10, 12, 16, 20


def _dot(a, b):
    return jnp.dot(a, b, preferred_element_type=F32)


def _rms_rows(x, w):
    ms = jnp.mean(x * x, axis=-1, keepdims=True)
    return x * lax.rsqrt(ms + NORM_EPS) * w


def _cparams(sem):
    return pltpu.CompilerParams(dimension_semantics=sem, vmem_limit_bytes=VMEM_LIMIT)


FFN_TM = 512
FFN_TF = 1408


def _ffn_kernel(x_ref, nw_ref, wg_ref, wu_ref, wd_ref, o_ref, h_ref, acc_ref):
    f = pl.program_id(1)

    @pl.when(f == 0)
    def _():
        h_ref[...] = _rms_rows(x_ref[...], nw_ref[...]).astype(BF16)
        acc_ref[...] = jnp.zeros_like(acc_ref)

    h = h_ref[...]
    g = _dot(h, wg_ref[...])
    u = _dot(h, wu_ref[...])
    a = (g * jax.nn.sigmoid(g)) * u
    acc_ref[...] += _dot(a.astype(BF16), wd_ref[...])

    @pl.when(f == pl.num_programs(1) - 1)
    def _():
        o_ref[...] = x_ref[...] + 0.5 * acc_ref[...]


def _ffn(x2, nw, wg, wu, wd):
    T = x2.shape[0]
    grid = (T // FFN_TM, D_FF // FFN_TF)
    return pl.pallas_call(
        _ffn_kernel,
        grid=grid,
        in_specs=[
            pl.BlockSpec((FFN_TM, D_MODEL), lambda i, f: (i, 0)),
            pl.BlockSpec((1, D_MODEL), lambda i, f: (0, 0)),
            pl.BlockSpec((D_MODEL, FFN_TF), lambda i, f: (0, f)),
            pl.BlockSpec((D_MODEL, FFN_TF), lambda i, f: (0, f)),
            pl.BlockSpec((FFN_TF, D_MODEL), lambda i, f: (f, 0)),
        ],
        out_specs=pl.BlockSpec((FFN_TM, D_MODEL), lambda i, f: (i, 0)),
        out_shape=jax.ShapeDtypeStruct((T, D_MODEL), F32),
        scratch_shapes=[pltpu.VMEM((FFN_TM, D_MODEL), BF16), pltpu.VMEM((FFN_TM, D_MODEL), F32)],
        compiler_params=_cparams(("parallel", "arbitrary")),
        name="ffn",
    )(x2, nw.reshape(1, D_MODEL), wg.astype(BF16), wu.astype(BF16), wd.astype(BF16))


PROJ_TM = 512


GATE_ROWS = 32

_NORMED = ("nq", "nks", "nkw", "wq", "wk")


def _proj_plan(scale_of):
    plan = []
    for kind, heads in (("row", ROW_HEADS), ("col", COL_HEADS)):
        for p in range(len(heads) // 2):
            name = heads[2 * p][0]
            assert heads[2 * p + 1][0] == name
            plan.append((kind, p, name in _NORMED, 1.0 if name in _NORMED else scale_of.get(name, 1.0)))
    plan.append(("gate", 0, False, 1.0))
    return tuple(plan)


def _proj_kernel(x_ref, nw_ref, w_ref, gain_ref, bd_ref, row_ref, col_ref, gate_ref, *, plan):
    h = _rms_rows(x_ref[...], nw_ref[...]).astype(BF16)
    pair = 2 * LANES
    for c0 in range(0, len(plan) * LANES, pair):
        width = min(pair, len(plan) * LANES - c0)
        zg = _dot(h, w_ref[:, c0:c0 + width])
        for u in range(width // LANES):
            t = c0 // LANES + u
            kind, p, normed, scale = plan[t]
            z = zg[:, u * LANES:(u + 1) * LANES]
            if normed:
                ms = _dot((z * z).astype(BF16), bd_ref[...])
                z = z * lax.rsqrt(ms + NORM_EPS) * gain_ref[:, t * LANES:(t + 1) * LANES]
            elif scale != 1.0:
                z = z * scale
            if kind == "row":
                row_ref[2 * p] = z[:, :HEAD_DIM].astype(row_ref.dtype)
                row_ref[2 * p + 1] = z[:, HEAD_DIM:].astype(row_ref.dtype)
            elif kind == "col":
                z_t = z.T
                for c in range(PROJ_TM // LANES):
                    col_ref[2 * p, c] = z_t[:HEAD_DIM, c * LANES:(c + 1) * LANES].astype(col_ref.dtype)
                    col_ref[2 * p + 1, c] = z_t[HEAD_DIM:, c * LANES:(c + 1) * LANES].astype(col_ref.dtype)
            else:
                gate_ref[...] = jax.nn.sigmoid(z).T[:GATE_ROWS]


def _proj(x2, nw, w, gain, plan):
    T = x2.shape[0]
    n_row, n_col = len(ROW_HEADS), len(COL_HEADS)
    bd = np.kron(np.eye(2, dtype=np.float32), np.full((HEAD_DIM, HEAD_DIM), 1.0 / HEAD_DIM, np.float32))
    return pl.pallas_call(
        functools.partial(_proj_kernel, plan=plan),
        grid=(T // PROJ_TM,),
        in_specs=[
            pl.BlockSpec((PROJ_TM, D_MODEL), lambda i: (i, 0)),
            pl.BlockSpec((1, D_MODEL), lambda i: (0, 0)),
            pl.BlockSpec(w.shape, lambda i: (0, 0)),
            pl.BlockSpec(gain.shape, lambda i: (0, 0)),
            pl.BlockSpec((LANES, LANES), lambda i: (0, 0)),
        ],
        out_specs=[
            pl.BlockSpec((n_row, PROJ_TM, HEAD_DIM), lambda i: (0, i, 0)),
            pl.BlockSpec((n_col, PROJ_TM // LANES, HEAD_DIM, LANES), lambda i: (0, i, 0, 0)),
            pl.BlockSpec((GATE_ROWS, PROJ_TM), lambda i: (0, i)),
        ],
        out_shape=[jax.ShapeDtypeStruct((n_row, T, HEAD_DIM), BF16),
                   jax.ShapeDtypeStruct((n_col, T // LANES, HEAD_DIM, LANES), BF16),
                   jax.ShapeDtypeStruct((GATE_ROWS, T), F32)],
        compiler_params=_cparams(("parallel",)),
        name="proj",
    )(x2, nw.reshape(1, D_MODEL), w, gain, jnp.asarray(bd, BF16))


def _compress_kernel(r_ref, w1_ref, w2_ref, pos_ref, kn_ref, row_ref, col_ref):
    kind = pl.program_id(0) // NSA_KV_HEADS
    r = r_ref[0, 0]
    nc = r.shape[0]
    half = CMP_STRIDE * HEAD_DIM
    w1 = w1_ref[0]
    p_lo = _dot(r, w1[:half])
    p_hi = _dot(r, w1[half:])
    p_pos = _dot(jnp.broadcast_to(pos_ref[...], (8, CMP_LEN * HEAD_DIM)).astype(BF16), w1)[:1]
    hdn = p_lo + pltpu.roll(p_hi, nc - 1, 0) + p_pos
    hdn = hdn * jax.nn.sigmoid(hdn)
    out = _dot(hdn.astype(BF16), w2_ref[0])
    ms = jnp.sum(out * out, axis=-1, keepdims=True) * (1.0 / HEAD_DIM)
    normed = out * lax.rsqrt(ms + NORM_EPS) * kn_ref[...]
    out = jnp.where(kind == 0, normed, out)
    row_ref[0, 0] = out[:, :HEAD_DIM].astype(row_ref.dtype)
    col_ref[0, 0] = out.T[:HEAD_DIM].astype(col_ref.dtype)


def _compress(rows4, w1, w2, pos, k_norm):
    _, B, NC, _ = rows4.shape
    kn = jnp.pad(k_norm.reshape(1, HEAD_DIM), ((0, 0), (0, LANES - HEAD_DIM)))
    w2p = jnp.pad(w2, ((0, 0), (0, 0), (0, LANES - HEAD_DIM))).astype(BF16)
    return pl.pallas_call(
        _compress_kernel,
        grid=(4, B),
        in_specs=[
            pl.BlockSpec((1, 1, NC, CMP_STRIDE * HEAD_DIM), lambda h, b: (h, b, 0, 0)),
            pl.BlockSpec((1, CMP_LEN * HEAD_DIM, CMP_HIDDEN), lambda h, b: (h // NSA_KV_HEADS, 0, 0)),
            pl.BlockSpec((1, CMP_HIDDEN, LANES), lambda h, b: (h // NSA_KV_HEADS, 0, 0)),
            pl.BlockSpec((1, CMP_LEN * HEAD_DIM), lambda h, b: (0, 0)),
            pl.BlockSpec((1, LANES), lambda h, b: (0, 0)),
        ],
        out_specs=[
            pl.BlockSpec((1, 1, NC, HEAD_DIM), lambda h, b: (h, b, 0, 0)),
            pl.BlockSpec((1, 1, HEAD_DIM, NC), lambda h, b: (h, b, 0, 0)),
        ],
        out_shape=[jax.ShapeDtypeStruct((4, B, NC, HEAD_DIM), BF16),
                   jax.ShapeDtypeStruct((4, B, HEAD_DIM, NC), BF16)],
        compiler_params=_cparams(("parallel", "parallel")),
        name="nsa_compress",
    )(rows4, w1.astype(BF16), w2p, pos.reshape(1, CMP_LEN * HEAD_DIM), kn)


def _softmax_tile_update(s, v_t, m, l, acc):
    m_new = jnp.maximum(m, jnp.max(s, axis=0, keepdims=True))
    alpha = jnp.exp2(m - m_new)
    p = jnp.exp2(s - m_new)
    l_new = alpha * l + jnp.sum(p, axis=0, keepdims=True)
    acc_new = alpha * acc + _dot(v_t, p.astype(BF16))
    return m_new, l_new, acc_new


def _pairs_to_rows(o_t):
    n = o_t.shape[1] // LANES
    outs = []
    for p in range(n // 2):
        a = o_t[:, (2 * p) * LANES:(2 * p + 1) * LANES]
        b = o_t[:, (2 * p + 1) * LANES:(2 * p + 2) * LANES]
        outs.append(jnp.concatenate([a, b], axis=0).T)
    return outs[0] if len(outs) == 1 else jnp.concatenate(outs, axis=1)


FAR_TILES = 4
LOG2E = math.log2(math.e)
NSA_TINY = 2.0 ** -100
BOUND_SLACK = 1.05


def _nsa_kernel(q_ref, kc_ref, vc_ref, ks_ref, vs_ref, kw_ref, vw_ref, gate_ref,
                cbias_ref, sbias_ref, wbias_ref, ov_ref, msel_ref, o_ref, neg_ref, *, nb):
    i = pl.program_id(2)
    R = NSA_REP
    N = R * Q_BLOCK
    nc = kc_ref.shape[2]
    q_t = jnp.concatenate([q_ref[r, 0] for r in range(R)], axis=1)
    n_win = NSA_WINDOW // Q_BLOCK

    def k_rows(ref, J0, n_tiles):
        return ref[0, pl.ds(pl.multiple_of(J0 * Q_BLOCK, Q_BLOCK), n_tiles * Q_BLOCK), :]

    def v_cols(ref, J0, n_tiles):
        return jnp.concatenate([ref[0, J0 + u] for u in range(n_tiles)], axis=1)

    def sel_mask(J0, n_tiles):
        parts = [jnp.broadcast_to(neg_ref[pl.ds(2 * J0 + u, 1), :], (SLC_LEN, Q_BLOCK)) for u in range(2 * n_tiles)]
        return jnp.concatenate([jnp.concatenate(parts, axis=0)] * R, axis=1)

    def compressed(fast):
        cb = cbias_ref[0, pl.ds(pl.multiple_of(8 * (nb - 1) - 8 * i, 8), nc), :]
        s = _dot(kc_ref[0, 0], q_t) + cb
        if fast:
            e = jnp.exp2(s)
        else:
            e = jnp.where(s > 0.5 * NEG_INF, jnp.exp2(s - jnp.max(s, axis=0, keepdims=True)), 0.0)
        l = jnp.sum(e, axis=0, keepdims=True)
        p = e * jnp.where(l > 0.0, 1.0 / l, 0.0)
        return p, l, _dot(vc_ref[0, 0], p.astype(BF16))

    def select_blocks(p_c):
        p_sum = p_c[:, 0:Q_BLOCK]
        for r in range(1, R):
            p_sum = p_sum + p_c[:, r * Q_BLOCK:(r + 1) * Q_BLOCK]
        p_hi = p_sum.astype(BF16)
        p_lo = (p_sum - p_hi.astype(F32)).astype(BF16)
        imp = _dot(ov_ref[...], p_hi) + _dot(ov_ref[...], p_lo)
        n_slc = imp.shape[0]
        jj = lax.broadcasted_iota(jnp.int32, imp.shape, 0)
        qq = lax.broadcasted_iota(jnp.int32, imp.shape, 1)
        cur = 2 * i + jnp.where(qq >= SLC_LEN, 1, 0)
        forced = (jj == 0) | (jj == cur) | (jj == cur - 1)
        score = jnp.where(forced, FORCE_SCORE, jnp.where(jj <= cur, imp, -1.0))
        sel = jnp.zeros(imp.shape, F32)
        for _ in range(min(SLC_TOPN, n_slc)):
            best = jnp.max(score, axis=0, keepdims=True)
            first = jnp.min(jnp.where(score == best, jj, n_slc), axis=0, keepdims=True)
            hit = jj == first
            sel = jnp.where(hit, 1.0, sel)
            score = jnp.where(hit, -2.0, score)
        neg_ref[...] = jnp.where(sel > 0.0, msel_ref[0], NEG_INF)

    def add_chunk(st, s, v_t):
        e = jnp.exp2(s)
        return st[0] + jnp.sum(e, axis=0, keepdims=True), st[1] + _dot(v_t, e.astype(BF16))

    zero = (jnp.zeros((1, N), F32), jnp.zeros((HEAD_DIM, N), F32))

    def selected_fast():
        J0 = jnp.maximum(i - 1, 0)
        bias = sbias_ref[0, pl.ds(pl.multiple_of((2 - (i - J0)) * Q_BLOCK, Q_BLOCK), 2 * Q_BLOCK), :]
        st = add_chunk(zero, _dot(k_rows(ks_ref, J0, 2), q_t) + bias + sel_mask(J0, 2), v_cols(vs_ref, J0, 2))
        n_far = jnp.maximum(i - 1, 0)

        def far_chunk(J0, st, n_valid):
            mask = sel_mask(J0, FAR_TILES)
            if n_valid is not None:
                row = lax.broadcasted_iota(jnp.int32, mask.shape, 0)
                mask = jnp.where(row < n_valid * Q_BLOCK, mask, NEG_INF)
            return add_chunk(st, _dot(k_rows(ks_ref, J0, FAR_TILES), q_t) + mask, v_cols(vs_ref, J0, FAR_TILES))

        n_chunks = n_far // FAR_TILES
        rem = n_far - FAR_TILES * n_chunks

        def pair(t, st):
            J0 = n_far - 2 * FAR_TILES * (t + 1)
            return far_chunk(J0, far_chunk(J0 + FAR_TILES, st, None), None)

        st = lax.fori_loop(0, n_chunks // 2, pair, st)
        st = lax.cond(n_chunks % 2 == 1, lambda st: far_chunk(rem, st, None), lambda st: st, st)
        return lax.cond(rem > 0, lambda st: far_chunk(0, st, rem), lambda st: st, st)

    def window_fast():
        J0 = jnp.maximum(i - n_win, 0)
        bias = wbias_ref[0, pl.ds(pl.multiple_of((n_win - (i - J0)) * Q_BLOCK, Q_BLOCK), (n_win + 1) * Q_BLOCK), :]
        return add_chunk(zero, _dot(k_rows(kw_ref, J0, n_win + 1), q_t) + bias, v_cols(vw_ref, J0, n_win + 1))

    def sweep_exact(k_ref, v_ref, bias_ref, far_dist, n_back, with_sel):
        def body(t, st):
            J = i - t
            off = pl.multiple_of((far_dist - jnp.minimum(t, far_dist)) * Q_BLOCK, Q_BLOCK)
            s = _dot(k_rows(k_ref, J, 1), q_t) + bias_ref[0, pl.ds(off, Q_BLOCK), :]
            if with_sel:
                s = s + sel_mask(J, 1)
            return _softmax_tile_update(s, v_ref[0, J], *st)

        init = (jnp.full((1, N), NEG_INF, F32),) + zero
        _, l, acc = lax.fori_loop(0, jnp.minimum(i, n_back) + 1, body, init)
        return l, acc

    p_c, l_c, o_c = compressed(True)
    l_w, a_w = window_fast()
    select_blocks(p_c)
    l_s, a_s = selected_fast()
    lane_q = lax.broadcasted_iota(jnp.int32, (1, N), 1) % Q_BLOCK
    has_cmp = (i > 0) | (lane_q >= CMP_LEN - 1)
    l_min = jnp.minimum(jnp.minimum(l_s, l_w), jnp.where(has_cmp, l_c, 1.0))

    def exact_path():
        p_x, _, o_x = compressed(False)
        select_blocks(p_x)
        l_sx, a_sx = sweep_exact(ks_ref, vs_ref, sbias_ref, 2, nb, True)
        l_wx, a_wx = sweep_exact(kw_ref, vw_ref, wbias_ref, n_win, n_win, False)
        return o_x, a_sx * (1.0 / l_sx), a_wx * (1.0 / l_wx)

    o_c, o_s, o_w = lax.cond(jnp.min(l_min) > NSA_TINY,
                             lambda: (o_c, a_s * (1.0 / l_s), a_w * (1.0 / l_w)), exact_path)

    def gate_row(br):
        return jnp.concatenate([gate_ref[br, 0, r:r + 1, :] for r in range(R)], axis=1)

    o_t = gate_row(0) * o_c + gate_row(1) * o_s + gate_row(2) * o_w
    o_ref[0] = _pairs_to_rows(o_t).astype(o_ref.dtype)


def _nsa_attention(colmat, rowmat, cmp_rows, cmp_cols, gates, cbias, sbias, wbias, ov_t, msel, B, S):
    nb = S // Q_BLOCK
    nc = S // CMP_STRIDE
    n_slc = S // SLC_LEN
    G, R = NSA_KV_HEADS, NSA_REP
    kern = functools.partial(_nsa_kernel, nb=nb)
    return pl.pallas_call(
        kern,
        grid=(B, G, nb),
        in_specs=[
            pl.BlockSpec((R, 1, HEAD_DIM, LANES), lambda b, g, i: (g, b * nb + i, 0, 0)),
            pl.BlockSpec((1, 1, nc, HEAD_DIM), lambda b, g, i: (g, b, 0, 0)),
            pl.BlockSpec((1, 1, HEAD_DIM, nc), lambda b, g, i: (NSA_KV_HEADS + g, b, 0, 0)),
            pl.BlockSpec((1, S, HEAD_DIM), lambda b, g, i: (ROW_KS + g, b, 0)),
            pl.BlockSpec((1, nb, HEAD_DIM, LANES), lambda b, g, i: (COL_VS + g, b, 0, 0)),
            pl.BlockSpec((1, S, HEAD_DIM), lambda b, g, i: (ROW_KW + g, b, 0)),
            pl.BlockSpec((1, nb, HEAD_DIM, LANES), lambda b, g, i: (COL_VW + g, b, 0, 0)),
            pl.BlockSpec((N_BRANCHES, 1, R, LANES), lambda b, g, i: (0, g, 0, b * nb + i)),
            pl.BlockSpec((1, cbias.shape[1], R * Q_BLOCK), lambda b, g, i: (g, 0, 0)),
            pl.BlockSpec((1, sbias.shape[1], R * Q_BLOCK), lambda b, g, i: (g, 0, 0)),
            pl.BlockSpec((1, wbias.shape[1], R * Q_BLOCK), lambda b, g, i: (g, 0, 0)),
            pl.BlockSpec((n_slc, nc), lambda b, g, i: (0, 0)),
            pl.BlockSpec((1, 1, Q_BLOCK), lambda b, g, i: (g, 0, 0)),
        ],
        out_specs=pl.BlockSpec((1, Q_BLOCK, R * HEAD_DIM), lambda b, g, i: (b, i, g)),
        out_shape=jax.ShapeDtypeStruct((B, S, NSA_HEADS * HEAD_DIM), BF16),
        scratch_shapes=[pltpu.VMEM((n_slc, Q_BLOCK), F32)],
        compiler_params=_cparams(("parallel", "parallel", "arbitrary")),
        name="nsa_attention",
    )(colmat, cmp_rows, cmp_cols, rowmat, colmat, rowmat, colmat, gates, cbias, sbias, wbias, ov_t, msel)


SB_QB = 512
SB_KT = 256
SB_ZERO_LOG = -110.0


def _sb_kernel(q_ref, k_ref, v_ref, tri_ref, o_ref):
    i = pl.program_id(1)
    H = SB_HEADS
    tri = tri_ref[...]
    rows = lax.broadcasted_iota(jnp.int32, (SB_QB, SB_KT), 0)
    cols = lax.broadcasted_iota(jnp.int32, (SB_QB, SB_KT), 1)
    key_minus_query = cols - rows
    per_q = SB_QB // SB_KT

    def step(J, st, masked):
        out = []
        for h in range(H):
            carry, acc = st[h]
            k_t = jnp.concatenate([k_ref[h, (SB_KT // LANES) * J + c] for c in range(SB_KT // LANES)], axis=1)
            z = _dot(q_ref[h], k_t)
            sp = jnp.maximum(z, 0.0) + jnp.log(1.0 + jnp.exp(-jnp.abs(z)))
            if masked:
                valid = key_minus_query < i * SB_QB - J * SB_KT
                lf = jnp.where(valid, -sp, 0.0)
            else:
                lf = -sp
            lf_hi = lf.astype(BF16)
            lf_lo = (lf - lf_hi.astype(F32)).astype(BF16)
            after = _dot(lf_hi, tri) + _dot(lf_lo, tri) + carry
            w = jnp.exp(z - sp + after)
            if masked:
                w = jnp.where(valid, w, 0.0)
            v = v_ref[h, pl.ds(pl.multiple_of(J * SB_KT, SB_KT), SB_KT), :]
            acc = acc + _dot(w.astype(BF16), v)
            carry = carry + jnp.sum(lf, axis=1, keepdims=True)
            out.append((carry, acc))
        return tuple(out)

    st = tuple((jnp.zeros((SB_QB, 1), F32), jnp.zeros((SB_QB, HEAD_DIM), F32)) for _ in range(H))
    for d in range(per_q):
        st = step(per_q * i + per_q - 1 - d, st, True)

    def alive(st):
        worst = st[0][0]
        for h in range(1, H):
            worst = jnp.maximum(worst, st[h][0])
        return jnp.max(worst) > SB_ZERO_LOG

    def body(c):
        J, _, st = c
        st = step(J, st, False)
        return J - 1, alive(st), st

    _, _, st = lax.while_loop(lambda c: (c[0] >= 0) & c[1], body, (per_q * i - 1, alive(st), st))
    o_ref[0] = jnp.concatenate([st[h][1] for h in range(H)], axis=1).astype(o_ref.dtype)


def _sb_attention(colmat, rowmat, tri, B, S):
    nq = S // SB_QB
    nb = S // LANES
    H = SB_HEADS
    return pl.pallas_call(
        _sb_kernel,
        grid=(B, nq),
        in_specs=[
            pl.BlockSpec((H, SB_QB, HEAD_DIM), lambda b, i: (ROW_SQ // H, b * nq + i, 0)),
            pl.BlockSpec((H, nb, HEAD_DIM, LANES), lambda b, i: (COL_SK // H, b, 0, 0)),
            pl.BlockSpec((H, S, HEAD_DIM), lambda b, i: (ROW_SV // H, b, 0)),
            pl.BlockSpec((SB_KT, SB_KT), lambda b, i: (0, 0)),
        ],
        out_specs=pl.BlockSpec((1, SB_QB, H * HEAD_DIM), lambda b, i: (b, i, 0)),
        out_shape=jax.ShapeDtypeStruct((B, S, H * HEAD_DIM), BF16),
        compiler_params=_cparams(("parallel", "arbitrary")),
        name="sb_attention",
    )(rowmat, colmat, rowmat, tri)


def _swa_kernel(q_ref, k_ref, v_ref, bias_ref, sink_ref, o_ref):
    i = pl.program_id(2)
    R = SWA_REP
    q_t = jnp.concatenate([q_ref[r, 0] for r in range(R)], axis=1)
    jp = jnp.maximum(i - 1, 0)
    k0 = k_ref[0, pl.ds(pl.multiple_of(i * Q_BLOCK, Q_BLOCK), Q_BLOCK), :]
    k1 = k_ref[0, pl.ds(pl.multiple_of(jp * Q_BLOCK, Q_BLOCK), Q_BLOCK), :]
    s0 = _dot(k0, q_t) + bias_ref[0, 0]
    s1 = _dot(k1, q_t) + bias_ref[0, 1] + jnp.where(i >= 1, 0.0, NEG_INF)
    sink = sink_ref[0]
    m = jnp.maximum(jnp.maximum(jnp.max(s0, axis=0, keepdims=True), jnp.max(s1, axis=0, keepdims=True)), sink)
    e0 = jnp.exp(s0 - m)
    e1 = jnp.exp(s1 - m)
    den = jnp.sum(e0, axis=0, keepdims=True) + jnp.sum(e1, axis=0, keepdims=True) + jnp.exp(sink - m)
    inv = 1.0 / den
    o_t = _dot(v_ref[0, i], (e0 * inv).astype(BF16)) + _dot(v_ref[0, jp], (e1 * inv).astype(BF16))
    o_ref[0] = _pairs_to_rows(o_t).astype(o_ref.dtype)


def _swa_attention(colmat, rowmat, bias, sinks, B, S):
    nb = S // Q_BLOCK
    G, R = SWA_KV_HEADS, SWA_REP
    return pl.pallas_call(
        _swa_kernel,
        grid=(B, G, nb),
        in_specs=[
            pl.BlockSpec((R, 1, HEAD_DIM, LANES), lambda b, g, i: (COL_WQ // R + g, b * nb + i, 0, 0)),
            pl.BlockSpec((1, S, HEAD_DIM), lambda b, g, i: (ROW_WK + g, b, 0)),
            pl.BlockSpec((1, nb, HEAD_DIM, LANES), lambda b, g, i: (COL_WV + g, b, 0, 0)),
            pl.BlockSpec((1, 2, Q_BLOCK, R * Q_BLOCK), lambda b, g, i: (g, 0, 0, 0)),
            pl.BlockSpec((1, 1, R * Q_BLOCK), lambda b, g, i: (g, 0, 0)),
        ],
        out_specs=pl.BlockSpec((1, Q_BLOCK, R * HEAD_DIM), lambda b, g, i: (b, i, g)),
        out_shape=jax.ShapeDtypeStruct((B, S, SWA_HEADS * HEAD_DIM), BF16),
        compiler_params=_cparams(("parallel", "parallel", "arbitrary")),
        name="swa_attention",
    )(colmat, rowmat, colmat, bias, sinks)


MERGE_TM = 512


def _merge_kernel(x_ref, nw_ref, wg_ref, ya_ref, yb_ref, yc_ref, ua_ref, ub_ref, uc_ref, wo_ref, o_ref):
    x = x_ref[...]
    h = _rms_rows(x, nw_ref[...]).astype(BF16)
    merged = None
    for br, (y_ref, u_ref) in enumerate(((ya_ref, ua_ref), (yb_ref, ub_ref), (yc_ref, uc_ref))):
        g = jax.nn.sigmoid(_dot(h, wg_ref[:, br * D_MODEL:(br + 1) * D_MODEL]))
        t = g * _dot(y_ref[...], u_ref[...])
        merged = t if merged is None else merged + t
    o_ref[...] = x + _dot(merged.astype(BF16), wo_ref[...])


def _merge(x2, nw, w_bgate, y_nsa, y_sb, y_swa, u_nsa, u_sb, u_swa, w_out):
    T = x2.shape[0]
    full = lambda shape: pl.BlockSpec(shape, lambda i: (0, 0))
    rows = lambda n: pl.BlockSpec((MERGE_TM, n), lambda i: (i, 0))
    return pl.pallas_call(
        _merge_kernel,
        grid=(T // MERGE_TM,),
        in_specs=[rows(D_MODEL), full((1, D_MODEL)), full((D_MODEL, N_BRANCHES * D_MODEL)),
                  rows(y_nsa.shape[1]), rows(y_sb.shape[1]), rows(y_swa.shape[1]),
                  full(u_nsa.shape), full(u_sb.shape), full(u_swa.shape), full((D_MODEL, D_MODEL))],
        out_specs=rows(D_MODEL),
        out_shape=jax.ShapeDtypeStruct((T, D_MODEL), F32),
        compiler_params=_cparams(("parallel",)),
        name="merge",
    )(x2, nw.reshape(1, D_MODEL), w_bgate.astype(BF16), y_nsa, y_sb, y_swa,
      u_nsa.astype(BF16), u_sb.astype(BF16), u_swa.astype(BF16), w_out.astype(BF16))


def _t5_bucket(dist):
    max_exact = NUM_BUCKETS // 2
    d = jnp.maximum(dist, 0)
    df = jnp.maximum(d, 1).astype(F32)
    large = max_exact + (jnp.log(df / max_exact) / math.log(MAX_DISTANCE / max_exact)
                         * (NUM_BUCKETS - max_exact)).astype(jnp.int32)
    large = jnp.minimum(large, NUM_BUCKETS - 1)
    return jnp.where(d < max_exact, d, large)


def _toeplitz(ext, delta):
    H = ext.shape[0]
    n = Q_BLOCK
    c = delta + MAX_DISTANCE
    w = ext[:, c - (n - 1):c + n]
    flat = jnp.pad(jnp.broadcast_to(w[:, None, :], (H, n, 2 * n - 1)), ((0, 0), (0, 0), (0, 1))).reshape(H, 2 * n * n)
    return flat[:, n - 1:n - 1 + n * (2 * n - 1)].reshape(H, n, 2 * n - 1)[:, :, :n]


def _lanes(t, G, R):
    K, Q = t.shape[1:]
    return t.reshape(G, R, K, Q).transpose(0, 2, 1, 3).reshape(G, K, R * Q)


def _bias_tables(rel_bias, nb, qk_bound):
    n = Q_BLOCK
    G, R = NSA_KV_HEADS, NSA_REP
    by_dist = rel_bias[_t5_bucket(jnp.arange(MAX_DISTANCE)), :].T.astype(F32)
    far = rel_bias[NUM_BUCKETS - 1, :].astype(F32)
    neg = jnp.full((1, n), NEG_INF, F32)
    k = jnp.arange(n)[:, None]
    q = jnp.arange(n)[None, :]
    above = (k > q)[None]

    rel = (by_dist[:NSA_HEADS] - far[:NSA_HEADS, None]) * LOG2E
    m_g = qk_bound + jnp.maximum(jnp.max(rel.reshape(G, -1), axis=1), 0.0)
    m_h = jnp.repeat(m_g, R)[:, None, None]
    ext = jnp.concatenate([jnp.broadcast_to(neg, (NSA_HEADS, n)), rel, jnp.zeros((NSA_HEADS, n), F32)], axis=1)
    t0, t1 = _toeplitz(ext, 0), _toeplitz(ext, n)
    zeros = jnp.zeros_like(t0)
    masked = jnp.full_like(t0, NEG_INF)
    sbias = _lanes(jnp.concatenate([zeros, t1, t0, masked], axis=1), G, R)
    n_win = NSA_WINDOW // n
    edge = jnp.where(above, zeros, NEG_INF)
    tiles = [edge] + [zeros] * (n_win - 2) + [t1, t0]
    wbias = _lanes(jnp.concatenate([t - m_h for t in tiles] + [masked] * n_win, axis=1), G, R)
    off = 8 * (nb - 1)
    near = jnp.stack([ext[:, 97 - CMP_STRIDE * m:97 - CMP_STRIDE * m + n] for m in range(-9, 7)], axis=1)
    n_rows = 16 * nb - 8
    cb = jnp.concatenate([jnp.zeros((NSA_HEADS, off - 9, n), F32), near,
                          jnp.full((NSA_HEADS, n_rows - off - 7, n), NEG_INF, F32)], axis=1) - m_h
    cbias = _lanes(cb, G, R)
    msel = jnp.broadcast_to(-m_g[:, None, None], (G, 1, n))

    hs = slice(NSA_HEADS, NSA_HEADS + SWA_HEADS)
    ext_w = jnp.concatenate([jnp.broadcast_to(neg, (SWA_HEADS, n)), by_dist[hs],
                             jnp.broadcast_to(far[hs, None], (SWA_HEADS, n))], axis=1)
    s0 = _toeplitz(ext_w, 0)
    s1 = jnp.where(above, _toeplitz(ext_w, n), NEG_INF)
    swa_bias = jnp.stack([_lanes(s0, SWA_KV_HEADS, SWA_REP), _lanes(s1, SWA_KV_HEADS, SWA_REP)], axis=1)
    return cbias, sbias, wbias, msel, swa_bias


def _overlap_t(S):
    nc, n_slc = S // CMP_STRIDE, S // SLC_LEN
    n = np.arange(nc)[None, :]
    j = np.arange(n_slc)[:, None]
    ov = (n * CMP_STRIDE < (j + 1) * SLC_LEN) & (n * CMP_STRIDE + CMP_LEN - 1 >= j * SLC_LEN) & (n < nc - 1)
    return jnp.asarray(ov, BF16)


def _head_cols(w, heads):
    return jnp.concatenate([w[:, off:off + HEAD_DIM] for _, off in heads], axis=1)


def _head_gains(heads, norm_of, scale_of):
    ones = jnp.ones((HEAD_DIM,), F32)
    gains = [norm_of[name].astype(F32) * scale_of.get(name, 1.0) if name in norm_of else ones for name, _ in heads]
    return jnp.concatenate(gains).reshape(1, -1)


def kernel(x, rel_bias, ffn1_norm, ffn1_w_gate, ffn1_w_up, ffn1_w_down, mix_norm, w_in, nsa_q_norm, nsa_k_norm, nsa_cmp_pos, nsa_cmp_k_w1, nsa_cmp_k_w2, nsa_cmp_v_w1, nsa_cmp_v_w2, swa_q_norm, swa_k_norm, swa_sinks, w_up_nsa, w_up_sb, w_up_swa, w_out, ffn2_norm, ffn2_w_gate, ffn2_w_up, ffn2_w_down):
    B, S, D = x.shape
    T = B * S
    nb = S // Q_BLOCK
    nc = S // CMP_STRIDE
    depth = w_in.shape[0]
    scale = HEAD_DIM ** -0.5
    scale_of = {"nq": scale * LOG2E, "sq": scale, "wq": scale}
    qk_bound = (BOUND_SLACK * LOG2E * HEAD_DIM ** 0.5
                * jnp.max(jnp.abs(nsa_q_norm.astype(F32))) * jnp.max(jnp.abs(nsa_k_norm.astype(F32))))
    cbias, sbias, wbias, msel, swa_bias = _bias_tables(rel_bias, nb, qk_bound)
    plan = _proj_plan(scale_of)
    ov_t = _overlap_t(S)
    tri = jnp.asarray(np.tril(np.ones((SB_KT, SB_KT), np.float32), -1), BF16)

    x2 = x.reshape(T, D)
    for l in range(depth):
        x2 = _ffn(x2, ffn1_norm[l], ffn1_w_gate[l], ffn1_w_up[l], ffn1_w_down[l])

        norm_of = {"nq": nsa_q_norm[l], "nks": nsa_k_norm[l], "nkw": nsa_k_norm[l],
                   "wq": swa_q_norm[l], "wk": swa_k_norm[l]}
        gain = jnp.concatenate([_head_gains(ROW_HEADS, norm_of, scale_of), _head_gains(COL_HEADS, norm_of, scale_of),
                                jnp.ones((1, LANES), F32)], axis=1)
        w_gate = jnp.pad(w_in[l][:, OFF_NGATE:OFF_NGATE + N_GATE], ((0, 0), (0, LANES - N_GATE)))
        w_all = jnp.concatenate([_head_cols(w_in[l], ROW_HEADS), _head_cols(w_in[l], COL_HEADS), w_gate], axis=1)
        rowmat, colmat, gates = _proj(x2, mix_norm[l], w_all.astype(BF16), gain, plan)
        gates = gates[:N_GATE].reshape(N_BRANCHES, NSA_KV_HEADS, NSA_REP, T)

        rows4 = rowmat[ROW_KC:ROW_KC + 4].reshape(4, B, nc, CMP_STRIDE * HEAD_DIM)
        cmp_rows, cmp_cols = _compress(rows4, jnp.stack([nsa_cmp_k_w1[l], nsa_cmp_v_w1[l]]),
                                       jnp.stack([nsa_cmp_k_w2[l], nsa_cmp_v_w2[l]]),
                                       nsa_cmp_pos[l], nsa_k_norm[l])

        y_nsa = _nsa_attention(colmat, rowmat, cmp_rows, cmp_cols, gates, cbias, sbias, wbias, ov_t, msel, B, S)
        y_sb = _sb_attention(colmat, rowmat, tri, B, S)
        sinks = jnp.repeat(swa_sinks[l].astype(F32).reshape(SWA_KV_HEADS, 1, SWA_REP), Q_BLOCK, axis=2)
        y_swa = _swa_attention(colmat, rowmat, swa_bias, sinks, B, S)

        x2 = _merge(x2, mix_norm[l], w_in[l][:, OFF_BGATE:], y_nsa.reshape(T, -1), y_sb.reshape(T, -1),
                    y_swa.reshape(T, -1), w_up_nsa[l], w_up_sb[l], w_up_swa[l], w_out[l])
        x2 = _ffn(x2, ffn2_norm[l], ffn2_w_gate[l], ffn2_w_up[l], ffn2_w_down[l])
    return x2.reshape(B, S, D)
```

```python
import functools
import math

import numpy as np
import jax
import jax.numpy as jnp
from jax import lax
from jax.experimental import pallas as pl
from jax.experimental.pallas import tpu as pltpu

D_MODEL = 1024
HEAD_DIM = 64
Q_BLOCK = 128
NSA_HEADS = 8
NSA_KV_HEADS = 2
NSA_REP = NSA_HEADS // NSA_KV_HEADS
CMP_LEN = 32
CMP_STRIDE = 16
CMP_HIDDEN = 256
SLC_LEN = 64
SLC_TOPN = 16
NSA_WINDOW = 512
FORCE_SCORE = 1e4
SB_HEADS = 4
SWA_HEADS = 4
SWA_KV_HEADS = 2
SWA_REP = SWA_HEADS // SWA_KV_HEADS
SWA_WINDOW = 128
NUM_BUCKETS = 32
MAX_DISTANCE = 128
D_FF = 2816
NORM_EPS = 1e-6
NEG_INF = -1e30
N_BRANCHES = 3
N_GATE = 3 * NSA_HEADS

LANES = 128
VMEM_LIMIT = 56 * 1024 * 1024
BF16 = jnp.bfloat16
F32 = jnp.float32

_sizes = (NSA_HEADS * HEAD_DIM,) + (NSA_KV_HEADS * HEAD_DIM,) * 6 + (N_GATE,) \
    + (SB_HEADS * HEAD_DIM,) * 3 + (SWA_HEADS * HEAD_DIM, SWA_KV_HEADS * HEAD_DIM,
                                    SWA_KV_HEADS * HEAD_DIM, N_BRANCHES * D_MODEL)
_offs = np.concatenate([[0], np.cumsum(_sizes)])
(OFF_NQ, OFF_NKC, OFF_NVC, OFF_NKS, OFF_NVS, OFF_NKW, OFF_NVW, OFF_NGATE,
 OFF_SQ, OFF_SK, OFF_SV, OFF_WQ, OFF_WK, OFF_WV, OFF_BGATE) = (int(o) for o in _offs[:-1])

ROW_HEADS = ([("nkc", OFF_NKC + 64 * g) for g in range(2)] + [("nvc", OFF_NVC + 64 * g) for g in range(2)]
             + [("nks", OFF_NKS + 64 * g) for g in range(2)] + [("nkw", OFF_NKW + 64 * g) for g in range(2)]
             + [("sq", OFF_SQ + 64 * h) for h in range(4)] + [("sv", OFF_SV + 64 * h) for h in range(4)]
             + [("wk", OFF_WK + 64 * g) for g in range(2)])
ROW_KC, ROW_VC, ROW_KS, ROW_KW, ROW_SQ, ROW_SV, ROW_WK = 0, 2, 4, 6, 8, 12, 16
COL_HEADS = ([("nq", OFF_NQ + 64 * h) for h in range(8)] + [("nvs", OFF_NVS + 64 * g) for g in range(2)]
             + [("nvw", OFF_NVW + 64 * g) for g in range(2)] + [("sk", OFF_SK + 64 * h) for h in range(4)]
             + [("wq", OFF_WQ + 64 * h) for h in range(4)] + [("wv", OFF_WV + 64 * g) for g in range(2)])
COL_NQ, COL_VS, COL_VW, COL_SK, COL_WQ, COL_WV = 0, 8, 10, 12, 16, 20


def _dot(a, b):
    return jnp.dot(a, b, preferred_element_type=F32)


def _rms_rows(x, w):
    ms = jnp.mean(x * x, axis=-1, keepdims=True)
    return x * lax.rsqrt(ms + NORM_EPS) * w


def _cparams(sem, flags=None):
    return pltpu.CompilerParams(dimension_semantics=sem, vmem_limit_bytes=VMEM_LIMIT, flags=flags)


FFN_TM = 512
FFN_TF = 1408


def _ffn_kernel(x_ref, nw_ref, wg_ref, wu_ref, wd_ref, o_ref, h_ref, acc_ref):
    f = pl.program_id(1)

    @pl.when(f == 0)
    def _():
        h_ref[...] = _rms_rows(x_ref[...], nw_ref[...]).astype(BF16)
        acc_ref[...] = jnp.zeros_like(acc_ref)

    h = h_ref[...]
    g = _dot(h, wg_ref[...])
    u = _dot(h, wu_ref[...])
    a = (g * jax.nn.sigmoid(g)) * u
    acc_ref[...] += _dot(a.astype(BF16), wd_ref[...])

    @pl.when(f == pl.num_programs(1) - 1)
    def _():
        o_ref[...] = x_ref[...] + 0.5 * acc_ref[...]


def _ffn(x2, nw, wg, wu, wd):
    T = x2.shape[0]
    grid = (T // FFN_TM, D_FF // FFN_TF)
    return pl.pallas_call(
        _ffn_kernel,
        grid=grid,
        in_specs=[
            pl.BlockSpec((FFN_TM, D_MODEL), lambda i, f: (i, 0)),
            pl.BlockSpec((1, D_MODEL), lambda i, f: (0, 0)),
            pl.BlockSpec((D_MODEL, FFN_TF), lambda i, f: (0, f)),
            pl.BlockSpec((D_MODEL, FFN_TF), lambda i, f: (0, f)),
            pl.BlockSpec((FFN_TF, D_MODEL), lambda i, f: (f, 0)),
        ],
        out_specs=pl.BlockSpec((FFN_TM, D_MODEL), lambda i, f: (i, 0)),
        out_shape=jax.ShapeDtypeStruct((T, D_MODEL), F32),
        scratch_shapes=[pltpu.VMEM((FFN_TM, D_MODEL), BF16), pltpu.VMEM((FFN_TM, D_MODEL), F32)],
        compiler_params=_cparams(("parallel", "arbitrary")),
        name="ffn",
    )(x2, nw.reshape(1, D_MODEL), wg.astype(BF16), wu.astype(BF16), wd.astype(BF16))


PROJ_TM = 512


GATE_ROWS = 32

_NORMED = ("nq", "nks", "nkw", "wq", "wk")


def _proj_plan(scale_of):
    plan = []
    for kind, heads in (("row", ROW_HEADS), ("col", COL_HEADS)):
        for p in range(len(heads) // 2):
            name = heads[2 * p][0]
            assert heads[2 * p + 1][0] == name
            plan.append((kind, p, name in _NORMED, 1.0 if name in _NORMED else scale_of.get(name, 1.0)))
    plan.append(("gate", 0, False, 1.0))
    return tuple(plan)


def _proj_kernel(x_ref, nw_ref, w_ref, gain_ref, bd_ref, row_ref, col_ref, gate_ref, *, plan):
    h = _rms_rows(x_ref[...], nw_ref[...]).astype(BF16)
    pair = 2 * LANES
    for c0 in range(0, len(plan) * LANES, pair):
        width = min(pair, len(plan) * LANES - c0)
        zg = _dot(h, w_ref[:, c0:c0 + width])
        for u in range(width // LANES):
            t = c0 // LANES + u
            kind, p, normed, scale = plan[t]
            z = zg[:, u * LANES:(u + 1) * LANES]
            if normed:
                ms = _dot((z * z).astype(BF16), bd_ref[...])
                z = z * lax.rsqrt(ms + NORM_EPS) * gain_ref[:, t * LANES:(t + 1) * LANES]
            elif scale != 1.0:
                z = z * scale
            if kind == "row":
                row_ref[2 * p] = z[:, :HEAD_DIM].astype(row_ref.dtype)
                row_ref[2 * p + 1] = z[:, HEAD_DIM:].astype(row_ref.dtype)
            elif kind == "col":
                z_t = z.T
                for c in range(PROJ_TM // LANES):
                    col_ref[2 * p, c] = z_t[:HEAD_DIM, c * LANES:(c + 1) * LANES].astype(col_ref.dtype)
                    col_ref[2 * p + 1, c] = z_t[HEAD_DIM:, c * LANES:(c + 1) * LANES].astype(col_ref.dtype)
            else:
                gate_ref[...] = jax.nn.sigmoid(z).T[:GATE_ROWS]


def _proj(x2, nw, w, gain, plan):
    T = x2.shape[0]
    n_row, n_col = len(ROW_HEADS), len(COL_HEADS)
    bd = np.kron(np.eye(2, dtype=np.float32), np.full((HEAD_DIM, HEAD_DIM), 1.0 / HEAD_DIM, np.float32))
    return pl.pallas_call(
        functools.partial(_proj_kernel, plan=plan),
        grid=(T // PROJ_TM,),
        in_specs=[
            pl.BlockSpec((PROJ_TM, D_MODEL), lambda i: (i, 0)),
            pl.BlockSpec((1, D_MODEL), lambda i: (0, 0)),
            pl.BlockSpec(w.shape, lambda i: (0, 0)),
            pl.BlockSpec(gain.shape, lambda i: (0, 0)),
            pl.BlockSpec((LANES, LANES), lambda i: (0, 0)),
        ],
        out_specs=[
            pl.BlockSpec((n_row, PROJ_TM, HEAD_DIM), lambda i: (0, i, 0)),
            pl.BlockSpec((n_col, PROJ_TM // LANES, HEAD_DIM, LANES), lambda i: (0, i, 0, 0)),
            pl.BlockSpec((GATE_ROWS, PROJ_TM), lambda i: (0, i)),
        ],
        out_shape=[jax.ShapeDtypeStruct((n_row, T, HEAD_DIM), BF16),
                   jax.ShapeDtypeStruct((n_col, T // LANES, HEAD_DIM, LANES), BF16),
                   jax.ShapeDtypeStruct((GATE_ROWS, T), F32)],
        compiler_params=_cparams(("parallel",)),
        name="proj",
    )(x2, nw.reshape(1, D_MODEL), w, gain, jnp.asarray(bd, BF16))


def _compress_kernel(r_ref, w1_ref, w2_ref, pos_ref, kn_ref, row_ref, col_ref):
    kind = pl.program_id(0) // NSA_KV_HEADS
    r = r_ref[0, 0]
    nc = r.shape[0]
    half = CMP_STRIDE * HEAD_DIM
    w1 = w1_ref[0]
    p_lo = _dot(r, w1[:half])
    p_hi = _dot(r, w1[half:])
    p_pos = _dot(jnp.broadcast_to(pos_ref[...], (8, CMP_LEN * HEAD_DIM)).astype(BF16), w1)[:1]
    hdn = p_lo + pltpu.roll(p_hi, nc - 1, 0) + p_pos
    hdn = hdn * jax.nn.sigmoid(hdn)
    out = _dot(hdn.astype(BF16), w2_ref[0])
    ms = jnp.sum(out * out, axis=-1, keepdims=True) * (1.0 / HEAD_DIM)
    normed = out * lax.rsqrt(ms + NORM_EPS) * kn_ref[...]
    out = jnp.where(kind == 0, normed, out)
    row_ref[0, 0] = out[:, :HEAD_DIM].astype(row_ref.dtype)
    col_ref[0, 0] = out.T[:HEAD_DIM].astype(col_ref.dtype)


def _compress(rows4, w1, w2, pos, k_norm):
    assert ROW_KC == 0 and ROW_VC == NSA_KV_HEADS
    _, B, NC, _ = rows4.shape
    kn = jnp.pad(k_norm.reshape(1, HEAD_DIM), ((0, 0), (0, LANES - HEAD_DIM)))
    w2p = jnp.pad(w2, ((0, 0), (0, 0), (0, LANES - HEAD_DIM))).astype(BF16)
    return pl.pallas_call(
        _compress_kernel,
        grid=(4, B),
        in_specs=[
            pl.BlockSpec((1, 1, NC, CMP_STRIDE * HEAD_DIM), lambda h, b: (h, b, 0, 0)),
            pl.BlockSpec((1, CMP_LEN * HEAD_DIM, CMP_HIDDEN), lambda h, b: (h // NSA_KV_HEADS, 0, 0)),
            pl.BlockSpec((1, CMP_HIDDEN, LANES), lambda h, b: (h // NSA_KV_HEADS, 0, 0)),
            pl.BlockSpec((1, CMP_LEN * HEAD_DIM), lambda h, b: (0, 0)),
            pl.BlockSpec((1, LANES), lambda h, b: (0, 0)),
        ],
        out_specs=[
            pl.BlockSpec((1, 1, NC, HEAD_DIM), lambda h, b: (h, b, 0, 0)),
            pl.BlockSpec((1, 1, HEAD_DIM, NC), lambda h, b: (h, b, 0, 0)),
        ],
        out_shape=[jax.ShapeDtypeStruct((4, B, NC, HEAD_DIM), BF16),
                   jax.ShapeDtypeStruct((4, B, HEAD_DIM, NC), BF16)],
        compiler_params=_cparams(("parallel", "parallel")),
        name="nsa_compress",
    )(rows4, w1.astype(BF16), w2p, pos.reshape(1, CMP_LEN * HEAD_DIM), kn)


def _softmax_tile_update(s, v_t, m, l, acc):
    m_new = jnp.maximum(m, jnp.max(s, axis=0, keepdims=True))
    alpha = jnp.exp2(m - m_new)
    p = jnp.exp2(s - m_new)
    l_new = alpha * l + jnp.sum(p, axis=0, keepdims=True)
    acc_new = alpha * acc + _dot(v_t, p.astype(BF16))
    return m_new, l_new, acc_new


def _pairs_to_rows(o_t):
    n = o_t.shape[1] // LANES
    outs = []
    for p in range(n // 2):
        a = o_t[:, (2 * p) * LANES:(2 * p + 1) * LANES]
        b = o_t[:, (2 * p + 1) * LANES:(2 * p + 2) * LANES]
        outs.append(jnp.concatenate([a, b], axis=0).T)
    return outs[0] if len(outs) == 1 else jnp.concatenate(outs, axis=1)


FAR_TILES = 4
LOG2E = math.log2(math.e)
NSA_TINY = 2.0 ** -100
BOUND_SLACK = 1.05


def _nsa_kernel(q_ref, kc_ref, vc_ref, ks_ref, vs_ref, kw_ref, vw_ref, gate_ref,
                cbias_ref, sbias_ref, wbias_ref, ov_ref, msel_ref, o_ref, neg_ref, *stage_refs, nb):
    i = pl.program_id(2)
    R = NSA_REP
    N = R * Q_BLOCK
    nc = kc_ref.shape[2]
    q_t = jnp.concatenate([q_ref[r, 0] for r in range(R)], axis=1)
    n_win = NSA_WINDOW // Q_BLOCK

    def k_rows(ref, J0, n_tiles):
        return ref[0, pl.ds(pl.multiple_of(J0 * Q_BLOCK, Q_BLOCK), n_tiles * Q_BLOCK), :]

    def v_cols(ref, J0, n_tiles):
        return jnp.concatenate([ref[0, J0 + u] for u in range(n_tiles)], axis=1)

    def sel_mask(J0, n_tiles):
        parts = [jnp.broadcast_to(neg_ref[pl.ds(2 * J0 + u, 1), :], (SLC_LEN, Q_BLOCK)) for u in range(2 * n_tiles)]
        return jnp.concatenate([jnp.concatenate(parts, axis=0)] * R, axis=1)

    def compressed(fast):
        cb = cbias_ref[0, pl.ds(pl.multiple_of(8 * (nb - 1) - 8 * i, 8), nc), :]
        s = _dot(kc_ref[0, 0], q_t) + cb
        if fast:
            e = jnp.exp2(s)
        else:
            e = jnp.where(s > 0.5 * NEG_INF, jnp.exp2(s - jnp.max(s, axis=0, keepdims=True)), 0.0)
        l = jnp.sum(e, axis=0, keepdims=True)
        p = e * jnp.where(l > 0.0, 1.0 / l, 0.0)
        return p, l, _dot(vc_ref[0, 0], p.astype(BF16))

    def select_blocks(p_c):
        p_sum = p_c[:, 0:Q_BLOCK]
        for r in range(1, R):
            p_sum = p_sum + p_c[:, r * Q_BLOCK:(r + 1) * Q_BLOCK]
        p_hi = p_sum.astype(BF16)
        p_lo = (p_sum - p_hi.astype(F32)).astype(BF16)
        imp = _dot(ov_ref[...], p_hi) + _dot(ov_ref[...], p_lo)
        n_slc = imp.shape[0]
        jj = lax.broadcasted_iota(jnp.int32, imp.shape, 0)
        qq = lax.broadcasted_iota(jnp.int32, imp.shape, 1)
        cur = 2 * i + jnp.where(qq >= SLC_LEN, 1, 0)
        forced = (jj == 0) | (jj == cur) | (jj == cur - 1)
        score = jnp.where(forced, FORCE_SCORE, jnp.where(jj <= cur, imp, -1.0))
        sel = jnp.zeros(imp.shape, F32)
        for _ in range(min(SLC_TOPN, n_slc)):
            best = jnp.max(score, axis=0, keepdims=True)
            first = jnp.min(jnp.where(score == best, jj, n_slc), axis=0, keepdims=True)
            hit = jj == first
            sel = jnp.where(hit, 1.0, sel)
            score = jnp.where(hit, -2.0, score)
        neg_ref[...] = jnp.where(sel > 0.0, msel_ref[0], NEG_INF)

    def add_chunk(st, s, v_t):
        e = jnp.exp2(s)
        return st[0] + jnp.sum(e, axis=0, keepdims=True), st[1] + _dot(v_t, e.astype(BF16))

    zero = (jnp.zeros((1, N), F32), jnp.zeros((HEAD_DIM, N), F32))

    def selected_fast():
        J0 = jnp.maximum(i - 1, 0)
        bias = sbias_ref[0, pl.ds(pl.multiple_of((2 - (i - J0)) * Q_BLOCK, Q_BLOCK), 2 * Q_BLOCK), :]
        st = add_chunk(zero, _dot(k_rows(ks_ref, J0, 2), q_t) + bias + sel_mask(J0, 2), v_cols(vs_ref, J0, 2))
        n_far = jnp.maximum(i - 1, 0)

        def far_chunk(J0, st, n_valid):
            mask = sel_mask(J0, FAR_TILES)
            if n_valid is not None:
                row = lax.broadcasted_iota(jnp.int32, mask.shape, 0)
                mask = jnp.where(row < n_valid * Q_BLOCK, mask, NEG_INF)
            return add_chunk(st, _dot(k_rows(ks_ref, J0, FAR_TILES), q_t) + mask, v_cols(vs_ref, J0, FAR_TILES))

        n_chunks = n_far // FAR_TILES
        rem = n_far - FAR_TILES * n_chunks

        def tile_of(c):
            return jnp.clip(n_far - FAR_TILES * (c + 1), 0, nb - FAR_TILES)

        def scores(c):
            return _dot(k_rows(ks_ref, tile_of(c), FAR_TILES), q_t)

        def values(c, e_ref, acc):
            return acc + _dot(v_cols(vs_ref, tile_of(c), FAR_TILES), e_ref[...])

        def stage(c, st, s_in, s_out, e_in, e_out):
            s_out[...] = scores(c + 1)
            acc = values(c - 1, e_in, st[1])
            e = jnp.exp2(s_in[...] + sel_mask(tile_of(c), FAR_TILES))
            e_out[...] = e.astype(BF16)
            return st[0] + jnp.sum(e, axis=0, keepdims=True), acc

        s_a, s_b, e_a, e_b = stage_refs
        e_b[...] = jnp.zeros_like(e_b)
        s_a[...] = scores(0)

        def pair(c, st):
            st = stage(c, st, s_a, s_b, e_b, e_a)
            return stage(c + 1, st, s_b, s_a, e_a, e_b)

        n_quads = n_chunks // 4
        st = lax.fori_loop(0, n_quads, lambda t, st: pair(4 * t + 2, pair(4 * t, st)), st)
        st = lax.cond(n_chunks % 4 >= 2, lambda st: pair(4 * n_quads, st), lambda st: st, st)

        def odd_tail(st):
            st = stage(n_chunks - 1, st, s_a, s_b, e_b, e_a)
            return st[0], values(n_chunks - 1, e_a, st[1])

        st = lax.cond(n_chunks % 2 == 1, odd_tail, lambda st: (st[0], values(n_chunks - 1, e_b, st[1])), st)
        return lax.cond(rem > 0, lambda st: far_chunk(0, st, rem), lambda st: st, st)

    def window_fast():
        J0 = jnp.maximum(i - n_win, 0)
        bias = wbias_ref[0, pl.ds(pl.multiple_of((n_win - (i - J0)) * Q_BLOCK, Q_BLOCK), (n_win + 1) * Q_BLOCK), :]
        return add_chunk(zero, _dot(k_rows(kw_ref, J0, n_win + 1), q_t) + bias, v_cols(vw_ref, J0, n_win + 1))

    def sweep_exact(k_ref, v_ref, bias_ref, far_dist, n_back, with_sel):
        def body(t, st):
            J = i - t
            off = pl.multiple_of((far_dist - jnp.minimum(t, far_dist)) * Q_BLOCK, Q_BLOCK)
            s = _dot(k_rows(k_ref, J, 1), q_t) + bias_ref[0, pl.ds(off, Q_BLOCK), :]
            if with_sel:
                s = s + sel_mask(J, 1)
            return _softmax_tile_update(s, v_ref[0, J], *st)

        init = (jnp.full((1, N), NEG_INF, F32),) + zero
        _, l, acc = lax.fori_loop(0, jnp.minimum(i, n_back) + 1, body, init)
        return l, acc

    p_c, l_c, o_c = compressed(True)
    select_blocks(p_c)
    l_w, a_w = window_fast()
    l_s, a_s = selected_fast()
    lane_q = lax.broadcasted_iota(jnp.int32, (1, N), 1) % Q_BLOCK
    has_cmp = (i > 0) | (lane_q >= CMP_LEN - 1)
    l_min = jnp.minimum(jnp.minimum(l_s, l_w), jnp.where(has_cmp, l_c, 1.0))

    def exact_path():
        p_x, _, o_x = compressed(False)
        select_blocks(p_x)
        l_sx, a_sx = sweep_exact(ks_ref, vs_ref, sbias_ref, 2, nb, True)
        l_wx, a_wx = sweep_exact(kw_ref, vw_ref, wbias_ref, n_win, n_win, False)
        return o_x, a_sx * (1.0 / l_sx), a_wx * (1.0 / l_wx)

    o_c, o_s, o_w = lax.cond(jnp.min(l_min) > NSA_TINY,
                             lambda: (o_c, a_s * (1.0 / l_s), a_w * (1.0 / l_w)), exact_path)

    def gate_row(br):
        return jnp.concatenate([gate_ref[br, 0, r:r + 1, :] for r in range(R)], axis=1)

    o_t = gate_row(0) * o_c + gate_row(1) * o_s + gate_row(2) * o_w
    o_ref[0] = _pairs_to_rows(o_t).astype(o_ref.dtype)


def _nsa_attention(colmat, rowmat, cmp_rows, cmp_cols, gates, cbias, sbias, wbias, ov_t, msel, B, S):
    nb = S // Q_BLOCK
    nc = S // CMP_STRIDE
    n_slc = S // SLC_LEN
    G, R = NSA_KV_HEADS, NSA_REP
    kern = functools.partial(_nsa_kernel, nb=nb)
    return pl.pallas_call(
        kern,
        grid=(B, G, nb),
        in_specs=[
            pl.BlockSpec((R, 1, HEAD_DIM, LANES), lambda b, g, i: (g, b * nb + i, 0, 0)),
            pl.BlockSpec((1, 1, nc, HEAD_DIM), lambda b, g, i: (g, b, 0, 0)),
            pl.BlockSpec((1, 1, HEAD_DIM, nc), lambda b, g, i: (NSA_KV_HEADS + g, b, 0, 0)),
            pl.BlockSpec((1, S, HEAD_DIM), lambda b, g, i: (ROW_KS + g, b, 0)),
            pl.BlockSpec((1, nb, HEAD_DIM, LANES), lambda b, g, i: (COL_VS + g, b, 0, 0)),
            pl.BlockSpec((1, S, HEAD_DIM), lambda b, g, i: (ROW_KW + g, b, 0)),
            pl.BlockSpec((1, nb, HEAD_DIM, LANES), lambda b, g, i: (COL_VW + g, b, 0, 0)),
            pl.BlockSpec((N_BRANCHES, 1, R, LANES), lambda b, g, i: (0, g, 0, b * nb + i)),
            pl.BlockSpec((1, cbias.shape[1], R * Q_BLOCK), lambda b, g, i: (g, 0, 0)),
            pl.BlockSpec((1, sbias.shape[1], R * Q_BLOCK), lambda b, g, i: (g, 0, 0)),
            pl.BlockSpec((1, wbias.shape[1], R * Q_BLOCK), lambda b, g, i: (g, 0, 0)),
            pl.BlockSpec((n_slc, nc), lambda b, g, i: (0, 0)),
            pl.BlockSpec((1, 1, Q_BLOCK), lambda b, g, i: (g, 0, 0)),
        ],
        out_specs=pl.BlockSpec((1, Q_BLOCK, R * HEAD_DIM), lambda b, g, i: (b, i, g)),
        out_shape=jax.ShapeDtypeStruct((B, S, NSA_HEADS * HEAD_DIM), BF16),
        scratch_shapes=[pltpu.VMEM((n_slc, Q_BLOCK), F32)]
        + [pltpu.VMEM((FAR_TILES * Q_BLOCK, R * Q_BLOCK), F32)] * 2
        + [pltpu.VMEM((FAR_TILES * Q_BLOCK, R * Q_BLOCK), BF16)] * 2,
        compiler_params=_cparams(("parallel", "parallel", "arbitrary")),
        name="nsa_attention",
    )(colmat, cmp_rows, cmp_cols, rowmat, colmat, rowmat, colmat, gates, cbias, sbias, wbias, ov_t, msel)


SB_QB = 512
SB_KT = 256
SB_ZERO_LOG = -110.0


def _sb_kernel(q_ref, k_ref, v_ref, tri_ref, o_ref):
    i = pl.program_id(1)
    H = SB_HEADS
    tri = tri_ref[...]
    rows = lax.broadcasted_iota(jnp.int32, (SB_QB, SB_KT), 0)
    cols = lax.broadcasted_iota(jnp.int32, (SB_QB, SB_KT), 1)
    key_minus_query = cols - rows
    per_q = SB_QB // SB_KT

    def step(J, st, masked):
        out = []
        for h in range(H):
            carry, acc = st[h]
            k_t = jnp.concatenate([k_ref[h, (SB_KT // LANES) * J + c] for c in range(SB_KT // LANES)], axis=1)
            z = _dot(q_ref[h], k_t)
            sp = jnp.maximum(z, 0.0) + jnp.log(1.0 + jnp.exp(-jnp.abs(z)))
            if masked:
                valid = key_minus_query < i * SB_QB - J * SB_KT
                lf = jnp.where(valid, -sp, 0.0)
            else:
                lf = -sp
            lf_hi = lf.astype(BF16)
            lf_lo = (lf - lf_hi.astype(F32)).astype(BF16)
            after = _dot(lf_hi, tri) + _dot(lf_lo, tri) + carry
            w = jnp.exp(z - sp + after)
            if masked:
                w = jnp.where(valid, w, 0.0)
            v = v_ref[h, pl.ds(pl.multiple_of(J * SB_KT, SB_KT), SB_KT), :]
            acc = acc + _dot(w.astype(BF16), v)
            carry = carry + jnp.sum(lf, axis=1, keepdims=True)
            out.append((carry, acc))
        return tuple(out)

    st = tuple((jnp.zeros((SB_QB, 1), F32), jnp.zeros((SB_QB, HEAD_DIM), F32)) for _ in range(H))
    for d in range(per_q):
        st = step(per_q * i + per_q - 1 - d, st, True)

    def alive(st):
        worst = st[0][0]
        for h in range(1, H):
            worst = jnp.maximum(worst, st[h][0])
        return jnp.max(worst) > SB_ZERO_LOG

    def body(c):
        J, _, st = c
        st = step(J, st, False)
        return J - 1, alive(st), st

    _, _, st = lax.while_loop(lambda c: (c[0] >= 0) & c[1], body, (per_q * i - 1, alive(st), st))
    o_ref[0] = jnp.concatenate([st[h][1] for h in range(H)], axis=1).astype(o_ref.dtype)


def _sb_attention(colmat, rowmat, tri, B, S):
    nq = S // SB_QB
    nb = S // LANES
    H = SB_HEADS
    return pl.pallas_call(
        _sb_kernel,
        grid=(B, nq),
        in_specs=[
            pl.BlockSpec((H, SB_QB, HEAD_DIM), lambda b, i: (ROW_SQ // H, b * nq + i, 0)),
            pl.BlockSpec((H, nb, HEAD_DIM, LANES), lambda b, i: (COL_SK // H, b, 0, 0)),
            pl.BlockSpec((H, S, HEAD_DIM), lambda b, i: (ROW_SV // H, b, 0)),
            pl.BlockSpec((SB_KT, SB_KT), lambda b, i: (0, 0)),
        ],
        out_specs=pl.BlockSpec((1, SB_QB, H * HEAD_DIM), lambda b, i: (b, i, 0)),
        out_shape=jax.ShapeDtypeStruct((B, S, H * HEAD_DIM), BF16),
        compiler_params=_cparams(("parallel", "arbitrary")),
        name="sb_attention",
    )(rowmat, colmat, rowmat, tri)


SWA_SUB = 4


def _swa_kernel(q_ref, k_ref, v_ref, bias_ref, sink_ref, o_ref):
    R = SWA_REP
    sink = sink_ref[0]
    for u in range(SWA_SUB):
        i = pl.program_id(2) * SWA_SUB + u
        q_t = jnp.concatenate([q_ref[r, u] for r in range(R)], axis=1)
        J0 = jnp.maximum(i - 1, 0)
        k = k_ref[0, pl.ds(pl.multiple_of(J0 * Q_BLOCK, Q_BLOCK), 2 * Q_BLOCK), :]
        bias = bias_ref[0, pl.ds(pl.multiple_of((1 - (i - J0)) * Q_BLOCK, Q_BLOCK), 2 * Q_BLOCK), :]
        s = _dot(k, q_t) + bias
        m = jnp.maximum(jnp.max(s, axis=0, keepdims=True), sink)
        e = jnp.exp(s - m)
        den = jnp.sum(e, axis=0, keepdims=True) + jnp.exp(sink - m)
        v_t = jnp.concatenate([v_ref[0, J0], v_ref[0, J0 + 1]], axis=1)
        o_t = _dot(v_t, e.astype(BF16)) * (1.0 / den)
        o_ref[0, u * Q_BLOCK:(u + 1) * Q_BLOCK, :] = _pairs_to_rows(o_t).astype(o_ref.dtype)


def _swa_attention(colmat, rowmat, bias, sinks, B, S):
    nb = S // Q_BLOCK
    ns = nb // SWA_SUB
    G, R = SWA_KV_HEADS, SWA_REP
    return pl.pallas_call(
        _swa_kernel,
        grid=(B, G, ns),
        in_specs=[
            pl.BlockSpec((R, SWA_SUB, HEAD_DIM, LANES), lambda b, g, i: (COL_WQ // R + g, b * ns + i, 0, 0)),
            pl.BlockSpec((1, S, HEAD_DIM), lambda b, g, i: (ROW_WK + g, b, 0)),
            pl.BlockSpec((1, nb, HEAD_DIM, LANES), lambda b, g, i: (COL_WV + g, b, 0, 0)),
            pl.BlockSpec((1, 3 * Q_BLOCK, R * Q_BLOCK), lambda b, g, i: (g, 0, 0)),
            pl.BlockSpec((1, 1, R * Q_BLOCK), lambda b, g, i: (g, 0, 0)),
        ],
        out_specs=pl.BlockSpec((1, SWA_SUB * Q_BLOCK, R * HEAD_DIM), lambda b, g, i: (b, i, g)),
        out_shape=jax.ShapeDtypeStruct((B, S, SWA_HEADS * HEAD_DIM), BF16),
        compiler_params=_cparams(("parallel", "parallel", "arbitrary")),
        name="swa_attention",
    )(colmat, rowmat, colmat, bias, sinks)


MERGE_TM = 512


def _merge_kernel(x_ref, nw_ref, wg_ref, ya_ref, yb_ref, yc_ref, ua_ref, ub_ref, uc_ref, wo_ref, o_ref):
    x = x_ref[...]
    h = _rms_rows(x, nw_ref[...]).astype(BF16)
    merged = None
    for br, (y_ref, u_ref) in enumerate(((ya_ref, ua_ref), (yb_ref, ub_ref), (yc_ref, uc_ref))):
        g = jax.nn.sigmoid(_dot(h, wg_ref[:, br * D_MODEL:(br + 1) * D_MODEL]))
        t = g * _dot(y_ref[...], u_ref[...])
        merged = t if merged is None else merged + t
    o_ref[...] = x + _dot(merged.astype(BF16), wo_ref[...])


def _merge(x2, nw, w_bgate, y_nsa, y_sb, y_swa, u_nsa, u_sb, u_swa, w_out):
    T = x2.shape[0]
    full = lambda shape: pl.BlockSpec(shape, lambda i: (0, 0))
    rows = lambda n: pl.BlockSpec((MERGE_TM, n), lambda i: (i, 0))
    return pl.pallas_call(
        _merge_kernel,
        grid=(T // MERGE_TM,),
        in_specs=[rows(D_MODEL), full((1, D_MODEL)), full((D_MODEL, N_BRANCHES * D_MODEL)),
                  rows(y_nsa.shape[1]), rows(y_sb.shape[1]), rows(y_swa.shape[1]),
                  full(u_nsa.shape), full(u_sb.shape), full(u_swa.shape), full((D_MODEL, D_MODEL))],
        out_specs=rows(D_MODEL),
        out_shape=jax.ShapeDtypeStruct((T, D_MODEL), F32),
        compiler_params=_cparams(("parallel",)),
        name="merge",
    )(x2, nw.reshape(1, D_MODEL), w_bgate.astype(BF16), y_nsa, y_sb, y_swa,
      u_nsa.astype(BF16), u_sb.astype(BF16), u_swa.astype(BF16), w_out.astype(BF16))


def _t5_bucket(dist):
    max_exact = NUM_BUCKETS // 2
    d = jnp.maximum(dist, 0)
    df = jnp.maximum(d, 1).astype(F32)
    large = max_exact + (jnp.log(df / max_exact) / math.log(MAX_DISTANCE / max_exact)
                         * (NUM_BUCKETS - max_exact)).astype(jnp.int32)
    large = jnp.minimum(large, NUM_BUCKETS - 1)
    return jnp.where(d < max_exact, d, large)


def _toeplitz(ext, delta):
    H = ext.shape[0]
    n = Q_BLOCK
    c = delta + MAX_DISTANCE
    w = ext[:, c - (n - 1):c + n]
    flat = jnp.pad(jnp.broadcast_to(w[:, None, :], (H, n, 2 * n - 1)), ((0, 0), (0, 0), (0, 1))).reshape(H, 2 * n * n)
    return flat[:, n - 1:n - 1 + n * (2 * n - 1)].reshape(H, n, 2 * n - 1)[:, :, :n]


def _lanes(t, G, R):
    K, Q = t.shape[1:]
    return t.reshape(G, R, K, Q).transpose(0, 2, 1, 3).reshape(G, K, R * Q)


def _bias_tables(rel_bias, nb, qk_bound):
    n = Q_BLOCK
    G, R = NSA_KV_HEADS, NSA_REP
    by_dist = rel_bias[_t5_bucket(jnp.arange(MAX_DISTANCE)), :].T.astype(F32)
    far = rel_bias[NUM_BUCKETS - 1, :].astype(F32)
    neg = jnp.full((1, n), NEG_INF, F32)
    k = jnp.arange(n)[:, None]
    q = jnp.arange(n)[None, :]
    above = (k > q)[None]

    rel = (by_dist[:NSA_HEADS] - far[:NSA_HEADS, None]) * LOG2E
    m_g = qk_bound + jnp.maximum(jnp.max(rel.reshape(G, -1), axis=1), 0.0)
    m_h = jnp.repeat(m_g, R)[:, None, None]
    ext = jnp.concatenate([jnp.broadcast_to(neg, (NSA_HEADS, n)), rel, jnp.zeros((NSA_HEADS, n), F32)], axis=1)
    t0, t1 = _toeplitz(ext, 0), _toeplitz(ext, n)
    zeros = jnp.zeros_like(t0)
    masked = jnp.full_like(t0, NEG_INF)
    sbias = _lanes(jnp.concatenate([zeros, t1, t0, masked], axis=1), G, R)
    n_win = NSA_WINDOW // n
    edge = jnp.where(above, zeros, NEG_INF)
    tiles = [edge] + [zeros] * (n_win - 2) + [t1, t0]
    wbias = _lanes(jnp.concatenate([t - m_h for t in tiles] + [masked] * n_win, axis=1), G, R)
    off = 8 * (nb - 1)
    near = jnp.stack([ext[:, 97 - CMP_STRIDE * m:97 - CMP_STRIDE * m + n] for m in range(-9, 7)], axis=1)
    n_rows = 16 * nb - 8
    cb = jnp.concatenate([jnp.zeros((NSA_HEADS, off - 9, n), F32), near,
                          jnp.full((NSA_HEADS, n_rows - off - 7, n), NEG_INF, F32)], axis=1) - m_h
    cbias = _lanes(cb, G, R)
    msel = jnp.broadcast_to(-m_g[:, None, None], (G, 1, n))

    hs = slice(NSA_HEADS, NSA_HEADS + SWA_HEADS)
    ext_w = jnp.concatenate([jnp.broadcast_to(neg, (SWA_HEADS, n)), by_dist[hs],
                             jnp.broadcast_to(far[hs, None], (SWA_HEADS, n))], axis=1)
    s0 = _toeplitz(ext_w, 0)
    s1 = jnp.where(above, _toeplitz(ext_w, n), NEG_INF)
    swa_bias = _lanes(jnp.concatenate([s1, s0, jnp.full_like(s0, NEG_INF)], axis=1), SWA_KV_HEADS, SWA_REP)
    return cbias, sbias, wbias, msel, swa_bias


def _overlap_t(S):
    nc, n_slc = S // CMP_STRIDE, S // SLC_LEN
    n = np.arange(nc)[None, :]
    j = np.arange(n_slc)[:, None]
    ov = (n * CMP_STRIDE < (j + 1) * SLC_LEN) & (n * CMP_STRIDE + CMP_LEN - 1 >= j * SLC_LEN) & (n < nc - 1)
    return jnp.asarray(ov, BF16)


def _head_cols(w, heads):
    return jnp.concatenate([w[:, off:off + HEAD_DIM] for _, off in heads], axis=1)


def _head_gains(heads, norm_of, scale_of):
    ones = jnp.ones((HEAD_DIM,), F32)
    gains = [norm_of[name].astype(F32) * scale_of.get(name, 1.0) if name in norm_of else ones for name, _ in heads]
    return jnp.concatenate(gains).reshape(1, -1)


def kernel(x, rel_bias, ffn1_norm, ffn1_w_gate, ffn1_w_up, ffn1_w_down, mix_norm, w_in, nsa_q_norm, nsa_k_norm, nsa_cmp_pos, nsa_cmp_k_w1, nsa_cmp_k_w2, nsa_cmp_v_w1, nsa_cmp_v_w2, swa_q_norm, swa_k_norm, swa_sinks, w_up_nsa, w_up_sb, w_up_swa, w_out, ffn2_norm, ffn2_w_gate, ffn2_w_up, ffn2_w_down):
    B, S, D = x.shape
    T = B * S
    nb = S // Q_BLOCK
    nc = S // CMP_STRIDE
    depth = w_in.shape[0]
    scale = HEAD_DIM ** -0.5
    scale_of = {"nq": scale * LOG2E, "sq": scale, "wq": scale}
    qk_bound = (BOUND_SLACK * LOG2E * HEAD_DIM ** 0.5
                * jnp.max(jnp.abs(nsa_q_norm.astype(F32))) * jnp.max(jnp.abs(nsa_k_norm.astype(F32))))
    cbias, sbias, wbias, msel, swa_bias = _bias_tables(rel_bias, nb, qk_bound)
    plan = _proj_plan(scale_of)
    ov_t = _overlap_t(S)
    tri = jnp.asarray(np.tril(np.ones((SB_KT, SB_KT), np.float32), -1), BF16)

    (ffn1_w_gate, ffn1_w_up, ffn1_w_down, ffn2_w_gate, ffn2_w_up, ffn2_w_down, w_up_nsa, w_up_sb, w_up_swa,
     w_out) = (w.astype(BF16) for w in (ffn1_w_gate, ffn1_w_up, ffn1_w_down, ffn2_w_gate, ffn2_w_up, ffn2_w_down,
                                        w_up_nsa, w_up_sb, w_up_swa, w_out))
    w_bgate = w_in[:, :, OFF_BGATE:].astype(BF16)

    x2 = x.reshape(T, D)
    for l in range(depth):
        x2 = _ffn(x2, ffn1_norm[l], ffn1_w_gate[l], ffn1_w_up[l], ffn1_w_down[l])

        norm_of = {"nq": nsa_q_norm[l], "nks": nsa_k_norm[l], "nkw": nsa_k_norm[l],
                   "wq": swa_q_norm[l], "wk": swa_k_norm[l]}
        gain = jnp.concatenate([_head_gains(ROW_HEADS, norm_of, scale_of), _head_gains(COL_HEADS, norm_of, scale_of),
                                jnp.ones((1, LANES), F32)], axis=1)
        w_gate = jnp.pad(w_in[l][:, OFF_NGATE:OFF_NGATE + N_GATE], ((0, 0), (0, LANES - N_GATE)))
        w_all = jnp.concatenate([_head_cols(w_in[l], ROW_HEADS), _head_cols(w_in[l], COL_HEADS), w_gate], axis=1)
        rowmat, colmat, gates = _proj(x2, mix_norm[l], w_all.astype(BF16), gain, plan)
        gates = gates[:N_GATE].reshape(N_BRANCHES, NSA_KV_HEADS, NSA_REP, T)

        rows4 = rowmat.reshape(len(ROW_HEADS), B, nc, CMP_STRIDE * HEAD_DIM)
        cmp_rows, cmp_cols = _compress(rows4, jnp.stack([nsa_cmp_k_w1[l], nsa_cmp_v_w1[l]]),
                                       jnp.stack([nsa_cmp_k_w2[l], nsa_cmp_v_w2[l]]),
                                       nsa_cmp_pos[l], nsa_k_norm[l])

        y_nsa = _nsa_attention(colmat, rowmat, cmp_rows, cmp_cols, gates, cbias, sbias, wbias, ov_t, msel, B, S)
        y_sb = _sb_attention(colmat, rowmat, tri, B, S)
        sinks = jnp.repeat(swa_sinks[l].astype(F32).reshape(SWA_KV_HEADS, 1, SWA_REP), Q_BLOCK, axis=2)
        y_swa = _swa_attention(colmat, rowmat, swa_bias, sinks, B, S)

        x2 = _merge(x2, mix_norm[l], w_bgate[l], y_nsa.reshape(T, -1), y_sb.reshape(T, -1),
                    y_swa.reshape(T, -1), w_up_nsa[l], w_up_sb[l], w_up_swa[l], w_out[l])
        x2 = _ffn(x2, ffn2_norm[l], ffn2_w_gate[l], ffn2_w_up[l], ffn2_w_down[l])
    return x2.reshape(B, S, D)
```

```python
import functools
import math

import numpy as np
import jax
import jax.numpy as jnp
from jax import lax
from jax.experimental import pallas as pl
from jax.experimental.pallas import tpu as pltpu

D_MODEL = 1024
HEAD_DIM = 64
Q_BLOCK = 128
NSA_HEADS = 8
NSA_KV_HEADS = 2
NSA_REP = NSA_HEADS // NSA_KV_HEADS
CMP_LEN = 32
CMP_STRIDE = 16
CMP_HIDDEN = 256
SLC_LEN = 64
SLC_TOPN = 16
NSA_WINDOW = 512
FORCE_SCORE = 1e4
SB_HEADS = 4
SWA_HEADS = 4
SWA_KV_HEADS = 2
SWA_REP = SWA_HEADS // SWA_KV_HEADS
SWA_WINDOW = 128
NUM_BUCKETS = 32
MAX_DISTANCE = 128
D_FF = 2816
NORM_EPS = 1e-6
NEG_INF = -1e30
N_BRANCHES = 3
N_GATE = 3 * NSA_HEADS

LANES = 128
VMEM_LIMIT = 56 * 1024 * 1024
BF16 = jnp.bfloat16
F32 = jnp.float32

_sizes = (NSA_HEADS * HEAD_DIM,) + (NSA_KV_HEADS * HEAD_DIM,) * 6 + (N_GATE,) \
    + (SB_HEADS * HEAD_DIM,) * 3 + (SWA_HEADS * HEAD_DIM, SWA_KV_HEADS * HEAD_DIM,
                                    SWA_KV_HEADS * HEAD_DIM, N_BRANCHES * D_MODEL)
_offs = np.concatenate([[0], np.cumsum(_sizes)])
(OFF_NQ, OFF_NKC, OFF_NVC, OFF_NKS, OFF_NVS, OFF_NKW, OFF_NVW, OFF_NGATE,
 OFF_SQ, OFF_SK, OFF_SV, OFF_WQ, OFF_WK, OFF_WV, OFF_BGATE) = (int(o) for o in _offs[:-1])

ROW_HEADS = ([("nkc", OFF_NKC + 64 * g) for g in range(2)] + [("nvc", OFF_NVC + 64 * g) for g in range(2)]
             + [("nks", OFF_NKS + 64 * g) for g in range(2)] + [("nkw", OFF_NKW + 64 * g) for g in range(2)]
             + [("sq", OFF_SQ + 64 * h) for h in range(4)] + [("sv", OFF_SV + 64 * h) for h in range(4)]
             + [("wk", OFF_WK + 64 * g) for g in range(2)])
ROW_KC, ROW_VC, ROW_KS, ROW_KW, ROW_SQ, ROW_SV, ROW_WK = 0, 2, 4, 6, 8, 12, 16
COL_HEADS = ([("nq", OFF_NQ + 64 * h) for h in range(8)] + [("nvs", OFF_NVS + 64 * g) for g in range(2)]
             + [("nvw", OFF_NVW + 64 * g) for g in range(2)] + [("sk", OFF_SK + 64 * h) for h in range(4)]
             + [("wq", OFF_WQ + 64 * h) for h in range(4)] + [("wv", OFF_WV + 64 * g) for g in range(2)])
COL_NQ, COL_VS, COL_VW, COL_SK, COL_WQ, COL_WV = 0, 8, 10, 12, 16, 20


def _dot(a, b):
    return jnp.dot(a, b, preferred_element_type=F32)


def _rms_rows(x, w):
    ms = jnp.mean(x * x, axis=-1, keepdims=True)
    return x * lax.rsqrt(ms + NORM_EPS) * w


def _cparams(sem, flags=None):
    return pltpu.CompilerParams(dimension_semantics=sem, vmem_limit_bytes=VMEM_LIMIT, flags=flags)


FFN_TM = 512
MXU_TILE = 256
FFN_FC = MXU_TILE


def _ffn_kernel(x_ref, nw_ref, wg_ref, wu_ref, wd_ref, o_ref):
    x = x_ref[...]
    h = _rms_rows(x, nw_ref[...]).astype(BF16)
    acc = jnp.zeros(x.shape, F32)
    for c in range(D_FF // FFN_FC):
        cols = slice(c * FFN_FC, (c + 1) * FFN_FC)
        g = _dot(h, wg_ref[:, cols])
        u = _dot(h, wu_ref[:, cols])
        a = (g * jax.nn.sigmoid(g)) * u
        acc = acc + _dot(a.astype(BF16), wd_ref[cols, :])
    o_ref[...] = x + 0.5 * acc


def _ffn(x2, nw, wg, wu, wd):
    T = x2.shape[0]
    resident = lambda shape: pl.BlockSpec(shape, lambda i: (0, 0), pipeline_mode=pl.Buffered(1))
    return pl.pallas_call(
        _ffn_kernel,
        grid=(T // FFN_TM,),
        in_specs=[
            pl.BlockSpec((FFN_TM, D_MODEL), lambda i: (i, 0)),
            pl.BlockSpec((1, D_MODEL), lambda i: (0, 0)),
            resident((D_MODEL, D_FF)), resident((D_MODEL, D_FF)), resident((D_FF, D_MODEL)),
        ],
        out_specs=pl.BlockSpec((FFN_TM, D_MODEL), lambda i: (i, 0)),
        out_shape=jax.ShapeDtypeStruct((T, D_MODEL), F32),
        compiler_params=_cparams(("parallel",)),
        name="ffn",
    )(x2, nw.reshape(1, D_MODEL), wg.astype(BF16), wu.astype(BF16), wd.astype(BF16))


PROJ_TM = 512


GATE_ROWS = 32

_NORMED = ("nq", "nks", "nkw", "wq", "wk")


def _proj_plan(scale_of):
    plan = []
    for kind, heads in (("row", ROW_HEADS), ("col", COL_HEADS)):
        for p in range(len(heads) // 2):
            name = heads[2 * p][0]
            assert heads[2 * p + 1][0] == name
            plan.append((kind, p, name in _NORMED, 1.0 if name in _NORMED else scale_of.get(name, 1.0)))
    plan.append(("gate", 0, False, 1.0))
    return tuple(plan)


def _proj_kernel(x_ref, nw_ref, w_ref, gain_ref, bd_ref, row_ref, col_ref, gate_ref, *, plan):
    h = _rms_rows(x_ref[...], nw_ref[...]).astype(BF16)
    pair = 2 * LANES
    for c0 in range(0, len(plan) * LANES, pair):
        width = min(pair, len(plan) * LANES - c0)
        zg = _dot(h, w_ref[:, c0:c0 + width])
        for u in range(width // LANES):
            t = c0 // LANES + u
            kind, p, normed, scale = plan[t]
            z = zg[:, u * LANES:(u + 1) * LANES]
            if normed:
                ms = _dot((z * z).astype(BF16), bd_ref[...])
                z = z * lax.rsqrt(ms + NORM_EPS) * gain_ref[:, t * LANES:(t + 1) * LANES]
            elif scale != 1.0:
                z = z * scale
            if kind == "row":
                row_ref[2 * p] = z[:, :HEAD_DIM].astype(row_ref.dtype)
                row_ref[2 * p + 1] = z[:, HEAD_DIM:].astype(row_ref.dtype)
            elif kind == "col":
                z_t = z.T
                for c in range(PROJ_TM // LANES):
                    col_ref[2 * p, c] = z_t[:HEAD_DIM, c * LANES:(c + 1) * LANES].astype(col_ref.dtype)
                    col_ref[2 * p + 1, c] = z_t[HEAD_DIM:, c * LANES:(c + 1) * LANES].astype(col_ref.dtype)
            else:
                gate_ref[...] = jax.nn.sigmoid(z).T[:GATE_ROWS]


def _proj(x2, nw, w, gain, plan):
    T = x2.shape[0]
    n_row, n_col = len(ROW_HEADS), len(COL_HEADS)
    bd = np.kron(np.eye(2, dtype=np.float32), np.full((HEAD_DIM, HEAD_DIM), 1.0 / HEAD_DIM, np.float32))
    return pl.pallas_call(
        functools.partial(_proj_kernel, plan=plan),
        grid=(T // PROJ_TM,),
        in_specs=[
            pl.BlockSpec((PROJ_TM, D_MODEL), lambda i: (i, 0)),
            pl.BlockSpec((1, D_MODEL), lambda i: (0, 0)),
            pl.BlockSpec(w.shape, lambda i: (0, 0)),
            pl.BlockSpec(gain.shape, lambda i: (0, 0)),
            pl.BlockSpec((LANES, LANES), lambda i: (0, 0)),
        ],
        out_specs=[
            pl.BlockSpec((n_row, PROJ_TM, HEAD_DIM), lambda i: (0, i, 0)),
            pl.BlockSpec((n_col, PROJ_TM // LANES, HEAD_DIM, LANES), lambda i: (0, i, 0, 0)),
            pl.BlockSpec((GATE_ROWS, PROJ_TM), lambda i: (0, i)),
        ],
        out_shape=[jax.ShapeDtypeStruct((n_row, T, HEAD_DIM), BF16),
                   jax.ShapeDtypeStruct((n_col, T // LANES, HEAD_DIM, LANES), BF16),
                   jax.ShapeDtypeStruct((GATE_ROWS, T), F32)],
        compiler_params=_cparams(("parallel",)),
        name="proj",
    )(x2, nw.reshape(1, D_MODEL), w, gain, jnp.asarray(bd, BF16))


def _compress_kernel(r_ref, w1_ref, w2_ref, pos_ref, kn_ref, row_ref, col_ref):
    kind = pl.program_id(0) // NSA_KV_HEADS
    r = r_ref[0, 0]
    nc = r.shape[0]
    half = CMP_STRIDE * HEAD_DIM
    w1 = w1_ref[0]
    p_lo = _dot(r, w1[:half])
    p_hi = _dot(r, w1[half:])
    p_pos = _dot(jnp.broadcast_to(pos_ref[...], (8, CMP_LEN * HEAD_DIM)).astype(BF16), w1)[:1]
    hdn = p_lo + pltpu.roll(p_hi, nc - 1, 0) + p_pos
    hdn = hdn * jax.nn.sigmoid(hdn)
    out = _dot(hdn.astype(BF16), w2_ref[0])
    ms = jnp.sum(out * out, axis=-1, keepdims=True) * (1.0 / HEAD_DIM)
    normed = out * lax.rsqrt(ms + NORM_EPS) * kn_ref[...]
    out = jnp.where(kind == 0, normed, out)
    row_ref[0, 0] = out[:, :HEAD_DIM].astype(row_ref.dtype)
    col_ref[0, 0] = out.T[:HEAD_DIM].astype(col_ref.dtype)


def _compress(rows4, w1, w2, pos, k_norm):
    assert ROW_KC == 0 and ROW_VC == NSA_KV_HEADS
    _, B, NC, _ = rows4.shape
    kn = jnp.pad(k_norm.reshape(1, HEAD_DIM), ((0, 0), (0, LANES - HEAD_DIM)))
    w2p = jnp.pad(w2, ((0, 0), (0, 0), (0, LANES - HEAD_DIM))).astype(BF16)
    return pl.pallas_call(
        _compress_kernel,
        grid=(4, B),
        in_specs=[
            pl.BlockSpec((1, 1, NC, CMP_STRIDE * HEAD_DIM), lambda h, b: (h, b, 0, 0)),
            pl.BlockSpec((1, CMP_LEN * HEAD_DIM, CMP_HIDDEN), lambda h, b: (h // NSA_KV_HEADS, 0, 0)),
            pl.BlockSpec((1, CMP_HIDDEN, LANES), lambda h, b: (h // NSA_KV_HEADS, 0, 0)),
            pl.BlockSpec((1, CMP_LEN * HEAD_DIM), lambda h, b: (0, 0)),
            pl.BlockSpec((1, LANES), lambda h, b: (0, 0)),
        ],
        out_specs=[
            pl.BlockSpec((1, 1, NC, HEAD_DIM), lambda h, b: (h, b, 0, 0)),
            pl.BlockSpec((1, 1, HEAD_DIM, NC), lambda h, b: (h, b, 0, 0)),
        ],
        out_shape=[jax.ShapeDtypeStruct((4, B, NC, HEAD_DIM), BF16),
                   jax.ShapeDtypeStruct((4, B, HEAD_DIM, NC), BF16)],
        compiler_params=_cparams(("parallel", "parallel")),
        name="nsa_compress",
    )(rows4, w1.astype(BF16), w2p, pos.reshape(1, CMP_LEN * HEAD_DIM), kn)


def _softmax_tile_update(s, v_t, m, l, acc):
    m_new = jnp.maximum(m, jnp.max(s, axis=0, keepdims=True))
    alpha = jnp.exp2(m - m_new)
    p = jnp.exp2(s - m_new)
    l_new = alpha * l + jnp.sum(p, axis=0, keepdims=True)
    acc_new = alpha * acc + _dot(v_t, p.astype(BF16))
    return m_new, l_new, acc_new


def _pairs_to_rows(o_t):
    n = o_t.shape[1] // LANES
    outs = []
    for p in range(n // 2):
        a = o_t[:, (2 * p) * LANES:(2 * p + 1) * LANES]
        b = o_t[:, (2 * p + 1) * LANES:(2 * p + 2) * LANES]
        outs.append(jnp.concatenate([a, b], axis=0).T)
    return outs[0] if len(outs) == 1 else jnp.concatenate(outs, axis=1)


FAR_TILES = 4
LOG2E = math.log2(math.e)
NSA_TINY = 2.0 ** -100
BOUND_SLACK = 1.05


def _nsa_kernel(q_ref, kc_ref, vc_ref, ks_ref, vs_ref, kw_ref, vw_ref, gate_ref,
                cbias_ref, sbias_ref, wbias_ref, ov_ref, msel_ref, o_ref, neg_ref, *stage_refs, nb):
    i = pl.program_id(2)
    R = NSA_REP
    N = R * Q_BLOCK
    nc = kc_ref.shape[2]
    q_t = jnp.concatenate([q_ref[r, 0] for r in range(R)], axis=1)
    n_win = NSA_WINDOW // Q_BLOCK

    def k_rows(ref, J0, n_tiles):
        return ref[0, pl.ds(pl.multiple_of(J0 * Q_BLOCK, Q_BLOCK), n_tiles * Q_BLOCK), :]

    def v_cols(ref, J0, n_tiles):
        return jnp.concatenate([ref[0, J0 + u] for u in range(n_tiles)], axis=1)

    def sel_mask(J0, n_tiles):
        parts = [jnp.broadcast_to(neg_ref[pl.ds(2 * J0 + u, 1), :], (SLC_LEN, Q_BLOCK)) for u in range(2 * n_tiles)]
        return jnp.concatenate([jnp.concatenate(parts, axis=0)] * R, axis=1)

    def compressed(fast):
        cb = cbias_ref[0, pl.ds(pl.multiple_of(8 * (nb - 1) - 8 * i, 8), nc), :]
        s = _dot(kc_ref[0, 0], q_t) + cb
        if fast:
            e = jnp.exp2(s)
        else:
            e = jnp.where(s > 0.5 * NEG_INF, jnp.exp2(s - jnp.max(s, axis=0, keepdims=True)), 0.0)
        l = jnp.sum(e, axis=0, keepdims=True)
        p = e * jnp.where(l > 0.0, 1.0 / l, 0.0)
        return p, l, _dot(vc_ref[0, 0], p.astype(BF16))

    def select_blocks(p_c):
        p_sum = p_c[:, 0:Q_BLOCK]
        for r in range(1, R):
            p_sum = p_sum + p_c[:, r * Q_BLOCK:(r + 1) * Q_BLOCK]
        p_hi = p_sum.astype(BF16)
        p_lo = (p_sum - p_hi.astype(F32)).astype(BF16)
        imp = _dot(ov_ref[...], p_hi) + _dot(ov_ref[...], p_lo)
        n_slc = imp.shape[0]
        jj = lax.broadcasted_iota(jnp.int32, imp.shape, 0)
        qq = lax.broadcasted_iota(jnp.int32, imp.shape, 1)
        cur = 2 * i + jnp.where(qq >= SLC_LEN, 1, 0)
        forced = (jj == 0) | (jj == cur) | (jj == cur - 1)
        score = jnp.where(forced, FORCE_SCORE, jnp.where(jj <= cur, imp, -1.0))
        sel = jnp.zeros(imp.shape, F32)
        for _ in range(min(SLC_TOPN, n_slc)):
            best = jnp.max(score, axis=0, keepdims=True)
            first = jnp.min(jnp.where(score == best, jj, n_slc), axis=0, keepdims=True)
            hit = jj == first
            sel = jnp.where(hit, 1.0, sel)
            score = jnp.where(hit, -2.0, score)
        neg_ref[...] = jnp.where(sel > 0.0, msel_ref[0], NEG_INF)

    def add_chunk(st, s, v_t):
        e = jnp.exp2(s)
        return st[0] + jnp.sum(e, axis=0, keepdims=True), st[1] + _dot(v_t, e.astype(BF16))

    zero = (jnp.zeros((1, N), F32), jnp.zeros((HEAD_DIM, N), F32))

    def selected_fast():
        J0 = jnp.maximum(i - 1, 0)
        bias = sbias_ref[0, pl.ds(pl.multiple_of((2 - (i - J0)) * Q_BLOCK, Q_BLOCK), 2 * Q_BLOCK), :]
        st = add_chunk(zero, _dot(k_rows(ks_ref, J0, 2), q_t) + bias + sel_mask(J0, 2), v_cols(vs_ref, J0, 2))
        n_far = jnp.maximum(i - 1, 0)

        def far_chunk(J0, st, n_valid):
            mask = sel_mask(J0, FAR_TILES)
            if n_valid is not None:
                row = lax.broadcasted_iota(jnp.int32, mask.shape, 0)
                mask = jnp.where(row < n_valid * Q_BLOCK, mask, NEG_INF)
            return add_chunk(st, _dot(k_rows(ks_ref, J0, FAR_TILES), q_t) + mask, v_cols(vs_ref, J0, FAR_TILES))

        n_chunks = n_far // FAR_TILES
        rem = n_far - FAR_TILES * n_chunks

        def tile_of(c):
            return jnp.clip(n_far - FAR_TILES * (c + 1), 0, nb - FAR_TILES)

        def scores(c):
            return _dot(k_rows(ks_ref, tile_of(c), FAR_TILES), q_t)

        def values(c, e_ref, acc):
            return acc + _dot(v_cols(vs_ref, tile_of(c), FAR_TILES), e_ref[...])

        def stage(c, st, s_in, s_out, e_in, e_out):
            s_out[...] = scores(c + 1)
            acc = values(c - 1, e_in, st[1])
            e = jnp.exp2(s_in[...] + sel_mask(tile_of(c), FAR_TILES))
            e_out[...] = e.astype(BF16)
            return st[0] + jnp.sum(e, axis=0, keepdims=True), acc

        s_a, s_b, e_a, e_b = stage_refs
        e_b[...] = jnp.zeros_like(e_b)
        s_a[...] = scores(0)

        def pair(c, st):
            st = stage(c, st, s_a, s_b, e_b, e_a)
            return stage(c + 1, st, s_b, s_a, e_a, e_b)

        n_quads = n_chunks // 4
        st = lax.fori_loop(0, n_quads, lambda t, st: pair(4 * t + 2, pair(4 * t, st)), st)
        st = lax.cond(n_chunks % 4 >= 2, lambda st: pair(4 * n_quads, st), lambda st: st, st)

        def odd_tail(st):
            st = stage(n_chunks - 1, st, s_a, s_b, e_b, e_a)
            return st[0], values(n_chunks - 1, e_a, st[1])

        st = lax.cond(n_chunks % 2 == 1, odd_tail, lambda st: (st[0], values(n_chunks - 1, e_b, st[1])), st)
        return lax.cond(rem > 0, lambda st: far_chunk(0, st, rem), lambda st: st, st)

    def window_fast():
        J0 = jnp.maximum(i - n_win, 0)
        bias = wbias_ref[0, pl.ds(pl.multiple_of((n_win - (i - J0)) * Q_BLOCK, Q_BLOCK), (n_win + 1) * Q_BLOCK), :]
        return add_chunk(zero, _dot(k_rows(kw_ref, J0, n_win + 1), q_t) + bias, v_cols(vw_ref, J0, n_win + 1))

    def sweep_exact(k_ref, v_ref, bias_ref, far_dist, n_back, with_sel):
        def body(t, st):
            J = i - t
            off = pl.multiple_of((far_dist - jnp.minimum(t, far_dist)) * Q_BLOCK, Q_BLOCK)
            s = _dot(k_rows(k_ref, J, 1), q_t) + bias_ref[0, pl.ds(off, Q_BLOCK), :]
            if with_sel:
                s = s + sel_mask(J, 1)
            return _softmax_tile_update(s, v_ref[0, J], *st)

        init = (jnp.full((1, N), NEG_INF, F32),) + zero
        _, l, acc = lax.fori_loop(0, jnp.minimum(i, n_back) + 1, body, init)
        return l, acc

    p_c, l_c, o_c = compressed(True)
    select_blocks(p_c)
    l_w, a_w = window_fast()
    l_s, a_s = selected_fast()
    lane_q = lax.broadcasted_iota(jnp.int32, (1, N), 1) % Q_BLOCK
    has_cmp = (i > 0) | (lane_q >= CMP_LEN - 1)
    l_min = jnp.minimum(jnp.minimum(l_s, l_w), jnp.where(has_cmp, l_c, 1.0))

    def exact_path():
        p_x, _, o_x = compressed(False)
        select_blocks(p_x)
        l_sx, a_sx = sweep_exact(ks_ref, vs_ref, sbias_ref, 2, nb, True)
        l_wx, a_wx = sweep_exact(kw_ref, vw_ref, wbias_ref, n_win, n_win, False)
        return o_x, a_sx * (1.0 / l_sx), a_wx * (1.0 / l_wx)

    o_c, o_s, o_w = lax.cond(jnp.min(l_min) > NSA_TINY,
                             lambda: (o_c, a_s * (1.0 / l_s), a_w * (1.0 / l_w)), exact_path)

    def gate_row(br):
        return jnp.concatenate([gate_ref[br, 0, r:r + 1, :] for r in range(R)], axis=1)

    o_t = gate_row(0) * o_c + gate_row(1) * o_s + gate_row(2) * o_w
    o_ref[0] = _pairs_to_rows(o_t).astype(o_ref.dtype)


def _nsa_attention(colmat, rowmat, cmp_rows, cmp_cols, gates, cbias, sbias, wbias, ov_t, msel, B, S):
    nb = S // Q_BLOCK
    nc = S // CMP_STRIDE
    n_slc = S // SLC_LEN
    G, R = NSA_KV_HEADS, NSA_REP
    kern = functools.partial(_nsa_kernel, nb=nb)
    return pl.pallas_call(
        kern,
        grid=(B, G, nb),
        in_specs=[
            pl.BlockSpec((R, 1, HEAD_DIM, LANES), lambda b, g, i: (g, b * nb + i, 0, 0)),
            pl.BlockSpec((1, 1, nc, HEAD_DIM), lambda b, g, i: (g, b, 0, 0)),
            pl.BlockSpec((1, 1, HEAD_DIM, nc), lambda b, g, i: (NSA_KV_HEADS + g, b, 0, 0)),
            pl.BlockSpec((1, S, HEAD_DIM), lambda b, g, i: (ROW_KS + g, b, 0)),
            pl.BlockSpec((1, nb, HEAD_DIM, LANES), lambda b, g, i: (COL_VS + g, b, 0, 0)),
            pl.BlockSpec((1, S, HEAD_DIM), lambda b, g, i: (ROW_KW + g, b, 0)),
            pl.BlockSpec((1, nb, HEAD_DIM, LANES), lambda b, g, i: (COL_VW + g, b, 0, 0)),
            pl.BlockSpec((N_BRANCHES, 1, R, LANES), lambda b, g, i: (0, g, 0, b * nb + i)),
            pl.BlockSpec((1, cbias.shape[1], R * Q_BLOCK), lambda b, g, i: (g, 0, 0)),
            pl.BlockSpec((1, sbias.shape[1], R * Q_BLOCK), lambda b, g, i: (g, 0, 0)),
            pl.BlockSpec((1, wbias.shape[1], R * Q_BLOCK), lambda b, g, i: (g, 0, 0)),
            pl.BlockSpec((n_slc, nc), lambda b, g, i: (0, 0)),
            pl.BlockSpec((1, 1, Q_BLOCK), lambda b, g, i: (g, 0, 0)),
        ],
        out_specs=pl.BlockSpec((1, Q_BLOCK, R * HEAD_DIM), lambda b, g, i: (b, i, g)),
        out_shape=jax.ShapeDtypeStruct((B, S, NSA_HEADS * HEAD_DIM), BF16),
        scratch_shapes=[pltpu.VMEM((n_slc, Q_BLOCK), F32)]
        + [pltpu.VMEM((FAR_TILES * Q_BLOCK, R * Q_BLOCK), F32)] * 2
        + [pltpu.VMEM((FAR_TILES * Q_BLOCK, R * Q_BLOCK), BF16)] * 2,
        compiler_params=_cparams(("parallel", "parallel", "arbitrary")),
        name="nsa_attention",
    )(colmat, cmp_rows, cmp_cols, rowmat, colmat, rowmat, colmat, gates, cbias, sbias, wbias, ov_t, msel)


SB_QB = 512
SB_KT = 256
SB_ZERO_LOG = -110.0


def _sb_kernel(q_ref, k_ref, v_ref, tri_ref, o_ref):
    i = pl.program_id(1)
    H = SB_HEADS
    tri = tri_ref[...]
    rows = lax.broadcasted_iota(jnp.int32, (SB_QB, SB_KT), 0)
    cols = lax.broadcasted_iota(jnp.int32, (SB_QB, SB_KT), 1)
    key_minus_query = cols - rows
    per_q = SB_QB // SB_KT

    def step(J, st, masked):
        out = []
        for h in range(H):
            carry, acc = st[h]
            k_t = jnp.concatenate([k_ref[h, (SB_KT // LANES) * J + c] for c in range(SB_KT // LANES)], axis=1)
            z = _dot(q_ref[h], k_t)
            sp = jnp.maximum(z, 0.0) + jnp.log(1.0 + jnp.exp(-jnp.abs(z)))
            if masked:
                valid = key_minus_query < i * SB_QB - J * SB_KT
                lf = jnp.where(valid, -sp, 0.0)
            else:
                lf = -sp
            lf_hi = lf.astype(BF16)
            lf_lo = (lf - lf_hi.astype(F32)).astype(BF16)
            after = _dot(lf_hi, tri) + _dot(lf_lo, tri) + carry
            w = jnp.exp(z - sp + after)
            if masked:
                w = jnp.where(valid, w, 0.0)
            v = v_ref[h, pl.ds(pl.multiple_of(J * SB_KT, SB_KT), SB_KT), :]
            acc = acc + _dot(w.astype(BF16), v)
            carry = carry + jnp.sum(lf, axis=1, keepdims=True)
            out.append((carry, acc))
        return tuple(out)

    st = tuple((jnp.zeros((SB_QB, 1), F32), jnp.zeros((SB_QB, HEAD_DIM), F32)) for _ in range(H))
    for d in range(per_q):
        st = step(per_q * i + per_q - 1 - d, st, True)

    def alive(st):
        worst = st[0][0]
        for h in range(1, H):
            worst = jnp.maximum(worst, st[h][0])
        return jnp.max(worst) > SB_ZERO_LOG

    def body(c):
        J, _, st = c
        st = step(J, st, False)
        return J - 1, alive(st), st

    _, _, st = lax.while_loop(lambda c: (c[0] >= 0) & c[1], body, (per_q * i - 1, alive(st), st))
    o_ref[0] = jnp.concatenate([st[h][1] for h in range(H)], axis=1).astype(o_ref.dtype)


def _sb_attention(colmat, rowmat, tri, B, S):
    nq = S // SB_QB
    nb = S // LANES
    H = SB_HEADS
    return pl.pallas_call(
        _sb_kernel,
        grid=(B, nq),
        in_specs=[
            pl.BlockSpec((H, SB_QB, HEAD_DIM), lambda b, i: (ROW_SQ // H, b * nq + i, 0)),
            pl.BlockSpec((H, nb, HEAD_DIM, LANES), lambda b, i: (COL_SK // H, b, 0, 0)),
            pl.BlockSpec((H, S, HEAD_DIM), lambda b, i: (ROW_SV // H, b, 0)),
            pl.BlockSpec((SB_KT, SB_KT), lambda b, i: (0, 0)),
        ],
        out_specs=pl.BlockSpec((1, SB_QB, H * HEAD_DIM), lambda b, i: (b, i, 0)),
        out_shape=jax.ShapeDtypeStruct((B, S, H * HEAD_DIM), BF16),
        compiler_params=_cparams(("parallel", "arbitrary")),
        name="sb_attention",
    )(rowmat, colmat, rowmat, tri)


SWA_SUB = 4


def _swa_kernel(q_ref, k_ref, v_ref, bias_ref, sink_ref, o_ref):
    R = SWA_REP
    sink = sink_ref[0]
    for u in range(SWA_SUB):
        i = pl.program_id(2) * SWA_SUB + u
        q_t = jnp.concatenate([q_ref[r, u] for r in range(R)], axis=1)
        J0 = jnp.maximum(i - 1, 0)
        k = k_ref[0, pl.ds(pl.multiple_of(J0 * Q_BLOCK, Q_BLOCK), 2 * Q_BLOCK), :]
        bias = bias_ref[0, pl.ds(pl.multiple_of((1 - (i - J0)) * Q_BLOCK, Q_BLOCK), 2 * Q_BLOCK), :]
        s = _dot(k, q_t) + bias
        m = jnp.maximum(jnp.max(s, axis=0, keepdims=True), sink)
        e = jnp.exp(s - m)
        den = jnp.sum(e, axis=0, keepdims=True) + jnp.exp(sink - m)
        v_t = jnp.concatenate([v_ref[0, J0], v_ref[0, J0 + 1]], axis=1)
        o_t = _dot(v_t, e.astype(BF16)) * (1.0 / den)
        o_ref[0, u * Q_BLOCK:(u + 1) * Q_BLOCK, :] = _pairs_to_rows(o_t).astype(o_ref.dtype)


def _swa_attention(colmat, rowmat, bias, sinks, B, S):
    nb = S // Q_BLOCK
    ns = nb // SWA_SUB
    G, R = SWA_KV_HEADS, SWA_REP
    return pl.pallas_call(
        _swa_kernel,
        grid=(B, G, ns),
        in_specs=[
            pl.BlockSpec((R, SWA_SUB, HEAD_DIM, LANES), lambda b, g, i: (COL_WQ // R + g, b * ns + i, 0, 0)),
            pl.BlockSpec((1, S, HEAD_DIM), lambda b, g, i: (ROW_WK + g, b, 0)),
            pl.BlockSpec((1, nb, HEAD_DIM, LANES), lambda b, g, i: (COL_WV + g, b, 0, 0)),
            pl.BlockSpec((1, 3 * Q_BLOCK, R * Q_BLOCK), lambda b, g, i: (g, 0, 0)),
            pl.BlockSpec((1, 1, R * Q_BLOCK), lambda b, g, i: (g, 0, 0)),
        ],
        out_specs=pl.BlockSpec((1, SWA_SUB * Q_BLOCK, R * HEAD_DIM), lambda b, g, i: (b, i, g)),
        out_shape=jax.ShapeDtypeStruct((B, S, SWA_HEADS * HEAD_DIM), BF16),
        compiler_params=_cparams(("parallel", "parallel", "arbitrary")),
        name="swa_attention",
    )(colmat, rowmat, colmat, bias, sinks)


MERGE_TM = 512


def _merge_kernel(x_ref, nw_ref, wg_ref, ya_ref, yb_ref, yc_ref, ua_ref, ub_ref, uc_ref, wo_ref, o_ref):
    x = x_ref[...]
    h = _rms_rows(x, nw_ref[...]).astype(BF16)
    merged = None
    for br, (y_ref, u_ref) in enumerate(((ya_ref, ua_ref), (yb_ref, ub_ref), (yc_ref, uc_ref))):
        g = jax.nn.sigmoid(_dot(h, wg_ref[:, br * D_MODEL:(br + 1) * D_MODEL]))
        t = g * _dot(y_ref[...], u_ref[...])
        merged = t if merged is None else merged + t
    o_ref[...] = x + _dot(merged.astype(BF16), wo_ref[...])


def _merge(x2, nw, w_bgate, y_nsa, y_sb, y_swa, u_nsa, u_sb, u_swa, w_out):
    T = x2.shape[0]
    full = lambda shape: pl.BlockSpec(shape, lambda i: (0, 0))
    rows = lambda n: pl.BlockSpec((MERGE_TM, n), lambda i: (i, 0))
    return pl.pallas_call(
        _merge_kernel,
        grid=(T // MERGE_TM,),
        in_specs=[rows(D_MODEL), full((1, D_MODEL)), full((D_MODEL, N_BRANCHES * D_MODEL)),
                  rows(y_nsa.shape[1]), rows(y_sb.shape[1]), rows(y_swa.shape[1]),
                  full(u_nsa.shape), full(u_sb.shape), full(u_swa.shape), full((D_MODEL, D_MODEL))],
        out_specs=rows(D_MODEL),
        out_shape=jax.ShapeDtypeStruct((T, D_MODEL), F32),
        compiler_params=_cparams(("parallel",)),
        name="merge",
    )(x2, nw.reshape(1, D_MODEL), w_bgate.astype(BF16), y_nsa, y_sb, y_swa,
      u_nsa.astype(BF16), u_sb.astype(BF16), u_swa.astype(BF16), w_out.astype(BF16))


def _t5_bucket(dist):
    max_exact = NUM_BUCKETS // 2
    d = jnp.maximum(dist, 0)
    df = jnp.maximum(d, 1).astype(F32)
    large = max_exact + (jnp.log(df / max_exact) / math.log(MAX_DISTANCE / max_exact)
                         * (NUM_BUCKETS - max_exact)).astype(jnp.int32)
    large = jnp.minimum(large, NUM_BUCKETS - 1)
    return jnp.where(d < max_exact, d, large)


def _toeplitz(ext, delta):
    H = ext.shape[0]
    n = Q_BLOCK
    c = delta + MAX_DISTANCE
    w = ext[:, c - (n - 1):c + n]
    flat = jnp.pad(jnp.broadcast_to(w[:, None, :], (H, n, 2 * n - 1)), ((0, 0), (0, 0), (0, 1))).reshape(H, 2 * n * n)
    return flat[:, n - 1:n - 1 + n * (2 * n - 1)].reshape(H, n, 2 * n - 1)[:, :, :n]


def _lanes(t, G, R):
    K, Q = t.shape[1:]
    return t.reshape(G, R, K, Q).transpose(0, 2, 1, 3).reshape(G, K, R * Q)


def _bias_tables(rel_bias, nb, qk_bound):
    n = Q_BLOCK
    G, R = NSA_KV_HEADS, NSA_REP
    by_dist = rel_bias[_t5_bucket(jnp.arange(MAX_DISTANCE)), :].T.astype(F32)
    far = rel_bias[NUM_BUCKETS - 1, :].astype(F32)
    neg = jnp.full((1, n), NEG_INF, F32)
    k = jnp.arange(n)[:, None]
    q = jnp.arange(n)[None, :]
    above = (k > q)[None]

    rel = (by_dist[:NSA_HEADS] - far[:NSA_HEADS, None]) * LOG2E
    m_g = qk_bound + jnp.maximum(jnp.max(rel.reshape(G, -1), axis=1), 0.0)
    m_h = jnp.repeat(m_g, R)[:, None, None]
    ext = jnp.concatenate([jnp.broadcast_to(neg, (NSA_HEADS, n)), rel, jnp.zeros((NSA_HEADS, n), F32)], axis=1)
    t0, t1 = _toeplitz(ext, 0), _toeplitz(ext, n)
    zeros = jnp.zeros_like(t0)
    masked = jnp.full_like(t0, NEG_INF)
    sbias = _lanes(jnp.concatenate([zeros, t1, t0, masked], axis=1), G, R)
    n_win = NSA_WINDOW // n
    edge = jnp.where(above, zeros, NEG_INF)
    tiles = [edge] + [zeros] * (n_win - 2) + [t1, t0]
    wbias = _lanes(jnp.concatenate([t - m_h for t in tiles] + [masked] * n_win, axis=1), G, R)
    off = 8 * (nb - 1)
    near = jnp.stack([ext[:, 97 - CMP_STRIDE * m:97 - CMP_STRIDE * m + n] for m in range(-9, 7)], axis=1)
    n_rows = 16 * nb - 8
    cb = jnp.concatenate([jnp.zeros((NSA_HEADS, off - 9, n), F32), near,
                          jnp.full((NSA_HEADS, n_rows - off - 7, n), NEG_INF, F32)], axis=1) - m_h
    cbias = _lanes(cb, G, R)
    msel = jnp.broadcast_to(-m_g[:, None, None], (G, 1, n))

    hs = slice(NSA_HEADS, NSA_HEADS + SWA_HEADS)
    ext_w = jnp.concatenate([jnp.broadcast_to(neg, (SWA_HEADS, n)), by_dist[hs],
                             jnp.broadcast_to(far[hs, None], (SWA_HEADS, n))], axis=1)
    s0 = _toeplitz(ext_w, 0)
    s1 = jnp.where(above, _toeplitz(ext_w, n), NEG_INF)
    swa_bias = _lanes(jnp.concatenate([s1, s0, jnp.full_like(s0, NEG_INF)], axis=1), SWA_KV_HEADS, SWA_REP)
    return cbias, sbias, wbias, msel, swa_bias


def _overlap_t(S):
    nc, n_slc = S // CMP_STRIDE, S // SLC_LEN
    n = np.arange(nc)[None, :]
    j = np.arange(n_slc)[:, None]
    ov = (n * CMP_STRIDE < (j + 1) * SLC_LEN) & (n * CMP_STRIDE + CMP_LEN - 1 >= j * SLC_LEN) & (n < nc - 1)
    return jnp.asarray(ov, BF16)


def _head_cols(w, heads):
    return jnp.concatenate([w[:, off:off + HEAD_DIM] for _, off in heads], axis=1)


def _head_gains(heads, norm_of, scale_of):
    ones = jnp.ones((HEAD_DIM,), F32)
    gains = [norm_of[name].astype(F32) * scale_of.get(name, 1.0) if name in norm_of else ones for name, _ in heads]
    return jnp.concatenate(gains).reshape(1, -1)


def kernel(x, rel_bias, ffn1_norm, ffn1_w_gate, ffn1_w_up, ffn1_w_down, mix_norm, w_in, nsa_q_norm, nsa_k_norm, nsa_cmp_pos, nsa_cmp_k_w1, nsa_cmp_k_w2, nsa_cmp_v_w1, nsa_cmp_v_w2, swa_q_norm, swa_k_norm, swa_sinks, w_up_nsa, w_up_sb, w_up_swa, w_out, ffn2_norm, ffn2_w_gate, ffn2_w_up, ffn2_w_down):
    B, S, D = x.shape
    T = B * S
    nb = S // Q_BLOCK
    nc = S // CMP_STRIDE
    depth = w_in.shape[0]
    scale = HEAD_DIM ** -0.5
    scale_of = {"nq": scale * LOG2E, "sq": scale, "wq": scale}
    qk_bound = (BOUND_SLACK * LOG2E * HEAD_DIM ** 0.5
                * jnp.max(jnp.abs(nsa_q_norm.astype(F32))) * jnp.max(jnp.abs(nsa_k_norm.astype(F32))))
    cbias, sbias, wbias, msel, swa_bias = _bias_tables(rel_bias, nb, qk_bound)
    plan = _proj_plan(scale_of)
    ov_t = _overlap_t(S)
    tri = jnp.asarray(np.tril(np.ones((SB_KT, SB_KT), np.float32), -1), BF16)

    (ffn1_w_gate, ffn1_w_up, ffn1_w_down, ffn2_w_gate, ffn2_w_up, ffn2_w_down, w_up_nsa, w_up_sb, w_up_swa,
     w_out) = (w.astype(BF16) for w in (ffn1_w_gate, ffn1_w_up, ffn1_w_down, ffn2_w_gate, ffn2_w_up, ffn2_w_down,
                                        w_up_nsa, w_up_sb, w_up_swa, w_out))
    w_bgate = w_in[:, :, OFF_BGATE:].astype(BF16)

    x2 = x.reshape(T, D)
    for l in range(depth):
        x2 = _ffn(x2, ffn1_norm[l], ffn1_w_gate[l], ffn1_w_up[l], ffn1_w_down[l])

        norm_of = {"nq": nsa_q_norm[l], "nks": nsa_k_norm[l], "nkw": nsa_k_norm[l],
                   "wq": swa_q_norm[l], "wk": swa_k_norm[l]}
        gain = jnp.concatenate([_head_gains(ROW_HEADS, norm_of, scale_of), _head_gains(COL_HEADS, norm_of, scale_of),
                                jnp.ones((1, LANES), F32)], axis=1)
        w_gate = jnp.pad(w_in[l][:, OFF_NGATE:OFF_NGATE + N_GATE], ((0, 0), (0, LANES - N_GATE)))
        w_all = jnp.concatenate([_head_cols(w_in[l], ROW_HEADS), _head_cols(w_in[l], COL_HEADS), w_gate], axis=1)
        rowmat, colmat, gates = _proj(x2, mix_norm[l], w_all.astype(BF16), gain, plan)
        gates = gates[:N_GATE].reshape(N_BRANCHES, NSA_KV_HEADS, NSA_REP, T)

        rows4 = rowmat[ROW_KC:ROW_KC + 2 * NSA_KV_HEADS].reshape(2 * NSA_KV_HEADS, B, nc, CMP_STRIDE * HEAD_DIM)
        cmp_rows, cmp_cols = _compress(rows4, jnp.stack([nsa_cmp_k_w1[l], nsa_cmp_v_w1[l]]),
                                       jnp.stack([nsa_cmp_k_w2[l], nsa_cmp_v_w2[l]]),
                                       nsa_cmp_pos[l], nsa_k_norm[l])

        y_nsa = _nsa_attention(colmat, rowmat, cmp_rows, cmp_cols, gates, cbias, sbias, wbias, ov_t, msel, B, S)
        y_sb = _sb_attention(colmat, rowmat, tri, B, S)
        sinks = jnp.repeat(swa_sinks[l].astype(F32).reshape(SWA_KV_HEADS, 1, SWA_REP), Q_BLOCK, axis=2)
        y_swa = _swa_attention(colmat, rowmat, swa_bias, sinks, B, S)

        x2 = _merge(x2, mix_norm[l], w_bgate[l], y_nsa.reshape(T, -1), y_sb.reshape(T, -1),
                    y_swa.reshape(T, -1), w_up_nsa[l], w_up_sb[l], w_up_swa[l], w_out[l])
        x2 = _ffn(x2, ffn2_norm[l], ffn2_w_gate[l], ffn2_w_up[l], ffn2_w_down[l])
    return x2.reshape(B, S, D)
```

```python
import functools
import math

import numpy as np
import jax
import jax.numpy as jnp
from jax import lax
from jax.experimental import pallas as pl
from jax.experimental.pallas import tpu as pltpu

D_MODEL = 1024
HEAD_DIM = 64
Q_BLOCK = 128
NSA_HEADS = 8
NSA_KV_HEADS = 2
NSA_REP = NSA_HEADS // NSA_KV_HEADS
CMP_LEN = 32
CMP_STRIDE = 16
CMP_HIDDEN = 256
SLC_LEN = 64
SLC_TOPN = 16
NSA_WINDOW = 512
FORCE_SCORE = 1e4
SB_HEADS = 4
SWA_HEADS = 4
SWA_KV_HEADS = 2
SWA_REP = SWA_HEADS // SWA_KV_HEADS
SWA_WINDOW = 128
NUM_BUCKETS = 32
MAX_DISTANCE = 128
D_FF = 2816
NORM_EPS = 1e-6
NEG_INF = -1e30
N_BRANCHES = 3
N_GATE = 3 * NSA_HEADS

LANES = 128
VMEM_LIMIT = 56 * 1024 * 1024
BF16 = jnp.bfloat16
F32 = jnp.float32

_sizes = (NSA_HEADS * HEAD_DIM,) + (NSA_KV_HEADS * HEAD_DIM,) * 6 + (N_GATE,) \
    + (SB_HEADS * HEAD_DIM,) * 3 + (SWA_HEADS * HEAD_DIM, SWA_KV_HEADS * HEAD_DIM,
                                    SWA_KV_HEADS * HEAD_DIM, N_BRANCHES * D_MODEL)
_offs = np.concatenate([[0], np.cumsum(_sizes)])
(OFF_NQ, OFF_NKC, OFF_NVC, OFF_NKS, OFF_NVS, OFF_NKW, OFF_NVW, OFF_NGATE,
 OFF_SQ, OFF_SK, OFF_SV, OFF_WQ, OFF_WK, OFF_WV, OFF_BGATE) = (int(o) for o in _offs[:-1])

ROW_HEADS = ([("nkc", OFF_NKC + 64 * g) for g in range(2)] + [("nvc", OFF_NVC + 64 * g) for g in range(2)]
             + [("nks", OFF_NKS + 64 * g) for g in range(2)] + [("nkw", OFF_NKW + 64 * g) for g in range(2)]
             + [("sq", OFF_SQ + 64 * h) for h in range(4)] + [("sv", OFF_SV + 64 * h) for h in range(4)]
             + [("wk", OFF_WK + 64 * g) for g in range(2)])
ROW_KC, ROW_VC, ROW_KS, ROW_KW, ROW_SQ, ROW_SV, ROW_WK = 0, 2, 4, 6, 8, 12, 16
COL_HEADS = ([("nq", OFF_NQ + 64 * h) for h in range(8)] + [("nvs", OFF_NVS + 64 * g) for g in range(2)]
             + [("nvw", OFF_NVW + 64 * g) for g in range(2)] + [("sk", OFF_SK + 64 * h) for h in range(4)]
             + [("wq", OFF_WQ + 64 * h) for h in range(4)] + [("wv", OFF_WV + 64 * g) for g in range(2)])
COL_NQ, COL_VS, COL_VW, COL_SK, COL_WQ, COL_WV = 0, 8, 10, 12, 16, 20


def _dot(a, b):
    return jnp.dot(a, b, preferred_element_type=F32)


def _rms_rows(x, w):
    ms = jnp.mean(x * x, axis=-1, keepdims=True)
    return x * lax.rsqrt(ms + NORM_EPS) * w


def _cparams(sem, flags=None):
    return pltpu.CompilerParams(dimension_semantics=sem, vmem_limit_bytes=VMEM_LIMIT, flags=flags)


FFN_TM = 512
MXU_TILE = 256
FFN_FC = MXU_TILE


def _ffn_kernel(x_ref, nw_ref, wg_ref, wu_ref, wd_ref, o_ref):
    x = x_ref[...]
    h = _rms_rows(x, nw_ref[...]).astype(BF16)
    acc = jnp.zeros(x.shape, F32)
    for c in range(D_FF // FFN_FC):
        cols = slice(c * FFN_FC, (c + 1) * FFN_FC)
        g = _dot(h, wg_ref[:, cols])
        u = _dot(h, wu_ref[:, cols])
        a = (g * jax.nn.sigmoid(g)) * u
        acc = acc + _dot(a.astype(BF16), wd_ref[cols, :])
    o_ref[...] = x + 0.5 * acc


def _ffn(x2, nw, wg, wu, wd):
    T = x2.shape[0]
    resident = lambda shape: pl.BlockSpec(shape, lambda i: (0, 0), pipeline_mode=pl.Buffered(1))
    return pl.pallas_call(
        _ffn_kernel,
        grid=(T // FFN_TM,),
        in_specs=[
            pl.BlockSpec((FFN_TM, D_MODEL), lambda i: (i, 0)),
            pl.BlockSpec((1, D_MODEL), lambda i: (0, 0)),
            resident((D_MODEL, D_FF)), resident((D_MODEL, D_FF)), resident((D_FF, D_MODEL)),
        ],
        out_specs=pl.BlockSpec((FFN_TM, D_MODEL), lambda i: (i, 0)),
        out_shape=jax.ShapeDtypeStruct((T, D_MODEL), F32),
        compiler_params=_cparams(("parallel",)),
        name="ffn",
    )(x2, nw.reshape(1, D_MODEL), wg.astype(BF16), wu.astype(BF16), wd.astype(BF16))


PROJ_TM = 512


GATE_ROWS = 32

_NORMED = ("nq", "nks", "nkw", "wq", "wk")


def _proj_plan(scale_of):
    plan = []
    for kind, heads in (("row", ROW_HEADS), ("col", COL_HEADS)):
        for p in range(len(heads) // 2):
            name = heads[2 * p][0]
            assert heads[2 * p + 1][0] == name
            plan.append((kind, p, name in _NORMED, 1.0 if name in _NORMED else scale_of.get(name, 1.0)))
    plan.append(("gate", 0, False, 1.0))
    return tuple(plan)


def _proj_kernel(x_ref, nw_ref, w_ref, gain_ref, bd_ref, row_ref, col_ref, gate_ref, *, plan):
    h = _rms_rows(x_ref[...], nw_ref[...]).astype(BF16)
    pair = 2 * LANES
    for c0 in range(0, len(plan) * LANES, pair):
        width = min(pair, len(plan) * LANES - c0)
        zg = _dot(h, w_ref[:, c0:c0 + width])
        for u in range(width // LANES):
            t = c0 // LANES + u
            kind, p, normed, scale = plan[t]
            z = zg[:, u * LANES:(u + 1) * LANES]
            if normed:
                ms = _dot((z * z).astype(BF16), bd_ref[...])
                z = z * lax.rsqrt(ms + NORM_EPS) * gain_ref[:, t * LANES:(t + 1) * LANES]
            elif scale != 1.0:
                z = z * scale
            if kind == "row":
                row_ref[2 * p] = z[:, :HEAD_DIM].astype(row_ref.dtype)
                row_ref[2 * p + 1] = z[:, HEAD_DIM:].astype(row_ref.dtype)
            elif kind == "col":
                z_t = z.T
                for c in range(PROJ_TM // LANES):
                    col_ref[2 * p, c] = z_t[:HEAD_DIM, c * LANES:(c + 1) * LANES].astype(col_ref.dtype)
                    col_ref[2 * p + 1, c] = z_t[HEAD_DIM:, c * LANES:(c + 1) * LANES].astype(col_ref.dtype)
            else:
                gate_ref[...] = jax.nn.sigmoid(z).T[:GATE_ROWS]


def _proj(x2, nw, w, gain, plan):
    T = x2.shape[0]
    n_row, n_col = len(ROW_HEADS), len(COL_HEADS)
    bd = np.kron(np.eye(2, dtype=np.float32), np.full((HEAD_DIM, HEAD_DIM), 1.0 / HEAD_DIM, np.float32))
    return pl.pallas_call(
        functools.partial(_proj_kernel, plan=plan),
        grid=(T // PROJ_TM,),
        in_specs=[
            pl.BlockSpec((PROJ_TM, D_MODEL), lambda i: (i, 0)),
            pl.BlockSpec((1, D_MODEL), lambda i: (0, 0)),
            pl.BlockSpec(w.shape, lambda i: (0, 0)),
            pl.BlockSpec(gain.shape, lambda i: (0, 0)),
            pl.BlockSpec((LANES, LANES), lambda i: (0, 0)),
        ],
        out_specs=[
            pl.BlockSpec((n_row, PROJ_TM, HEAD_DIM), lambda i: (0, i, 0)),
            pl.BlockSpec((n_col, PROJ_TM // LANES, HEAD_DIM, LANES), lambda i: (0, i, 0, 0)),
            pl.BlockSpec((GATE_ROWS, PROJ_TM), lambda i: (0, i)),
        ],
        out_shape=[jax.ShapeDtypeStruct((n_row, T, HEAD_DIM), BF16),
                   jax.ShapeDtypeStruct((n_col, T // LANES, HEAD_DIM, LANES), BF16),
                   jax.ShapeDtypeStruct((GATE_ROWS, T), F32)],
        compiler_params=_cparams(("parallel",)),
        name="proj",
    )(x2, nw.reshape(1, D_MODEL), w, gain, jnp.asarray(bd, BF16))


def _compress_kernel(r_ref, w1_ref, w2_ref, pos_ref, kn_ref, row_ref, col_ref):
    kind = pl.program_id(0) // NSA_KV_HEADS
    r = r_ref[0, 0]
    nc = r.shape[0]
    half = CMP_STRIDE * HEAD_DIM
    w1 = w1_ref[0]
    p_lo = _dot(r, w1[:half])
    p_hi = _dot(r, w1[half:])
    p_pos = _dot(jnp.broadcast_to(pos_ref[...], (8, CMP_LEN * HEAD_DIM)).astype(BF16), w1)[:1]
    hdn = p_lo + pltpu.roll(p_hi, nc - 1, 0) + p_pos
    hdn = hdn * jax.nn.sigmoid(hdn)
    out = _dot(hdn.astype(BF16), w2_ref[0])
    ms = jnp.sum(out * out, axis=-1, keepdims=True) * (1.0 / HEAD_DIM)
    normed = out * lax.rsqrt(ms + NORM_EPS) * kn_ref[...]
    out = jnp.where(kind == 0, normed, out)
    row_ref[0, 0] = out[:, :HEAD_DIM].astype(row_ref.dtype)
    col_ref[0, 0] = out.T[:HEAD_DIM].astype(col_ref.dtype)


def _compress(rows4, w1, w2, pos, k_norm):
    assert ROW_KC == 0 and ROW_VC == NSA_KV_HEADS
    _, B, NC, _ = rows4.shape
    kn = jnp.pad(k_norm.reshape(1, HEAD_DIM), ((0, 0), (0, LANES - HEAD_DIM)))
    w2p = jnp.pad(w2, ((0, 0), (0, 0), (0, LANES - HEAD_DIM))).astype(BF16)
    return pl.pallas_call(
        _compress_kernel,
        grid=(4, B),
        in_specs=[
            pl.BlockSpec((1, 1, NC, CMP_STRIDE * HEAD_DIM), lambda h, b: (h, b, 0, 0)),
            pl.BlockSpec((1, CMP_LEN * HEAD_DIM, CMP_HIDDEN), lambda h, b: (h // NSA_KV_HEADS, 0, 0)),
            pl.BlockSpec((1, CMP_HIDDEN, LANES), lambda h, b: (h // NSA_KV_HEADS, 0, 0)),
            pl.BlockSpec((1, CMP_LEN * HEAD_DIM), lambda h, b: (0, 0)),
            pl.BlockSpec((1, LANES), lambda h, b: (0, 0)),
        ],
        out_specs=[
            pl.BlockSpec((1, 1, NC, HEAD_DIM), lambda h, b: (h, b, 0, 0)),
            pl.BlockSpec((1, 1, HEAD_DIM, NC), lambda h, b: (h, b, 0, 0)),
        ],
        out_shape=[jax.ShapeDtypeStruct((4, B, NC, HEAD_DIM), BF16),
                   jax.ShapeDtypeStruct((4, B, HEAD_DIM, NC), BF16)],
        compiler_params=_cparams(("parallel", "parallel")),
        name="nsa_compress",
    )(rows4, w1.astype(BF16), w2p, pos.reshape(1, CMP_LEN * HEAD_DIM), kn)


def _softmax_tile_update(s, v_t, m, l, acc):
    m_new = jnp.maximum(m, jnp.max(s, axis=0, keepdims=True))
    alpha = jnp.exp2(m - m_new)
    p = jnp.exp2(s - m_new)
    l_new = alpha * l + jnp.sum(p, axis=0, keepdims=True)
    acc_new = alpha * acc + _dot(v_t, p.astype(BF16))
    return m_new, l_new, acc_new


def _pairs_to_rows(o_t):
    n = o_t.shape[1] // LANES
    outs = []
    for p in range(n // 2):
        a = o_t[:, (2 * p) * LANES:(2 * p + 1) * LANES]
        b = o_t[:, (2 * p + 1) * LANES:(2 * p + 2) * LANES]
        outs.append(jnp.concatenate([a, b], axis=0).T)
    return outs[0] if len(outs) == 1 else jnp.concatenate(outs, axis=1)


FAR_TILES = 4
NSA_SUB = 2
LOG2E = math.log2(math.e)
NSA_TINY = 2.0 ** -100
BOUND_SLACK = 1.05


def _nsa_kernel(q_ref, kc_ref, vc_ref, ks_ref, vs_ref, kw_ref, vw_ref, gate_ref,
                cbias_ref, sbias_ref, wbias_ref, ov_ref, msel_ref, o_ref, *scratch, nb):
    neg_refs, stage_refs = scratch[:NSA_SUB], scratch[NSA_SUB:]
    refs = (q_ref, kc_ref, vc_ref, ks_ref, vs_ref, kw_ref, vw_ref, gate_ref, cbias_ref, sbias_ref, wbias_ref,
            ov_ref, msel_ref)
    finish = [_nsa_block(u, refs, neg_refs[u], stage_refs, nb) for u in range(NSA_SUB)]
    for u in range(NSA_SUB):
        o_ref[0, u * Q_BLOCK:(u + 1) * Q_BLOCK, :] = finish[u]().astype(o_ref.dtype)


def _nsa_block(u, refs, neg_ref, stage_refs, nb):
    (q_ref, kc_ref, vc_ref, ks_ref, vs_ref, kw_ref, vw_ref, gate_ref, cbias_ref, sbias_ref, wbias_ref,
     ov_ref, msel_ref) = refs
    i = pl.program_id(2) * NSA_SUB + u
    R = NSA_REP
    N = R * Q_BLOCK
    nc = kc_ref.shape[2]
    q_t = jnp.concatenate([q_ref[r, u] for r in range(R)], axis=1)
    n_win = NSA_WINDOW // Q_BLOCK

    def k_rows(ref, J0, n_tiles):
        return ref[0, pl.ds(pl.multiple_of(J0 * Q_BLOCK, Q_BLOCK), n_tiles * Q_BLOCK), :]

    def v_cols(ref, J0, n_tiles):
        return jnp.concatenate([ref[0, J0 + u] for u in range(n_tiles)], axis=1)

    def sel_mask(J0, n_tiles):
        parts = [jnp.broadcast_to(neg_ref[pl.ds(2 * J0 + u, 1), :], (SLC_LEN, Q_BLOCK)) for u in range(2 * n_tiles)]
        return jnp.concatenate([jnp.concatenate(parts, axis=0)] * R, axis=1)

    def compressed(fast):
        cb = cbias_ref[0, pl.ds(pl.multiple_of(8 * (nb - 1) - 8 * i, 8), nc), :]
        s = _dot(kc_ref[0, 0], q_t) + cb
        if fast:
            e = jnp.exp2(s)
        else:
            e = jnp.where(s > 0.5 * NEG_INF, jnp.exp2(s - jnp.max(s, axis=0, keepdims=True)), 0.0)
        l = jnp.sum(e, axis=0, keepdims=True)
        p = e * jnp.where(l > 0.0, 1.0 / l, 0.0)
        return p, l, _dot(vc_ref[0, 0], p.astype(BF16))

    def select_blocks(p_c):
        p_sum = p_c[:, 0:Q_BLOCK]
        for r in range(1, R):
            p_sum = p_sum + p_c[:, r * Q_BLOCK:(r + 1) * Q_BLOCK]
        p_hi = p_sum.astype(BF16)
        p_lo = (p_sum - p_hi.astype(F32)).astype(BF16)
        imp = _dot(ov_ref[...], p_hi) + _dot(ov_ref[...], p_lo)
        n_slc = imp.shape[0]
        jj = lax.broadcasted_iota(jnp.int32, imp.shape, 0)
        qq = lax.broadcasted_iota(jnp.int32, imp.shape, 1)
        cur = 2 * i + jnp.where(qq >= SLC_LEN, 1, 0)
        forced = (jj == 0) | (jj == cur) | (jj == cur - 1)
        score = jnp.where(forced, FORCE_SCORE, jnp.where(jj <= cur, imp, -1.0))
        sel = jnp.zeros(imp.shape, F32)
        for _ in range(min(SLC_TOPN, n_slc)):
            best = jnp.max(score, axis=0, keepdims=True)
            first = jnp.min(jnp.where(score == best, jj, n_slc), axis=0, keepdims=True)
            hit = jj == first
            sel = jnp.where(hit, 1.0, sel)
            score = jnp.where(hit, -2.0, score)
        neg_ref[...] = jnp.where(sel > 0.0, msel_ref[0], NEG_INF)

    def add_chunk(st, s, v_t):
        e = jnp.exp2(s)
        return st[0] + jnp.sum(e, axis=0, keepdims=True), st[1] + _dot(v_t, e.astype(BF16))

    zero = (jnp.zeros((1, N), F32), jnp.zeros((HEAD_DIM, N), F32))

    def selected_near():
        J0 = jnp.maximum(i - 1, 0)
        bias = sbias_ref[0, pl.ds(pl.multiple_of((2 - (i - J0)) * Q_BLOCK, Q_BLOCK), 2 * Q_BLOCK), :]
        return add_chunk(zero, _dot(k_rows(ks_ref, J0, 2), q_t) + bias + sel_mask(J0, 2), v_cols(vs_ref, J0, 2))

    def selected_far(st):
        n_far = jnp.maximum(i - 1, 0)

        def far_chunk(J0, st, n_valid):
            mask = sel_mask(J0, FAR_TILES)
            if n_valid is not None:
                row = lax.broadcasted_iota(jnp.int32, mask.shape, 0)
                mask = jnp.where(row < n_valid * Q_BLOCK, mask, NEG_INF)
            return add_chunk(st, _dot(k_rows(ks_ref, J0, FAR_TILES), q_t) + mask, v_cols(vs_ref, J0, FAR_TILES))

        n_chunks = n_far // FAR_TILES
        rem = n_far - FAR_TILES * n_chunks

        def tile_of(c):
            return jnp.clip(n_far - FAR_TILES * (c + 1), 0, nb - FAR_TILES)

        def scores(c):
            return _dot(k_rows(ks_ref, tile_of(c), FAR_TILES), q_t)

        def values(c, e_ref, acc):
            return acc + _dot(v_cols(vs_ref, tile_of(c), FAR_TILES), e_ref[...])

        def stage(c, st, s_in, s_out, e_in, e_out):
            s_out[...] = scores(c + 1)
            acc = values(c - 1, e_in, st[1])
            e = jnp.exp2(s_in[...] + sel_mask(tile_of(c), FAR_TILES))
            e_out[...] = e.astype(BF16)
            return st[0] + jnp.sum(e, axis=0, keepdims=True), acc

        s_a, s_b, e_a, e_b = stage_refs
        e_b[...] = jnp.zeros_like(e_b)
        s_a[...] = scores(0)

        def pair(c, st):
            st = stage(c, st, s_a, s_b, e_b, e_a)
            return stage(c + 1, st, s_b, s_a, e_a, e_b)

        n_quads = n_chunks // 4
        st = lax.fori_loop(0, n_quads, lambda t, st: pair(4 * t + 2, pair(4 * t, st)), st)
        st = lax.cond(n_chunks % 4 >= 2, lambda st: pair(4 * n_quads, st), lambda st: st, st)

        def odd_tail(st):
            st = stage(n_chunks - 1, st, s_a, s_b, e_b, e_a)
            return st[0], values(n_chunks - 1, e_a, st[1])

        st = lax.cond(n_chunks % 2 == 1, odd_tail, lambda st: (st[0], values(n_chunks - 1, e_b, st[1])), st)
        return lax.cond(rem > 0, lambda st: far_chunk(0, st, rem), lambda st: st, st)

    def window_fast():
        J0 = jnp.maximum(i - n_win, 0)
        bias = wbias_ref[0, pl.ds(pl.multiple_of((n_win - (i - J0)) * Q_BLOCK, Q_BLOCK), (n_win + 1) * Q_BLOCK), :]
        return add_chunk(zero, _dot(k_rows(kw_ref, J0, n_win + 1), q_t) + bias, v_cols(vw_ref, J0, n_win + 1))

    def sweep_exact(k_ref, v_ref, bias_ref, far_dist, n_back, with_sel):
        def body(t, st):
            J = i - t
            off = pl.multiple_of((far_dist - jnp.minimum(t, far_dist)) * Q_BLOCK, Q_BLOCK)
            s = _dot(k_rows(k_ref, J, 1), q_t) + bias_ref[0, pl.ds(off, Q_BLOCK), :]
            if with_sel:
                s = s + sel_mask(J, 1)
            return _softmax_tile_update(s, v_ref[0, J], *st)

        init = (jnp.full((1, N), NEG_INF, F32),) + zero
        _, l, acc = lax.fori_loop(0, jnp.minimum(i, n_back) + 1, body, init)
        return l, acc

    p_c, l_c, o_c = compressed(True)
    select_blocks(p_c)
    l_w, a_w = window_fast()
    near = selected_near()

    def exact_path():
        p_x, _, o_x = compressed(False)
        select_blocks(p_x)
        l_sx, a_sx = sweep_exact(ks_ref, vs_ref, sbias_ref, 2, nb, True)
        l_wx, a_wx = sweep_exact(kw_ref, vw_ref, wbias_ref, n_win, n_win, False)
        return o_x, a_sx * (1.0 / l_sx), a_wx * (1.0 / l_wx)

    def gate_row(br):
        return jnp.concatenate([gate_ref[br, 0, r:r + 1, u * Q_BLOCK:(u + 1) * Q_BLOCK] for r in range(R)], axis=1)

    def finish():
        l_s, a_s = selected_far(near)
        lane_q = lax.broadcasted_iota(jnp.int32, (1, N), 1) % Q_BLOCK
        has_cmp = (i > 0) | (lane_q >= CMP_LEN - 1)
        l_min = jnp.minimum(jnp.minimum(l_s, l_w), jnp.where(has_cmp, l_c, 1.0))
        o_cx, o_s, o_w = lax.cond(jnp.min(l_min) > NSA_TINY,
                                  lambda: (o_c, a_s * (1.0 / l_s), a_w * (1.0 / l_w)), exact_path)
        o_t = gate_row(0) * o_cx + gate_row(1) * o_s + gate_row(2) * o_w
        return _pairs_to_rows(o_t)

    return finish


def _nsa_attention(colmat, rowmat, cmp_rows, cmp_cols, gates, cbias, sbias, wbias, ov_t, msel, B, S):
    nb = S // Q_BLOCK
    nc = S // CMP_STRIDE
    n_slc = S // SLC_LEN
    G, R = NSA_KV_HEADS, NSA_REP
    kern = functools.partial(_nsa_kernel, nb=nb)
    ns = nb // NSA_SUB
    return pl.pallas_call(
        kern,
        grid=(B, G, ns),
        in_specs=[
            pl.BlockSpec((R, NSA_SUB, HEAD_DIM, LANES), lambda b, g, i: (g, b * ns + i, 0, 0)),
            pl.BlockSpec((1, 1, nc, HEAD_DIM), lambda b, g, i: (g, b, 0, 0)),
            pl.BlockSpec((1, 1, HEAD_DIM, nc), lambda b, g, i: (NSA_KV_HEADS + g, b, 0, 0)),
            pl.BlockSpec((1, S, HEAD_DIM), lambda b, g, i: (ROW_KS + g, b, 0)),
            pl.BlockSpec((1, nb, HEAD_DIM, LANES), lambda b, g, i: (COL_VS + g, b, 0, 0)),
            pl.BlockSpec((1, S, HEAD_DIM), lambda b, g, i: (ROW_KW + g, b, 0)),
            pl.BlockSpec((1, nb, HEAD_DIM, LANES), lambda b, g, i: (COL_VW + g, b, 0, 0)),
            pl.BlockSpec((N_BRANCHES, 1, R, NSA_SUB * LANES), lambda b, g, i: (0, g, 0, b * ns + i)),
            pl.BlockSpec((1, cbias.shape[1], R * Q_BLOCK), lambda b, g, i: (g, 0, 0)),
            pl.BlockSpec((1, sbias.shape[1], R * Q_BLOCK), lambda b, g, i: (g, 0, 0)),
            pl.BlockSpec((1, wbias.shape[1], R * Q_BLOCK), lambda b, g, i: (g, 0, 0)),
            pl.BlockSpec((n_slc, nc), lambda b, g, i: (0, 0)),
            pl.BlockSpec((1, 1, Q_BLOCK), lambda b, g, i: (g, 0, 0)),
        ],
        out_specs=pl.BlockSpec((1, NSA_SUB * Q_BLOCK, R * HEAD_DIM), lambda b, g, i: (b, i, g)),
        out_shape=jax.ShapeDtypeStruct((B, S, NSA_HEADS * HEAD_DIM), BF16),
        scratch_shapes=[pltpu.VMEM((n_slc, Q_BLOCK), F32)] * NSA_SUB
        + [pltpu.VMEM((FAR_TILES * Q_BLOCK, R * Q_BLOCK), F32)] * 2
        + [pltpu.VMEM((FAR_TILES * Q_BLOCK, R * Q_BLOCK), BF16)] * 2,
        compiler_params=_cparams(("parallel", "parallel", "arbitrary")),
        name="nsa_attention",
    )(colmat, cmp_rows, cmp_cols, rowmat, colmat, rowmat, colmat, gates, cbias, sbias, wbias, ov_t, msel)


SB_QB = 512
SB_KT = 256
SB_ZERO_LOG = -110.0


def _sb_kernel(q_ref, k_ref, v_ref, tri_ref, o_ref):
    i = pl.program_id(1)
    H = SB_HEADS
    tri = tri_ref[...]
    rows = lax.broadcasted_iota(jnp.int32, (SB_QB, SB_KT), 0)
    cols = lax.broadcasted_iota(jnp.int32, (SB_QB, SB_KT), 1)
    key_minus_query = cols - rows
    per_q = SB_QB // SB_KT

    def step(J, st, masked):
        out = []
        for h in range(H):
            carry, acc = st[h]
            k_t = jnp.concatenate([k_ref[h, (SB_KT // LANES) * J + c] for c in range(SB_KT // LANES)], axis=1)
            z = _dot(q_ref[h], k_t)
            sp = jnp.maximum(z, 0.0) + jnp.log(1.0 + jnp.exp(-jnp.abs(z)))
            if masked:
                valid = key_minus_query < i * SB_QB - J * SB_KT
                lf = jnp.where(valid, -sp, 0.0)
            else:
                lf = -sp
            lf_hi = lf.astype(BF16)
            lf_lo = (lf - lf_hi.astype(F32)).astype(BF16)
            after = _dot(lf_hi, tri) + _dot(lf_lo, tri) + carry
            w = jnp.exp(z - sp + after)
            if masked:
                w = jnp.where(valid, w, 0.0)
            v = v_ref[h, pl.ds(pl.multiple_of(J * SB_KT, SB_KT), SB_KT), :]
            acc = acc + _dot(w.astype(BF16), v)
            carry = carry + jnp.sum(lf, axis=1, keepdims=True)
            out.append((carry, acc))
        return tuple(out)

    st = tuple((jnp.zeros((SB_QB, 1), F32), jnp.zeros((SB_QB, HEAD_DIM), F32)) for _ in range(H))
    for d in range(per_q):
        st = step(per_q * i + per_q - 1 - d, st, True)

    def alive(st):
        worst = st[0][0]
        for h in range(1, H):
            worst = jnp.maximum(worst, st[h][0])
        return jnp.max(worst) > SB_ZERO_LOG

    def body(c):
        J, _, st = c
        st = step(J, st, False)
        return J - 1, alive(st), st

    _, _, st = lax.while_loop(lambda c: (c[0] >= 0) & c[1], body, (per_q * i - 1, alive(st), st))
    o_ref[0] = jnp.concatenate([st[h][1] for h in range(H)], axis=1).astype(o_ref.dtype)


def _sb_attention(colmat, rowmat, tri, B, S):
    nq = S // SB_QB
    nb = S // LANES
    H = SB_HEADS
    return pl.pallas_call(
        _sb_kernel,
        grid=(B, nq),
        in_specs=[
            pl.BlockSpec((H, SB_QB, HEAD_DIM), lambda b, i: (ROW_SQ // H, b * nq + i, 0)),
            pl.BlockSpec((H, nb, HEAD_DIM, LANES), lambda b, i: (COL_SK // H, b, 0, 0)),
            pl.BlockSpec((H, S, HEAD_DIM), lambda b, i: (ROW_SV // H, b, 0)),
            pl.BlockSpec((SB_KT, SB_KT), lambda b, i: (0, 0)),
        ],
        out_specs=pl.BlockSpec((1, SB_QB, H * HEAD_DIM), lambda b, i: (b, i, 0)),
        out_shape=jax.ShapeDtypeStruct((B, S, H * HEAD_DIM), BF16),
        compiler_params=_cparams(("parallel", "arbitrary")),
        name="sb_attention",
    )(rowmat, colmat, rowmat, tri)


SWA_SUB = 4


def _swa_kernel(q_ref, k_ref, v_ref, bias_ref, sink_ref, o_ref):
    R = SWA_REP
    sink = sink_ref[0]
    for u in range(SWA_SUB):
        i = pl.program_id(2) * SWA_SUB + u
        q_t = jnp.concatenate([q_ref[r, u] for r in range(R)], axis=1)
        J0 = jnp.maximum(i - 1, 0)
        k = k_ref[0, pl.ds(pl.multiple_of(J0 * Q_BLOCK, Q_BLOCK), 2 * Q_BLOCK), :]
        bias = bias_ref[0, pl.ds(pl.multiple_of((1 - (i - J0)) * Q_BLOCK, Q_BLOCK), 2 * Q_BLOCK), :]
        s = _dot(k, q_t) + bias
        m = jnp.maximum(jnp.max(s, axis=0, keepdims=True), sink)
        e = jnp.exp(s - m)
        den = jnp.sum(e, axis=0, keepdims=True) + jnp.exp(sink - m)
        v_t = jnp.concatenate([v_ref[0, J0], v_ref[0, J0 + 1]], axis=1)
        o_t = _dot(v_t, e.astype(BF16)) * (1.0 / den)
        o_ref[0, u * Q_BLOCK:(u + 1) * Q_BLOCK, :] = _pairs_to_rows(o_t).astype(o_ref.dtype)


def _swa_attention(colmat, rowmat, bias, sinks, B, S):
    nb = S // Q_BLOCK
    ns = nb // SWA_SUB
    G, R = SWA_KV_HEADS, SWA_REP
    return pl.pallas_call(
        _swa_kernel,
        grid=(B, G, ns),
        in_specs=[
            pl.BlockSpec((R, SWA_SUB, HEAD_DIM, LANES), lambda b, g, i: (COL_WQ // R + g, b * ns + i, 0, 0)),
            pl.BlockSpec((1, S, HEAD_DIM), lambda b, g, i: (ROW_WK + g, b, 0)),
            pl.BlockSpec((1, nb, HEAD_DIM, LANES), lambda b, g, i: (COL_WV + g, b, 0, 0)),
            pl.BlockSpec((1, 3 * Q_BLOCK, R * Q_BLOCK), lambda b, g, i: (g, 0, 0)),
            pl.BlockSpec((1, 1, R * Q_BLOCK), lambda b, g, i: (g, 0, 0)),
        ],
        out_specs=pl.BlockSpec((1, SWA_SUB * Q_BLOCK, R * HEAD_DIM), lambda b, g, i: (b, i, g)),
        out_shape=jax.ShapeDtypeStruct((B, S, SWA_HEADS * HEAD_DIM), BF16),
        compiler_params=_cparams(("parallel", "parallel", "arbitrary")),
        name="swa_attention",
    )(colmat, rowmat, colmat, bias, sinks)


MERGE_TM = 512


def _merge_kernel(x_ref, nw_ref, wg_ref, ya_ref, yb_ref, yc_ref, ua_ref, ub_ref, uc_ref, wo_ref, o_ref):
    x = x_ref[...]
    h = _rms_rows(x, nw_ref[...]).astype(BF16)
    merged = None
    for br, (y_ref, u_ref) in enumerate(((ya_ref, ua_ref), (yb_ref, ub_ref), (yc_ref, uc_ref))):
        g = jax.nn.sigmoid(_dot(h, wg_ref[:, br * D_MODEL:(br + 1) * D_MODEL]))
        t = g * _dot(y_ref[...], u_ref[...])
        merged = t if merged is None else merged + t
    o_ref[...] = x + _dot(merged.astype(BF16), wo_ref[...])


def _merge(x2, nw, w_bgate, y_nsa, y_sb, y_swa, u_nsa, u_sb, u_swa, w_out):
    T = x2.shape[0]
    full = lambda shape: pl.BlockSpec(shape, lambda i: (0, 0))
    rows = lambda n: pl.BlockSpec((MERGE_TM, n), lambda i: (i, 0))
    return pl.pallas_call(
        _merge_kernel,
        grid=(T // MERGE_TM,),
        in_specs=[rows(D_MODEL), full((1, D_MODEL)), full((D_MODEL, N_BRANCHES * D_MODEL)),
                  rows(y_nsa.shape[1]), rows(y_sb.shape[1]), rows(y_swa.shape[1]),
                  full(u_nsa.shape), full(u_sb.shape), full(u_swa.shape), full((D_MODEL, D_MODEL))],
        out_specs=rows(D_MODEL),
        out_shape=jax.ShapeDtypeStruct((T, D_MODEL), F32),
        compiler_params=_cparams(("parallel",)),
        name="merge",
    )(x2, nw.reshape(1, D_MODEL), w_bgate.astype(BF16), y_nsa, y_sb, y_swa,
      u_nsa.astype(BF16), u_sb.astype(BF16), u_swa.astype(BF16), w_out.astype(BF16))


def _t5_bucket(dist):
    max_exact = NUM_BUCKETS // 2
    d = jnp.maximum(dist, 0)
    df = jnp.maximum(d, 1).astype(F32)
    large = max_exact + (jnp.log(df / max_exact) / math.log(MAX_DISTANCE / max_exact)
                         * (NUM_BUCKETS - max_exact)).astype(jnp.int32)
    large = jnp.minimum(large, NUM_BUCKETS - 1)
    return jnp.where(d < max_exact, d, large)


def _toeplitz(ext, delta):
    H = ext.shape[0]
    n = Q_BLOCK
    c = delta + MAX_DISTANCE
    w = ext[:, c - (n - 1):c + n]
    flat = jnp.pad(jnp.broadcast_to(w[:, None, :], (H, n, 2 * n - 1)), ((0, 0), (0, 0), (0, 1))).reshape(H, 2 * n * n)
    return flat[:, n - 1:n - 1 + n * (2 * n - 1)].reshape(H, n, 2 * n - 1)[:, :, :n]


def _lanes(t, G, R):
    K, Q = t.shape[1:]
    return t.reshape(G, R, K, Q).transpose(0, 2, 1, 3).reshape(G, K, R * Q)


def _bias_tables(rel_bias, nb, qk_bound):
    n = Q_BLOCK
    G, R = NSA_KV_HEADS, NSA_REP
    by_dist = rel_bias[_t5_bucket(jnp.arange(MAX_DISTANCE)), :].T.astype(F32)
    far = rel_bias[NUM_BUCKETS - 1, :].astype(F32)
    neg = jnp.full((1, n), NEG_INF, F32)
    k = jnp.arange(n)[:, None]
    q = jnp.arange(n)[None, :]
    above = (k > q)[None]

    rel = (by_dist[:NSA_HEADS] - far[:NSA_HEADS, None]) * LOG2E
    m_g = qk_bound + jnp.maximum(jnp.max(rel.reshape(G, -1), axis=1), 0.0)
    m_h = jnp.repeat(m_g, R)[:, None, None]
    ext = jnp.concatenate([jnp.broadcast_to(neg, (NSA_HEADS, n)), rel, jnp.zeros((NSA_HEADS, n), F32)], axis=1)
    t0, t1 = _toeplitz(ext, 0), _toeplitz(ext, n)
    zeros = jnp.zeros_like(t0)
    masked = jnp.full_like(t0, NEG_INF)
    sbias = _lanes(jnp.concatenate([zeros, t1, t0, masked], axis=1), G, R)
    n_win = NSA_WINDOW // n
    edge = jnp.where(above, zeros, NEG_INF)
    tiles = [edge] + [zeros] * (n_win - 2) + [t1, t0]
    wbias = _lanes(jnp.concatenate([t - m_h for t in tiles] + [masked] * n_win, axis=1), G, R)
    off = 8 * (nb - 1)
    near = jnp.stack([ext[:, 97 - CMP_STRIDE * m:97 - CMP_STRIDE * m + n] for m in range(-9, 7)], axis=1)
    n_rows = 16 * nb - 8
    cb = jnp.concatenate([jnp.zeros((NSA_HEADS, off - 9, n), F32), near,
                          jnp.full((NSA_HEADS, n_rows - off - 7, n), NEG_INF, F32)], axis=1) - m_h
    cbias = _lanes(cb, G, R)
    msel = jnp.broadcast_to(-m_g[:, None, None], (G, 1, n))

    hs = slice(NSA_HEADS, NSA_HEADS + SWA_HEADS)
    ext_w = jnp.concatenate([jnp.broadcast_to(neg, (SWA_HEADS, n)), by_dist[hs],
                             jnp.broadcast_to(far[hs, None], (SWA_HEADS, n))], axis=1)
    s0 = _toeplitz(ext_w, 0)
    s1 = jnp.where(above, _toeplitz(ext_w, n), NEG_INF)
    swa_bias = _lanes(jnp.concatenate([s1, s0, jnp.full_like(s0, NEG_INF)], axis=1), SWA_KV_HEADS, SWA_REP)
    return cbias, sbias, wbias, msel, swa_bias


def _overlap_t(S):
    nc, n_slc = S // CMP_STRIDE, S // SLC_LEN
    n = np.arange(nc)[None, :]
    j = np.arange(n_slc)[:, None]
    ov = (n * CMP_STRIDE < (j + 1) * SLC_LEN) & (n * CMP_STRIDE + CMP_LEN - 1 >= j * SLC_LEN) & (n < nc - 1)
    return jnp.asarray(ov, BF16)


def _head_cols(w, heads):
    return jnp.concatenate([w[:, off:off + HEAD_DIM] for _, off in heads], axis=1)


def _head_gains(heads, norm_of, scale_of):
    ones = jnp.ones((HEAD_DIM,), F32)
    gains = [norm_of[name].astype(F32) * scale_of.get(name, 1.0) if name in norm_of else ones for name, _ in heads]
    return jnp.concatenate(gains).reshape(1, -1)


def kernel(x, rel_bias, ffn1_norm, ffn1_w_gate, ffn1_w_up, ffn1_w_down, mix_norm, w_in, nsa_q_norm, nsa_k_norm, nsa_cmp_pos, nsa_cmp_k_w1, nsa_cmp_k_w2, nsa_cmp_v_w1, nsa_cmp_v_w2, swa_q_norm, swa_k_norm, swa_sinks, w_up_nsa, w_up_sb, w_up_swa, w_out, ffn2_norm, ffn2_w_gate, ffn2_w_up, ffn2_w_down):
    B, S, D = x.shape
    T = B * S
    nb = S // Q_BLOCK
    nc = S // CMP_STRIDE
    depth = w_in.shape[0]
    scale = HEAD_DIM ** -0.5
    scale_of = {"nq": scale * LOG2E, "sq": scale, "wq": scale}
    qk_bound = (BOUND_SLACK * LOG2E * HEAD_DIM ** 0.5
                * jnp.max(jnp.abs(nsa_q_norm.astype(F32))) * jnp.max(jnp.abs(nsa_k_norm.astype(F32))))
    cbias, sbias, wbias, msel, swa_bias = _bias_tables(rel_bias, nb, qk_bound)
    plan = _proj_plan(scale_of)
    ov_t = _overlap_t(S)
    tri = jnp.asarray(np.tril(np.ones((SB_KT, SB_KT), np.float32), -1), BF16)

    (ffn1_w_gate, ffn1_w_up, ffn1_w_down, ffn2_w_gate, ffn2_w_up, ffn2_w_down, w_up_nsa, w_up_sb, w_up_swa,
     w_out) = (w.astype(BF16) for w in (ffn1_w_gate, ffn1_w_up, ffn1_w_down, ffn2_w_gate, ffn2_w_up, ffn2_w_down,
                                        w_up_nsa, w_up_sb, w_up_swa, w_out))
    w_bgate = w_in[:, :, OFF_BGATE:].astype(BF16)

    x2 = x.reshape(T, D)
    for l in range(depth):
        x2 = _ffn(x2, ffn1_norm[l], ffn1_w_gate[l], ffn1_w_up[l], ffn1_w_down[l])

        norm_of = {"nq": nsa_q_norm[l], "nks": nsa_k_norm[l], "nkw": nsa_k_norm[l],
                   "wq": swa_q_norm[l], "wk": swa_k_norm[l]}
        gain = jnp.concatenate([_head_gains(ROW_HEADS, norm_of, scale_of), _head_gains(COL_HEADS, norm_of, scale_of),
                                jnp.ones((1, LANES), F32)], axis=1)
        w_gate = jnp.pad(w_in[l][:, OFF_NGATE:OFF_NGATE + N_GATE], ((0, 0), (0, LANES - N_GATE)))
        w_all = jnp.concatenate([_head_cols(w_in[l], ROW_HEADS), _head_cols(w_in[l], COL_HEADS), w_gate], axis=1)
        rowmat, colmat, gates = _proj(x2, mix_norm[l], w_all.astype(BF16), gain, plan)
        gates = gates[:N_GATE].reshape(N_BRANCHES, NSA_KV_HEADS, NSA_REP, T)

        rows4 = rowmat[ROW_KC:ROW_KC + 2 * NSA_KV_HEADS].reshape(2 * NSA_KV_HEADS, B, nc, CMP_STRIDE * HEAD_DIM)
        cmp_rows, cmp_cols = _compress(rows4, jnp.stack([nsa_cmp_k_w1[l], nsa_cmp_v_w1[l]]),
                                       jnp.stack([nsa_cmp_k_w2[l], nsa_cmp_v_w2[l]]),
                                       nsa_cmp_pos[l], nsa_k_norm[l])

        y_nsa = _nsa_attention(colmat, rowmat, cmp_rows, cmp_cols, gates, cbias, sbias, wbias, ov_t, msel, B, S)
        y_sb = _sb_attention(colmat, rowmat, tri, B, S)
        sinks = jnp.repeat(swa_sinks[l].astype(F32).reshape(SWA_KV_HEADS, 1, SWA_REP), Q_BLOCK, axis=2)
        y_swa = _swa_attention(colmat, rowmat, swa_bias, sinks, B, S)

        x2 = _merge(x2, mix_norm[l], w_bgate[l], y_nsa.reshape(T, -1), y_sb.reshape(T, -1),
                    y_swa.reshape(T, -1), w_up_nsa[l], w_up_sb[l], w_up_swa[l], w_out[l])
        x2 = _ffn(x2, ffn2_norm[l], ffn2_w_gate[l], ffn2_w_up[l], ffn2_w_down[l])
    return x2.reshape(B, S, D)
```

```python
import functools
import math

import numpy as np
import jax
import jax.numpy as jnp
from jax import lax
from jax.experimental import pallas as pl
from jax.experimental.pallas import tpu as pltpu

D_MODEL = 1024
HEAD_DIM = 64
Q_BLOCK = 128
NSA_HEADS = 8
NSA_KV_HEADS = 2
NSA_REP = NSA_HEADS // NSA_KV_HEADS
CMP_LEN = 32
CMP_STRIDE = 16
CMP_HIDDEN = 256
SLC_LEN = 64
SLC_TOPN = 16
NSA_WINDOW = 512
FORCE_SCORE = 1e4
SB_HEADS = 4
SWA_HEADS = 4
SWA_KV_HEADS = 2
SWA_REP = SWA_HEADS // SWA_KV_HEADS
SWA_WINDOW = 128
NUM_BUCKETS = 32
MAX_DISTANCE = 128
D_FF = 2816
NORM_EPS = 1e-6
NEG_INF = -1e30
N_BRANCHES = 3
N_GATE = 3 * NSA_HEADS

LANES = 128
VMEM_LIMIT = 56 * 1024 * 1024
BF16 = jnp.bfloat16
F32 = jnp.float32

_sizes = (NSA_HEADS * HEAD_DIM,) + (NSA_KV_HEADS * HEAD_DIM,) * 6 + (N_GATE,) \
    + (SB_HEADS * HEAD_DIM,) * 3 + (SWA_HEADS * HEAD_DIM, SWA_KV_HEADS * HEAD_DIM,
                                    SWA_KV_HEADS * HEAD_DIM, N_BRANCHES * D_MODEL)
_offs = np.concatenate([[0], np.cumsum(_sizes)])
(OFF_NQ, OFF_NKC, OFF_NVC, OFF_NKS, OFF_NVS, OFF_NKW, OFF_NVW, OFF_NGATE,
 OFF_SQ, OFF_SK, OFF_SV, OFF_WQ, OFF_WK, OFF_WV, OFF_BGATE) = (int(o) for o in _offs[:-1])

ROW_HEADS = ([("nkc", OFF_NKC + 64 * g) for g in range(2)] + [("nvc", OFF_NVC + 64 * g) for g in range(2)]
             + [("nks", OFF_NKS + 64 * g) for g in range(2)] + [("nkw", OFF_NKW + 64 * g) for g in range(2)]
             + [("sq", OFF_SQ + 64 * h) for h in range(4)] + [("sv", OFF_SV + 64 * h) for h in range(4)]
             + [("wk", OFF_WK + 64 * g) for g in range(2)])
ROW_KC, ROW_VC, ROW_KS, ROW_KW, ROW_SQ, ROW_SV, ROW_WK = 0, 2, 4, 6, 8, 12, 16
COL_HEADS = ([("nq", OFF_NQ + 64 * h) for h in range(8)] + [("nvs", OFF_NVS + 64 * g) for g in range(2)]
             + [("nvw", OFF_NVW + 64 * g) for g in range(2)] + [("sk", OFF_SK + 64 * h) for h in range(4)]
             + [("wq", OFF_WQ + 64 * h) for h in range(4)] + [("wv", OFF_WV + 64 * g) for g in range(2)])
COL_NQ, COL_VS, COL_VW, COL_SK, COL_WQ, COL_WV = 0, 8, 10, 12, 16, 20


def _dot(a, b):
    return jnp.dot(a, b, preferred_element_type=F32)


def _rms_rows(x, w):
    ms = jnp.mean(x * x, axis=-1, keepdims=True)
    return x * lax.rsqrt(ms + NORM_EPS) * w


def _cparams(sem, flags=None):
    return pltpu.CompilerParams(dimension_semantics=sem, vmem_limit_bytes=VMEM_LIMIT, flags=flags)


FFN_TM = 512
MXU_TILE = 256
FFN_FC = MXU_TILE


def _ffn_kernel(x_ref, nw_ref, wg_ref, wu_ref, wd_ref, o_ref):
    x = x_ref[...]
    h = _rms_rows(x, nw_ref[...]).astype(BF16)
    acc = jnp.zeros(x.shape, F32)
    for c in range(D_FF // FFN_FC):
        cols = slice(c * FFN_FC, (c + 1) * FFN_FC)
        g = _dot(h, wg_ref[:, cols])
        u = _dot(h, wu_ref[:, cols])
        a = (g * jax.nn.sigmoid(g)) * u
        acc = acc + _dot(a.astype(BF16), wd_ref[cols, :])
    o_ref[...] = x + 0.5 * acc


def _ffn(x2, nw, wg, wu, wd):
    T = x2.shape[0]
    resident = lambda shape: pl.BlockSpec(shape, lambda i: (0, 0), pipeline_mode=pl.Buffered(1))
    return pl.pallas_call(
        _ffn_kernel,
        grid=(T // FFN_TM,),
        in_specs=[
            pl.BlockSpec((FFN_TM, D_MODEL), lambda i: (i, 0)),
            pl.BlockSpec((1, D_MODEL), lambda i: (0, 0)),
            resident((D_MODEL, D_FF)), resident((D_MODEL, D_FF)), resident((D_FF, D_MODEL)),
        ],
        out_specs=pl.BlockSpec((FFN_TM, D_MODEL), lambda i: (i, 0)),
        out_shape=jax.ShapeDtypeStruct((T, D_MODEL), F32),
        compiler_params=_cparams(("parallel",)),
        name="ffn",
    )(x2, nw.reshape(1, D_MODEL), wg.astype(BF16), wu.astype(BF16), wd.astype(BF16))


PROJ_TM = 512


GATE_ROWS = 32

_NORMED = ("nq", "nks", "nkw", "wq", "wk")


def _proj_plan(scale_of):
    plan = []
    for kind, heads in (("row", ROW_HEADS), ("col", COL_HEADS)):
        for p in range(len(heads) // 2):
            name = heads[2 * p][0]
            assert heads[2 * p + 1][0] == name
            plan.append((kind, p, name in _NORMED, 1.0 if name in _NORMED else scale_of.get(name, 1.0)))
    plan.append(("gate", 0, False, 1.0))
    return tuple(plan)


def _proj_kernel(x_ref, nw_ref, w_ref, gain_ref, bd_ref, row_ref, col_ref, gate_ref, *, plan):
    h = _rms_rows(x_ref[...], nw_ref[...]).astype(BF16)
    pair = 2 * LANES
    for c0 in range(0, len(plan) * LANES, pair):
        width = min(pair, len(plan) * LANES - c0)
        zg = _dot(h, w_ref[:, c0:c0 + width])
        for u in range(width // LANES):
            t = c0 // LANES + u
            kind, p, normed, scale = plan[t]
            z = zg[:, u * LANES:(u + 1) * LANES]
            if normed:
                ms = _dot((z * z).astype(BF16), bd_ref[...])
                z = z * lax.rsqrt(ms + NORM_EPS) * gain_ref[:, t * LANES:(t + 1) * LANES]
            elif scale != 1.0:
                z = z * scale
            if kind == "row":
                row_ref[2 * p] = z[:, :HEAD_DIM].astype(row_ref.dtype)
                row_ref[2 * p + 1] = z[:, HEAD_DIM:].astype(row_ref.dtype)
            elif kind == "col":
                z_t = z.T
                for c in range(PROJ_TM // LANES):
                    col_ref[2 * p, c] = z_t[:HEAD_DIM, c * LANES:(c + 1) * LANES].astype(col_ref.dtype)
                    col_ref[2 * p + 1, c] = z_t[HEAD_DIM:, c * LANES:(c + 1) * LANES].astype(col_ref.dtype)
            else:
                gate_ref[...] = jax.nn.sigmoid(z).T[:GATE_ROWS]


def _proj(x2, nw, w, gain, plan):
    T = x2.shape[0]
    n_row, n_col = len(ROW_HEADS), len(COL_HEADS)
    bd = np.kron(np.eye(2, dtype=np.float32), np.full((HEAD_DIM, HEAD_DIM), 1.0 / HEAD_DIM, np.float32))
    return pl.pallas_call(
        functools.partial(_proj_kernel, plan=plan),
        grid=(T // PROJ_TM,),
        in_specs=[
            pl.BlockSpec((PROJ_TM, D_MODEL), lambda i: (i, 0)),
            pl.BlockSpec((1, D_MODEL), lambda i: (0, 0)),
            pl.BlockSpec(w.shape, lambda i: (0, 0)),
            pl.BlockSpec(gain.shape, lambda i: (0, 0)),
            pl.BlockSpec((LANES, LANES), lambda i: (0, 0)),
        ],
        out_specs=[
            pl.BlockSpec((n_row, PROJ_TM, HEAD_DIM), lambda i: (0, i, 0)),
            pl.BlockSpec((n_col, PROJ_TM // LANES, HEAD_DIM, LANES), lambda i: (0, i, 0, 0)),
            pl.BlockSpec((GATE_ROWS, PROJ_TM), lambda i: (0, i)),
        ],
        out_shape=[jax.ShapeDtypeStruct((n_row, T, HEAD_DIM), BF16),
                   jax.ShapeDtypeStruct((n_col, T // LANES, HEAD_DIM, LANES), BF16),
                   jax.ShapeDtypeStruct((GATE_ROWS, T), F32)],
        compiler_params=_cparams(("parallel",)),
        name="proj",
    )(x2, nw.reshape(1, D_MODEL), w, gain, jnp.asarray(bd, BF16))


def _compress_kernel(r_ref, w1_ref, w2_ref, pos_ref, kn_ref, row_ref, col_ref):
    kind = pl.program_id(0) // NSA_KV_HEADS
    r = r_ref[0, 0]
    nc = r.shape[0]
    half = CMP_STRIDE * HEAD_DIM
    w1 = w1_ref[0]
    p_lo = _dot(r, w1[:half])
    p_hi = _dot(r, w1[half:])
    p_pos = _dot(jnp.broadcast_to(pos_ref[...], (8, CMP_LEN * HEAD_DIM)).astype(BF16), w1)[:1]
    hdn = p_lo + pltpu.roll(p_hi, nc - 1, 0) + p_pos
    hdn = hdn * jax.nn.sigmoid(hdn)
    out = _dot(hdn.astype(BF16), w2_ref[0])
    ms = jnp.sum(out * out, axis=-1, keepdims=True) * (1.0 / HEAD_DIM)
    normed = out * lax.rsqrt(ms + NORM_EPS) * kn_ref[...]
    out = jnp.where(kind == 0, normed, out)
    row_ref[0, 0] = out[:, :HEAD_DIM].astype(row_ref.dtype)
    col_ref[0, 0] = out.T[:HEAD_DIM].astype(col_ref.dtype)


def _compress(rows4, w1, w2, pos, k_norm):
    assert ROW_KC == 0 and ROW_VC == NSA_KV_HEADS
    _, B, NC, _ = rows4.shape
    kn = jnp.pad(k_norm.reshape(1, HEAD_DIM), ((0, 0), (0, LANES - HEAD_DIM)))
    w2p = jnp.pad(w2, ((0, 0), (0, 0), (0, LANES - HEAD_DIM))).astype(BF16)
    return pl.pallas_call(
        _compress_kernel,
        grid=(4, B),
        in_specs=[
            pl.BlockSpec((1, 1, NC, CMP_STRIDE * HEAD_DIM), lambda h, b: (h, b, 0, 0)),
            pl.BlockSpec((1, CMP_LEN * HEAD_DIM, CMP_HIDDEN), lambda h, b: (h // NSA_KV_HEADS, 0, 0)),
            pl.BlockSpec((1, CMP_HIDDEN, LANES), lambda h, b: (h // NSA_KV_HEADS, 0, 0)),
            pl.BlockSpec((1, CMP_LEN * HEAD_DIM), lambda h, b: (0, 0)),
            pl.BlockSpec((1, LANES), lambda h, b: (0, 0)),
        ],
        out_specs=[
            pl.BlockSpec((1, 1, NC, HEAD_DIM), lambda h, b: (h, b, 0, 0)),
            pl.BlockSpec((1, 1, HEAD_DIM, NC), lambda h, b: (h, b, 0, 0)),
        ],
        out_shape=[jax.ShapeDtypeStruct((4, B, NC, HEAD_DIM), BF16),
                   jax.ShapeDtypeStruct((4, B, HEAD_DIM, NC), BF16)],
        compiler_params=_cparams(("parallel", "parallel")),
        name="nsa_compress",
    )(rows4, w1.astype(BF16), w2p, pos.reshape(1, CMP_LEN * HEAD_DIM), kn)


def _softmax_tile_update(s, v_t, m, l, acc):
    m_new = jnp.maximum(m, jnp.max(s, axis=0, keepdims=True))
    alpha = jnp.exp2(m - m_new)
    p = jnp.exp2(s - m_new)
    l_new = alpha * l + jnp.sum(p, axis=0, keepdims=True)
    acc_new = alpha * acc + _dot(v_t, p.astype(BF16))
    return m_new, l_new, acc_new


def _pairs_to_rows(o_t):
    n = o_t.shape[1] // LANES
    outs = []
    for p in range(n // 2):
        a = o_t[:, (2 * p) * LANES:(2 * p + 1) * LANES]
        b = o_t[:, (2 * p + 1) * LANES:(2 * p + 2) * LANES]
        outs.append(jnp.concatenate([a, b], axis=0).T)
    return outs[0] if len(outs) == 1 else jnp.concatenate(outs, axis=1)


FAR_TILES = 4
NSA_SUB = 2
TAKEN = -2.0
LOG2E = math.log2(math.e)
NSA_TINY = 2.0 ** -100
BOUND_SLACK = 1.05


def _nsa_kernel(q_ref, kc_ref, vc_ref, ks_ref, vs_ref, kw_ref, vw_ref, gate_ref,
                cbias_ref, sbias_ref, wbias_ref, ov_ref, msel_ref, o_ref, *scratch, nb):
    neg_refs, stage_refs = scratch[:NSA_SUB], scratch[NSA_SUB:]
    refs = (q_ref, kc_ref, vc_ref, ks_ref, vs_ref, kw_ref, vw_ref, gate_ref, cbias_ref, sbias_ref, wbias_ref,
            ov_ref, msel_ref)
    finish = [_nsa_block(u, refs, neg_refs[u], stage_refs, nb) for u in range(NSA_SUB)]
    for u in range(NSA_SUB):
        o_ref[0, u * Q_BLOCK:(u + 1) * Q_BLOCK, :] = finish[u]().astype(o_ref.dtype)


def _nsa_block(u, refs, neg_ref, stage_refs, nb):
    (q_ref, kc_ref, vc_ref, ks_ref, vs_ref, kw_ref, vw_ref, gate_ref, cbias_ref, sbias_ref, wbias_ref,
     ov_ref, msel_ref) = refs
    i = pl.program_id(2) * NSA_SUB + u
    R = NSA_REP
    N = R * Q_BLOCK
    nc = kc_ref.shape[2]
    q_t = jnp.concatenate([q_ref[r, u] for r in range(R)], axis=1)
    n_win = NSA_WINDOW // Q_BLOCK

    def k_rows(ref, J0, n_tiles):
        return ref[0, pl.ds(pl.multiple_of(J0 * Q_BLOCK, Q_BLOCK), n_tiles * Q_BLOCK), :]

    def v_cols(ref, J0, n_tiles):
        return jnp.concatenate([ref[0, J0 + u] for u in range(n_tiles)], axis=1)

    def sel_mask(J0, n_tiles):
        parts = [jnp.broadcast_to(neg_ref[pl.ds(2 * J0 + u, 1), :], (SLC_LEN, Q_BLOCK)) for u in range(2 * n_tiles)]
        return jnp.concatenate([jnp.concatenate(parts, axis=0)] * R, axis=1)

    def compressed(fast):
        cb = cbias_ref[0, pl.ds(pl.multiple_of(8 * (nb - 1) - 8 * i, 8), nc), :]
        s = _dot(kc_ref[0, 0], q_t) + cb
        if fast:
            e = jnp.exp2(s)
        else:
            e = jnp.where(s > 0.5 * NEG_INF, jnp.exp2(s - jnp.max(s, axis=0, keepdims=True)), 0.0)
        l = jnp.sum(e, axis=0, keepdims=True)
        p = e * jnp.where(l > 0.0, 1.0 / l, 0.0)
        return p, l, _dot(vc_ref[0, 0], p.astype(BF16))

    def select_blocks(p_c):
        p_sum = p_c[:, 0:Q_BLOCK]
        for r in range(1, R):
            p_sum = p_sum + p_c[:, r * Q_BLOCK:(r + 1) * Q_BLOCK]
        p_hi = p_sum.astype(BF16)
        p_lo = (p_sum - p_hi.astype(F32)).astype(BF16)
        imp = _dot(ov_ref[...], p_hi) + _dot(ov_ref[...], p_lo)
        n_slc = imp.shape[0]
        jj = lax.broadcasted_iota(jnp.int32, imp.shape, 0)
        qq = lax.broadcasted_iota(jnp.int32, imp.shape, 1)
        cur = 2 * i + jnp.where(qq >= SLC_LEN, 1, 0)
        forced = (jj == 0) | (jj == cur) | (jj == cur - 1)
        score = jnp.where(forced, FORCE_SCORE, jnp.where(jj <= cur, imp, -1.0))
        grp = 8
        j_rows = [jj[a:a + grp] for a in range(0, n_slc, grp)]
        for _ in range(min(SLC_TOPN, n_slc)):
            cand = [(score[a:a + grp], j_rows[a // grp]) for a in range(0, n_slc, grp)]
            while len(cand) > 1:
                nxt = []
                for a in range(0, len(cand) - 1, 2):
                    (s0, j0), (s1, j1) = cand[a], cand[a + 1]
                    right = s1 > s0
                    nxt.append((jnp.maximum(s0, s1), jnp.where(right, j1, j0)))
                cand = nxt + cand[len(cand) - len(cand) % 2:]
            s8, j8 = cand[0]
            best = jnp.max(s8, axis=0, keepdims=True)
            first = jnp.min(jnp.where(s8 == best, j8, n_slc), axis=0, keepdims=True)
            score = jnp.where(jj == first, TAKEN, score)
        neg_ref[...] = jnp.where(score == TAKEN, msel_ref[0], NEG_INF)

    def add_chunk(st, s, v_t):
        e = jnp.exp2(s)
        return st[0] + jnp.sum(e, axis=0, keepdims=True), st[1] + _dot(v_t, e.astype(BF16))

    zero = (jnp.zeros((1, N), F32), jnp.zeros((HEAD_DIM, N), F32))

    def selected_near():
        J0 = jnp.maximum(i - 1, 0)
        bias = sbias_ref[0, pl.ds(pl.multiple_of((2 - (i - J0)) * Q_BLOCK, Q_BLOCK), 2 * Q_BLOCK), :]
        return add_chunk(zero, _dot(k_rows(ks_ref, J0, 2), q_t) + bias + sel_mask(J0, 2), v_cols(vs_ref, J0, 2))

    def selected_far(st):
        n_far = jnp.maximum(i - 1, 0)

        def far_chunk(J0, st, n_valid):
            mask = sel_mask(J0, FAR_TILES)
            if n_valid is not None:
                row = lax.broadcasted_iota(jnp.int32, mask.shape, 0)
                mask = jnp.where(row < n_valid * Q_BLOCK, mask, NEG_INF)
            return add_chunk(st, _dot(k_rows(ks_ref, J0, FAR_TILES), q_t) + mask, v_cols(vs_ref, J0, FAR_TILES))

        n_chunks = n_far // FAR_TILES
        rem = n_far - FAR_TILES * n_chunks

        def tile_of(c):
            return jnp.clip(n_far - FAR_TILES * (c + 1), 0, nb - FAR_TILES)

        def scores(c):
            return _dot(k_rows(ks_ref, tile_of(c), FAR_TILES), q_t)

        def values(c, e_ref, acc):
            return acc + _dot(v_cols(vs_ref, tile_of(c), FAR_TILES), e_ref[...])

        def stage(c, st, s_in, s_out, e_in, e_out):
            s_out[...] = scores(c + 1)
            acc = values(c - 1, e_in, st[1])
            e = jnp.exp2(s_in[...] + sel_mask(tile_of(c), FAR_TILES))
            e_out[...] = e.astype(BF16)
            return st[0] + jnp.sum(e, axis=0, keepdims=True), acc

        s_a, s_b, e_a, e_b = stage_refs
        e_b[...] = jnp.zeros_like(e_b)
        s_a[...] = scores(0)

        def pair(c, st):
            st = stage(c, st, s_a, s_b, e_b, e_a)
            return stage(c + 1, st, s_b, s_a, e_a, e_b)

        n_quads = n_chunks // 4
        st = lax.fori_loop(0, n_quads, lambda t, st: pair(4 * t + 2, pair(4 * t, st)), st)
        st = lax.cond(n_chunks % 4 >= 2, lambda st: pair(4 * n_quads, st), lambda st: st, st)

        def odd_tail(st):
            st = stage(n_chunks - 1, st, s_a, s_b, e_b, e_a)
            return st[0], values(n_chunks - 1, e_a, st[1])

        st = lax.cond(n_chunks % 2 == 1, odd_tail, lambda st: (st[0], values(n_chunks - 1, e_b, st[1])), st)
        return lax.cond(rem > 0, lambda st: far_chunk(0, st, rem), lambda st: st, st)

    def window_fast():
        J0 = jnp.maximum(i - n_win, 0)
        bias = wbias_ref[0, pl.ds(pl.multiple_of((n_win - (i - J0)) * Q_BLOCK, Q_BLOCK), (n_win + 1) * Q_BLOCK), :]
        return add_chunk(zero, _dot(k_rows(kw_ref, J0, n_win + 1), q_t) + bias, v_cols(vw_ref, J0, n_win + 1))

    def sweep_exact(k_ref, v_ref, bias_ref, far_dist, n_back, with_sel):
        def body(t, st):
            J = i - t
            off = pl.multiple_of((far_dist - jnp.minimum(t, far_dist)) * Q_BLOCK, Q_BLOCK)
            s = _dot(k_rows(k_ref, J, 1), q_t) + bias_ref[0, pl.ds(off, Q_BLOCK), :]
            if with_sel:
                s = s + sel_mask(J, 1)
            return _softmax_tile_update(s, v_ref[0, J], *st)

        init = (jnp.full((1, N), NEG_INF, F32),) + zero
        _, l, acc = lax.fori_loop(0, jnp.minimum(i, n_back) + 1, body, init)
        return l, acc

    p_c, l_c, o_c = compressed(True)
    select_blocks(p_c)
    l_w, a_w = window_fast()
    near = selected_near()

    def exact_path():
        p_x, _, o_x = compressed(False)
        select_blocks(p_x)
        l_sx, a_sx = sweep_exact(ks_ref, vs_ref, sbias_ref, 2, nb, True)
        l_wx, a_wx = sweep_exact(kw_ref, vw_ref, wbias_ref, n_win, n_win, False)
        return o_x, a_sx * (1.0 / l_sx), a_wx * (1.0 / l_wx)

    def gate_row(br):
        return jnp.concatenate([gate_ref[br, 0, r:r + 1, u * Q_BLOCK:(u + 1) * Q_BLOCK] for r in range(R)], axis=1)

    def finish():
        l_s, a_s = selected_far(near)
        lane_q = lax.broadcasted_iota(jnp.int32, (1, N), 1) % Q_BLOCK
        has_cmp = (i > 0) | (lane_q >= CMP_LEN - 1)
        l_min = jnp.minimum(jnp.minimum(l_s, l_w), jnp.where(has_cmp, l_c, 1.0))
        o_cx, o_s, o_w = lax.cond(jnp.min(l_min) > NSA_TINY,
                                  lambda: (o_c, a_s * (1.0 / l_s), a_w * (1.0 / l_w)), exact_path)
        o_t = gate_row(0) * o_cx + gate_row(1) * o_s + gate_row(2) * o_w
        return _pairs_to_rows(o_t)

    return finish


def _nsa_attention(colmat, rowmat, cmp_rows, cmp_cols, gates, cbias, sbias, wbias, ov_t, msel, B, S):
    nb = S // Q_BLOCK
    nc = S // CMP_STRIDE
    n_slc = S // SLC_LEN
    G, R = NSA_KV_HEADS, NSA_REP
    kern = functools.partial(_nsa_kernel, nb=nb)
    ns = nb // NSA_SUB
    return pl.pallas_call(
        kern,
        grid=(B, G, ns),
        in_specs=[
            pl.BlockSpec((R, NSA_SUB, HEAD_DIM, LANES), lambda b, g, i: (g, b * ns + i, 0, 0)),
            pl.BlockSpec((1, 1, nc, HEAD_DIM), lambda b, g, i: (g, b, 0, 0)),
            pl.BlockSpec((1, 1, HEAD_DIM, nc), lambda b, g, i: (NSA_KV_HEADS + g, b, 0, 0)),
            pl.BlockSpec((1, S, HEAD_DIM), lambda b, g, i: (ROW_KS + g, b, 0)),
            pl.BlockSpec((1, nb, HEAD_DIM, LANES), lambda b, g, i: (COL_VS + g, b, 0, 0)),
            pl.BlockSpec((1, S, HEAD_DIM), lambda b, g, i: (ROW_KW + g, b, 0)),
            pl.BlockSpec((1, nb, HEAD_DIM, LANES), lambda b, g, i: (COL_VW + g, b, 0, 0)),
            pl.BlockSpec((N_BRANCHES, 1, R, NSA_SUB * LANES), lambda b, g, i: (0, g, 0, b * ns + i)),
            pl.BlockSpec((1, cbias.shape[1], R * Q_BLOCK), lambda b, g, i: (g, 0, 0)),
            pl.BlockSpec((1, sbias.shape[1], R * Q_BLOCK), lambda b, g, i: (g, 0, 0)),
            pl.BlockSpec((1, wbias.shape[1], R * Q_BLOCK), lambda b, g, i: (g, 0, 0)),
            pl.BlockSpec((n_slc, nc), lambda b, g, i: (0, 0)),
            pl.BlockSpec((1, 1, Q_BLOCK), lambda b, g, i: (g, 0, 0)),
        ],
        out_specs=pl.BlockSpec((1, NSA_SUB * Q_BLOCK, R * HEAD_DIM), lambda b, g, i: (b, i, g)),
        out_shape=jax.ShapeDtypeStruct((B, S, NSA_HEADS * HEAD_DIM), BF16),
        scratch_shapes=[pltpu.VMEM((n_slc, Q_BLOCK), F32)] * NSA_SUB
        + [pltpu.VMEM((FAR_TILES * Q_BLOCK, R * Q_BLOCK), F32)] * 2
        + [pltpu.VMEM((FAR_TILES * Q_BLOCK, R * Q_BLOCK), BF16)] * 2,
        compiler_params=_cparams(("parallel", "parallel", "arbitrary")),
        name="nsa_attention",
    )(colmat, cmp_rows, cmp_cols, rowmat, colmat, rowmat, colmat, gates, cbias, sbias, wbias, ov_t, msel)


SB_QB = 512
SB_KT = 256
SB_ZERO_LOG = -110.0


def _sb_kernel(q_ref, k_ref, v_ref, tri_ref, o_ref):
    i = pl.program_id(1)
    H = SB_HEADS
    tri = tri_ref[...]
    rows = lax.broadcasted_iota(jnp.int32, (SB_QB, SB_KT), 0)
    cols = lax.broadcasted_iota(jnp.int32, (SB_QB, SB_KT), 1)
    key_minus_query = cols - rows
    per_q = SB_QB // SB_KT

    def step(J, st, masked):
        out = []
        for h in range(H):
            carry, acc = st[h]
            k_t = jnp.concatenate([k_ref[h, (SB_KT // LANES) * J + c] for c in range(SB_KT // LANES)], axis=1)
            z = _dot(q_ref[h], k_t)
            sp = jnp.maximum(z, 0.0) + jnp.log(1.0 + jnp.exp(-jnp.abs(z)))
            if masked:
                valid = key_minus_query < i * SB_QB - J * SB_KT
                lf = jnp.where(valid, -sp, 0.0)
            else:
                lf = -sp
            lf_hi = lf.astype(BF16)
            lf_lo = (lf - lf_hi.astype(F32)).astype(BF16)
            after = _dot(lf_hi, tri) + _dot(lf_lo, tri) + carry
            w = jnp.exp(z - sp + after)
            if masked:
                w = jnp.where(valid, w, 0.0)
            v = v_ref[h, pl.ds(pl.multiple_of(J * SB_KT, SB_KT), SB_KT), :]
            acc = acc + _dot(w.astype(BF16), v)
            carry = carry + jnp.sum(lf, axis=1, keepdims=True)
            out.append((carry, acc))
        return tuple(out)

    st = tuple((jnp.zeros((SB_QB, 1), F32), jnp.zeros((SB_QB, HEAD_DIM), F32)) for _ in range(H))
    for d in range(per_q):
        st = step(per_q * i + per_q - 1 - d, st, True)

    def alive(st):
        worst = st[0][0]
        for h in range(1, H):
            worst = jnp.maximum(worst, st[h][0])
        return jnp.max(worst) > SB_ZERO_LOG

    def body(c):
        J, _, st = c
        st = step(J, st, False)
        return J - 1, alive(st), st

    _, _, st = lax.while_loop(lambda c: (c[0] >= 0) & c[1], body, (per_q * i - 1, alive(st), st))
    o_ref[0] = jnp.concatenate([st[h][1] for h in range(H)], axis=1).astype(o_ref.dtype)


def _sb_attention(colmat, rowmat, tri, B, S):
    nq = S // SB_QB
    nb = S // LANES
    H = SB_HEADS
    return pl.pallas_call(
        _sb_kernel,
        grid=(B, nq),
        in_specs=[
            pl.BlockSpec((H, SB_QB, HEAD_DIM), lambda b, i: (ROW_SQ // H, b * nq + i, 0)),
            pl.BlockSpec((H, nb, HEAD_DIM, LANES), lambda b, i: (COL_SK // H, b, 0, 0)),
            pl.BlockSpec((H, S, HEAD_DIM), lambda b, i: (ROW_SV // H, b, 0)),
            pl.BlockSpec((SB_KT, SB_KT), lambda b, i: (0, 0)),
        ],
        out_specs=pl.BlockSpec((1, SB_QB, H * HEAD_DIM), lambda b, i: (b, i, 0)),
        out_shape=jax.ShapeDtypeStruct((B, S, H * HEAD_DIM), BF16),
        compiler_params=_cparams(("parallel", "arbitrary")),
        name="sb_attention",
    )(rowmat, colmat, rowmat, tri)


SWA_SUB = 4


def _swa_kernel(q_ref, k_ref, v_ref, bias_ref, sink_ref, o_ref):
    R = SWA_REP
    sink = sink_ref[0]
    for u in range(SWA_SUB):
        i = pl.program_id(2) * SWA_SUB + u
        q_t = jnp.concatenate([q_ref[r, u] for r in range(R)], axis=1)
        J0 = jnp.maximum(i - 1, 0)
        k = k_ref[0, pl.ds(pl.multiple_of(J0 * Q_BLOCK, Q_BLOCK), 2 * Q_BLOCK), :]
        bias = bias_ref[0, pl.ds(pl.multiple_of((1 - (i - J0)) * Q_BLOCK, Q_BLOCK), 2 * Q_BLOCK), :]
        s = _dot(k, q_t) + bias
        m = jnp.maximum(jnp.max(s, axis=0, keepdims=True), sink)
        e = jnp.exp(s - m)
        den = jnp.sum(e, axis=0, keepdims=True) + jnp.exp(sink - m)
        v_t = jnp.concatenate([v_ref[0, J0], v_ref[0, J0 + 1]], axis=1)
        o_t = _dot(v_t, e.astype(BF16)) * (1.0 / den)
        o_ref[0, u * Q_BLOCK:(u + 1) * Q_BLOCK, :] = _pairs_to_rows(o_t).astype(o_ref.dtype)


def _swa_attention(colmat, rowmat, bias, sinks, B, S):
    nb = S // Q_BLOCK
    ns = nb // SWA_SUB
    G, R = SWA_KV_HEADS, SWA_REP
    return pl.pallas_call(
        _swa_kernel,
        grid=(B, G, ns),
        in_specs=[
            pl.BlockSpec((R, SWA_SUB, HEAD_DIM, LANES), lambda b, g, i: (COL_WQ // R + g, b * ns + i, 0, 0)),
            pl.BlockSpec((1, S, HEAD_DIM), lambda b, g, i: (ROW_WK + g, b, 0)),
            pl.BlockSpec((1, nb, HEAD_DIM, LANES), lambda b, g, i: (COL_WV + g, b, 0, 0)),
            pl.BlockSpec((1, 3 * Q_BLOCK, R * Q_BLOCK), lambda b, g, i: (g, 0, 0)),
            pl.BlockSpec((1, 1, R * Q_BLOCK), lambda b, g, i: (g, 0, 0)),
        ],
        out_specs=pl.BlockSpec((1, SWA_SUB * Q_BLOCK, R * HEAD_DIM), lambda b, g, i: (b, i, g)),
        out_shape=jax.ShapeDtypeStruct((B, S, SWA_HEADS * HEAD_DIM), BF16),
        compiler_params=_cparams(("parallel", "parallel", "arbitrary")),
        name="swa_attention",
    )(colmat, rowmat, colmat, bias, sinks)


MERGE_TM = 512


def _merge_kernel(x_ref, nw_ref, wg_ref, ya_ref, yb_ref, yc_ref, ua_ref, ub_ref, uc_ref, wo_ref, o_ref):
    x = x_ref[...]
    h = _rms_rows(x, nw_ref[...]).astype(BF16)
    merged = None
    for br, (y_ref, u_ref) in enumerate(((ya_ref, ua_ref), (yb_ref, ub_ref), (yc_ref, uc_ref))):
        g = jax.nn.sigmoid(_dot(h, wg_ref[:, br * D_MODEL:(br + 1) * D_MODEL]))
        t = g * _dot(y_ref[...], u_ref[...])
        merged = t if merged is None else merged + t
    o_ref[...] = x + _dot(merged.astype(BF16), wo_ref[...])


def _merge(x2, nw, w_bgate, y_nsa, y_sb, y_swa, u_nsa, u_sb, u_swa, w_out):
    T = x2.shape[0]
    full = lambda shape: pl.BlockSpec(shape, lambda i: (0, 0))
    rows = lambda n: pl.BlockSpec((MERGE_TM, n), lambda i: (i, 0))
    return pl.pallas_call(
        _merge_kernel,
        grid=(T // MERGE_TM,),
        in_specs=[rows(D_MODEL), full((1, D_MODEL)), full((D_MODEL, N_BRANCHES * D_MODEL)),
                  rows(y_nsa.shape[1]), rows(y_sb.shape[1]), rows(y_swa.shape[1]),
                  full(u_nsa.shape), full(u_sb.shape), full(u_swa.shape), full((D_MODEL, D_MODEL))],
        out_specs=rows(D_MODEL),
        out_shape=jax.ShapeDtypeStruct((T, D_MODEL), F32),
        compiler_params=_cparams(("parallel",)),
        name="merge",
    )(x2, nw.reshape(1, D_MODEL), w_bgate.astype(BF16), y_nsa, y_sb, y_swa,
      u_nsa.astype(BF16), u_sb.astype(BF16), u_swa.astype(BF16), w_out.astype(BF16))


def _t5_bucket(dist):
    max_exact = NUM_BUCKETS // 2
    d = jnp.maximum(dist, 0)
    df = jnp.maximum(d, 1).astype(F32)
    large = max_exact + (jnp.log(df / max_exact) / math.log(MAX_DISTANCE / max_exact)
                         * (NUM_BUCKETS - max_exact)).astype(jnp.int32)
    large = jnp.minimum(large, NUM_BUCKETS - 1)
    return jnp.where(d < max_exact, d, large)


def _toeplitz(ext, delta):
    H = ext.shape[0]
    n = Q_BLOCK
    c = delta + MAX_DISTANCE
    w = ext[:, c - (n - 1):c + n]
    flat = jnp.pad(jnp.broadcast_to(w[:, None, :], (H, n, 2 * n - 1)), ((0, 0), (0, 0), (0, 1))).reshape(H, 2 * n * n)
    return flat[:, n - 1:n - 1 + n * (2 * n - 1)].reshape(H, n, 2 * n - 1)[:, :, :n]


def _lanes(t, G, R):
    K, Q = t.shape[1:]
    return t.reshape(G, R, K, Q).transpose(0, 2, 1, 3).reshape(G, K, R * Q)


def _bias_tables(rel_bias, nb, qk_bound):
    n = Q_BLOCK
    G, R = NSA_KV_HEADS, NSA_REP
    by_dist = rel_bias[_t5_bucket(jnp.arange(MAX_DISTANCE)), :].T.astype(F32)
    far = rel_bias[NUM_BUCKETS - 1, :].astype(F32)
    neg = jnp.full((1, n), NEG_INF, F32)
    k = jnp.arange(n)[:, None]
    q = jnp.arange(n)[None, :]
    above = (k > q)[None]

    rel = (by_dist[:NSA_HEADS] - far[:NSA_HEADS, None]) * LOG2E
    m_g = qk_bound + jnp.maximum(jnp.max(rel.reshape(G, -1), axis=1), 0.0)
    m_h = jnp.repeat(m_g, R)[:, None, None]
    ext = jnp.concatenate([jnp.broadcast_to(neg, (NSA_HEADS, n)), rel, jnp.zeros((NSA_HEADS, n), F32)], axis=1)
    t0, t1 = _toeplitz(ext, 0), _toeplitz(ext, n)
    zeros = jnp.zeros_like(t0)
    masked = jnp.full_like(t0, NEG_INF)
    sbias = _lanes(jnp.concatenate([zeros, t1, t0, masked], axis=1), G, R)
    n_win = NSA_WINDOW // n
    edge = jnp.where(above, zeros, NEG_INF)
    tiles = [edge] + [zeros] * (n_win - 2) + [t1, t0]
    wbias = _lanes(jnp.concatenate([t - m_h for t in tiles] + [masked] * n_win, axis=1), G, R)
    off = 8 * (nb - 1)
    near = jnp.stack([ext[:, 97 - CMP_STRIDE * m:97 - CMP_STRIDE * m + n] for m in range(-9, 7)], axis=1)
    n_rows = 16 * nb - 8
    cb = jnp.concatenate([jnp.zeros((NSA_HEADS, off - 9, n), F32), near,
                          jnp.full((NSA_HEADS, n_rows - off - 7, n), NEG_INF, F32)], axis=1) - m_h
    cbias = _lanes(cb, G, R)
    msel = jnp.broadcast_to(-m_g[:, None, None], (G, 1, n))

    hs = slice(NSA_HEADS, NSA_HEADS + SWA_HEADS)
    ext_w = jnp.concatenate([jnp.broadcast_to(neg, (SWA_HEADS, n)), by_dist[hs],
                             jnp.broadcast_to(far[hs, None], (SWA_HEADS, n))], axis=1)
    s0 = _toeplitz(ext_w, 0)
    s1 = jnp.where(above, _toeplitz(ext_w, n), NEG_INF)
    swa_bias = _lanes(jnp.concatenate([s1, s0, jnp.full_like(s0, NEG_INF)], axis=1), SWA_KV_HEADS, SWA_REP)
    return cbias, sbias, wbias, msel, swa_bias


def _overlap_t(S):
    nc, n_slc = S // CMP_STRIDE, S // SLC_LEN
    n = np.arange(nc)[None, :]
    j = np.arange(n_slc)[:, None]
    ov = (n * CMP_STRIDE < (j + 1) * SLC_LEN) & (n * CMP_STRIDE + CMP_LEN - 1 >= j * SLC_LEN) & (n < nc - 1)
    return jnp.asarray(ov, BF16)


def _head_cols(w, heads):
    return jnp.concatenate([w[:, off:off + HEAD_DIM] for _, off in heads], axis=1)


def _head_gains(heads, norm_of, scale_of):
    ones = jnp.ones((HEAD_DIM,), F32)
    gains = [norm_of[name].astype(F32) * scale_of.get(name, 1.0) if name in norm_of else ones for name, _ in heads]
    return jnp.concatenate(gains).reshape(1, -1)


def kernel(x, rel_bias, ffn1_norm, ffn1_w_gate, ffn1_w_up, ffn1_w_down, mix_norm, w_in, nsa_q_norm, nsa_k_norm, nsa_cmp_pos, nsa_cmp_k_w1, nsa_cmp_k_w2, nsa_cmp_v_w1, nsa_cmp_v_w2, swa_q_norm, swa_k_norm, swa_sinks, w_up_nsa, w_up_sb, w_up_swa, w_out, ffn2_norm, ffn2_w_gate, ffn2_w_up, ffn2_w_down):
    B, S, D = x.shape
    T = B * S
    nb = S // Q_BLOCK
    nc = S // CMP_STRIDE
    depth = w_in.shape[0]
    scale = HEAD_DIM ** -0.5
    scale_of = {"nq": scale * LOG2E, "sq": scale, "wq": scale}
    qk_bound = (BOUND_SLACK * LOG2E * HEAD_DIM ** 0.5
                * jnp.max(jnp.abs(nsa_q_norm.astype(F32))) * jnp.max(jnp.abs(nsa_k_norm.astype(F32))))
    cbias, sbias, wbias, msel, swa_bias = _bias_tables(rel_bias, nb, qk_bound)
    plan = _proj_plan(scale_of)
    ov_t = _overlap_t(S)
    tri = jnp.asarray(np.tril(np.ones((SB_KT, SB_KT), np.float32), -1), BF16)

    (ffn1_w_gate, ffn1_w_up, ffn1_w_down, ffn2_w_gate, ffn2_w_up, ffn2_w_down, w_up_nsa, w_up_sb, w_up_swa,
     w_out) = (w.astype(BF16) for w in (ffn1_w_gate, ffn1_w_up, ffn1_w_down, ffn2_w_gate, ffn2_w_up, ffn2_w_down,
                                        w_up_nsa, w_up_sb, w_up_swa, w_out))
    w_bgate = w_in[:, :, OFF_BGATE:].astype(BF16)

    x2 = x.reshape(T, D)
    for l in range(depth):
        x2 = _ffn(x2, ffn1_norm[l], ffn1_w_gate[l], ffn1_w_up[l], ffn1_w_down[l])

        norm_of = {"nq": nsa_q_norm[l], "nks": nsa_k_norm[l], "nkw": nsa_k_norm[l],
                   "wq": swa_q_norm[l], "wk": swa_k_norm[l]}
        gain = jnp.concatenate([_head_gains(ROW_HEADS, norm_of, scale_of), _head_gains(COL_HEADS, norm_of, scale_of),
                                jnp.ones((1, LANES), F32)], axis=1)
        w_gate = jnp.pad(w_in[l][:, OFF_NGATE:OFF_NGATE + N_GATE], ((0, 0), (0, LANES - N_GATE)))
        w_all = jnp.concatenate([_head_cols(w_in[l], ROW_HEADS), _head_cols(w_in[l], COL_HEADS), w_gate], axis=1)
        rowmat, colmat, gates = _proj(x2, mix_norm[l], w_all.astype(BF16), gain, plan)
        gates = gates[:N_GATE].reshape(N_BRANCHES, NSA_KV_HEADS, NSA_REP, T)

        rows4 = rowmat[ROW_KC:ROW_KC + 2 * NSA_KV_HEADS].reshape(2 * NSA_KV_HEADS, B, nc, CMP_STRIDE * HEAD_DIM)
        cmp_rows, cmp_cols = _compress(rows4, jnp.stack([nsa_cmp_k_w1[l], nsa_cmp_v_w1[l]]),
                                       jnp.stack([nsa_cmp_k_w2[l], nsa_cmp_v_w2[l]]),
                                       nsa_cmp_pos[l], nsa_k_norm[l])

        y_nsa = _nsa_attention(colmat, rowmat, cmp_rows, cmp_cols, gates, cbias, sbias, wbias, ov_t, msel, B, S)
        y_sb = _sb_attention(colmat, rowmat, tri, B, S)
        sinks = jnp.repeat(swa_sinks[l].astype(F32).reshape(SWA_KV_HEADS, 1, SWA_REP), Q_BLOCK, axis=2)
        y_swa = _swa_attention(colmat, rowmat, swa_bias, sinks, B, S)

        x2 = _merge(x2, mix_norm[l], w_bgate[l], y_nsa.reshape(T, -1), y_sb.reshape(T, -1),
                    y_swa.reshape(T, -1), w_up_nsa[l], w_up_sb[l], w_up_swa[l], w_out[l])
        x2 = _ffn(x2, ffn2_norm[l], ffn2_w_gate[l], ffn2_w_up[l], ffn2_w_down[l])
    return x2.reshape(B, S, D)
```

```python
import functools
import math

import numpy as np
import jax
import jax.numpy as jnp
from jax import lax
from jax.experimental import pallas as pl
from jax.experimental.pallas import tpu as pltpu

D_MODEL = 1024
HEAD_DIM = 64
Q_BLOCK = 128
NSA_HEADS = 8
NSA_KV_HEADS = 2
NSA_REP = NSA_HEADS // NSA_KV_HEADS
CMP_LEN = 32
CMP_STRIDE = 16
CMP_HIDDEN = 256
SLC_LEN = 64
SLC_TOPN = 16
NSA_WINDOW = 512
SB_HEADS = 4
SWA_HEADS = 4
SWA_KV_HEADS = 2
SWA_REP = SWA_HEADS // SWA_KV_HEADS
SWA_WINDOW = 128
NUM_BUCKETS = 32
MAX_DISTANCE = 128
D_FF = 2816
NORM_EPS = 1e-6
NEG_INF = -1e30
N_BRANCHES = 3
N_GATE = 3 * NSA_HEADS

LANES = 128
VMEM_LIMIT = 56 * 1024 * 1024
BF16 = jnp.bfloat16
F32 = jnp.float32

_sizes = (NSA_HEADS * HEAD_DIM,) + (NSA_KV_HEADS * HEAD_DIM,) * 6 + (N_GATE,) \
    + (SB_HEADS * HEAD_DIM,) * 3 + (SWA_HEADS * HEAD_DIM, SWA_KV_HEADS * HEAD_DIM,
                                    SWA_KV_HEADS * HEAD_DIM, N_BRANCHES * D_MODEL)
_offs = np.concatenate([[0], np.cumsum(_sizes)])
(OFF_NQ, OFF_NKC, OFF_NVC, OFF_NKS, OFF_NVS, OFF_NKW, OFF_NVW, OFF_NGATE,
 OFF_SQ, OFF_SK, OFF_SV, OFF_WQ, OFF_WK, OFF_WV, OFF_BGATE) = (int(o) for o in _offs[:-1])

ROW_HEADS = ([("nkc", OFF_NKC + 64 * g) for g in range(2)] + [("nvc", OFF_NVC + 64 * g) for g in range(2)]
             + [("nks", OFF_NKS + 64 * g) for g in range(2)] + [("nkw", OFF_NKW + 64 * g) for g in range(2)]
             + [("sq", OFF_SQ + 64 * h) for h in range(4)] + [("sv", OFF_SV + 64 * h) for h in range(4)]
             + [("wk", OFF_WK + 64 * g) for g in range(2)])
ROW_KC, ROW_VC, ROW_KS, ROW_KW, ROW_SQ, ROW_SV, ROW_WK = 0, 2, 4, 6, 8, 12, 16
COL_HEADS = ([("nq", OFF_NQ + 64 * h) for h in range(8)] + [("nvs", OFF_NVS + 64 * g) for g in range(2)]
             + [("nvw", OFF_NVW + 64 * g) for g in range(2)] + [("sk", OFF_SK + 64 * h) for h in range(4)]
             + [("wq", OFF_WQ + 64 * h) for h in range(4)] + [("wv", OFF_WV + 64 * g) for g in range(2)])
COL_NQ, COL_VS, COL_VW, COL_SK, COL_WQ, COL_WV = 0, 8, 10, 12, 16, 20


def _dot(a, b):
    return jnp.dot(a, b, preferred_element_type=F32)


def _rms_rows(x, w):
    ms = jnp.mean(x * x, axis=-1, keepdims=True)
    return x * lax.rsqrt(ms + NORM_EPS) * w


def _cparams(sem, flags=None):
    return pltpu.CompilerParams(dimension_semantics=sem, vmem_limit_bytes=VMEM_LIMIT, flags=flags)


FFN_TM = 512
MXU_TILE = 256
FFN_FC = MXU_TILE


def _ffn_kernel(x_ref, nw_ref, wg_ref, wu_ref, wd_ref, o_ref):
    x = x_ref[...]
    h = _rms_rows(x, nw_ref[...]).astype(BF16)
    acc = jnp.zeros(x.shape, F32)
    for c in range(D_FF // FFN_FC):
        cols = slice(c * FFN_FC, (c + 1) * FFN_FC)
        g = _dot(h, wg_ref[:, cols])
        u = _dot(h, wu_ref[:, cols])
        a = (g * jax.nn.sigmoid(g)) * u
        acc = acc + _dot(a.astype(BF16), wd_ref[cols, :])
    o_ref[...] = x + 0.5 * acc


def _ffn(x2, nw, wg, wu, wd):
    T = x2.shape[0]
    resident = lambda shape: pl.BlockSpec(shape, lambda i: (0, 0), pipeline_mode=pl.Buffered(1))
    return pl.pallas_call(
        _ffn_kernel,
        grid=(T // FFN_TM,),
        in_specs=[
            pl.BlockSpec((FFN_TM, D_MODEL), lambda i: (i, 0)),
            pl.BlockSpec((1, D_MODEL), lambda i: (0, 0)),
            resident((D_MODEL, D_FF)), resident((D_MODEL, D_FF)), resident((D_FF, D_MODEL)),
        ],
        out_specs=pl.BlockSpec((FFN_TM, D_MODEL), lambda i: (i, 0)),
        out_shape=jax.ShapeDtypeStruct((T, D_MODEL), F32),
        compiler_params=_cparams(("parallel",)),
        name="ffn",
    )(x2, nw.reshape(1, D_MODEL), wg.astype(BF16), wu.astype(BF16), wd.astype(BF16))


PROJ_TM = 512


GATE_ROWS = 32

_NORMED = ("nq", "nks", "nkw", "wq", "wk")


def _proj_plan(scale_of):
    plan = []
    for kind, heads in (("row", ROW_HEADS), ("col", COL_HEADS)):
        for p in range(len(heads) // 2):
            name = heads[2 * p][0]
            assert heads[2 * p + 1][0] == name
            plan.append((kind, p, name in _NORMED, 1.0 if name in _NORMED else scale_of.get(name, 1.0)))
    plan.append(("gate", 0, False, 1.0))
    return tuple(plan)


def _proj_kernel(x_ref, nw_ref, w_ref, gain_ref, bd_ref, row_ref, col_ref, gate_ref, *, plan):
    h = _rms_rows(x_ref[...], nw_ref[...]).astype(BF16)
    pair = 2 * LANES
    for c0 in range(0, len(plan) * LANES, pair):
        width = min(pair, len(plan) * LANES - c0)
        zg = _dot(h, w_ref[:, c0:c0 + width])
        for u in range(width // LANES):
            t = c0 // LANES + u
            kind, p, normed, scale = plan[t]
            z = zg[:, u * LANES:(u + 1) * LANES]
            if normed:
                ms = _dot((z * z).astype(BF16), bd_ref[...])
                z = z * lax.rsqrt(ms + NORM_EPS) * gain_ref[:, t * LANES:(t + 1) * LANES]
            elif scale != 1.0:
                z = z * scale
            if kind == "row":
                row_ref[2 * p] = z[:, :HEAD_DIM].astype(row_ref.dtype)
                row_ref[2 * p + 1] = z[:, HEAD_DIM:].astype(row_ref.dtype)
            elif kind == "col":
                z_t = z.T
                for c in range(PROJ_TM // LANES):
                    col_ref[2 * p, c] = z_t[:HEAD_DIM, c * LANES:(c + 1) * LANES].astype(col_ref.dtype)
                    col_ref[2 * p + 1, c] = z_t[HEAD_DIM:, c * LANES:(c + 1) * LANES].astype(col_ref.dtype)
            else:
                gate_ref[...] = jax.nn.sigmoid(z).T[:GATE_ROWS]


def _proj(x2, nw, w, gain, plan):
    T = x2.shape[0]
    n_row, n_col = len(ROW_HEADS), len(COL_HEADS)
    bd = np.kron(np.eye(2, dtype=np.float32), np.full((HEAD_DIM, HEAD_DIM), 1.0 / HEAD_DIM, np.float32))
    return pl.pallas_call(
        functools.partial(_proj_kernel, plan=plan),
        grid=(T // PROJ_TM,),
        in_specs=[
            pl.BlockSpec((PROJ_TM, D_MODEL), lambda i: (i, 0)),
            pl.BlockSpec((1, D_MODEL), lambda i: (0, 0)),
            pl.BlockSpec(w.shape, lambda i: (0, 0)),
            pl.BlockSpec(gain.shape, lambda i: (0, 0)),
            pl.BlockSpec((LANES, LANES), lambda i: (0, 0)),
        ],
        out_specs=[
            pl.BlockSpec((n_row, PROJ_TM, HEAD_DIM), lambda i: (0, i, 0)),
            pl.BlockSpec((n_col, PROJ_TM // LANES, HEAD_DIM, LANES), lambda i: (0, i, 0, 0)),
            pl.BlockSpec((GATE_ROWS, PROJ_TM), lambda i: (0, i)),
        ],
        out_shape=[jax.ShapeDtypeStruct((n_row, T, HEAD_DIM), BF16),
                   jax.ShapeDtypeStruct((n_col, T // LANES, HEAD_DIM, LANES), BF16),
                   jax.ShapeDtypeStruct((GATE_ROWS, T), F32)],
        compiler_params=_cparams(("parallel",)),
        name="proj",
    )(x2, nw.reshape(1, D_MODEL), w, gain, jnp.asarray(bd, BF16))


def _compress_kernel(r_ref, w1_ref, w2_ref, pos_ref, kn_ref, row_ref, col_ref):
    kind = pl.program_id(0) // NSA_KV_HEADS
    r = r_ref[0, 0]
    nc = r.shape[0]
    half = CMP_STRIDE * HEAD_DIM
    w1 = w1_ref[0]
    p_lo = _dot(r, w1[:half])
    p_hi = _dot(r, w1[half:])
    p_pos = _dot(jnp.broadcast_to(pos_ref[...], (8, CMP_LEN * HEAD_DIM)).astype(BF16), w1)[:1]
    hdn = p_lo + pltpu.roll(p_hi, nc - 1, 0) + p_pos
    hdn = hdn * jax.nn.sigmoid(hdn)
    out = _dot(hdn.astype(BF16), w2_ref[0])
    ms = jnp.sum(out * out, axis=-1, keepdims=True) * (1.0 / HEAD_DIM)
    normed = out * lax.rsqrt(ms + NORM_EPS) * kn_ref[...]
    out = jnp.where(kind == 0, normed, out)
    row_ref[0, 0] = out[:, :HEAD_DIM].astype(row_ref.dtype)
    col_ref[0, 0] = out.T[:HEAD_DIM].astype(col_ref.dtype)


def _compress(rows4, w1, w2, pos, k_norm):
    assert ROW_KC == 0 and ROW_VC == NSA_KV_HEADS
    _, B, NC, _ = rows4.shape
    kn = jnp.pad(k_norm.reshape(1, HEAD_DIM), ((0, 0), (0, LANES - HEAD_DIM)))
    w2p = jnp.pad(w2, ((0, 0), (0, 0), (0, LANES - HEAD_DIM))).astype(BF16)
    return pl.pallas_call(
        _compress_kernel,
        grid=(4, B),
        in_specs=[
            pl.BlockSpec((1, 1, NC, CMP_STRIDE * HEAD_DIM), lambda h, b: (h, b, 0, 0)),
            pl.BlockSpec((1, CMP_LEN * HEAD_DIM, CMP_HIDDEN), lambda h, b: (h // NSA_KV_HEADS, 0, 0)),
            pl.BlockSpec((1, CMP_HIDDEN, LANES), lambda h, b: (h // NSA_KV_HEADS, 0, 0)),
            pl.BlockSpec((1, CMP_LEN * HEAD_DIM), lambda h, b: (0, 0)),
            pl.BlockSpec((1, LANES), lambda h, b: (0, 0)),
        ],
        out_specs=[
            pl.BlockSpec((1, 1, NC, HEAD_DIM), lambda h, b: (h, b, 0, 0)),
            pl.BlockSpec((1, 1, HEAD_DIM, NC), lambda h, b: (h, b, 0, 0)),
        ],
        out_shape=[jax.ShapeDtypeStruct((4, B, NC, HEAD_DIM), BF16),
                   jax.ShapeDtypeStruct((4, B, HEAD_DIM, NC), BF16)],
        compiler_params=_cparams(("parallel", "parallel")),
        name="nsa_compress",
    )(rows4, w1.astype(BF16), w2p, pos.reshape(1, CMP_LEN * HEAD_DIM), kn)


def _softmax_tile_update(s, v_t, m, l, acc):
    m_new = jnp.maximum(m, jnp.max(s, axis=0, keepdims=True))
    alpha = jnp.exp2(m - m_new)
    p = jnp.exp2(s - m_new)
    l_new = alpha * l + jnp.sum(p, axis=0, keepdims=True)
    acc_new = alpha * acc + _dot(v_t, p.astype(BF16))
    return m_new, l_new, acc_new


def _pairs_to_rows(o_t):
    n = o_t.shape[1] // LANES
    outs = []
    for p in range(n // 2):
        a = o_t[:, (2 * p) * LANES:(2 * p + 1) * LANES]
        b = o_t[:, (2 * p + 1) * LANES:(2 * p + 2) * LANES]
        outs.append(jnp.concatenate([a, b], axis=0).T)
    return outs[0] if len(outs) == 1 else jnp.concatenate(outs, axis=1)


FAR_TILES = 4
NSA_SUB = 2
TAKEN = -2.0
N_FORCED = 3
LOG2E = math.log2(math.e)
NSA_TINY = 2.0 ** -100
BOUND_SLACK = 1.05


def _nsa_kernel(q_ref, kc_ref, vc_ref, ks_ref, vs_ref, kw_ref, vw_ref, gate_ref,
                cbias_ref, sbias_ref, wbias_ref, ov_ref, msel_ref, o_ref, *scratch, nb):
    neg_refs, stage_refs = scratch[:NSA_SUB], scratch[NSA_SUB:]
    refs = (q_ref, kc_ref, vc_ref, ks_ref, vs_ref, kw_ref, vw_ref, gate_ref, cbias_ref, sbias_ref, wbias_ref,
            ov_ref, msel_ref)
    finish = [_nsa_block(u, refs, neg_refs[u], stage_refs, nb) for u in range(NSA_SUB)]
    for u in range(NSA_SUB):
        o_ref[0, u * Q_BLOCK:(u + 1) * Q_BLOCK, :] = finish[u]().astype(o_ref.dtype)


def _nsa_block(u, refs, neg_ref, stage_refs, nb):
    (q_ref, kc_ref, vc_ref, ks_ref, vs_ref, kw_ref, vw_ref, gate_ref, cbias_ref, sbias_ref, wbias_ref,
     ov_ref, msel_ref) = refs
    i = pl.program_id(2) * NSA_SUB + u
    R = NSA_REP
    N = R * Q_BLOCK
    nc = kc_ref.shape[2]
    q_t = jnp.concatenate([q_ref[r, u] for r in range(R)], axis=1)
    n_win = NSA_WINDOW // Q_BLOCK

    def k_rows(ref, J0, n_tiles):
        return ref[0, pl.ds(pl.multiple_of(J0 * Q_BLOCK, Q_BLOCK), n_tiles * Q_BLOCK), :]

    def v_cols(ref, J0, n_tiles):
        return jnp.concatenate([ref[0, J0 + u] for u in range(n_tiles)], axis=1)

    def sel_mask(J0, n_tiles):
        parts = [jnp.broadcast_to(neg_ref[pl.ds(2 * J0 + u, 1), :], (SLC_LEN, Q_BLOCK)) for u in range(2 * n_tiles)]
        return jnp.concatenate([jnp.concatenate(parts, axis=0)] * R, axis=1)

    def compressed(fast):
        cb = cbias_ref[0, pl.ds(pl.multiple_of(8 * (nb - 1) - 8 * i, 8), nc), :]
        s = _dot(kc_ref[0, 0], q_t) + cb
        if fast:
            e = jnp.exp2(s)
        else:
            e = jnp.where(s > 0.5 * NEG_INF, jnp.exp2(s - jnp.max(s, axis=0, keepdims=True)), 0.0)
        l = jnp.sum(e, axis=0, keepdims=True)
        p = e * jnp.where(l > 0.0, 1.0 / l, 0.0)
        return p, l, _dot(vc_ref[0, 0], p.astype(BF16))

    def select_blocks(p_c):
        p_sum = p_c[:, 0:Q_BLOCK]
        for r in range(1, R):
            p_sum = p_sum + p_c[:, r * Q_BLOCK:(r + 1) * Q_BLOCK]
        p_hi = p_sum.astype(BF16)
        p_lo = (p_sum - p_hi.astype(F32)).astype(BF16)
        imp = _dot(ov_ref[...], p_hi) + _dot(ov_ref[...], p_lo)
        n_slc = imp.shape[0]
        jj = lax.broadcasted_iota(jnp.int32, imp.shape, 0)
        qq = lax.broadcasted_iota(jnp.int32, imp.shape, 1)
        cur = 2 * i + jnp.where(qq >= SLC_LEN, 1, 0)
        forced = (jj == 0) | (jj == cur) | (jj == cur - 1)
        score = jnp.where(forced, TAKEN, jnp.where(jj <= cur, imp, -1.0))
        grp = 8
        j_rows = [jj[a:a + grp] for a in range(0, n_slc, grp)]
        for _ in range(min(SLC_TOPN, n_slc) - N_FORCED):
            cand = [(score[a:a + grp], j_rows[a // grp]) for a in range(0, n_slc, grp)]
            while len(cand) > 1:
                nxt = []
                for a in range(0, len(cand) - 1, 2):
                    (s0, j0), (s1, j1) = cand[a], cand[a + 1]
                    right = s1 > s0
                    nxt.append((jnp.maximum(s0, s1), jnp.where(right, j1, j0)))
                cand = nxt + cand[len(cand) - len(cand) % 2:]
            s8, j8 = cand[0]
            best = jnp.max(s8, axis=0, keepdims=True)
            first = jnp.min(jnp.where(s8 == best, j8, n_slc), axis=0, keepdims=True)
            score = jnp.where(jj == first, TAKEN, score)
        neg_ref[...] = jnp.where(score == TAKEN, msel_ref[0], NEG_INF)

    def add_chunk(st, s, v_t):
        e = jnp.exp2(s)
        return st[0] + jnp.sum(e, axis=0, keepdims=True), st[1] + _dot(v_t, e.astype(BF16))

    zero = (jnp.zeros((1, N), F32), jnp.zeros((HEAD_DIM, N), F32))

    def selected_near():
        J0 = jnp.maximum(i - 1, 0)
        bias = sbias_ref[0, pl.ds(pl.multiple_of((2 - (i - J0)) * Q_BLOCK, Q_BLOCK), 2 * Q_BLOCK), :]
        return add_chunk(zero, _dot(k_rows(ks_ref, J0, 2), q_t) + bias + sel_mask(J0, 2), v_cols(vs_ref, J0, 2))

    def selected_far(st):
        n_far = jnp.maximum(i - 1, 0)

        def far_chunk(J0, st, n_valid):
            mask = sel_mask(J0, FAR_TILES)
            if n_valid is not None:
                row = lax.broadcasted_iota(jnp.int32, mask.shape, 0)
                mask = jnp.where(row < n_valid * Q_BLOCK, mask, NEG_INF)
            return add_chunk(st, _dot(k_rows(ks_ref, J0, FAR_TILES), q_t) + mask, v_cols(vs_ref, J0, FAR_TILES))

        n_chunks = n_far // FAR_TILES
        rem = n_far - FAR_TILES * n_chunks

        def tile_of(c):
            return jnp.clip(n_far - FAR_TILES * (c + 1), 0, nb - FAR_TILES)

        def scores(c):
            return _dot(k_rows(ks_ref, tile_of(c), FAR_TILES), q_t)

        def values(c, e_ref, acc):
            return acc + _dot(v_cols(vs_ref, tile_of(c), FAR_TILES), e_ref[...])

        def stage(c, st, s_in, s_out, e_in, e_out):
            s_out[...] = scores(c + 1)
            acc = values(c - 1, e_in, st[1])
            e = jnp.exp2(s_in[...] + sel_mask(tile_of(c), FAR_TILES))
            e_out[...] = e.astype(BF16)
            return st[0] + jnp.sum(e, axis=0, keepdims=True), acc

        s_a, s_b, e_a, e_b = stage_refs
        e_b[...] = jnp.zeros_like(e_b)
        s_a[...] = scores(0)

        def pair(c, st):
            st = stage(c, st, s_a, s_b, e_b, e_a)
            return stage(c + 1, st, s_b, s_a, e_a, e_b)

        n_quads = n_chunks // 4
        st = lax.fori_loop(0, n_quads, lambda t, st: pair(4 * t + 2, pair(4 * t, st)), st)
        st = lax.cond(n_chunks % 4 >= 2, lambda st: pair(4 * n_quads, st), lambda st: st, st)

        def odd_tail(st):
            st = stage(n_chunks - 1, st, s_a, s_b, e_b, e_a)
            return st[0], values(n_chunks - 1, e_a, st[1])

        st = lax.cond(n_chunks % 2 == 1, odd_tail, lambda st: (st[0], values(n_chunks - 1, e_b, st[1])), st)
        return lax.cond(rem > 0, lambda st: far_chunk(0, st, rem), lambda st: st, st)

    def window_fast():
        J0 = jnp.maximum(i - n_win, 0)
        bias = wbias_ref[0, pl.ds(pl.multiple_of((n_win - (i - J0)) * Q_BLOCK, Q_BLOCK), (n_win + 1) * Q_BLOCK), :]
        return add_chunk(zero, _dot(k_rows(kw_ref, J0, n_win + 1), q_t) + bias, v_cols(vw_ref, J0, n_win + 1))

    def sweep_exact(k_ref, v_ref, bias_ref, far_dist, n_back, with_sel):
        def body(t, st):
            J = i - t
            off = pl.multiple_of((far_dist - jnp.minimum(t, far_dist)) * Q_BLOCK, Q_BLOCK)
            s = _dot(k_rows(k_ref, J, 1), q_t) + bias_ref[0, pl.ds(off, Q_BLOCK), :]
            if with_sel:
                s = s + sel_mask(J, 1)
            return _softmax_tile_update(s, v_ref[0, J], *st)

        init = (jnp.full((1, N), NEG_INF, F32),) + zero
        _, l, acc = lax.fori_loop(0, jnp.minimum(i, n_back) + 1, body, init)
        return l, acc

    p_c, l_c, o_c = compressed(True)
    select_blocks(p_c)
    l_w, a_w = window_fast()
    near = selected_near()

    def exact_path():
        p_x, _, o_x = compressed(False)
        select_blocks(p_x)
        l_sx, a_sx = sweep_exact(ks_ref, vs_ref, sbias_ref, 2, nb, True)
        l_wx, a_wx = sweep_exact(kw_ref, vw_ref, wbias_ref, n_win, n_win, False)
        return o_x, a_sx * (1.0 / l_sx), a_wx * (1.0 / l_wx)

    def gate_row(br):
        return jnp.concatenate([gate_ref[br, 0, r:r + 1, u * Q_BLOCK:(u + 1) * Q_BLOCK] for r in range(R)], axis=1)

    def finish():
        l_s, a_s = selected_far(near)
        lane_q = lax.broadcasted_iota(jnp.int32, (1, N), 1) % Q_BLOCK
        has_cmp = (i > 0) | (lane_q >= CMP_LEN - 1)
        l_min = jnp.minimum(jnp.minimum(l_s, l_w), jnp.where(has_cmp, l_c, 1.0))
        o_cx, o_s, o_w = lax.cond(jnp.min(l_min) > NSA_TINY,
                                  lambda: (o_c, a_s * (1.0 / l_s), a_w * (1.0 / l_w)), exact_path)
        o_t = gate_row(0) * o_cx + gate_row(1) * o_s + gate_row(2) * o_w
        return _pairs_to_rows(o_t)

    return finish


def _nsa_attention(colmat, rowmat, cmp_rows, cmp_cols, gates, cbias, sbias, wbias, ov_t, msel, B, S):
    nb = S // Q_BLOCK
    nc = S // CMP_STRIDE
    n_slc = S // SLC_LEN
    G, R = NSA_KV_HEADS, NSA_REP
    kern = functools.partial(_nsa_kernel, nb=nb)
    ns = nb // NSA_SUB
    return pl.pallas_call(
        kern,
        grid=(B, G, ns),
        in_specs=[
            pl.BlockSpec((R, NSA_SUB, HEAD_DIM, LANES), lambda b, g, i: (g, b * ns + i, 0, 0)),
            pl.BlockSpec((1, 1, nc, HEAD_DIM), lambda b, g, i: (g, b, 0, 0)),
            pl.BlockSpec((1, 1, HEAD_DIM, nc), lambda b, g, i: (NSA_KV_HEADS + g, b, 0, 0)),
            pl.BlockSpec((1, S, HEAD_DIM), lambda b, g, i: (ROW_KS + g, b, 0)),
            pl.BlockSpec((1, nb, HEAD_DIM, LANES), lambda b, g, i: (COL_VS + g, b, 0, 0)),
            pl.BlockSpec((1, S, HEAD_DIM), lambda b, g, i: (ROW_KW + g, b, 0)),
            pl.BlockSpec((1, nb, HEAD_DIM, LANES), lambda b, g, i: (COL_VW + g, b, 0, 0)),
            pl.BlockSpec((N_BRANCHES, 1, R, NSA_SUB * LANES), lambda b, g, i: (0, g, 0, b * ns + i)),
            pl.BlockSpec((1, cbias.shape[1], R * Q_BLOCK), lambda b, g, i: (g, 0, 0)),
            pl.BlockSpec((1, sbias.shape[1], R * Q_BLOCK), lambda b, g, i: (g, 0, 0)),
            pl.BlockSpec((1, wbias.shape[1], R * Q_BLOCK), lambda b, g, i: (g, 0, 0)),
            pl.BlockSpec((n_slc, nc), lambda b, g, i: (0, 0)),
            pl.BlockSpec((1, 1, Q_BLOCK), lambda b, g, i: (g, 0, 0)),
        ],
        out_specs=pl.BlockSpec((1, NSA_SUB * Q_BLOCK, R * HEAD_DIM), lambda b, g, i: (b, i, g)),
        out_shape=jax.ShapeDtypeStruct((B, S, NSA_HEADS * HEAD_DIM), BF16),
        scratch_shapes=[pltpu.VMEM((n_slc, Q_BLOCK), F32)] * NSA_SUB
        + [pltpu.VMEM((FAR_TILES * Q_BLOCK, R * Q_BLOCK), F32)] * 2
        + [pltpu.VMEM((FAR_TILES * Q_BLOCK, R * Q_BLOCK), BF16)] * 2,
        compiler_params=_cparams(("parallel", "parallel", "arbitrary")),
        name="nsa_attention",
    )(colmat, cmp_rows, cmp_cols, rowmat, colmat, rowmat, colmat, gates, cbias, sbias, wbias, ov_t, msel)


SB_QB = 512
SB_KT = 256
SB_ZERO_LOG = -160.0


def _sb_kernel(q_ref, k_ref, v_ref, tri_ref, o_ref):
    i = pl.program_id(1)
    H = SB_HEADS
    tri = tri_ref[...]
    assert SB_QB == 2 * SB_KT
    below_diag = (lax.broadcasted_iota(jnp.int32, (SB_KT, SB_KT), 1)
                  < lax.broadcasted_iota(jnp.int32, (SB_KT, SB_KT), 0))

    def step(J, rows, st, masked):
        out = []
        for h in range(H):
            carry, acc = st[h]
            k_t = jnp.concatenate([k_ref[h, (SB_KT // LANES) * J + c] for c in range(SB_KT // LANES)], axis=1)
            z = _dot(q_ref[h, rows, :], k_t)
            sp = jnp.maximum(z, 0.0) + jnp.log2(1.0 + jnp.exp2(-jnp.abs(z)))
            lf = jnp.where(below_diag, -sp, 0.0) if masked else -sp
            lf_hi = lf.astype(BF16)
            lf_lo = (lf - lf_hi.astype(F32)).astype(BF16)
            after = _dot(lf_hi, tri) + _dot(lf_lo, tri) + carry
            w = jnp.exp2(z - sp + after)
            if masked:
                w = jnp.where(below_diag, w, 0.0)
            v = v_ref[h, pl.ds(pl.multiple_of(J * SB_KT, SB_KT), SB_KT), :]
            acc = acc + _dot(w.astype(BF16), v)
            carry = carry + jnp.sum(lf, axis=1, keepdims=True)
            out.append((carry, acc))
        return tuple(out)

    lo, hi = slice(0, SB_KT), slice(SB_KT, SB_QB)
    zero = tuple((jnp.zeros((SB_KT, 1), F32), jnp.zeros((SB_KT, HEAD_DIM), F32)) for _ in range(H))
    st_hi = step(2 * i, hi, step(2 * i + 1, hi, zero, True), False)
    st_lo = step(2 * i, lo, zero, True)
    st = tuple(tuple(jnp.concatenate([a, b], axis=0) for a, b in zip(st_lo[h], st_hi[h])) for h in range(H))
    per_q = SB_QB // SB_KT
    all_rows = slice(0, SB_QB)

    def alive(st):
        worst = st[0][0]
        for h in range(1, H):
            worst = jnp.maximum(worst, st[h][0])
        return jnp.max(worst) > SB_ZERO_LOG

    def body(c):
        J, _, st = c
        st = step(J, all_rows, st, False)
        return J - 1, alive(st), st

    _, _, st = lax.while_loop(lambda c: (c[0] >= 0) & c[1], body, (per_q * i - 1, alive(st), st))
    o_ref[0] = jnp.concatenate([st[h][1] for h in range(H)], axis=1).astype(o_ref.dtype)


def _sb_attention(colmat, rowmat, tri, B, S):
    nq = S // SB_QB
    nb = S // LANES
    H = SB_HEADS
    return pl.pallas_call(
        _sb_kernel,
        grid=(B, nq),
        in_specs=[
            pl.BlockSpec((H, SB_QB, HEAD_DIM), lambda b, i: (ROW_SQ // H, b * nq + i, 0)),
            pl.BlockSpec((H, nb, HEAD_DIM, LANES), lambda b, i: (COL_SK // H, b, 0, 0)),
            pl.BlockSpec((H, S, HEAD_DIM), lambda b, i: (ROW_SV // H, b, 0)),
            pl.BlockSpec((SB_KT, SB_KT), lambda b, i: (0, 0)),
        ],
        out_specs=pl.BlockSpec((1, SB_QB, H * HEAD_DIM), lambda b, i: (b, i, 0)),
        out_shape=jax.ShapeDtypeStruct((B, S, H * HEAD_DIM), BF16),
        compiler_params=_cparams(("parallel", "arbitrary")),
        name="sb_attention",
    )(rowmat, colmat, rowmat, tri)


SWA_SUB = 4


def _swa_kernel(q_ref, k_ref, v_ref, bias_ref, sink_ref, o_ref):
    R = SWA_REP
    sink = sink_ref[0]
    for u in range(SWA_SUB):
        i = pl.program_id(2) * SWA_SUB + u
        q_t = jnp.concatenate([q_ref[r, u] for r in range(R)], axis=1)
        J0 = jnp.maximum(i - 1, 0)
        k = k_ref[0, pl.ds(pl.multiple_of(J0 * Q_BLOCK, Q_BLOCK), 2 * Q_BLOCK), :]
        bias = bias_ref[0, pl.ds(pl.multiple_of((1 - (i - J0)) * Q_BLOCK, Q_BLOCK), 2 * Q_BLOCK), :]
        s = _dot(k, q_t) + bias
        m = jnp.maximum(jnp.max(s, axis=0, keepdims=True), sink)
        e = jnp.exp(s - m)
        den = jnp.sum(e, axis=0, keepdims=True) + jnp.exp(sink - m)
        v_t = jnp.concatenate([v_ref[0, J0], v_ref[0, J0 + 1]], axis=1)
        o_t = _dot(v_t, e.astype(BF16)) * (1.0 / den)
        o_ref[0, u * Q_BLOCK:(u + 1) * Q_BLOCK, :] = _pairs_to_rows(o_t).astype(o_ref.dtype)


def _swa_attention(colmat, rowmat, bias, sinks, B, S):
    nb = S // Q_BLOCK
    ns = nb // SWA_SUB
    G, R = SWA_KV_HEADS, SWA_REP
    return pl.pallas_call(
        _swa_kernel,
        grid=(B, G, ns),
        in_specs=[
            pl.BlockSpec((R, SWA_SUB, HEAD_DIM, LANES), lambda b, g, i: (COL_WQ // R + g, b * ns + i, 0, 0)),
            pl.BlockSpec((1, S, HEAD_DIM), lambda b, g, i: (ROW_WK + g, b, 0)),
            pl.BlockSpec((1, nb, HEAD_DIM, LANES), lambda b, g, i: (COL_WV + g, b, 0, 0)),
            pl.BlockSpec((1, 3 * Q_BLOCK, R * Q_BLOCK), lambda b, g, i: (g, 0, 0)),
            pl.BlockSpec((1, 1, R * Q_BLOCK), lambda b, g, i: (g, 0, 0)),
        ],
        out_specs=pl.BlockSpec((1, SWA_SUB * Q_BLOCK, R * HEAD_DIM), lambda b, g, i: (b, i, g)),
        out_shape=jax.ShapeDtypeStruct((B, S, SWA_HEADS * HEAD_DIM), BF16),
        compiler_params=_cparams(("parallel", "parallel", "arbitrary")),
        name="swa_attention",
    )(colmat, rowmat, colmat, bias, sinks)


MERGE_TM = 512


def _merge_kernel(x_ref, nw_ref, wg_ref, ya_ref, yb_ref, yc_ref, ua_ref, ub_ref, uc_ref, wo_ref, o_ref):
    x = x_ref[...]
    h = _rms_rows(x, nw_ref[...]).astype(BF16)
    merged = None
    for br, (y_ref, u_ref) in enumerate(((ya_ref, ua_ref), (yb_ref, ub_ref), (yc_ref, uc_ref))):
        g = jax.nn.sigmoid(_dot(h, wg_ref[:, br * D_MODEL:(br + 1) * D_MODEL]))
        t = g * _dot(y_ref[...], u_ref[...])
        merged = t if merged is None else merged + t
    o_ref[...] = x + _dot(merged.astype(BF16), wo_ref[...])


def _merge(x2, nw, w_bgate, y_nsa, y_sb, y_swa, u_nsa, u_sb, u_swa, w_out):
    T = x2.shape[0]
    full = lambda shape: pl.BlockSpec(shape, lambda i: (0, 0))
    rows = lambda n: pl.BlockSpec((MERGE_TM, n), lambda i: (i, 0))
    return pl.pallas_call(
        _merge_kernel,
        grid=(T // MERGE_TM,),
        in_specs=[rows(D_MODEL), full((1, D_MODEL)), full((D_MODEL, N_BRANCHES * D_MODEL)),
                  rows(y_nsa.shape[1]), rows(y_sb.shape[1]), rows(y_swa.shape[1]),
                  full(u_nsa.shape), full(u_sb.shape), full(u_swa.shape), full((D_MODEL, D_MODEL))],
        out_specs=rows(D_MODEL),
        out_shape=jax.ShapeDtypeStruct((T, D_MODEL), F32),
        compiler_params=_cparams(("parallel",)),
        name="merge",
    )(x2, nw.reshape(1, D_MODEL), w_bgate.astype(BF16), y_nsa, y_sb, y_swa,
      u_nsa.astype(BF16), u_sb.astype(BF16), u_swa.astype(BF16), w_out.astype(BF16))


def _t5_bucket(dist):
    max_exact = NUM_BUCKETS // 2
    d = jnp.maximum(dist, 0)
    df = jnp.maximum(d, 1).astype(F32)
    large = max_exact + (jnp.log(df / max_exact) / math.log(MAX_DISTANCE / max_exact)
                         * (NUM_BUCKETS - max_exact)).astype(jnp.int32)
    large = jnp.minimum(large, NUM_BUCKETS - 1)
    return jnp.where(d < max_exact, d, large)


def _toeplitz(ext, delta):
    H = ext.shape[0]
    n = Q_BLOCK
    c = delta + MAX_DISTANCE
    w = ext[:, c - (n - 1):c + n]
    flat = jnp.pad(jnp.broadcast_to(w[:, None, :], (H, n, 2 * n - 1)), ((0, 0), (0, 0), (0, 1))).reshape(H, 2 * n * n)
    return flat[:, n - 1:n - 1 + n * (2 * n - 1)].reshape(H, n, 2 * n - 1)[:, :, :n]


def _lanes(t, G, R):
    K, Q = t.shape[1:]
    return t.reshape(G, R, K, Q).transpose(0, 2, 1, 3).reshape(G, K, R * Q)


def _bias_tables(rel_bias, nb, qk_bound):
    n = Q_BLOCK
    G, R = NSA_KV_HEADS, NSA_REP
    by_dist = rel_bias[_t5_bucket(jnp.arange(MAX_DISTANCE)), :].T.astype(F32)
    far = rel_bias[NUM_BUCKETS - 1, :].astype(F32)
    neg = jnp.full((1, n), NEG_INF, F32)
    k = jnp.arange(n)[:, None]
    q = jnp.arange(n)[None, :]
    above = (k > q)[None]

    rel = (by_dist[:NSA_HEADS] - far[:NSA_HEADS, None]) * LOG2E
    m_g = qk_bound + jnp.maximum(jnp.max(rel.reshape(G, -1), axis=1), 0.0)
    m_h = jnp.repeat(m_g, R)[:, None, None]
    ext = jnp.concatenate([jnp.broadcast_to(neg, (NSA_HEADS, n)), rel, jnp.zeros((NSA_HEADS, n), F32)], axis=1)
    t0, t1 = _toeplitz(ext, 0), _toeplitz(ext, n)
    zeros = jnp.zeros_like(t0)
    masked = jnp.full_like(t0, NEG_INF)
    sbias = _lanes(jnp.concatenate([zeros, t1, t0, masked], axis=1), G, R)
    n_win = NSA_WINDOW // n
    edge = jnp.where(above, zeros, NEG_INF)
    tiles = [edge] + [zeros] * (n_win - 2) + [t1, t0]
    wbias = _lanes(jnp.concatenate([t - m_h for t in tiles] + [masked] * n_win, axis=1), G, R)
    off = 8 * (nb - 1)
    near = jnp.stack([ext[:, 97 - CMP_STRIDE * m:97 - CMP_STRIDE * m + n] for m in range(-9, 7)], axis=1)
    n_rows = 16 * nb - 8
    cb = jnp.concatenate([jnp.zeros((NSA_HEADS, off - 9, n), F32), near,
                          jnp.full((NSA_HEADS, n_rows - off - 7, n), NEG_INF, F32)], axis=1) - m_h
    cbias = _lanes(cb, G, R)
    msel = jnp.broadcast_to(-m_g[:, None, None], (G, 1, n))

    hs = slice(NSA_HEADS, NSA_HEADS + SWA_HEADS)
    ext_w = jnp.concatenate([jnp.broadcast_to(neg, (SWA_HEADS, n)), by_dist[hs],
                             jnp.broadcast_to(far[hs, None], (SWA_HEADS, n))], axis=1)
    s0 = _toeplitz(ext_w, 0)
    s1 = jnp.where(above, _toeplitz(ext_w, n), NEG_INF)
    swa_bias = _lanes(jnp.concatenate([s1, s0, jnp.full_like(s0, NEG_INF)], axis=1), SWA_KV_HEADS, SWA_REP)
    return cbias, sbias, wbias, msel, swa_bias


def _overlap_t(S):
    nc, n_slc = S // CMP_STRIDE, S // SLC_LEN
    n = np.arange(nc)[None, :]
    j = np.arange(n_slc)[:, None]
    ov = (n * CMP_STRIDE < (j + 1) * SLC_LEN) & (n * CMP_STRIDE + CMP_LEN - 1 >= j * SLC_LEN) & (n < nc - 1)
    return jnp.asarray(ov, BF16)


def _head_gains(heads, norm_of, scale_of):
    ones = jnp.ones((HEAD_DIM,), F32)
    gains = [norm_of[name].astype(F32) * scale_of.get(name, 1.0) if name in norm_of else ones for name, _ in heads]
    return jnp.concatenate(gains).reshape(1, -1)


def kernel(x, rel_bias, ffn1_norm, ffn1_w_gate, ffn1_w_up, ffn1_w_down, mix_norm, w_in, nsa_q_norm, nsa_k_norm, nsa_cmp_pos, nsa_cmp_k_w1, nsa_cmp_k_w2, nsa_cmp_v_w1, nsa_cmp_v_w2, swa_q_norm, swa_k_norm, swa_sinks, w_up_nsa, w_up_sb, w_up_swa, w_out, ffn2_norm, ffn2_w_gate, ffn2_w_up, ffn2_w_down):
    B, S, D = x.shape
    T = B * S
    nb = S // Q_BLOCK
    nc = S // CMP_STRIDE
    depth = w_in.shape[0]
    scale = HEAD_DIM ** -0.5
    scale_of = {"nq": scale * LOG2E, "sq": scale * LOG2E, "wq": scale}
    qk_bound = (BOUND_SLACK * LOG2E * HEAD_DIM ** 0.5
                * jnp.max(jnp.abs(nsa_q_norm.astype(F32))) * jnp.max(jnp.abs(nsa_k_norm.astype(F32))))
    cbias, sbias, wbias, msel, swa_bias = _bias_tables(rel_bias, nb, qk_bound)
    plan = _proj_plan(scale_of)
    ov_t = _overlap_t(S)
    tri = jnp.asarray(np.tril(np.ones((SB_KT, SB_KT), np.float32), -1), BF16)

    (ffn1_w_gate, ffn1_w_up, ffn1_w_down, ffn2_w_gate, ffn2_w_up, ffn2_w_down, w_up_nsa, w_up_sb, w_up_swa,
     w_out) = (w.astype(BF16) for w in (ffn1_w_gate, ffn1_w_up, ffn1_w_down, ffn2_w_gate, ffn2_w_up, ffn2_w_down,
                                        w_up_nsa, w_up_sb, w_up_swa, w_out))
    w_bgate = w_in[:, :, OFF_BGATE:].astype(BF16)
    w_gate = jnp.pad(w_in[:, :, OFF_NGATE:OFF_NGATE + N_GATE], ((0, 0), (0, 0), (0, LANES - N_GATE)))
    w_proj = jnp.concatenate([w_in[:, :, off:off + HEAD_DIM] for _, off in ROW_HEADS + COL_HEADS] + [w_gate],
                             axis=2).astype(BF16)

    x2 = x.reshape(T, D)
    for l in range(depth):
        x2 = _ffn(x2, ffn1_norm[l], ffn1_w_gate[l], ffn1_w_up[l], ffn1_w_down[l])

        norm_of = {"nq": nsa_q_norm[l], "nks": nsa_k_norm[l], "nkw": nsa_k_norm[l],
                   "wq": swa_q_norm[l], "wk": swa_k_norm[l]}
        gain = jnp.concatenate([_head_gains(ROW_HEADS, norm_of, scale_of), _head_gains(COL_HEADS, norm_of, scale_of),
                                jnp.ones((1, LANES), F32)], axis=1)
        rowmat, colmat, gates = _proj(x2, mix_norm[l], w_proj[l], gain, plan)
        gates = gates[:N_GATE].reshape(N_BRANCHES, NSA_KV_HEADS, NSA_REP, T)

        rows4 = rowmat[ROW_KC:ROW_KC + 2 * NSA_KV_HEADS].reshape(2 * NSA_KV_HEADS, B, nc, CMP_STRIDE * HEAD_DIM)
        cmp_rows, cmp_cols = _compress(rows4, jnp.stack([nsa_cmp_k_w1[l], nsa_cmp_v_w1[l]]),
                                       jnp.stack([nsa_cmp_k_w2[l], nsa_cmp_v_w2[l]]),
                                       nsa_cmp_pos[l], nsa_k_norm[l])

        y_nsa = _nsa_attention(colmat, rowmat, cmp_rows, cmp_cols, gates, cbias, sbias, wbias, ov_t, msel, B, S)
        y_sb = _sb_attention(colmat, rowmat, tri, B, S)
        sinks = jnp.repeat(swa_sinks[l].astype(F32).reshape(SWA_KV_HEADS, 1, SWA_REP), Q_BLOCK, axis=2)
        y_swa = _swa_attention(colmat, rowmat, swa_bias, sinks, B, S)

        x2 = _merge(x2, mix_norm[l], w_bgate[l], y_nsa.reshape(T, -1), y_sb.reshape(T, -1),
                    y_swa.reshape(T, -1), w_up_nsa[l], w_up_sb[l], w_up_swa[l], w_out[l])
        x2 = _ffn(x2, ffn2_norm[l], ffn2_w_gate[l], ffn2_w_up[l], ffn2_w_down[l])
    return x2.reshape(B, S, D)
```

```python
import functools
import math

import numpy as np
import jax
import jax.numpy as jnp
from jax import lax
from jax.experimental import pallas as pl
from jax.experimental.pallas import tpu as pltpu

D_MODEL = 1024
HEAD_DIM = 64
Q_BLOCK = 128
NSA_HEADS = 8
NSA_KV_HEADS = 2
NSA_REP = NSA_HEADS // NSA_KV_HEADS
CMP_LEN = 32
CMP_STRIDE = 16
CMP_HIDDEN = 256
SLC_LEN = 64
SLC_TOPN = 16
NSA_WINDOW = 512
SB_HEADS = 4
SWA_HEADS = 4
SWA_KV_HEADS = 2
SWA_REP = SWA_HEADS // SWA_KV_HEADS
SWA_WINDOW = 128
NUM_BUCKETS = 32
MAX_DISTANCE = 128
D_FF = 2816
NORM_EPS = 1e-6
NEG_INF = -1e30
N_BRANCHES = 3
N_GATE = 3 * NSA_HEADS

LANES = 128
VMEM_LIMIT = 56 * 1024 * 1024
BF16 = jnp.bfloat16
F32 = jnp.float32

_sizes = (NSA_HEADS * HEAD_DIM,) + (NSA_KV_HEADS * HEAD_DIM,) * 6 + (N_GATE,) \
    + (SB_HEADS * HEAD_DIM,) * 3 + (SWA_HEADS * HEAD_DIM, SWA_KV_HEADS * HEAD_DIM,
                                    SWA_KV_HEADS * HEAD_DIM, N_BRANCHES * D_MODEL)
_offs = np.concatenate([[0], np.cumsum(_sizes)])
(OFF_NQ, OFF_NKC, OFF_NVC, OFF_NKS, OFF_NVS, OFF_NKW, OFF_NVW, OFF_NGATE,
 OFF_SQ, OFF_SK, OFF_SV, OFF_WQ, OFF_WK, OFF_WV, OFF_BGATE) = (int(o) for o in _offs[:-1])

ROW_HEADS = ([("nkc", OFF_NKC + 64 * g) for g in range(2)] + [("nvc", OFF_NVC + 64 * g) for g in range(2)]
             + [("nks", OFF_NKS + 64 * g) for g in range(2)] + [("nkw", OFF_NKW + 64 * g) for g in range(2)]
             + [("sq", OFF_SQ + 64 * h) for h in range(4)] + [("sv", OFF_SV + 64 * h) for h in range(4)]
             + [("wk", OFF_WK + 64 * g) for g in range(2)])
ROW_KC, ROW_VC, ROW_KS, ROW_KW, ROW_SQ, ROW_SV, ROW_WK = 0, 2, 4, 6, 8, 12, 16
COL_HEADS = ([("nq", OFF_NQ + 64 * h) for h in range(8)] + [("nvs", OFF_NVS + 64 * g) for g in range(2)]
             + [("nvw", OFF_NVW + 64 * g) for g in range(2)] + [("sk", OFF_SK + 64 * h) for h in range(4)]
             + [("wq", OFF_WQ + 64 * h) for h in range(4)] + [("wv", OFF_WV + 64 * g) for g in range(2)])
COL_NQ, COL_VS, COL_VW, COL_SK, COL_WQ, COL_WV = 0, 8, 10, 12, 16, 20


def _dot(a, b):
    return jnp.dot(a, b, preferred_element_type=F32)


def _rms_rows(x, w):
    ms = jnp.mean(x * x, axis=-1, keepdims=True)
    return x * lax.rsqrt(ms + NORM_EPS) * w


def _cparams(sem, flags=None):
    return pltpu.CompilerParams(dimension_semantics=sem, vmem_limit_bytes=VMEM_LIMIT, flags=flags)


FFN_TM = 512
MXU_TILE = 256
FFN_FC = MXU_TILE


def _ffn_kernel(x_ref, nw_ref, wg_ref, wu_ref, wd_ref, o_ref):
    x = x_ref[...]
    h = _rms_rows(x, nw_ref[...]).astype(BF16)
    acc = jnp.zeros(x.shape, F32)
    for c in range(D_FF // FFN_FC):
        cols = slice(c * FFN_FC, (c + 1) * FFN_FC)
        g = _dot(h, wg_ref[:, cols])
        u = _dot(h, wu_ref[:, cols])
        a = (g * jax.nn.sigmoid(g)) * u
        acc = acc + _dot(a.astype(BF16), wd_ref[cols, :])
    o_ref[...] = x + 0.5 * acc


def _ffn(x2, nw, wg, wu, wd):
    T = x2.shape[0]
    resident = lambda shape: pl.BlockSpec(shape, lambda i: (0, 0), pipeline_mode=pl.Buffered(1))
    return pl.pallas_call(
        _ffn_kernel,
        grid=(T // FFN_TM,),
        in_specs=[
            pl.BlockSpec((FFN_TM, D_MODEL), lambda i: (i, 0)),
            pl.BlockSpec((1, D_MODEL), lambda i: (0, 0)),
            resident((D_MODEL, D_FF)), resident((D_MODEL, D_FF)), resident((D_FF, D_MODEL)),
        ],
        out_specs=pl.BlockSpec((FFN_TM, D_MODEL), lambda i: (i, 0)),
        out_shape=jax.ShapeDtypeStruct((T, D_MODEL), F32),
        compiler_params=_cparams(("parallel",)),
        name="ffn",
    )(x2, nw.reshape(1, D_MODEL), wg.astype(BF16), wu.astype(BF16), wd.astype(BF16))


PROJ_TM = 512


GATE_ROWS = 32

_NORMED = ("nq", "nks", "nkw", "wq", "wk")


def _proj_plan(scale_of):
    plan = []
    for kind, heads in (("row", ROW_HEADS), ("col", COL_HEADS)):
        for p in range(len(heads) // 2):
            name = heads[2 * p][0]
            assert heads[2 * p + 1][0] == name
            plan.append((kind, p, name in _NORMED, 1.0 if name in _NORMED else scale_of.get(name, 1.0)))
    plan.append(("gate", 0, False, 1.0))
    return tuple(plan)


def _proj_kernel(x_ref, nw_ref, w_ref, gain_ref, bd_ref, row_ref, col_ref, gate_ref, *, plan):
    h = _rms_rows(x_ref[...], nw_ref[...]).astype(BF16)
    pair = 2 * LANES
    for c0 in range(0, len(plan) * LANES, pair):
        width = min(pair, len(plan) * LANES - c0)
        zg = _dot(h, w_ref[:, c0:c0 + width])
        for u in range(width // LANES):
            t = c0 // LANES + u
            kind, p, normed, scale = plan[t]
            z = zg[:, u * LANES:(u + 1) * LANES]
            if normed:
                ms = _dot((z * z).astype(BF16), bd_ref[...])
                z = z * lax.rsqrt(ms + NORM_EPS) * gain_ref[:, t * LANES:(t + 1) * LANES]
            elif scale != 1.0:
                z = z * scale
            if kind == "row":
                row_ref[2 * p] = z[:, :HEAD_DIM].astype(row_ref.dtype)
                row_ref[2 * p + 1] = z[:, HEAD_DIM:].astype(row_ref.dtype)
            elif kind == "col":
                z_t = z.T
                for c in range(PROJ_TM // LANES):
                    col_ref[2 * p, c] = z_t[:HEAD_DIM, c * LANES:(c + 1) * LANES].astype(col_ref.dtype)
                    col_ref[2 * p + 1, c] = z_t[HEAD_DIM:, c * LANES:(c + 1) * LANES].astype(col_ref.dtype)
            else:
                gate_ref[...] = jax.nn.sigmoid(z).T[:GATE_ROWS]


def _proj(x2, nw, w, gain, plan):
    T = x2.shape[0]
    n_row, n_col = len(ROW_HEADS), len(COL_HEADS)
    bd = np.kron(np.eye(2, dtype=np.float32), np.full((HEAD_DIM, HEAD_DIM), 1.0 / HEAD_DIM, np.float32))
    return pl.pallas_call(
        functools.partial(_proj_kernel, plan=plan),
        grid=(T // PROJ_TM,),
        in_specs=[
            pl.BlockSpec((PROJ_TM, D_MODEL), lambda i: (i, 0)),
            pl.BlockSpec((1, D_MODEL), lambda i: (0, 0)),
            pl.BlockSpec(w.shape, lambda i: (0, 0)),
            pl.BlockSpec(gain.shape, lambda i: (0, 0)),
            pl.BlockSpec((LANES, LANES), lambda i: (0, 0)),
        ],
        out_specs=[
            pl.BlockSpec((n_row, PROJ_TM, HEAD_DIM), lambda i: (0, i, 0)),
            pl.BlockSpec((n_col, PROJ_TM // LANES, HEAD_DIM, LANES), lambda i: (0, i, 0, 0)),
            pl.BlockSpec((GATE_ROWS, PROJ_TM), lambda i: (0, i)),
        ],
        out_shape=[jax.ShapeDtypeStruct((n_row, T, HEAD_DIM), BF16),
                   jax.ShapeDtypeStruct((n_col, T // LANES, HEAD_DIM, LANES), BF16),
                   jax.ShapeDtypeStruct((GATE_ROWS, T), F32)],
        compiler_params=_cparams(("parallel",)),
        name="proj",
    )(x2, nw.reshape(1, D_MODEL), w, gain, jnp.asarray(bd, BF16))


def _compress_kernel(r_ref, w1_ref, w2_ref, pos_ref, kn_ref, row_ref, col_ref):
    kind = pl.program_id(0) // NSA_KV_HEADS
    r = r_ref[0, 0]
    nc = r.shape[0]
    half = CMP_STRIDE * HEAD_DIM
    w1 = w1_ref[0]
    p_lo = _dot(r, w1[:half])
    p_hi = _dot(r, w1[half:])
    p_pos = _dot(jnp.broadcast_to(pos_ref[...], (8, CMP_LEN * HEAD_DIM)).astype(BF16), w1)[:1]
    hdn = p_lo + pltpu.roll(p_hi, nc - 1, 0) + p_pos
    hdn = hdn * jax.nn.sigmoid(hdn)
    out = _dot(hdn.astype(BF16), w2_ref[0])
    ms = jnp.sum(out * out, axis=-1, keepdims=True) * (1.0 / HEAD_DIM)
    normed = out * lax.rsqrt(ms + NORM_EPS) * kn_ref[...]
    out = jnp.where(kind == 0, normed, out)
    row_ref[0, 0] = out[:, :HEAD_DIM].astype(row_ref.dtype)
    col_ref[0, 0] = out.T[:HEAD_DIM].astype(col_ref.dtype)


def _compress(rows4, w1, w2, pos, k_norm):
    assert ROW_KC == 0 and ROW_VC == NSA_KV_HEADS
    _, B, NC, _ = rows4.shape
    kn = jnp.pad(k_norm.reshape(1, HEAD_DIM), ((0, 0), (0, LANES - HEAD_DIM)))
    w2p = jnp.pad(w2, ((0, 0), (0, 0), (0, LANES - HEAD_DIM))).astype(BF16)
    return pl.pallas_call(
        _compress_kernel,
        grid=(4, B),
        in_specs=[
            pl.BlockSpec((1, 1, NC, CMP_STRIDE * HEAD_DIM), lambda h, b: (h, b, 0, 0)),
            pl.BlockSpec((1, CMP_LEN * HEAD_DIM, CMP_HIDDEN), lambda h, b: (h // NSA_KV_HEADS, 0, 0)),
            pl.BlockSpec((1, CMP_HIDDEN, LANES), lambda h, b: (h // NSA_KV_HEADS, 0, 0)),
            pl.BlockSpec((1, CMP_LEN * HEAD_DIM), lambda h, b: (0, 0)),
            pl.BlockSpec((1, LANES), lambda h, b: (0, 0)),
        ],
        out_specs=[
            pl.BlockSpec((1, 1, NC, HEAD_DIM), lambda h, b: (h, b, 0, 0)),
            pl.BlockSpec((1, 1, HEAD_DIM, NC), lambda h, b: (h, b, 0, 0)),
        ],
        out_shape=[jax.ShapeDtypeStruct((4, B, NC, HEAD_DIM), BF16),
                   jax.ShapeDtypeStruct((4, B, HEAD_DIM, NC), BF16)],
        compiler_params=_cparams(("parallel", "parallel")),
        name="nsa_compress",
    )(rows4, w1.astype(BF16), w2p, pos.reshape(1, CMP_LEN * HEAD_DIM), kn)


def _softmax_tile_update(s, v_t, m, l, acc):
    m_new = jnp.maximum(m, jnp.max(s, axis=0, keepdims=True))
    alpha = jnp.exp2(m - m_new)
    p = jnp.exp2(s - m_new)
    l_new = alpha * l + jnp.sum(p, axis=0, keepdims=True)
    acc_new = alpha * acc + _dot(v_t, p.astype(BF16))
    return m_new, l_new, acc_new


def _pairs_to_rows(o_t):
    n = o_t.shape[1] // LANES
    outs = []
    for p in range(n // 2):
        a = o_t[:, (2 * p) * LANES:(2 * p + 1) * LANES]
        b = o_t[:, (2 * p + 1) * LANES:(2 * p + 2) * LANES]
        outs.append(jnp.concatenate([a, b], axis=0).T)
    return outs[0] if len(outs) == 1 else jnp.concatenate(outs, axis=1)


FAR_TILES = 4
NSA_SUB = 2
TAKEN = -2.0
N_FORCED = 3
LOG2E = math.log2(math.e)
NSA_TINY = 2.0 ** -100
BOUND_SLACK = 1.05


def _nsa_kernel(q_ref, kc_ref, vc_ref, ks_ref, vs_ref, kw_ref, vw_ref, gate_ref,
                cbias_ref, sbias_ref, wbias_ref, ov_ref, msel_ref, o_ref, *scratch, nb):
    neg_refs, stage_refs = scratch[:NSA_SUB], scratch[NSA_SUB:]
    refs = (q_ref, kc_ref, vc_ref, ks_ref, vs_ref, kw_ref, vw_ref, gate_ref, cbias_ref, sbias_ref, wbias_ref,
            ov_ref, msel_ref)
    blocks = [_nsa_block(u, refs, neg_refs[u], stage_refs, nb) for u in range(NSA_SUB)]
    for phase in range(3):
        for blk in blocks:
            blk[phase]()
    rows = [blk[3]() for blk in blocks]
    for u in range(NSA_SUB):
        o_ref[0, u * Q_BLOCK:(u + 1) * Q_BLOCK, :] = rows[u].astype(o_ref.dtype)


def _nsa_block(u, refs, neg_ref, stage_refs, nb):
    (q_ref, kc_ref, vc_ref, ks_ref, vs_ref, kw_ref, vw_ref, gate_ref, cbias_ref, sbias_ref, wbias_ref,
     ov_ref, msel_ref) = refs
    i = pl.program_id(2) * NSA_SUB + u
    R = NSA_REP
    N = R * Q_BLOCK
    nc = kc_ref.shape[2]
    q_t = jnp.concatenate([q_ref[r, u] for r in range(R)], axis=1)
    n_win = NSA_WINDOW // Q_BLOCK

    def k_rows(ref, J0, n_tiles):
        return ref[0, pl.ds(pl.multiple_of(J0 * Q_BLOCK, Q_BLOCK), n_tiles * Q_BLOCK), :]

    def v_cols(ref, J0, n_tiles):
        return jnp.concatenate([ref[0, J0 + u] for u in range(n_tiles)], axis=1)

    def sel_mask(J0, n_tiles):
        parts = [jnp.broadcast_to(neg_ref[pl.ds(2 * J0 + u, 1), :], (SLC_LEN, Q_BLOCK)) for u in range(2 * n_tiles)]
        return jnp.concatenate([jnp.concatenate(parts, axis=0)] * R, axis=1)

    def compressed(fast):
        cb = cbias_ref[0, pl.ds(pl.multiple_of(8 * (nb - 1) - 8 * i, 8), nc), :]
        s = _dot(kc_ref[0, 0], q_t) + cb
        if fast:
            e = jnp.exp2(s)
        else:
            e = jnp.where(s > 0.5 * NEG_INF, jnp.exp2(s - jnp.max(s, axis=0, keepdims=True)), 0.0)
        l = jnp.sum(e, axis=0, keepdims=True)
        p = e * jnp.where(l > 0.0, 1.0 / l, 0.0)
        return p, l, _dot(vc_ref[0, 0], p.astype(BF16))

    def select_blocks(p_c):
        p_sum = p_c[:, 0:Q_BLOCK]
        for r in range(1, R):
            p_sum = p_sum + p_c[:, r * Q_BLOCK:(r + 1) * Q_BLOCK]
        p_hi = p_sum.astype(BF16)
        p_lo = (p_sum - p_hi.astype(F32)).astype(BF16)
        imp = _dot(ov_ref[...], p_hi) + _dot(ov_ref[...], p_lo)
        n_slc = imp.shape[0]
        jj = lax.broadcasted_iota(jnp.int32, imp.shape, 0)
        qq = lax.broadcasted_iota(jnp.int32, imp.shape, 1)
        cur = 2 * i + jnp.where(qq >= SLC_LEN, 1, 0)
        forced = (jj == 0) | (jj == cur) | (jj == cur - 1)
        score = jnp.where(forced, TAKEN, jnp.where(jj <= cur, imp, -1.0))
        grp = 8
        j_rows = [jj[a:a + grp] for a in range(0, n_slc, grp)]
        for _ in range(min(SLC_TOPN, n_slc) - N_FORCED):
            cand = [(score[a:a + grp], j_rows[a // grp]) for a in range(0, n_slc, grp)]
            while len(cand) > 1:
                nxt = []
                for a in range(0, len(cand) - 1, 2):
                    (s0, j0), (s1, j1) = cand[a], cand[a + 1]
                    right = s1 > s0
                    nxt.append((jnp.maximum(s0, s1), jnp.where(right, j1, j0)))
                cand = nxt + cand[len(cand) - len(cand) % 2:]
            s8, j8 = cand[0]
            best = jnp.max(s8, axis=0, keepdims=True)
            first = jnp.min(jnp.where(s8 == best, j8, n_slc), axis=0, keepdims=True)
            score = jnp.where(jj == first, TAKEN, score)
        neg_ref[...] = jnp.where(score == TAKEN, msel_ref[0], NEG_INF)

    def add_chunk(st, s, v_t):
        e = jnp.exp2(s)
        return st[0] + jnp.sum(e, axis=0, keepdims=True), st[1] + _dot(v_t, e.astype(BF16))

    zero = (jnp.zeros((1, N), F32), jnp.zeros((HEAD_DIM, N), F32))

    def selected_near():
        J0 = jnp.maximum(i - 1, 0)
        bias = sbias_ref[0, pl.ds(pl.multiple_of((2 - (i - J0)) * Q_BLOCK, Q_BLOCK), 2 * Q_BLOCK), :]
        return add_chunk(zero, _dot(k_rows(ks_ref, J0, 2), q_t) + bias + sel_mask(J0, 2), v_cols(vs_ref, J0, 2))

    def selected_far(st):
        n_far = jnp.maximum(i - 1, 0)

        def far_chunk(J0, st, n_valid):
            mask = sel_mask(J0, FAR_TILES)
            if n_valid is not None:
                row = lax.broadcasted_iota(jnp.int32, mask.shape, 0)
                mask = jnp.where(row < n_valid * Q_BLOCK, mask, NEG_INF)
            return add_chunk(st, _dot(k_rows(ks_ref, J0, FAR_TILES), q_t) + mask, v_cols(vs_ref, J0, FAR_TILES))

        n_chunks = n_far // FAR_TILES
        rem = n_far - FAR_TILES * n_chunks

        def tile_of(c):
            return jnp.clip(n_far - FAR_TILES * (c + 1), 0, nb - FAR_TILES)

        def scores(c):
            return _dot(k_rows(ks_ref, tile_of(c), FAR_TILES), q_t)

        def values(c, e_ref, acc):
            return acc + _dot(v_cols(vs_ref, tile_of(c), FAR_TILES), e_ref[...])

        def stage(c, st, s_in, s_out, e_in, e_out):
            acc = values(c - 1, e_in, st[1])
            mask = sel_mask(tile_of(c), FAR_TILES)
            k_next = k_rows(ks_ref, tile_of(c + 1), FAR_TILES)
            s_out[...] = _dot(k_next, q_t)
            e = jnp.exp2(s_in[...] + mask)
            e_out[...] = e.astype(BF16)
            return st[0] + jnp.sum(e, axis=0, keepdims=True), acc

        s_a, s_b, e_a, e_b = stage_refs
        e_b[...] = jnp.zeros_like(e_b)
        s_a[...] = scores(0)

        def pair(c, st):
            st = stage(c, st, s_a, s_b, e_b, e_a)
            return stage(c + 1, st, s_b, s_a, e_a, e_b)

        n_quads = n_chunks // 4
        st = lax.fori_loop(0, n_quads, lambda t, st: pair(4 * t + 2, pair(4 * t, st)), st)
        st = lax.cond(n_chunks % 4 >= 2, lambda st: pair(4 * n_quads, st), lambda st: st, st)

        def odd_tail(st):
            st = stage(n_chunks - 1, st, s_a, s_b, e_b, e_a)
            return st[0], values(n_chunks - 1, e_a, st[1])

        st = lax.cond(n_chunks % 2 == 1, odd_tail, lambda st: (st[0], values(n_chunks - 1, e_b, st[1])), st)
        return lax.cond(rem > 0, lambda st: far_chunk(0, st, rem), lambda st: st, st)

    def window_fast():
        J0 = jnp.maximum(i - n_win, 0)
        bias = wbias_ref[0, pl.ds(pl.multiple_of((n_win - (i - J0)) * Q_BLOCK, Q_BLOCK), (n_win + 1) * Q_BLOCK), :]
        return add_chunk(zero, _dot(k_rows(kw_ref, J0, n_win + 1), q_t) + bias, v_cols(vw_ref, J0, n_win + 1))

    def sweep_exact(k_ref, v_ref, bias_ref, far_dist, n_back, with_sel):
        def body(t, st):
            J = i - t
            off = pl.multiple_of((far_dist - jnp.minimum(t, far_dist)) * Q_BLOCK, Q_BLOCK)
            s = _dot(k_rows(k_ref, J, 1), q_t) + bias_ref[0, pl.ds(off, Q_BLOCK), :]
            if with_sel:
                s = s + sel_mask(J, 1)
            return _softmax_tile_update(s, v_ref[0, J], *st)

        init = (jnp.full((1, N), NEG_INF, F32),) + zero
        _, l, acc = lax.fori_loop(0, jnp.minimum(i, n_back) + 1, body, init)
        return l, acc

    vals = {}

    def load_phase():
        vals["cmp"] = compressed(True)
        vals["win"] = window_fast()

    def select_phase():
        select_blocks(vals["cmp"][0])

    def near_phase():
        vals["near"] = selected_near()

    def exact_path():
        p_x, _, o_x = compressed(False)
        select_blocks(p_x)
        l_sx, a_sx = sweep_exact(ks_ref, vs_ref, sbias_ref, 2, nb, True)
        l_wx, a_wx = sweep_exact(kw_ref, vw_ref, wbias_ref, n_win, n_win, False)
        return o_x, a_sx * (1.0 / l_sx), a_wx * (1.0 / l_wx)

    def gate_row(br):
        return jnp.concatenate([gate_ref[br, 0, r:r + 1, u * Q_BLOCK:(u + 1) * Q_BLOCK] for r in range(R)], axis=1)

    def finish():
        _, l_c, o_c = vals["cmp"]
        l_w, a_w = vals["win"]
        l_s, a_s = selected_far(vals["near"])
        lane_q = lax.broadcasted_iota(jnp.int32, (1, N), 1) % Q_BLOCK
        has_cmp = (i > 0) | (lane_q >= CMP_LEN - 1)
        l_min = jnp.minimum(jnp.minimum(l_s, l_w), jnp.where(has_cmp, l_c, 1.0))
        o_cx, o_s, o_w = lax.cond(jnp.min(l_min) > NSA_TINY,
                                  lambda: (o_c, a_s * (1.0 / l_s), a_w * (1.0 / l_w)), exact_path)
        o_t = gate_row(0) * o_cx + gate_row(1) * o_s + gate_row(2) * o_w
        return _pairs_to_rows(o_t)

    return load_phase, select_phase, near_phase, finish


def _nsa_attention(colmat, rowmat, cmp_rows, cmp_cols, gates, cbias, sbias, wbias, ov_t, msel, B, S):
    nb = S // Q_BLOCK
    nc = S // CMP_STRIDE
    n_slc = S // SLC_LEN
    G, R = NSA_KV_HEADS, NSA_REP
    kern = functools.partial(_nsa_kernel, nb=nb)
    ns = nb // NSA_SUB
    return pl.pallas_call(
        kern,
        grid=(B, G, ns),
        in_specs=[
            pl.BlockSpec((R, NSA_SUB, HEAD_DIM, LANES), lambda b, g, i: (g, b * ns + i, 0, 0)),
            pl.BlockSpec((1, 1, nc, HEAD_DIM), lambda b, g, i: (g, b, 0, 0)),
            pl.BlockSpec((1, 1, HEAD_DIM, nc), lambda b, g, i: (NSA_KV_HEADS + g, b, 0, 0)),
            pl.BlockSpec((1, S, HEAD_DIM), lambda b, g, i: (ROW_KS + g, b, 0)),
            pl.BlockSpec((1, nb, HEAD_DIM, LANES), lambda b, g, i: (COL_VS + g, b, 0, 0)),
            pl.BlockSpec((1, S, HEAD_DIM), lambda b, g, i: (ROW_KW + g, b, 0)),
            pl.BlockSpec((1, nb, HEAD_DIM, LANES), lambda b, g, i: (COL_VW + g, b, 0, 0)),
            pl.BlockSpec((N_BRANCHES, 1, R, NSA_SUB * LANES), lambda b, g, i: (0, g, 0, b * ns + i)),
            pl.BlockSpec((1, cbias.shape[1], R * Q_BLOCK), lambda b, g, i: (g, 0, 0)),
            pl.BlockSpec((1, sbias.shape[1], R * Q_BLOCK), lambda b, g, i: (g, 0, 0)),
            pl.BlockSpec((1, wbias.shape[1], R * Q_BLOCK), lambda b, g, i: (g, 0, 0)),
            pl.BlockSpec((n_slc, nc), lambda b, g, i: (0, 0)),
            pl.BlockSpec((1, 1, Q_BLOCK), lambda b, g, i: (g, 0, 0)),
        ],
        out_specs=pl.BlockSpec((1, NSA_SUB * Q_BLOCK, R * HEAD_DIM), lambda b, g, i: (b, i, g)),
        out_shape=jax.ShapeDtypeStruct((B, S, NSA_HEADS * HEAD_DIM), BF16),
        scratch_shapes=[pltpu.VMEM((n_slc, Q_BLOCK), F32)] * NSA_SUB
        + [pltpu.VMEM((FAR_TILES * Q_BLOCK, R * Q_BLOCK), F32)] * 2
        + [pltpu.VMEM((FAR_TILES * Q_BLOCK, R * Q_BLOCK), BF16)] * 2,
        compiler_params=_cparams(("parallel", "parallel", "arbitrary")),
        name="nsa_attention",
    )(colmat, cmp_rows, cmp_cols, rowmat, colmat, rowmat, colmat, gates, cbias, sbias, wbias, ov_t, msel)


SB_QB = 512
SB_KT = 256
SB_ZERO_LOG = -160.0


def _sb_kernel(q_ref, k_ref, v_ref, tri_ref, o_ref):
    i = pl.program_id(1)
    H = SB_HEADS
    tri = tri_ref[...]
    assert SB_QB == 2 * SB_KT
    below_diag = (lax.broadcasted_iota(jnp.int32, (SB_KT, SB_KT), 1)
                  < lax.broadcasted_iota(jnp.int32, (SB_KT, SB_KT), 0))

    def step(J, rows, st, masked):
        out = []
        for h in range(H):
            carry, acc = st[h]
            k_t = jnp.concatenate([k_ref[h, (SB_KT // LANES) * J + c] for c in range(SB_KT // LANES)], axis=1)
            z = _dot(q_ref[h, rows, :], k_t)
            sp = jnp.maximum(z, 0.0) + jnp.log2(1.0 + jnp.exp2(-jnp.abs(z)))
            lf = jnp.where(below_diag, -sp, 0.0) if masked else -sp
            lf_hi = lf.astype(BF16)
            lf_lo = (lf - lf_hi.astype(F32)).astype(BF16)
            after = _dot(lf_hi, tri) + _dot(lf_lo, tri) + carry
            w = jnp.exp2(z - sp + after)
            if masked:
                w = jnp.where(below_diag, w, 0.0)
            v = v_ref[h, pl.ds(pl.multiple_of(J * SB_KT, SB_KT), SB_KT), :]
            acc = acc + _dot(w.astype(BF16), v)
            carry = carry + jnp.sum(lf, axis=1, keepdims=True)
            out.append((carry, acc))
        return tuple(out)

    lo, hi = slice(0, SB_KT), slice(SB_KT, SB_QB)
    zero = tuple((jnp.zeros((SB_KT, 1), F32), jnp.zeros((SB_KT, HEAD_DIM), F32)) for _ in range(H))
    st_hi = step(2 * i, hi, step(2 * i + 1, hi, zero, True), False)
    st_lo = step(2 * i, lo, zero, True)
    st = tuple(tuple(jnp.concatenate([a, b], axis=0) for a, b in zip(st_lo[h], st_hi[h])) for h in range(H))
    per_q = SB_QB // SB_KT
    all_rows = slice(0, SB_QB)

    def alive(st):
        worst = st[0][0]
        for h in range(1, H):
            worst = jnp.maximum(worst, st[h][0])
        return jnp.max(worst) > SB_ZERO_LOG

    def body(c):
        J, _, st = c
        st = step(J, all_rows, st, False)
        return J - 1, alive(st), st

    _, _, st = lax.while_loop(lambda c: (c[0] >= 0) & c[1], body, (per_q * i - 1, alive(st), st))
    o_ref[0] = jnp.concatenate([st[h][1] for h in range(H)], axis=1).astype(o_ref.dtype)


def _sb_attention(colmat, rowmat, tri, B, S):
    nq = S // SB_QB
    nb = S // LANES
    H = SB_HEADS
    return pl.pallas_call(
        _sb_kernel,
        grid=(B, nq),
        in_specs=[
            pl.BlockSpec((H, SB_QB, HEAD_DIM), lambda b, i: (ROW_SQ // H, b * nq + i, 0)),
            pl.BlockSpec((H, nb, HEAD_DIM, LANES), lambda b, i: (COL_SK // H, b, 0, 0)),
            pl.BlockSpec((H, S, HEAD_DIM), lambda b, i: (ROW_SV // H, b, 0)),
            pl.BlockSpec((SB_KT, SB_KT), lambda b, i: (0, 0)),
        ],
        out_specs=pl.BlockSpec((1, SB_QB, H * HEAD_DIM), lambda b, i: (b, i, 0)),
        out_shape=jax.ShapeDtypeStruct((B, S, H * HEAD_DIM), BF16),
        compiler_params=_cparams(("parallel", "arbitrary")),
        name="sb_attention",
    )(rowmat, colmat, rowmat, tri)


SWA_SUB = 4


def _swa_kernel(q_ref, k_ref, v_ref, bias_ref, sink_ref, o_ref):
    R = SWA_REP
    sink = sink_ref[0]
    for u in range(SWA_SUB):
        i = pl.program_id(2) * SWA_SUB + u
        q_t = jnp.concatenate([q_ref[r, u] for r in range(R)], axis=1)
        J0 = jnp.maximum(i - 1, 0)
        k = k_ref[0, pl.ds(pl.multiple_of(J0 * Q_BLOCK, Q_BLOCK), 2 * Q_BLOCK), :]
        bias = bias_ref[0, pl.ds(pl.multiple_of((1 - (i - J0)) * Q_BLOCK, Q_BLOCK), 2 * Q_BLOCK), :]
        s = _dot(k, q_t) + bias
        m = jnp.maximum(jnp.max(s, axis=0, keepdims=True), sink)
        e = jnp.exp(s - m)
        den = jnp.sum(e, axis=0, keepdims=True) + jnp.exp(sink - m)
        v_t = jnp.concatenate([v_ref[0, J0], v_ref[0, J0 + 1]], axis=1)
        o_t = _dot(v_t, e.astype(BF16)) * (1.0 / den)
        o_ref[0, u * Q_BLOCK:(u + 1) * Q_BLOCK, :] = _pairs_to_rows(o_t).astype(o_ref.dtype)


def _swa_attention(colmat, rowmat, bias, sinks, B, S):
    nb = S // Q_BLOCK
    ns = nb // SWA_SUB
    G, R = SWA_KV_HEADS, SWA_REP
    return pl.pallas_call(
        _swa_kernel,
        grid=(B, G, ns),
        in_specs=[
            pl.BlockSpec((R, SWA_SUB, HEAD_DIM, LANES), lambda b, g, i: (COL_WQ // R + g, b * ns + i, 0, 0)),
            pl.BlockSpec((1, S, HEAD_DIM), lambda b, g, i: (ROW_WK + g, b, 0)),
            pl.BlockSpec((1, nb, HEAD_DIM, LANES), lambda b, g, i: (COL_WV + g, b, 0, 0)),
            pl.BlockSpec((1, 3 * Q_BLOCK, R * Q_BLOCK), lambda b, g, i: (g, 0, 0)),
            pl.BlockSpec((1, 1, R * Q_BLOCK), lambda b, g, i: (g, 0, 0)),
        ],
        out_specs=pl.BlockSpec((1, SWA_SUB * Q_BLOCK, R * HEAD_DIM), lambda b, g, i: (b, i, g)),
        out_shape=jax.ShapeDtypeStruct((B, S, SWA_HEADS * HEAD_DIM), BF16),
        compiler_params=_cparams(("parallel", "parallel", "arbitrary")),
        name="swa_attention",
    )(colmat, rowmat, colmat, bias, sinks)


MERGE_TM = 512


def _merge_kernel(x_ref, nw_ref, wg_ref, ya_ref, yb_ref, yc_ref, ua_ref, ub_ref, uc_ref, wo_ref, o_ref):
    x = x_ref[...]
    h = _rms_rows(x, nw_ref[...]).astype(BF16)
    merged = None
    for br, (y_ref, u_ref) in enumerate(((ya_ref, ua_ref), (yb_ref, ub_ref), (yc_ref, uc_ref))):
        g = jax.nn.sigmoid(_dot(h, wg_ref[:, br * D_MODEL:(br + 1) * D_MODEL]))
        t = g * _dot(y_ref[...], u_ref[...])
        merged = t if merged is None else merged + t
    o_ref[...] = x + _dot(merged.astype(BF16), wo_ref[...])


def _merge(x2, nw, w_bgate, y_nsa, y_sb, y_swa, u_nsa, u_sb, u_swa, w_out):
    T = x2.shape[0]
    full = lambda shape: pl.BlockSpec(shape, lambda i: (0, 0))
    rows = lambda n: pl.BlockSpec((MERGE_TM, n), lambda i: (i, 0))
    return pl.pallas_call(
        _merge_kernel,
        grid=(T // MERGE_TM,),
        in_specs=[rows(D_MODEL), full((1, D_MODEL)), full((D_MODEL, N_BRANCHES * D_MODEL)),
                  rows(y_nsa.shape[1]), rows(y_sb.shape[1]), rows(y_swa.shape[1]),
                  full(u_nsa.shape), full(u_sb.shape), full(u_swa.shape), full((D_MODEL, D_MODEL))],
        out_specs=rows(D_MODEL),
        out_shape=jax.ShapeDtypeStruct((T, D_MODEL), F32),
        compiler_params=_cparams(("parallel",)),
        name="merge",
    )(x2, nw.reshape(1, D_MODEL), w_bgate.astype(BF16), y_nsa, y_sb, y_swa,
      u_nsa.astype(BF16), u_sb.astype(BF16), u_swa.astype(BF16), w_out.astype(BF16))


def _t5_bucket(dist):
    max_exact = NUM_BUCKETS // 2
    d = jnp.maximum(dist, 0)
    df = jnp.maximum(d, 1).astype(F32)
    large = max_exact + (jnp.log(df / max_exact) / math.log(MAX_DISTANCE / max_exact)
                         * (NUM_BUCKETS - max_exact)).astype(jnp.int32)
    large = jnp.minimum(large, NUM_BUCKETS - 1)
    return jnp.where(d < max_exact, d, large)


def _toeplitz(ext, delta):
    H = ext.shape[0]
    n = Q_BLOCK
    c = delta + MAX_DISTANCE
    w = ext[:, c - (n - 1):c + n]
    flat = jnp.pad(jnp.broadcast_to(w[:, None, :], (H, n, 2 * n - 1)), ((0, 0), (0, 0), (0, 1))).reshape(H, 2 * n * n)
    return flat[:, n - 1:n - 1 + n * (2 * n - 1)].reshape(H, n, 2 * n - 1)[:, :, :n]


def _lanes(t, G, R):
    K, Q = t.shape[1:]
    return t.reshape(G, R, K, Q).transpose(0, 2, 1, 3).reshape(G, K, R * Q)


def _bias_tables(rel_bias, nb, qk_bound):
    n = Q_BLOCK
    G, R = NSA_KV_HEADS, NSA_REP
    by_dist = rel_bias[_t5_bucket(jnp.arange(MAX_DISTANCE)), :].T.astype(F32)
    far = rel_bias[NUM_BUCKETS - 1, :].astype(F32)
    neg = jnp.full((1, n), NEG_INF, F32)
    k = jnp.arange(n)[:, None]
    q = jnp.arange(n)[None, :]
    above = (k > q)[None]

    rel = (by_dist[:NSA_HEADS] - far[:NSA_HEADS, None]) * LOG2E
    m_g = qk_bound + jnp.maximum(jnp.max(rel.reshape(G, -1), axis=1), 0.0)
    m_h = jnp.repeat(m_g, R)[:, None, None]
    ext = jnp.concatenate([jnp.broadcast_to(neg, (NSA_HEADS, n)), rel, jnp.zeros((NSA_HEADS, n), F32)], axis=1)
    t0, t1 = _toeplitz(ext, 0), _toeplitz(ext, n)
    zeros = jnp.zeros_like(t0)
    masked = jnp.full_like(t0, NEG_INF)
    sbias = _lanes(jnp.concatenate([zeros, t1, t0, masked], axis=1), G, R)
    n_win = NSA_WINDOW // n
    edge = jnp.where(above, zeros, NEG_INF)
    tiles = [edge] + [zeros] * (n_win - 2) + [t1, t0]
    wbias = _lanes(jnp.concatenate([t - m_h for t in tiles] + [masked] * n_win, axis=1), G, R)
    off = 8 * (nb - 1)
    near = jnp.stack([ext[:, 97 - CMP_STRIDE * m:97 - CMP_STRIDE * m + n] for m in range(-9, 7)], axis=1)
    n_rows = 16 * nb - 8
    cb = jnp.concatenate([jnp.zeros((NSA_HEADS, off - 9, n), F32), near,
                          jnp.full((NSA_HEADS, n_rows - off - 7, n), NEG_INF, F32)], axis=1) - m_h
    cbias = _lanes(cb, G, R)
    msel = jnp.broadcast_to(-m_g[:, None, None], (G, 1, n))

    hs = slice(NSA_HEADS, NSA_HEADS + SWA_HEADS)
    ext_w = jnp.concatenate([jnp.broadcast_to(neg, (SWA_HEADS, n)), by_dist[hs],
                             jnp.broadcast_to(far[hs, None], (SWA_HEADS, n))], axis=1)
    s0 = _toeplitz(ext_w, 0)
    s1 = jnp.where(above, _toeplitz(ext_w, n), NEG_INF)
    swa_bias = _lanes(jnp.concatenate([s1, s0, jnp.full_like(s0, NEG_INF)], axis=1), SWA_KV_HEADS, SWA_REP)
    return cbias, sbias, wbias, msel, swa_bias


def _overlap_t(S):
    nc, n_slc = S // CMP_STRIDE, S // SLC_LEN
    n = np.arange(nc)[None, :]
    j = np.arange(n_slc)[:, None]
    ov = (n * CMP_STRIDE < (j + 1) * SLC_LEN) & (n * CMP_STRIDE + CMP_LEN - 1 >= j * SLC_LEN) & (n < nc - 1)
    return jnp.asarray(ov, BF16)


def _head_gains(heads, norm_of, scale_of):
    ones = jnp.ones((HEAD_DIM,), F32)
    gains = [norm_of[name].astype(F32) * scale_of.get(name, 1.0) if name in norm_of else ones for name, _ in heads]
    return jnp.concatenate(gains).reshape(1, -1)


def kernel(x, rel_bias, ffn1_norm, ffn1_w_gate, ffn1_w_up, ffn1_w_down, mix_norm, w_in, nsa_q_norm, nsa_k_norm, nsa_cmp_pos, nsa_cmp_k_w1, nsa_cmp_k_w2, nsa_cmp_v_w1, nsa_cmp_v_w2, swa_q_norm, swa_k_norm, swa_sinks, w_up_nsa, w_up_sb, w_up_swa, w_out, ffn2_norm, ffn2_w_gate, ffn2_w_up, ffn2_w_down):
    B, S, D = x.shape
    T = B * S
    nb = S // Q_BLOCK
    nc = S // CMP_STRIDE
    depth = w_in.shape[0]
    scale = HEAD_DIM ** -0.5
    scale_of = {"nq": scale * LOG2E, "sq": scale * LOG2E, "wq": scale}
    qk_bound = (BOUND_SLACK * LOG2E * HEAD_DIM ** 0.5
                * jnp.max(jnp.abs(nsa_q_norm.astype(F32))) * jnp.max(jnp.abs(nsa_k_norm.astype(F32))))
    cbias, sbias, wbias, msel, swa_bias = _bias_tables(rel_bias, nb, qk_bound)
    plan = _proj_plan(scale_of)
    ov_t = _overlap_t(S)
    tri = jnp.asarray(np.tril(np.ones((SB_KT, SB_KT), np.float32), -1), BF16)

    (ffn1_w_gate, ffn1_w_up, ffn1_w_down, ffn2_w_gate, ffn2_w_up, ffn2_w_down, w_up_nsa, w_up_sb, w_up_swa,
     w_out) = (w.astype(BF16) for w in (ffn1_w_gate, ffn1_w_up, ffn1_w_down, ffn2_w_gate, ffn2_w_up, ffn2_w_down,
                                        w_up_nsa, w_up_sb, w_up_swa, w_out))
    w_bgate = w_in[:, :, OFF_BGATE:].astype(BF16)
    w_gate = jnp.pad(w_in[:, :, OFF_NGATE:OFF_NGATE + N_GATE], ((0, 0), (0, 0), (0, LANES - N_GATE)))
    w_proj = jnp.concatenate([w_in[:, :, off:off + HEAD_DIM] for _, off in ROW_HEADS + COL_HEADS] + [w_gate],
                             axis=2).astype(BF16)

    x2 = x.reshape(T, D)
    for l in range(depth):
        x2 = _ffn(x2, ffn1_norm[l], ffn1_w_gate[l], ffn1_w_up[l], ffn1_w_down[l])

        norm_of = {"nq": nsa_q_norm[l], "nks": nsa_k_norm[l], "nkw": nsa_k_norm[l],
                   "wq": swa_q_norm[l], "wk": swa_k_norm[l]}
        gain = jnp.concatenate([_head_gains(ROW_HEADS, norm_of, scale_of), _head_gains(COL_HEADS, norm_of, scale_of),
                                jnp.ones((1, LANES), F32)], axis=1)
        rowmat, colmat, gates = _proj(x2, mix_norm[l], w_proj[l], gain, plan)
        gates = gates[:N_GATE].reshape(N_BRANCHES, NSA_KV_HEADS, NSA_REP, T)

        rows4 = rowmat[ROW_KC:ROW_KC + 2 * NSA_KV_HEADS].reshape(2 * NSA_KV_HEADS, B, nc, CMP_STRIDE * HEAD_DIM)
        cmp_rows, cmp_cols = _compress(rows4, jnp.stack([nsa_cmp_k_w1[l], nsa_cmp_v_w1[l]]),
                                       jnp.stack([nsa_cmp_k_w2[l], nsa_cmp_v_w2[l]]),
                                       nsa_cmp_pos[l], nsa_k_norm[l])

        y_nsa = _nsa_attention(colmat, rowmat, cmp_rows, cmp_cols, gates, cbias, sbias, wbias, ov_t, msel, B, S)
        y_sb = _sb_attention(colmat, rowmat, tri, B, S)
        sinks = jnp.repeat(swa_sinks[l].astype(F32).reshape(SWA_KV_HEADS, 1, SWA_REP), Q_BLOCK, axis=2)
        y_swa = _swa_attention(colmat, rowmat, swa_bias, sinks, B, S)

        x2 = _merge(x2, mix_norm[l], w_bgate[l], y_nsa.reshape(T, -1), y_sb.reshape(T, -1),
                    y_swa.reshape(T, -1), w_up_nsa[l], w_up_sb[l], w_up_swa[l], w_out[l])
        x2 = _ffn(x2, ffn2_norm[l], ffn2_w_gate[l], ffn2_w_up[l], ffn2_w_down[l])
    return x2.reshape(B, S, D)
```

```python
import functools
import math

import numpy as np
import jax
import jax.numpy as jnp
from jax import lax
from jax.experimental import pallas as pl
from jax.experimental.pallas import tpu as pltpu

D_MODEL = 1024
HEAD_DIM = 64
Q_BLOCK = 128
NSA_HEADS = 8
NSA_KV_HEADS = 2
NSA_REP = NSA_HEADS // NSA_KV_HEADS
CMP_LEN = 32
CMP_STRIDE = 16
CMP_HIDDEN = 256
SLC_LEN = 64
SLC_TOPN = 16
NSA_WINDOW = 512
SB_HEADS = 4
SWA_HEADS = 4
SWA_KV_HEADS = 2
SWA_REP = SWA_HEADS // SWA_KV_HEADS
SWA_WINDOW = 128
NUM_BUCKETS = 32
MAX_DISTANCE = 128
D_FF = 2816
NORM_EPS = 1e-6
NEG_INF = -1e30
N_BRANCHES = 3
N_GATE = 3 * NSA_HEADS

LANES = 128
VMEM_LIMIT = 56 * 1024 * 1024
BF16 = jnp.bfloat16
F32 = jnp.float32

_sizes = (NSA_HEADS * HEAD_DIM,) + (NSA_KV_HEADS * HEAD_DIM,) * 6 + (N_GATE,) \
    + (SB_HEADS * HEAD_DIM,) * 3 + (SWA_HEADS * HEAD_DIM, SWA_KV_HEADS * HEAD_DIM,
                                    SWA_KV_HEADS * HEAD_DIM, N_BRANCHES * D_MODEL)
_offs = np.concatenate([[0], np.cumsum(_sizes)])
(OFF_NQ, OFF_NKC, OFF_NVC, OFF_NKS, OFF_NVS, OFF_NKW, OFF_NVW, OFF_NGATE,
 OFF_SQ, OFF_SK, OFF_SV, OFF_WQ, OFF_WK, OFF_WV, OFF_BGATE) = (int(o) for o in _offs[:-1])

ROW_HEADS = ([("nkc", OFF_NKC + 64 * g) for g in range(2)] + [("nvc", OFF_NVC + 64 * g) for g in range(2)]
             + [("nks", OFF_NKS + 64 * g) for g in range(2)] + [("nkw", OFF_NKW + 64 * g) for g in range(2)]
             + [("sq", OFF_SQ + 64 * h) for h in range(4)] + [("sv", OFF_SV + 64 * h) for h in range(4)]
             + [("wk", OFF_WK + 64 * g) for g in range(2)])
ROW_KC, ROW_VC, ROW_KS, ROW_KW, ROW_SQ, ROW_SV, ROW_WK = 0, 2, 4, 6, 8, 12, 16
COL_HEADS = ([("nq", OFF_NQ + 64 * h) for h in range(8)] + [("nvs", OFF_NVS + 64 * g) for g in range(2)]
             + [("nvw", OFF_NVW + 64 * g) for g in range(2)] + [("sk", OFF_SK + 64 * h) for h in range(4)]
             + [("wq", OFF_WQ + 64 * h) for h in range(4)] + [("wv", OFF_WV + 64 * g) for g in range(2)])
COL_NQ, COL_VS, COL_VW, COL_SK, COL_WQ, COL_WV = 0, 8, 10, 12, 16, 20


def _dot(a, b):
    return jnp.dot(a, b, preferred_element_type=F32)


def _rms_rows(x, w):
    ms = jnp.mean(x * x, axis=-1, keepdims=True)
    return x * lax.rsqrt(ms + NORM_EPS) * w


def _cparams(sem, flags=None):
    return pltpu.CompilerParams(dimension_semantics=sem, vmem_limit_bytes=VMEM_LIMIT, flags=flags)


FFN_TM = 512
MXU_TILE = 256
FFN_FC = MXU_TILE


def _ffn_kernel(x_ref, nw_ref, wg_ref, wu_ref, wd_ref, o_ref):
    x = x_ref[...]
    h = _rms_rows(x, nw_ref[...]).astype(BF16)
    acc = jnp.zeros(x.shape, F32)
    for c in range(D_FF // FFN_FC):
        cols = slice(c * FFN_FC, (c + 1) * FFN_FC)
        g = _dot(h, wg_ref[:, cols])
        u = _dot(h, wu_ref[:, cols])
        a = (g * jax.nn.sigmoid(g)) * u
        acc = acc + _dot(a.astype(BF16), wd_ref[cols, :])
    o_ref[...] = x + 0.5 * acc


def _ffn(x2, nw, wg, wu, wd):
    T = x2.shape[0]
    resident = lambda shape: pl.BlockSpec(shape, lambda i: (0, 0), pipeline_mode=pl.Buffered(1))
    return pl.pallas_call(
        _ffn_kernel,
        grid=(T // FFN_TM,),
        in_specs=[
            pl.BlockSpec((FFN_TM, D_MODEL), lambda i: (i, 0)),
            pl.BlockSpec((1, D_MODEL), lambda i: (0, 0)),
            resident((D_MODEL, D_FF)), resident((D_MODEL, D_FF)), resident((D_FF, D_MODEL)),
        ],
        out_specs=pl.BlockSpec((FFN_TM, D_MODEL), lambda i: (i, 0)),
        out_shape=jax.ShapeDtypeStruct((T, D_MODEL), F32),
        compiler_params=_cparams(("parallel",)),
        name="ffn",
    )(x2, nw.reshape(1, D_MODEL), wg.astype(BF16), wu.astype(BF16), wd.astype(BF16))


PROJ_TM = 512


GATE_ROWS = 32

_NORMED = ("nq", "nks", "nkw", "wq", "wk")


def _proj_plan(scale_of):
    plan = []
    for kind, heads in (("row", ROW_HEADS), ("col", COL_HEADS)):
        for p in range(len(heads) // 2):
            name = heads[2 * p][0]
            assert heads[2 * p + 1][0] == name
            plan.append((kind, p, name in _NORMED, 1.0 if name in _NORMED else scale_of.get(name, 1.0)))
    plan.append(("gate", 0, False, 1.0))
    return tuple(plan)


def _proj_kernel(x_ref, nw_ref, w_ref, gain_ref, bd_ref, row_ref, col_ref, gate_ref, *, plan):
    h = _rms_rows(x_ref[...], nw_ref[...]).astype(BF16)
    pair = 2 * MXU_TILE
    for c0 in range(0, len(plan) * LANES, pair):
        width = min(pair, len(plan) * LANES - c0)
        zg = _dot(h, w_ref[:, c0:c0 + width])
        for u in range(width // LANES):
            t = c0 // LANES + u
            kind, p, normed, scale = plan[t]
            z = zg[:, u * LANES:(u + 1) * LANES]
            if normed:
                ms = _dot((z * z).astype(BF16), bd_ref[...])
                z = z * lax.rsqrt(ms + NORM_EPS) * gain_ref[:, t * LANES:(t + 1) * LANES]
            elif scale != 1.0:
                z = z * scale
            if kind == "row":
                row_ref[2 * p] = z[:, :HEAD_DIM].astype(row_ref.dtype)
                row_ref[2 * p + 1] = z[:, HEAD_DIM:].astype(row_ref.dtype)
            elif kind == "col":
                z_t = z.T
                for c in range(PROJ_TM // LANES):
                    col_ref[2 * p, c] = z_t[:HEAD_DIM, c * LANES:(c + 1) * LANES].astype(col_ref.dtype)
                    col_ref[2 * p + 1, c] = z_t[HEAD_DIM:, c * LANES:(c + 1) * LANES].astype(col_ref.dtype)
            else:
                gate_ref[...] = jax.nn.sigmoid(z).T[:GATE_ROWS]


def _proj(x2, nw, w, gain, plan):
    T = x2.shape[0]
    n_row, n_col = len(ROW_HEADS), len(COL_HEADS)
    bd = np.kron(np.eye(2, dtype=np.float32), np.full((HEAD_DIM, HEAD_DIM), 1.0 / HEAD_DIM, np.float32))
    return pl.pallas_call(
        functools.partial(_proj_kernel, plan=plan),
        grid=(T // PROJ_TM,),
        in_specs=[
            pl.BlockSpec((PROJ_TM, D_MODEL), lambda i: (i, 0)),
            pl.BlockSpec((1, D_MODEL), lambda i: (0, 0)),
            pl.BlockSpec(w.shape, lambda i: (0, 0)),
            pl.BlockSpec(gain.shape, lambda i: (0, 0)),
            pl.BlockSpec((LANES, LANES), lambda i: (0, 0)),
        ],
        out_specs=[
            pl.BlockSpec((n_row, PROJ_TM, HEAD_DIM), lambda i: (0, i, 0)),
            pl.BlockSpec((n_col, PROJ_TM // LANES, HEAD_DIM, LANES), lambda i: (0, i, 0, 0)),
            pl.BlockSpec((GATE_ROWS, PROJ_TM), lambda i: (0, i)),
        ],
        out_shape=[jax.ShapeDtypeStruct((n_row, T, HEAD_DIM), BF16),
                   jax.ShapeDtypeStruct((n_col, T // LANES, HEAD_DIM, LANES), BF16),
                   jax.ShapeDtypeStruct((GATE_ROWS, T), F32)],
        compiler_params=_cparams(("parallel",)),
        name="proj",
    )(x2, nw.reshape(1, D_MODEL), w, gain, jnp.asarray(bd, BF16))


def _compress_kernel(r_ref, w1_ref, w2_ref, pos_ref, kn_ref, row_ref, col_ref):
    kind = pl.program_id(0) // NSA_KV_HEADS
    r = r_ref[0, 0]
    nc = r.shape[0]
    half = CMP_STRIDE * HEAD_DIM
    w1 = w1_ref[0]
    p_lo = _dot(r, w1[:half])
    p_hi = _dot(r, w1[half:])
    p_pos = _dot(jnp.broadcast_to(pos_ref[...], (8, CMP_LEN * HEAD_DIM)).astype(BF16), w1)[:1]
    hdn = p_lo + pltpu.roll(p_hi, nc - 1, 0) + p_pos
    hdn = hdn * jax.nn.sigmoid(hdn)
    out = _dot(hdn.astype(BF16), w2_ref[0])
    ms = jnp.sum(out * out, axis=-1, keepdims=True) * (1.0 / HEAD_DIM)
    normed = out * lax.rsqrt(ms + NORM_EPS) * kn_ref[...]
    out = jnp.where(kind == 0, normed, out)
    row_ref[0, 0] = out[:, :HEAD_DIM].astype(row_ref.dtype)
    col_ref[0, 0] = out.T[:HEAD_DIM].astype(col_ref.dtype)


def _compress(rows4, w1, w2, pos, k_norm):
    assert ROW_KC == 0 and ROW_VC == NSA_KV_HEADS
    _, B, NC, _ = rows4.shape
    kn = jnp.pad(k_norm.reshape(1, HEAD_DIM), ((0, 0), (0, LANES - HEAD_DIM)))
    w2p = jnp.pad(w2, ((0, 0), (0, 0), (0, LANES - HEAD_DIM))).astype(BF16)
    return pl.pallas_call(
        _compress_kernel,
        grid=(4, B),
        in_specs=[
            pl.BlockSpec((1, 1, NC, CMP_STRIDE * HEAD_DIM), lambda h, b: (h, b, 0, 0)),
            pl.BlockSpec((1, CMP_LEN * HEAD_DIM, CMP_HIDDEN), lambda h, b: (h // NSA_KV_HEADS, 0, 0)),
            pl.BlockSpec((1, CMP_HIDDEN, LANES), lambda h, b: (h // NSA_KV_HEADS, 0, 0)),
            pl.BlockSpec((1, CMP_LEN * HEAD_DIM), lambda h, b: (0, 0)),
            pl.BlockSpec((1, LANES), lambda h, b: (0, 0)),
        ],
        out_specs=[
            pl.BlockSpec((1, 1, NC, HEAD_DIM), lambda h, b: (h, b, 0, 0)),
            pl.BlockSpec((1, 1, HEAD_DIM, NC), lambda h, b: (h, b, 0, 0)),
        ],
        out_shape=[jax.ShapeDtypeStruct((4, B, NC, HEAD_DIM), BF16),
                   jax.ShapeDtypeStruct((4, B, HEAD_DIM, NC), BF16)],
        compiler_params=_cparams(("parallel", "parallel")),
        name="nsa_compress",
    )(rows4, w1.astype(BF16), w2p, pos.reshape(1, CMP_LEN * HEAD_DIM), kn)


def _softmax_tile_update(s, v_t, m, l, acc):
    m_new = jnp.maximum(m, jnp.max(s, axis=0, keepdims=True))
    alpha = jnp.exp2(m - m_new)
    p = jnp.exp2(s - m_new)
    l_new = alpha * l + jnp.sum(p, axis=0, keepdims=True)
    acc_new = alpha * acc + _dot(v_t, p.astype(BF16))
    return m_new, l_new, acc_new


def _pairs_to_rows(o_t):
    n = o_t.shape[1] // LANES
    outs = []
    for p in range(n // 2):
        a = o_t[:, (2 * p) * LANES:(2 * p + 1) * LANES]
        b = o_t[:, (2 * p + 1) * LANES:(2 * p + 2) * LANES]
        outs.append(jnp.concatenate([a, b], axis=0).T)
    return outs[0] if len(outs) == 1 else jnp.concatenate(outs, axis=1)


FAR_TILES = 4
NSA_SUB = 2
TAKEN = -2.0
N_FORCED = 3
LOG2E = math.log2(math.e)
NSA_TINY = 2.0 ** -100
BOUND_SLACK = 1.05


def _nsa_kernel(q_ref, kc_ref, vc_ref, ks_ref, vs_ref, kw_ref, vw_ref, gate_ref,
                cbias_ref, sbias_ref, wbias_ref, ov_ref, msel_ref, o_ref, *scratch, nb):
    neg_refs, stage_refs = scratch[:NSA_SUB], scratch[NSA_SUB:]
    refs = (q_ref, kc_ref, vc_ref, ks_ref, vs_ref, kw_ref, vw_ref, gate_ref, cbias_ref, sbias_ref, wbias_ref,
            ov_ref, msel_ref)
    blocks = [_nsa_block(u, refs, neg_refs[u], stage_refs, nb) for u in range(NSA_SUB)]
    for phase in range(3):
        for blk in blocks:
            blk[phase]()
    rows = [blk[3]() for blk in blocks]
    for u in range(NSA_SUB):
        o_ref[0, u * Q_BLOCK:(u + 1) * Q_BLOCK, :] = rows[u].astype(o_ref.dtype)


def _nsa_block(u, refs, neg_ref, stage_refs, nb):
    (q_ref, kc_ref, vc_ref, ks_ref, vs_ref, kw_ref, vw_ref, gate_ref, cbias_ref, sbias_ref, wbias_ref,
     ov_ref, msel_ref) = refs
    i = pl.program_id(2) * NSA_SUB + u
    R = NSA_REP
    N = R * Q_BLOCK
    nc = kc_ref.shape[2]
    q_t = jnp.concatenate([q_ref[r, u] for r in range(R)], axis=1)
    n_win = NSA_WINDOW // Q_BLOCK

    def k_rows(ref, J0, n_tiles):
        return ref[0, pl.ds(pl.multiple_of(J0 * Q_BLOCK, Q_BLOCK), n_tiles * Q_BLOCK), :]

    def v_cols(ref, J0, n_tiles):
        return jnp.concatenate([ref[0, J0 + u] for u in range(n_tiles)], axis=1)

    def sel_mask(J0, n_tiles):
        parts = [jnp.broadcast_to(neg_ref[pl.ds(2 * J0 + u, 1), :], (SLC_LEN, Q_BLOCK)) for u in range(2 * n_tiles)]
        return jnp.concatenate([jnp.concatenate(parts, axis=0)] * R, axis=1)

    def compressed(fast):
        cb = cbias_ref[0, pl.ds(pl.multiple_of(8 * (nb - 1) - 8 * i, 8), nc), :]
        s = _dot(kc_ref[0, 0], q_t) + cb
        if fast:
            e = jnp.exp2(s)
        else:
            e = jnp.where(s > 0.5 * NEG_INF, jnp.exp2(s - jnp.max(s, axis=0, keepdims=True)), 0.0)
        l = jnp.sum(e, axis=0, keepdims=True)
        p = e * jnp.where(l > 0.0, 1.0 / l, 0.0)
        return p, l, _dot(vc_ref[0, 0], p.astype(BF16))

    def select_blocks(p_c):
        p_sum = p_c[:, 0:Q_BLOCK]
        for r in range(1, R):
            p_sum = p_sum + p_c[:, r * Q_BLOCK:(r + 1) * Q_BLOCK]
        p_hi = p_sum.astype(BF16)
        p_lo = (p_sum - p_hi.astype(F32)).astype(BF16)
        imp = _dot(ov_ref[...], p_hi) + _dot(ov_ref[...], p_lo)
        n_slc = imp.shape[0]
        jj = lax.broadcasted_iota(jnp.int32, imp.shape, 0)
        qq = lax.broadcasted_iota(jnp.int32, imp.shape, 1)
        cur = 2 * i + jnp.where(qq >= SLC_LEN, 1, 0)
        forced = (jj == 0) | (jj == cur) | (jj == cur - 1)
        score = jnp.where(forced, TAKEN, jnp.where(jj <= cur, imp, -1.0))
        grp = 8
        j_rows = [jj[a:a + grp] for a in range(0, n_slc, grp)]
        for _ in range(min(SLC_TOPN, n_slc) - N_FORCED):
            cand = [(score[a:a + grp], j_rows[a // grp]) for a in range(0, n_slc, grp)]
            while len(cand) > 1:
                nxt = []
                for a in range(0, len(cand) - 1, 2):
                    (s0, j0), (s1, j1) = cand[a], cand[a + 1]
                    right = s1 > s0
                    nxt.append((jnp.maximum(s0, s1), jnp.where(right, j1, j0)))
                cand = nxt + cand[len(cand) - len(cand) % 2:]
            s8, j8 = cand[0]
            best = jnp.max(s8, axis=0, keepdims=True)
            first = jnp.min(jnp.where(s8 == best, j8, n_slc), axis=0, keepdims=True)
            score = jnp.where(jj == first, TAKEN, score)
        neg_ref[...] = jnp.where(score == TAKEN, msel_ref[0], NEG_INF)

    def add_chunk(st, s, v_t):
        e = jnp.exp2(s)
        return st[0] + jnp.sum(e, axis=0, keepdims=True), st[1] + _dot(v_t, e.astype(BF16))

    zero = (jnp.zeros((1, N), F32), jnp.zeros((HEAD_DIM, N), F32))

    def selected_near():
        J0 = jnp.maximum(i - 1, 0)
        bias = sbias_ref[0, pl.ds(pl.multiple_of((2 - (i - J0)) * Q_BLOCK, Q_BLOCK), 2 * Q_BLOCK), :]
        return add_chunk(zero, _dot(k_rows(ks_ref, J0, 2), q_t) + bias + sel_mask(J0, 2), v_cols(vs_ref, J0, 2))

    def selected_far(st):
        n_far = jnp.maximum(i - 1, 0)

        def far_chunk(J0, st, n_valid):
            mask = sel_mask(J0, FAR_TILES)
            if n_valid is not None:
                row = lax.broadcasted_iota(jnp.int32, mask.shape, 0)
                mask = jnp.where(row < n_valid * Q_BLOCK, mask, NEG_INF)
            return add_chunk(st, _dot(k_rows(ks_ref, J0, FAR_TILES), q_t) + mask, v_cols(vs_ref, J0, FAR_TILES))

        n_chunks = n_far // FAR_TILES
        rem = n_far - FAR_TILES * n_chunks

        def tile_of(c):
            return jnp.clip(n_far - FAR_TILES * (c + 1), 0, nb - FAR_TILES)

        def scores(c):
            return _dot(k_rows(ks_ref, tile_of(c), FAR_TILES), q_t)

        def values(c, e_ref, acc):
            return acc + _dot(v_cols(vs_ref, tile_of(c), FAR_TILES), e_ref[...])

        def stage(c, st, s_in, s_out, e_in, e_out):
            acc = values(c - 1, e_in, st[1])
            mask = sel_mask(tile_of(c), FAR_TILES)
            k_next = k_rows(ks_ref, tile_of(c + 1), FAR_TILES)
            s_out[...] = _dot(k_next, q_t)
            e = jnp.exp2(s_in[...] + mask)
            e_out[...] = e.astype(BF16)
            return st[0] + jnp.sum(e, axis=0, keepdims=True), acc

        s_a, s_b, e_a, e_b = stage_refs
        e_b[...] = jnp.zeros_like(e_b)
        s_a[...] = scores(0)

        def pair(c, st):
            st = stage(c, st, s_a, s_b, e_b, e_a)
            return stage(c + 1, st, s_b, s_a, e_a, e_b)

        n_quads = n_chunks // 4
        st = lax.fori_loop(0, n_quads, lambda t, st: pair(4 * t + 2, pair(4 * t, st)), st)
        st = lax.cond(n_chunks % 4 >= 2, lambda st: pair(4 * n_quads, st), lambda st: st, st)

        def odd_tail(st):
            st = stage(n_chunks - 1, st, s_a, s_b, e_b, e_a)
            return st[0], values(n_chunks - 1, e_a, st[1])

        st = lax.cond(n_chunks % 2 == 1, odd_tail, lambda st: (st[0], values(n_chunks - 1, e_b, st[1])), st)
        return lax.cond(rem > 0, lambda st: far_chunk(0, st, rem), lambda st: st, st)

    def window_fast():
        J0 = jnp.maximum(i - n_win, 0)
        bias = wbias_ref[0, pl.ds(pl.multiple_of((n_win - (i - J0)) * Q_BLOCK, Q_BLOCK), (n_win + 1) * Q_BLOCK), :]
        return add_chunk(zero, _dot(k_rows(kw_ref, J0, n_win + 1), q_t) + bias, v_cols(vw_ref, J0, n_win + 1))

    def sweep_exact(k_ref, v_ref, bias_ref, far_dist, n_back, with_sel):
        def body(t, st):
            J = i - t
            off = pl.multiple_of((far_dist - jnp.minimum(t, far_dist)) * Q_BLOCK, Q_BLOCK)
            s = _dot(k_rows(k_ref, J, 1), q_t) + bias_ref[0, pl.ds(off, Q_BLOCK), :]
            if with_sel:
                s = s + sel_mask(J, 1)
            return _softmax_tile_update(s, v_ref[0, J], *st)

        init = (jnp.full((1, N), NEG_INF, F32),) + zero
        _, l, acc = lax.fori_loop(0, jnp.minimum(i, n_back) + 1, body, init)
        return l, acc

    vals = {}

    def load_phase():
        vals["cmp"] = compressed(True)
        vals["win"] = window_fast()

    def select_phase():
        select_blocks(vals["cmp"][0])

    def near_phase():
        vals["near"] = selected_near()

    def exact_path():
        p_x, _, o_x = compressed(False)
        select_blocks(p_x)
        l_sx, a_sx = sweep_exact(ks_ref, vs_ref, sbias_ref, 2, nb, True)
        l_wx, a_wx = sweep_exact(kw_ref, vw_ref, wbias_ref, n_win, n_win, False)
        return o_x, a_sx * (1.0 / l_sx), a_wx * (1.0 / l_wx)

    def gate_row(br):
        return jnp.concatenate([gate_ref[br, 0, r:r + 1, u * Q_BLOCK:(u + 1) * Q_BLOCK] for r in range(R)], axis=1)

    def finish():
        _, l_c, o_c = vals["cmp"]
        l_w, a_w = vals["win"]
        l_s, a_s = selected_far(vals["near"])
        lane_q = lax.broadcasted_iota(jnp.int32, (1, N), 1) % Q_BLOCK
        has_cmp = (i > 0) | (lane_q >= CMP_LEN - 1)
        l_min = jnp.minimum(jnp.minimum(l_s, l_w), jnp.where(has_cmp, l_c, 1.0))
        o_cx, o_s, o_w = lax.cond(jnp.min(l_min) > NSA_TINY,
                                  lambda: (o_c, a_s * (1.0 / l_s), a_w * (1.0 / l_w)), exact_path)
        o_t = gate_row(0) * o_cx + gate_row(1) * o_s + gate_row(2) * o_w
        return _pairs_to_rows(o_t)

    return load_phase, select_phase, near_phase, finish


def _nsa_attention(colmat, rowmat, cmp_rows, cmp_cols, gates, cbias, sbias, wbias, ov_t, msel, B, S):
    nb = S // Q_BLOCK
    nc = S // CMP_STRIDE
    n_slc = S // SLC_LEN
    G, R = NSA_KV_HEADS, NSA_REP
    kern = functools.partial(_nsa_kernel, nb=nb)
    ns = nb // NSA_SUB
    return pl.pallas_call(
        kern,
        grid=(B, G, ns),
        in_specs=[
            pl.BlockSpec((R, NSA_SUB, HEAD_DIM, LANES), lambda b, g, i: (g, b * ns + i, 0, 0)),
            pl.BlockSpec((1, 1, nc, HEAD_DIM), lambda b, g, i: (g, b, 0, 0)),
            pl.BlockSpec((1, 1, HEAD_DIM, nc), lambda b, g, i: (NSA_KV_HEADS + g, b, 0, 0)),
            pl.BlockSpec((1, S, HEAD_DIM), lambda b, g, i: (ROW_KS + g, b, 0)),
            pl.BlockSpec((1, nb, HEAD_DIM, LANES), lambda b, g, i: (COL_VS + g, b, 0, 0)),
            pl.BlockSpec((1, S, HEAD_DIM), lambda b, g, i: (ROW_KW + g, b, 0)),
            pl.BlockSpec((1, nb, HEAD_DIM, LANES), lambda b, g, i: (COL_VW + g, b, 0, 0)),
            pl.BlockSpec((N_BRANCHES, 1, R, NSA_SUB * LANES), lambda b, g, i: (0, g, 0, b * ns + i)),
            pl.BlockSpec((1, cbias.shape[1], R * Q_BLOCK), lambda b, g, i: (g, 0, 0)),
            pl.BlockSpec((1, sbias.shape[1], R * Q_BLOCK), lambda b, g, i: (g, 0, 0)),
            pl.BlockSpec((1, wbias.shape[1], R * Q_BLOCK), lambda b, g, i: (g, 0, 0)),
            pl.BlockSpec((n_slc, nc), lambda b, g, i: (0, 0)),
            pl.BlockSpec((1, 1, Q_BLOCK), lambda b, g, i: (g, 0, 0)),
        ],
        out_specs=pl.BlockSpec((1, NSA_SUB * Q_BLOCK, R * HEAD_DIM), lambda b, g, i: (b, i, g)),
        out_shape=jax.ShapeDtypeStruct((B, S, NSA_HEADS * HEAD_DIM), BF16),
        scratch_shapes=[pltpu.VMEM((n_slc, Q_BLOCK), F32)] * NSA_SUB
        + [pltpu.VMEM((FAR_TILES * Q_BLOCK, R * Q_BLOCK), F32)] * 2
        + [pltpu.VMEM((FAR_TILES * Q_BLOCK, R * Q_BLOCK), BF16)] * 2,
        compiler_params=_cparams(("parallel", "parallel", "arbitrary")),
        name="nsa_attention",
    )(colmat, cmp_rows, cmp_cols, rowmat, colmat, rowmat, colmat, gates, cbias, sbias, wbias, ov_t, msel)


SB_QB = 512
SB_KT = 256
SB_ZERO_LOG = -160.0


def _sb_kernel(q_ref, k_ref, v_ref, tri_ref, o_ref):
    i = pl.program_id(1)
    H = SB_HEADS
    tri = tri_ref[...]
    assert SB_QB == 2 * SB_KT
    below_diag = (lax.broadcasted_iota(jnp.int32, (SB_KT, SB_KT), 1)
                  < lax.broadcasted_iota(jnp.int32, (SB_KT, SB_KT), 0))

    def step(J, rows, st, masked):
        out = []
        for h in range(H):
            carry, acc = st[h]
            k_t = jnp.concatenate([k_ref[h, (SB_KT // LANES) * J + c] for c in range(SB_KT // LANES)], axis=1)
            z = _dot(q_ref[h, rows, :], k_t)
            sp = jnp.maximum(z, 0.0) + jnp.log2(1.0 + jnp.exp2(-jnp.abs(z)))
            lf = jnp.where(below_diag, -sp, 0.0) if masked else -sp
            lf_hi = lf.astype(BF16)
            lf_lo = (lf - lf_hi.astype(F32)).astype(BF16)
            after = _dot(lf_hi, tri) + _dot(lf_lo, tri) + carry
            w = jnp.exp2(z - sp + after)
            if masked:
                w = jnp.where(below_diag, w, 0.0)
            v = v_ref[h, pl.ds(pl.multiple_of(J * SB_KT, SB_KT), SB_KT), :]
            acc = acc + _dot(w.astype(BF16), v)
            carry = carry + jnp.sum(lf, axis=1, keepdims=True)
            out.append((carry, acc))
        return tuple(out)

    lo, hi = slice(0, SB_KT), slice(SB_KT, SB_QB)
    zero = tuple((jnp.zeros((SB_KT, 1), F32), jnp.zeros((SB_KT, HEAD_DIM), F32)) for _ in range(H))
    st_hi = step(2 * i, hi, step(2 * i + 1, hi, zero, True), False)
    st_lo = step(2 * i, lo, zero, True)
    st = tuple(tuple(jnp.concatenate([a, b], axis=0) for a, b in zip(st_lo[h], st_hi[h])) for h in range(H))
    per_q = SB_QB // SB_KT
    all_rows = slice(0, SB_QB)

    def alive(st):
        worst = st[0][0]
        for h in range(1, H):
            worst = jnp.maximum(worst, st[h][0])
        return jnp.max(worst) > SB_ZERO_LOG

    def body(c):
        J, _, st = c
        st = step(J, all_rows, st, False)
        return J - 1, alive(st), st

    _, _, st = lax.while_loop(lambda c: (c[0] >= 0) & c[1], body, (per_q * i - 1, alive(st), st))
    o_ref[0] = jnp.concatenate([st[h][1] for h in range(H)], axis=1).astype(o_ref.dtype)


def _sb_attention(colmat, rowmat, tri, B, S):
    nq = S // SB_QB
    nb = S // LANES
    H = SB_HEADS
    return pl.pallas_call(
        _sb_kernel,
        grid=(B, nq),
        in_specs=[
            pl.BlockSpec((H, SB_QB, HEAD_DIM), lambda b, i: (ROW_SQ // H, b * nq + i, 0)),
            pl.BlockSpec((H, nb, HEAD_DIM, LANES), lambda b, i: (COL_SK // H, b, 0, 0)),
            pl.BlockSpec((H, S, HEAD_DIM), lambda b, i: (ROW_SV // H, b, 0)),
            pl.BlockSpec((SB_KT, SB_KT), lambda b, i: (0, 0)),
        ],
        out_specs=pl.BlockSpec((1, SB_QB, H * HEAD_DIM), lambda b, i: (b, i, 0)),
        out_shape=jax.ShapeDtypeStruct((B, S, H * HEAD_DIM), BF16),
        compiler_params=_cparams(("parallel", "arbitrary")),
        name="sb_attention",
    )(rowmat, colmat, rowmat, tri)


SWA_SUB = 4


def _swa_kernel(q_ref, k_ref, v_ref, bias_ref, sink_ref, o_ref):
    R = SWA_REP
    sink = sink_ref[0]
    for u in range(SWA_SUB):
        i = pl.program_id(2) * SWA_SUB + u
        q_t = jnp.concatenate([q_ref[r, u] for r in range(R)], axis=1)
        J0 = jnp.maximum(i - 1, 0)
        k = k_ref[0, pl.ds(pl.multiple_of(J0 * Q_BLOCK, Q_BLOCK), 2 * Q_BLOCK), :]
        bias = bias_ref[0, pl.ds(pl.multiple_of((1 - (i - J0)) * Q_BLOCK, Q_BLOCK), 2 * Q_BLOCK), :]
        s = _dot(k, q_t) + bias
        m = jnp.maximum(jnp.max(s, axis=0, keepdims=True), sink)
        e = jnp.exp(s - m)
        den = jnp.sum(e, axis=0, keepdims=True) + jnp.exp(sink - m)
        v_t = jnp.concatenate([v_ref[0, J0], v_ref[0, J0 + 1]], axis=1)
        o_t = _dot(v_t, e.astype(BF16)) * (1.0 / den)
        o_ref[0, u * Q_BLOCK:(u + 1) * Q_BLOCK, :] = _pairs_to_rows(o_t).astype(o_ref.dtype)


def _swa_attention(colmat, rowmat, bias, sinks, B, S):
    nb = S // Q_BLOCK
    ns = nb // SWA_SUB
    G, R = SWA_KV_HEADS, SWA_REP
    return pl.pallas_call(
        _swa_kernel,
        grid=(B, G, ns),
        in_specs=[
            pl.BlockSpec((R, SWA_SUB, HEAD_DIM, LANES), lambda b, g, i: (COL_WQ // R + g, b * ns + i, 0, 0)),
            pl.BlockSpec((1, S, HEAD_DIM), lambda b, g, i: (ROW_WK + g, b, 0)),
            pl.BlockSpec((1, nb, HEAD_DIM, LANES), lambda b, g, i: (COL_WV + g, b, 0, 0)),
            pl.BlockSpec((1, 3 * Q_BLOCK, R * Q_BLOCK), lambda b, g, i: (g, 0, 0)),
            pl.BlockSpec((1, 1, R * Q_BLOCK), lambda b, g, i: (g, 0, 0)),
        ],
        out_specs=pl.BlockSpec((1, SWA_SUB * Q_BLOCK, R * HEAD_DIM), lambda b, g, i: (b, i, g)),
        out_shape=jax.ShapeDtypeStruct((B, S, SWA_HEADS * HEAD_DIM), BF16),
        compiler_params=_cparams(("parallel", "parallel", "arbitrary")),
        name="swa_attention",
    )(colmat, rowmat, colmat, bias, sinks)


MERGE_TM = 512


def _merge_kernel(x_ref, nw_ref, wg_ref, ya_ref, yb_ref, yc_ref, ua_ref, ub_ref, uc_ref, wo_ref, o_ref):
    x = x_ref[...]
    h = _rms_rows(x, nw_ref[...]).astype(BF16)
    merged = None
    for br, (y_ref, u_ref) in enumerate(((ya_ref, ua_ref), (yb_ref, ub_ref), (yc_ref, uc_ref))):
        g = jax.nn.sigmoid(_dot(h, wg_ref[:, br * D_MODEL:(br + 1) * D_MODEL]))
        t = g * _dot(y_ref[...], u_ref[...])
        merged = t if merged is None else merged + t
    o_ref[...] = x + _dot(merged.astype(BF16), wo_ref[...])


def _merge(x2, nw, w_bgate, y_nsa, y_sb, y_swa, u_nsa, u_sb, u_swa, w_out):
    T = x2.shape[0]
    full = lambda shape: pl.BlockSpec(shape, lambda i: (0, 0))
    rows = lambda n: pl.BlockSpec((MERGE_TM, n), lambda i: (i, 0))
    return pl.pallas_call(
        _merge_kernel,
        grid=(T // MERGE_TM,),
        in_specs=[rows(D_MODEL), full((1, D_MODEL)), full((D_MODEL, N_BRANCHES * D_MODEL)),
                  rows(y_nsa.shape[1]), rows(y_sb.shape[1]), rows(y_swa.shape[1]),
                  full(u_nsa.shape), full(u_sb.shape), full(u_swa.shape), full((D_MODEL, D_MODEL))],
        out_specs=rows(D_MODEL),
        out_shape=jax.ShapeDtypeStruct((T, D_MODEL), F32),
        compiler_params=_cparams(("parallel",)),
        name="merge",
    )(x2, nw.reshape(1, D_MODEL), w_bgate.astype(BF16), y_nsa, y_sb, y_swa,
      u_nsa.astype(BF16), u_sb.astype(BF16), u_swa.astype(BF16), w_out.astype(BF16))


def _t5_bucket(dist):
    max_exact = NUM_BUCKETS // 2
    d = jnp.maximum(dist, 0)
    df = jnp.maximum(d, 1).astype(F32)
    large = max_exact + (jnp.log(df / max_exact) / math.log(MAX_DISTANCE / max_exact)
                         * (NUM_BUCKETS - max_exact)).astype(jnp.int32)
    large = jnp.minimum(large, NUM_BUCKETS - 1)
    return jnp.where(d < max_exact, d, large)


def _toeplitz(ext, delta):
    H = ext.shape[0]
    n = Q_BLOCK
    c = delta + MAX_DISTANCE
    w = ext[:, c - (n - 1):c + n]
    flat = jnp.pad(jnp.broadcast_to(w[:, None, :], (H, n, 2 * n - 1)), ((0, 0), (0, 0), (0, 1))).reshape(H, 2 * n * n)
    return flat[:, n - 1:n - 1 + n * (2 * n - 1)].reshape(H, n, 2 * n - 1)[:, :, :n]


def _lanes(t, G, R):
    K, Q = t.shape[1:]
    return t.reshape(G, R, K, Q).transpose(0, 2, 1, 3).reshape(G, K, R * Q)


def _bias_tables(rel_bias, nb, qk_bound):
    n = Q_BLOCK
    G, R = NSA_KV_HEADS, NSA_REP
    by_dist = rel_bias[_t5_bucket(jnp.arange(MAX_DISTANCE)), :].T.astype(F32)
    far = rel_bias[NUM_BUCKETS - 1, :].astype(F32)
    neg = jnp.full((1, n), NEG_INF, F32)
    k = jnp.arange(n)[:, None]
    q = jnp.arange(n)[None, :]
    above = (k > q)[None]

    rel = (by_dist[:NSA_HEADS] - far[:NSA_HEADS, None]) * LOG2E
    m_g = qk_bound + jnp.maximum(jnp.max(rel.reshape(G, -1), axis=1), 0.0)
    m_h = jnp.repeat(m_g, R)[:, None, None]
    ext = jnp.concatenate([jnp.broadcast_to(neg, (NSA_HEADS, n)), rel, jnp.zeros((NSA_HEADS, n), F32)], axis=1)
    t0, t1 = _toeplitz(ext, 0), _toeplitz(ext, n)
    zeros = jnp.zeros_like(t0)
    masked = jnp.full_like(t0, NEG_INF)
    sbias = _lanes(jnp.concatenate([zeros, t1, t0, masked], axis=1), G, R)
    n_win = NSA_WINDOW // n
    edge = jnp.where(above, zeros, NEG_INF)
    tiles = [edge] + [zeros] * (n_win - 2) + [t1, t0]
    wbias = _lanes(jnp.concatenate([t - m_h for t in tiles] + [masked] * n_win, axis=1), G, R)
    off = 8 * (nb - 1)
    near = jnp.stack([ext[:, 97 - CMP_STRIDE * m:97 - CMP_STRIDE * m + n] for m in range(-9, 7)], axis=1)
    n_rows = 16 * nb - 8
    cb = jnp.concatenate([jnp.zeros((NSA_HEADS, off - 9, n), F32), near,
                          jnp.full((NSA_HEADS, n_rows - off - 7, n), NEG_INF, F32)], axis=1) - m_h
    cbias = _lanes(cb, G, R)
    msel = jnp.broadcast_to(-m_g[:, None, None], (G, 1, n))

    hs = slice(NSA_HEADS, NSA_HEADS + SWA_HEADS)
    ext_w = jnp.concatenate([jnp.broadcast_to(neg, (SWA_HEADS, n)), by_dist[hs],
                             jnp.broadcast_to(far[hs, None], (SWA_HEADS, n))], axis=1)
    s0 = _toeplitz(ext_w, 0)
    s1 = jnp.where(above, _toeplitz(ext_w, n), NEG_INF)
    swa_bias = _lanes(jnp.concatenate([s1, s0, jnp.full_like(s0, NEG_INF)], axis=1), SWA_KV_HEADS, SWA_REP)
    return cbias, sbias, wbias, msel, swa_bias


def _overlap_t(S):
    nc, n_slc = S // CMP_STRIDE, S // SLC_LEN
    n = np.arange(nc)[None, :]
    j = np.arange(n_slc)[:, None]
    ov = (n * CMP_STRIDE < (j + 1) * SLC_LEN) & (n * CMP_STRIDE + CMP_LEN - 1 >= j * SLC_LEN) & (n < nc - 1)
    return jnp.asarray(ov, BF16)


def _head_gains(heads, norm_of, scale_of):
    ones = jnp.ones((HEAD_DIM,), F32)
    gains = [norm_of[name].astype(F32) * scale_of.get(name, 1.0) if name in norm_of else ones for name, _ in heads]
    return jnp.concatenate(gains).reshape(1, -1)


def kernel(x, rel_bias, ffn1_norm, ffn1_w_gate, ffn1_w_up, ffn1_w_down, mix_norm, w_in, nsa_q_norm, nsa_k_norm, nsa_cmp_pos, nsa_cmp_k_w1, nsa_cmp_k_w2, nsa_cmp_v_w1, nsa_cmp_v_w2, swa_q_norm, swa_k_norm, swa_sinks, w_up_nsa, w_up_sb, w_up_swa, w_out, ffn2_norm, ffn2_w_gate, ffn2_w_up, ffn2_w_down):
    B, S, D = x.shape
    T = B * S
    nb = S // Q_BLOCK
    nc = S // CMP_STRIDE
    depth = w_in.shape[0]
    scale = HEAD_DIM ** -0.5
    scale_of = {"nq": scale * LOG2E, "sq": scale * LOG2E, "wq": scale}
    qk_bound = (BOUND_SLACK * LOG2E * HEAD_DIM ** 0.5
                * jnp.max(jnp.abs(nsa_q_norm.astype(F32))) * jnp.max(jnp.abs(nsa_k_norm.astype(F32))))
    cbias, sbias, wbias, msel, swa_bias = _bias_tables(rel_bias, nb, qk_bound)
    plan = _proj_plan(scale_of)
    ov_t = _overlap_t(S)
    tri = jnp.asarray(np.tril(np.ones((SB_KT, SB_KT), np.float32), -1), BF16)

    (ffn1_w_gate, ffn1_w_up, ffn1_w_down, ffn2_w_gate, ffn2_w_up, ffn2_w_down, w_up_nsa, w_up_sb, w_up_swa,
     w_out) = (w.astype(BF16) for w in (ffn1_w_gate, ffn1_w_up, ffn1_w_down, ffn2_w_gate, ffn2_w_up, ffn2_w_down,
                                        w_up_nsa, w_up_sb, w_up_swa, w_out))
    w_bgate = w_in[:, :, OFF_BGATE:].astype(BF16)
    w_gate = jnp.pad(w_in[:, :, OFF_NGATE:OFF_NGATE + N_GATE], ((0, 0), (0, 0), (0, LANES - N_GATE)))
    w_proj = jnp.concatenate([w_in[:, :, off:off + HEAD_DIM] for _, off in ROW_HEADS + COL_HEADS] + [w_gate],
                             axis=2).astype(BF16)

    x2 = x.reshape(T, D)
    for l in range(depth):
        x2 = _ffn(x2, ffn1_norm[l], ffn1_w_gate[l], ffn1_w_up[l], ffn1_w_down[l])

        norm_of = {"nq": nsa_q_norm[l], "nks": nsa_k_norm[l], "nkw": nsa_k_norm[l],
                   "wq": swa_q_norm[l], "wk": swa_k_norm[l]}
        gain = jnp.concatenate([_head_gains(ROW_HEADS, norm_of, scale_of), _head_gains(COL_HEADS, norm_of, scale_of),
                                jnp.ones((1, LANES), F32)], axis=1)
        rowmat, colmat, gates = _proj(x2, mix_norm[l], w_proj[l], gain, plan)
        gates = gates[:N_GATE].reshape(N_BRANCHES, NSA_KV_HEADS, NSA_REP, T)

        rows4 = rowmat[ROW_KC:ROW_KC + 2 * NSA_KV_HEADS].reshape(2 * NSA_KV_HEADS, B, nc, CMP_STRIDE * HEAD_DIM)
        cmp_rows, cmp_cols = _compress(rows4, jnp.stack([nsa_cmp_k_w1[l], nsa_cmp_v_w1[l]]),
                                       jnp.stack([nsa_cmp_k_w2[l], nsa_cmp_v_w2[l]]),
                                       nsa_cmp_pos[l], nsa_k_norm[l])

        y_nsa = _nsa_attention(colmat, rowmat, cmp_rows, cmp_cols, gates, cbias, sbias, wbias, ov_t, msel, B, S)
        y_sb = _sb_attention(colmat, rowmat, tri, B, S)
        sinks = jnp.repeat(swa_sinks[l].astype(F32).reshape(SWA_KV_HEADS, 1, SWA_REP), Q_BLOCK, axis=2)
        y_swa = _swa_attention(colmat, rowmat, swa_bias, sinks, B, S)

        x2 = _merge(x2, mix_norm[l], w_bgate[l], y_nsa.reshape(T, -1), y_sb.reshape(T, -1),
                    y_swa.reshape(T, -1), w_up_nsa[l], w_up_sb[l], w_up_swa[l], w_out[l])
        x2 = _ffn(x2, ffn2_norm[l], ffn2_w_gate[l], ffn2_w_up[l], ffn2_w_down[l])
    return x2.reshape(B, S, D)
```

```python
import functools
import math

import numpy as np
import jax
import jax.numpy as jnp
from jax import lax
from jax.experimental import pallas as pl
from jax.experimental.pallas import tpu as pltpu

D_MODEL = 1024
HEAD_DIM = 64
Q_BLOCK = 128
NSA_HEADS = 8
NSA_KV_HEADS = 2
NSA_REP = NSA_HEADS // NSA_KV_HEADS
CMP_LEN = 32
CMP_STRIDE = 16
CMP_HIDDEN = 256
SLC_LEN = 64
SLC_TOPN = 16
NSA_WINDOW = 512
SB_HEADS = 4
SWA_HEADS = 4
SWA_KV_HEADS = 2
SWA_REP = SWA_HEADS // SWA_KV_HEADS
SWA_WINDOW = 128
NUM_BUCKETS = 32
MAX_DISTANCE = 128
D_FF = 2816
NORM_EPS = 1e-6
NEG_INF = -1e30
N_BRANCHES = 3
N_GATE = 3 * NSA_HEADS

LANES = 128
VMEM_LIMIT = 56 * 1024 * 1024
BF16 = jnp.bfloat16
F32 = jnp.float32

_sizes = (NSA_HEADS * HEAD_DIM,) + (NSA_KV_HEADS * HEAD_DIM,) * 6 + (N_GATE,) \
    + (SB_HEADS * HEAD_DIM,) * 3 + (SWA_HEADS * HEAD_DIM, SWA_KV_HEADS * HEAD_DIM,
                                    SWA_KV_HEADS * HEAD_DIM, N_BRANCHES * D_MODEL)
_offs = np.concatenate([[0], np.cumsum(_sizes)])
(OFF_NQ, OFF_NKC, OFF_NVC, OFF_NKS, OFF_NVS, OFF_NKW, OFF_NVW, OFF_NGATE,
 OFF_SQ, OFF_SK, OFF_SV, OFF_WQ, OFF_WK, OFF_WV, OFF_BGATE) = (int(o) for o in _offs[:-1])

ROW_HEADS = ([("nkc", OFF_NKC + 64 * g) for g in range(2)] + [("nvc", OFF_NVC + 64 * g) for g in range(2)]
             + [("nks", OFF_NKS + 64 * g) for g in range(2)] + [("nkw", OFF_NKW + 64 * g) for g in range(2)]
             + [("sq", OFF_SQ + 64 * h) for h in range(4)] + [("sv", OFF_SV + 64 * h) for h in range(4)]
             + [("wk", OFF_WK + 64 * g) for g in range(2)])
ROW_KC, ROW_VC, ROW_KS, ROW_KW, ROW_SQ, ROW_SV, ROW_WK = 0, 2, 4, 6, 8, 12, 16
COL_HEADS = ([("nq", OFF_NQ + 64 * h) for h in range(8)] + [("nvs", OFF_NVS + 64 * g) for g in range(2)]
             + [("nvw", OFF_NVW + 64 * g) for g in range(2)] + [("sk", OFF_SK + 64 * h) for h in range(4)]
             + [("wq", OFF_WQ + 64 * h) for h in range(4)] + [("wv", OFF_WV + 64 * g) for g in range(2)])
COL_NQ, COL_VS, COL_VW, COL_SK, COL_WQ, COL_WV = 0, 8, 10, 12, 16, 20


def _dot(a, b):
    return jnp.dot(a, b, preferred_element_type=F32)


def _rms_rows(x, w):
    ms = jnp.mean(x * x, axis=-1, keepdims=True)
    return x * lax.rsqrt(ms + NORM_EPS) * w


def _cparams(sem, flags=None):
    return pltpu.CompilerParams(dimension_semantics=sem, vmem_limit_bytes=VMEM_LIMIT, flags=flags)


FFN_TM = 512
MXU_TILE = 256
FFN_FC = MXU_TILE


def _ffn_kernel(x_ref, nw_ref, wg_ref, wu_ref, wd_ref, o_ref):
    x = x_ref[...]
    h = _rms_rows(x, nw_ref[...]).astype(BF16)
    acc = jnp.zeros(x.shape, F32)
    for c in range(D_FF // FFN_FC):
        cols = slice(c * FFN_FC, (c + 1) * FFN_FC)
        g = _dot(h, wg_ref[:, cols])
        u = _dot(h, wu_ref[:, cols])
        a = (g * jax.nn.sigmoid(g)) * u
        acc = acc + _dot(a.astype(BF16), wd_ref[cols, :])
    o_ref[...] = x + 0.5 * acc


def _ffn(x2, nw, wg, wu, wd):
    T = x2.shape[0]
    resident = lambda shape: pl.BlockSpec(shape, lambda i: (0, 0), pipeline_mode=pl.Buffered(1))
    return pl.pallas_call(
        _ffn_kernel,
        grid=(T // FFN_TM,),
        in_specs=[
            pl.BlockSpec((FFN_TM, D_MODEL), lambda i: (i, 0)),
            pl.BlockSpec((1, D_MODEL), lambda i: (0, 0)),
            resident((D_MODEL, D_FF)), resident((D_MODEL, D_FF)), resident((D_FF, D_MODEL)),
        ],
        out_specs=pl.BlockSpec((FFN_TM, D_MODEL), lambda i: (i, 0)),
        out_shape=jax.ShapeDtypeStruct((T, D_MODEL), F32),
        compiler_params=_cparams(("parallel",)),
        name="ffn",
    )(x2, nw.reshape(1, D_MODEL), wg.astype(BF16), wu.astype(BF16), wd.astype(BF16))


PROJ_TM = 512


GATE_ROWS = 32

_NORMED = ("nq", "nks", "nkw", "wq", "wk")


def _proj_plan(scale_of):
    plan = []
    for kind, heads in (("row", ROW_HEADS), ("col", COL_HEADS)):
        for p in range(len(heads) // 2):
            name = heads[2 * p][0]
            assert heads[2 * p + 1][0] == name
            plan.append((kind, p, name in _NORMED, 1.0 if name in _NORMED else scale_of.get(name, 1.0)))
    plan.append(("gate", 0, False, 1.0))
    return tuple(plan)


def _proj_kernel(x_ref, nw_ref, w_ref, gain_ref, bd_ref, row_ref, col_ref, gate_ref, *, plan):
    h = _rms_rows(x_ref[...], nw_ref[...]).astype(BF16)
    pair = 2 * MXU_TILE
    for c0 in range(0, len(plan) * LANES, pair):
        width = min(pair, len(plan) * LANES - c0)
        zg = _dot(h, w_ref[:, c0:c0 + width])
        for u in range(width // LANES):
            t = c0 // LANES + u
            kind, p, normed, scale = plan[t]
            z = zg[:, u * LANES:(u + 1) * LANES]
            if normed:
                ms = _dot((z * z).astype(BF16), bd_ref[...])
                z = z * lax.rsqrt(ms + NORM_EPS) * gain_ref[:, t * LANES:(t + 1) * LANES]
            elif scale != 1.0:
                z = z * scale
            if kind == "row":
                row_ref[2 * p] = z[:, :HEAD_DIM].astype(row_ref.dtype)
                row_ref[2 * p + 1] = z[:, HEAD_DIM:].astype(row_ref.dtype)
            elif kind == "col":
                z_t = z.T
                for c in range(PROJ_TM // LANES):
                    col_ref[2 * p, c] = z_t[:HEAD_DIM, c * LANES:(c + 1) * LANES].astype(col_ref.dtype)
                    col_ref[2 * p + 1, c] = z_t[HEAD_DIM:, c * LANES:(c + 1) * LANES].astype(col_ref.dtype)
            else:
                gate_ref[...] = jax.nn.sigmoid(z).T[:GATE_ROWS]


def _proj(x2, nw, w, gain, plan):
    T = x2.shape[0]
    n_row, n_col = len(ROW_HEADS), len(COL_HEADS)
    bd = np.kron(np.eye(2, dtype=np.float32), np.full((HEAD_DIM, HEAD_DIM), 1.0 / HEAD_DIM, np.float32))
    return pl.pallas_call(
        functools.partial(_proj_kernel, plan=plan),
        grid=(T // PROJ_TM,),
        in_specs=[
            pl.BlockSpec((PROJ_TM, D_MODEL), lambda i: (i, 0)),
            pl.BlockSpec((1, D_MODEL), lambda i: (0, 0)),
            pl.BlockSpec(w.shape, lambda i: (0, 0)),
            pl.BlockSpec(gain.shape, lambda i: (0, 0)),
            pl.BlockSpec((LANES, LANES), lambda i: (0, 0)),
        ],
        out_specs=[
            pl.BlockSpec((n_row, PROJ_TM, HEAD_DIM), lambda i: (0, i, 0)),
            pl.BlockSpec((n_col, PROJ_TM // LANES, HEAD_DIM, LANES), lambda i: (0, i, 0, 0)),
            pl.BlockSpec((GATE_ROWS, PROJ_TM), lambda i: (0, i)),
        ],
        out_shape=[jax.ShapeDtypeStruct((n_row, T, HEAD_DIM), BF16),
                   jax.ShapeDtypeStruct((n_col, T // LANES, HEAD_DIM, LANES), BF16),
                   jax.ShapeDtypeStruct((GATE_ROWS, T), F32)],
        compiler_params=_cparams(("parallel",)),
        name="proj",
    )(x2, nw.reshape(1, D_MODEL), w, gain, jnp.asarray(bd, BF16))


def _compress_kernel(r_ref, w1_ref, w2_ref, pos_ref, kn_ref, row_ref, col_ref):
    kind = pl.program_id(0) // NSA_KV_HEADS
    r = r_ref[0, 0]
    nc = r.shape[0]
    half = CMP_STRIDE * HEAD_DIM
    w1 = w1_ref[0]
    p_lo = _dot(r, w1[:half])
    p_hi = _dot(r, w1[half:])
    p_pos = _dot(jnp.broadcast_to(pos_ref[...], (8, CMP_LEN * HEAD_DIM)).astype(BF16), w1)[:1]
    hdn = p_lo + pltpu.roll(p_hi, nc - 1, 0) + p_pos
    hdn = hdn * jax.nn.sigmoid(hdn)
    out = _dot(hdn.astype(BF16), w2_ref[0])
    ms = jnp.sum(out * out, axis=-1, keepdims=True) * (1.0 / HEAD_DIM)
    normed = out * lax.rsqrt(ms + NORM_EPS) * kn_ref[...]
    out = jnp.where(kind == 0, normed, out)
    row_ref[0, 0] = out[:, :HEAD_DIM].astype(row_ref.dtype)
    col_ref[0, 0] = out.T[:HEAD_DIM].astype(col_ref.dtype)


def _compress(rows4, w1, w2, pos, k_norm):
    assert ROW_KC == 0 and ROW_VC == NSA_KV_HEADS
    _, B, NC, _ = rows4.shape
    kn = jnp.pad(k_norm.reshape(1, HEAD_DIM), ((0, 0), (0, LANES - HEAD_DIM)))
    w2p = jnp.pad(w2, ((0, 0), (0, 0), (0, LANES - HEAD_DIM))).astype(BF16)
    return pl.pallas_call(
        _compress_kernel,
        grid=(4, B),
        in_specs=[
            pl.BlockSpec((1, 1, NC, CMP_STRIDE * HEAD_DIM), lambda h, b: (h, b, 0, 0)),
            pl.BlockSpec((1, CMP_LEN * HEAD_DIM, CMP_HIDDEN), lambda h, b: (h // NSA_KV_HEADS, 0, 0)),
            pl.BlockSpec((1, CMP_HIDDEN, LANES), lambda h, b: (h // NSA_KV_HEADS, 0, 0)),
            pl.BlockSpec((1, CMP_LEN * HEAD_DIM), lambda h, b: (0, 0)),
            pl.BlockSpec((1, LANES), lambda h, b: (0, 0)),
        ],
        out_specs=[
            pl.BlockSpec((1, 1, NC, HEAD_DIM), lambda h, b: (h, b, 0, 0)),
            pl.BlockSpec((1, 1, HEAD_DIM, NC), lambda h, b: (h, b, 0, 0)),
        ],
        out_shape=[jax.ShapeDtypeStruct((4, B, NC, HEAD_DIM), BF16),
                   jax.ShapeDtypeStruct((4, B, HEAD_DIM, NC), BF16)],
        compiler_params=_cparams(("parallel", "parallel")),
        name="nsa_compress",
    )(rows4, w1.astype(BF16), w2p, pos.reshape(1, CMP_LEN * HEAD_DIM), kn)


def _softmax_tile_update(s, v_t, m, l, acc):
    m_new = jnp.maximum(m, jnp.max(s, axis=0, keepdims=True))
    alpha = jnp.exp2(m - m_new)
    p = jnp.exp2(s - m_new)
    l_new = alpha * l + jnp.sum(p, axis=0, keepdims=True)
    acc_new = alpha * acc + _dot(v_t, p.astype(BF16))
    return m_new, l_new, acc_new


def _pairs_to_rows(o_t):
    n = o_t.shape[1] // LANES
    outs = []
    for p in range(n // 2):
        a = o_t[:, (2 * p) * LANES:(2 * p + 1) * LANES]
        b = o_t[:, (2 * p + 1) * LANES:(2 * p + 2) * LANES]
        outs.append(jnp.concatenate([a, b], axis=0).T)
    return outs[0] if len(outs) == 1 else jnp.concatenate(outs, axis=1)


FAR_TILES = 4
NSA_SUB = 2
TAKEN = -2.0
NEG_ROWS = 8
N_FORCED = 3
LOG2E = math.log2(math.e)
NSA_TINY = 2.0 ** -100
BOUND_SLACK = 1.05


def _nsa_kernel(q_ref, kc_ref, vc_ref, ks_ref, vs_ref, kw_ref, vw_ref, gate_ref,
                cbias_ref, sbias_ref, wbias_ref, ov_ref, msel_ref, o_ref, *scratch, nb):
    neg_refs, stage_refs = scratch[:NSA_SUB], scratch[NSA_SUB:]
    refs = (q_ref, kc_ref, vc_ref, ks_ref, vs_ref, kw_ref, vw_ref, gate_ref, cbias_ref, sbias_ref, wbias_ref,
            ov_ref, msel_ref)
    blocks = [_nsa_block(u, refs, neg_refs[u], stage_refs, nb) for u in range(NSA_SUB)]
    for phase in range(3):
        for blk in blocks:
            blk[phase]()
    rows = [blk[3]() for blk in blocks]
    for u in range(NSA_SUB):
        o_ref[0, u * Q_BLOCK:(u + 1) * Q_BLOCK, :] = rows[u].astype(o_ref.dtype)


def _nsa_block(u, refs, neg_ref, stage_refs, nb):
    (q_ref, kc_ref, vc_ref, ks_ref, vs_ref, kw_ref, vw_ref, gate_ref, cbias_ref, sbias_ref, wbias_ref,
     ov_ref, msel_ref) = refs
    i = pl.program_id(2) * NSA_SUB + u
    R = NSA_REP
    N = R * Q_BLOCK
    nc = kc_ref.shape[2]
    q_t = jnp.concatenate([q_ref[r, u] for r in range(R)], axis=1)
    n_win = NSA_WINDOW // Q_BLOCK

    def k_rows(ref, J0, n_tiles):
        return ref[0, pl.ds(pl.multiple_of(J0 * Q_BLOCK, Q_BLOCK), n_tiles * Q_BLOCK), :]

    def v_cols(ref, J0, n_tiles):
        return jnp.concatenate([ref[0, J0 + u] for u in range(n_tiles)], axis=1)

    def sel_mask(J0, n_tiles, n_valid=None):
        n_slc = neg_ref.shape[0] - NEG_ROWS
        parts = []
        for u in range(2 * n_tiles):
            row = 2 * J0 + u
            if n_valid is not None:
                row = jnp.where(u // 2 < n_valid, row, n_slc)
            parts.append(jnp.broadcast_to(neg_ref[pl.ds(row, 1), :], (SLC_LEN, Q_BLOCK)))
        return jnp.concatenate([jnp.concatenate(parts, axis=0)] * R, axis=1)

    def compressed(fast):
        cb = cbias_ref[0, pl.ds(pl.multiple_of(8 * (nb - 1) - 8 * i, 8), nc), :]
        s = _dot(kc_ref[0, 0], q_t) + cb
        if fast:
            e = jnp.exp2(s)
        else:
            e = jnp.where(s > 0.5 * NEG_INF, jnp.exp2(s - jnp.max(s, axis=0, keepdims=True)), 0.0)
        l = jnp.sum(e, axis=0, keepdims=True)
        p = e * jnp.where(l > 0.0, 1.0 / l, 0.0)
        return p, l, _dot(vc_ref[0, 0], p.astype(BF16))

    def select_blocks(p_c):
        p_sum = p_c[:, 0:Q_BLOCK]
        for r in range(1, R):
            p_sum = p_sum + p_c[:, r * Q_BLOCK:(r + 1) * Q_BLOCK]
        p_hi = p_sum.astype(BF16)
        p_lo = (p_sum - p_hi.astype(F32)).astype(BF16)
        imp = _dot(ov_ref[...], p_hi) + _dot(ov_ref[...], p_lo)
        n_slc = imp.shape[0]
        jj = lax.broadcasted_iota(jnp.int32, imp.shape, 0)
        qq = lax.broadcasted_iota(jnp.int32, imp.shape, 1)
        cur = 2 * i + jnp.where(qq >= SLC_LEN, 1, 0)
        forced = (jj == 0) | (jj == cur) | (jj == cur - 1)
        score = jnp.where(forced, TAKEN, jnp.where(jj <= cur, imp, -1.0))
        grp = 8
        j_rows = [jj[a:a + grp] for a in range(0, n_slc, grp)]
        for _ in range(min(SLC_TOPN, n_slc) - N_FORCED):
            cand = [(score[a:a + grp], j_rows[a // grp]) for a in range(0, n_slc, grp)]
            while len(cand) > 1:
                nxt = []
                for a in range(0, len(cand) - 1, 2):
                    (s0, j0), (s1, j1) = cand[a], cand[a + 1]
                    right = s1 > s0
                    nxt.append((jnp.maximum(s0, s1), jnp.where(right, j1, j0)))
                cand = nxt + cand[len(cand) - len(cand) % 2:]
            s8, j8 = cand[0]
            best = jnp.max(s8, axis=0, keepdims=True)
            first = jnp.min(jnp.where(s8 == best, j8, n_slc), axis=0, keepdims=True)
            score = jnp.where(jj == first, TAKEN, score)
        neg_ref[:n_slc] = jnp.where(score == TAKEN, msel_ref[0], NEG_INF)
        neg_ref[n_slc:] = jnp.full((NEG_ROWS, Q_BLOCK), NEG_INF, F32)

    def add_chunk(st, s, v_t):
        e = jnp.exp2(s)
        return st[0] + jnp.sum(e, axis=0, keepdims=True), st[1] + _dot(v_t, e.astype(BF16))

    zero = (jnp.zeros((1, N), F32), jnp.zeros((HEAD_DIM, N), F32))

    def selected_near():
        J0 = jnp.maximum(i - 1, 0)
        bias = sbias_ref[0, pl.ds(pl.multiple_of((2 - (i - J0)) * Q_BLOCK, Q_BLOCK), 2 * Q_BLOCK), :]
        return add_chunk(zero, _dot(k_rows(ks_ref, J0, 2), q_t) + bias + sel_mask(J0, 2), v_cols(vs_ref, J0, 2))

    def selected_far(st):
        n_far = jnp.maximum(i - 1, 0)
        n_chunks = (n_far + FAR_TILES - 1) // FAR_TILES

        def tile_of(c):
            return jnp.clip(FAR_TILES * c, 0, nb - FAR_TILES)

        def scores(c):
            return _dot(k_rows(ks_ref, tile_of(c), FAR_TILES), q_t)

        def values(c, e_ref, acc):
            return acc + _dot(v_cols(vs_ref, tile_of(c), FAR_TILES), e_ref[...])

        def stage(c, st, s_in, s_out, e_in, e_out):
            acc = values(c - 1, e_in, st[1])
            mask = sel_mask(tile_of(c), FAR_TILES, n_far - FAR_TILES * c)
            k_next = k_rows(ks_ref, tile_of(c + 1), FAR_TILES)
            s_out[...] = _dot(k_next, q_t)
            e = jnp.exp2(s_in[...] + mask)
            e_out[...] = e.astype(BF16)
            return st[0] + jnp.sum(e, axis=0, keepdims=True), acc

        s_a, s_b, e_a, e_b = stage_refs
        e_b[...] = jnp.zeros_like(e_b)
        s_a[...] = scores(0)

        def pair(c, st):
            st = stage(c, st, s_a, s_b, e_b, e_a)
            return stage(c + 1, st, s_b, s_a, e_a, e_b)

        n_quads = n_chunks // 4
        st = lax.fori_loop(0, n_quads, lambda t, st: pair(4 * t + 2, pair(4 * t, st)), st)
        st = lax.cond(n_chunks % 4 >= 2, lambda st: pair(4 * n_quads, st), lambda st: st, st)

        def odd_tail(st):
            st = stage(n_chunks - 1, st, s_a, s_b, e_b, e_a)
            return st[0], values(n_chunks - 1, e_a, st[1])

        return lax.cond(n_chunks % 2 == 1, odd_tail, lambda st: (st[0], values(n_chunks - 1, e_b, st[1])), st)

    def window_fast():
        J0 = jnp.maximum(i - n_win, 0)
        bias = wbias_ref[0, pl.ds(pl.multiple_of((n_win - (i - J0)) * Q_BLOCK, Q_BLOCK), (n_win + 1) * Q_BLOCK), :]
        return add_chunk(zero, _dot(k_rows(kw_ref, J0, n_win + 1), q_t) + bias, v_cols(vw_ref, J0, n_win + 1))

    def sweep_exact(k_ref, v_ref, bias_ref, far_dist, n_back, with_sel):
        def body(t, st):
            J = i - t
            off = pl.multiple_of((far_dist - jnp.minimum(t, far_dist)) * Q_BLOCK, Q_BLOCK)
            s = _dot(k_rows(k_ref, J, 1), q_t) + bias_ref[0, pl.ds(off, Q_BLOCK), :]
            if with_sel:
                s = s + sel_mask(J, 1)
            return _softmax_tile_update(s, v_ref[0, J], *st)

        init = (jnp.full((1, N), NEG_INF, F32),) + zero
        _, l, acc = lax.fori_loop(0, jnp.minimum(i, n_back) + 1, body, init)
        return l, acc

    vals = {}

    def load_phase():
        vals["cmp"] = compressed(True)
        vals["win"] = window_fast()

    def select_phase():
        select_blocks(vals["cmp"][0])

    def near_phase():
        vals["near"] = selected_near()

    def exact_path():
        p_x, _, o_x = compressed(False)
        select_blocks(p_x)
        l_sx, a_sx = sweep_exact(ks_ref, vs_ref, sbias_ref, 2, nb, True)
        l_wx, a_wx = sweep_exact(kw_ref, vw_ref, wbias_ref, n_win, n_win, False)
        return o_x, a_sx * (1.0 / l_sx), a_wx * (1.0 / l_wx)

    def gate_row(br):
        return jnp.concatenate([gate_ref[br, 0, r:r + 1, u * Q_BLOCK:(u + 1) * Q_BLOCK] for r in range(R)], axis=1)

    def finish():
        _, l_c, o_c = vals["cmp"]
        l_w, a_w = vals["win"]
        l_s, a_s = selected_far(vals["near"])
        lane_q = lax.broadcasted_iota(jnp.int32, (1, N), 1) % Q_BLOCK
        has_cmp = (i > 0) | (lane_q >= CMP_LEN - 1)
        l_min = jnp.minimum(jnp.minimum(l_s, l_w), jnp.where(has_cmp, l_c, 1.0))
        o_cx, o_s, o_w = lax.cond(jnp.min(l_min) > NSA_TINY,
                                  lambda: (o_c, a_s * (1.0 / l_s), a_w * (1.0 / l_w)), exact_path)
        o_t = gate_row(0) * o_cx + gate_row(1) * o_s + gate_row(2) * o_w
        return _pairs_to_rows(o_t)

    return load_phase, select_phase, near_phase, finish


def _nsa_attention(colmat, rowmat, cmp_rows, cmp_cols, gates, cbias, sbias, wbias, ov_t, msel, B, S):
    nb = S // Q_BLOCK
    nc = S // CMP_STRIDE
    n_slc = S // SLC_LEN
    G, R = NSA_KV_HEADS, NSA_REP
    kern = functools.partial(_nsa_kernel, nb=nb)
    ns = nb // NSA_SUB
    return pl.pallas_call(
        kern,
        grid=(B, G, ns),
        in_specs=[
            pl.BlockSpec((R, NSA_SUB, HEAD_DIM, LANES), lambda b, g, i: (g, b * ns + i, 0, 0)),
            pl.BlockSpec((1, 1, nc, HEAD_DIM), lambda b, g, i: (g, b, 0, 0)),
            pl.BlockSpec((1, 1, HEAD_DIM, nc), lambda b, g, i: (NSA_KV_HEADS + g, b, 0, 0)),
            pl.BlockSpec((1, S, HEAD_DIM), lambda b, g, i: (ROW_KS + g, b, 0)),
            pl.BlockSpec((1, nb, HEAD_DIM, LANES), lambda b, g, i: (COL_VS + g, b, 0, 0)),
            pl.BlockSpec((1, S, HEAD_DIM), lambda b, g, i: (ROW_KW + g, b, 0)),
            pl.BlockSpec((1, nb, HEAD_DIM, LANES), lambda b, g, i: (COL_VW + g, b, 0, 0)),
            pl.BlockSpec((N_BRANCHES, 1, R, NSA_SUB * LANES), lambda b, g, i: (0, g, 0, b * ns + i)),
            pl.BlockSpec((1, cbias.shape[1], R * Q_BLOCK), lambda b, g, i: (g, 0, 0)),
            pl.BlockSpec((1, sbias.shape[1], R * Q_BLOCK), lambda b, g, i: (g, 0, 0)),
            pl.BlockSpec((1, wbias.shape[1], R * Q_BLOCK), lambda b, g, i: (g, 0, 0)),
            pl.BlockSpec((n_slc, nc), lambda b, g, i: (0, 0)),
            pl.BlockSpec((1, 1, Q_BLOCK), lambda b, g, i: (g, 0, 0)),
        ],
        out_specs=pl.BlockSpec((1, NSA_SUB * Q_BLOCK, R * HEAD_DIM), lambda b, g, i: (b, i, g)),
        out_shape=jax.ShapeDtypeStruct((B, S, NSA_HEADS * HEAD_DIM), BF16),
        scratch_shapes=[pltpu.VMEM((n_slc + NEG_ROWS, Q_BLOCK), F32)] * NSA_SUB
        + [pltpu.VMEM((FAR_TILES * Q_BLOCK, R * Q_BLOCK), F32)] * 2
        + [pltpu.VMEM((FAR_TILES * Q_BLOCK, R * Q_BLOCK), BF16)] * 2,
        compiler_params=_cparams(("parallel", "parallel", "arbitrary")),
        name="nsa_attention",
    )(colmat, cmp_rows, cmp_cols, rowmat, colmat, rowmat, colmat, gates, cbias, sbias, wbias, ov_t, msel)


SB_QB = 512
SB_KT = 256
SB_ZERO_LOG = -160.0


def _sb_kernel(q_ref, k_ref, v_ref, tri_ref, o_ref):
    i = pl.program_id(1)
    H = SB_HEADS
    tri = tri_ref[...]
    assert SB_QB == 2 * SB_KT
    below_diag = (lax.broadcasted_iota(jnp.int32, (SB_KT, SB_KT), 1)
                  < lax.broadcasted_iota(jnp.int32, (SB_KT, SB_KT), 0))

    def step(J, rows, st, masked):
        out = []
        for h in range(H):
            carry, acc = st[h]
            k_t = jnp.concatenate([k_ref[h, (SB_KT // LANES) * J + c] for c in range(SB_KT // LANES)], axis=1)
            z = _dot(q_ref[h, rows, :], k_t)
            sp = jnp.maximum(z, 0.0) + jnp.log2(1.0 + jnp.exp2(-jnp.abs(z)))
            lf = jnp.where(below_diag, -sp, 0.0) if masked else -sp
            lf_hi = lf.astype(BF16)
            lf_lo = (lf - lf_hi.astype(F32)).astype(BF16)
            after = _dot(lf_hi, tri) + _dot(lf_lo, tri) + carry
            w = jnp.exp2(z - sp + after)
            if masked:
                w = jnp.where(below_diag, w, 0.0)
            v = v_ref[h, pl.ds(pl.multiple_of(J * SB_KT, SB_KT), SB_KT), :]
            acc = acc + _dot(w.astype(BF16), v)
            carry = carry + jnp.sum(lf, axis=1, keepdims=True)
            out.append((carry, acc))
        return tuple(out)

    lo, hi = slice(0, SB_KT), slice(SB_KT, SB_QB)
    zero = tuple((jnp.zeros((SB_KT, 1), F32), jnp.zeros((SB_KT, HEAD_DIM), F32)) for _ in range(H))
    st_hi = step(2 * i, hi, step(2 * i + 1, hi, zero, True), False)
    st_lo = step(2 * i, lo, zero, True)
    st = tuple(tuple(jnp.concatenate([a, b], axis=0) for a, b in zip(st_lo[h], st_hi[h])) for h in range(H))
    per_q = SB_QB // SB_KT
    all_rows = slice(0, SB_QB)

    def alive(st):
        worst = st[0][0]
        for h in range(1, H):
            worst = jnp.maximum(worst, st[h][0])
        return jnp.max(worst) > SB_ZERO_LOG

    def body(c):
        J, _, st = c
        st = step(J, all_rows, st, False)
        return J - 1, alive(st), st

    _, _, st = lax.while_loop(lambda c: (c[0] >= 0) & c[1], body, (per_q * i - 1, alive(st), st))
    o_ref[0] = jnp.concatenate([st[h][1] for h in range(H)], axis=1).astype(o_ref.dtype)


def _sb_attention(colmat, rowmat, tri, B, S):
    nq = S // SB_QB
    nb = S // LANES
    H = SB_HEADS
    return pl.pallas_call(
        _sb_kernel,
        grid=(B, nq),
        in_specs=[
            pl.BlockSpec((H, SB_QB, HEAD_DIM), lambda b, i: (ROW_SQ // H, b * nq + i, 0)),
            pl.BlockSpec((H, nb, HEAD_DIM, LANES), lambda b, i: (COL_SK // H, b, 0, 0)),
            pl.BlockSpec((H, S, HEAD_DIM), lambda b, i: (ROW_SV // H, b, 0)),
            pl.BlockSpec((SB_KT, SB_KT), lambda b, i: (0, 0)),
        ],
        out_specs=pl.BlockSpec((1, SB_QB, H * HEAD_DIM), lambda b, i: (b, i, 0)),
        out_shape=jax.ShapeDtypeStruct((B, S, H * HEAD_DIM), BF16),
        compiler_params=_cparams(("parallel", "arbitrary")),
        name="sb_attention",
    )(rowmat, colmat, rowmat, tri)


SWA_SUB = 4


def _swa_kernel(q_ref, k_ref, v_ref, bias_ref, sink_ref, o_ref):
    R = SWA_REP
    sink = sink_ref[0]
    for u in range(SWA_SUB):
        i = pl.program_id(2) * SWA_SUB + u
        q_t = jnp.concatenate([q_ref[r, u] for r in range(R)], axis=1)
        J0 = jnp.maximum(i - 1, 0)
        k = k_ref[0, pl.ds(pl.multiple_of(J0 * Q_BLOCK, Q_BLOCK), 2 * Q_BLOCK), :]
        bias = bias_ref[0, pl.ds(pl.multiple_of((1 - (i - J0)) * Q_BLOCK, Q_BLOCK), 2 * Q_BLOCK), :]
        s = _dot(k, q_t) + bias
        m = jnp.maximum(jnp.max(s, axis=0, keepdims=True), sink)
        e = jnp.exp(s - m)
        den = jnp.sum(e, axis=0, keepdims=True) + jnp.exp(sink - m)
        v_t = jnp.concatenate([v_ref[0, J0], v_ref[0, J0 + 1]], axis=1)
        o_t = _dot(v_t, e.astype(BF16)) * (1.0 / den)
        o_ref[0, u * Q_BLOCK:(u + 1) * Q_BLOCK, :] = _pairs_to_rows(o_t).astype(o_ref.dtype)


def _swa_attention(colmat, rowmat, bias, sinks, B, S):
    nb = S // Q_BLOCK
    ns = nb // SWA_SUB
    G, R = SWA_KV_HEADS, SWA_REP
    return pl.pallas_call(
        _swa_kernel,
        grid=(B, G, ns),
        in_specs=[
            pl.BlockSpec((R, SWA_SUB, HEAD_DIM, LANES), lambda b, g, i: (COL_WQ // R + g, b * ns + i, 0, 0)),
            pl.BlockSpec((1, S, HEAD_DIM), lambda b, g, i: (ROW_WK + g, b, 0)),
            pl.BlockSpec((1, nb, HEAD_DIM, LANES), lambda b, g, i: (COL_WV + g, b, 0, 0)),
            pl.BlockSpec((1, 3 * Q_BLOCK, R * Q_BLOCK), lambda b, g, i: (g, 0, 0)),
            pl.BlockSpec((1, 1, R * Q_BLOCK), lambda b, g, i: (g, 0, 0)),
        ],
        out_specs=pl.BlockSpec((1, SWA_SUB * Q_BLOCK, R * HEAD_DIM), lambda b, g, i: (b, i, g)),
        out_shape=jax.ShapeDtypeStruct((B, S, SWA_HEADS * HEAD_DIM), BF16),
        compiler_params=_cparams(("parallel", "parallel", "arbitrary")),
        name="swa_attention",
    )(colmat, rowmat, colmat, bias, sinks)


MERGE_TM = 512


def _merge_kernel(x_ref, nw_ref, wg_ref, ya_ref, yb_ref, yc_ref, ua_ref, ub_ref, uc_ref, wo_ref, o_ref):
    x = x_ref[...]
    h = _rms_rows(x, nw_ref[...]).astype(BF16)
    merged = None
    for br, (y_ref, u_ref) in enumerate(((ya_ref, ua_ref), (yb_ref, ub_ref), (yc_ref, uc_ref))):
        g = jax.nn.sigmoid(_dot(h, wg_ref[:, br * D_MODEL:(br + 1) * D_MODEL]))
        t = g * _dot(y_ref[...], u_ref[...])
        merged = t if merged is None else merged + t
    o_ref[...] = x + _dot(merged.astype(BF16), wo_ref[...])


def _merge(x2, nw, w_bgate, y_nsa, y_sb, y_swa, u_nsa, u_sb, u_swa, w_out):
    T = x2.shape[0]
    full = lambda shape: pl.BlockSpec(shape, lambda i: (0, 0))
    rows = lambda n: pl.BlockSpec((MERGE_TM, n), lambda i: (i, 0))
    return pl.pallas_call(
        _merge_kernel,
        grid=(T // MERGE_TM,),
        in_specs=[rows(D_MODEL), full((1, D_MODEL)), full((D_MODEL, N_BRANCHES * D_MODEL)),
                  rows(y_nsa.shape[1]), rows(y_sb.shape[1]), rows(y_swa.shape[1]),
                  full(u_nsa.shape), full(u_sb.shape), full(u_swa.shape), full((D_MODEL, D_MODEL))],
        out_specs=rows(D_MODEL),
        out_shape=jax.ShapeDtypeStruct((T, D_MODEL), F32),
        compiler_params=_cparams(("parallel",)),
        name="merge",
    )(x2, nw.reshape(1, D_MODEL), w_bgate.astype(BF16), y_nsa, y_sb, y_swa,
      u_nsa.astype(BF16), u_sb.astype(BF16), u_swa.astype(BF16), w_out.astype(BF16))


def _t5_bucket(dist):
    max_exact = NUM_BUCKETS // 2
    d = jnp.maximum(dist, 0)
    df = jnp.maximum(d, 1).astype(F32)
    large = max_exact + (jnp.log(df / max_exact) / math.log(MAX_DISTANCE / max_exact)
                         * (NUM_BUCKETS - max_exact)).astype(jnp.int32)
    large = jnp.minimum(large, NUM_BUCKETS - 1)
    return jnp.where(d < max_exact, d, large)


def _toeplitz(ext, delta):
    H = ext.shape[0]
    n = Q_BLOCK
    c = delta + MAX_DISTANCE
    w = ext[:, c - (n - 1):c + n]
    flat = jnp.pad(jnp.broadcast_to(w[:, None, :], (H, n, 2 * n - 1)), ((0, 0), (0, 0), (0, 1))).reshape(H, 2 * n * n)
    return flat[:, n - 1:n - 1 + n * (2 * n - 1)].reshape(H, n, 2 * n - 1)[:, :, :n]


def _lanes(t, G, R):
    K, Q = t.shape[1:]
    return t.reshape(G, R, K, Q).transpose(0, 2, 1, 3).reshape(G, K, R * Q)


def _bias_tables(rel_bias, nb, qk_bound):
    n = Q_BLOCK
    G, R = NSA_KV_HEADS, NSA_REP
    by_dist = rel_bias[_t5_bucket(jnp.arange(MAX_DISTANCE)), :].T.astype(F32)
    far = rel_bias[NUM_BUCKETS - 1, :].astype(F32)
    neg = jnp.full((1, n), NEG_INF, F32)
    k = jnp.arange(n)[:, None]
    q = jnp.arange(n)[None, :]
    above = (k > q)[None]

    rel = (by_dist[:NSA_HEADS] - far[:NSA_HEADS, None]) * LOG2E
    m_g = qk_bound + jnp.maximum(jnp.max(rel.reshape(G, -1), axis=1), 0.0)
    m_h = jnp.repeat(m_g, R)[:, None, None]
    ext = jnp.concatenate([jnp.broadcast_to(neg, (NSA_HEADS, n)), rel, jnp.zeros((NSA_HEADS, n), F32)], axis=1)
    t0, t1 = _toeplitz(ext, 0), _toeplitz(ext, n)
    zeros = jnp.zeros_like(t0)
    masked = jnp.full_like(t0, NEG_INF)
    sbias = _lanes(jnp.concatenate([zeros, t1, t0, masked], axis=1), G, R)
    n_win = NSA_WINDOW // n
    edge = jnp.where(above, zeros, NEG_INF)
    tiles = [edge] + [zeros] * (n_win - 2) + [t1, t0]
    wbias = _lanes(jnp.concatenate([t - m_h for t in tiles] + [masked] * n_win, axis=1), G, R)
    off = 8 * (nb - 1)
    near = jnp.stack([ext[:, 97 - CMP_STRIDE * m:97 - CMP_STRIDE * m + n] for m in range(-9, 7)], axis=1)
    n_rows = 16 * nb - 8
    cb = jnp.concatenate([jnp.zeros((NSA_HEADS, off - 9, n), F32), near,
                          jnp.full((NSA_HEADS, n_rows - off - 7, n), NEG_INF, F32)], axis=1) - m_h
    cbias = _lanes(cb, G, R)
    msel = jnp.broadcast_to(-m_g[:, None, None], (G, 1, n))

    hs = slice(NSA_HEADS, NSA_HEADS + SWA_HEADS)
    ext_w = jnp.concatenate([jnp.broadcast_to(neg, (SWA_HEADS, n)), by_dist[hs],
                             jnp.broadcast_to(far[hs, None], (SWA_HEADS, n))], axis=1)
    s0 = _toeplitz(ext_w, 0)
    s1 = jnp.where(above, _toeplitz(ext_w, n), NEG_INF)
    swa_bias = _lanes(jnp.concatenate([s1, s0, jnp.full_like(s0, NEG_INF)], axis=1), SWA_KV_HEADS, SWA_REP)
    return cbias, sbias, wbias, msel, swa_bias


def _overlap_t(S):
    nc, n_slc = S // CMP_STRIDE, S // SLC_LEN
    n = np.arange(nc)[None, :]
    j = np.arange(n_slc)[:, None]
    ov = (n * CMP_STRIDE < (j + 1) * SLC_LEN) & (n * CMP_STRIDE + CMP_LEN - 1 >= j * SLC_LEN) & (n < nc - 1)
    return jnp.asarray(ov, BF16)


def _head_gains(heads, norm_of, scale_of):
    ones = jnp.ones((HEAD_DIM,), F32)
    gains = [norm_of[name].astype(F32) * scale_of.get(name, 1.0) if name in norm_of else ones for name, _ in heads]
    return jnp.concatenate(gains).reshape(1, -1)


def kernel(x, rel_bias, ffn1_norm, ffn1_w_gate, ffn1_w_up, ffn1_w_down, mix_norm, w_in, nsa_q_norm, nsa_k_norm, nsa_cmp_pos, nsa_cmp_k_w1, nsa_cmp_k_w2, nsa_cmp_v_w1, nsa_cmp_v_w2, swa_q_norm, swa_k_norm, swa_sinks, w_up_nsa, w_up_sb, w_up_swa, w_out, ffn2_norm, ffn2_w_gate, ffn2_w_up, ffn2_w_down):
    B, S, D = x.shape
    T = B * S
    nb = S // Q_BLOCK
    nc = S // CMP_STRIDE
    depth = w_in.shape[0]
    scale = HEAD_DIM ** -0.5
    scale_of = {"nq": scale * LOG2E, "sq": scale * LOG2E, "wq": scale}
    qk_bound = (BOUND_SLACK * LOG2E * HEAD_DIM ** 0.5
                * jnp.max(jnp.abs(nsa_q_norm.astype(F32))) * jnp.max(jnp.abs(nsa_k_norm.astype(F32))))
    cbias, sbias, wbias, msel, swa_bias = _bias_tables(rel_bias, nb, qk_bound)
    plan = _proj_plan(scale_of)
    ov_t = _overlap_t(S)
    tri = jnp.asarray(np.tril(np.ones((SB_KT, SB_KT), np.float32), -1), BF16)

    (ffn1_w_gate, ffn1_w_up, ffn1_w_down, ffn2_w_gate, ffn2_w_up, ffn2_w_down, w_up_nsa, w_up_sb, w_up_swa,
     w_out) = (w.astype(BF16) for w in (ffn1_w_gate, ffn1_w_up, ffn1_w_down, ffn2_w_gate, ffn2_w_up, ffn2_w_down,
                                        w_up_nsa, w_up_sb, w_up_swa, w_out))
    w_bgate = w_in[:, :, OFF_BGATE:].astype(BF16)
    w_gate = jnp.pad(w_in[:, :, OFF_NGATE:OFF_NGATE + N_GATE], ((0, 0), (0, 0), (0, LANES - N_GATE)))
    runs = []
    for _, off in ROW_HEADS + COL_HEADS:
        if runs and runs[-1][1] == off:
            runs[-1][1] = off + HEAD_DIM
        else:
            runs.append([off, off + HEAD_DIM])
    w_proj = jnp.concatenate([w_in[:, :, a:b] for a, b in runs] + [w_gate], axis=2).astype(BF16)

    x2 = x.reshape(T, D)
    for l in range(depth):
        x2 = _ffn(x2, ffn1_norm[l], ffn1_w_gate[l], ffn1_w_up[l], ffn1_w_down[l])

        norm_of = {"nq": nsa_q_norm[l], "nks": nsa_k_norm[l], "nkw": nsa_k_norm[l],
                   "wq": swa_q_norm[l], "wk": swa_k_norm[l]}
        gain = jnp.concatenate([_head_gains(ROW_HEADS, norm_of, scale_of), _head_gains(COL_HEADS, norm_of, scale_of),
                                jnp.ones((1, LANES), F32)], axis=1)
        rowmat, colmat, gates = _proj(x2, mix_norm[l], w_proj[l], gain, plan)
        gates = gates[:N_GATE].reshape(N_BRANCHES, NSA_KV_HEADS, NSA_REP, T)

        rows4 = rowmat[ROW_KC:ROW_KC + 2 * NSA_KV_HEADS].reshape(2 * NSA_KV_HEADS, B, nc, CMP_STRIDE * HEAD_DIM)
        cmp_rows, cmp_cols = _compress(rows4, jnp.stack([nsa_cmp_k_w1[l], nsa_cmp_v_w1[l]]),
                                       jnp.stack([nsa_cmp_k_w2[l], nsa_cmp_v_w2[l]]),
                                       nsa_cmp_pos[l], nsa_k_norm[l])

        y_nsa = _nsa_attention(colmat, rowmat, cmp_rows, cmp_cols, gates, cbias, sbias, wbias, ov_t, msel, B, S)
        y_sb = _sb_attention(colmat, rowmat, tri, B, S)
        sinks = jnp.repeat(swa_sinks[l].astype(F32).reshape(SWA_KV_HEADS, 1, SWA_REP), Q_BLOCK, axis=2)
        y_swa = _swa_attention(colmat, rowmat, swa_bias, sinks, B, S)

        x2 = _merge(x2, mix_norm[l], w_bgate[l], y_nsa.reshape(T, -1), y_sb.reshape(T, -1),
                    y_swa.reshape(T, -1), w_up_nsa[l], w_up_sb[l], w_up_swa[l], w_out[l])
        x2 = _ffn(x2, ffn2_norm[l], ffn2_w_gate[l], ffn2_w_up[l], ffn2_w_down[l])
    return x2.reshape(B, S, D)
```

```python
import functools
import math

import numpy as np
import jax
import jax.numpy as jnp
from jax import lax
from jax.experimental import pallas as pl
from jax.experimental.pallas import tpu as pltpu

D_MODEL = 1024
HEAD_DIM = 64
Q_BLOCK = 128
NSA_HEADS = 8
NSA_KV_HEADS = 2
NSA_REP = NSA_HEADS // NSA_KV_HEADS
CMP_LEN = 32
CMP_STRIDE = 16
CMP_HIDDEN = 256
SLC_LEN = 64
SLC_TOPN = 16
NSA_WINDOW = 512
SB_HEADS = 4
SWA_HEADS = 4
SWA_KV_HEADS = 2
SWA_REP = SWA_HEADS // SWA_KV_HEADS
SWA_WINDOW = 128
NUM_BUCKETS = 32
MAX_DISTANCE = 128
D_FF = 2816
NORM_EPS = 1e-6
NEG_INF = -1e30
N_BRANCHES = 3
N_GATE = 3 * NSA_HEADS

LANES = 128
VMEM_LIMIT = 56 * 1024 * 1024
BF16 = jnp.bfloat16
F32 = jnp.float32

_sizes = (NSA_HEADS * HEAD_DIM,) + (NSA_KV_HEADS * HEAD_DIM,) * 6 + (N_GATE,) \
    + (SB_HEADS * HEAD_DIM,) * 3 + (SWA_HEADS * HEAD_DIM, SWA_KV_HEADS * HEAD_DIM,
                                    SWA_KV_HEADS * HEAD_DIM, N_BRANCHES * D_MODEL)
_offs = np.concatenate([[0], np.cumsum(_sizes)])
(OFF_NQ, OFF_NKC, OFF_NVC, OFF_NKS, OFF_NVS, OFF_NKW, OFF_NVW, OFF_NGATE,
 OFF_SQ, OFF_SK, OFF_SV, OFF_WQ, OFF_WK, OFF_WV, OFF_BGATE) = (int(o) for o in _offs[:-1])

ROW_HEADS = ([("nkc", OFF_NKC + 64 * g) for g in range(2)] + [("nvc", OFF_NVC + 64 * g) for g in range(2)]
             + [("nks", OFF_NKS + 64 * g) for g in range(2)] + [("nkw", OFF_NKW + 64 * g) for g in range(2)]
             + [("sq", OFF_SQ + 64 * h) for h in range(4)] + [("sv", OFF_SV + 64 * h) for h in range(4)]
             + [("wk", OFF_WK + 64 * g) for g in range(2)])
ROW_KC, ROW_VC, ROW_KS, ROW_KW, ROW_SQ, ROW_SV, ROW_WK = 0, 2, 4, 6, 8, 12, 16
COL_HEADS = ([("nq", OFF_NQ + 64 * h) for h in range(8)] + [("nvs", OFF_NVS + 64 * g) for g in range(2)]
             + [("nvw", OFF_NVW + 64 * g) for g in range(2)] + [("sk", OFF_SK + 64 * h) for h in range(4)]
             + [("wq", OFF_WQ + 64 * h) for h in range(4)] + [("wv", OFF_WV + 64 * g) for g in range(2)])
COL_NQ, COL_VS, COL_VW, COL_SK, COL_WQ, COL_WV = 0, 8, 10, 12, 16, 20


def _dot(a, b):
    return jnp.dot(a, b, preferred_element_type=F32)


def _rms_rows(x, w):
    ms = jnp.mean(x * x, axis=-1, keepdims=True)
    return x * lax.rsqrt(ms + NORM_EPS) * w


def _cparams(sem, flags=None):
    return pltpu.CompilerParams(dimension_semantics=sem, vmem_limit_bytes=VMEM_LIMIT, flags=flags)


FFN_TM = 512
MXU_TILE = 256
FFN_FC = MXU_TILE


def _ffn_kernel(x_ref, nw_ref, wg_ref, wu_ref, wd_ref, o_ref):
    x = x_ref[...]
    h = _rms_rows(x, nw_ref[...]).astype(BF16)
    acc = jnp.zeros(x.shape, F32)
    for c in range(D_FF // FFN_FC):
        cols = slice(c * FFN_FC, (c + 1) * FFN_FC)
        g = _dot(h, wg_ref[:, cols])
        u = _dot(h, wu_ref[:, cols])
        a = (g * jax.nn.sigmoid(g)) * u
        acc = acc + _dot(a.astype(BF16), wd_ref[cols, :])
    o_ref[...] = x + 0.5 * acc


def _ffn(x2, nw, wg, wu, wd):
    T = x2.shape[0]
    resident = lambda shape: pl.BlockSpec(shape, lambda i: (0, 0), pipeline_mode=pl.Buffered(1))
    return pl.pallas_call(
        _ffn_kernel,
        grid=(T // FFN_TM,),
        in_specs=[
            pl.BlockSpec((FFN_TM, D_MODEL), lambda i: (i, 0)),
            pl.BlockSpec((1, D_MODEL), lambda i: (0, 0)),
            resident((D_MODEL, D_FF)), resident((D_MODEL, D_FF)), resident((D_FF, D_MODEL)),
        ],
        out_specs=pl.BlockSpec((FFN_TM, D_MODEL), lambda i: (i, 0)),
        out_shape=jax.ShapeDtypeStruct((T, D_MODEL), F32),
        compiler_params=_cparams(("parallel",)),
        name="ffn",
    )(x2, nw.reshape(1, D_MODEL), wg.astype(BF16), wu.astype(BF16), wd.astype(BF16))


PROJ_TM = 512


GATE_ROWS = 32

_NORMED = ("nq", "nks", "nkw", "wq", "wk")


def _proj_plan(scale_of):
    plan = []
    for kind, heads in (("row", ROW_HEADS), ("col", COL_HEADS)):
        for p in range(len(heads) // 2):
            name = heads[2 * p][0]
            assert heads[2 * p + 1][0] == name
            plan.append((kind, p, name in _NORMED, 1.0 if name in _NORMED else scale_of.get(name, 1.0)))
    plan.append(("gate", 0, False, 1.0))
    return tuple(plan)


def _proj_kernel(x_ref, nw_ref, w_ref, gain_ref, bd_ref, row_ref, col_ref, gate_ref, *, plan):
    h = _rms_rows(x_ref[...], nw_ref[...]).astype(BF16)
    pair = 2 * MXU_TILE
    for c0 in range(0, len(plan) * LANES, pair):
        width = min(pair, len(plan) * LANES - c0)
        zg = _dot(h, w_ref[:, c0:c0 + width])
        for u in range(width // LANES):
            t = c0 // LANES + u
            kind, p, normed, scale = plan[t]
            z = zg[:, u * LANES:(u + 1) * LANES]
            if normed:
                ms = _dot((z * z).astype(BF16), bd_ref[...])
                z = z * lax.rsqrt(ms + NORM_EPS) * gain_ref[:, t * LANES:(t + 1) * LANES]
            elif scale != 1.0:
                z = z * scale
            if kind == "row":
                row_ref[2 * p] = z[:, :HEAD_DIM].astype(row_ref.dtype)
                row_ref[2 * p + 1] = z[:, HEAD_DIM:].astype(row_ref.dtype)
            elif kind == "col":
                z_t = z.T
                for c in range(PROJ_TM // LANES):
                    col_ref[2 * p, c] = z_t[:HEAD_DIM, c * LANES:(c + 1) * LANES].astype(col_ref.dtype)
                    col_ref[2 * p + 1, c] = z_t[HEAD_DIM:, c * LANES:(c + 1) * LANES].astype(col_ref.dtype)
            else:
                gate_ref[...] = jax.nn.sigmoid(z).T[:GATE_ROWS]


def _proj(x2, nw, w, gain, plan):
    T = x2.shape[0]
    n_row, n_col = len(ROW_HEADS), len(COL_HEADS)
    bd = np.kron(np.eye(2, dtype=np.float32), np.full((HEAD_DIM, HEAD_DIM), 1.0 / HEAD_DIM, np.float32))
    return pl.pallas_call(
        functools.partial(_proj_kernel, plan=plan),
        grid=(T // PROJ_TM,),
        in_specs=[
            pl.BlockSpec((PROJ_TM, D_MODEL), lambda i: (i, 0)),
            pl.BlockSpec((1, D_MODEL), lambda i: (0, 0)),
            pl.BlockSpec(w.shape, lambda i: (0, 0)),
            pl.BlockSpec(gain.shape, lambda i: (0, 0)),
            pl.BlockSpec((LANES, LANES), lambda i: (0, 0)),
        ],
        out_specs=[
            pl.BlockSpec((n_row, PROJ_TM, HEAD_DIM), lambda i: (0, i, 0)),
            pl.BlockSpec((n_col, PROJ_TM // LANES, HEAD_DIM, LANES), lambda i: (0, i, 0, 0)),
            pl.BlockSpec((GATE_ROWS, PROJ_TM), lambda i: (0, i)),
        ],
        out_shape=[jax.ShapeDtypeStruct((n_row, T, HEAD_DIM), BF16),
                   jax.ShapeDtypeStruct((n_col, T // LANES, HEAD_DIM, LANES), BF16),
                   jax.ShapeDtypeStruct((GATE_ROWS, T), F32)],
        compiler_params=_cparams(("parallel",)),
        name="proj",
    )(x2, nw.reshape(1, D_MODEL), w, gain, jnp.asarray(bd, BF16))


def _compress_kernel(r_ref, w1_ref, w2_ref, pos_ref, kn_ref, row_ref, col_ref):
    kind = pl.program_id(0) // NSA_KV_HEADS
    r = r_ref[0, 0]
    nc = r.shape[0]
    half = CMP_STRIDE * HEAD_DIM
    w1 = w1_ref[0]
    p_lo = _dot(r, w1[:half])
    p_hi = _dot(r, w1[half:])
    p_pos = _dot(jnp.broadcast_to(pos_ref[...], (8, CMP_LEN * HEAD_DIM)).astype(BF16), w1)[:1]
    hdn = p_lo + pltpu.roll(p_hi, nc - 1, 0) + p_pos
    hdn = hdn * jax.nn.sigmoid(hdn)
    out = _dot(hdn.astype(BF16), w2_ref[0])
    ms = jnp.sum(out * out, axis=-1, keepdims=True) * (1.0 / HEAD_DIM)
    normed = out * lax.rsqrt(ms + NORM_EPS) * kn_ref[...]
    out = jnp.where(kind == 0, normed, out)
    row_ref[0, 0] = out[:, :HEAD_DIM].astype(row_ref.dtype)
    col_ref[0, 0] = out.T[:HEAD_DIM].astype(col_ref.dtype)


def _compress(rows4, w1, w2, pos, k_norm):
    assert ROW_KC == 0 and ROW_VC == NSA_KV_HEADS
    _, B, NC, _ = rows4.shape
    kn = jnp.pad(k_norm.reshape(1, HEAD_DIM), ((0, 0), (0, LANES - HEAD_DIM)))
    w2p = jnp.pad(w2, ((0, 0), (0, 0), (0, LANES - HEAD_DIM))).astype(BF16)
    return pl.pallas_call(
        _compress_kernel,
        grid=(4, B),
        in_specs=[
            pl.BlockSpec((1, 1, NC, CMP_STRIDE * HEAD_DIM), lambda h, b: (h, b, 0, 0)),
            pl.BlockSpec((1, CMP_LEN * HEAD_DIM, CMP_HIDDEN), lambda h, b: (h // NSA_KV_HEADS, 0, 0)),
            pl.BlockSpec((1, CMP_HIDDEN, LANES), lambda h, b: (h // NSA_KV_HEADS, 0, 0)),
            pl.BlockSpec((1, CMP_LEN * HEAD_DIM), lambda h, b: (0, 0)),
            pl.BlockSpec((1, LANES), lambda h, b: (0, 0)),
        ],
        out_specs=[
            pl.BlockSpec((1, 1, NC, HEAD_DIM), lambda h, b: (h, b, 0, 0)),
            pl.BlockSpec((1, 1, HEAD_DIM, NC), lambda h, b: (h, b, 0, 0)),
        ],
        out_shape=[jax.ShapeDtypeStruct((4, B, NC, HEAD_DIM), BF16),
                   jax.ShapeDtypeStruct((4, B, HEAD_DIM, NC), BF16)],
        compiler_params=_cparams(("parallel", "parallel")),
        name="nsa_compress",
    )(rows4, w1.astype(BF16), w2p, pos.reshape(1, CMP_LEN * HEAD_DIM), kn)


def _softmax_tile_update(s, v_t, m, l, acc):
    m_new = jnp.maximum(m, jnp.max(s, axis=0, keepdims=True))
    alpha = jnp.exp2(m - m_new)
    p = jnp.exp2(s - m_new)
    l_new = alpha * l + jnp.sum(p, axis=0, keepdims=True)
    acc_new = alpha * acc + _dot(v_t, p.astype(BF16))
    return m_new, l_new, acc_new


def _pairs_to_rows(o_t):
    n = o_t.shape[1] // LANES
    outs = []
    for p in range(n // 2):
        a = o_t[:, (2 * p) * LANES:(2 * p + 1) * LANES]
        b = o_t[:, (2 * p + 1) * LANES:(2 * p + 2) * LANES]
        outs.append(jnp.concatenate([a, b], axis=0).T)
    return outs[0] if len(outs) == 1 else jnp.concatenate(outs, axis=1)


FAR_TILES = 4
NSA_SUB = 2
TAKEN = -2.0
NEG_ROWS = 8
N_FORCED = 3
LOG2E = math.log2(math.e)
NSA_TINY = 2.0 ** -100
BOUND_SLACK = 1.05


def _nsa_kernel(q_ref, kc_ref, vc_ref, ks_ref, vs_ref, kw_ref, vw_ref, gate_ref,
                cbias_ref, sbias_ref, wbias_ref, ov_ref, msel_ref, o_ref, *scratch, nb):
    neg_refs, stage_refs = scratch[:NSA_SUB], scratch[NSA_SUB:]
    refs = (q_ref, kc_ref, vc_ref, ks_ref, vs_ref, kw_ref, vw_ref, gate_ref, cbias_ref, sbias_ref, wbias_ref,
            ov_ref, msel_ref)
    blocks = [_nsa_block(u, refs, neg_refs[u], stage_refs, nb) for u in range(NSA_SUB)]
    for phase in range(3):
        for blk in blocks:
            blk[phase]()
    rows = [blk[3]() for blk in blocks]
    for u in range(NSA_SUB):
        o_ref[0, u * Q_BLOCK:(u + 1) * Q_BLOCK, :] = rows[u].astype(o_ref.dtype)


def _nsa_block(u, refs, neg_ref, stage_refs, nb):
    (q_ref, kc_ref, vc_ref, ks_ref, vs_ref, kw_ref, vw_ref, gate_ref, cbias_ref, sbias_ref, wbias_ref,
     ov_ref, msel_ref) = refs
    i = pl.program_id(2) * NSA_SUB + u
    R = NSA_REP
    N = R * Q_BLOCK
    nc = kc_ref.shape[2]
    q_t = jnp.concatenate([q_ref[r, u] for r in range(R)], axis=1)
    n_win = NSA_WINDOW // Q_BLOCK

    def k_rows(ref, J0, n_tiles):
        return ref[0, pl.ds(pl.multiple_of(J0 * Q_BLOCK, Q_BLOCK), n_tiles * Q_BLOCK), :]

    def v_cols(ref, J0, n_tiles):
        return jnp.concatenate([ref[0, J0 + u] for u in range(n_tiles)], axis=1)

    def sel_mask(J0, n_tiles, n_valid=None):
        n_slc = neg_ref.shape[0] - NEG_ROWS
        parts = []
        for u in range(2 * n_tiles):
            row = 2 * J0 + u
            if n_valid is not None:
                row = jnp.where(u // 2 < n_valid, row, n_slc)
            parts.append(jnp.broadcast_to(neg_ref[pl.ds(row, 1), :], (SLC_LEN, Q_BLOCK)))
        return jnp.concatenate([jnp.concatenate(parts, axis=0)] * R, axis=1)

    def compressed(fast):
        cb = cbias_ref[0, pl.ds(pl.multiple_of(8 * (nb - 1) - 8 * i, 8), nc), :]
        s = _dot(kc_ref[0, 0], q_t) + cb
        if fast:
            e = jnp.exp2(s)
        else:
            e = jnp.where(s > 0.5 * NEG_INF, jnp.exp2(s - jnp.max(s, axis=0, keepdims=True)), 0.0)
        l = jnp.sum(e, axis=0, keepdims=True)
        p = e * jnp.where(l > 0.0, 1.0 / l, 0.0)
        return p, l, _dot(vc_ref[0, 0], p.astype(BF16))

    def select_blocks(p_c):
        p_sum = p_c[:, 0:Q_BLOCK]
        for r in range(1, R):
            p_sum = p_sum + p_c[:, r * Q_BLOCK:(r + 1) * Q_BLOCK]
        p_hi = p_sum.astype(BF16)
        p_lo = (p_sum - p_hi.astype(F32)).astype(BF16)
        imp = _dot(ov_ref[...], p_hi) + _dot(ov_ref[...], p_lo)
        n_slc = imp.shape[0]
        jj = lax.broadcasted_iota(jnp.int32, imp.shape, 0)
        qq = lax.broadcasted_iota(jnp.int32, imp.shape, 1)
        cur = 2 * i + jnp.where(qq >= SLC_LEN, 1, 0)
        forced = (jj == 0) | (jj == cur) | (jj == cur - 1)
        score = jnp.where(forced, TAKEN, jnp.where(jj <= cur, imp, -1.0))
        grp = 8
        j_rows = [jj[a:a + grp] for a in range(0, n_slc, grp)]
        for _ in range(min(SLC_TOPN, n_slc) - N_FORCED):
            cand = [(score[a:a + grp], j_rows[a // grp]) for a in range(0, n_slc, grp)]
            while len(cand) > 1:
                nxt = []
                for a in range(0, len(cand) - 1, 2):
                    (s0, j0), (s1, j1) = cand[a], cand[a + 1]
                    right = s1 > s0
                    nxt.append((jnp.maximum(s0, s1), jnp.where(right, j1, j0)))
                cand = nxt + cand[len(cand) - len(cand) % 2:]
            s8, j8 = cand[0]
            best = jnp.max(s8, axis=0, keepdims=True)
            first = jnp.min(jnp.where(s8 == best, j8, n_slc), axis=0, keepdims=True)
            score = jnp.where(jj == first, TAKEN, score)
        neg_ref[:n_slc] = jnp.where(score == TAKEN, msel_ref[0], NEG_INF)
        neg_ref[n_slc:] = jnp.full((NEG_ROWS, Q_BLOCK), NEG_INF, F32)

    def add_chunk(st, s, v_t):
        e = jnp.exp2(s)
        return st[0] + jnp.sum(e, axis=0, keepdims=True), st[1] + _dot(v_t, e.astype(BF16))

    zero = (jnp.zeros((1, N), F32), jnp.zeros((HEAD_DIM, N), F32))

    def selected_near():
        J0 = jnp.maximum(i - 1, 0)
        bias = sbias_ref[0, pl.ds(pl.multiple_of((2 - (i - J0)) * Q_BLOCK, Q_BLOCK), 2 * Q_BLOCK), :]
        return add_chunk(zero, _dot(k_rows(ks_ref, J0, 2), q_t) + bias + sel_mask(J0, 2), v_cols(vs_ref, J0, 2))

    def selected_far(st):
        n_far = jnp.maximum(i - 1, 0)
        n_chunks = (n_far + FAR_TILES - 1) // FAR_TILES

        def tile_of(c):
            return jnp.clip(FAR_TILES * c, 0, nb - FAR_TILES)

        def scores(c):
            return _dot(k_rows(ks_ref, tile_of(c), FAR_TILES), q_t)

        def values(c, e_ref, acc):
            return acc + _dot(v_cols(vs_ref, tile_of(c), FAR_TILES), e_ref[...])

        def stage(c, st, s_in, s_out, e_in, e_out):
            acc = values(c - 1, e_in, st[1])
            mask = sel_mask(tile_of(c), FAR_TILES, n_far - FAR_TILES * c)
            k_next = k_rows(ks_ref, tile_of(c + 1), FAR_TILES)
            s_out[...] = _dot(k_next, q_t)
            e = jnp.exp2(s_in[...] + mask)
            e_out[...] = e.astype(BF16)
            return st[0] + jnp.sum(e, axis=0, keepdims=True), acc

        s_a, s_b, e_a, e_b = stage_refs
        e_b[...] = jnp.zeros_like(e_b)
        s_a[...] = scores(0)

        def pair(c, st):
            st = stage(c, st, s_a, s_b, e_b, e_a)
            return stage(c + 1, st, s_b, s_a, e_a, e_b)

        def quad(c, st):
            return pair(c + 2, pair(c, st))

        n_octs = n_chunks // 8
        st = lax.fori_loop(0, n_octs, lambda t, st: quad(8 * t + 4, quad(8 * t, st)), st)
        done = 8 * n_octs
        st = lax.cond(n_chunks - done >= 4, lambda st: quad(done, st), lambda st: st, st)
        done = done + 4 * ((n_chunks - done) // 4)
        st = lax.cond(n_chunks - done >= 2, lambda st: pair(done, st), lambda st: st, st)

        def odd_tail(st):
            st = stage(n_chunks - 1, st, s_a, s_b, e_b, e_a)
            return st[0], values(n_chunks - 1, e_a, st[1])

        return lax.cond(n_chunks % 2 == 1, odd_tail, lambda st: (st[0], values(n_chunks - 1, e_b, st[1])), st)

    def window_fast():
        J0 = jnp.maximum(i - n_win, 0)
        bias = wbias_ref[0, pl.ds(pl.multiple_of((n_win - (i - J0)) * Q_BLOCK, Q_BLOCK), (n_win + 1) * Q_BLOCK), :]
        return add_chunk(zero, _dot(k_rows(kw_ref, J0, n_win + 1), q_t) + bias, v_cols(vw_ref, J0, n_win + 1))

    def sweep_exact(k_ref, v_ref, bias_ref, far_dist, n_back, with_sel):
        def body(t, st):
            J = i - t
            off = pl.multiple_of((far_dist - jnp.minimum(t, far_dist)) * Q_BLOCK, Q_BLOCK)
            s = _dot(k_rows(k_ref, J, 1), q_t) + bias_ref[0, pl.ds(off, Q_BLOCK), :]
            if with_sel:
                s = s + sel_mask(J, 1)
            return _softmax_tile_update(s, v_ref[0, J], *st)

        init = (jnp.full((1, N), NEG_INF, F32),) + zero
        _, l, acc = lax.fori_loop(0, jnp.minimum(i, n_back) + 1, body, init)
        return l, acc

    vals = {}

    def load_phase():
        vals["cmp"] = compressed(True)
        vals["win"] = window_fast()

    def select_phase():
        select_blocks(vals["cmp"][0])

    def near_phase():
        vals["near"] = selected_near()

    def exact_path():
        p_x, _, o_x = compressed(False)
        select_blocks(p_x)
        l_sx, a_sx = sweep_exact(ks_ref, vs_ref, sbias_ref, 2, nb, True)
        l_wx, a_wx = sweep_exact(kw_ref, vw_ref, wbias_ref, n_win, n_win, False)
        return o_x, a_sx * (1.0 / l_sx), a_wx * (1.0 / l_wx)

    def gate_row(br):
        return jnp.concatenate([gate_ref[br, 0, r:r + 1, u * Q_BLOCK:(u + 1) * Q_BLOCK] for r in range(R)], axis=1)

    def finish():
        _, l_c, o_c = vals["cmp"]
        l_w, a_w = vals["win"]
        l_s, a_s = selected_far(vals["near"])
        lane_q = lax.broadcasted_iota(jnp.int32, (1, N), 1) % Q_BLOCK
        has_cmp = (i > 0) | (lane_q >= CMP_LEN - 1)
        l_min = jnp.minimum(jnp.minimum(l_s, l_w), jnp.where(has_cmp, l_c, 1.0))
        o_cx, o_s, o_w = lax.cond(jnp.min(l_min) > NSA_TINY,
                                  lambda: (o_c, a_s * (1.0 / l_s), a_w * (1.0 / l_w)), exact_path)
        o_t = gate_row(0) * o_cx + gate_row(1) * o_s + gate_row(2) * o_w
        return _pairs_to_rows(o_t)

    return load_phase, select_phase, near_phase, finish


def _nsa_attention(colmat, rowmat, cmp_rows, cmp_cols, gates, cbias, sbias, wbias, ov_t, msel, B, S):
    nb = S // Q_BLOCK
    nc = S // CMP_STRIDE
    n_slc = S // SLC_LEN
    G, R = NSA_KV_HEADS, NSA_REP
    kern = functools.partial(_nsa_kernel, nb=nb)
    ns = nb // NSA_SUB
    return pl.pallas_call(
        kern,
        grid=(B, G, ns),
        in_specs=[
            pl.BlockSpec((R, NSA_SUB, HEAD_DIM, LANES), lambda b, g, i: (g, b * ns + i, 0, 0)),
            pl.BlockSpec((1, 1, nc, HEAD_DIM), lambda b, g, i: (g, b, 0, 0)),
            pl.BlockSpec((1, 1, HEAD_DIM, nc), lambda b, g, i: (NSA_KV_HEADS + g, b, 0, 0)),
            pl.BlockSpec((1, S, HEAD_DIM), lambda b, g, i: (ROW_KS + g, b, 0)),
            pl.BlockSpec((1, nb, HEAD_DIM, LANES), lambda b, g, i: (COL_VS + g, b, 0, 0)),
            pl.BlockSpec((1, S, HEAD_DIM), lambda b, g, i: (ROW_KW + g, b, 0)),
            pl.BlockSpec((1, nb, HEAD_DIM, LANES), lambda b, g, i: (COL_VW + g, b, 0, 0)),
            pl.BlockSpec((N_BRANCHES, 1, R, NSA_SUB * LANES), lambda b, g, i: (0, g, 0, b * ns + i)),
            pl.BlockSpec((1, cbias.shape[1], R * Q_BLOCK), lambda b, g, i: (g, 0, 0)),
            pl.BlockSpec((1, sbias.shape[1], R * Q_BLOCK), lambda b, g, i: (g, 0, 0)),
            pl.BlockSpec((1, wbias.shape[1], R * Q_BLOCK), lambda b, g, i: (g, 0, 0)),
            pl.BlockSpec((n_slc, nc), lambda b, g, i: (0, 0)),
            pl.BlockSpec((1, 1, Q_BLOCK), lambda b, g, i: (g, 0, 0)),
        ],
        out_specs=pl.BlockSpec((1, NSA_SUB * Q_BLOCK, R * HEAD_DIM), lambda b, g, i: (b, i, g)),
        out_shape=jax.ShapeDtypeStruct((B, S, NSA_HEADS * HEAD_DIM), BF16),
        scratch_shapes=[pltpu.VMEM((n_slc + NEG_ROWS, Q_BLOCK), F32)] * NSA_SUB
        + [pltpu.VMEM((FAR_TILES * Q_BLOCK, R * Q_BLOCK), F32)] * 2
        + [pltpu.VMEM((FAR_TILES * Q_BLOCK, R * Q_BLOCK), BF16)] * 2,
        compiler_params=_cparams(("parallel", "parallel", "arbitrary")),
        name="nsa_attention",
    )(colmat, cmp_rows, cmp_cols, rowmat, colmat, rowmat, colmat, gates, cbias, sbias, wbias, ov_t, msel)


SB_QB = 512
SB_KT = 256
SB_ZERO_LOG = -160.0


def _sb_kernel(q_ref, k_ref, v_ref, tri_ref, o_ref):
    i = pl.program_id(1)
    H = SB_HEADS
    tri = tri_ref[...]
    assert SB_QB == 2 * SB_KT
    below_diag = (lax.broadcasted_iota(jnp.int32, (SB_KT, SB_KT), 1)
                  < lax.broadcasted_iota(jnp.int32, (SB_KT, SB_KT), 0))

    def step(J, rows, st, masked):
        out = []
        for h in range(H):
            carry, acc = st[h]
            k_t = jnp.concatenate([k_ref[h, (SB_KT // LANES) * J + c] for c in range(SB_KT // LANES)], axis=1)
            z = _dot(q_ref[h, rows, :], k_t)
            sp = jnp.maximum(z, 0.0) + jnp.log2(1.0 + jnp.exp2(-jnp.abs(z)))
            lf = jnp.where(below_diag, -sp, 0.0) if masked else -sp
            lf_hi = lf.astype(BF16)
            lf_lo = (lf - lf_hi.astype(F32)).astype(BF16)
            after = _dot(lf_hi, tri) + _dot(lf_lo, tri) + carry
            w = jnp.exp2(z - sp + after)
            if masked:
                w = jnp.where(below_diag, w, 0.0)
            v = v_ref[h, pl.ds(pl.multiple_of(J * SB_KT, SB_KT), SB_KT), :]
            acc = acc + _dot(w.astype(BF16), v)
            carry = carry + jnp.sum(lf, axis=1, keepdims=True)
            out.append((carry, acc))
        return tuple(out)

    lo, hi = slice(0, SB_KT), slice(SB_KT, SB_QB)
    zero = tuple((jnp.zeros((SB_KT, 1), F32), jnp.zeros((SB_KT, HEAD_DIM), F32)) for _ in range(H))
    st_hi = step(2 * i, hi, step(2 * i + 1, hi, zero, True), False)
    st_lo = step(2 * i, lo, zero, True)
    st = tuple(tuple(jnp.concatenate([a, b], axis=0) for a, b in zip(st_lo[h], st_hi[h])) for h in range(H))
    per_q = SB_QB // SB_KT
    all_rows = slice(0, SB_QB)

    def alive(st):
        worst = st[0][0]
        for h in range(1, H):
            worst = jnp.maximum(worst, st[h][0])
        return jnp.max(worst) > SB_ZERO_LOG

    def body(c):
        J, _, st = c
        st = step(J, all_rows, st, False)
        return J - 1, alive(st), st

    _, _, st = lax.while_loop(lambda c: (c[0] >= 0) & c[1], body, (per_q * i - 1, alive(st), st))
    o_ref[0] = jnp.concatenate([st[h][1] for h in range(H)], axis=1).astype(o_ref.dtype)


def _sb_attention(colmat, rowmat, tri, B, S):
    nq = S // SB_QB
    nb = S // LANES
    H = SB_HEADS
    return pl.pallas_call(
        _sb_kernel,
        grid=(B, nq),
        in_specs=[
            pl.BlockSpec((H, SB_QB, HEAD_DIM), lambda b, i: (ROW_SQ // H, b * nq + i, 0)),
            pl.BlockSpec((H, nb, HEAD_DIM, LANES), lambda b, i: (COL_SK // H, b, 0, 0)),
            pl.BlockSpec((H, S, HEAD_DIM), lambda b, i: (ROW_SV // H, b, 0)),
            pl.BlockSpec((SB_KT, SB_KT), lambda b, i: (0, 0)),
        ],
        out_specs=pl.BlockSpec((1, SB_QB, H * HEAD_DIM), lambda b, i: (b, i, 0)),
        out_shape=jax.ShapeDtypeStruct((B, S, H * HEAD_DIM), BF16),
        compiler_params=_cparams(("parallel", "arbitrary")),
        name="sb_attention",
    )(rowmat, colmat, rowmat, tri)


SWA_SUB = 4


def _swa_kernel(q_ref, k_ref, v_ref, bias_ref, sink_ref, o_ref):
    R = SWA_REP
    sink = sink_ref[0]
    for u in range(SWA_SUB):
        i = pl.program_id(2) * SWA_SUB + u
        q_t = jnp.concatenate([q_ref[r, u] for r in range(R)], axis=1)
        J0 = jnp.maximum(i - 1, 0)
        k = k_ref[0, pl.ds(pl.multiple_of(J0 * Q_BLOCK, Q_BLOCK), 2 * Q_BLOCK), :]
        bias = bias_ref[0, pl.ds(pl.multiple_of((1 - (i - J0)) * Q_BLOCK, Q_BLOCK), 2 * Q_BLOCK), :]
        s = _dot(k, q_t) + bias
        m = jnp.maximum(jnp.max(s, axis=0, keepdims=True), sink)
        e = jnp.exp(s - m)
        den = jnp.sum(e, axis=0, keepdims=True) + jnp.exp(sink - m)
        v_t = jnp.concatenate([v_ref[0, J0], v_ref[0, J0 + 1]], axis=1)
        o_t = _dot(v_t, e.astype(BF16)) * (1.0 / den)
        o_ref[0, u * Q_BLOCK:(u + 1) * Q_BLOCK, :] = _pairs_to_rows(o_t).astype(o_ref.dtype)


def _swa_attention(colmat, rowmat, bias, sinks, B, S):
    nb = S // Q_BLOCK
    ns = nb // SWA_SUB
    G, R = SWA_KV_HEADS, SWA_REP
    return pl.pallas_call(
        _swa_kernel,
        grid=(B, G, ns),
        in_specs=[
            pl.BlockSpec((R, SWA_SUB, HEAD_DIM, LANES), lambda b, g, i: (COL_WQ // R + g, b * ns + i, 0, 0)),
            pl.BlockSpec((1, S, HEAD_DIM), lambda b, g, i: (ROW_WK + g, b, 0)),
            pl.BlockSpec((1, nb, HEAD_DIM, LANES), lambda b, g, i: (COL_WV + g, b, 0, 0)),
            pl.BlockSpec((1, 3 * Q_BLOCK, R * Q_BLOCK), lambda b, g, i: (g, 0, 0)),
            pl.BlockSpec((1, 1, R * Q_BLOCK), lambda b, g, i: (g, 0, 0)),
        ],
        out_specs=pl.BlockSpec((1, SWA_SUB * Q_BLOCK, R * HEAD_DIM), lambda b, g, i: (b, i, g)),
        out_shape=jax.ShapeDtypeStruct((B, S, SWA_HEADS * HEAD_DIM), BF16),
        compiler_params=_cparams(("parallel", "parallel", "arbitrary")),
        name="swa_attention",
    )(colmat, rowmat, colmat, bias, sinks)


MERGE_TM = 512


def _merge_kernel(x_ref, nw_ref, wg_ref, ya_ref, yb_ref, yc_ref, ua_ref, ub_ref, uc_ref, wo_ref, o_ref):
    x = x_ref[...]
    h = _rms_rows(x, nw_ref[...]).astype(BF16)
    merged = None
    for br, (y_ref, u_ref) in enumerate(((ya_ref, ua_ref), (yb_ref, ub_ref), (yc_ref, uc_ref))):
        g = jax.nn.sigmoid(_dot(h, wg_ref[:, br * D_MODEL:(br + 1) * D_MODEL]))
        t = g * _dot(y_ref[...], u_ref[...])
        merged = t if merged is None else merged + t
    o_ref[...] = x + _dot(merged.astype(BF16), wo_ref[...])


def _merge(x2, nw, w_bgate, y_nsa, y_sb, y_swa, u_nsa, u_sb, u_swa, w_out):
    T = x2.shape[0]
    full = lambda shape: pl.BlockSpec(shape, lambda i: (0, 0))
    rows = lambda n: pl.BlockSpec((MERGE_TM, n), lambda i: (i, 0))
    return pl.pallas_call(
        _merge_kernel,
        grid=(T // MERGE_TM,),
        in_specs=[rows(D_MODEL), full((1, D_MODEL)), full((D_MODEL, N_BRANCHES * D_MODEL)),
                  rows(y_nsa.shape[1]), rows(y_sb.shape[1]), rows(y_swa.shape[1]),
                  full(u_nsa.shape), full(u_sb.shape), full(u_swa.shape), full((D_MODEL, D_MODEL))],
        out_specs=rows(D_MODEL),
        out_shape=jax.ShapeDtypeStruct((T, D_MODEL), F32),
        compiler_params=_cparams(("parallel",)),
        name="merge",
    )(x2, nw.reshape(1, D_MODEL), w_bgate.astype(BF16), y_nsa, y_sb, y_swa,
      u_nsa.astype(BF16), u_sb.astype(BF16), u_swa.astype(BF16), w_out.astype(BF16))


def _t5_bucket(dist):
    max_exact = NUM_BUCKETS // 2
    d = jnp.maximum(dist, 0)
    df = jnp.maximum(d, 1).astype(F32)
    large = max_exact + (jnp.log(df / max_exact) / math.log(MAX_DISTANCE / max_exact)
                         * (NUM_BUCKETS - max_exact)).astype(jnp.int32)
    large = jnp.minimum(large, NUM_BUCKETS - 1)
    return jnp.where(d < max_exact, d, large)


def _toeplitz(ext, delta):
    H = ext.shape[0]
    n = Q_BLOCK
    c = delta + MAX_DISTANCE
    w = ext[:, c - (n - 1):c + n]
    flat = jnp.pad(jnp.broadcast_to(w[:, None, :], (H, n, 2 * n - 1)), ((0, 0), (0, 0), (0, 1))).reshape(H, 2 * n * n)
    return flat[:, n - 1:n - 1 + n * (2 * n - 1)].reshape(H, n, 2 * n - 1)[:, :, :n]


def _lanes(t, G, R):
    K, Q = t.shape[1:]
    return t.reshape(G, R, K, Q).transpose(0, 2, 1, 3).reshape(G, K, R * Q)


def _bias_tables(rel_bias, nb, qk_bound):
    n = Q_BLOCK
    G, R = NSA_KV_HEADS, NSA_REP
    by_dist = rel_bias[_t5_bucket(jnp.arange(MAX_DISTANCE)), :].T.astype(F32)
    far = rel_bias[NUM_BUCKETS - 1, :].astype(F32)
    neg = jnp.full((1, n), NEG_INF, F32)
    k = jnp.arange(n)[:, None]
    q = jnp.arange(n)[None, :]
    above = (k > q)[None]

    rel = (by_dist[:NSA_HEADS] - far[:NSA_HEADS, None]) * LOG2E
    m_g = qk_bound + jnp.maximum(jnp.max(rel.reshape(G, -1), axis=1), 0.0)
    m_h = jnp.repeat(m_g, R)[:, None, None]
    ext = jnp.concatenate([jnp.broadcast_to(neg, (NSA_HEADS, n)), rel, jnp.zeros((NSA_HEADS, n), F32)], axis=1)
    t0, t1 = _toeplitz(ext, 0), _toeplitz(ext, n)
    zeros = jnp.zeros_like(t0)
    masked = jnp.full_like(t0, NEG_INF)
    sbias = _lanes(jnp.concatenate([zeros, t1, t0, masked], axis=1), G, R)
    n_win = NSA_WINDOW // n
    edge = jnp.where(above, zeros, NEG_INF)
    tiles = [edge] + [zeros] * (n_win - 2) + [t1, t0]
    wbias = _lanes(jnp.concatenate([t - m_h for t in tiles] + [masked] * n_win, axis=1), G, R)
    off = 8 * (nb - 1)
    near = jnp.stack([ext[:, 97 - CMP_STRIDE * m:97 - CMP_STRIDE * m + n] for m in range(-9, 7)], axis=1)
    n_rows = 16 * nb - 8
    cb = jnp.concatenate([jnp.zeros((NSA_HEADS, off - 9, n), F32), near,
                          jnp.full((NSA_HEADS, n_rows - off - 7, n), NEG_INF, F32)], axis=1) - m_h
    cbias = _lanes(cb, G, R)
    msel = jnp.broadcast_to(-m_g[:, None, None], (G, 1, n))

    hs = slice(NSA_HEADS, NSA_HEADS + SWA_HEADS)
    ext_w = jnp.concatenate([jnp.broadcast_to(neg, (SWA_HEADS, n)), by_dist[hs],
                             jnp.broadcast_to(far[hs, None], (SWA_HEADS, n))], axis=1)
    s0 = _toeplitz(ext_w, 0)
    s1 = jnp.where(above, _toeplitz(ext_w, n), NEG_INF)
    swa_bias = _lanes(jnp.concatenate([s1, s0, jnp.full_like(s0, NEG_INF)], axis=1), SWA_KV_HEADS, SWA_REP)
    return cbias, sbias, wbias, msel, swa_bias


def _overlap_t(S):
    nc, n_slc = S // CMP_STRIDE, S // SLC_LEN
    n = np.arange(nc)[None, :]
    j = np.arange(n_slc)[:, None]
    ov = (n * CMP_STRIDE < (j + 1) * SLC_LEN) & (n * CMP_STRIDE + CMP_LEN - 1 >= j * SLC_LEN) & (n < nc - 1)
    return jnp.asarray(ov, BF16)


def _head_gains(heads, norm_of, scale_of):
    ones = jnp.ones((HEAD_DIM,), F32)
    gains = [norm_of[name].astype(F32) * scale_of.get(name, 1.0) if name in norm_of else ones for name, _ in heads]
    return jnp.concatenate(gains).reshape(1, -1)


def kernel(x, rel_bias, ffn1_norm, ffn1_w_gate, ffn1_w_up, ffn1_w_down, mix_norm, w_in, nsa_q_norm, nsa_k_norm, nsa_cmp_pos, nsa_cmp_k_w1, nsa_cmp_k_w2, nsa_cmp_v_w1, nsa_cmp_v_w2, swa_q_norm, swa_k_norm, swa_sinks, w_up_nsa, w_up_sb, w_up_swa, w_out, ffn2_norm, ffn2_w_gate, ffn2_w_up, ffn2_w_down):
    B, S, D = x.shape
    T = B * S
    nb = S // Q_BLOCK
    nc = S // CMP_STRIDE
    depth = w_in.shape[0]
    scale = HEAD_DIM ** -0.5
    scale_of = {"nq": scale * LOG2E, "sq": scale * LOG2E, "wq": scale}
    qk_bound = (BOUND_SLACK * LOG2E * HEAD_DIM ** 0.5
                * jnp.max(jnp.abs(nsa_q_norm.astype(F32))) * jnp.max(jnp.abs(nsa_k_norm.astype(F32))))
    cbias, sbias, wbias, msel, swa_bias = _bias_tables(rel_bias, nb, qk_bound)
    plan = _proj_plan(scale_of)
    ov_t = _overlap_t(S)
    tri = jnp.asarray(np.tril(np.ones((SB_KT, SB_KT), np.float32), -1), BF16)

    (ffn1_w_gate, ffn1_w_up, ffn1_w_down, ffn2_w_gate, ffn2_w_up, ffn2_w_down, w_up_nsa, w_up_sb, w_up_swa,
     w_out) = (w.astype(BF16) for w in (ffn1_w_gate, ffn1_w_up, ffn1_w_down, ffn2_w_gate, ffn2_w_up, ffn2_w_down,
                                        w_up_nsa, w_up_sb, w_up_swa, w_out))
    tail0 = OFF_NGATE + N_GATE
    w_head, w_tail = w_in[:, :, :OFF_NGATE].astype(BF16), w_in[:, :, tail0:].astype(BF16)
    w_bgate = w_tail[:, :, OFF_BGATE - tail0:]
    w_gate = jnp.pad(w_in[:, :, OFF_NGATE:tail0], ((0, 0), (0, 0), (0, LANES - N_GATE))).astype(BF16)
    runs = []
    for _, off in ROW_HEADS + COL_HEADS:
        if runs and runs[-1][1] == off:
            runs[-1][1] = off + HEAD_DIM
        else:
            runs.append([off, off + HEAD_DIM])
    pieces = [w_head[:, :, a:b] if a < OFF_NGATE else w_tail[:, :, a - tail0:b - tail0] for a, b in runs]
    w_proj = jnp.concatenate(pieces + [w_gate], axis=2)

    x2 = x.reshape(T, D)
    for l in range(depth):
        x2 = _ffn(x2, ffn1_norm[l], ffn1_w_gate[l], ffn1_w_up[l], ffn1_w_down[l])

        norm_of = {"nq": nsa_q_norm[l], "nks": nsa_k_norm[l], "nkw": nsa_k_norm[l],
                   "wq": swa_q_norm[l], "wk": swa_k_norm[l]}
        gain = jnp.concatenate([_head_gains(ROW_HEADS, norm_of, scale_of), _head_gains(COL_HEADS, norm_of, scale_of),
                                jnp.ones((1, LANES), F32)], axis=1)
        rowmat, colmat, gates = _proj(x2, mix_norm[l], w_proj[l], gain, plan)
        gates = gates[:N_GATE].reshape(N_BRANCHES, NSA_KV_HEADS, NSA_REP, T)

        rows4 = rowmat[ROW_KC:ROW_KC + 2 * NSA_KV_HEADS].reshape(2 * NSA_KV_HEADS, B, nc, CMP_STRIDE * HEAD_DIM)
        cmp_rows, cmp_cols = _compress(rows4, jnp.stack([nsa_cmp_k_w1[l], nsa_cmp_v_w1[l]]),
                                       jnp.stack([nsa_cmp_k_w2[l], nsa_cmp_v_w2[l]]),
                                       nsa_cmp_pos[l], nsa_k_norm[l])

        y_nsa = _nsa_attention(colmat, rowmat, cmp_rows, cmp_cols, gates, cbias, sbias, wbias, ov_t, msel, B, S)
        y_sb = _sb_attention(colmat, rowmat, tri, B, S)
        sinks = jnp.repeat(swa_sinks[l].astype(F32).reshape(SWA_KV_HEADS, 1, SWA_REP), Q_BLOCK, axis=2)
        y_swa = _swa_attention(colmat, rowmat, swa_bias, sinks, B, S)

        x2 = _merge(x2, mix_norm[l], w_bgate[l], y_nsa.reshape(T, -1), y_sb.reshape(T, -1),
                    y_swa.reshape(T, -1), w_up_nsa[l], w_up_sb[l], w_up_swa[l], w_out[l])
        x2 = _ffn(x2, ffn2_norm[l], ffn2_w_gate[l], ffn2_w_up[l], ffn2_w_down[l])
    return x2.reshape(B, S, D)
```

```python
import functools
import math

import numpy as np
import jax
import jax.numpy as jnp
from jax import lax
from jax.experimental import pallas as pl
from jax.experimental.pallas import tpu as pltpu

D_MODEL = 1024
HEAD_DIM = 64
Q_BLOCK = 128
NSA_HEADS = 8
NSA_KV_HEADS = 2
NSA_REP = NSA_HEADS // NSA_KV_HEADS
CMP_LEN = 32
CMP_STRIDE = 16
CMP_HIDDEN = 256
SLC_LEN = 64
SLC_TOPN = 16
NSA_WINDOW = 512
SB_HEADS = 4
SWA_HEADS = 4
SWA_KV_HEADS = 2
SWA_REP = SWA_HEADS // SWA_KV_HEADS
SWA_WINDOW = 128
NUM_BUCKETS = 32
MAX_DISTANCE = 128
D_FF = 2816
NORM_EPS = 1e-6
NEG_INF = -1e30
N_BRANCHES = 3
N_GATE = 3 * NSA_HEADS

LANES = 128
VMEM_LIMIT = 56 * 1024 * 1024
BF16 = jnp.bfloat16
F32 = jnp.float32

_sizes = (NSA_HEADS * HEAD_DIM,) + (NSA_KV_HEADS * HEAD_DIM,) * 6 + (N_GATE,) \
    + (SB_HEADS * HEAD_DIM,) * 3 + (SWA_HEADS * HEAD_DIM, SWA_KV_HEADS * HEAD_DIM,
                                    SWA_KV_HEADS * HEAD_DIM, N_BRANCHES * D_MODEL)
_offs = np.concatenate([[0], np.cumsum(_sizes)])
(OFF_NQ, OFF_NKC, OFF_NVC, OFF_NKS, OFF_NVS, OFF_NKW, OFF_NVW, OFF_NGATE,
 OFF_SQ, OFF_SK, OFF_SV, OFF_WQ, OFF_WK, OFF_WV, OFF_BGATE) = (int(o) for o in _offs[:-1])

ROW_HEADS = ([("nkc", OFF_NKC + 64 * g) for g in range(2)] + [("nvc", OFF_NVC + 64 * g) for g in range(2)]
             + [("nks", OFF_NKS + 64 * g) for g in range(2)] + [("nkw", OFF_NKW + 64 * g) for g in range(2)]
             + [("sq", OFF_SQ + 64 * h) for h in range(4)] + [("sv", OFF_SV + 64 * h) for h in range(4)]
             + [("wk", OFF_WK + 64 * g) for g in range(2)])
ROW_KC, ROW_VC, ROW_KS, ROW_KW, ROW_SQ, ROW_SV, ROW_WK = 0, 2, 4, 6, 8, 12, 16
COL_HEADS = ([("nq", OFF_NQ + 64 * h) for h in range(8)] + [("nvs", OFF_NVS + 64 * g) for g in range(2)]
             + [("nvw", OFF_NVW + 64 * g) for g in range(2)] + [("sk", OFF_SK + 64 * h) for h in range(4)]
             + [("wq", OFF_WQ + 64 * h) for h in range(4)] + [("wv", OFF_WV + 64 * g) for g in range(2)])
COL_NQ, COL_VS, COL_VW, COL_SK, COL_WQ, COL_WV = 0, 8, 10, 12, 16, 20


def _dot(a, b):
    return jnp.dot(a, b, preferred_element_type=F32)


def _rms_rows(x, w):
    ms = jnp.mean(x * x, axis=-1, keepdims=True)
    return x * lax.rsqrt(ms + NORM_EPS) * w


def _cparams(sem, flags=None):
    return pltpu.CompilerParams(dimension_semantics=sem, vmem_limit_bytes=VMEM_LIMIT, flags=flags)


FFN_TM = 512
MXU_TILE = 256
FFN_FC = MXU_TILE


def _ffn_kernel(x_ref, nw_ref, wg_ref, wu_ref, wd_ref, o_ref):
    x = x_ref[...]
    h = _rms_rows(x, nw_ref[...]).astype(BF16)
    acc = jnp.zeros(x.shape, F32)
    for c in range(D_FF // FFN_FC):
        cols = slice(c * FFN_FC, (c + 1) * FFN_FC)
        g = _dot(h, wg_ref[:, cols])
        u = _dot(h, wu_ref[:, cols])
        a = (g * jax.nn.sigmoid(g)) * u
        acc = acc + _dot(a.astype(BF16), wd_ref[cols, :])
    o_ref[...] = x + 0.5 * acc


def _layer_spec(l, shape, **kw):
    return pl.BlockSpec((None,) + tuple(shape), lambda i: (l, 0, 0), **kw)


def _ffn(x2, nw, wg, wu, wd, l):
    T = x2.shape[0]
    resident = lambda shape: _layer_spec(l, shape, pipeline_mode=pl.Buffered(1))
    return pl.pallas_call(
        _ffn_kernel,
        grid=(T // FFN_TM,),
        in_specs=[
            pl.BlockSpec((FFN_TM, D_MODEL), lambda i: (i, 0)),
            pl.BlockSpec((1, D_MODEL), lambda i: (0, 0)),
            resident((D_MODEL, D_FF)), resident((D_MODEL, D_FF)), resident((D_FF, D_MODEL)),
        ],
        out_specs=pl.BlockSpec((FFN_TM, D_MODEL), lambda i: (i, 0)),
        out_shape=jax.ShapeDtypeStruct((T, D_MODEL), F32),
        compiler_params=_cparams(("parallel",)),
        name="ffn",
    )(x2, nw.reshape(1, D_MODEL), wg, wu, wd)


PROJ_TM = 512


GATE_ROWS = 32

_NORMED = ("nq", "nks", "nkw", "wq", "wk")


def _proj_plan(scale_of):
    plan = []
    for kind, heads in (("row", ROW_HEADS), ("col", COL_HEADS)):
        for p in range(len(heads) // 2):
            name = heads[2 * p][0]
            assert heads[2 * p + 1][0] == name
            plan.append((kind, p, name in _NORMED, 1.0 if name in _NORMED else scale_of.get(name, 1.0)))
    plan.append(("gate", 0, False, 1.0))
    return tuple(plan)


def _proj_kernel(x_ref, nw_ref, w_ref, gain_ref, bd_ref, row_ref, col_ref, gate_ref, *, plan):
    h = _rms_rows(x_ref[...], nw_ref[...]).astype(BF16)
    pair = 2 * MXU_TILE
    for c0 in range(0, len(plan) * LANES, pair):
        width = min(pair, len(plan) * LANES - c0)
        zg = _dot(h, w_ref[:, c0:c0 + width])
        for u in range(width // LANES):
            t = c0 // LANES + u
            kind, p, normed, scale = plan[t]
            z = zg[:, u * LANES:(u + 1) * LANES]
            if normed:
                ms = _dot((z * z).astype(BF16), bd_ref[...])
                z = z * lax.rsqrt(ms + NORM_EPS) * gain_ref[:, t * LANES:(t + 1) * LANES]
            elif scale != 1.0:
                z = z * scale
            if kind == "row":
                row_ref[2 * p] = z[:, :HEAD_DIM].astype(row_ref.dtype)
                row_ref[2 * p + 1] = z[:, HEAD_DIM:].astype(row_ref.dtype)
            elif kind == "col":
                z_t = z.T
                for c in range(PROJ_TM // LANES):
                    col_ref[2 * p, c] = z_t[:HEAD_DIM, c * LANES:(c + 1) * LANES].astype(col_ref.dtype)
                    col_ref[2 * p + 1, c] = z_t[HEAD_DIM:, c * LANES:(c + 1) * LANES].astype(col_ref.dtype)
            else:
                gate_ref[...] = jax.nn.sigmoid(z).T[:GATE_ROWS]


def _proj(x2, nw, w, gain, plan, l):
    T = x2.shape[0]
    n_row, n_col = len(ROW_HEADS), len(COL_HEADS)
    bd = np.kron(np.eye(2, dtype=np.float32), np.full((HEAD_DIM, HEAD_DIM), 1.0 / HEAD_DIM, np.float32))
    return pl.pallas_call(
        functools.partial(_proj_kernel, plan=plan),
        grid=(T // PROJ_TM,),
        in_specs=[
            pl.BlockSpec((PROJ_TM, D_MODEL), lambda i: (i, 0)),
            pl.BlockSpec((1, D_MODEL), lambda i: (0, 0)),
            _layer_spec(l, w.shape[1:]),
            pl.BlockSpec(gain.shape, lambda i: (0, 0)),
            pl.BlockSpec((LANES, LANES), lambda i: (0, 0)),
        ],
        out_specs=[
            pl.BlockSpec((n_row, PROJ_TM, HEAD_DIM), lambda i: (0, i, 0)),
            pl.BlockSpec((n_col, PROJ_TM // LANES, HEAD_DIM, LANES), lambda i: (0, i, 0, 0)),
            pl.BlockSpec((GATE_ROWS, PROJ_TM), lambda i: (0, i)),
        ],
        out_shape=[jax.ShapeDtypeStruct((n_row, T, HEAD_DIM), BF16),
                   jax.ShapeDtypeStruct((n_col, T // LANES, HEAD_DIM, LANES), BF16),
                   jax.ShapeDtypeStruct((GATE_ROWS, T), F32)],
        compiler_params=_cparams(("parallel",)),
        name="proj",
    )(x2, nw.reshape(1, D_MODEL), w, gain, jnp.asarray(bd, BF16))


def _compress_kernel(r_ref, w1_ref, w2_ref, pos_ref, kn_ref, row_ref, col_ref):
    kind = pl.program_id(0) // NSA_KV_HEADS
    r = r_ref[0, 0]
    nc = r.shape[0]
    half = CMP_STRIDE * HEAD_DIM
    w1 = w1_ref[0]
    p_lo = _dot(r, w1[:half])
    p_hi = _dot(r, w1[half:])
    p_pos = _dot(jnp.broadcast_to(pos_ref[...], (8, CMP_LEN * HEAD_DIM)).astype(BF16), w1)[:1]
    hdn = p_lo + pltpu.roll(p_hi, nc - 1, 0) + p_pos
    hdn = hdn * jax.nn.sigmoid(hdn)
    out = _dot(hdn.astype(BF16), w2_ref[0])
    ms = jnp.sum(out * out, axis=-1, keepdims=True) * (1.0 / HEAD_DIM)
    normed = out * lax.rsqrt(ms + NORM_EPS) * kn_ref[...]
    out = jnp.where(kind == 0, normed, out)
    row_ref[0, 0] = out[:, :HEAD_DIM].astype(row_ref.dtype)
    col_ref[0, 0] = out.T[:HEAD_DIM].astype(col_ref.dtype)


def _compress(rows4, w1, w2, pos, k_norm):
    assert ROW_KC == 0 and ROW_VC == NSA_KV_HEADS
    _, B, NC, _ = rows4.shape
    kn = jnp.pad(k_norm.reshape(1, HEAD_DIM), ((0, 0), (0, LANES - HEAD_DIM)))
    w2p = jnp.pad(w2, ((0, 0), (0, 0), (0, LANES - HEAD_DIM))).astype(BF16)
    return pl.pallas_call(
        _compress_kernel,
        grid=(4, B),
        in_specs=[
            pl.BlockSpec((1, 1, NC, CMP_STRIDE * HEAD_DIM), lambda h, b: (h, b, 0, 0)),
            pl.BlockSpec((1, CMP_LEN * HEAD_DIM, CMP_HIDDEN), lambda h, b: (h // NSA_KV_HEADS, 0, 0)),
            pl.BlockSpec((1, CMP_HIDDEN, LANES), lambda h, b: (h // NSA_KV_HEADS, 0, 0)),
            pl.BlockSpec((1, CMP_LEN * HEAD_DIM), lambda h, b: (0, 0)),
            pl.BlockSpec((1, LANES), lambda h, b: (0, 0)),
        ],
        out_specs=[
            pl.BlockSpec((1, 1, NC, HEAD_DIM), lambda h, b: (h, b, 0, 0)),
            pl.BlockSpec((1, 1, HEAD_DIM, NC), lambda h, b: (h, b, 0, 0)),
        ],
        out_shape=[jax.ShapeDtypeStruct((4, B, NC, HEAD_DIM), BF16),
                   jax.ShapeDtypeStruct((4, B, HEAD_DIM, NC), BF16)],
        compiler_params=_cparams(("parallel", "parallel")),
        name="nsa_compress",
    )(rows4, w1.astype(BF16), w2p, pos.reshape(1, CMP_LEN * HEAD_DIM), kn)


def _softmax_tile_update(s, v_t, m, l, acc):
    m_new = jnp.maximum(m, jnp.max(s, axis=0, keepdims=True))
    alpha = jnp.exp2(m - m_new)
    p = jnp.exp2(s - m_new)
    l_new = alpha * l + jnp.sum(p, axis=0, keepdims=True)
    acc_new = alpha * acc + _dot(v_t, p.astype(BF16))
    return m_new, l_new, acc_new


def _pairs_to_rows(o_t):
    n = o_t.shape[1] // LANES
    outs = []
    for p in range(n // 2):
        a = o_t[:, (2 * p) * LANES:(2 * p + 1) * LANES]
        b = o_t[:, (2 * p + 1) * LANES:(2 * p + 2) * LANES]
        outs.append(jnp.concatenate([a, b], axis=0).T)
    return outs[0] if len(outs) == 1 else jnp.concatenate(outs, axis=1)


FAR_TILES = 4
NSA_SUB = 2
TAKEN = -2.0
NEG_ROWS = 8
N_FORCED = 3
LOG2E = math.log2(math.e)
NSA_TINY = 2.0 ** -100
BOUND_SLACK = 1.05


def _nsa_kernel(q_ref, kc_ref, vc_ref, ks_ref, vs_ref, kw_ref, vw_ref, gate_ref,
                cbias_ref, sbias_ref, wbias_ref, ov_ref, msel_ref, o_ref, *scratch, nb):
    neg_refs, stage_refs = scratch[:NSA_SUB], scratch[NSA_SUB:]
    refs = (q_ref, kc_ref, vc_ref, ks_ref, vs_ref, kw_ref, vw_ref, gate_ref, cbias_ref, sbias_ref, wbias_ref,
            ov_ref, msel_ref)
    blocks = [_nsa_block(u, refs, neg_refs[u], stage_refs, nb) for u in range(NSA_SUB)]
    for phase in range(3):
        for blk in blocks:
            blk[phase]()
    rows = [blk[3]() for blk in blocks]
    for u in range(NSA_SUB):
        o_ref[0, u * Q_BLOCK:(u + 1) * Q_BLOCK, :] = rows[u].astype(o_ref.dtype)


def _nsa_block(u, refs, neg_ref, stage_refs, nb):
    (q_ref, kc_ref, vc_ref, ks_ref, vs_ref, kw_ref, vw_ref, gate_ref, cbias_ref, sbias_ref, wbias_ref,
     ov_ref, msel_ref) = refs
    i = pl.program_id(2) * NSA_SUB + u
    R = NSA_REP
    N = R * Q_BLOCK
    nc = kc_ref.shape[2]
    q_t = jnp.concatenate([q_ref[r, u] for r in range(R)], axis=1)
    n_win = NSA_WINDOW // Q_BLOCK

    def k_rows(ref, J0, n_tiles):
        return ref[0, pl.ds(pl.multiple_of(J0 * Q_BLOCK, Q_BLOCK), n_tiles * Q_BLOCK), :]

    def v_cols(ref, J0, n_tiles):
        return jnp.concatenate([ref[0, J0 + u] for u in range(n_tiles)], axis=1)

    def sel_mask(J0, n_tiles, n_valid=None):
        n_slc = neg_ref.shape[0] - NEG_ROWS
        parts = []
        for u in range(2 * n_tiles):
            row = 2 * J0 + u
            if n_valid is not None:
                row = jnp.where(u // 2 < n_valid, row, n_slc)
            parts.append(jnp.broadcast_to(neg_ref[pl.ds(row, 1), :], (SLC_LEN, Q_BLOCK)))
        return jnp.concatenate([jnp.concatenate(parts, axis=0)] * R, axis=1)

    def compressed(fast):
        cb = cbias_ref[0, pl.ds(pl.multiple_of(8 * (nb - 1) - 8 * i, 8), nc), :]
        s = _dot(kc_ref[0, 0], q_t) + cb
        if fast:
            e = jnp.exp2(s)
        else:
            e = jnp.where(s > 0.5 * NEG_INF, jnp.exp2(s - jnp.max(s, axis=0, keepdims=True)), 0.0)
        l = jnp.sum(e, axis=0, keepdims=True)
        p = e * jnp.where(l > 0.0, 1.0 / l, 0.0)
        return p, l, _dot(vc_ref[0, 0], p.astype(BF16))

    def select_blocks(p_c):
        p_sum = p_c[:, 0:Q_BLOCK]
        for r in range(1, R):
            p_sum = p_sum + p_c[:, r * Q_BLOCK:(r + 1) * Q_BLOCK]
        p_hi = p_sum.astype(BF16)
        p_lo = (p_sum - p_hi.astype(F32)).astype(BF16)
        imp = _dot(ov_ref[...], p_hi) + _dot(ov_ref[...], p_lo)
        n_slc = imp.shape[0]
        jj = lax.broadcasted_iota(jnp.int32, imp.shape, 0)
        qq = lax.broadcasted_iota(jnp.int32, imp.shape, 1)
        cur = 2 * i + jnp.where(qq >= SLC_LEN, 1, 0)
        forced = (jj == 0) | (jj == cur) | (jj == cur - 1)
        score = jnp.where(forced, TAKEN, jnp.where(jj <= cur, imp, -1.0))
        grp = 8
        j_rows = [jj[a:a + grp] for a in range(0, n_slc, grp)]
        for _ in range(min(SLC_TOPN, n_slc) - N_FORCED):
            cand = [(score[a:a + grp], j_rows[a // grp]) for a in range(0, n_slc, grp)]
            while len(cand) > 1:
                nxt = []
                for a in range(0, len(cand) - 1, 2):
                    (s0, j0), (s1, j1) = cand[a], cand[a + 1]
                    right = s1 > s0
                    nxt.append((jnp.maximum(s0, s1), jnp.where(right, j1, j0)))
                cand = nxt + cand[len(cand) - len(cand) % 2:]
            s8, j8 = cand[0]
            best = jnp.max(s8, axis=0, keepdims=True)
            first = jnp.min(jnp.where(s8 == best, j8, n_slc), axis=0, keepdims=True)
            score = jnp.where(jj == first, TAKEN, score)
        neg_ref[:n_slc] = jnp.where(score == TAKEN, msel_ref[0], NEG_INF)
        neg_ref[n_slc:] = jnp.full((NEG_ROWS, Q_BLOCK), NEG_INF, F32)

    def add_chunk(st, s, v_t):
        e = jnp.exp2(s)
        return st[0] + jnp.sum(e, axis=0, keepdims=True), st[1] + _dot(v_t, e.astype(BF16))

    zero = (jnp.zeros((1, N), F32), jnp.zeros((HEAD_DIM, N), F32))

    def selected_near():
        J0 = jnp.maximum(i - 1, 0)
        bias = sbias_ref[0, pl.ds(pl.multiple_of((2 - (i - J0)) * Q_BLOCK, Q_BLOCK), 2 * Q_BLOCK), :]
        return add_chunk(zero, _dot(k_rows(ks_ref, J0, 2), q_t) + bias + sel_mask(J0, 2), v_cols(vs_ref, J0, 2))

    def selected_far(st):
        n_far = jnp.maximum(i - 1, 0)
        n_chunks = (n_far + FAR_TILES - 1) // FAR_TILES

        def tile_of(c):
            return jnp.clip(FAR_TILES * c, 0, nb - FAR_TILES)

        def scores(c):
            return _dot(k_rows(ks_ref, tile_of(c), FAR_TILES), q_t)

        def values(c, e_ref, acc):
            return acc + _dot(v_cols(vs_ref, tile_of(c), FAR_TILES), e_ref[...])

        def stage(c, st, s_in, s_out, e_in, e_out):
            acc = values(c - 1, e_in, st[1])
            mask = sel_mask(tile_of(c), FAR_TILES, n_far - FAR_TILES * c)
            k_next = k_rows(ks_ref, tile_of(c + 1), FAR_TILES)
            s_out[...] = _dot(k_next, q_t)
            e = jnp.exp2(s_in[...] + mask)
            e_out[...] = e.astype(BF16)
            return st[0] + jnp.sum(e, axis=0, keepdims=True), acc

        s_a, s_b, e_a, e_b = stage_refs
        e_b[...] = jnp.zeros_like(e_b)
        s_a[...] = scores(0)

        def pair(c, st):
            st = stage(c, st, s_a, s_b, e_b, e_a)
            return stage(c + 1, st, s_b, s_a, e_a, e_b)

        def quad(c, st):
            return pair(c + 2, pair(c, st))

        n_octs = n_chunks // 8
        st = lax.fori_loop(0, n_octs, lambda t, st: quad(8 * t + 4, quad(8 * t, st)), st)
        done = 8 * n_octs
        st = lax.cond(n_chunks - done >= 4, lambda st: quad(done, st), lambda st: st, st)
        done = done + 4 * ((n_chunks - done) // 4)
        st = lax.cond(n_chunks - done >= 2, lambda st: pair(done, st), lambda st: st, st)

        def odd_tail(st):
            st = stage(n_chunks - 1, st, s_a, s_b, e_b, e_a)
            return st[0], values(n_chunks - 1, e_a, st[1])

        return lax.cond(n_chunks % 2 == 1, odd_tail, lambda st: (st[0], values(n_chunks - 1, e_b, st[1])), st)

    def window_fast():
        J0 = jnp.maximum(i - n_win, 0)
        bias = wbias_ref[0, pl.ds(pl.multiple_of((n_win - (i - J0)) * Q_BLOCK, Q_BLOCK), (n_win + 1) * Q_BLOCK), :]
        return add_chunk(zero, _dot(k_rows(kw_ref, J0, n_win + 1), q_t) + bias, v_cols(vw_ref, J0, n_win + 1))

    def sweep_exact(k_ref, v_ref, bias_ref, far_dist, n_back, with_sel):
        def body(t, st):
            J = i - t
            off = pl.multiple_of((far_dist - jnp.minimum(t, far_dist)) * Q_BLOCK, Q_BLOCK)
            s = _dot(k_rows(k_ref, J, 1), q_t) + bias_ref[0, pl.ds(off, Q_BLOCK), :]
            if with_sel:
                s = s + sel_mask(J, 1)
            return _softmax_tile_update(s, v_ref[0, J], *st)

        init = (jnp.full((1, N), NEG_INF, F32),) + zero
        _, l, acc = lax.fori_loop(0, jnp.minimum(i, n_back) + 1, body, init)
        return l, acc

    vals = {}

    def load_phase():
        vals["cmp"] = compressed(True)
        vals["win"] = window_fast()

    def select_phase():
        select_blocks(vals["cmp"][0])

    def near_phase():
        vals["near"] = selected_near()

    def exact_path():
        p_x, _, o_x = compressed(False)
        select_blocks(p_x)
        l_sx, a_sx = sweep_exact(ks_ref, vs_ref, sbias_ref, 2, nb, True)
        l_wx, a_wx = sweep_exact(kw_ref, vw_ref, wbias_ref, n_win, n_win, False)
        return o_x, a_sx * (1.0 / l_sx), a_wx * (1.0 / l_wx)

    def gate_row(br):
        return jnp.concatenate([gate_ref[br, 0, r:r + 1, u * Q_BLOCK:(u + 1) * Q_BLOCK] for r in range(R)], axis=1)

    def finish():
        _, l_c, o_c = vals["cmp"]
        l_w, a_w = vals["win"]
        l_s, a_s = selected_far(vals["near"])
        lane_q = lax.broadcasted_iota(jnp.int32, (1, N), 1) % Q_BLOCK
        has_cmp = (i > 0) | (lane_q >= CMP_LEN - 1)
        l_min = jnp.minimum(jnp.minimum(l_s, l_w), jnp.where(has_cmp, l_c, 1.0))
        o_cx, o_s, o_w = lax.cond(jnp.min(l_min) > NSA_TINY,
                                  lambda: (o_c, a_s * (1.0 / l_s), a_w * (1.0 / l_w)), exact_path)
        o_t = gate_row(0) * o_cx + gate_row(1) * o_s + gate_row(2) * o_w
        return _pairs_to_rows(o_t)

    return load_phase, select_phase, near_phase, finish


def _nsa_attention(colmat, rowmat, cmp_rows, cmp_cols, gates, cbias, sbias, wbias, ov_t, msel, B, S):
    nb = S // Q_BLOCK
    nc = S // CMP_STRIDE
    n_slc = S // SLC_LEN
    G, R = NSA_KV_HEADS, NSA_REP
    kern = functools.partial(_nsa_kernel, nb=nb)
    ns = nb // NSA_SUB
    return pl.pallas_call(
        kern,
        grid=(B, G, ns),
        in_specs=[
            pl.BlockSpec((R, NSA_SUB, HEAD_DIM, LANES), lambda b, g, i: (g, b * ns + i, 0, 0)),
            pl.BlockSpec((1, 1, nc, HEAD_DIM), lambda b, g, i: (g, b, 0, 0)),
            pl.BlockSpec((1, 1, HEAD_DIM, nc), lambda b, g, i: (NSA_KV_HEADS + g, b, 0, 0)),
            pl.BlockSpec((1, S, HEAD_DIM), lambda b, g, i: (ROW_KS + g, b, 0)),
            pl.BlockSpec((1, nb, HEAD_DIM, LANES), lambda b, g, i: (COL_VS + g, b, 0, 0)),
            pl.BlockSpec((1, S, HEAD_DIM), lambda b, g, i: (ROW_KW + g, b, 0)),
            pl.BlockSpec((1, nb, HEAD_DIM, LANES), lambda b, g, i: (COL_VW + g, b, 0, 0)),
            pl.BlockSpec((N_BRANCHES, 1, R, NSA_SUB * LANES), lambda b, g, i: (0, g, 0, b * ns + i)),
            pl.BlockSpec((1, cbias.shape[1], R * Q_BLOCK), lambda b, g, i: (g, 0, 0)),
            pl.BlockSpec((1, sbias.shape[1], R * Q_BLOCK), lambda b, g, i: (g, 0, 0)),
            pl.BlockSpec((1, wbias.shape[1], R * Q_BLOCK), lambda b, g, i: (g, 0, 0)),
            pl.BlockSpec((n_slc, nc), lambda b, g, i: (0, 0)),
            pl.BlockSpec((1, 1, Q_BLOCK), lambda b, g, i: (g, 0, 0)),
        ],
        out_specs=pl.BlockSpec((1, NSA_SUB * Q_BLOCK, R * HEAD_DIM), lambda b, g, i: (b, i, g)),
        out_shape=jax.ShapeDtypeStruct((B, S, NSA_HEADS * HEAD_DIM), BF16),
        scratch_shapes=[pltpu.VMEM((n_slc + NEG_ROWS, Q_BLOCK), F32)] * NSA_SUB
        + [pltpu.VMEM((FAR_TILES * Q_BLOCK, R * Q_BLOCK), F32)] * 2
        + [pltpu.VMEM((FAR_TILES * Q_BLOCK, R * Q_BLOCK), BF16)] * 2,
        compiler_params=_cparams(("parallel", "parallel", "arbitrary")),
        name="nsa_attention",
    )(colmat, cmp_rows, cmp_cols, rowmat, colmat, rowmat, colmat, gates, cbias, sbias, wbias, ov_t, msel)


SB_QB = 512
SB_KT = 256
SB_ZERO_LOG = -160.0


def _sb_kernel(q_ref, k_ref, v_ref, tri_ref, o_ref):
    i = pl.program_id(1)
    H = SB_HEADS
    tri = tri_ref[...]
    assert SB_QB == 2 * SB_KT
    below_diag = (lax.broadcasted_iota(jnp.int32, (SB_KT, SB_KT), 1)
                  < lax.broadcasted_iota(jnp.int32, (SB_KT, SB_KT), 0))

    def step(J, rows, st, masked):
        out = []
        for h in range(H):
            carry, acc = st[h]
            k_t = jnp.concatenate([k_ref[h, (SB_KT // LANES) * J + c] for c in range(SB_KT // LANES)], axis=1)
            z = _dot(q_ref[h, rows, :], k_t)
            sp = jnp.maximum(z, 0.0) + jnp.log2(1.0 + jnp.exp2(-jnp.abs(z)))
            lf = jnp.where(below_diag, -sp, 0.0) if masked else -sp
            lf_hi = lf.astype(BF16)
            lf_lo = (lf - lf_hi.astype(F32)).astype(BF16)
            after = _dot(lf_hi, tri) + _dot(lf_lo, tri) + carry
            w = jnp.exp2(z - sp + after)
            if masked:
                w = jnp.where(below_diag, w, 0.0)
            v = v_ref[h, pl.ds(pl.multiple_of(J * SB_KT, SB_KT), SB_KT), :]
            acc = acc + _dot(w.astype(BF16), v)
            carry = carry + jnp.sum(lf, axis=1, keepdims=True)
            out.append((carry, acc))
        return tuple(out)

    lo, hi = slice(0, SB_KT), slice(SB_KT, SB_QB)
    zero = tuple((jnp.zeros((SB_KT, 1), F32), jnp.zeros((SB_KT, HEAD_DIM), F32)) for _ in range(H))
    st_hi = step(2 * i, hi, step(2 * i + 1, hi, zero, True), False)
    st_lo = step(2 * i, lo, zero, True)
    st = tuple(tuple(jnp.concatenate([a, b], axis=0) for a, b in zip(st_lo[h], st_hi[h])) for h in range(H))
    per_q = SB_QB // SB_KT
    all_rows = slice(0, SB_QB)

    def alive(st):
        worst = st[0][0]
        for h in range(1, H):
            worst = jnp.maximum(worst, st[h][0])
        return jnp.max(worst) > SB_ZERO_LOG

    def body(c):
        J, _, st = c
        st = step(J, all_rows, st, False)
        return J - 1, alive(st), st

    _, _, st = lax.while_loop(lambda c: (c[0] >= 0) & c[1], body, (per_q * i - 1, alive(st), st))
    o_ref[0] = jnp.concatenate([st[h][1] for h in range(H)], axis=1).astype(o_ref.dtype)


def _sb_attention(colmat, rowmat, tri, B, S):
    nq = S // SB_QB
    nb = S // LANES
    H = SB_HEADS
    return pl.pallas_call(
        _sb_kernel,
        grid=(B, nq),
        in_specs=[
            pl.BlockSpec((H, SB_QB, HEAD_DIM), lambda b, i: (ROW_SQ // H, b * nq + i, 0)),
            pl.BlockSpec((H, nb, HEAD_DIM, LANES), lambda b, i: (COL_SK // H, b, 0, 0)),
            pl.BlockSpec((H, S, HEAD_DIM), lambda b, i: (ROW_SV // H, b, 0)),
            pl.BlockSpec((SB_KT, SB_KT), lambda b, i: (0, 0)),
        ],
        out_specs=pl.BlockSpec((1, SB_QB, H * HEAD_DIM), lambda b, i: (b, i, 0)),
        out_shape=jax.ShapeDtypeStruct((B, S, H * HEAD_DIM), BF16),
        compiler_params=_cparams(("parallel", "arbitrary")),
        name="sb_attention",
    )(rowmat, colmat, rowmat, tri)


SWA_SUB = 4


def _swa_kernel(q_ref, k_ref, v_ref, bias_ref, sink_ref, o_ref):
    R = SWA_REP
    sink = sink_ref[0]
    for u in range(SWA_SUB):
        i = pl.program_id(2) * SWA_SUB + u
        q_t = jnp.concatenate([q_ref[r, u] for r in range(R)], axis=1)
        J0 = jnp.maximum(i - 1, 0)
        k = k_ref[0, pl.ds(pl.multiple_of(J0 * Q_BLOCK, Q_BLOCK), 2 * Q_BLOCK), :]
        bias = bias_ref[0, pl.ds(pl.multiple_of((1 - (i - J0)) * Q_BLOCK, Q_BLOCK), 2 * Q_BLOCK), :]
        s = _dot(k, q_t) + bias
        m = jnp.maximum(jnp.max(s, axis=0, keepdims=True), sink)
        e = jnp.exp(s - m)
        den = jnp.sum(e, axis=0, keepdims=True) + jnp.exp(sink - m)
        v_t = jnp.concatenate([v_ref[0, J0], v_ref[0, J0 + 1]], axis=1)
        o_t = _dot(v_t, e.astype(BF16)) * (1.0 / den)
        o_ref[0, u * Q_BLOCK:(u + 1) * Q_BLOCK, :] = _pairs_to_rows(o_t).astype(o_ref.dtype)


def _swa_attention(colmat, rowmat, bias, sinks, B, S):
    nb = S // Q_BLOCK
    ns = nb // SWA_SUB
    G, R = SWA_KV_HEADS, SWA_REP
    return pl.pallas_call(
        _swa_kernel,
        grid=(B, G, ns),
        in_specs=[
            pl.BlockSpec((R, SWA_SUB, HEAD_DIM, LANES), lambda b, g, i: (COL_WQ // R + g, b * ns + i, 0, 0)),
            pl.BlockSpec((1, S, HEAD_DIM), lambda b, g, i: (ROW_WK + g, b, 0)),
            pl.BlockSpec((1, nb, HEAD_DIM, LANES), lambda b, g, i: (COL_WV + g, b, 0, 0)),
            pl.BlockSpec((1, 3 * Q_BLOCK, R * Q_BLOCK), lambda b, g, i: (g, 0, 0)),
            pl.BlockSpec((1, 1, R * Q_BLOCK), lambda b, g, i: (g, 0, 0)),
        ],
        out_specs=pl.BlockSpec((1, SWA_SUB * Q_BLOCK, R * HEAD_DIM), lambda b, g, i: (b, i, g)),
        out_shape=jax.ShapeDtypeStruct((B, S, SWA_HEADS * HEAD_DIM), BF16),
        compiler_params=_cparams(("parallel", "parallel", "arbitrary")),
        name="swa_attention",
    )(colmat, rowmat, colmat, bias, sinks)


MERGE_TM = 512


def _merge_kernel(x_ref, nw_ref, wg_ref, ya_ref, yb_ref, yc_ref, ua_ref, ub_ref, uc_ref, wo_ref, o_ref):
    x = x_ref[...]
    h = _rms_rows(x, nw_ref[...]).astype(BF16)
    merged = None
    for br, (y_ref, u_ref) in enumerate(((ya_ref, ua_ref), (yb_ref, ub_ref), (yc_ref, uc_ref))):
        g = jax.nn.sigmoid(_dot(h, wg_ref[:, br * D_MODEL:(br + 1) * D_MODEL]))
        t = g * _dot(y_ref[...], u_ref[...])
        merged = t if merged is None else merged + t
    o_ref[...] = x + _dot(merged.astype(BF16), wo_ref[...])


def _merge(x2, nw, w_bgate, y_nsa, y_sb, y_swa, u_nsa, u_sb, u_swa, w_out, l):
    T = x2.shape[0]
    full = lambda w: _layer_spec(l, w.shape[1:])
    rows = lambda n: pl.BlockSpec((MERGE_TM, n), lambda i: (i, 0))
    return pl.pallas_call(
        _merge_kernel,
        grid=(T // MERGE_TM,),
        in_specs=[rows(D_MODEL), pl.BlockSpec((1, D_MODEL), lambda i: (0, 0)), full(w_bgate),
                  rows(y_nsa.shape[1]), rows(y_sb.shape[1]), rows(y_swa.shape[1]),
                  full(u_nsa), full(u_sb), full(u_swa), full(w_out)],
        out_specs=rows(D_MODEL),
        out_shape=jax.ShapeDtypeStruct((T, D_MODEL), F32),
        compiler_params=_cparams(("parallel",)),
        name="merge",
    )(x2, nw.reshape(1, D_MODEL), w_bgate, y_nsa, y_sb, y_swa, u_nsa, u_sb, u_swa, w_out)


def _t5_bucket(dist):
    max_exact = NUM_BUCKETS // 2
    d = jnp.maximum(dist, 0)
    df = jnp.maximum(d, 1).astype(F32)
    large = max_exact + (jnp.log(df / max_exact) / math.log(MAX_DISTANCE / max_exact)
                         * (NUM_BUCKETS - max_exact)).astype(jnp.int32)
    large = jnp.minimum(large, NUM_BUCKETS - 1)
    return jnp.where(d < max_exact, d, large)


def _toeplitz(ext, delta):
    H = ext.shape[0]
    n = Q_BLOCK
    c = delta + MAX_DISTANCE
    w = ext[:, c - (n - 1):c + n]
    flat = jnp.pad(jnp.broadcast_to(w[:, None, :], (H, n, 2 * n - 1)), ((0, 0), (0, 0), (0, 1))).reshape(H, 2 * n * n)
    return flat[:, n - 1:n - 1 + n * (2 * n - 1)].reshape(H, n, 2 * n - 1)[:, :, :n]


def _lanes(t, G, R):
    K, Q = t.shape[1:]
    return t.reshape(G, R, K, Q).transpose(0, 2, 1, 3).reshape(G, K, R * Q)


def _bias_tables(rel_bias, nb, qk_bound):
    n = Q_BLOCK
    G, R = NSA_KV_HEADS, NSA_REP
    by_dist = rel_bias[_t5_bucket(jnp.arange(MAX_DISTANCE)), :].T.astype(F32)
    far = rel_bias[NUM_BUCKETS - 1, :].astype(F32)
    neg = jnp.full((1, n), NEG_INF, F32)
    k = jnp.arange(n)[:, None]
    q = jnp.arange(n)[None, :]
    above = (k > q)[None]

    rel = (by_dist[:NSA_HEADS] - far[:NSA_HEADS, None]) * LOG2E
    m_g = qk_bound + jnp.maximum(jnp.max(rel.reshape(G, -1), axis=1), 0.0)
    m_h = jnp.repeat(m_g, R)[:, None, None]
    ext = jnp.concatenate([jnp.broadcast_to(neg, (NSA_HEADS, n)), rel, jnp.zeros((NSA_HEADS, n), F32)], axis=1)
    t0, t1 = _toeplitz(ext, 0), _toeplitz(ext, n)
    zeros = jnp.zeros_like(t0)
    masked = jnp.full_like(t0, NEG_INF)
    sbias = _lanes(jnp.concatenate([zeros, t1, t0, masked], axis=1), G, R)
    n_win = NSA_WINDOW // n
    edge = jnp.where(above, zeros, NEG_INF)
    tiles = [edge] + [zeros] * (n_win - 2) + [t1, t0]
    wbias = _lanes(jnp.concatenate([t - m_h for t in tiles] + [masked] * n_win, axis=1), G, R)
    off = 8 * (nb - 1)
    near = jnp.stack([ext[:, 97 - CMP_STRIDE * m:97 - CMP_STRIDE * m + n] for m in range(-9, 7)], axis=1)
    n_rows = 16 * nb - 8
    cb = jnp.concatenate([jnp.zeros((NSA_HEADS, off - 9, n), F32), near,
                          jnp.full((NSA_HEADS, n_rows - off - 7, n), NEG_INF, F32)], axis=1) - m_h
    cbias = _lanes(cb, G, R)
    msel = jnp.broadcast_to(-m_g[:, None, None], (G, 1, n))

    hs = slice(NSA_HEADS, NSA_HEADS + SWA_HEADS)
    ext_w = jnp.concatenate([jnp.broadcast_to(neg, (SWA_HEADS, n)), by_dist[hs],
                             jnp.broadcast_to(far[hs, None], (SWA_HEADS, n))], axis=1)
    s0 = _toeplitz(ext_w, 0)
    s1 = jnp.where(above, _toeplitz(ext_w, n), NEG_INF)
    swa_bias = _lanes(jnp.concatenate([s1, s0, jnp.full_like(s0, NEG_INF)], axis=1), SWA_KV_HEADS, SWA_REP)
    return cbias, sbias, wbias, msel, swa_bias


def _overlap_t(S):
    nc, n_slc = S // CMP_STRIDE, S // SLC_LEN
    n = np.arange(nc)[None, :]
    j = np.arange(n_slc)[:, None]
    ov = (n * CMP_STRIDE < (j + 1) * SLC_LEN) & (n * CMP_STRIDE + CMP_LEN - 1 >= j * SLC_LEN) & (n < nc - 1)
    return jnp.asarray(ov, BF16)


def _head_gains(heads, norm_of, scale_of):
    ones = jnp.ones((HEAD_DIM,), F32)
    gains = [norm_of[name].astype(F32) * scale_of.get(name, 1.0) if name in norm_of else ones for name, _ in heads]
    return jnp.concatenate(gains).reshape(1, -1)


def kernel(x, rel_bias, ffn1_norm, ffn1_w_gate, ffn1_w_up, ffn1_w_down, mix_norm, w_in, nsa_q_norm, nsa_k_norm, nsa_cmp_pos, nsa_cmp_k_w1, nsa_cmp_k_w2, nsa_cmp_v_w1, nsa_cmp_v_w2, swa_q_norm, swa_k_norm, swa_sinks, w_up_nsa, w_up_sb, w_up_swa, w_out, ffn2_norm, ffn2_w_gate, ffn2_w_up, ffn2_w_down):
    B, S, D = x.shape
    T = B * S
    nb = S // Q_BLOCK
    nc = S // CMP_STRIDE
    depth = w_in.shape[0]
    scale = HEAD_DIM ** -0.5
    scale_of = {"nq": scale * LOG2E, "sq": scale * LOG2E, "wq": scale}
    qk_bound = (BOUND_SLACK * LOG2E * HEAD_DIM ** 0.5
                * jnp.max(jnp.abs(nsa_q_norm.astype(F32))) * jnp.max(jnp.abs(nsa_k_norm.astype(F32))))
    cbias, sbias, wbias, msel, swa_bias = _bias_tables(rel_bias, nb, qk_bound)
    plan = _proj_plan(scale_of)
    ov_t = _overlap_t(S)
    tri = jnp.asarray(np.tril(np.ones((SB_KT, SB_KT), np.float32), -1), BF16)

    (ffn1_w_gate, ffn1_w_up, ffn1_w_down, ffn2_w_gate, ffn2_w_up, ffn2_w_down, w_up_nsa, w_up_sb, w_up_swa,
     w_out) = (w.astype(BF16) for w in (ffn1_w_gate, ffn1_w_up, ffn1_w_down, ffn2_w_gate, ffn2_w_up, ffn2_w_down,
                                        w_up_nsa, w_up_sb, w_up_swa, w_out))
    tail0 = OFF_NGATE + N_GATE
    w_head, w_tail = w_in[:, :, :OFF_NGATE].astype(BF16), w_in[:, :, tail0:].astype(BF16)
    w_bgate = w_tail[:, :, OFF_BGATE - tail0:]
    w_gate = jnp.pad(w_in[:, :, OFF_NGATE:tail0], ((0, 0), (0, 0), (0, LANES - N_GATE))).astype(BF16)
    runs = []
    for _, off in ROW_HEADS + COL_HEADS:
        if runs and runs[-1][1] == off:
            runs[-1][1] = off + HEAD_DIM
        else:
            runs.append([off, off + HEAD_DIM])
    pieces = [w_head[:, :, a:b] if a < OFF_NGATE else w_tail[:, :, a - tail0:b - tail0] for a, b in runs]
    w_proj = jnp.concatenate(pieces + [w_gate], axis=2)

    x2 = x.reshape(T, D)
    for l in range(depth):
        x2 = _ffn(x2, ffn1_norm[l], ffn1_w_gate, ffn1_w_up, ffn1_w_down, l)

        norm_of = {"nq": nsa_q_norm[l], "nks": nsa_k_norm[l], "nkw": nsa_k_norm[l],
                   "wq": swa_q_norm[l], "wk": swa_k_norm[l]}
        gain = jnp.concatenate([_head_gains(ROW_HEADS, norm_of, scale_of), _head_gains(COL_HEADS, norm_of, scale_of),
                                jnp.ones((1, LANES), F32)], axis=1)
        rowmat, colmat, gates = _proj(x2, mix_norm[l], w_proj, gain, plan, l)
        gates = gates[:N_GATE].reshape(N_BRANCHES, NSA_KV_HEADS, NSA_REP, T)

        rows4 = rowmat[ROW_KC:ROW_KC + 2 * NSA_KV_HEADS].reshape(2 * NSA_KV_HEADS, B, nc, CMP_STRIDE * HEAD_DIM)
        cmp_rows, cmp_cols = _compress(rows4, jnp.stack([nsa_cmp_k_w1[l], nsa_cmp_v_w1[l]]),
                                       jnp.stack([nsa_cmp_k_w2[l], nsa_cmp_v_w2[l]]),
                                       nsa_cmp_pos[l], nsa_k_norm[l])

        y_nsa = _nsa_attention(colmat, rowmat, cmp_rows, cmp_cols, gates, cbias, sbias, wbias, ov_t, msel, B, S)
        y_sb = _sb_attention(colmat, rowmat, tri, B, S)
        sinks = jnp.repeat(swa_sinks[l].astype(F32).reshape(SWA_KV_HEADS, 1, SWA_REP), Q_BLOCK, axis=2)
        y_swa = _swa_attention(colmat, rowmat, swa_bias, sinks, B, S)

        x2 = _merge(x2, mix_norm[l], w_bgate, y_nsa.reshape(T, -1), y_sb.reshape(T, -1),
                    y_swa.reshape(T, -1), w_up_nsa, w_up_sb, w_up_swa, w_out, l)
        x2 = _ffn(x2, ffn2_norm[l], ffn2_w_gate, ffn2_w_up, ffn2_w_down, l)
    return x2.reshape(B, S, D)
```

```python
import functools
import math

import numpy as np
import jax
import jax.numpy as jnp
from jax import lax
from jax.experimental import pallas as pl
from jax.experimental.pallas import tpu as pltpu

D_MODEL = 1024
HEAD_DIM = 64
Q_BLOCK = 128
NSA_HEADS = 8
NSA_KV_HEADS = 2
NSA_REP = NSA_HEADS // NSA_KV_HEADS
CMP_LEN = 32
CMP_STRIDE = 16
CMP_HIDDEN = 256
SLC_LEN = 64
SLC_TOPN = 16
NSA_WINDOW = 512
SB_HEADS = 4
SWA_HEADS = 4
SWA_KV_HEADS = 2
SWA_REP = SWA_HEADS // SWA_KV_HEADS
SWA_WINDOW = 128
NUM_BUCKETS = 32
MAX_DISTANCE = 128
D_FF = 2816
NORM_EPS = 1e-6
NEG_INF = -1e30
N_BRANCHES = 3
N_GATE = 3 * NSA_HEADS

LANES = 128
VMEM_LIMIT = 56 * 1024 * 1024
BF16 = jnp.bfloat16
F32 = jnp.float32

_sizes = (NSA_HEADS * HEAD_DIM,) + (NSA_KV_HEADS * HEAD_DIM,) * 6 + (N_GATE,) \
    + (SB_HEADS * HEAD_DIM,) * 3 + (SWA_HEADS * HEAD_DIM, SWA_KV_HEADS * HEAD_DIM,
                                    SWA_KV_HEADS * HEAD_DIM, N_BRANCHES * D_MODEL)
_offs = np.concatenate([[0], np.cumsum(_sizes)])
(OFF_NQ, OFF_NKC, OFF_NVC, OFF_NKS, OFF_NVS, OFF_NKW, OFF_NVW, OFF_NGATE,
 OFF_SQ, OFF_SK, OFF_SV, OFF_WQ, OFF_WK, OFF_WV, OFF_BGATE) = (int(o) for o in _offs[:-1])

ROW_HEADS = ([("nkc", OFF_NKC + 64 * g) for g in range(2)] + [("nvc", OFF_NVC + 64 * g) for g in range(2)]
             + [("nks", OFF_NKS + 64 * g) for g in range(2)] + [("nkw", OFF_NKW + 64 * g) for g in range(2)]
             + [("sq", OFF_SQ + 64 * h) for h in range(4)] + [("sv", OFF_SV + 64 * h) for h in range(4)]
             + [("wk", OFF_WK + 64 * g) for g in range(2)])
ROW_KC, ROW_VC, ROW_KS, ROW_KW, ROW_SQ, ROW_SV, ROW_WK = 0, 2, 4, 6, 8, 12, 16
COL_HEADS = ([("nq", OFF_NQ + 64 * h) for h in range(8)] + [("nvs", OFF_NVS + 64 * g) for g in range(2)]
             + [("nvw", OFF_NVW + 64 * g) for g in range(2)] + [("sk", OFF_SK + 64 * h) for h in range(4)]
             + [("wq", OFF_WQ + 64 * h) for h in range(4)] + [("wv", OFF_WV + 64 * g) for g in range(2)])
COL_NQ, COL_VS, COL_VW, COL_SK, COL_WQ, COL_WV = 0, 8, 10, 12, 16, 20


def _dot(a, b):
    return jnp.dot(a, b, preferred_element_type=F32)


def _rms_rows(x, w):
    ms = jnp.mean(x * x, axis=-1, keepdims=True)
    return x * lax.rsqrt(ms + NORM_EPS) * w


def _cparams(sem, flags=None):
    return pltpu.CompilerParams(dimension_semantics=sem, vmem_limit_bytes=VMEM_LIMIT, flags=flags)


FFN_TM = 512
MXU_TILE = 256
FFN_FC = MXU_TILE


def _ffn_kernel(x_ref, nw_ref, wg_ref, wu_ref, wd_ref, o_ref):
    x = x_ref[...]
    h = _rms_rows(x, nw_ref[...]).astype(BF16)
    acc = jnp.zeros(x.shape, F32)
    for c in range(D_FF // FFN_FC):
        cols = slice(c * FFN_FC, (c + 1) * FFN_FC)
        g = _dot(h, wg_ref[:, cols])
        u = _dot(h, wu_ref[:, cols])
        a = (g * jax.nn.sigmoid(g)) * u
        acc = acc + _dot(a.astype(BF16), wd_ref[cols, :])
    o_ref[...] = x + 0.5 * acc


def _layer_spec(l, shape, **kw):
    return pl.BlockSpec((None,) + tuple(shape), lambda i: (l, 0, 0), **kw)


def _ffn(x2, nw, wg, wu, wd, l):
    T = x2.shape[0]
    resident = lambda shape: _layer_spec(l, shape, pipeline_mode=pl.Buffered(1))
    return pl.pallas_call(
        _ffn_kernel,
        grid=(T // FFN_TM,),
        in_specs=[
            pl.BlockSpec((FFN_TM, D_MODEL), lambda i: (i, 0)),
            pl.BlockSpec((1, D_MODEL), lambda i: (0, 0)),
            resident((D_MODEL, D_FF)), resident((D_MODEL, D_FF)), resident((D_FF, D_MODEL)),
        ],
        out_specs=pl.BlockSpec((FFN_TM, D_MODEL), lambda i: (i, 0)),
        out_shape=jax.ShapeDtypeStruct((T, D_MODEL), F32),
        compiler_params=_cparams(("parallel",)),
        name="ffn",
    )(x2, nw.reshape(1, D_MODEL), wg, wu, wd)


PROJ_TM = 512


GATE_ROWS = 32

_NORMED = ("nq", "nks", "nkw", "wq", "wk")


def _proj_plan(scale_of):
    plan = []
    for kind, heads in (("row", ROW_HEADS), ("col", COL_HEADS)):
        for p in range(len(heads) // 2):
            name = heads[2 * p][0]
            assert heads[2 * p + 1][0] == name
            plan.append((kind, p, name in _NORMED, 1.0 if name in _NORMED else scale_of.get(name, 1.0)))
    plan.append(("gate", 0, False, 1.0))
    return tuple(plan)


def _proj_kernel(x_ref, nw_ref, w_ref, gain_ref, bd_ref, row_ref, col_ref, gate_ref, *, plan):
    h = _rms_rows(x_ref[...], nw_ref[...]).astype(BF16)
    pair = 2 * MXU_TILE
    for c0 in range(0, len(plan) * LANES, pair):
        width = min(pair, len(plan) * LANES - c0)
        zg = _dot(h, w_ref[:, c0:c0 + width])
        for u in range(width // LANES):
            t = c0 // LANES + u
            kind, p, normed, scale = plan[t]
            z = zg[:, u * LANES:(u + 1) * LANES]
            if normed:
                ms = _dot((z * z).astype(BF16), bd_ref[...])
                z = z * lax.rsqrt(ms + NORM_EPS) * gain_ref[:, t * LANES:(t + 1) * LANES]
            elif scale != 1.0:
                z = z * scale
            if kind == "row":
                row_ref[2 * p] = z[:, :HEAD_DIM].astype(row_ref.dtype)
                row_ref[2 * p + 1] = z[:, HEAD_DIM:].astype(row_ref.dtype)
            elif kind == "col":
                z_t = z.T
                for c in range(PROJ_TM // LANES):
                    col_ref[2 * p, c] = z_t[:HEAD_DIM, c * LANES:(c + 1) * LANES].astype(col_ref.dtype)
                    col_ref[2 * p + 1, c] = z_t[HEAD_DIM:, c * LANES:(c + 1) * LANES].astype(col_ref.dtype)
            else:
                gate_ref[...] = jax.nn.sigmoid(z).T[:GATE_ROWS]


def _proj(x2, nw, w, gain, plan, l):
    T = x2.shape[0]
    n_row, n_col = len(ROW_HEADS), len(COL_HEADS)
    bd = np.kron(np.eye(2, dtype=np.float32), np.full((HEAD_DIM, HEAD_DIM), 1.0 / HEAD_DIM, np.float32))
    return pl.pallas_call(
        functools.partial(_proj_kernel, plan=plan),
        grid=(T // PROJ_TM,),
        in_specs=[
            pl.BlockSpec((PROJ_TM, D_MODEL), lambda i: (i, 0)),
            pl.BlockSpec((1, D_MODEL), lambda i: (0, 0)),
            _layer_spec(l, w.shape[1:]),
            pl.BlockSpec(gain.shape, lambda i: (0, 0)),
            pl.BlockSpec((LANES, LANES), lambda i: (0, 0)),
        ],
        out_specs=[
            pl.BlockSpec((n_row, PROJ_TM, HEAD_DIM), lambda i: (0, i, 0)),
            pl.BlockSpec((n_col, PROJ_TM // LANES, HEAD_DIM, LANES), lambda i: (0, i, 0, 0)),
            pl.BlockSpec((GATE_ROWS, PROJ_TM), lambda i: (0, i)),
        ],
        out_shape=[jax.ShapeDtypeStruct((n_row, T, HEAD_DIM), BF16),
                   jax.ShapeDtypeStruct((n_col, T // LANES, HEAD_DIM, LANES), BF16),
                   jax.ShapeDtypeStruct((GATE_ROWS, T), F32)],
        compiler_params=_cparams(("parallel",)),
        name="proj",
    )(x2, nw.reshape(1, D_MODEL), w, gain, jnp.asarray(bd, BF16))


def _compress_kernel(r_ref, w1_ref, w2_ref, pos_ref, kn_ref, row_ref, col_ref):
    kind = pl.program_id(0) // NSA_KV_HEADS
    r = r_ref[0, 0]
    nc = r.shape[0]
    half = CMP_STRIDE * HEAD_DIM
    w1 = w1_ref[0]
    p_lo = _dot(r, w1[:half])
    p_hi = _dot(r, w1[half:])
    p_pos = _dot(jnp.broadcast_to(pos_ref[...], (8, CMP_LEN * HEAD_DIM)).astype(BF16), w1)[:1]
    hdn = p_lo + pltpu.roll(p_hi, nc - 1, 0) + p_pos
    hdn = hdn * jax.nn.sigmoid(hdn)
    out = _dot(hdn.astype(BF16), w2_ref[0])
    ms = jnp.sum(out * out, axis=-1, keepdims=True) * (1.0 / HEAD_DIM)
    normed = out * lax.rsqrt(ms + NORM_EPS) * kn_ref[...]
    out = jnp.where(kind == 0, normed, out)
    row_ref[0, 0] = out[:, :HEAD_DIM].astype(row_ref.dtype)
    col_ref[0, 0] = out.T[:HEAD_DIM].astype(col_ref.dtype)


def _compress(rows4, w1, w2, pos, k_norm):
    assert ROW_KC == 0 and ROW_VC == NSA_KV_HEADS
    _, B, NC, _ = rows4.shape
    kn = jnp.pad(k_norm.reshape(1, HEAD_DIM), ((0, 0), (0, LANES - HEAD_DIM)))
    w2p = jnp.pad(w2, ((0, 0), (0, 0), (0, LANES - HEAD_DIM))).astype(BF16)
    return pl.pallas_call(
        _compress_kernel,
        grid=(4, B),
        in_specs=[
            pl.BlockSpec((1, 1, NC, CMP_STRIDE * HEAD_DIM), lambda h, b: (h, b, 0, 0)),
            pl.BlockSpec((1, CMP_LEN * HEAD_DIM, CMP_HIDDEN), lambda h, b: (h // NSA_KV_HEADS, 0, 0)),
            pl.BlockSpec((1, CMP_HIDDEN, LANES), lambda h, b: (h // NSA_KV_HEADS, 0, 0)),
            pl.BlockSpec((1, CMP_LEN * HEAD_DIM), lambda h, b: (0, 0)),
            pl.BlockSpec((1, LANES), lambda h, b: (0, 0)),
        ],
        out_specs=[
            pl.BlockSpec((1, 1, NC, HEAD_DIM), lambda h, b: (h, b, 0, 0)),
            pl.BlockSpec((1, 1, HEAD_DIM, NC), lambda h, b: (h, b, 0, 0)),
        ],
        out_shape=[jax.ShapeDtypeStruct((4, B, NC, HEAD_DIM), BF16),
                   jax.ShapeDtypeStruct((4, B, HEAD_DIM, NC), BF16)],
        compiler_params=_cparams(("parallel", "parallel")),
        name="nsa_compress",
    )(rows4, w1.astype(BF16), w2p, pos.reshape(1, CMP_LEN * HEAD_DIM), kn)


def _softmax_tile_update(s, v_t, m, l, acc):
    m_new = jnp.maximum(m, jnp.max(s, axis=0, keepdims=True))
    alpha = jnp.exp2(m - m_new)
    p = jnp.exp2(s - m_new)
    l_new = alpha * l + jnp.sum(p, axis=0, keepdims=True)
    acc_new = alpha * acc + _dot(v_t, p.astype(BF16))
    return m_new, l_new, acc_new


def _pairs_to_rows(o_t):
    n = o_t.shape[1] // LANES
    outs = []
    for p in range(n // 2):
        a = o_t[:, (2 * p) * LANES:(2 * p + 1) * LANES]
        b = o_t[:, (2 * p + 1) * LANES:(2 * p + 2) * LANES]
        outs.append(jnp.concatenate([a, b], axis=0).T)
    return outs[0] if len(outs) == 1 else jnp.concatenate(outs, axis=1)


FAR_TILES = 4
NSA_SUB = 2
TAKEN = -2.0
NEG_ROWS = 8
N_FORCED = 3
LOG2E = math.log2(math.e)
NSA_TINY = 2.0 ** -100
BOUND_SLACK = 1.05


def _nsa_kernel(q_ref, kc_ref, vc_ref, ks_ref, vs_ref, kw_ref, vw_ref, gate_ref,
                cbias_ref, sbias_ref, wbias_ref, ov_ref, msel_ref, o_ref, *scratch, nb):
    neg_refs, stage_refs = scratch[:NSA_SUB], scratch[NSA_SUB:]
    refs = (q_ref, kc_ref, vc_ref, ks_ref, vs_ref, kw_ref, vw_ref, gate_ref, cbias_ref, sbias_ref, wbias_ref,
            ov_ref, msel_ref)
    blocks = [_nsa_block(u, refs, neg_refs[u], stage_refs, nb) for u in range(NSA_SUB)]
    for phase in range(3):
        for blk in blocks:
            blk[phase]()
    rows = [blk[3]() for blk in blocks]
    for u in range(NSA_SUB):
        o_ref[0, u * Q_BLOCK:(u + 1) * Q_BLOCK, :] = rows[u].astype(o_ref.dtype)


def _nsa_block(u, refs, neg_ref, stage_refs, nb):
    (q_ref, kc_ref, vc_ref, ks_ref, vs_ref, kw_ref, vw_ref, gate_ref, cbias_ref, sbias_ref, wbias_ref,
     ov_ref, msel_ref) = refs
    i = pl.program_id(2) * NSA_SUB + u
    R = NSA_REP
    N = R * Q_BLOCK
    nc = kc_ref.shape[2]
    q_t = jnp.concatenate([q_ref[r, u] for r in range(R)], axis=1)
    n_win = NSA_WINDOW // Q_BLOCK

    def k_rows(ref, J0, n_tiles):
        return ref[0, pl.ds(pl.multiple_of(J0 * Q_BLOCK, Q_BLOCK), n_tiles * Q_BLOCK), :]

    def v_cols(ref, J0, n_tiles):
        return jnp.concatenate([ref[0, J0 + u] for u in range(n_tiles)], axis=1)

    def sel_mask(J0, n_tiles, n_valid=None):
        n_slc = neg_ref.shape[0] - NEG_ROWS
        parts = []
        for u in range(2 * n_tiles):
            row = 2 * J0 + u
            if n_valid is not None:
                row = jnp.where(u // 2 < n_valid, row, n_slc)
            parts.append(jnp.broadcast_to(neg_ref[pl.ds(row, 1), :], (SLC_LEN, Q_BLOCK)))
        return jnp.concatenate([jnp.concatenate(parts, axis=0)] * R, axis=1)

    def compressed(fast):
        cb = cbias_ref[0, pl.ds(pl.multiple_of(8 * (nb - 1) - 8 * i, 8), nc), :]
        s = _dot(kc_ref[0, 0], q_t) + cb
        if fast:
            e = jnp.exp2(s)
        else:
            e = jnp.where(s > 0.5 * NEG_INF, jnp.exp2(s - jnp.max(s, axis=0, keepdims=True)), 0.0)
        l = jnp.sum(e, axis=0, keepdims=True)
        p = e * jnp.where(l > 0.0, 1.0 / l, 0.0)
        return p, l, _dot(vc_ref[0, 0], p.astype(BF16))

    def select_blocks(p_c):
        p_sum = p_c[:, 0:Q_BLOCK]
        for r in range(1, R):
            p_sum = p_sum + p_c[:, r * Q_BLOCK:(r + 1) * Q_BLOCK]
        p_hi = p_sum.astype(BF16)
        p_lo = (p_sum - p_hi.astype(F32)).astype(BF16)
        imp = _dot(ov_ref[...], p_hi) + _dot(ov_ref[...], p_lo)
        n_slc = imp.shape[0]
        jj = lax.broadcasted_iota(jnp.int32, imp.shape, 0)
        qq = lax.broadcasted_iota(jnp.int32, imp.shape, 1)
        cur = 2 * i + jnp.where(qq >= SLC_LEN, 1, 0)
        forced = (jj == 0) | (jj == cur) | (jj == cur - 1)
        score = jnp.where(forced, TAKEN, jnp.where(jj <= cur, imp, -1.0))
        grp = 8
        j_rows = [jj[a:a + grp] for a in range(0, n_slc, grp)]
        for _ in range(min(SLC_TOPN, n_slc) - N_FORCED):
            cand = [(score[a:a + grp], j_rows[a // grp]) for a in range(0, n_slc, grp)]
            while len(cand) > 1:
                nxt = []
                for a in range(0, len(cand) - 1, 2):
                    (s0, j0), (s1, j1) = cand[a], cand[a + 1]
                    right = s1 > s0
                    nxt.append((jnp.maximum(s0, s1), jnp.where(right, j1, j0)))
                cand = nxt + cand[len(cand) - len(cand) % 2:]
            s8, j8 = cand[0]
            best = jnp.max(s8, axis=0, keepdims=True)
            first = jnp.min(jnp.where(s8 == best, j8, n_slc), axis=0, keepdims=True)
            score = jnp.where(jj == first, TAKEN, score)
        neg_ref[:n_slc] = jnp.where(score == TAKEN, msel_ref[0], NEG_INF)
        neg_ref[n_slc:] = jnp.full((NEG_ROWS, Q_BLOCK), NEG_INF, F32)

    def add_chunk(st, s, v_t):
        e = jnp.exp2(s)
        return st[0] + jnp.sum(e, axis=0, keepdims=True), st[1] + _dot(v_t, e.astype(BF16))

    zero = (jnp.zeros((1, N), F32), jnp.zeros((HEAD_DIM, N), F32))

    def selected_near():
        J0 = jnp.maximum(i - 1, 0)
        bias = sbias_ref[0, pl.ds(pl.multiple_of((2 - (i - J0)) * Q_BLOCK, Q_BLOCK), 2 * Q_BLOCK), :]
        return add_chunk(zero, _dot(k_rows(ks_ref, J0, 2), q_t) + bias + sel_mask(J0, 2), v_cols(vs_ref, J0, 2))

    def selected_far(st):
        n_far = jnp.maximum(i - 1, 0)
        n_chunks = (n_far + FAR_TILES - 1) // FAR_TILES

        def tile_of(c):
            return jnp.clip(FAR_TILES * c, 0, nb - FAR_TILES)

        def scores(c):
            return _dot(k_rows(ks_ref, tile_of(c), FAR_TILES), q_t)

        def values(c, e_ref, acc):
            return acc + _dot(v_cols(vs_ref, tile_of(c), FAR_TILES), e_ref[...])

        def stage(c, st, s_in, s_out, e_in, e_out):
            acc = values(c - 1, e_in, st[1])
            mask = sel_mask(tile_of(c), FAR_TILES, n_far - FAR_TILES * c)
            k_next = k_rows(ks_ref, tile_of(c + 1), FAR_TILES)
            s_out[...] = _dot(k_next, q_t)
            e = jnp.exp2(s_in[...] + mask)
            e_out[...] = e.astype(BF16)
            return st[0] + jnp.sum(e, axis=0, keepdims=True), acc

        s_a, s_b, e_a, e_b = stage_refs
        e_b[...] = jnp.zeros_like(e_b)
        s_a[...] = scores(0)

        def pair(c, st):
            st = stage(c, st, s_a, s_b, e_b, e_a)
            return stage(c + 1, st, s_b, s_a, e_a, e_b)

        def quad(c, st):
            return pair(c + 2, pair(c, st))

        n_octs = n_chunks // 8
        st = lax.fori_loop(0, n_octs, lambda t, st: quad(8 * t + 4, quad(8 * t, st)), st)
        done = 8 * n_octs
        st = lax.cond(n_chunks - done >= 4, lambda st: quad(done, st), lambda st: st, st)
        done = done + 4 * ((n_chunks - done) // 4)
        st = lax.cond(n_chunks - done >= 2, lambda st: pair(done, st), lambda st: st, st)

        def odd_tail(st):
            st = stage(n_chunks - 1, st, s_a, s_b, e_b, e_a)
            return st[0], values(n_chunks - 1, e_a, st[1])

        return lax.cond(n_chunks % 2 == 1, odd_tail, lambda st: (st[0], values(n_chunks - 1, e_b, st[1])), st)

    def window_fast():
        J0 = jnp.maximum(i - n_win, 0)
        bias = wbias_ref[0, pl.ds(pl.multiple_of((n_win - (i - J0)) * Q_BLOCK, Q_BLOCK), (n_win + 1) * Q_BLOCK), :]
        return add_chunk(zero, _dot(k_rows(kw_ref, J0, n_win + 1), q_t) + bias, v_cols(vw_ref, J0, n_win + 1))

    def sweep_exact(k_ref, v_ref, bias_ref, far_dist, n_back, with_sel):
        def body(t, st):
            J = i - t
            off = pl.multiple_of((far_dist - jnp.minimum(t, far_dist)) * Q_BLOCK, Q_BLOCK)
            s = _dot(k_rows(k_ref, J, 1), q_t) + bias_ref[0, pl.ds(off, Q_BLOCK), :]
            if with_sel:
                s = s + sel_mask(J, 1)
            return _softmax_tile_update(s, v_ref[0, J], *st)

        init = (jnp.full((1, N), NEG_INF, F32),) + zero
        _, l, acc = lax.fori_loop(0, jnp.minimum(i, n_back) + 1, body, init)
        return l, acc

    vals = {}

    def load_phase():
        vals["cmp"] = compressed(True)
        vals["win"] = window_fast()

    def select_phase():
        select_blocks(vals["cmp"][0])

    def near_phase():
        vals["near"] = selected_near()

    def exact_path():
        p_x, _, o_x = compressed(False)
        select_blocks(p_x)
        l_sx, a_sx = sweep_exact(ks_ref, vs_ref, sbias_ref, 2, nb, True)
        l_wx, a_wx = sweep_exact(kw_ref, vw_ref, wbias_ref, n_win, n_win, False)
        return o_x, a_sx * (1.0 / l_sx), a_wx * (1.0 / l_wx)

    def gate_row(br):
        return jnp.concatenate([gate_ref[br, 0, r:r + 1, u * Q_BLOCK:(u + 1) * Q_BLOCK] for r in range(R)], axis=1)

    def finish():
        _, l_c, o_c = vals["cmp"]
        l_w, a_w = vals["win"]
        l_s, a_s = selected_far(vals["near"])
        lane_q = lax.broadcasted_iota(jnp.int32, (1, N), 1) % Q_BLOCK
        has_cmp = (i > 0) | (lane_q >= CMP_LEN - 1)
        l_min = jnp.minimum(jnp.minimum(l_s, l_w), jnp.where(has_cmp, l_c, 1.0))
        o_cx, o_s, o_w = lax.cond(jnp.min(l_min) > NSA_TINY,
                                  lambda: (o_c, a_s * (1.0 / l_s), a_w * (1.0 / l_w)), exact_path)
        o_t = gate_row(0) * o_cx + gate_row(1) * o_s + gate_row(2) * o_w
        return _pairs_to_rows(o_t)

    return load_phase, select_phase, near_phase, finish


def _nsa_attention(colmat, rowmat, cmp_rows, cmp_cols, gates, cbias, sbias, wbias, ov_t, msel, B, S):
    nb = S // Q_BLOCK
    nc = S // CMP_STRIDE
    n_slc = S // SLC_LEN
    G, R = NSA_KV_HEADS, NSA_REP
    kern = functools.partial(_nsa_kernel, nb=nb)
    ns = nb // NSA_SUB
    return pl.pallas_call(
        kern,
        grid=(B, G, ns),
        in_specs=[
            pl.BlockSpec((R, NSA_SUB, HEAD_DIM, LANES), lambda b, g, i: (g, b * ns + i, 0, 0)),
            pl.BlockSpec((1, 1, nc, HEAD_DIM), lambda b, g, i: (g, b, 0, 0)),
            pl.BlockSpec((1, 1, HEAD_DIM, nc), lambda b, g, i: (NSA_KV_HEADS + g, b, 0, 0)),
            pl.BlockSpec((1, S, HEAD_DIM), lambda b, g, i: (ROW_KS + g, b, 0)),
            pl.BlockSpec((1, nb, HEAD_DIM, LANES), lambda b, g, i: (COL_VS + g, b, 0, 0)),
            pl.BlockSpec((1, S, HEAD_DIM), lambda b, g, i: (ROW_KW + g, b, 0)),
            pl.BlockSpec((1, nb, HEAD_DIM, LANES), lambda b, g, i: (COL_VW + g, b, 0, 0)),
            pl.BlockSpec((N_BRANCHES, 1, R, NSA_SUB * LANES), lambda b, g, i: (0, g, 0, b * ns + i)),
            pl.BlockSpec((1, cbias.shape[1], R * Q_BLOCK), lambda b, g, i: (g, 0, 0)),
            pl.BlockSpec((1, sbias.shape[1], R * Q_BLOCK), lambda b, g, i: (g, 0, 0)),
            pl.BlockSpec((1, wbias.shape[1], R * Q_BLOCK), lambda b, g, i: (g, 0, 0)),
            pl.BlockSpec((n_slc, nc), lambda b, g, i: (0, 0)),
            pl.BlockSpec((1, 1, Q_BLOCK), lambda b, g, i: (g, 0, 0)),
        ],
        out_specs=pl.BlockSpec((1, NSA_SUB * Q_BLOCK, R * HEAD_DIM), lambda b, g, i: (b, i, g)),
        out_shape=jax.ShapeDtypeStruct((B, S, NSA_HEADS * HEAD_DIM), BF16),
        scratch_shapes=[pltpu.VMEM((n_slc + NEG_ROWS, Q_BLOCK), F32)] * NSA_SUB
        + [pltpu.VMEM((FAR_TILES * Q_BLOCK, R * Q_BLOCK), F32)] * 2
        + [pltpu.VMEM((FAR_TILES * Q_BLOCK, R * Q_BLOCK), BF16)] * 2,
        compiler_params=_cparams(("parallel", "parallel", "arbitrary")),
        name="nsa_attention",
    )(colmat, cmp_rows, cmp_cols, rowmat, colmat, rowmat, colmat, gates, cbias, sbias, wbias, ov_t, msel)


SB_QB = 512
SB_KT = 256
SB_ZERO_LOG = -160.0


def _sb_kernel(q_ref, k_ref, v_ref, tri_ref, o_ref):
    i = pl.program_id(1)
    H = SB_HEADS
    tri = tri_ref[...]
    assert SB_QB == 2 * SB_KT
    below_diag = (lax.broadcasted_iota(jnp.int32, (SB_KT, SB_KT), 1)
                  < lax.broadcasted_iota(jnp.int32, (SB_KT, SB_KT), 0))

    def step(J, rows, st, masked):
        out = []
        for h in range(H):
            carry, acc = st[h]
            k_t = jnp.concatenate([k_ref[h, (SB_KT // LANES) * J + c] for c in range(SB_KT // LANES)], axis=1)
            z = _dot(q_ref[h, rows, :], k_t)
            sp = jnp.maximum(z, 0.0) + jnp.log2(1.0 + jnp.exp2(-jnp.abs(z)))
            lf = jnp.where(below_diag, -sp, 0.0) if masked else -sp
            lf_hi = lf.astype(BF16)
            lf_lo = (lf - lf_hi.astype(F32)).astype(BF16)
            after = _dot(lf_hi, tri) + _dot(lf_lo, tri) + carry
            w = jnp.exp2(z - sp + after)
            if masked:
                w = jnp.where(below_diag, w, 0.0)
            v = v_ref[h, pl.ds(pl.multiple_of(J * SB_KT, SB_KT), SB_KT), :]
            acc = acc + _dot(w.astype(BF16), v)
            carry = carry + jnp.sum(lf, axis=1, keepdims=True)
            out.append((carry, acc))
        return tuple(out)

    lo, hi = slice(0, SB_KT), slice(SB_KT, SB_QB)
    zero = tuple((jnp.zeros((SB_KT, 1), F32), jnp.zeros((SB_KT, HEAD_DIM), F32)) for _ in range(H))
    st_hi = step(2 * i, hi, step(2 * i + 1, hi, zero, True), False)
    st_lo = step(2 * i, lo, zero, True)

    def alive(st):
        worst = st[0][0]
        for h in range(1, H):
            worst = jnp.maximum(worst, st[h][0])
        return jnp.max(worst) > SB_ZERO_LOG

    def sweep(rows, st):
        def body(c):
            J, _, st = c
            st = step(J, rows, st, False)
            return J - 1, alive(st), st

        return lax.while_loop(lambda c: (c[0] >= 0) & c[1], body, (2 * i - 1, alive(st), st))[2]

    st_lo, st_hi = sweep(lo, st_lo), sweep(hi, st_hi)
    o_ref[0, lo, :] = jnp.concatenate([st_lo[h][1] for h in range(H)], axis=1).astype(o_ref.dtype)
    o_ref[0, hi, :] = jnp.concatenate([st_hi[h][1] for h in range(H)], axis=1).astype(o_ref.dtype)


def _sb_attention(colmat, rowmat, tri, B, S):
    nq = S // SB_QB
    nb = S // LANES
    H = SB_HEADS
    return pl.pallas_call(
        _sb_kernel,
        grid=(B, nq),
        in_specs=[
            pl.BlockSpec((H, SB_QB, HEAD_DIM), lambda b, i: (ROW_SQ // H, b * nq + i, 0)),
            pl.BlockSpec((H, nb, HEAD_DIM, LANES), lambda b, i: (COL_SK // H, b, 0, 0)),
            pl.BlockSpec((H, S, HEAD_DIM), lambda b, i: (ROW_SV // H, b, 0)),
            pl.BlockSpec((SB_KT, SB_KT), lambda b, i: (0, 0)),
        ],
        out_specs=pl.BlockSpec((1, SB_QB, H * HEAD_DIM), lambda b, i: (b, i, 0)),
        out_shape=jax.ShapeDtypeStruct((B, S, H * HEAD_DIM), BF16),
        compiler_params=_cparams(("parallel", "arbitrary")),
        name="sb_attention",
    )(rowmat, colmat, rowmat, tri)


SWA_SUB = 4


def _swa_kernel(q_ref, k_ref, v_ref, bias_ref, sink_ref, o_ref):
    R = SWA_REP
    sink = sink_ref[0]
    outs = []
    for u in range(SWA_SUB):
        i = pl.program_id(2) * SWA_SUB + u
        q_t = jnp.concatenate([q_ref[r, u] for r in range(R)], axis=1)
        J0 = jnp.maximum(i - 1, 0)
        k = k_ref[0, pl.ds(pl.multiple_of(J0 * Q_BLOCK, Q_BLOCK), 2 * Q_BLOCK), :]
        bias = bias_ref[0, pl.ds(pl.multiple_of((1 - (i - J0)) * Q_BLOCK, Q_BLOCK), 2 * Q_BLOCK), :]
        s = _dot(k, q_t) + bias
        m = jnp.maximum(jnp.max(s, axis=0, keepdims=True), sink)
        e = jnp.exp(s - m)
        den = jnp.sum(e, axis=0, keepdims=True) + jnp.exp(sink - m)
        v_t = jnp.concatenate([v_ref[0, J0], v_ref[0, J0 + 1]], axis=1)
        o_t = _dot(v_t, e.astype(BF16)) * (1.0 / den)
        outs.append(_pairs_to_rows(o_t).astype(o_ref.dtype))
    for u in range(SWA_SUB):
        o_ref[0, u * Q_BLOCK:(u + 1) * Q_BLOCK, :] = outs[u]


def _swa_attention(colmat, rowmat, bias, sinks, B, S):
    nb = S // Q_BLOCK
    ns = nb // SWA_SUB
    G, R = SWA_KV_HEADS, SWA_REP
    return pl.pallas_call(
        _swa_kernel,
        grid=(B, G, ns),
        in_specs=[
            pl.BlockSpec((R, SWA_SUB, HEAD_DIM, LANES), lambda b, g, i: (COL_WQ // R + g, b * ns + i, 0, 0)),
            pl.BlockSpec((1, S, HEAD_DIM), lambda b, g, i: (ROW_WK + g, b, 0)),
            pl.BlockSpec((1, nb, HEAD_DIM, LANES), lambda b, g, i: (COL_WV + g, b, 0, 0)),
            pl.BlockSpec((1, 3 * Q_BLOCK, R * Q_BLOCK), lambda b, g, i: (g, 0, 0)),
            pl.BlockSpec((1, 1, R * Q_BLOCK), lambda b, g, i: (g, 0, 0)),
        ],
        out_specs=pl.BlockSpec((1, SWA_SUB * Q_BLOCK, R * HEAD_DIM), lambda b, g, i: (b, i, g)),
        out_shape=jax.ShapeDtypeStruct((B, S, SWA_HEADS * HEAD_DIM), BF16),
        compiler_params=_cparams(("parallel", "parallel", "arbitrary")),
        name="swa_attention",
    )(colmat, rowmat, colmat, bias, sinks)


MERGE_TM = 512


def _merge_kernel(x_ref, nw_ref, wg_ref, ya_ref, yb_ref, yc_ref, ua_ref, ub_ref, uc_ref, wo_ref, o_ref):
    x = x_ref[...]
    h = _rms_rows(x, nw_ref[...]).astype(BF16)
    merged = None
    for br, (y_ref, u_ref) in enumerate(((ya_ref, ua_ref), (yb_ref, ub_ref), (yc_ref, uc_ref))):
        g = jax.nn.sigmoid(_dot(h, wg_ref[:, br * D_MODEL:(br + 1) * D_MODEL]))
        t = g * _dot(y_ref[...], u_ref[...])
        merged = t if merged is None else merged + t
    o_ref[...] = x + _dot(merged.astype(BF16), wo_ref[...])


def _merge(x2, nw, w_bgate, y_nsa, y_sb, y_swa, u_nsa, u_sb, u_swa, w_out, l):
    T = x2.shape[0]
    full = lambda w: _layer_spec(l, w.shape[1:])
    rows = lambda n: pl.BlockSpec((MERGE_TM, n), lambda i: (i, 0))
    return pl.pallas_call(
        _merge_kernel,
        grid=(T // MERGE_TM,),
        in_specs=[rows(D_MODEL), pl.BlockSpec((1, D_MODEL), lambda i: (0, 0)), full(w_bgate),
                  rows(y_nsa.shape[1]), rows(y_sb.shape[1]), rows(y_swa.shape[1]),
                  full(u_nsa), full(u_sb), full(u_swa), full(w_out)],
        out_specs=rows(D_MODEL),
        out_shape=jax.ShapeDtypeStruct((T, D_MODEL), F32),
        compiler_params=_cparams(("parallel",)),
        name="merge",
    )(x2, nw.reshape(1, D_MODEL), w_bgate, y_nsa, y_sb, y_swa, u_nsa, u_sb, u_swa, w_out)


def _t5_bucket(dist):
    max_exact = NUM_BUCKETS // 2
    d = jnp.maximum(dist, 0)
    df = jnp.maximum(d, 1).astype(F32)
    large = max_exact + (jnp.log(df / max_exact) / math.log(MAX_DISTANCE / max_exact)
                         * (NUM_BUCKETS - max_exact)).astype(jnp.int32)
    large = jnp.minimum(large, NUM_BUCKETS - 1)
    return jnp.where(d < max_exact, d, large)


def _toeplitz(ext, delta):
    H = ext.shape[0]
    n = Q_BLOCK
    c = delta + MAX_DISTANCE
    w = ext[:, c - (n - 1):c + n]
    flat = jnp.pad(jnp.broadcast_to(w[:, None, :], (H, n, 2 * n - 1)), ((0, 0), (0, 0), (0, 1))).reshape(H, 2 * n * n)
    return flat[:, n - 1:n - 1 + n * (2 * n - 1)].reshape(H, n, 2 * n - 1)[:, :, :n]


def _lanes(t, G, R):
    K, Q = t.shape[1:]
    return t.reshape(G, R, K, Q).transpose(0, 2, 1, 3).reshape(G, K, R * Q)


def _bias_tables(rel_bias, nb, qk_bound):
    n = Q_BLOCK
    G, R = NSA_KV_HEADS, NSA_REP
    by_dist = rel_bias[_t5_bucket(jnp.arange(MAX_DISTANCE)), :].T.astype(F32)
    far = rel_bias[NUM_BUCKETS - 1, :].astype(F32)
    neg = jnp.full((1, n), NEG_INF, F32)
    k = jnp.arange(n)[:, None]
    q = jnp.arange(n)[None, :]
    above = (k > q)[None]

    rel = (by_dist[:NSA_HEADS] - far[:NSA_HEADS, None]) * LOG2E
    m_g = qk_bound + jnp.maximum(jnp.max(rel.reshape(G, -1), axis=1), 0.0)
    m_h = jnp.repeat(m_g, R)[:, None, None]
    ext = jnp.concatenate([jnp.broadcast_to(neg, (NSA_HEADS, n)), rel, jnp.zeros((NSA_HEADS, n), F32)], axis=1)
    t0, t1 = _toeplitz(ext, 0), _toeplitz(ext, n)
    zeros = jnp.zeros_like(t0)
    masked = jnp.full_like(t0, NEG_INF)
    sbias = _lanes(jnp.concatenate([zeros, t1, t0, masked], axis=1), G, R)
    n_win = NSA_WINDOW // n
    edge = jnp.where(above, zeros, NEG_INF)
    tiles = [edge] + [zeros] * (n_win - 2) + [t1, t0]
    wbias = _lanes(jnp.concatenate([t - m_h for t in tiles] + [masked] * n_win, axis=1), G, R)
    off = 8 * (nb - 1)
    near = jnp.stack([ext[:, 97 - CMP_STRIDE * m:97 - CMP_STRIDE * m + n] for m in range(-9, 7)], axis=1)
    n_rows = 16 * nb - 8
    cb = jnp.concatenate([jnp.zeros((NSA_HEADS, off - 9, n), F32), near,
                          jnp.full((NSA_HEADS, n_rows - off - 7, n), NEG_INF, F32)], axis=1) - m_h
    cbias = _lanes(cb, G, R)
    msel = jnp.broadcast_to(-m_g[:, None, None], (G, 1, n))

    hs = slice(NSA_HEADS, NSA_HEADS + SWA_HEADS)
    ext_w = jnp.concatenate([jnp.broadcast_to(neg, (SWA_HEADS, n)), by_dist[hs],
                             jnp.broadcast_to(far[hs, None], (SWA_HEADS, n))], axis=1)
    s0 = _toeplitz(ext_w, 0)
    s1 = jnp.where(above, _toeplitz(ext_w, n), NEG_INF)
    swa_bias = _lanes(jnp.concatenate([s1, s0, jnp.full_like(s0, NEG_INF)], axis=1), SWA_KV_HEADS, SWA_REP)
    return cbias, sbias, wbias, msel, swa_bias


def _overlap_t(S):
    nc, n_slc = S // CMP_STRIDE, S // SLC_LEN
    n = np.arange(nc)[None, :]
    j = np.arange(n_slc)[:, None]
    ov = (n * CMP_STRIDE < (j + 1) * SLC_LEN) & (n * CMP_STRIDE + CMP_LEN - 1 >= j * SLC_LEN) & (n < nc - 1)
    return jnp.asarray(ov, BF16)


def _head_gains(heads, norm_of, scale_of):
    ones = jnp.ones((HEAD_DIM,), F32)
    gains = [norm_of[name].astype(F32) * scale_of.get(name, 1.0) if name in norm_of else ones for name, _ in heads]
    return jnp.concatenate(gains).reshape(1, -1)


def kernel(x, rel_bias, ffn1_norm, ffn1_w_gate, ffn1_w_up, ffn1_w_down, mix_norm, w_in, nsa_q_norm, nsa_k_norm, nsa_cmp_pos, nsa_cmp_k_w1, nsa_cmp_k_w2, nsa_cmp_v_w1, nsa_cmp_v_w2, swa_q_norm, swa_k_norm, swa_sinks, w_up_nsa, w_up_sb, w_up_swa, w_out, ffn2_norm, ffn2_w_gate, ffn2_w_up, ffn2_w_down):
    B, S, D = x.shape
    T = B * S
    nb = S // Q_BLOCK
    nc = S // CMP_STRIDE
    depth = w_in.shape[0]
    scale = HEAD_DIM ** -0.5
    scale_of = {"nq": scale * LOG2E, "sq": scale * LOG2E, "wq": scale}
    qk_bound = (BOUND_SLACK * LOG2E * HEAD_DIM ** 0.5
                * jnp.max(jnp.abs(nsa_q_norm.astype(F32))) * jnp.max(jnp.abs(nsa_k_norm.astype(F32))))
    cbias, sbias, wbias, msel, swa_bias = _bias_tables(rel_bias, nb, qk_bound)
    plan = _proj_plan(scale_of)
    ov_t = _overlap_t(S)
    tri = jnp.asarray(np.tril(np.ones((SB_KT, SB_KT), np.float32), -1), BF16)

    (ffn1_w_gate, ffn1_w_up, ffn1_w_down, ffn2_w_gate, ffn2_w_up, ffn2_w_down, w_up_nsa, w_up_sb, w_up_swa,
     w_out) = (w.astype(BF16) for w in (ffn1_w_gate, ffn1_w_up, ffn1_w_down, ffn2_w_gate, ffn2_w_up, ffn2_w_down,
                                        w_up_nsa, w_up_sb, w_up_swa, w_out))
    tail0 = OFF_NGATE + N_GATE
    w_head, w_tail = w_in[:, :, :OFF_NGATE].astype(BF16), w_in[:, :, tail0:].astype(BF16)
    w_bgate = w_tail[:, :, OFF_BGATE - tail0:]
    w_gate = jnp.pad(w_in[:, :, OFF_NGATE:tail0], ((0, 0), (0, 0), (0, LANES - N_GATE))).astype(BF16)
    runs = []
    for _, off in ROW_HEADS + COL_HEADS:
        if runs and runs[-1][1] == off:
            runs[-1][1] = off + HEAD_DIM
        else:
            runs.append([off, off + HEAD_DIM])
    pieces = [w_head[:, :, a:b] if a < OFF_NGATE else w_tail[:, :, a - tail0:b - tail0] for a, b in runs]
    w_proj = jnp.concatenate(pieces + [w_gate], axis=2)

    x2 = x.reshape(T, D)
    for l in range(depth):
        x2 = _ffn(x2, ffn1_norm[l], ffn1_w_gate, ffn1_w_up, ffn1_w_down, l)

        norm_of = {"nq": nsa_q_norm[l], "nks": nsa_k_norm[l], "nkw": nsa_k_norm[l],
                   "wq": swa_q_norm[l], "wk": swa_k_norm[l]}
        gain = jnp.concatenate([_head_gains(ROW_HEADS, norm_of, scale_of), _head_gains(COL_HEADS, norm_of, scale_of),
                                jnp.ones((1, LANES), F32)], axis=1)
        rowmat, colmat, gates = _proj(x2, mix_norm[l], w_proj, gain, plan, l)
        gates = gates[:N_GATE].reshape(N_BRANCHES, NSA_KV_HEADS, NSA_REP, T)

        rows4 = rowmat[ROW_KC:ROW_KC + 2 * NSA_KV_HEADS].reshape(2 * NSA_KV_HEADS, B, nc, CMP_STRIDE * HEAD_DIM)
        cmp_rows, cmp_cols = _compress(rows4, jnp.stack([nsa_cmp_k_w1[l], nsa_cmp_v_w1[l]]),
                                       jnp.stack([nsa_cmp_k_w2[l], nsa_cmp_v_w2[l]]),
                                       nsa_cmp_pos[l], nsa_k_norm[l])

        y_nsa = _nsa_attention(colmat, rowmat, cmp_rows, cmp_cols, gates, cbias, sbias, wbias, ov_t, msel, B, S)
        y_sb = _sb_attention(colmat, rowmat, tri, B, S)
        sinks = jnp.repeat(swa_sinks[l].astype(F32).reshape(SWA_KV_HEADS, 1, SWA_REP), Q_BLOCK, axis=2)
        y_swa = _swa_attention(colmat, rowmat, swa_bias, sinks, B, S)

        x2 = _merge(x2, mix_norm[l], w_bgate, y_nsa.reshape(T, -1), y_sb.reshape(T, -1),
                    y_swa.reshape(T, -1), w_up_nsa, w_up_sb, w_up_swa, w_out, l)
        x2 = _ffn(x2, ffn2_norm[l], ffn2_w_gate, ffn2_w_up, ffn2_w_down, l)
    return x2.reshape(B, S, D)
```

```python
import functools
import math

import numpy as np
import jax
import jax.numpy as jnp
from jax import lax
from jax.experimental import pallas as pl
from jax.experimental.pallas import tpu as pltpu

D_MODEL = 1024
HEAD_DIM = 64
Q_BLOCK = 128
NSA_HEADS = 8
NSA_KV_HEADS = 2
NSA_REP = NSA_HEADS // NSA_KV_HEADS
CMP_LEN = 32
CMP_STRIDE = 16
CMP_HIDDEN = 256
SLC_LEN = 64
SLC_TOPN = 16
NSA_WINDOW = 512
SB_HEADS = 4
SWA_HEADS = 4
SWA_KV_HEADS = 2
SWA_REP = SWA_HEADS // SWA_KV_HEADS
SWA_WINDOW = 128
NUM_BUCKETS = 32
MAX_DISTANCE = 128
D_FF = 2816
NORM_EPS = 1e-6
NEG_INF = -1e30
N_BRANCHES = 3
N_GATE = 3 * NSA_HEADS

LANES = 128
VMEM_LIMIT = 56 * 1024 * 1024
BF16 = jnp.bfloat16
F32 = jnp.float32

_sizes = (NSA_HEADS * HEAD_DIM,) + (NSA_KV_HEADS * HEAD_DIM,) * 6 + (N_GATE,) \
    + (SB_HEADS * HEAD_DIM,) * 3 + (SWA_HEADS * HEAD_DIM, SWA_KV_HEADS * HEAD_DIM,
                                    SWA_KV_HEADS * HEAD_DIM, N_BRANCHES * D_MODEL)
_offs = np.concatenate([[0], np.cumsum(_sizes)])
(OFF_NQ, OFF_NKC, OFF_NVC, OFF_NKS, OFF_NVS, OFF_NKW, OFF_NVW, OFF_NGATE,
 OFF_SQ, OFF_SK, OFF_SV, OFF_WQ, OFF_WK, OFF_WV, OFF_BGATE) = (int(o) for o in _offs[:-1])

ROW_HEADS = ([("nkc", OFF_NKC + 64 * g) for g in range(2)] + [("nvc", OFF_NVC + 64 * g) for g in range(2)]
             + [("nks", OFF_NKS + 64 * g) for g in range(2)] + [("nkw", OFF_NKW + 64 * g) for g in range(2)]
             + [("sq", OFF_SQ + 64 * h) for h in range(4)] + [("sv", OFF_SV + 64 * h) for h in range(4)]
             + [("wk", OFF_WK + 64 * g) for g in range(2)])
ROW_KC, ROW_VC, ROW_KS, ROW_KW, ROW_SQ, ROW_SV, ROW_WK = 0, 2, 4, 6, 8, 12, 16
COL_HEADS = ([("nq", OFF_NQ + 64 * h) for h in range(8)] + [("nvs", OFF_NVS + 64 * g) for g in range(2)]
             + [("nvw", OFF_NVW + 64 * g) for g in range(2)] + [("sk", OFF_SK + 64 * h) for h in range(4)]
             + [("wq", OFF_WQ + 64 * h) for h in range(4)] + [("wv", OFF_WV + 64 * g) for g in range(2)])
COL_NQ, COL_VS, COL_VW, COL_SK, COL_WQ, COL_WV = 0, 8, 10, 12, 16, 20


def _dot(a, b):
    return jnp.dot(a, b, preferred_element_type=F32)


def _rms_rows(x, w):
    ms = jnp.mean(x * x, axis=-1, keepdims=True)
    return x * lax.rsqrt(ms + NORM_EPS) * w


def _cparams(sem, flags=None):
    return pltpu.CompilerParams(dimension_semantics=sem, vmem_limit_bytes=VMEM_LIMIT, flags=flags)


FFN_TM = 512
MXU_TILE = 256
FFN_FC = MXU_TILE


def _ffn_kernel(x_ref, nw_ref, wg_ref, wu_ref, wd_ref, o_ref):
    x = x_ref[...]
    h = _rms_rows(x, nw_ref[...]).astype(BF16)
    acc = jnp.zeros(x.shape, F32)
    for c in range(D_FF // FFN_FC):
        cols = slice(c * FFN_FC, (c + 1) * FFN_FC)
        g = _dot(h, wg_ref[:, cols])
        u = _dot(h, wu_ref[:, cols])
        a = (g * jax.nn.sigmoid(g)) * u
        acc = acc + _dot(a.astype(BF16), wd_ref[cols, :])
    o_ref[...] = x + 0.5 * acc


def _layer_spec(l, shape, **kw):
    return pl.BlockSpec((None,) + tuple(shape), lambda i: (l, 0, 0), **kw)


def _ffn(x2, nw, wg, wu, wd, l):
    T = x2.shape[0]
    resident = lambda shape: _layer_spec(l, shape, pipeline_mode=pl.Buffered(1))
    return pl.pallas_call(
        _ffn_kernel,
        grid=(T // FFN_TM,),
        in_specs=[
            pl.BlockSpec((FFN_TM, D_MODEL), lambda i: (i, 0)),
            pl.BlockSpec((1, D_MODEL), lambda i: (0, 0)),
            resident((D_MODEL, D_FF)), resident((D_MODEL, D_FF)), resident((D_FF, D_MODEL)),
        ],
        out_specs=pl.BlockSpec((FFN_TM, D_MODEL), lambda i: (i, 0)),
        out_shape=jax.ShapeDtypeStruct((T, D_MODEL), F32),
        compiler_params=_cparams(("parallel",)),
        name="ffn",
    )(x2, nw.reshape(1, D_MODEL), wg, wu, wd)


PROJ_TM = 512


GATE_ROWS = 32

_NORMED = ("nq", "nks", "nkw", "wq", "wk")


def _proj_plan(scale_of):
    plan = []
    for kind, heads in (("row", ROW_HEADS), ("col", COL_HEADS)):
        for p in range(len(heads) // 2):
            name = heads[2 * p][0]
            assert heads[2 * p + 1][0] == name
            plan.append((kind, p, name in _NORMED, 1.0 if name in _NORMED else scale_of.get(name, 1.0)))
    plan.append(("gate", 0, False, 1.0))
    return tuple(plan)


def _proj_kernel(x_ref, nw_ref, w_ref, gain_ref, bd_ref, row_ref, col_ref, gate_ref, *, plan):
    h = _rms_rows(x_ref[...], nw_ref[...]).astype(BF16)
    pair = 2 * MXU_TILE
    for c0 in range(0, len(plan) * LANES, pair):
        width = min(pair, len(plan) * LANES - c0)
        zg = _dot(h, w_ref[:, c0:c0 + width])
        for u in range(width // LANES):
            t = c0 // LANES + u
            kind, p, normed, scale = plan[t]
            z = zg[:, u * LANES:(u + 1) * LANES]
            if normed:
                ms = _dot((z * z).astype(BF16), bd_ref[...])
                z = z * lax.rsqrt(ms + NORM_EPS) * gain_ref[:, t * LANES:(t + 1) * LANES]
            elif scale != 1.0:
                z = z * scale
            if kind == "row":
                row_ref[2 * p] = z[:, :HEAD_DIM].astype(row_ref.dtype)
                row_ref[2 * p + 1] = z[:, HEAD_DIM:].astype(row_ref.dtype)
            elif kind == "col":
                z_t = z.T
                for c in range(PROJ_TM // LANES):
                    col_ref[2 * p, c] = z_t[:HEAD_DIM, c * LANES:(c + 1) * LANES].astype(col_ref.dtype)
                    col_ref[2 * p + 1, c] = z_t[HEAD_DIM:, c * LANES:(c + 1) * LANES].astype(col_ref.dtype)
            else:
                gate_ref[...] = jax.nn.sigmoid(z).T[:GATE_ROWS]


def _proj(x2, nw, w, gain, plan, l):
    T = x2.shape[0]
    n_row, n_col = len(ROW_HEADS), len(COL_HEADS)
    bd = np.kron(np.eye(2, dtype=np.float32), np.full((HEAD_DIM, HEAD_DIM), 1.0 / HEAD_DIM, np.float32))
    return pl.pallas_call(
        functools.partial(_proj_kernel, plan=plan),
        grid=(T // PROJ_TM,),
        in_specs=[
            pl.BlockSpec((PROJ_TM, D_MODEL), lambda i: (i, 0)),
            pl.BlockSpec((1, D_MODEL), lambda i: (0, 0)),
            _layer_spec(l, w.shape[1:]),
            pl.BlockSpec(gain.shape, lambda i: (0, 0)),
            pl.BlockSpec((LANES, LANES), lambda i: (0, 0)),
        ],
        out_specs=[
            pl.BlockSpec((n_row, PROJ_TM, HEAD_DIM), lambda i: (0, i, 0)),
            pl.BlockSpec((n_col, PROJ_TM // LANES, HEAD_DIM, LANES), lambda i: (0, i, 0, 0)),
            pl.BlockSpec((GATE_ROWS, PROJ_TM), lambda i: (0, i)),
        ],
        out_shape=[jax.ShapeDtypeStruct((n_row, T, HEAD_DIM), BF16),
                   jax.ShapeDtypeStruct((n_col, T // LANES, HEAD_DIM, LANES), BF16),
                   jax.ShapeDtypeStruct((GATE_ROWS, T), F32)],
        compiler_params=_cparams(("parallel",)),
        name="proj",
    )(x2, nw.reshape(1, D_MODEL), w, gain, jnp.asarray(bd, BF16))


def _compress_kernel(r_ref, w1_ref, w2_ref, pos_ref, kn_ref, row_ref, col_ref):
    kind = pl.program_id(0) // NSA_KV_HEADS
    r = r_ref[0, 0]
    nc = r.shape[0]
    half = CMP_STRIDE * HEAD_DIM
    w1 = w1_ref[0]
    p_lo = _dot(r, w1[:half])
    p_hi = _dot(r, w1[half:])
    p_pos = _dot(jnp.broadcast_to(pos_ref[...], (8, CMP_LEN * HEAD_DIM)).astype(BF16), w1)[:1]
    hdn = p_lo + pltpu.roll(p_hi, nc - 1, 0) + p_pos
    hdn = hdn * jax.nn.sigmoid(hdn)
    out = _dot(hdn.astype(BF16), w2_ref[0])
    ms = jnp.sum(out * out, axis=-1, keepdims=True) * (1.0 / HEAD_DIM)
    normed = out * lax.rsqrt(ms + NORM_EPS) * kn_ref[...]
    out = jnp.where(kind == 0, normed, out)
    row_ref[0, 0] = out[:, :HEAD_DIM].astype(row_ref.dtype)
    col_ref[0, 0] = out.T[:HEAD_DIM].astype(col_ref.dtype)


def _compress(rows4, w1, w2, pos, k_norm):
    assert ROW_KC == 0 and ROW_VC == NSA_KV_HEADS
    _, B, NC, _ = rows4.shape
    kn = jnp.pad(k_norm.reshape(1, HEAD_DIM), ((0, 0), (0, LANES - HEAD_DIM)))
    w2p = jnp.pad(w2, ((0, 0), (0, 0), (0, LANES - HEAD_DIM))).astype(BF16)
    return pl.pallas_call(
        _compress_kernel,
        grid=(4, B),
        in_specs=[
            pl.BlockSpec((1, 1, NC, CMP_STRIDE * HEAD_DIM), lambda h, b: (h, b, 0, 0)),
            pl.BlockSpec((1, CMP_LEN * HEAD_DIM, CMP_HIDDEN), lambda h, b: (h // NSA_KV_HEADS, 0, 0)),
            pl.BlockSpec((1, CMP_HIDDEN, LANES), lambda h, b: (h // NSA_KV_HEADS, 0, 0)),
            pl.BlockSpec((1, CMP_LEN * HEAD_DIM), lambda h, b: (0, 0)),
            pl.BlockSpec((1, LANES), lambda h, b: (0, 0)),
        ],
        out_specs=[
            pl.BlockSpec((1, 1, NC, HEAD_DIM), lambda h, b: (h, b, 0, 0)),
            pl.BlockSpec((1, 1, HEAD_DIM, NC), lambda h, b: (h, b, 0, 0)),
        ],
        out_shape=[jax.ShapeDtypeStruct((4, B, NC, HEAD_DIM), BF16),
                   jax.ShapeDtypeStruct((4, B, HEAD_DIM, NC), BF16)],
        compiler_params=_cparams(("parallel", "parallel")),
        name="nsa_compress",
    )(rows4, w1.astype(BF16), w2p, pos.reshape(1, CMP_LEN * HEAD_DIM), kn)


def _softmax_tile_update(s, v_t, m, l, acc):
    m_new = jnp.maximum(m, jnp.max(s, axis=0, keepdims=True))
    alpha = jnp.exp2(m - m_new)
    p = jnp.exp2(s - m_new)
    l_new = alpha * l + jnp.sum(p, axis=0, keepdims=True)
    acc_new = alpha * acc + _dot(v_t, p.astype(BF16))
    return m_new, l_new, acc_new


def _pairs_to_rows(o_t):
    n = o_t.shape[1] // LANES
    outs = []
    for p in range(n // 2):
        a = o_t[:, (2 * p) * LANES:(2 * p + 1) * LANES]
        b = o_t[:, (2 * p + 1) * LANES:(2 * p + 2) * LANES]
        outs.append(jnp.concatenate([a, b], axis=0).T)
    return outs[0] if len(outs) == 1 else jnp.concatenate(outs, axis=1)


FAR_TILES = 4
NSA_SUB = 4
TAKEN = -2.0
NEG_ROWS = 8
N_FORCED = 3
LOG2E = math.log2(math.e)
NSA_TINY = 2.0 ** -100
BOUND_SLACK = 1.05


def _nsa_kernel(q_ref, kc_ref, vc_ref, ks_ref, vs_ref, kw_ref, vw_ref, gate_ref,
                cbias_ref, sbias_ref, wbias_ref, ov_ref, msel_ref, o_ref, *scratch, nb):
    neg_refs, stage_refs = scratch[:NSA_SUB], scratch[NSA_SUB:]
    refs = (q_ref, kc_ref, vc_ref, ks_ref, vs_ref, kw_ref, vw_ref, gate_ref, cbias_ref, sbias_ref, wbias_ref,
            ov_ref, msel_ref)
    blocks = [_nsa_block(u, refs, neg_refs[u], stage_refs, nb) for u in range(NSA_SUB)]
    for phase in range(3):
        for blk in blocks:
            blk[phase]()
    rows = [blk[3]() for blk in blocks]
    for u in range(NSA_SUB):
        o_ref[0, u * Q_BLOCK:(u + 1) * Q_BLOCK, :] = rows[u].astype(o_ref.dtype)


def _nsa_block(u, refs, neg_ref, stage_refs, nb):
    (q_ref, kc_ref, vc_ref, ks_ref, vs_ref, kw_ref, vw_ref, gate_ref, cbias_ref, sbias_ref, wbias_ref,
     ov_ref, msel_ref) = refs
    i = pl.program_id(2) * NSA_SUB + u
    R = NSA_REP
    N = R * Q_BLOCK
    nc = kc_ref.shape[2]
    q_t = jnp.concatenate([q_ref[r, u] for r in range(R)], axis=1)
    n_win = NSA_WINDOW // Q_BLOCK

    def k_rows(ref, J0, n_tiles):
        return ref[0, pl.ds(pl.multiple_of(J0 * Q_BLOCK, Q_BLOCK), n_tiles * Q_BLOCK), :]

    def v_cols(ref, J0, n_tiles):
        return jnp.concatenate([ref[0, J0 + u] for u in range(n_tiles)], axis=1)

    def sel_mask(J0, n_tiles, n_valid=None):
        n_slc = neg_ref.shape[0] - NEG_ROWS
        parts = []
        for u in range(2 * n_tiles):
            row = 2 * J0 + u
            if n_valid is not None:
                row = jnp.where(u // 2 < n_valid, row, n_slc)
            parts.append(jnp.broadcast_to(neg_ref[pl.ds(row, 1), :], (SLC_LEN, Q_BLOCK)))
        return jnp.concatenate([jnp.concatenate(parts, axis=0)] * R, axis=1)

    def compressed(fast):
        cb = cbias_ref[0, pl.ds(pl.multiple_of(8 * (nb - 1) - 8 * i, 8), nc), :]
        s = _dot(kc_ref[0, 0], q_t) + cb
        if fast:
            e = jnp.exp2(s)
        else:
            e = jnp.where(s > 0.5 * NEG_INF, jnp.exp2(s - jnp.max(s, axis=0, keepdims=True)), 0.0)
        l = jnp.sum(e, axis=0, keepdims=True)
        p = e * jnp.where(l > 0.0, 1.0 / l, 0.0)
        return p, l, _dot(vc_ref[0, 0], p.astype(BF16))

    def select_blocks(p_c):
        p_sum = p_c[:, 0:Q_BLOCK]
        for r in range(1, R):
            p_sum = p_sum + p_c[:, r * Q_BLOCK:(r + 1) * Q_BLOCK]
        p_hi = p_sum.astype(BF16)
        p_lo = (p_sum - p_hi.astype(F32)).astype(BF16)
        imp = _dot(ov_ref[...], p_hi) + _dot(ov_ref[...], p_lo)
        n_slc = imp.shape[0]
        jj = lax.broadcasted_iota(jnp.int32, imp.shape, 0)
        qq = lax.broadcasted_iota(jnp.int32, imp.shape, 1)
        cur = 2 * i + jnp.where(qq >= SLC_LEN, 1, 0)
        forced = (jj == 0) | (jj == cur) | (jj == cur - 1)
        score = jnp.where(forced, TAKEN, jnp.where(jj <= cur, imp, -1.0))
        grp = 8
        j_rows = [jj[a:a + grp] for a in range(0, n_slc, grp)]
        for _ in range(min(SLC_TOPN, n_slc) - N_FORCED):
            cand = [(score[a:a + grp], j_rows[a // grp]) for a in range(0, n_slc, grp)]
            while len(cand) > 1:
                nxt = []
                for a in range(0, len(cand) - 1, 2):
                    (s0, j0), (s1, j1) = cand[a], cand[a + 1]
                    right = s1 > s0
                    nxt.append((jnp.maximum(s0, s1), jnp.where(right, j1, j0)))
                cand = nxt + cand[len(cand) - len(cand) % 2:]
            s8, j8 = cand[0]
            best = jnp.max(s8, axis=0, keepdims=True)
            first = jnp.min(jnp.where(s8 == best, j8, n_slc), axis=0, keepdims=True)
            score = jnp.where(jj == first, TAKEN, score)
        neg_ref[:n_slc] = jnp.where(score == TAKEN, msel_ref[0], NEG_INF)
        neg_ref[n_slc:] = jnp.full((NEG_ROWS, Q_BLOCK), NEG_INF, F32)

    def add_chunk(st, s, v_t):
        e = jnp.exp2(s)
        return st[0] + jnp.sum(e, axis=0, keepdims=True), st[1] + _dot(v_t, e.astype(BF16))

    zero = (jnp.zeros((1, N), F32), jnp.zeros((HEAD_DIM, N), F32))

    def selected_near():
        J0 = jnp.maximum(i - 1, 0)
        bias = sbias_ref[0, pl.ds(pl.multiple_of((2 - (i - J0)) * Q_BLOCK, Q_BLOCK), 2 * Q_BLOCK), :]
        return add_chunk(zero, _dot(k_rows(ks_ref, J0, 2), q_t) + bias + sel_mask(J0, 2), v_cols(vs_ref, J0, 2))

    def selected_far(st):
        n_far = jnp.maximum(i - 1, 0)
        n_chunks = (n_far + FAR_TILES - 1) // FAR_TILES

        def tile_of(c):
            return jnp.clip(FAR_TILES * c, 0, nb - FAR_TILES)

        def scores(c):
            return _dot(k_rows(ks_ref, tile_of(c), FAR_TILES), q_t)

        def values(c, e_ref, acc):
            return acc + _dot(v_cols(vs_ref, tile_of(c), FAR_TILES), e_ref[...])

        def stage(c, st, s_in, s_out, e_in, e_out):
            acc = values(c - 1, e_in, st[1])
            mask = sel_mask(tile_of(c), FAR_TILES, n_far - FAR_TILES * c)
            k_next = k_rows(ks_ref, tile_of(c + 1), FAR_TILES)
            s_out[...] = _dot(k_next, q_t)
            e = jnp.exp2(s_in[...] + mask)
            e_out[...] = e.astype(BF16)
            return st[0] + jnp.sum(e, axis=0, keepdims=True), acc

        s_a, s_b, e_a, e_b = stage_refs
        e_b[...] = jnp.zeros_like(e_b)
        s_a[...] = scores(0)

        def pair(c, st):
            st = stage(c, st, s_a, s_b, e_b, e_a)
            return stage(c + 1, st, s_b, s_a, e_a, e_b)

        def quad(c, st):
            return pair(c + 2, pair(c, st))

        n_octs = n_chunks // 8
        st = lax.fori_loop(0, n_octs, lambda t, st: quad(8 * t + 4, quad(8 * t, st)), st)
        done = 8 * n_octs
        st = lax.cond(n_chunks - done >= 4, lambda st: quad(done, st), lambda st: st, st)
        done = done + 4 * ((n_chunks - done) // 4)
        st = lax.cond(n_chunks - done >= 2, lambda st: pair(done, st), lambda st: st, st)

        def odd_tail(st):
            st = stage(n_chunks - 1, st, s_a, s_b, e_b, e_a)
            return st[0], values(n_chunks - 1, e_a, st[1])

        return lax.cond(n_chunks % 2 == 1, odd_tail, lambda st: (st[0], values(n_chunks - 1, e_b, st[1])), st)

    def window_fast():
        J0 = jnp.maximum(i - n_win, 0)
        bias = wbias_ref[0, pl.ds(pl.multiple_of((n_win - (i - J0)) * Q_BLOCK, Q_BLOCK), (n_win + 1) * Q_BLOCK), :]
        return add_chunk(zero, _dot(k_rows(kw_ref, J0, n_win + 1), q_t) + bias, v_cols(vw_ref, J0, n_win + 1))

    def sweep_exact(k_ref, v_ref, bias_ref, far_dist, n_back, with_sel):
        def body(t, st):
            J = i - t
            off = pl.multiple_of((far_dist - jnp.minimum(t, far_dist)) * Q_BLOCK, Q_BLOCK)
            s = _dot(k_rows(k_ref, J, 1), q_t) + bias_ref[0, pl.ds(off, Q_BLOCK), :]
            if with_sel:
                s = s + sel_mask(J, 1)
            return _softmax_tile_update(s, v_ref[0, J], *st)

        init = (jnp.full((1, N), NEG_INF, F32),) + zero
        _, l, acc = lax.fori_loop(0, jnp.minimum(i, n_back) + 1, body, init)
        return l, acc

    vals = {}

    def load_phase():
        vals["cmp"] = compressed(True)
        vals["win"] = window_fast()

    def select_phase():
        select_blocks(vals["cmp"][0])

    def near_phase():
        vals["near"] = selected_near()

    def exact_path():
        p_x, _, o_x = compressed(False)
        select_blocks(p_x)
        l_sx, a_sx = sweep_exact(ks_ref, vs_ref, sbias_ref, 2, nb, True)
        l_wx, a_wx = sweep_exact(kw_ref, vw_ref, wbias_ref, n_win, n_win, False)
        return o_x, a_sx * (1.0 / l_sx), a_wx * (1.0 / l_wx)

    def gate_row(br):
        return jnp.concatenate([gate_ref[br, 0, r:r + 1, u * Q_BLOCK:(u + 1) * Q_BLOCK] for r in range(R)], axis=1)

    def finish():
        _, l_c, o_c = vals["cmp"]
        l_w, a_w = vals["win"]
        l_s, a_s = selected_far(vals["near"])
        lane_q = lax.broadcasted_iota(jnp.int32, (1, N), 1) % Q_BLOCK
        has_cmp = (i > 0) | (lane_q >= CMP_LEN - 1)
        l_min = jnp.minimum(jnp.minimum(l_s, l_w), jnp.where(has_cmp, l_c, 1.0))
        o_cx, o_s, o_w = lax.cond(jnp.min(l_min) > NSA_TINY,
                                  lambda: (o_c, a_s * (1.0 / l_s), a_w * (1.0 / l_w)), exact_path)
        o_t = gate_row(0) * o_cx + gate_row(1) * o_s + gate_row(2) * o_w
        return _pairs_to_rows(o_t)

    return load_phase, select_phase, near_phase, finish


def _nsa_attention(colmat, rowmat, cmp_rows, cmp_cols, gates, cbias, sbias, wbias, ov_t, msel, B, S):
    nb = S // Q_BLOCK
    nc = S // CMP_STRIDE
    n_slc = S // SLC_LEN
    G, R = NSA_KV_HEADS, NSA_REP
    kern = functools.partial(_nsa_kernel, nb=nb)
    ns = nb // NSA_SUB
    return pl.pallas_call(
        kern,
        grid=(B, G, ns),
        in_specs=[
            pl.BlockSpec((R, NSA_SUB, HEAD_DIM, LANES), lambda b, g, i: (g, b * ns + i, 0, 0)),
            pl.BlockSpec((1, 1, nc, HEAD_DIM), lambda b, g, i: (g, b, 0, 0)),
            pl.BlockSpec((1, 1, HEAD_DIM, nc), lambda b, g, i: (NSA_KV_HEADS + g, b, 0, 0)),
            pl.BlockSpec((1, S, HEAD_DIM), lambda b, g, i: (ROW_KS + g, b, 0)),
            pl.BlockSpec((1, nb, HEAD_DIM, LANES), lambda b, g, i: (COL_VS + g, b, 0, 0)),
            pl.BlockSpec((1, S, HEAD_DIM), lambda b, g, i: (ROW_KW + g, b, 0)),
            pl.BlockSpec((1, nb, HEAD_DIM, LANES), lambda b, g, i: (COL_VW + g, b, 0, 0)),
            pl.BlockSpec((N_BRANCHES, 1, R, NSA_SUB * LANES), lambda b, g, i: (0, g, 0, b * ns + i)),
            pl.BlockSpec((1, cbias.shape[1], R * Q_BLOCK), lambda b, g, i: (g, 0, 0)),
            pl.BlockSpec((1, sbias.shape[1], R * Q_BLOCK), lambda b, g, i: (g, 0, 0)),
            pl.BlockSpec((1, wbias.shape[1], R * Q_BLOCK), lambda b, g, i: (g, 0, 0)),
            pl.BlockSpec((n_slc, nc), lambda b, g, i: (0, 0)),
            pl.BlockSpec((1, 1, Q_BLOCK), lambda b, g, i: (g, 0, 0)),
        ],
        out_specs=pl.BlockSpec((1, NSA_SUB * Q_BLOCK, R * HEAD_DIM), lambda b, g, i: (b, i, g)),
        out_shape=jax.ShapeDtypeStruct((B, S, NSA_HEADS * HEAD_DIM), BF16),
        scratch_shapes=[pltpu.VMEM((n_slc + NEG_ROWS, Q_BLOCK), F32)] * NSA_SUB
        + [pltpu.VMEM((FAR_TILES * Q_BLOCK, R * Q_BLOCK), F32)] * 2
        + [pltpu.VMEM((FAR_TILES * Q_BLOCK, R * Q_BLOCK), BF16)] * 2,
        compiler_params=_cparams(("parallel", "parallel", "arbitrary")),
        name="nsa_attention",
    )(colmat, cmp_rows, cmp_cols, rowmat, colmat, rowmat, colmat, gates, cbias, sbias, wbias, ov_t, msel)


SB_QB = 512
SB_KT = 256
SB_ZERO_LOG = -160.0


def _sb_kernel(q_ref, k_ref, v_ref, tri_ref, o_ref):
    i = pl.program_id(1)
    H = SB_HEADS
    tri = tri_ref[...]
    assert SB_QB == 2 * SB_KT
    below_diag = (lax.broadcasted_iota(jnp.int32, (SB_KT, SB_KT), 1)
                  < lax.broadcasted_iota(jnp.int32, (SB_KT, SB_KT), 0))

    def step(J, rows, st, masked):
        out = []
        for h in range(H):
            carry, acc = st[h]
            k_t = jnp.concatenate([k_ref[h, (SB_KT // LANES) * J + c] for c in range(SB_KT // LANES)], axis=1)
            z = _dot(q_ref[h, rows, :], k_t)
            sp = jnp.maximum(z, 0.0) + jnp.log2(1.0 + jnp.exp2(-jnp.abs(z)))
            lf = jnp.where(below_diag, -sp, 0.0) if masked else -sp
            lf_hi = lf.astype(BF16)
            lf_lo = (lf - lf_hi.astype(F32)).astype(BF16)
            after = _dot(lf_hi, tri) + _dot(lf_lo, tri) + carry
            w = jnp.exp2(z - sp + after)
            if masked:
                w = jnp.where(below_diag, w, 0.0)
            v = v_ref[h, pl.ds(pl.multiple_of(J * SB_KT, SB_KT), SB_KT), :]
            acc = acc + _dot(w.astype(BF16), v)
            carry = carry + jnp.sum(lf, axis=1, keepdims=True)
            out.append((carry, acc))
        return tuple(out)

    lo, hi = slice(0, SB_KT), slice(SB_KT, SB_QB)
    zero = tuple((jnp.zeros((SB_KT, 1), F32), jnp.zeros((SB_KT, HEAD_DIM), F32)) for _ in range(H))
    st_hi = step(2 * i, hi, step(2 * i + 1, hi, zero, True), False)
    st_lo = step(2 * i, lo, zero, True)

    def alive(st):
        worst = st[0][0]
        for h in range(1, H):
            worst = jnp.maximum(worst, st[h][0])
        return jnp.max(worst) > SB_ZERO_LOG

    def sweep(rows, st):
        def body(c):
            J, _, st = c
            st = step(J, rows, st, False)
            return J - 1, alive(st), st

        return lax.while_loop(lambda c: (c[0] >= 0) & c[1], body, (2 * i - 1, alive(st), st))[2]

    st_lo, st_hi = sweep(lo, st_lo), sweep(hi, st_hi)
    o_ref[0, lo, :] = jnp.concatenate([st_lo[h][1] for h in range(H)], axis=1).astype(o_ref.dtype)
    o_ref[0, hi, :] = jnp.concatenate([st_hi[h][1] for h in range(H)], axis=1).astype(o_ref.dtype)


def _sb_attention(colmat, rowmat, tri, B, S):
    nq = S // SB_QB
    nb = S // LANES
    H = SB_HEADS
    return pl.pallas_call(
        _sb_kernel,
        grid=(B, nq),
        in_specs=[
            pl.BlockSpec((H, SB_QB, HEAD_DIM), lambda b, i: (ROW_SQ // H, b * nq + i, 0)),
            pl.BlockSpec((H, nb, HEAD_DIM, LANES), lambda b, i: (COL_SK // H, b, 0, 0)),
            pl.BlockSpec((H, S, HEAD_DIM), lambda b, i: (ROW_SV // H, b, 0)),
            pl.BlockSpec((SB_KT, SB_KT), lambda b, i: (0, 0)),
        ],
        out_specs=pl.BlockSpec((1, SB_QB, H * HEAD_DIM), lambda b, i: (b, i, 0)),
        out_shape=jax.ShapeDtypeStruct((B, S, H * HEAD_DIM), BF16),
        compiler_params=_cparams(("parallel", "arbitrary")),
        name="sb_attention",
    )(rowmat, colmat, rowmat, tri)


SWA_SUB = 4


def _swa_kernel(q_ref, k_ref, v_ref, bias_ref, sink_ref, o_ref):
    R = SWA_REP
    sink = sink_ref[0]
    outs = []
    for u in range(SWA_SUB):
        i = pl.program_id(2) * SWA_SUB + u
        q_t = jnp.concatenate([q_ref[r, u] for r in range(R)], axis=1)
        J0 = jnp.maximum(i - 1, 0)
        k = k_ref[0, pl.ds(pl.multiple_of(J0 * Q_BLOCK, Q_BLOCK), 2 * Q_BLOCK), :]
        bias = bias_ref[0, pl.ds(pl.multiple_of((1 - (i - J0)) * Q_BLOCK, Q_BLOCK), 2 * Q_BLOCK), :]
        s = _dot(k, q_t) + bias
        m = jnp.maximum(jnp.max(s, axis=0, keepdims=True), sink)
        e = jnp.exp(s - m)
        den = jnp.sum(e, axis=0, keepdims=True) + jnp.exp(sink - m)
        v_t = jnp.concatenate([v_ref[0, J0], v_ref[0, J0 + 1]], axis=1)
        o_t = _dot(v_t, e.astype(BF16)) * (1.0 / den)
        outs.append(_pairs_to_rows(o_t).astype(o_ref.dtype))
    for u in range(SWA_SUB):
        o_ref[0, u * Q_BLOCK:(u + 1) * Q_BLOCK, :] = outs[u]


def _swa_attention(colmat, rowmat, bias, sinks, B, S):
    nb = S // Q_BLOCK
    ns = nb // SWA_SUB
    G, R = SWA_KV_HEADS, SWA_REP
    return pl.pallas_call(
        _swa_kernel,
        grid=(B, G, ns),
        in_specs=[
            pl.BlockSpec((R, SWA_SUB, HEAD_DIM, LANES), lambda b, g, i: (COL_WQ // R + g, b * ns + i, 0, 0)),
            pl.BlockSpec((1, S, HEAD_DIM), lambda b, g, i: (ROW_WK + g, b, 0)),
            pl.BlockSpec((1, nb, HEAD_DIM, LANES), lambda b, g, i: (COL_WV + g, b, 0, 0)),
            pl.BlockSpec((1, 3 * Q_BLOCK, R * Q_BLOCK), lambda b, g, i: (g, 0, 0)),
            pl.BlockSpec((1, 1, R * Q_BLOCK), lambda b, g, i: (g, 0, 0)),
        ],
        out_specs=pl.BlockSpec((1, SWA_SUB * Q_BLOCK, R * HEAD_DIM), lambda b, g, i: (b, i, g)),
        out_shape=jax.ShapeDtypeStruct((B, S, SWA_HEADS * HEAD_DIM), BF16),
        compiler_params=_cparams(("parallel", "parallel", "arbitrary")),
        name="swa_attention",
    )(colmat, rowmat, colmat, bias, sinks)


MERGE_TM = 512


def _merge_kernel(x_ref, nw_ref, wg_ref, ya_ref, yb_ref, yc_ref, ua_ref, ub_ref, uc_ref, wo_ref, o_ref):
    x = x_ref[...]
    h = _rms_rows(x, nw_ref[...]).astype(BF16)
    merged = None
    for br, (y_ref, u_ref) in enumerate(((ya_ref, ua_ref), (yb_ref, ub_ref), (yc_ref, uc_ref))):
        g = jax.nn.sigmoid(_dot(h, wg_ref[:, br * D_MODEL:(br + 1) * D_MODEL]))
        t = g * _dot(y_ref[...], u_ref[...])
        merged = t if merged is None else merged + t
    o_ref[...] = x + _dot(merged.astype(BF16), wo_ref[...])


def _merge(x2, nw, w_bgate, y_nsa, y_sb, y_swa, u_nsa, u_sb, u_swa, w_out, l):
    T = x2.shape[0]
    full = lambda w: _layer_spec(l, w.shape[1:])
    rows = lambda n: pl.BlockSpec((MERGE_TM, n), lambda i: (i, 0))
    return pl.pallas_call(
        _merge_kernel,
        grid=(T // MERGE_TM,),
        in_specs=[rows(D_MODEL), pl.BlockSpec((1, D_MODEL), lambda i: (0, 0)), full(w_bgate),
                  rows(y_nsa.shape[1]), rows(y_sb.shape[1]), rows(y_swa.shape[1]),
                  full(u_nsa), full(u_sb), full(u_swa), full(w_out)],
        out_specs=rows(D_MODEL),
        out_shape=jax.ShapeDtypeStruct((T, D_MODEL), F32),
        compiler_params=_cparams(("parallel",)),
        name="merge",
    )(x2, nw.reshape(1, D_MODEL), w_bgate, y_nsa, y_sb, y_swa, u_nsa, u_sb, u_swa, w_out)


def _t5_bucket(dist):
    max_exact = NUM_BUCKETS // 2
    d = jnp.maximum(dist, 0)
    df = jnp.maximum(d, 1).astype(F32)
    large = max_exact + (jnp.log(df / max_exact) / math.log(MAX_DISTANCE / max_exact)
                         * (NUM_BUCKETS - max_exact)).astype(jnp.int32)
    large = jnp.minimum(large, NUM_BUCKETS - 1)
    return jnp.where(d < max_exact, d, large)


def _toeplitz(ext, delta):
    H = ext.shape[0]
    n = Q_BLOCK
    c = delta + MAX_DISTANCE
    w = ext[:, c - (n - 1):c + n]
    flat = jnp.pad(jnp.broadcast_to(w[:, None, :], (H, n, 2 * n - 1)), ((0, 0), (0, 0), (0, 1))).reshape(H, 2 * n * n)
    return flat[:, n - 1:n - 1 + n * (2 * n - 1)].reshape(H, n, 2 * n - 1)[:, :, :n]


def _lanes(t, G, R):
    K, Q = t.shape[1:]
    return t.reshape(G, R, K, Q).transpose(0, 2, 1, 3).reshape(G, K, R * Q)


def _bias_tables(rel_bias, nb, qk_bound):
    n = Q_BLOCK
    G, R = NSA_KV_HEADS, NSA_REP
    by_dist = rel_bias[_t5_bucket(jnp.arange(MAX_DISTANCE)), :].T.astype(F32)
    far = rel_bias[NUM_BUCKETS - 1, :].astype(F32)
    neg = jnp.full((1, n), NEG_INF, F32)
    k = jnp.arange(n)[:, None]
    q = jnp.arange(n)[None, :]
    above = (k > q)[None]

    rel = (by_dist[:NSA_HEADS] - far[:NSA_HEADS, None]) * LOG2E
    m_g = qk_bound + jnp.maximum(jnp.max(rel.reshape(G, -1), axis=1), 0.0)
    m_h = jnp.repeat(m_g, R)[:, None, None]
    ext = jnp.concatenate([jnp.broadcast_to(neg, (NSA_HEADS, n)), rel, jnp.zeros((NSA_HEADS, n), F32)], axis=1)
    t0, t1 = _toeplitz(ext, 0), _toeplitz(ext, n)
    zeros = jnp.zeros_like(t0)
    masked = jnp.full_like(t0, NEG_INF)
    sbias = _lanes(jnp.concatenate([zeros, t1, t0, masked], axis=1), G, R)
    n_win = NSA_WINDOW // n
    edge = jnp.where(above, zeros, NEG_INF)
    tiles = [edge] + [zeros] * (n_win - 2) + [t1, t0]
    wbias = _lanes(jnp.concatenate([t - m_h for t in tiles] + [masked] * n_win, axis=1), G, R)
    off = 8 * (nb - 1)
    near = jnp.stack([ext[:, 97 - CMP_STRIDE * m:97 - CMP_STRIDE * m + n] for m in range(-9, 7)], axis=1)
    n_rows = 16 * nb - 8
    cb = jnp.concatenate([jnp.zeros((NSA_HEADS, off - 9, n), F32), near,
                          jnp.full((NSA_HEADS, n_rows - off - 7, n), NEG_INF, F32)], axis=1) - m_h
    cbias = _lanes(cb, G, R)
    msel = jnp.broadcast_to(-m_g[:, None, None], (G, 1, n))

    hs = slice(NSA_HEADS, NSA_HEADS + SWA_HEADS)
    ext_w = jnp.concatenate([jnp.broadcast_to(neg, (SWA_HEADS, n)), by_dist[hs],
                             jnp.broadcast_to(far[hs, None], (SWA_HEADS, n))], axis=1)
    s0 = _toeplitz(ext_w, 0)
    s1 = jnp.where(above, _toeplitz(ext_w, n), NEG_INF)
    swa_bias = _lanes(jnp.concatenate([s1, s0, jnp.full_like(s0, NEG_INF)], axis=1), SWA_KV_HEADS, SWA_REP)
    return cbias, sbias, wbias, msel, swa_bias


def _overlap_t(S):
    nc, n_slc = S // CMP_STRIDE, S // SLC_LEN
    n = np.arange(nc)[None, :]
    j = np.arange(n_slc)[:, None]
    ov = (n * CMP_STRIDE < (j + 1) * SLC_LEN) & (n * CMP_STRIDE + CMP_LEN - 1 >= j * SLC_LEN) & (n < nc - 1)
    return jnp.asarray(ov, BF16)


def _head_gains(heads, norm_of, scale_of):
    ones = jnp.ones((HEAD_DIM,), F32)
    gains = [norm_of[name].astype(F32) * scale_of.get(name, 1.0) if name in norm_of else ones for name, _ in heads]
    return jnp.concatenate(gains).reshape(1, -1)


def kernel(x, rel_bias, ffn1_norm, ffn1_w_gate, ffn1_w_up, ffn1_w_down, mix_norm, w_in, nsa_q_norm, nsa_k_norm, nsa_cmp_pos, nsa_cmp_k_w1, nsa_cmp_k_w2, nsa_cmp_v_w1, nsa_cmp_v_w2, swa_q_norm, swa_k_norm, swa_sinks, w_up_nsa, w_up_sb, w_up_swa, w_out, ffn2_norm, ffn2_w_gate, ffn2_w_up, ffn2_w_down):
    B, S, D = x.shape
    T = B * S
    nb = S // Q_BLOCK
    nc = S // CMP_STRIDE
    depth = w_in.shape[0]
    scale = HEAD_DIM ** -0.5
    scale_of = {"nq": scale * LOG2E, "sq": scale * LOG2E, "wq": scale}
    qk_bound = (BOUND_SLACK * LOG2E * HEAD_DIM ** 0.5
                * jnp.max(jnp.abs(nsa_q_norm.astype(F32))) * jnp.max(jnp.abs(nsa_k_norm.astype(F32))))
    cbias, sbias, wbias, msel, swa_bias = _bias_tables(rel_bias, nb, qk_bound)
    plan = _proj_plan(scale_of)
    ov_t = _overlap_t(S)
    tri = jnp.asarray(np.tril(np.ones((SB_KT, SB_KT), np.float32), -1), BF16)

    (ffn1_w_gate, ffn1_w_up, ffn1_w_down, ffn2_w_gate, ffn2_w_up, ffn2_w_down, w_up_nsa, w_up_sb, w_up_swa,
     w_out) = (w.astype(BF16) for w in (ffn1_w_gate, ffn1_w_up, ffn1_w_down, ffn2_w_gate, ffn2_w_up, ffn2_w_down,
                                        w_up_nsa, w_up_sb, w_up_swa, w_out))
    tail0 = OFF_NGATE + N_GATE
    w_head, w_tail = w_in[:, :, :OFF_NGATE].astype(BF16), w_in[:, :, tail0:].astype(BF16)
    w_bgate = w_tail[:, :, OFF_BGATE - tail0:]
    w_gate = jnp.pad(w_in[:, :, OFF_NGATE:tail0], ((0, 0), (0, 0), (0, LANES - N_GATE))).astype(BF16)
    runs = []
    for _, off in ROW_HEADS + COL_HEADS:
        if runs and runs[-1][1] == off:
            runs[-1][1] = off + HEAD_DIM
        else:
            runs.append([off, off + HEAD_DIM])
    pieces = [w_head[:, :, a:b] if a < OFF_NGATE else w_tail[:, :, a - tail0:b - tail0] for a, b in runs]
    w_proj = jnp.concatenate(pieces + [w_gate], axis=2)

    x2 = x.reshape(T, D)
    for l in range(depth):
        x2 = _ffn(x2, ffn1_norm[l], ffn1_w_gate, ffn1_w_up, ffn1_w_down, l)

        norm_of = {"nq": nsa_q_norm[l], "nks": nsa_k_norm[l], "nkw": nsa_k_norm[l],
                   "wq": swa_q_norm[l], "wk": swa_k_norm[l]}
        gain = jnp.concatenate([_head_gains(ROW_HEADS, norm_of, scale_of), _head_gains(COL_HEADS, norm_of, scale_of),
                                jnp.ones((1, LANES), F32)], axis=1)
        rowmat, colmat, gates = _proj(x2, mix_norm[l], w_proj, gain, plan, l)
        gates = gates[:N_GATE].reshape(N_BRANCHES, NSA_KV_HEADS, NSA_REP, T)

        rows4 = rowmat[ROW_KC:ROW_KC + 2 * NSA_KV_HEADS].reshape(2 * NSA_KV_HEADS, B, nc, CMP_STRIDE * HEAD_DIM)
        cmp_rows, cmp_cols = _compress(rows4, jnp.stack([nsa_cmp_k_w1[l], nsa_cmp_v_w1[l]]),
                                       jnp.stack([nsa_cmp_k_w2[l], nsa_cmp_v_w2[l]]),
                                       nsa_cmp_pos[l], nsa_k_norm[l])

        y_nsa = _nsa_attention(colmat, rowmat, cmp_rows, cmp_cols, gates, cbias, sbias, wbias, ov_t, msel, B, S)
        y_sb = _sb_attention(colmat, rowmat, tri, B, S)
        sinks = jnp.repeat(swa_sinks[l].astype(F32).reshape(SWA_KV_HEADS, 1, SWA_REP), Q_BLOCK, axis=2)
        y_swa = _swa_attention(colmat, rowmat, swa_bias, sinks, B, S)

        x2 = _merge(x2, mix_norm[l], w_bgate, y_nsa.reshape(T, -1), y_sb.reshape(T, -1),
                    y_swa.reshape(T, -1), w_up_nsa, w_up_sb, w_up_swa, w_out, l)
        x2 = _ffn(x2, ffn2_norm[l], ffn2_w_gate, ffn2_w_up, ffn2_w_down, l)
    return x2.reshape(B, S, D)
```

```python
import functools
import math

import numpy as np
import jax
import jax.numpy as jnp
from jax import lax
from jax.experimental import pallas as pl
from jax.experimental.pallas import tpu as pltpu

D_MODEL = 1024
HEAD_DIM = 64
Q_BLOCK = 128
NSA_HEADS = 8
NSA_KV_HEADS = 2
NSA_REP = NSA_HEADS // NSA_KV_HEADS
CMP_LEN = 32
CMP_STRIDE = 16
CMP_HIDDEN = 256
SLC_LEN = 64
SLC_TOPN = 16
NSA_WINDOW = 512
SB_HEADS = 4
SWA_HEADS = 4
SWA_KV_HEADS = 2
SWA_REP = SWA_HEADS // SWA_KV_HEADS
SWA_WINDOW = 128
NUM_BUCKETS = 32
MAX_DISTANCE = 128
D_FF = 2816
NORM_EPS = 1e-6
NEG_INF = -1e30
N_BRANCHES = 3
N_GATE = 3 * NSA_HEADS

LANES = 128
VMEM_LIMIT = 56 * 1024 * 1024
BF16 = jnp.bfloat16
F32 = jnp.float32

_sizes = (NSA_HEADS * HEAD_DIM,) + (NSA_KV_HEADS * HEAD_DIM,) * 6 + (N_GATE,) \
    + (SB_HEADS * HEAD_DIM,) * 3 + (SWA_HEADS * HEAD_DIM, SWA_KV_HEADS * HEAD_DIM,
                                    SWA_KV_HEADS * HEAD_DIM, N_BRANCHES * D_MODEL)
_offs = np.concatenate([[0], np.cumsum(_sizes)])
(OFF_NQ, OFF_NKC, OFF_NVC, OFF_NKS, OFF_NVS, OFF_NKW, OFF_NVW, OFF_NGATE,
 OFF_SQ, OFF_SK, OFF_SV, OFF_WQ, OFF_WK, OFF_WV, OFF_BGATE) = (int(o) for o in _offs[:-1])

ROW_HEADS = ([("nkc", OFF_NKC + 64 * g) for g in range(2)] + [("nvc", OFF_NVC + 64 * g) for g in range(2)]
             + [("nks", OFF_NKS + 64 * g) for g in range(2)] + [("nkw", OFF_NKW + 64 * g) for g in range(2)]
             + [("sq", OFF_SQ + 64 * h) for h in range(4)] + [("sv", OFF_SV + 64 * h) for h in range(4)]
             + [("wk", OFF_WK + 64 * g) for g in range(2)])
ROW_KC, ROW_VC, ROW_KS, ROW_KW, ROW_SQ, ROW_SV, ROW_WK = 0, 2, 4, 6, 8, 12, 16
COL_HEADS = ([("nq", OFF_NQ + 64 * h) for h in range(8)] + [("nvs", OFF_NVS + 64 * g) for g in range(2)]
             + [("nvw", OFF_NVW + 64 * g) for g in range(2)] + [("sk", OFF_SK + 64 * h) for h in range(4)]
             + [("wq", OFF_WQ + 64 * h) for h in range(4)] + [("wv", OFF_WV + 64 * g) for g in range(2)])
COL_NQ, COL_VS, COL_VW, COL_SK, COL_WQ, COL_WV = 0, 8, 10, 12, 16, 20


def _dot(a, b):
    return jnp.dot(a, b, preferred_element_type=F32)


def _rms_rows(x, w):
    ms = jnp.mean(x * x, axis=-1, keepdims=True)
    return x * lax.rsqrt(ms + NORM_EPS) * w


def _cparams(sem, flags=None):
    return pltpu.CompilerParams(dimension_semantics=sem, vmem_limit_bytes=VMEM_LIMIT, flags=flags)


FFN_TM = 512
MXU_TILE = 256
FFN_FC = MXU_TILE


def _ffn_kernel(x_ref, nw_ref, wg_ref, wu_ref, wd_ref, o_ref):
    x = x_ref[...]
    h = _rms_rows(x, nw_ref[...]).astype(BF16)
    acc = jnp.zeros(x.shape, F32)
    for c in range(D_FF // FFN_FC):
        cols = slice(c * FFN_FC, (c + 1) * FFN_FC)
        g = _dot(h, wg_ref[:, cols])
        u = _dot(h, wu_ref[:, cols])
        a = (g * jax.nn.sigmoid(g)) * u
        acc = acc + _dot(a.astype(BF16), wd_ref[cols, :])
    o_ref[...] = x + 0.5 * acc


def _layer_spec(l, shape, **kw):
    return pl.BlockSpec((None,) + tuple(shape), lambda i: (l, 0, 0), **kw)


def _ffn(x2, nw, wg, wu, wd, l):
    T = x2.shape[0]
    resident = lambda shape: _layer_spec(l, shape, pipeline_mode=pl.Buffered(1))
    return pl.pallas_call(
        _ffn_kernel,
        grid=(T // FFN_TM,),
        in_specs=[
            pl.BlockSpec((FFN_TM, D_MODEL), lambda i: (i, 0)),
            pl.BlockSpec((1, D_MODEL), lambda i: (0, 0)),
            resident((D_MODEL, D_FF)), resident((D_MODEL, D_FF)), resident((D_FF, D_MODEL)),
        ],
        out_specs=pl.BlockSpec((FFN_TM, D_MODEL), lambda i: (i, 0)),
        out_shape=jax.ShapeDtypeStruct((T, D_MODEL), F32),
        compiler_params=_cparams(("parallel",)),
        name="ffn",
    )(x2, nw.reshape(1, D_MODEL), wg, wu, wd)


PROJ_TM = 512


GATE_ROWS = 32

_NORMED = ("nq", "nks", "nkw", "wq", "wk")


def _proj_plan(scale_of):
    plan = []
    for kind, heads in (("row", ROW_HEADS), ("col", COL_HEADS)):
        for p in range(len(heads) // 2):
            name = heads[2 * p][0]
            assert heads[2 * p + 1][0] == name
            plan.append((kind, p, name in _NORMED, 1.0 if name in _NORMED else scale_of.get(name, 1.0)))
    plan.append(("gate", 0, False, 1.0))
    return tuple(plan)


def _proj_kernel(x_ref, nw_ref, w_ref, gain_ref, bd_ref, row_ref, col_ref, gate_ref, *, plan):
    h = _rms_rows(x_ref[...], nw_ref[...]).astype(BF16)
    pair = 2 * MXU_TILE
    for c0 in range(0, len(plan) * LANES, pair):
        width = min(pair, len(plan) * LANES - c0)
        zg = _dot(h, w_ref[:, c0:c0 + width])
        for u in range(width // LANES):
            t = c0 // LANES + u
            kind, p, normed, scale = plan[t]
            z = zg[:, u * LANES:(u + 1) * LANES]
            if normed:
                ms = _dot((z * z).astype(BF16), bd_ref[...])
                z = z * lax.rsqrt(ms + NORM_EPS) * gain_ref[:, t * LANES:(t + 1) * LANES]
            elif scale != 1.0:
                z = z * scale
            if kind == "row":
                row_ref[2 * p] = z[:, :HEAD_DIM].astype(row_ref.dtype)
                row_ref[2 * p + 1] = z[:, HEAD_DIM:].astype(row_ref.dtype)
            elif kind == "col":
                z_t = z.T
                for c in range(PROJ_TM // LANES):
                    col_ref[2 * p, c] = z_t[:HEAD_DIM, c * LANES:(c + 1) * LANES].astype(col_ref.dtype)
                    col_ref[2 * p + 1, c] = z_t[HEAD_DIM:, c * LANES:(c + 1) * LANES].astype(col_ref.dtype)
            else:
                gate_ref[...] = jax.nn.sigmoid(z).T[:GATE_ROWS]


def _proj(x2, nw, w, gain, plan, l):
    T = x2.shape[0]
    n_row, n_col = len(ROW_HEADS), len(COL_HEADS)
    bd = np.kron(np.eye(2, dtype=np.float32), np.full((HEAD_DIM, HEAD_DIM), 1.0 / HEAD_DIM, np.float32))
    return pl.pallas_call(
        functools.partial(_proj_kernel, plan=plan),
        grid=(T // PROJ_TM,),
        in_specs=[
            pl.BlockSpec((PROJ_TM, D_MODEL), lambda i: (i, 0)),
            pl.BlockSpec((1, D_MODEL), lambda i: (0, 0)),
            _layer_spec(l, w.shape[1:]),
            pl.BlockSpec(gain.shape, lambda i: (0, 0)),
            pl.BlockSpec((LANES, LANES), lambda i: (0, 0)),
        ],
        out_specs=[
            pl.BlockSpec((n_row, PROJ_TM, HEAD_DIM), lambda i: (0, i, 0)),
            pl.BlockSpec((n_col, PROJ_TM // LANES, HEAD_DIM, LANES), lambda i: (0, i, 0, 0)),
            pl.BlockSpec((GATE_ROWS, PROJ_TM), lambda i: (0, i)),
        ],
        out_shape=[jax.ShapeDtypeStruct((n_row, T, HEAD_DIM), BF16),
                   jax.ShapeDtypeStruct((n_col, T // LANES, HEAD_DIM, LANES), BF16),
                   jax.ShapeDtypeStruct((GATE_ROWS, T), F32)],
        compiler_params=_cparams(("parallel",)),
        name="proj",
    )(x2, nw.reshape(1, D_MODEL), w, gain, jnp.asarray(bd, BF16))


def _compress_kernel(r_ref, w1_ref, w2_ref, pos_ref, kn_ref, row_ref, col_ref):
    kind = pl.program_id(0) // NSA_KV_HEADS
    r = r_ref[0, 0]
    nc = r.shape[0]
    half = CMP_STRIDE * HEAD_DIM
    w1 = w1_ref[0]
    p_lo = _dot(r, w1[:half])
    p_hi = _dot(r, w1[half:])
    p_pos = _dot(jnp.broadcast_to(pos_ref[...], (8, CMP_LEN * HEAD_DIM)).astype(BF16), w1)[:1]
    hdn = p_lo + pltpu.roll(p_hi, nc - 1, 0) + p_pos
    hdn = hdn * jax.nn.sigmoid(hdn)
    out = _dot(hdn.astype(BF16), w2_ref[0])
    ms = jnp.sum(out * out, axis=-1, keepdims=True) * (1.0 / HEAD_DIM)
    normed = out * lax.rsqrt(ms + NORM_EPS) * kn_ref[...]
    out = jnp.where(kind == 0, normed, out)
    row_ref[0, 0] = out[:, :HEAD_DIM].astype(row_ref.dtype)
    col_ref[0, 0] = out.T[:HEAD_DIM].astype(col_ref.dtype)


def _compress(rows4, w1, w2, pos, k_norm):
    assert ROW_KC == 0 and ROW_VC == NSA_KV_HEADS
    _, B, NC, _ = rows4.shape
    kn = jnp.pad(k_norm.reshape(1, HEAD_DIM), ((0, 0), (0, LANES - HEAD_DIM)))
    w2p = jnp.pad(w2, ((0, 0), (0, 0), (0, LANES - HEAD_DIM))).astype(BF16)
    return pl.pallas_call(
        _compress_kernel,
        grid=(4, B),
        in_specs=[
            pl.BlockSpec((1, 1, NC, CMP_STRIDE * HEAD_DIM), lambda h, b: (h, b, 0, 0)),
            pl.BlockSpec((1, CMP_LEN * HEAD_DIM, CMP_HIDDEN), lambda h, b: (h // NSA_KV_HEADS, 0, 0)),
            pl.BlockSpec((1, CMP_HIDDEN, LANES), lambda h, b: (h // NSA_KV_HEADS, 0, 0)),
            pl.BlockSpec((1, CMP_LEN * HEAD_DIM), lambda h, b: (0, 0)),
            pl.BlockSpec((1, LANES), lambda h, b: (0, 0)),
        ],
        out_specs=[
            pl.BlockSpec((1, 1, NC, HEAD_DIM), lambda h, b: (h, b, 0, 0)),
            pl.BlockSpec((1, 1, HEAD_DIM, NC), lambda h, b: (h, b, 0, 0)),
        ],
        out_shape=[jax.ShapeDtypeStruct((4, B, NC, HEAD_DIM), BF16),
                   jax.ShapeDtypeStruct((4, B, HEAD_DIM, NC), BF16)],
        compiler_params=_cparams(("parallel", "parallel")),
        name="nsa_compress",
    )(rows4, w1.astype(BF16), w2p, pos.reshape(1, CMP_LEN * HEAD_DIM), kn)


def _softmax_tile_update(s, v_t, m, l, acc):
    m_new = jnp.maximum(m, jnp.max(s, axis=0, keepdims=True))
    alpha = jnp.exp2(m - m_new)
    p = jnp.exp2(s - m_new)
    l_new = alpha * l + jnp.sum(p, axis=0, keepdims=True)
    acc_new = alpha * acc + _dot(v_t, p.astype(BF16))
    return m_new, l_new, acc_new


def _pairs_to_rows(o_t):
    n = o_t.shape[1] // LANES
    outs = []
    for p in range(n // 2):
        a = o_t[:, (2 * p) * LANES:(2 * p + 1) * LANES]
        b = o_t[:, (2 * p + 1) * LANES:(2 * p + 2) * LANES]
        outs.append(jnp.concatenate([a, b], axis=0).T)
    return outs[0] if len(outs) == 1 else jnp.concatenate(outs, axis=1)


FAR_TILES = 4
NSA_SUB = 4
TAKEN = -2.0
NEG_ROWS = 8
CMP_TILE = 128
N_FORCED = 3
LOG2E = math.log2(math.e)
NSA_TINY = 2.0 ** -100
BOUND_SLACK = 1.05


def _nsa_kernel(q_ref, kc_ref, vc_ref, ks_ref, vs_ref, kw_ref, vw_ref, gate_ref,
                cbias_ref, sbias_ref, wbias_ref, ov_ref, msel_ref, o_ref, *scratch, nb):
    neg_refs, stage_refs = scratch[:NSA_SUB], scratch[NSA_SUB:]
    refs = (q_ref, kc_ref, vc_ref, ks_ref, vs_ref, kw_ref, vw_ref, gate_ref, cbias_ref, sbias_ref, wbias_ref,
            ov_ref, msel_ref)
    blocks = [_nsa_block(u, refs, neg_refs[u], stage_refs, nb) for u in range(NSA_SUB)]
    nc = kc_ref.shape[2]
    cmp_per_q = Q_BLOCK // CMP_STRIDE
    assert CMP_TILE % (cmp_per_q * NSA_SUB) == 0 and nc % CMP_TILE == 0
    variant = (pl.program_id(2) * NSA_SUB * cmp_per_q) // CMP_TILE
    cmps = lax.switch(variant, [lambda n=n: tuple(blk[4](n) for blk in blocks)
                                for n in range(CMP_TILE, nc + 1, CMP_TILE)])
    for blk, cmp in zip(blocks, cmps):
        blk[0](cmp)
    for phase in (1, 2):
        for blk in blocks:
            blk[phase]()
    rows = [blk[3]() for blk in blocks]
    for u in range(NSA_SUB):
        o_ref[0, u * Q_BLOCK:(u + 1) * Q_BLOCK, :] = rows[u].astype(o_ref.dtype)


def _nsa_block(u, refs, neg_ref, stage_refs, nb):
    (q_ref, kc_ref, vc_ref, ks_ref, vs_ref, kw_ref, vw_ref, gate_ref, cbias_ref, sbias_ref, wbias_ref,
     ov_ref, msel_ref) = refs
    i = pl.program_id(2) * NSA_SUB + u
    R = NSA_REP
    N = R * Q_BLOCK
    nc = kc_ref.shape[2]
    q_t = jnp.concatenate([q_ref[r, u] for r in range(R)], axis=1)
    n_win = NSA_WINDOW // Q_BLOCK

    def k_rows(ref, J0, n_tiles):
        return ref[0, pl.ds(pl.multiple_of(J0 * Q_BLOCK, Q_BLOCK), n_tiles * Q_BLOCK), :]

    def v_cols(ref, J0, n_tiles):
        return jnp.concatenate([ref[0, J0 + u] for u in range(n_tiles)], axis=1)

    def sel_mask(J0, n_tiles, n_valid=None):
        n_slc = neg_ref.shape[0] - NEG_ROWS
        parts = []
        for u in range(2 * n_tiles):
            row = 2 * J0 + u
            if n_valid is not None:
                row = jnp.where(u // 2 < n_valid, row, n_slc)
            parts.append(jnp.broadcast_to(neg_ref[pl.ds(row, 1), :], (SLC_LEN, Q_BLOCK)))
        return jnp.concatenate([jnp.concatenate(parts, axis=0)] * R, axis=1)

    def compressed(fast, n_keys):
        cb = cbias_ref[0, pl.ds(pl.multiple_of(8 * (nb - 1) - 8 * i, 8), n_keys), :]
        s = _dot(kc_ref[0, 0, :n_keys, :], q_t) + cb
        if fast:
            e = jnp.exp2(s)
        else:
            e = jnp.where(s > 0.5 * NEG_INF, jnp.exp2(s - jnp.max(s, axis=0, keepdims=True)), 0.0)
        l = jnp.sum(e, axis=0, keepdims=True)
        p = e * jnp.where(l > 0.0, 1.0 / l, 0.0)
        p_sum = p[:, 0:Q_BLOCK]
        for r in range(1, R):
            p_sum = p_sum + p[:, r * Q_BLOCK:(r + 1) * Q_BLOCK]
        p_hi = p_sum.astype(BF16)
        p_lo = (p_sum - p_hi.astype(F32)).astype(BF16)
        ov = ov_ref[:, :n_keys]
        imp = _dot(ov, p_hi) + _dot(ov, p_lo)
        return imp, l, _dot(vc_ref[0, 0, :, :n_keys], p.astype(BF16))

    def select_blocks(imp):
        n_slc = imp.shape[0]
        jj = lax.broadcasted_iota(jnp.int32, imp.shape, 0)
        qq = lax.broadcasted_iota(jnp.int32, imp.shape, 1)
        cur = 2 * i + jnp.where(qq >= SLC_LEN, 1, 0)
        forced = (jj == 0) | (jj == cur) | (jj == cur - 1)
        score = jnp.where(forced, TAKEN, jnp.where(jj <= cur, imp, -1.0))
        grp = 8
        j_rows = [jj[a:a + grp] for a in range(0, n_slc, grp)]
        for _ in range(min(SLC_TOPN, n_slc) - N_FORCED):
            cand = [(score[a:a + grp], j_rows[a // grp]) for a in range(0, n_slc, grp)]
            while len(cand) > 1:
                nxt = []
                for a in range(0, len(cand) - 1, 2):
                    (s0, j0), (s1, j1) = cand[a], cand[a + 1]
                    right = s1 > s0
                    nxt.append((jnp.maximum(s0, s1), jnp.where(right, j1, j0)))
                cand = nxt + cand[len(cand) - len(cand) % 2:]
            s8, j8 = cand[0]
            best = jnp.max(s8, axis=0, keepdims=True)
            first = jnp.min(jnp.where(s8 == best, j8, n_slc), axis=0, keepdims=True)
            score = jnp.where(jj == first, TAKEN, score)
        neg_ref[:n_slc] = jnp.where(score == TAKEN, msel_ref[0], NEG_INF)
        neg_ref[n_slc:] = jnp.full((NEG_ROWS, Q_BLOCK), NEG_INF, F32)

    def add_chunk(st, s, v_t):
        e = jnp.exp2(s)
        return st[0] + jnp.sum(e, axis=0, keepdims=True), st[1] + _dot(v_t, e.astype(BF16))

    zero = (jnp.zeros((1, N), F32), jnp.zeros((HEAD_DIM, N), F32))

    def selected_near():
        J0 = jnp.maximum(i - 1, 0)
        bias = sbias_ref[0, pl.ds(pl.multiple_of((2 - (i - J0)) * Q_BLOCK, Q_BLOCK), 2 * Q_BLOCK), :]
        return add_chunk(zero, _dot(k_rows(ks_ref, J0, 2), q_t) + bias + sel_mask(J0, 2), v_cols(vs_ref, J0, 2))

    def selected_far(st):
        n_far = jnp.maximum(i - 1, 0)
        n_chunks = (n_far + FAR_TILES - 1) // FAR_TILES

        def tile_of(c):
            return jnp.clip(FAR_TILES * c, 0, nb - FAR_TILES)

        def scores(c):
            return _dot(k_rows(ks_ref, tile_of(c), FAR_TILES), q_t)

        def values(c, e_ref, acc):
            return acc + _dot(v_cols(vs_ref, tile_of(c), FAR_TILES), e_ref[...])

        def stage(c, st, s_in, s_out, e_in, e_out):
            acc = values(c - 1, e_in, st[1])
            mask = sel_mask(tile_of(c), FAR_TILES, n_far - FAR_TILES * c)
            k_next = k_rows(ks_ref, tile_of(c + 1), FAR_TILES)
            s_out[...] = _dot(k_next, q_t)
            e = jnp.exp2(s_in[...] + mask)
            e_out[...] = e.astype(BF16)
            return st[0] + jnp.sum(e, axis=0, keepdims=True), acc

        s_a, s_b, e_a, e_b = stage_refs
        e_b[...] = jnp.zeros_like(e_b)
        s_a[...] = scores(0)

        def pair(c, st):
            st = stage(c, st, s_a, s_b, e_b, e_a)
            return stage(c + 1, st, s_b, s_a, e_a, e_b)

        def quad(c, st):
            return pair(c + 2, pair(c, st))

        n_octs = n_chunks // 8
        st = lax.fori_loop(0, n_octs, lambda t, st: quad(8 * t + 4, quad(8 * t, st)), st)
        done = 8 * n_octs
        st = lax.cond(n_chunks - done >= 4, lambda st: quad(done, st), lambda st: st, st)
        done = done + 4 * ((n_chunks - done) // 4)
        st = lax.cond(n_chunks - done >= 2, lambda st: pair(done, st), lambda st: st, st)

        def odd_tail(st):
            st = stage(n_chunks - 1, st, s_a, s_b, e_b, e_a)
            return st[0], values(n_chunks - 1, e_a, st[1])

        return lax.cond(n_chunks % 2 == 1, odd_tail, lambda st: (st[0], values(n_chunks - 1, e_b, st[1])), st)

    def window_fast():
        J0 = jnp.maximum(i - n_win, 0)
        bias = wbias_ref[0, pl.ds(pl.multiple_of((n_win - (i - J0)) * Q_BLOCK, Q_BLOCK), (n_win + 1) * Q_BLOCK), :]
        return add_chunk(zero, _dot(k_rows(kw_ref, J0, n_win + 1), q_t) + bias, v_cols(vw_ref, J0, n_win + 1))

    def sweep_exact(k_ref, v_ref, bias_ref, far_dist, n_back, with_sel):
        def body(t, st):
            J = i - t
            off = pl.multiple_of((far_dist - jnp.minimum(t, far_dist)) * Q_BLOCK, Q_BLOCK)
            s = _dot(k_rows(k_ref, J, 1), q_t) + bias_ref[0, pl.ds(off, Q_BLOCK), :]
            if with_sel:
                s = s + sel_mask(J, 1)
            return _softmax_tile_update(s, v_ref[0, J], *st)

        init = (jnp.full((1, N), NEG_INF, F32),) + zero
        _, l, acc = lax.fori_loop(0, jnp.minimum(i, n_back) + 1, body, init)
        return l, acc

    vals = {}

    def load_phase(cmp):
        vals["cmp"] = cmp
        vals["win"] = window_fast()

    def select_phase():
        select_blocks(vals["cmp"][0])

    def near_phase():
        vals["near"] = selected_near()

    def exact_path():
        imp_x, _, o_x = compressed(False, nc)
        select_blocks(imp_x)
        l_sx, a_sx = sweep_exact(ks_ref, vs_ref, sbias_ref, 2, nb, True)
        l_wx, a_wx = sweep_exact(kw_ref, vw_ref, wbias_ref, n_win, n_win, False)
        return o_x, a_sx * (1.0 / l_sx), a_wx * (1.0 / l_wx)

    def gate_row(br):
        return jnp.concatenate([gate_ref[br, 0, r:r + 1, u * Q_BLOCK:(u + 1) * Q_BLOCK] for r in range(R)], axis=1)

    def finish():
        _, l_c, o_c = vals["cmp"]
        l_w, a_w = vals["win"]
        l_s, a_s = selected_far(vals["near"])
        lane_q = lax.broadcasted_iota(jnp.int32, (1, N), 1) % Q_BLOCK
        has_cmp = (i > 0) | (lane_q >= CMP_LEN - 1)
        l_min = jnp.minimum(jnp.minimum(l_s, l_w), jnp.where(has_cmp, l_c, 1.0))
        o_cx, o_s, o_w = lax.cond(jnp.min(l_min) > NSA_TINY,
                                  lambda: (o_c, a_s * (1.0 / l_s), a_w * (1.0 / l_w)), exact_path)
        o_t = gate_row(0) * o_cx + gate_row(1) * o_s + gate_row(2) * o_w
        return _pairs_to_rows(o_t)

    return load_phase, select_phase, near_phase, finish, functools.partial(compressed, True)


def _nsa_attention(colmat, rowmat, cmp_rows, cmp_cols, gates, cbias, sbias, wbias, ov_t, msel, B, S):
    nb = S // Q_BLOCK
    nc = S // CMP_STRIDE
    n_slc = S // SLC_LEN
    G, R = NSA_KV_HEADS, NSA_REP
    kern = functools.partial(_nsa_kernel, nb=nb)
    ns = nb // NSA_SUB
    return pl.pallas_call(
        kern,
        grid=(B, G, ns),
        in_specs=[
            pl.BlockSpec((R, NSA_SUB, HEAD_DIM, LANES), lambda b, g, i: (g, b * ns + i, 0, 0)),
            pl.BlockSpec((1, 1, nc, HEAD_DIM), lambda b, g, i: (g, b, 0, 0)),
            pl.BlockSpec((1, 1, HEAD_DIM, nc), lambda b, g, i: (NSA_KV_HEADS + g, b, 0, 0)),
            pl.BlockSpec((1, S, HEAD_DIM), lambda b, g, i: (ROW_KS + g, b, 0)),
            pl.BlockSpec((1, nb, HEAD_DIM, LANES), lambda b, g, i: (COL_VS + g, b, 0, 0)),
            pl.BlockSpec((1, S, HEAD_DIM), lambda b, g, i: (ROW_KW + g, b, 0)),
            pl.BlockSpec((1, nb, HEAD_DIM, LANES), lambda b, g, i: (COL_VW + g, b, 0, 0)),
            pl.BlockSpec((N_BRANCHES, 1, R, NSA_SUB * LANES), lambda b, g, i: (0, g, 0, b * ns + i)),
            pl.BlockSpec((1, cbias.shape[1], R * Q_BLOCK), lambda b, g, i: (g, 0, 0)),
            pl.BlockSpec((1, sbias.shape[1], R * Q_BLOCK), lambda b, g, i: (g, 0, 0)),
            pl.BlockSpec((1, wbias.shape[1], R * Q_BLOCK), lambda b, g, i: (g, 0, 0)),
            pl.BlockSpec((n_slc, nc), lambda b, g, i: (0, 0)),
            pl.BlockSpec((1, 1, Q_BLOCK), lambda b, g, i: (g, 0, 0)),
        ],
        out_specs=pl.BlockSpec((1, NSA_SUB * Q_BLOCK, R * HEAD_DIM), lambda b, g, i: (b, i, g)),
        out_shape=jax.ShapeDtypeStruct((B, S, NSA_HEADS * HEAD_DIM), BF16),
        scratch_shapes=[pltpu.VMEM((n_slc + NEG_ROWS, Q_BLOCK), F32)] * NSA_SUB
        + [pltpu.VMEM((FAR_TILES * Q_BLOCK, R * Q_BLOCK), F32)] * 2
        + [pltpu.VMEM((FAR_TILES * Q_BLOCK, R * Q_BLOCK), BF16)] * 2,
        compiler_params=_cparams(("parallel", "parallel", "arbitrary")),
        name="nsa_attention",
    )(colmat, cmp_rows, cmp_cols, rowmat, colmat, rowmat, colmat, gates, cbias, sbias, wbias, ov_t, msel)


SB_QB = 512
SB_KT = 256
SB_ZERO_LOG = -160.0


def _sb_kernel(q_ref, k_ref, v_ref, tri_ref, o_ref):
    i = pl.program_id(1)
    H = SB_HEADS
    tri = tri_ref[...]
    assert SB_QB == 2 * SB_KT
    below_diag = (lax.broadcasted_iota(jnp.int32, (SB_KT, SB_KT), 1)
                  < lax.broadcasted_iota(jnp.int32, (SB_KT, SB_KT), 0))

    def step(J, rows, st, masked):
        out = []
        for h in range(H):
            carry, acc = st[h]
            k_t = jnp.concatenate([k_ref[h, (SB_KT // LANES) * J + c] for c in range(SB_KT // LANES)], axis=1)
            z = _dot(q_ref[h, rows, :], k_t)
            sp = jnp.maximum(z, 0.0) + jnp.log2(1.0 + jnp.exp2(-jnp.abs(z)))
            lf = jnp.where(below_diag, -sp, 0.0) if masked else -sp
            lf_hi = lf.astype(BF16)
            lf_lo = (lf - lf_hi.astype(F32)).astype(BF16)
            after = _dot(lf_hi, tri) + _dot(lf_lo, tri) + carry
            w = jnp.exp2(z - sp + after)
            if masked:
                w = jnp.where(below_diag, w, 0.0)
            v = v_ref[h, pl.ds(pl.multiple_of(J * SB_KT, SB_KT), SB_KT), :]
            acc = acc + _dot(w.astype(BF16), v)
            carry = carry + jnp.sum(lf, axis=1, keepdims=True)
            out.append((carry, acc))
        return tuple(out)

    lo, hi = slice(0, SB_KT), slice(SB_KT, SB_QB)
    zero = tuple((jnp.zeros((SB_KT, 1), F32), jnp.zeros((SB_KT, HEAD_DIM), F32)) for _ in range(H))
    st_hi = step(2 * i, hi, step(2 * i + 1, hi, zero, True), False)
    st_lo = step(2 * i, lo, zero, True)

    def alive(st):
        worst = st[0][0]
        for h in range(1, H):
            worst = jnp.maximum(worst, st[h][0])
        return jnp.max(worst) > SB_ZERO_LOG

    def sweep(rows, st):
        def body(c):
            J, _, st = c
            st = step(J, rows, st, False)
            return J - 1, alive(st), st

        return lax.while_loop(lambda c: (c[0] >= 0) & c[1], body, (2 * i - 1, alive(st), st))[2]

    st_lo, st_hi = sweep(lo, st_lo), sweep(hi, st_hi)
    o_ref[0, lo, :] = jnp.concatenate([st_lo[h][1] for h in range(H)], axis=1).astype(o_ref.dtype)
    o_ref[0, hi, :] = jnp.concatenate([st_hi[h][1] for h in range(H)], axis=1).astype(o_ref.dtype)


def _sb_attention(colmat, rowmat, tri, B, S):
    nq = S // SB_QB
    nb = S // LANES
    H = SB_HEADS
    return pl.pallas_call(
        _sb_kernel,
        grid=(B, nq),
        in_specs=[
            pl.BlockSpec((H, SB_QB, HEAD_DIM), lambda b, i: (ROW_SQ // H, b * nq + i, 0)),
            pl.BlockSpec((H, nb, HEAD_DIM, LANES), lambda b, i: (COL_SK // H, b, 0, 0)),
            pl.BlockSpec((H, S, HEAD_DIM), lambda b, i: (ROW_SV // H, b, 0)),
            pl.BlockSpec((SB_KT, SB_KT), lambda b, i: (0, 0)),
        ],
        out_specs=pl.BlockSpec((1, SB_QB, H * HEAD_DIM), lambda b, i: (b, i, 0)),
        out_shape=jax.ShapeDtypeStruct((B, S, H * HEAD_DIM), BF16),
        compiler_params=_cparams(("parallel", "arbitrary")),
        name="sb_attention",
    )(rowmat, colmat, rowmat, tri)


SWA_SUB = 4


def _swa_kernel(q_ref, k_ref, v_ref, bias_ref, sink_ref, o_ref):
    R = SWA_REP
    sink = sink_ref[0]
    outs = []
    for u in range(SWA_SUB):
        i = pl.program_id(2) * SWA_SUB + u
        q_t = jnp.concatenate([q_ref[r, u] for r in range(R)], axis=1)
        J0 = jnp.maximum(i - 1, 0)
        k = k_ref[0, pl.ds(pl.multiple_of(J0 * Q_BLOCK, Q_BLOCK), 2 * Q_BLOCK), :]
        bias = bias_ref[0, pl.ds(pl.multiple_of((1 - (i - J0)) * Q_BLOCK, Q_BLOCK), 2 * Q_BLOCK), :]
        s = _dot(k, q_t) + bias
        m = jnp.maximum(jnp.max(s, axis=0, keepdims=True), sink)
        e = jnp.exp(s - m)
        den = jnp.sum(e, axis=0, keepdims=True) + jnp.exp(sink - m)
        v_t = jnp.concatenate([v_ref[0, J0], v_ref[0, J0 + 1]], axis=1)
        o_t = _dot(v_t, e.astype(BF16)) * (1.0 / den)
        outs.append(_pairs_to_rows(o_t).astype(o_ref.dtype))
    for u in range(SWA_SUB):
        o_ref[0, u * Q_BLOCK:(u + 1) * Q_BLOCK, :] = outs[u]


def _swa_attention(colmat, rowmat, bias, sinks, B, S):
    nb = S // Q_BLOCK
    ns = nb // SWA_SUB
    G, R = SWA_KV_HEADS, SWA_REP
    return pl.pallas_call(
        _swa_kernel,
        grid=(B, G, ns),
        in_specs=[
            pl.BlockSpec((R, SWA_SUB, HEAD_DIM, LANES), lambda b, g, i: (COL_WQ // R + g, b * ns + i, 0, 0)),
            pl.BlockSpec((1, S, HEAD_DIM), lambda b, g, i: (ROW_WK + g, b, 0)),
            pl.BlockSpec((1, nb, HEAD_DIM, LANES), lambda b, g, i: (COL_WV + g, b, 0, 0)),
            pl.BlockSpec((1, 3 * Q_BLOCK, R * Q_BLOCK), lambda b, g, i: (g, 0, 0)),
            pl.BlockSpec((1, 1, R * Q_BLOCK), lambda b, g, i: (g, 0, 0)),
        ],
        out_specs=pl.BlockSpec((1, SWA_SUB * Q_BLOCK, R * HEAD_DIM), lambda b, g, i: (b, i, g)),
        out_shape=jax.ShapeDtypeStruct((B, S, SWA_HEADS * HEAD_DIM), BF16),
        compiler_params=_cparams(("parallel", "parallel", "arbitrary")),
        name="swa_attention",
    )(colmat, rowmat, colmat, bias, sinks)


MERGE_TM = 512


def _merge_kernel(x_ref, nw_ref, wg_ref, ya_ref, yb_ref, yc_ref, ua_ref, ub_ref, uc_ref, wo_ref, o_ref):
    x = x_ref[...]
    h = _rms_rows(x, nw_ref[...]).astype(BF16)
    merged = None
    for br, (y_ref, u_ref) in enumerate(((ya_ref, ua_ref), (yb_ref, ub_ref), (yc_ref, uc_ref))):
        g = jax.nn.sigmoid(_dot(h, wg_ref[:, br * D_MODEL:(br + 1) * D_MODEL]))
        t = g * _dot(y_ref[...], u_ref[...])
        merged = t if merged is None else merged + t
    o_ref[...] = x + _dot(merged.astype(BF16), wo_ref[...])


def _merge(x2, nw, w_bgate, y_nsa, y_sb, y_swa, u_nsa, u_sb, u_swa, w_out, l):
    T = x2.shape[0]
    full = lambda w: _layer_spec(l, w.shape[1:])
    rows = lambda n: pl.BlockSpec((MERGE_TM, n), lambda i: (i, 0))
    return pl.pallas_call(
        _merge_kernel,
        grid=(T // MERGE_TM,),
        in_specs=[rows(D_MODEL), pl.BlockSpec((1, D_MODEL), lambda i: (0, 0)), full(w_bgate),
                  rows(y_nsa.shape[1]), rows(y_sb.shape[1]), rows(y_swa.shape[1]),
                  full(u_nsa), full(u_sb), full(u_swa), full(w_out)],
        out_specs=rows(D_MODEL),
        out_shape=jax.ShapeDtypeStruct((T, D_MODEL), F32),
        compiler_params=_cparams(("parallel",)),
        name="merge",
    )(x2, nw.reshape(1, D_MODEL), w_bgate, y_nsa, y_sb, y_swa, u_nsa, u_sb, u_swa, w_out)


def _t5_bucket(dist):
    max_exact = NUM_BUCKETS // 2
    d = jnp.maximum(dist, 0)
    df = jnp.maximum(d, 1).astype(F32)
    large = max_exact + (jnp.log(df / max_exact) / math.log(MAX_DISTANCE / max_exact)
                         * (NUM_BUCKETS - max_exact)).astype(jnp.int32)
    large = jnp.minimum(large, NUM_BUCKETS - 1)
    return jnp.where(d < max_exact, d, large)


def _toeplitz(ext, delta):
    H = ext.shape[0]
    n = Q_BLOCK
    c = delta + MAX_DISTANCE
    w = ext[:, c - (n - 1):c + n]
    flat = jnp.pad(jnp.broadcast_to(w[:, None, :], (H, n, 2 * n - 1)), ((0, 0), (0, 0), (0, 1))).reshape(H, 2 * n * n)
    return flat[:, n - 1:n - 1 + n * (2 * n - 1)].reshape(H, n, 2 * n - 1)[:, :, :n]


def _lanes(t, G, R):
    K, Q = t.shape[1:]
    return t.reshape(G, R, K, Q).transpose(0, 2, 1, 3).reshape(G, K, R * Q)


def _bias_tables(rel_bias, nb, qk_bound):
    n = Q_BLOCK
    G, R = NSA_KV_HEADS, NSA_REP
    by_dist = rel_bias[_t5_bucket(jnp.arange(MAX_DISTANCE)), :].T.astype(F32)
    far = rel_bias[NUM_BUCKETS - 1, :].astype(F32)
    neg = jnp.full((1, n), NEG_INF, F32)
    k = jnp.arange(n)[:, None]
    q = jnp.arange(n)[None, :]
    above = (k > q)[None]

    rel = (by_dist[:NSA_HEADS] - far[:NSA_HEADS, None]) * LOG2E
    m_g = qk_bound + jnp.maximum(jnp.max(rel.reshape(G, -1), axis=1), 0.0)
    m_h = jnp.repeat(m_g, R)[:, None, None]
    ext = jnp.concatenate([jnp.broadcast_to(neg, (NSA_HEADS, n)), rel, jnp.zeros((NSA_HEADS, n), F32)], axis=1)
    t0, t1 = _toeplitz(ext, 0), _toeplitz(ext, n)
    zeros = jnp.zeros_like(t0)
    masked = jnp.full_like(t0, NEG_INF)
    sbias = _lanes(jnp.concatenate([zeros, t1, t0, masked], axis=1), G, R)
    n_win = NSA_WINDOW // n
    edge = jnp.where(above, zeros, NEG_INF)
    tiles = [edge] + [zeros] * (n_win - 2) + [t1, t0]
    wbias = _lanes(jnp.concatenate([t - m_h for t in tiles] + [masked] * n_win, axis=1), G, R)
    off = 8 * (nb - 1)
    near = jnp.stack([ext[:, 97 - CMP_STRIDE * m:97 - CMP_STRIDE * m + n] for m in range(-9, 7)], axis=1)
    n_rows = 16 * nb - 8
    cb = jnp.concatenate([jnp.zeros((NSA_HEADS, off - 9, n), F32), near,
                          jnp.full((NSA_HEADS, n_rows - off - 7, n), NEG_INF, F32)], axis=1) - m_h
    cbias = _lanes(cb, G, R)
    msel = jnp.broadcast_to(-m_g[:, None, None], (G, 1, n))

    hs = slice(NSA_HEADS, NSA_HEADS + SWA_HEADS)
    ext_w = jnp.concatenate([jnp.broadcast_to(neg, (SWA_HEADS, n)), by_dist[hs],
                             jnp.broadcast_to(far[hs, None], (SWA_HEADS, n))], axis=1)
    s0 = _toeplitz(ext_w, 0)
    s1 = jnp.where(above, _toeplitz(ext_w, n), NEG_INF)
    swa_bias = _lanes(jnp.concatenate([s1, s0, jnp.full_like(s0, NEG_INF)], axis=1), SWA_KV_HEADS, SWA_REP)
    return cbias, sbias, wbias, msel, swa_bias


def _overlap_t(S):
    nc, n_slc = S // CMP_STRIDE, S // SLC_LEN
    n = np.arange(nc)[None, :]
    j = np.arange(n_slc)[:, None]
    ov = (n * CMP_STRIDE < (j + 1) * SLC_LEN) & (n * CMP_STRIDE + CMP_LEN - 1 >= j * SLC_LEN) & (n < nc - 1)
    return jnp.asarray(ov, BF16)


def _head_gains(heads, norm_of, scale_of):
    ones = jnp.ones((HEAD_DIM,), F32)
    gains = [norm_of[name].astype(F32) * scale_of.get(name, 1.0) if name in norm_of else ones for name, _ in heads]
    return jnp.concatenate(gains).reshape(1, -1)


def kernel(x, rel_bias, ffn1_norm, ffn1_w_gate, ffn1_w_up, ffn1_w_down, mix_norm, w_in, nsa_q_norm, nsa_k_norm, nsa_cmp_pos, nsa_cmp_k_w1, nsa_cmp_k_w2, nsa_cmp_v_w1, nsa_cmp_v_w2, swa_q_norm, swa_k_norm, swa_sinks, w_up_nsa, w_up_sb, w_up_swa, w_out, ffn2_norm, ffn2_w_gate, ffn2_w_up, ffn2_w_down):
    B, S, D = x.shape
    T = B * S
    nb = S // Q_BLOCK
    nc = S // CMP_STRIDE
    depth = w_in.shape[0]
    scale = HEAD_DIM ** -0.5
    scale_of = {"nq": scale * LOG2E, "sq": scale * LOG2E, "wq": scale}
    qk_bound = (BOUND_SLACK * LOG2E * HEAD_DIM ** 0.5
                * jnp.max(jnp.abs(nsa_q_norm.astype(F32))) * jnp.max(jnp.abs(nsa_k_norm.astype(F32))))
    cbias, sbias, wbias, msel, swa_bias = _bias_tables(rel_bias, nb, qk_bound)
    plan = _proj_plan(scale_of)
    ov_t = _overlap_t(S)
    tri = jnp.asarray(np.tril(np.ones((SB_KT, SB_KT), np.float32), -1), BF16)

    (ffn1_w_gate, ffn1_w_up, ffn1_w_down, ffn2_w_gate, ffn2_w_up, ffn2_w_down, w_up_nsa, w_up_sb, w_up_swa,
     w_out) = (w.astype(BF16) for w in (ffn1_w_gate, ffn1_w_up, ffn1_w_down, ffn2_w_gate, ffn2_w_up, ffn2_w_down,
                                        w_up_nsa, w_up_sb, w_up_swa, w_out))
    tail0 = OFF_NGATE + N_GATE
    w_head, w_tail = w_in[:, :, :OFF_NGATE].astype(BF16), w_in[:, :, tail0:].astype(BF16)
    w_bgate = w_tail[:, :, OFF_BGATE - tail0:]
    w_gate = jnp.pad(w_in[:, :, OFF_NGATE:tail0], ((0, 0), (0, 0), (0, LANES - N_GATE))).astype(BF16)
    runs = []
    for _, off in ROW_HEADS + COL_HEADS:
        if runs and runs[-1][1] == off:
            runs[-1][1] = off + HEAD_DIM
        else:
            runs.append([off, off + HEAD_DIM])
    pieces = [w_head[:, :, a:b] if a < OFF_NGATE else w_tail[:, :, a - tail0:b - tail0] for a, b in runs]
    w_proj = jnp.concatenate(pieces + [w_gate], axis=2)

    x2 = x.reshape(T, D)
    for l in range(depth):
        x2 = _ffn(x2, ffn1_norm[l], ffn1_w_gate, ffn1_w_up, ffn1_w_down, l)

        norm_of = {"nq": nsa_q_norm[l], "nks": nsa_k_norm[l], "nkw": nsa_k_norm[l],
                   "wq": swa_q_norm[l], "wk": swa_k_norm[l]}
        gain = jnp.concatenate([_head_gains(ROW_HEADS, norm_of, scale_of), _head_gains(COL_HEADS, norm_of, scale_of),
                                jnp.ones((1, LANES), F32)], axis=1)
        rowmat, colmat, gates = _proj(x2, mix_norm[l], w_proj, gain, plan, l)
        gates = gates[:N_GATE].reshape(N_BRANCHES, NSA_KV_HEADS, NSA_REP, T)

        rows4 = rowmat[ROW_KC:ROW_KC + 2 * NSA_KV_HEADS].reshape(2 * NSA_KV_HEADS, B, nc, CMP_STRIDE * HEAD_DIM)
        cmp_rows, cmp_cols = _compress(rows4, jnp.stack([nsa_cmp_k_w1[l], nsa_cmp_v_w1[l]]),
                                       jnp.stack([nsa_cmp_k_w2[l], nsa_cmp_v_w2[l]]),
                                       nsa_cmp_pos[l], nsa_k_norm[l])

        y_nsa = _nsa_attention(colmat, rowmat, cmp_rows, cmp_cols, gates, cbias, sbias, wbias, ov_t, msel, B, S)
        y_sb = _sb_attention(colmat, rowmat, tri, B, S)
        sinks = jnp.repeat(swa_sinks[l].astype(F32).reshape(SWA_KV_HEADS, 1, SWA_REP), Q_BLOCK, axis=2)
        y_swa = _swa_attention(colmat, rowmat, swa_bias, sinks, B, S)

        x2 = _merge(x2, mix_norm[l], w_bgate, y_nsa.reshape(T, -1), y_sb.reshape(T, -1),
                    y_swa.reshape(T, -1), w_up_nsa, w_up_sb, w_up_swa, w_out, l)
        x2 = _ffn(x2, ffn2_norm[l], ffn2_w_gate, ffn2_w_up, ffn2_w_down, l)
    return x2.reshape(B, S, D)
```

```python
import functools
import math

import numpy as np
import jax
import jax.numpy as jnp
from jax import lax
from jax.experimental import pallas as pl
from jax.experimental.pallas import tpu as pltpu

D_MODEL = 1024
HEAD_DIM = 64
Q_BLOCK = 128
NSA_HEADS = 8
NSA_KV_HEADS = 2
NSA_REP = NSA_HEADS // NSA_KV_HEADS
CMP_LEN = 32
CMP_STRIDE = 16
CMP_HIDDEN = 256
SLC_LEN = 64
SLC_TOPN = 16
NSA_WINDOW = 512
SB_HEADS = 4
SWA_HEADS = 4
SWA_KV_HEADS = 2
SWA_REP = SWA_HEADS // SWA_KV_HEADS
SWA_WINDOW = 128
NUM_BUCKETS = 32
MAX_DISTANCE = 128
D_FF = 2816
NORM_EPS = 1e-6
NEG_INF = -1e30
N_BRANCHES = 3
N_GATE = 3 * NSA_HEADS

LANES = 128
VMEM_LIMIT = 56 * 1024 * 1024
BF16 = jnp.bfloat16
F32 = jnp.float32

_sizes = (NSA_HEADS * HEAD_DIM,) + (NSA_KV_HEADS * HEAD_DIM,) * 6 + (N_GATE,) \
    + (SB_HEADS * HEAD_DIM,) * 3 + (SWA_HEADS * HEAD_DIM, SWA_KV_HEADS * HEAD_DIM,
                                    SWA_KV_HEADS * HEAD_DIM, N_BRANCHES * D_MODEL)
_offs = np.concatenate([[0], np.cumsum(_sizes)])
(OFF_NQ, OFF_NKC, OFF_NVC, OFF_NKS, OFF_NVS, OFF_NKW, OFF_NVW, OFF_NGATE,
 OFF_SQ, OFF_SK, OFF_SV, OFF_WQ, OFF_WK, OFF_WV, OFF_BGATE) = (int(o) for o in _offs[:-1])

ROW_HEADS = ([("nkc", OFF_NKC + 64 * g) for g in range(2)] + [("nvc", OFF_NVC + 64 * g) for g in range(2)]
             + [("nks", OFF_NKS + 64 * g) for g in range(2)] + [("nkw", OFF_NKW + 64 * g) for g in range(2)]
             + [("sq", OFF_SQ + 64 * h) for h in range(4)] + [("sv", OFF_SV + 64 * h) for h in range(4)]
             + [("wk", OFF_WK + 64 * g) for g in range(2)])
ROW_KC, ROW_VC, ROW_KS, ROW_KW, ROW_SQ, ROW_SV, ROW_WK = 0, 2, 4, 6, 8, 12, 16
COL_HEADS = ([("nq", OFF_NQ + 64 * h) for h in range(8)] + [("nvs", OFF_NVS + 64 * g) for g in range(2)]
             + [("nvw", OFF_NVW + 64 * g) for g in range(2)] + [("sk", OFF_SK + 64 * h) for h in range(4)]
             + [("wq", OFF_WQ + 64 * h) for h in range(4)] + [("wv", OFF_WV + 64 * g) for g in range(2)])
COL_NQ, COL_VS, COL_VW, COL_SK, COL_WQ, COL_WV = 0, 8, 10, 12, 16, 20


def _dot(a, b):
    return jnp.dot(a, b, preferred_element_type=F32)


def _rms_rows(x, w):
    ms = jnp.mean(x * x, axis=-1, keepdims=True)
    return x * lax.rsqrt(ms + NORM_EPS) * w


def _cparams(sem, flags=None):
    return pltpu.CompilerParams(dimension_semantics=sem, vmem_limit_bytes=VMEM_LIMIT, flags=flags)


FFN_TM = 512
MXU_TILE = 256
FFN_FC = MXU_TILE


def _ffn_kernel(x_ref, nw_ref, wg_ref, wu_ref, wd_ref, o_ref):
    x = x_ref[...]
    h = _rms_rows(x, nw_ref[...]).astype(BF16)
    acc = jnp.zeros(x.shape, F32)
    for c in range(D_FF // FFN_FC):
        cols = slice(c * FFN_FC, (c + 1) * FFN_FC)
        g = _dot(h, wg_ref[:, cols])
        u = _dot(h, wu_ref[:, cols])
        a = (g * jax.nn.sigmoid(g)) * u
        acc = acc + _dot(a.astype(BF16), wd_ref[cols, :])
    o_ref[...] = x + 0.5 * acc


def _layer_spec(l, shape, **kw):
    return pl.BlockSpec((None,) + tuple(shape), lambda i: (l, 0, 0), **kw)


def _ffn(x2, nw, wg, wu, wd, l):
    T = x2.shape[0]
    resident = lambda shape: _layer_spec(l, shape, pipeline_mode=pl.Buffered(1))
    return pl.pallas_call(
        _ffn_kernel,
        grid=(T // FFN_TM,),
        in_specs=[
            pl.BlockSpec((FFN_TM, D_MODEL), lambda i: (i, 0)),
            pl.BlockSpec((1, D_MODEL), lambda i: (0, 0)),
            resident((D_MODEL, D_FF)), resident((D_MODEL, D_FF)), resident((D_FF, D_MODEL)),
        ],
        out_specs=pl.BlockSpec((FFN_TM, D_MODEL), lambda i: (i, 0)),
        out_shape=jax.ShapeDtypeStruct((T, D_MODEL), F32),
        compiler_params=_cparams(("parallel",)),
        name="ffn",
    )(x2, nw.reshape(1, D_MODEL), wg, wu, wd)


PROJ_TM = 512


GATE_ROWS = 32

_NORMED = ("nq", "nks", "nkw", "wq", "wk")


def _proj_plan(scale_of):
    plan = []
    for kind, heads in (("row", ROW_HEADS), ("col", COL_HEADS)):
        for p in range(len(heads) // 2):
            name = heads[2 * p][0]
            assert heads[2 * p + 1][0] == name
            plan.append((kind, p, name in _NORMED, 1.0 if name in _NORMED else scale_of.get(name, 1.0)))
    plan.append(("gate", 0, False, 1.0))
    return tuple(plan)


def _proj_kernel(x_ref, nw_ref, w_ref, gain_ref, bd_ref, row_ref, col_ref, gate_ref, *, plan):
    h = _rms_rows(x_ref[...], nw_ref[...]).astype(BF16)
    pair = 2 * MXU_TILE
    for c0 in range(0, len(plan) * LANES, pair):
        width = min(pair, len(plan) * LANES - c0)
        zg = _dot(h, w_ref[:, c0:c0 + width])
        for u in range(width // LANES):
            t = c0 // LANES + u
            kind, p, normed, scale = plan[t]
            z = zg[:, u * LANES:(u + 1) * LANES]
            if normed:
                ms = _dot((z * z).astype(BF16), bd_ref[...])
                z = z * lax.rsqrt(ms + NORM_EPS) * gain_ref[:, t * LANES:(t + 1) * LANES]
            elif scale != 1.0:
                z = z * scale
            if kind == "row":
                row_ref[2 * p] = z[:, :HEAD_DIM].astype(row_ref.dtype)
                row_ref[2 * p + 1] = z[:, HEAD_DIM:].astype(row_ref.dtype)
            elif kind == "col":
                z_t = z.T
                for c in range(PROJ_TM // LANES):
                    col_ref[2 * p, c] = z_t[:HEAD_DIM, c * LANES:(c + 1) * LANES].astype(col_ref.dtype)
                    col_ref[2 * p + 1, c] = z_t[HEAD_DIM:, c * LANES:(c + 1) * LANES].astype(col_ref.dtype)
            else:
                gate_ref[...] = jax.nn.sigmoid(z).T[:GATE_ROWS]


def _proj(x2, nw, w, gain, plan, l):
    T = x2.shape[0]
    n_row, n_col = len(ROW_HEADS), len(COL_HEADS)
    bd = np.kron(np.eye(2, dtype=np.float32), np.full((HEAD_DIM, HEAD_DIM), 1.0 / HEAD_DIM, np.float32))
    return pl.pallas_call(
        functools.partial(_proj_kernel, plan=plan),
        grid=(T // PROJ_TM,),
        in_specs=[
            pl.BlockSpec((PROJ_TM, D_MODEL), lambda i: (i, 0)),
            pl.BlockSpec((1, D_MODEL), lambda i: (0, 0)),
            _layer_spec(l, w.shape[1:]),
            pl.BlockSpec(gain.shape, lambda i: (0, 0)),
            pl.BlockSpec((LANES, LANES), lambda i: (0, 0)),
        ],
        out_specs=[
            pl.BlockSpec((n_row, PROJ_TM, HEAD_DIM), lambda i: (0, i, 0)),
            pl.BlockSpec((n_col, PROJ_TM // LANES, HEAD_DIM, LANES), lambda i: (0, i, 0, 0)),
            pl.BlockSpec((GATE_ROWS, PROJ_TM), lambda i: (0, i)),
        ],
        out_shape=[jax.ShapeDtypeStruct((n_row, T, HEAD_DIM), BF16),
                   jax.ShapeDtypeStruct((n_col, T // LANES, HEAD_DIM, LANES), BF16),
                   jax.ShapeDtypeStruct((GATE_ROWS, T), F32)],
        compiler_params=_cparams(("parallel",)),
        name="proj",
    )(x2, nw.reshape(1, D_MODEL), w, gain, jnp.asarray(bd, BF16))


def _compress_kernel(r_ref, w1_ref, w2_ref, pos_ref, kn_ref, row_ref, col_ref):
    kind = pl.program_id(0) // NSA_KV_HEADS
    r = r_ref[0, 0]
    nc = r.shape[0]
    half = CMP_STRIDE * HEAD_DIM
    w1 = w1_ref[0]
    p_lo = _dot(r, w1[:half])
    p_hi = _dot(r, w1[half:])
    p_pos = _dot(jnp.broadcast_to(pos_ref[...], (8, CMP_LEN * HEAD_DIM)).astype(BF16), w1)[:1]
    hdn = p_lo + pltpu.roll(p_hi, nc - 1, 0) + p_pos
    hdn = hdn * jax.nn.sigmoid(hdn)
    out = _dot(hdn.astype(BF16), w2_ref[0])
    ms = jnp.sum(out * out, axis=-1, keepdims=True) * (1.0 / HEAD_DIM)
    normed = out * lax.rsqrt(ms + NORM_EPS) * kn_ref[...]
    out = jnp.where(kind == 0, normed, out)
    row_ref[0, 0] = out[:, :HEAD_DIM].astype(row_ref.dtype)
    col_ref[0, 0] = out.T[:HEAD_DIM].astype(col_ref.dtype)


def _compress(rows4, w1, w2, pos, k_norm):
    assert ROW_KC == 0 and ROW_VC == NSA_KV_HEADS
    _, B, NC, _ = rows4.shape
    kn = jnp.pad(k_norm.reshape(1, HEAD_DIM), ((0, 0), (0, LANES - HEAD_DIM)))
    w2p = jnp.pad(w2, ((0, 0), (0, 0), (0, LANES - HEAD_DIM))).astype(BF16)
    return pl.pallas_call(
        _compress_kernel,
        grid=(4, B),
        in_specs=[
            pl.BlockSpec((1, 1, NC, CMP_STRIDE * HEAD_DIM), lambda h, b: (h, b, 0, 0)),
            pl.BlockSpec((1, CMP_LEN * HEAD_DIM, CMP_HIDDEN), lambda h, b: (h // NSA_KV_HEADS, 0, 0)),
            pl.BlockSpec((1, CMP_HIDDEN, LANES), lambda h, b: (h // NSA_KV_HEADS, 0, 0)),
            pl.BlockSpec((1, CMP_LEN * HEAD_DIM), lambda h, b: (0, 0)),
            pl.BlockSpec((1, LANES), lambda h, b: (0, 0)),
        ],
        out_specs=[
            pl.BlockSpec((1, 1, NC, HEAD_DIM), lambda h, b: (h, b, 0, 0)),
            pl.BlockSpec((1, 1, HEAD_DIM, NC), lambda h, b: (h, b, 0, 0)),
        ],
        out_shape=[jax.ShapeDtypeStruct((4, B, NC, HEAD_DIM), BF16),
                   jax.ShapeDtypeStruct((4, B, HEAD_DIM, NC), BF16)],
        compiler_params=_cparams(("parallel", "parallel")),
        name="nsa_compress",
    )(rows4, w1.astype(BF16), w2p, pos.reshape(1, CMP_LEN * HEAD_DIM), kn)


def _softmax_tile_update(s, v_t, m, l, acc):
    m_new = jnp.maximum(m, jnp.max(s, axis=0, keepdims=True))
    alpha = jnp.exp2(m - m_new)
    p = jnp.exp2(s - m_new)
    l_new = alpha * l + jnp.sum(p, axis=0, keepdims=True)
    acc_new = alpha * acc + _dot(v_t, p.astype(BF16))
    return m_new, l_new, acc_new


def _pairs_to_rows(o_t):
    n = o_t.shape[1] // LANES
    outs = []
    for p in range(n // 2):
        a = o_t[:, (2 * p) * LANES:(2 * p + 1) * LANES]
        b = o_t[:, (2 * p + 1) * LANES:(2 * p + 2) * LANES]
        outs.append(jnp.concatenate([a, b], axis=0).T)
    return outs[0] if len(outs) == 1 else jnp.concatenate(outs, axis=1)


FAR_TILES = 4
NSA_SUB = 4
TAKEN = -2.0
NEG_ROWS = 8
N_FORCED = 3
LOG2E = math.log2(math.e)
NSA_TINY = 2.0 ** -100
BOUND_SLACK = 1.05


def _nsa_kernel(q_ref, kc_ref, vc_ref, ks_ref, vs_ref, kw_ref, vw_ref, gate_ref,
                cbias_ref, sbias_ref, wbias_ref, ov_ref, msel_ref, o_ref,
                neg_all, s_a, s_b, e_a, e_b, acc_ref, l_ref, *, nb):
    refs = (q_ref, kc_ref, vc_ref, ks_ref, vs_ref, kw_ref, vw_ref, gate_ref, cbias_ref, sbias_ref, wbias_ref,
            ov_ref, msel_ref)
    blocks = [_nsa_block(u, refs, neg_all.at[u], acc_ref, l_ref, nb) for u in range(NSA_SUB)]
    for phase in range(3):
        for blk in blocks:
            blk[phase]()
    _far_sweep(q_ref, ks_ref, vs_ref, neg_all, (s_a, s_b, e_a, e_b), acc_ref, l_ref, nb)
    rows = [blk[3]() for blk in blocks]
    for u in range(NSA_SUB):
        o_ref[0, u * Q_BLOCK:(u + 1) * Q_BLOCK, :] = rows[u].astype(o_ref.dtype)


def _far_sweep(q_ref, ks_ref, vs_ref, neg_all, stage_refs, acc_ref, l_ref, nb):
    R = NSA_REP
    n_slc = neg_all.shape[1] - NEG_ROWS
    i0 = pl.program_id(2) * NSA_SUB
    n_far = [jnp.maximum(i0 + u - 1, 0) for u in range(NSA_SUB)]
    offs = [0]
    for u in range(NSA_SUB):
        offs.append(offs[-1] + (n_far[u] + FAR_TILES - 1) // FAR_TILES)
    n_tot = offs[-1]

    def locate(c):
        u = sum(jnp.where(c >= offs[k], 1, 0) for k in range(1, NSA_SUB))
        start = sum(jnp.where(u == k, offs[k], 0) for k in range(NSA_SUB))
        far = sum(jnp.where(u == k, n_far[k], 0) for k in range(NSA_SUB))
        tile = FAR_TILES * (c - start)
        return u, jnp.clip(tile, 0, nb - FAR_TILES), far - tile

    def k_rows(J0):
        return ks_ref[0, pl.ds(pl.multiple_of(J0 * Q_BLOCK, Q_BLOCK), FAR_TILES * Q_BLOCK), :]

    def v_cols(J0):
        return jnp.concatenate([vs_ref[0, J0 + t] for t in range(FAR_TILES)], axis=1)

    def q_cols(u):
        return jnp.concatenate([q_ref[r, u] for r in range(R)], axis=1)

    def sel_mask(u, J0, n_valid):
        parts = []
        for t in range(2 * FAR_TILES):
            row = jnp.where(t // 2 < n_valid, 2 * J0 + t, n_slc)
            parts.append(jnp.broadcast_to(neg_all[u, pl.ds(row, 1), :], (SLC_LEN, Q_BLOCK)))
        return jnp.concatenate([jnp.concatenate(parts, axis=0)] * R, axis=1)

    def add_values(c, e_ref):
        u, tile, _ = locate(c)
        acc_ref[u] = acc_ref[u] + _dot(v_cols(tile), e_ref[...])

    def stage(c, s_in, s_out, e_in, e_out):
        u0, tile0, _ = locate(c - 1)
        u1, tile1, valid1 = locate(c)
        u2, tile2, _ = locate(c + 1)
        v_prev, acc_prev = v_cols(tile0), acc_ref[u0]
        mask, l_cur = sel_mask(u1, tile1, valid1), l_ref[u1]
        k_next, q_next = k_rows(tile2), q_cols(u2)
        pv = _dot(v_prev, e_in[...])
        s_out[...] = _dot(k_next, q_next)
        e = jnp.exp2(s_in[...] + mask)
        e_out[...] = e.astype(BF16)
        acc_ref[u0] = acc_prev + pv
        l_ref[u1] = l_cur + jnp.sum(e, axis=0, keepdims=True)

    s_a, s_b, e_a, e_b = stage_refs
    e_b[...] = jnp.zeros_like(e_b)
    u_first, tile_first, _ = locate(0)
    s_a[...] = _dot(k_rows(tile_first), q_cols(u_first))

    def pair(c):
        stage(c, s_a, s_b, e_b, e_a)
        stage(c + 1, s_b, s_a, e_a, e_b)

    def quad(c):
        pair(c)
        pair(c + 2)

    def oct_body(t, carry):
        quad(8 * t)
        quad(8 * t + 4)
        return carry

    n_octs = n_tot // 8
    lax.fori_loop(0, n_octs, oct_body, 0)
    done = 8 * n_octs

    @pl.when(n_tot - done >= 4)
    def _():
        quad(done)

    done = done + 4 * ((n_tot - done) // 4)

    @pl.when(n_tot - done >= 2)
    def _():
        pair(done)

    @pl.when(n_tot % 2 == 1)
    def _():
        stage(n_tot - 1, s_a, s_b, e_b, e_a)
        add_values(n_tot - 1, e_a)

    @pl.when((n_tot % 2 == 0) & (n_tot > 0))
    def _():
        add_values(n_tot - 1, e_b)


def _nsa_block(u, refs, neg_ref, acc_ref, l_ref, nb):
    (q_ref, kc_ref, vc_ref, ks_ref, vs_ref, kw_ref, vw_ref, gate_ref, cbias_ref, sbias_ref, wbias_ref,
     ov_ref, msel_ref) = refs
    i = pl.program_id(2) * NSA_SUB + u
    R = NSA_REP
    N = R * Q_BLOCK
    nc = kc_ref.shape[2]
    q_t = jnp.concatenate([q_ref[r, u] for r in range(R)], axis=1)
    n_win = NSA_WINDOW // Q_BLOCK

    def k_rows(ref, J0, n_tiles):
        return ref[0, pl.ds(pl.multiple_of(J0 * Q_BLOCK, Q_BLOCK), n_tiles * Q_BLOCK), :]

    def v_cols(ref, J0, n_tiles):
        return jnp.concatenate([ref[0, J0 + u] for u in range(n_tiles)], axis=1)

    def sel_mask(J0, n_tiles, n_valid=None):
        n_slc = neg_ref.shape[0] - NEG_ROWS
        parts = []
        for u in range(2 * n_tiles):
            row = 2 * J0 + u
            if n_valid is not None:
                row = jnp.where(u // 2 < n_valid, row, n_slc)
            parts.append(jnp.broadcast_to(neg_ref[pl.ds(row, 1), :], (SLC_LEN, Q_BLOCK)))
        return jnp.concatenate([jnp.concatenate(parts, axis=0)] * R, axis=1)

    def compressed(fast):
        cb = cbias_ref[0, pl.ds(pl.multiple_of(8 * (nb - 1) - 8 * i, 8), nc), :]
        s = _dot(kc_ref[0, 0], q_t) + cb
        if fast:
            e = jnp.exp2(s)
        else:
            e = jnp.where(s > 0.5 * NEG_INF, jnp.exp2(s - jnp.max(s, axis=0, keepdims=True)), 0.0)
        l = jnp.sum(e, axis=0, keepdims=True)
        p = e * jnp.where(l > 0.0, 1.0 / l, 0.0)
        return p, l, _dot(vc_ref[0, 0], p.astype(BF16))

    def select_blocks(p_c):
        p_sum = p_c[:, 0:Q_BLOCK]
        for r in range(1, R):
            p_sum = p_sum + p_c[:, r * Q_BLOCK:(r + 1) * Q_BLOCK]
        p_hi = p_sum.astype(BF16)
        p_lo = (p_sum - p_hi.astype(F32)).astype(BF16)
        imp = _dot(ov_ref[...], p_hi) + _dot(ov_ref[...], p_lo)
        n_slc = imp.shape[0]
        jj = lax.broadcasted_iota(jnp.int32, imp.shape, 0)
        qq = lax.broadcasted_iota(jnp.int32, imp.shape, 1)
        cur = 2 * i + jnp.where(qq >= SLC_LEN, 1, 0)
        forced = (jj == 0) | (jj == cur) | (jj == cur - 1)
        score = jnp.where(forced, TAKEN, jnp.where(jj <= cur, imp, -1.0))
        grp = 8
        j_rows = [jj[a:a + grp] for a in range(0, n_slc, grp)]
        for _ in range(min(SLC_TOPN, n_slc) - N_FORCED):
            cand = [(score[a:a + grp], j_rows[a // grp]) for a in range(0, n_slc, grp)]
            while len(cand) > 1:
                nxt = []
                for a in range(0, len(cand) - 1, 2):
                    (s0, j0), (s1, j1) = cand[a], cand[a + 1]
                    right = s1 > s0
                    nxt.append((jnp.maximum(s0, s1), jnp.where(right, j1, j0)))
                cand = nxt + cand[len(cand) - len(cand) % 2:]
            s8, j8 = cand[0]
            best = jnp.max(s8, axis=0, keepdims=True)
            first = jnp.min(jnp.where(s8 == best, j8, n_slc), axis=0, keepdims=True)
            score = jnp.where(jj == first, TAKEN, score)
        neg_ref[:n_slc] = jnp.where(score == TAKEN, msel_ref[0], NEG_INF)
        neg_ref[n_slc:] = jnp.full((NEG_ROWS, Q_BLOCK), NEG_INF, F32)

    def add_chunk(st, s, v_t):
        e = jnp.exp2(s)
        return st[0] + jnp.sum(e, axis=0, keepdims=True), st[1] + _dot(v_t, e.astype(BF16))

    zero = (jnp.zeros((1, N), F32), jnp.zeros((HEAD_DIM, N), F32))

    def selected_near():
        J0 = jnp.maximum(i - 1, 0)
        bias = sbias_ref[0, pl.ds(pl.multiple_of((2 - (i - J0)) * Q_BLOCK, Q_BLOCK), 2 * Q_BLOCK), :]
        return add_chunk(zero, _dot(k_rows(ks_ref, J0, 2), q_t) + bias + sel_mask(J0, 2), v_cols(vs_ref, J0, 2))

    def window_fast():
        J0 = jnp.maximum(i - n_win, 0)
        bias = wbias_ref[0, pl.ds(pl.multiple_of((n_win - (i - J0)) * Q_BLOCK, Q_BLOCK), (n_win + 1) * Q_BLOCK), :]
        return add_chunk(zero, _dot(k_rows(kw_ref, J0, n_win + 1), q_t) + bias, v_cols(vw_ref, J0, n_win + 1))

    def sweep_exact(k_ref, v_ref, bias_ref, far_dist, n_back, with_sel):
        def body(t, st):
            J = i - t
            off = pl.multiple_of((far_dist - jnp.minimum(t, far_dist)) * Q_BLOCK, Q_BLOCK)
            s = _dot(k_rows(k_ref, J, 1), q_t) + bias_ref[0, pl.ds(off, Q_BLOCK), :]
            if with_sel:
                s = s + sel_mask(J, 1)
            return _softmax_tile_update(s, v_ref[0, J], *st)

        init = (jnp.full((1, N), NEG_INF, F32),) + zero
        _, l, acc = lax.fori_loop(0, jnp.minimum(i, n_back) + 1, body, init)
        return l, acc

    vals = {}

    def load_phase():
        vals["cmp"] = compressed(True)
        vals["win"] = window_fast()

    def select_phase():
        select_blocks(vals["cmp"][0])

    def near_phase():
        l_near, acc_near = selected_near()
        l_ref[u] = jnp.broadcast_to(l_near, l_ref.shape[1:])
        acc_ref[u] = acc_near

    def exact_path():
        p_x, _, o_x = compressed(False)
        select_blocks(p_x)
        l_sx, a_sx = sweep_exact(ks_ref, vs_ref, sbias_ref, 2, nb, True)
        l_wx, a_wx = sweep_exact(kw_ref, vw_ref, wbias_ref, n_win, n_win, False)
        return o_x, a_sx * (1.0 / l_sx), a_wx * (1.0 / l_wx)

    def gate_row(br):
        return jnp.concatenate([gate_ref[br, 0, r:r + 1, u * Q_BLOCK:(u + 1) * Q_BLOCK] for r in range(R)], axis=1)

    def finish():
        _, l_c, o_c = vals["cmp"]
        l_w, a_w = vals["win"]
        l_s, a_s = l_ref[u, 0:1, :], acc_ref[u]
        lane_q = lax.broadcasted_iota(jnp.int32, (1, N), 1) % Q_BLOCK
        has_cmp = (i > 0) | (lane_q >= CMP_LEN - 1)
        l_min = jnp.minimum(jnp.minimum(l_s, l_w), jnp.where(has_cmp, l_c, 1.0))
        o_cx, o_s, o_w = lax.cond(jnp.min(l_min) > NSA_TINY,
                                  lambda: (o_c, a_s * (1.0 / l_s), a_w * (1.0 / l_w)), exact_path)
        o_t = gate_row(0) * o_cx + gate_row(1) * o_s + gate_row(2) * o_w
        return _pairs_to_rows(o_t)

    return load_phase, select_phase, near_phase, finish


def _nsa_attention(colmat, rowmat, cmp_rows, cmp_cols, gates, cbias, sbias, wbias, ov_t, msel, B, S):
    nb = S // Q_BLOCK
    nc = S // CMP_STRIDE
    n_slc = S // SLC_LEN
    G, R = NSA_KV_HEADS, NSA_REP
    kern = functools.partial(_nsa_kernel, nb=nb)
    ns = nb // NSA_SUB
    return pl.pallas_call(
        kern,
        grid=(B, G, ns),
        in_specs=[
            pl.BlockSpec((R, NSA_SUB, HEAD_DIM, LANES), lambda b, g, i: (g, b * ns + i, 0, 0)),
            pl.BlockSpec((1, 1, nc, HEAD_DIM), lambda b, g, i: (g, b, 0, 0)),
            pl.BlockSpec((1, 1, HEAD_DIM, nc), lambda b, g, i: (NSA_KV_HEADS + g, b, 0, 0)),
            pl.BlockSpec((1, S, HEAD_DIM), lambda b, g, i: (ROW_KS + g, b, 0)),
            pl.BlockSpec((1, nb, HEAD_DIM, LANES), lambda b, g, i: (COL_VS + g, b, 0, 0)),
            pl.BlockSpec((1, S, HEAD_DIM), lambda b, g, i: (ROW_KW + g, b, 0)),
            pl.BlockSpec((1, nb, HEAD_DIM, LANES), lambda b, g, i: (COL_VW + g, b, 0, 0)),
            pl.BlockSpec((N_BRANCHES, 1, R, NSA_SUB * LANES), lambda b, g, i: (0, g, 0, b * ns + i)),
            pl.BlockSpec((1, cbias.shape[1], R * Q_BLOCK), lambda b, g, i: (g, 0, 0)),
            pl.BlockSpec((1, sbias.shape[1], R * Q_BLOCK), lambda b, g, i: (g, 0, 0)),
            pl.BlockSpec((1, wbias.shape[1], R * Q_BLOCK), lambda b, g, i: (g, 0, 0)),
            pl.BlockSpec((n_slc, nc), lambda b, g, i: (0, 0)),
            pl.BlockSpec((1, 1, Q_BLOCK), lambda b, g, i: (g, 0, 0)),
        ],
        out_specs=pl.BlockSpec((1, NSA_SUB * Q_BLOCK, R * HEAD_DIM), lambda b, g, i: (b, i, g)),
        out_shape=jax.ShapeDtypeStruct((B, S, NSA_HEADS * HEAD_DIM), BF16),
        scratch_shapes=[pltpu.VMEM((NSA_SUB, n_slc + NEG_ROWS, Q_BLOCK), F32)]
        + [pltpu.VMEM((FAR_TILES * Q_BLOCK, R * Q_BLOCK), F32)] * 2
        + [pltpu.VMEM((FAR_TILES * Q_BLOCK, R * Q_BLOCK), BF16)] * 2
        + [pltpu.VMEM((NSA_SUB, HEAD_DIM, R * Q_BLOCK), F32),
           pltpu.VMEM((NSA_SUB, 8, R * Q_BLOCK), F32)],
        compiler_params=_cparams(("parallel", "parallel", "arbitrary")),
        name="nsa_attention",
    )(colmat, cmp_rows, cmp_cols, rowmat, colmat, rowmat, colmat, gates, cbias, sbias, wbias, ov_t, msel)


SB_QB = 512
SB_KT = 256
SB_ZERO_LOG = -160.0


def _sb_kernel(q_ref, k_ref, v_ref, tri_ref, o_ref):
    i = pl.program_id(1)
    H = SB_HEADS
    tri = tri_ref[...]
    assert SB_QB == 2 * SB_KT
    below_diag = (lax.broadcasted_iota(jnp.int32, (SB_KT, SB_KT), 1)
                  < lax.broadcasted_iota(jnp.int32, (SB_KT, SB_KT), 0))

    def step(J, rows, st, masked):
        out = []
        for h in range(H):
            carry, acc = st[h]
            k_t = jnp.concatenate([k_ref[h, (SB_KT // LANES) * J + c] for c in range(SB_KT // LANES)], axis=1)
            z = _dot(q_ref[h, rows, :], k_t)
            sp = jnp.maximum(z, 0.0) + jnp.log2(1.0 + jnp.exp2(-jnp.abs(z)))
            lf = jnp.where(below_diag, -sp, 0.0) if masked else -sp
            lf_hi = lf.astype(BF16)
            lf_lo = (lf - lf_hi.astype(F32)).astype(BF16)
            after = _dot(lf_hi, tri) + _dot(lf_lo, tri) + carry
            w = jnp.exp2(z - sp + after)
            if masked:
                w = jnp.where(below_diag, w, 0.0)
            v = v_ref[h, pl.ds(pl.multiple_of(J * SB_KT, SB_KT), SB_KT), :]
            acc = acc + _dot(w.astype(BF16), v)
            carry = carry + jnp.sum(lf, axis=1, keepdims=True)
            out.append((carry, acc))
        return tuple(out)

    lo, hi = slice(0, SB_KT), slice(SB_KT, SB_QB)
    zero = tuple((jnp.zeros((SB_KT, 1), F32), jnp.zeros((SB_KT, HEAD_DIM), F32)) for _ in range(H))
    st_hi = step(2 * i, hi, step(2 * i + 1, hi, zero, True), False)
    st_lo = step(2 * i, lo, zero, True)

    def alive(st):
        worst = st[0][0]
        for h in range(1, H):
            worst = jnp.maximum(worst, st[h][0])
        return jnp.max(worst) > SB_ZERO_LOG

    def sweep(rows, st):
        def body(c):
            J, _, st = c
            st = step(J, rows, st, False)
            return J - 1, alive(st), st

        return lax.while_loop(lambda c: (c[0] >= 0) & c[1], body, (2 * i - 1, alive(st), st))[2]

    st_lo, st_hi = sweep(lo, st_lo), sweep(hi, st_hi)
    o_ref[0, lo, :] = jnp.concatenate([st_lo[h][1] for h in range(H)], axis=1).astype(o_ref.dtype)
    o_ref[0, hi, :] = jnp.concatenate([st_hi[h][1] for h in range(H)], axis=1).astype(o_ref.dtype)


def _sb_attention(colmat, rowmat, tri, B, S):
    nq = S // SB_QB
    nb = S // LANES
    H = SB_HEADS
    return pl.pallas_call(
        _sb_kernel,
        grid=(B, nq),
        in_specs=[
            pl.BlockSpec((H, SB_QB, HEAD_DIM), lambda b, i: (ROW_SQ // H, b * nq + i, 0)),
            pl.BlockSpec((H, nb, HEAD_DIM, LANES), lambda b, i: (COL_SK // H, b, 0, 0)),
            pl.BlockSpec((H, S, HEAD_DIM), lambda b, i: (ROW_SV // H, b, 0)),
            pl.BlockSpec((SB_KT, SB_KT), lambda b, i: (0, 0)),
        ],
        out_specs=pl.BlockSpec((1, SB_QB, H * HEAD_DIM), lambda b, i: (b, i, 0)),
        out_shape=jax.ShapeDtypeStruct((B, S, H * HEAD_DIM), BF16),
        compiler_params=_cparams(("parallel", "arbitrary")),
        name="sb_attention",
    )(rowmat, colmat, rowmat, tri)


SWA_SUB = 4


def _swa_kernel(q_ref, k_ref, v_ref, bias_ref, sink_ref, o_ref):
    R = SWA_REP
    sink = sink_ref[0]
    outs = []
    for u in range(SWA_SUB):
        i = pl.program_id(2) * SWA_SUB + u
        q_t = jnp.concatenate([q_ref[r, u] for r in range(R)], axis=1)
        J0 = jnp.maximum(i - 1, 0)
        k = k_ref[0, pl.ds(pl.multiple_of(J0 * Q_BLOCK, Q_BLOCK), 2 * Q_BLOCK), :]
        bias = bias_ref[0, pl.ds(pl.multiple_of((1 - (i - J0)) * Q_BLOCK, Q_BLOCK), 2 * Q_BLOCK), :]
        s = _dot(k, q_t) + bias
        m = jnp.maximum(jnp.max(s, axis=0, keepdims=True), sink)
        e = jnp.exp(s - m)
        den = jnp.sum(e, axis=0, keepdims=True) + jnp.exp(sink - m)
        v_t = jnp.concatenate([v_ref[0, J0], v_ref[0, J0 + 1]], axis=1)
        o_t = _dot(v_t, e.astype(BF16)) * (1.0 / den)
        outs.append(_pairs_to_rows(o_t).astype(o_ref.dtype))
    for u in range(SWA_SUB):
        o_ref[0, u * Q_BLOCK:(u + 1) * Q_BLOCK, :] = outs[u]


def _swa_attention(colmat, rowmat, bias, sinks, B, S):
    nb = S // Q_BLOCK
    ns = nb // SWA_SUB
    G, R = SWA_KV_HEADS, SWA_REP
    return pl.pallas_call(
        _swa_kernel,
        grid=(B, G, ns),
        in_specs=[
            pl.BlockSpec((R, SWA_SUB, HEAD_DIM, LANES), lambda b, g, i: (COL_WQ // R + g, b * ns + i, 0, 0)),
            pl.BlockSpec((1, S, HEAD_DIM), lambda b, g, i: (ROW_WK + g, b, 0)),
            pl.BlockSpec((1, nb, HEAD_DIM, LANES), lambda b, g, i: (COL_WV + g, b, 0, 0)),
            pl.BlockSpec((1, 3 * Q_BLOCK, R * Q_BLOCK), lambda b, g, i: (g, 0, 0)),
            pl.BlockSpec((1, 1, R * Q_BLOCK), lambda b, g, i: (g, 0, 0)),
        ],
        out_specs=pl.BlockSpec((1, SWA_SUB * Q_BLOCK, R * HEAD_DIM), lambda b, g, i: (b, i, g)),
        out_shape=jax.ShapeDtypeStruct((B, S, SWA_HEADS * HEAD_DIM), BF16),
        compiler_params=_cparams(("parallel", "parallel", "arbitrary")),
        name="swa_attention",
    )(colmat, rowmat, colmat, bias, sinks)


MERGE_TM = 512


def _merge_kernel(x_ref, nw_ref, wg_ref, ya_ref, yb_ref, yc_ref, ua_ref, ub_ref, uc_ref, wo_ref, o_ref):
    x = x_ref[...]
    h = _rms_rows(x, nw_ref[...]).astype(BF16)
    merged = None
    for br, (y_ref, u_ref) in enumerate(((ya_ref, ua_ref), (yb_ref, ub_ref), (yc_ref, uc_ref))):
        g = jax.nn.sigmoid(_dot(h, wg_ref[:, br * D_MODEL:(br + 1) * D_MODEL]))
        t = g * _dot(y_ref[...], u_ref[...])
        merged = t if merged is None else merged + t
    o_ref[...] = x + _dot(merged.astype(BF16), wo_ref[...])


def _merge(x2, nw, w_bgate, y_nsa, y_sb, y_swa, u_nsa, u_sb, u_swa, w_out, l):
    T = x2.shape[0]
    full = lambda w: _layer_spec(l, w.shape[1:])
    rows = lambda n: pl.BlockSpec((MERGE_TM, n), lambda i: (i, 0))
    return pl.pallas_call(
        _merge_kernel,
        grid=(T // MERGE_TM,),
        in_specs=[rows(D_MODEL), pl.BlockSpec((1, D_MODEL), lambda i: (0, 0)), full(w_bgate),
                  rows(y_nsa.shape[1]), rows(y_sb.shape[1]), rows(y_swa.shape[1]),
                  full(u_nsa), full(u_sb), full(u_swa), full(w_out)],
        out_specs=rows(D_MODEL),
        out_shape=jax.ShapeDtypeStruct((T, D_MODEL), F32),
        compiler_params=_cparams(("parallel",)),
        name="merge",
    )(x2, nw.reshape(1, D_MODEL), w_bgate, y_nsa, y_sb, y_swa, u_nsa, u_sb, u_swa, w_out)


def _t5_bucket(dist):
    max_exact = NUM_BUCKETS // 2
    d = jnp.maximum(dist, 0)
    df = jnp.maximum(d, 1).astype(F32)
    large = max_exact + (jnp.log(df / max_exact) / math.log(MAX_DISTANCE / max_exact)
                         * (NUM_BUCKETS - max_exact)).astype(jnp.int32)
    large = jnp.minimum(large, NUM_BUCKETS - 1)
    return jnp.where(d < max_exact, d, large)


def _toeplitz(ext, delta):
    H = ext.shape[0]
    n = Q_BLOCK
    c = delta + MAX_DISTANCE
    w = ext[:, c - (n - 1):c + n]
    flat = jnp.pad(jnp.broadcast_to(w[:, None, :], (H, n, 2 * n - 1)), ((0, 0), (0, 0), (0, 1))).reshape(H, 2 * n * n)
    return flat[:, n - 1:n - 1 + n * (2 * n - 1)].reshape(H, n, 2 * n - 1)[:, :, :n]


def _lanes(t, G, R):
    K, Q = t.shape[1:]
    return t.reshape(G, R, K, Q).transpose(0, 2, 1, 3).reshape(G, K, R * Q)


def _bias_tables(rel_bias, nb, qk_bound):
    n = Q_BLOCK
    G, R = NSA_KV_HEADS, NSA_REP
    by_dist = rel_bias[_t5_bucket(jnp.arange(MAX_DISTANCE)), :].T.astype(F32)
    far = rel_bias[NUM_BUCKETS - 1, :].astype(F32)
    neg = jnp.full((1, n), NEG_INF, F32)
    k = jnp.arange(n)[:, None]
    q = jnp.arange(n)[None, :]
    above = (k > q)[None]

    rel = (by_dist[:NSA_HEADS] - far[:NSA_HEADS, None]) * LOG2E
    m_g = qk_bound + jnp.maximum(jnp.max(rel.reshape(G, -1), axis=1), 0.0)
    m_h = jnp.repeat(m_g, R)[:, None, None]
    ext = jnp.concatenate([jnp.broadcast_to(neg, (NSA_HEADS, n)), rel, jnp.zeros((NSA_HEADS, n), F32)], axis=1)
    t0, t1 = _toeplitz(ext, 0), _toeplitz(ext, n)
    zeros = jnp.zeros_like(t0)
    masked = jnp.full_like(t0, NEG_INF)
    sbias = _lanes(jnp.concatenate([zeros, t1, t0, masked], axis=1), G, R)
    n_win = NSA_WINDOW // n
    edge = jnp.where(above, zeros, NEG_INF)
    tiles = [edge] + [zeros] * (n_win - 2) + [t1, t0]
    wbias = _lanes(jnp.concatenate([t - m_h for t in tiles] + [masked] * n_win, axis=1), G, R)
    off = 8 * (nb - 1)
    near = jnp.stack([ext[:, 97 - CMP_STRIDE * m:97 - CMP_STRIDE * m + n] for m in range(-9, 7)], axis=1)
    n_rows = 16 * nb - 8
    cb = jnp.concatenate([jnp.zeros((NSA_HEADS, off - 9, n), F32), near,
                          jnp.full((NSA_HEADS, n_rows - off - 7, n), NEG_INF, F32)], axis=1) - m_h
    cbias = _lanes(cb, G, R)
    msel = jnp.broadcast_to(-m_g[:, None, None], (G, 1, n))

    hs = slice(NSA_HEADS, NSA_HEADS + SWA_HEADS)
    ext_w = jnp.concatenate([jnp.broadcast_to(neg, (SWA_HEADS, n)), by_dist[hs],
                             jnp.broadcast_to(far[hs, None], (SWA_HEADS, n))], axis=1)
    s0 = _toeplitz(ext_w, 0)
    s1 = jnp.where(above, _toeplitz(ext_w, n), NEG_INF)
    swa_bias = _lanes(jnp.concatenate([s1, s0, jnp.full_like(s0, NEG_INF)], axis=1), SWA_KV_HEADS, SWA_REP)
    return cbias, sbias, wbias, msel, swa_bias


def _overlap_t(S):
    nc, n_slc = S // CMP_STRIDE, S // SLC_LEN
    n = np.arange(nc)[None, :]
    j = np.arange(n_slc)[:, None]
    ov = (n * CMP_STRIDE < (j + 1) * SLC_LEN) & (n * CMP_STRIDE + CMP_LEN - 1 >= j * SLC_LEN) & (n < nc - 1)
    return jnp.asarray(ov, BF16)


def _head_gains(heads, norm_of, scale_of):
    ones = jnp.ones((HEAD_DIM,), F32)
    gains = [norm_of[name].astype(F32) * scale_of.get(name, 1.0) if name in norm_of else ones for name, _ in heads]
    return jnp.concatenate(gains).reshape(1, -1)


def kernel(x, rel_bias, ffn1_norm, ffn1_w_gate, ffn1_w_up, ffn1_w_down, mix_norm, w_in, nsa_q_norm, nsa_k_norm, nsa_cmp_pos, nsa_cmp_k_w1, nsa_cmp_k_w2, nsa_cmp_v_w1, nsa_cmp_v_w2, swa_q_norm, swa_k_norm, swa_sinks, w_up_nsa, w_up_sb, w_up_swa, w_out, ffn2_norm, ffn2_w_gate, ffn2_w_up, ffn2_w_down):
    B, S, D = x.shape
    T = B * S
    nb = S // Q_BLOCK
    nc = S // CMP_STRIDE
    depth = w_in.shape[0]
    scale = HEAD_DIM ** -0.5
    scale_of = {"nq": scale * LOG2E, "sq": scale * LOG2E, "wq": scale}
    qk_bound = (BOUND_SLACK * LOG2E * HEAD_DIM ** 0.5
                * jnp.max(jnp.abs(nsa_q_norm.astype(F32))) * jnp.max(jnp.abs(nsa_k_norm.astype(F32))))
    cbias, sbias, wbias, msel, swa_bias = _bias_tables(rel_bias, nb, qk_bound)
    plan = _proj_plan(scale_of)
    ov_t = _overlap_t(S)
    tri = jnp.asarray(np.tril(np.ones((SB_KT, SB_KT), np.float32), -1), BF16)

    (ffn1_w_gate, ffn1_w_up, ffn1_w_down, ffn2_w_gate, ffn2_w_up, ffn2_w_down, w_up_nsa, w_up_sb, w_up_swa,
     w_out) = (w.astype(BF16) for w in (ffn1_w_gate, ffn1_w_up, ffn1_w_down, ffn2_w_gate, ffn2_w_up, ffn2_w_down,
                                        w_up_nsa, w_up_sb, w_up_swa, w_out))
    tail0 = OFF_NGATE + N_GATE
    w_head, w_tail = w_in[:, :, :OFF_NGATE].astype(BF16), w_in[:, :, tail0:].astype(BF16)
    w_bgate = w_tail[:, :, OFF_BGATE - tail0:]
    w_gate = jnp.pad(w_in[:, :, OFF_NGATE:tail0], ((0, 0), (0, 0), (0, LANES - N_GATE))).astype(BF16)
    runs = []
    for _, off in ROW_HEADS + COL_HEADS:
        if runs and runs[-1][1] == off:
            runs[-1][1] = off + HEAD_DIM
        else:
            runs.append([off, off + HEAD_DIM])
    pieces = [w_head[:, :, a:b] if a < OFF_NGATE else w_tail[:, :, a - tail0:b - tail0] for a, b in runs]
    w_proj = jnp.concatenate(pieces + [w_gate], axis=2)

    x2 = x.reshape(T, D)
    for l in range(depth):
        x2 = _ffn(x2, ffn1_norm[l], ffn1_w_gate, ffn1_w_up, ffn1_w_down, l)

        norm_of = {"nq": nsa_q_norm[l], "nks": nsa_k_norm[l], "nkw": nsa_k_norm[l],
                   "wq": swa_q_norm[l], "wk": swa_k_norm[l]}
        gain = jnp.concatenate([_head_gains(ROW_HEADS, norm_of, scale_of), _head_gains(COL_HEADS, norm_of, scale_of),
                                jnp.ones((1, LANES), F32)], axis=1)
        rowmat, colmat, gates = _proj(x2, mix_norm[l], w_proj, gain, plan, l)
        gates = gates[:N_GATE].reshape(N_BRANCHES, NSA_KV_HEADS, NSA_REP, T)

        rows4 = rowmat[ROW_KC:ROW_KC + 2 * NSA_KV_HEADS].reshape(2 * NSA_KV_HEADS, B, nc, CMP_STRIDE * HEAD_DIM)
        cmp_rows, cmp_cols = _compress(rows4, jnp.stack([nsa_cmp_k_w1[l], nsa_cmp_v_w1[l]]),
                                       jnp.stack([nsa_cmp_k_w2[l], nsa_cmp_v_w2[l]]),
                                       nsa_cmp_pos[l], nsa_k_norm[l])

        y_nsa = _nsa_attention(colmat, rowmat, cmp_rows, cmp_cols, gates, cbias, sbias, wbias, ov_t, msel, B, S)
        y_sb = _sb_attention(colmat, rowmat, tri, B, S)
        sinks = jnp.repeat(swa_sinks[l].astype(F32).reshape(SWA_KV_HEADS, 1, SWA_REP), Q_BLOCK, axis=2)
        y_swa = _swa_attention(colmat, rowmat, swa_bias, sinks, B, S)

        x2 = _merge(x2, mix_norm[l], w_bgate, y_nsa.reshape(T, -1), y_sb.reshape(T, -1),
                    y_swa.reshape(T, -1), w_up_nsa, w_up_sb, w_up_swa, w_out, l)
        x2 = _ffn(x2, ffn2_norm[l], ffn2_w_gate, ffn2_w_up, ffn2_w_down, l)
    return x2.reshape(B, S, D)
```

```python
import functools
import math

import numpy as np
import jax
import jax.numpy as jnp
from jax import lax
from jax.experimental import pallas as pl
from jax.experimental.pallas import tpu as pltpu

D_MODEL = 1024
HEAD_DIM = 64
Q_BLOCK = 128
NSA_HEADS = 8
NSA_KV_HEADS = 2
NSA_REP = NSA_HEADS // NSA_KV_HEADS
CMP_LEN = 32
CMP_STRIDE = 16
CMP_HIDDEN = 256
SLC_LEN = 64
SLC_TOPN = 16
NSA_WINDOW = 512
SB_HEADS = 4
SWA_HEADS = 4
SWA_KV_HEADS = 2
SWA_REP = SWA_HEADS // SWA_KV_HEADS
SWA_WINDOW = 128
NUM_BUCKETS = 32
MAX_DISTANCE = 128
D_FF = 2816
NORM_EPS = 1e-6
NEG_INF = -1e30
N_BRANCHES = 3
N_GATE = 3 * NSA_HEADS

LANES = 128
VMEM_LIMIT = 56 * 1024 * 1024
BF16 = jnp.bfloat16
F32 = jnp.float32

_sizes = (NSA_HEADS * HEAD_DIM,) + (NSA_KV_HEADS * HEAD_DIM,) * 6 + (N_GATE,) \
    + (SB_HEADS * HEAD_DIM,) * 3 + (SWA_HEADS * HEAD_DIM, SWA_KV_HEADS * HEAD_DIM,
                                    SWA_KV_HEADS * HEAD_DIM, N_BRANCHES * D_MODEL)
_offs = np.concatenate([[0], np.cumsum(_sizes)])
(OFF_NQ, OFF_NKC, OFF_NVC, OFF_NKS, OFF_NVS, OFF_NKW, OFF_NVW, OFF_NGATE,
 OFF_SQ, OFF_SK, OFF_SV, OFF_WQ, OFF_WK, OFF_WV, OFF_BGATE) = (int(o) for o in _offs[:-1])

ROW_HEADS = ([("nkc", OFF_NKC + 64 * g) for g in range(2)] + [("nvc", OFF_NVC + 64 * g) for g in range(2)]
             + [("nks", OFF_NKS + 64 * g) for g in range(2)] + [("nkw", OFF_NKW + 64 * g) for g in range(2)]
             + [("sq", OFF_SQ + 64 * h) for h in range(4)] + [("sv", OFF_SV + 64 * h) for h in range(4)]
             + [("wk", OFF_WK + 64 * g) for g in range(2)])
ROW_KC, ROW_VC, ROW_KS, ROW_KW, ROW_SQ, ROW_SV, ROW_WK = 0, 2, 4, 6, 8, 12, 16
COL_HEADS = ([("nq", OFF_NQ + 64 * h) for h in range(8)] + [("nvs", OFF_NVS + 64 * g) for g in range(2)]
             + [("nvw", OFF_NVW + 64 * g) for g in range(2)] + [("sk", OFF_SK + 64 * h) for h in range(4)]
             + [("wq", OFF_WQ + 64 * h) for h in range(4)] + [("wv", OFF_WV + 64 * g) for g in range(2)])
COL_NQ, COL_VS, COL_VW, COL_SK, COL_WQ, COL_WV = 0, 8, 10, 12, 16, 20


def _dot(a, b):
    return jnp.dot(a, b, preferred_element_type=F32)


def _rms_rows(x, w):
    ms = jnp.mean(x * x, axis=-1, keepdims=True)
    return x * lax.rsqrt(ms + NORM_EPS) * w


def _cparams(sem, flags=None):
    return pltpu.CompilerParams(dimension_semantics=sem, vmem_limit_bytes=VMEM_LIMIT, flags=flags)


FFN_TM = 512
MXU_TILE = 256
FFN_FC = MXU_TILE


def _ffn_kernel(x_ref, nw_ref, wg_ref, wu_ref, wd_ref, o_ref):
    x = x_ref[...]
    h = _rms_rows(x, nw_ref[...]).astype(BF16)
    acc = jnp.zeros(x.shape, F32)
    for c in range(D_FF // FFN_FC):
        cols = slice(c * FFN_FC, (c + 1) * FFN_FC)
        g = _dot(h, wg_ref[:, cols])
        u = _dot(h, wu_ref[:, cols])
        a = (g * jax.nn.sigmoid(g)) * u
        acc = acc + _dot(a.astype(BF16), wd_ref[cols, :])
    o_ref[...] = x + 0.5 * acc


def _layer_spec(l, shape, **kw):
    return pl.BlockSpec((None,) + tuple(shape), lambda i: (l, 0, 0), **kw)


def _ffn(x2, nw, wg, wu, wd, l):
    T = x2.shape[0]
    resident = lambda shape: _layer_spec(l, shape, pipeline_mode=pl.Buffered(1))
    return pl.pallas_call(
        _ffn_kernel,
        grid=(T // FFN_TM,),
        in_specs=[
            pl.BlockSpec((FFN_TM, D_MODEL), lambda i: (i, 0)),
            pl.BlockSpec((1, D_MODEL), lambda i: (0, 0)),
            resident((D_MODEL, D_FF)), resident((D_MODEL, D_FF)), resident((D_FF, D_MODEL)),
        ],
        out_specs=pl.BlockSpec((FFN_TM, D_MODEL), lambda i: (i, 0)),
        out_shape=jax.ShapeDtypeStruct((T, D_MODEL), F32),
        compiler_params=_cparams(("parallel",)),
        name="ffn",
    )(x2, nw.reshape(1, D_MODEL), wg, wu, wd)


PROJ_TM = 512


GATE_ROWS = 32

_NORMED = ("nq", "nks", "nkw", "wq", "wk")


def _proj_plan(scale_of):
    plan = []
    for kind, heads in (("row", ROW_HEADS), ("col", COL_HEADS)):
        for p in range(len(heads) // 2):
            name = heads[2 * p][0]
            assert heads[2 * p + 1][0] == name
            plan.append((kind, p, name in _NORMED, 1.0 if name in _NORMED else scale_of.get(name, 1.0)))
    plan.append(("gate", 0, False, 1.0))
    return tuple(plan)


def _proj_kernel(x_ref, nw_ref, w_ref, gain_ref, bd_ref, row_ref, col_ref, gate_ref, *, plan):
    h = _rms_rows(x_ref[...], nw_ref[...]).astype(BF16)
    pair = 2 * MXU_TILE
    for c0 in range(0, len(plan) * LANES, pair):
        width = min(pair, len(plan) * LANES - c0)
        zg = _dot(h, w_ref[:, c0:c0 + width])
        for u in range(width // LANES):
            t = c0 // LANES + u
            kind, p, normed, scale = plan[t]
            z = zg[:, u * LANES:(u + 1) * LANES]
            if normed:
                ms = _dot((z * z).astype(BF16), bd_ref[...])
                z = z * lax.rsqrt(ms + NORM_EPS) * gain_ref[:, t * LANES:(t + 1) * LANES]
            elif scale != 1.0:
                z = z * scale
            if kind == "row":
                row_ref[2 * p] = z[:, :HEAD_DIM].astype(row_ref.dtype)
                row_ref[2 * p + 1] = z[:, HEAD_DIM:].astype(row_ref.dtype)
            elif kind == "col":
                z_t = z.T
                for c in range(PROJ_TM // LANES):
                    col_ref[2 * p, c] = z_t[:HEAD_DIM, c * LANES:(c + 1) * LANES].astype(col_ref.dtype)
                    col_ref[2 * p + 1, c] = z_t[HEAD_DIM:, c * LANES:(c + 1) * LANES].astype(col_ref.dtype)
            else:
                gate_ref[...] = jax.nn.sigmoid(z).T[:GATE_ROWS]


def _proj(x2, nw, w, gain, plan, l):
    T = x2.shape[0]
    n_row, n_col = len(ROW_HEADS), len(COL_HEADS)
    bd = np.kron(np.eye(2, dtype=np.float32), np.full((HEAD_DIM, HEAD_DIM), 1.0 / HEAD_DIM, np.float32))
    return pl.pallas_call(
        functools.partial(_proj_kernel, plan=plan),
        grid=(T // PROJ_TM,),
        in_specs=[
            pl.BlockSpec((PROJ_TM, D_MODEL), lambda i: (i, 0)),
            pl.BlockSpec((1, D_MODEL), lambda i: (0, 0)),
            _layer_spec(l, w.shape[1:]),
            pl.BlockSpec(gain.shape, lambda i: (0, 0)),
            pl.BlockSpec((LANES, LANES), lambda i: (0, 0)),
        ],
        out_specs=[
            pl.BlockSpec((n_row, PROJ_TM, HEAD_DIM), lambda i: (0, i, 0)),
            pl.BlockSpec((n_col, PROJ_TM // LANES, HEAD_DIM, LANES), lambda i: (0, i, 0, 0)),
            pl.BlockSpec((GATE_ROWS, PROJ_TM), lambda i: (0, i)),
        ],
        out_shape=[jax.ShapeDtypeStruct((n_row, T, HEAD_DIM), BF16),
                   jax.ShapeDtypeStruct((n_col, T // LANES, HEAD_DIM, LANES), BF16),
                   jax.ShapeDtypeStruct((GATE_ROWS, T), F32)],
        compiler_params=_cparams(("parallel",)),
        name="proj",
    )(x2, nw.reshape(1, D_MODEL), w, gain, jnp.asarray(bd, BF16))


def _compress_kernel(r_ref, w1_ref, w2_ref, pos_ref, kn_ref, row_ref, col_ref):
    kind = pl.program_id(0) // NSA_KV_HEADS
    r = r_ref[0, 0]
    nc = r.shape[0]
    half = CMP_STRIDE * HEAD_DIM
    w1 = w1_ref[0]
    p_lo = _dot(r, w1[:half])
    p_hi = _dot(r, w1[half:])
    p_pos = _dot(jnp.broadcast_to(pos_ref[...], (8, CMP_LEN * HEAD_DIM)).astype(BF16), w1)[:1]
    hdn = p_lo + pltpu.roll(p_hi, nc - 1, 0) + p_pos
    hdn = hdn * jax.nn.sigmoid(hdn)
    out = _dot(hdn.astype(BF16), w2_ref[0])
    ms = jnp.sum(out * out, axis=-1, keepdims=True) * (1.0 / HEAD_DIM)
    normed = out * lax.rsqrt(ms + NORM_EPS) * kn_ref[...]
    out = jnp.where(kind == 0, normed, out)
    row_ref[0, 0] = out[:, :HEAD_DIM].astype(row_ref.dtype)
    col_ref[0, 0] = out.T[:HEAD_DIM].astype(col_ref.dtype)


def _compress(rows4, w1, w2, pos, k_norm):
    assert ROW_KC == 0 and ROW_VC == NSA_KV_HEADS
    _, B, NC, _ = rows4.shape
    kn = jnp.pad(k_norm.reshape(1, HEAD_DIM), ((0, 0), (0, LANES - HEAD_DIM)))
    w2p = jnp.pad(w2, ((0, 0), (0, 0), (0, LANES - HEAD_DIM))).astype(BF16)
    return pl.pallas_call(
        _compress_kernel,
        grid=(4, B),
        in_specs=[
            pl.BlockSpec((1, 1, NC, CMP_STRIDE * HEAD_DIM), lambda h, b: (h, b, 0, 0)),
            pl.BlockSpec((1, CMP_LEN * HEAD_DIM, CMP_HIDDEN), lambda h, b: (h // NSA_KV_HEADS, 0, 0)),
            pl.BlockSpec((1, CMP_HIDDEN, LANES), lambda h, b: (h // NSA_KV_HEADS, 0, 0)),
            pl.BlockSpec((1, CMP_LEN * HEAD_DIM), lambda h, b: (0, 0)),
            pl.BlockSpec((1, LANES), lambda h, b: (0, 0)),
        ],
        out_specs=[
            pl.BlockSpec((1, 1, NC, HEAD_DIM), lambda h, b: (h, b, 0, 0)),
            pl.BlockSpec((1, 1, HEAD_DIM, NC), lambda h, b: (h, b, 0, 0)),
        ],
        out_shape=[jax.ShapeDtypeStruct((4, B, NC, HEAD_DIM), BF16),
                   jax.ShapeDtypeStruct((4, B, HEAD_DIM, NC), BF16)],
        compiler_params=_cparams(("parallel", "parallel")),
        name="nsa_compress",
    )(rows4, w1.astype(BF16), w2p, pos.reshape(1, CMP_LEN * HEAD_DIM), kn)


def _softmax_tile_update(s, v_t, m, l, acc):
    m_new = jnp.maximum(m, jnp.max(s, axis=0, keepdims=True))
    alpha = jnp.exp2(m - m_new)
    p = jnp.exp2(s - m_new)
    l_new = alpha * l + jnp.sum(p, axis=0, keepdims=True)
    acc_new = alpha * acc + _dot(v_t, p.astype(BF16))
    return m_new, l_new, acc_new


def _pairs_to_rows(o_t):
    n = o_t.shape[1] // LANES
    outs = []
    for p in range(n // 2):
        a = o_t[:, (2 * p) * LANES:(2 * p + 1) * LANES]
        b = o_t[:, (2 * p + 1) * LANES:(2 * p + 2) * LANES]
        outs.append(jnp.concatenate([a, b], axis=0).T)
    return outs[0] if len(outs) == 1 else jnp.concatenate(outs, axis=1)


FAR_TILES = 4
NSA_SUB = 8
TAKEN = -2.0
NEG_ROWS = 8
N_FORCED = 3
LOG2E = math.log2(math.e)
NSA_TINY = 2.0 ** -100
BOUND_SLACK = 1.05


def _nsa_kernel(q_ref, kc_ref, vc_ref, ks_ref, vs_ref, kw_ref, vw_ref, gate_ref,
                cbias_ref, sbias_ref, wbias_ref, ov_ref, msel_ref, o_ref,
                neg_all, s_a, s_b, e_a, e_b, acc_ref, l_ref, *, nb):
    refs = (q_ref, kc_ref, vc_ref, ks_ref, vs_ref, kw_ref, vw_ref, gate_ref, cbias_ref, sbias_ref, wbias_ref,
            ov_ref, msel_ref)
    blocks = [_nsa_block(u, refs, neg_all.at[u], acc_ref, l_ref, nb) for u in range(NSA_SUB)]
    for phase in range(3):
        for blk in blocks:
            blk[phase]()
    _far_sweep(q_ref, ks_ref, vs_ref, neg_all, (s_a, s_b, e_a, e_b), acc_ref, l_ref, nb)
    rows = [blk[3]() for blk in blocks]
    for u in range(NSA_SUB):
        o_ref[0, u * Q_BLOCK:(u + 1) * Q_BLOCK, :] = rows[u].astype(o_ref.dtype)


def _far_sweep(q_ref, ks_ref, vs_ref, neg_all, stage_refs, acc_ref, l_ref, nb):
    R = NSA_REP
    n_slc = neg_all.shape[1] - NEG_ROWS
    i0 = pl.program_id(2) * NSA_SUB
    n_far = [jnp.maximum(i0 + u - 1, 0) for u in range(NSA_SUB)]
    offs = [0]
    for u in range(NSA_SUB):
        offs.append(offs[-1] + (n_far[u] + FAR_TILES - 1) // FAR_TILES)
    n_tot = offs[-1]

    def locate(c):
        u = sum(jnp.where(c >= offs[k], 1, 0) for k in range(1, NSA_SUB))
        start = sum(jnp.where(u == k, offs[k], 0) for k in range(NSA_SUB))
        far = sum(jnp.where(u == k, n_far[k], 0) for k in range(NSA_SUB))
        tile = FAR_TILES * (c - start)
        return u, jnp.clip(tile, 0, nb - FAR_TILES), far - tile

    def k_rows(J0):
        return ks_ref[0, pl.ds(pl.multiple_of(J0 * Q_BLOCK, Q_BLOCK), FAR_TILES * Q_BLOCK), :]

    def v_cols(J0):
        return jnp.concatenate([vs_ref[0, J0 + t] for t in range(FAR_TILES)], axis=1)

    def q_cols(u):
        return jnp.concatenate([q_ref[r, u] for r in range(R)], axis=1)

    def sel_mask(u, J0, n_valid):
        parts = []
        for t in range(2 * FAR_TILES):
            row = jnp.where(t // 2 < n_valid, 2 * J0 + t, n_slc)
            parts.append(jnp.broadcast_to(neg_all[u, pl.ds(row, 1), :], (SLC_LEN, Q_BLOCK)))
        return jnp.concatenate([jnp.concatenate(parts, axis=0)] * R, axis=1)

    def add_values(c, e_ref):
        u, tile, _ = locate(c)
        acc_ref[u] = acc_ref[u] + _dot(v_cols(tile), e_ref[...])

    def stage(c, s_in, s_out, e_in, e_out):
        u0, tile0, _ = locate(c - 1)
        u1, tile1, valid1 = locate(c)
        u2, tile2, _ = locate(c + 1)
        v_prev, acc_prev = v_cols(tile0), acc_ref[u0]
        mask, l_cur = sel_mask(u1, tile1, valid1), l_ref[u1]
        k_next, q_next = k_rows(tile2), q_cols(u2)
        pv = _dot(v_prev, e_in[...])
        s_out[...] = _dot(k_next, q_next)
        e = jnp.exp2(s_in[...] + mask)
        e_out[...] = e.astype(BF16)
        acc_ref[u0] = acc_prev + pv
        l_ref[u1] = l_cur + jnp.sum(e, axis=0, keepdims=True)

    s_a, s_b, e_a, e_b = stage_refs
    e_b[...] = jnp.zeros_like(e_b)
    u_first, tile_first, _ = locate(0)
    s_a[...] = _dot(k_rows(tile_first), q_cols(u_first))

    def pair(c):
        stage(c, s_a, s_b, e_b, e_a)
        stage(c + 1, s_b, s_a, e_a, e_b)

    def quad(c):
        pair(c)
        pair(c + 2)

    def oct_body(t, carry):
        quad(8 * t)
        quad(8 * t + 4)
        return carry

    n_octs = n_tot // 8
    lax.fori_loop(0, n_octs, oct_body, 0)
    done = 8 * n_octs

    @pl.when(n_tot - done >= 4)
    def _():
        quad(done)

    done = done + 4 * ((n_tot - done) // 4)

    @pl.when(n_tot - done >= 2)
    def _():
        pair(done)

    @pl.when(n_tot % 2 == 1)
    def _():
        stage(n_tot - 1, s_a, s_b, e_b, e_a)
        add_values(n_tot - 1, e_a)

    @pl.when((n_tot % 2 == 0) & (n_tot > 0))
    def _():
        add_values(n_tot - 1, e_b)


def _nsa_block(u, refs, neg_ref, acc_ref, l_ref, nb):
    (q_ref, kc_ref, vc_ref, ks_ref, vs_ref, kw_ref, vw_ref, gate_ref, cbias_ref, sbias_ref, wbias_ref,
     ov_ref, msel_ref) = refs
    i = pl.program_id(2) * NSA_SUB + u
    R = NSA_REP
    N = R * Q_BLOCK
    nc = kc_ref.shape[2]
    q_t = jnp.concatenate([q_ref[r, u] for r in range(R)], axis=1)
    n_win = NSA_WINDOW // Q_BLOCK

    def k_rows(ref, J0, n_tiles):
        return ref[0, pl.ds(pl.multiple_of(J0 * Q_BLOCK, Q_BLOCK), n_tiles * Q_BLOCK), :]

    def v_cols(ref, J0, n_tiles):
        return jnp.concatenate([ref[0, J0 + u] for u in range(n_tiles)], axis=1)

    def sel_mask(J0, n_tiles, n_valid=None):
        n_slc = neg_ref.shape[0] - NEG_ROWS
        parts = []
        for u in range(2 * n_tiles):
            row = 2 * J0 + u
            if n_valid is not None:
                row = jnp.where(u // 2 < n_valid, row, n_slc)
            parts.append(jnp.broadcast_to(neg_ref[pl.ds(row, 1), :], (SLC_LEN, Q_BLOCK)))
        return jnp.concatenate([jnp.concatenate(parts, axis=0)] * R, axis=1)

    def compressed(fast):
        cb = cbias_ref[0, pl.ds(pl.multiple_of(8 * (nb - 1) - 8 * i, 8), nc), :]
        s = _dot(kc_ref[0, 0], q_t) + cb
        if fast:
            e = jnp.exp2(s)
        else:
            e = jnp.where(s > 0.5 * NEG_INF, jnp.exp2(s - jnp.max(s, axis=0, keepdims=True)), 0.0)
        l = jnp.sum(e, axis=0, keepdims=True)
        p = e * jnp.where(l > 0.0, 1.0 / l, 0.0)
        return p, l, _dot(vc_ref[0, 0], p.astype(BF16))

    def select_blocks(p_c):
        p_sum = p_c[:, 0:Q_BLOCK]
        for r in range(1, R):
            p_sum = p_sum + p_c[:, r * Q_BLOCK:(r + 1) * Q_BLOCK]
        p_hi = p_sum.astype(BF16)
        p_lo = (p_sum - p_hi.astype(F32)).astype(BF16)
        imp = _dot(ov_ref[...], p_hi) + _dot(ov_ref[...], p_lo)
        n_slc = imp.shape[0]
        jj = lax.broadcasted_iota(jnp.int32, imp.shape, 0)
        qq = lax.broadcasted_iota(jnp.int32, imp.shape, 1)
        cur = 2 * i + jnp.where(qq >= SLC_LEN, 1, 0)
        forced = (jj == 0) | (jj == cur) | (jj == cur - 1)
        score = jnp.where(forced, TAKEN, jnp.where(jj <= cur, imp, -1.0))
        grp = 8
        j_rows = [jj[a:a + grp] for a in range(0, n_slc, grp)]
        for _ in range(min(SLC_TOPN, n_slc) - N_FORCED):
            cand = [(score[a:a + grp], j_rows[a // grp]) for a in range(0, n_slc, grp)]
            while len(cand) > 1:
                nxt = []
                for a in range(0, len(cand) - 1, 2):
                    (s0, j0), (s1, j1) = cand[a], cand[a + 1]
                    right = s1 > s0
                    nxt.append((jnp.maximum(s0, s1), jnp.where(right, j1, j0)))
                cand = nxt + cand[len(cand) - len(cand) % 2:]
            s8, j8 = cand[0]
            best = jnp.max(s8, axis=0, keepdims=True)
            first = jnp.min(jnp.where(s8 == best, j8, n_slc), axis=0, keepdims=True)
            score = jnp.where(jj == first, TAKEN, score)
        neg_ref[:n_slc] = jnp.where(score == TAKEN, msel_ref[0], NEG_INF)
        neg_ref[n_slc:] = jnp.full((NEG_ROWS, Q_BLOCK), NEG_INF, F32)

    def add_chunk(st, s, v_t):
        e = jnp.exp2(s)
        return st[0] + jnp.sum(e, axis=0, keepdims=True), st[1] + _dot(v_t, e.astype(BF16))

    zero = (jnp.zeros((1, N), F32), jnp.zeros((HEAD_DIM, N), F32))

    def selected_near():
        J0 = jnp.maximum(i - 1, 0)
        bias = sbias_ref[0, pl.ds(pl.multiple_of((2 - (i - J0)) * Q_BLOCK, Q_BLOCK), 2 * Q_BLOCK), :]
        return add_chunk(zero, _dot(k_rows(ks_ref, J0, 2), q_t) + bias + sel_mask(J0, 2), v_cols(vs_ref, J0, 2))

    def window_fast():
        J0 = jnp.maximum(i - n_win, 0)
        bias = wbias_ref[0, pl.ds(pl.multiple_of((n_win - (i - J0)) * Q_BLOCK, Q_BLOCK), (n_win + 1) * Q_BLOCK), :]
        return add_chunk(zero, _dot(k_rows(kw_ref, J0, n_win + 1), q_t) + bias, v_cols(vw_ref, J0, n_win + 1))

    def sweep_exact(k_ref, v_ref, bias_ref, far_dist, n_back, with_sel):
        def body(t, st):
            J = i - t
            off = pl.multiple_of((far_dist - jnp.minimum(t, far_dist)) * Q_BLOCK, Q_BLOCK)
            s = _dot(k_rows(k_ref, J, 1), q_t) + bias_ref[0, pl.ds(off, Q_BLOCK), :]
            if with_sel:
                s = s + sel_mask(J, 1)
            return _softmax_tile_update(s, v_ref[0, J], *st)

        init = (jnp.full((1, N), NEG_INF, F32),) + zero
        _, l, acc = lax.fori_loop(0, jnp.minimum(i, n_back) + 1, body, init)
        return l, acc

    vals = {}

    def load_phase():
        vals["cmp"] = compressed(True)
        vals["win"] = window_fast()

    def select_phase():
        select_blocks(vals["cmp"][0])

    def near_phase():
        l_near, acc_near = selected_near()
        l_ref[u] = jnp.broadcast_to(l_near, l_ref.shape[1:])
        acc_ref[u] = acc_near

    def exact_path():
        p_x, _, o_x = compressed(False)
        select_blocks(p_x)
        l_sx, a_sx = sweep_exact(ks_ref, vs_ref, sbias_ref, 2, nb, True)
        l_wx, a_wx = sweep_exact(kw_ref, vw_ref, wbias_ref, n_win, n_win, False)
        return o_x, a_sx * (1.0 / l_sx), a_wx * (1.0 / l_wx)

    def gate_row(br):
        return jnp.concatenate([gate_ref[br, 0, r:r + 1, u * Q_BLOCK:(u + 1) * Q_BLOCK] for r in range(R)], axis=1)

    def finish():
        _, l_c, o_c = vals["cmp"]
        l_w, a_w = vals["win"]
        l_s, a_s = l_ref[u, 0:1, :], acc_ref[u]
        lane_q = lax.broadcasted_iota(jnp.int32, (1, N), 1) % Q_BLOCK
        has_cmp = (i > 0) | (lane_q >= CMP_LEN - 1)
        l_min = jnp.minimum(jnp.minimum(l_s, l_w), jnp.where(has_cmp, l_c, 1.0))
        o_cx, o_s, o_w = lax.cond(jnp.min(l_min) > NSA_TINY,
                                  lambda: (o_c, a_s * (1.0 / l_s), a_w * (1.0 / l_w)), exact_path)
        o_t = gate_row(0) * o_cx + gate_row(1) * o_s + gate_row(2) * o_w
        return _pairs_to_rows(o_t)

    return load_phase, select_phase, near_phase, finish


def _nsa_attention(colmat, rowmat, cmp_rows, cmp_cols, gates, cbias, sbias, wbias, ov_t, msel, B, S):
    nb = S // Q_BLOCK
    nc = S // CMP_STRIDE
    n_slc = S // SLC_LEN
    G, R = NSA_KV_HEADS, NSA_REP
    kern = functools.partial(_nsa_kernel, nb=nb)
    ns = nb // NSA_SUB
    return pl.pallas_call(
        kern,
        grid=(B, G, ns),
        in_specs=[
            pl.BlockSpec((R, NSA_SUB, HEAD_DIM, LANES), lambda b, g, i: (g, b * ns + i, 0, 0)),
            pl.BlockSpec((1, 1, nc, HEAD_DIM), lambda b, g, i: (g, b, 0, 0)),
            pl.BlockSpec((1, 1, HEAD_DIM, nc), lambda b, g, i: (NSA_KV_HEADS + g, b, 0, 0)),
            pl.BlockSpec((1, S, HEAD_DIM), lambda b, g, i: (ROW_KS + g, b, 0)),
            pl.BlockSpec((1, nb, HEAD_DIM, LANES), lambda b, g, i: (COL_VS + g, b, 0, 0)),
            pl.BlockSpec((1, S, HEAD_DIM), lambda b, g, i: (ROW_KW + g, b, 0)),
            pl.BlockSpec((1, nb, HEAD_DIM, LANES), lambda b, g, i: (COL_VW + g, b, 0, 0)),
            pl.BlockSpec((N_BRANCHES, 1, R, NSA_SUB * LANES), lambda b, g, i: (0, g, 0, b * ns + i)),
            pl.BlockSpec((1, cbias.shape[1], R * Q_BLOCK), lambda b, g, i: (g, 0, 0)),
            pl.BlockSpec((1, sbias.shape[1], R * Q_BLOCK), lambda b, g, i: (g, 0, 0)),
            pl.BlockSpec((1, wbias.shape[1], R * Q_BLOCK), lambda b, g, i: (g, 0, 0)),
            pl.BlockSpec((n_slc, nc), lambda b, g, i: (0, 0)),
            pl.BlockSpec((1, 1, Q_BLOCK), lambda b, g, i: (g, 0, 0)),
        ],
        out_specs=pl.BlockSpec((1, NSA_SUB * Q_BLOCK, R * HEAD_DIM), lambda b, g, i: (b, i, g)),
        out_shape=jax.ShapeDtypeStruct((B, S, NSA_HEADS * HEAD_DIM), BF16),
        scratch_shapes=[pltpu.VMEM((NSA_SUB, n_slc + NEG_ROWS, Q_BLOCK), F32)]
        + [pltpu.VMEM((FAR_TILES * Q_BLOCK, R * Q_BLOCK), F32)] * 2
        + [pltpu.VMEM((FAR_TILES * Q_BLOCK, R * Q_BLOCK), BF16)] * 2
        + [pltpu.VMEM((NSA_SUB, HEAD_DIM, R * Q_BLOCK), F32),
           pltpu.VMEM((NSA_SUB, 8, R * Q_BLOCK), F32)],
        compiler_params=_cparams(("parallel", "parallel", "arbitrary")),
        name="nsa_attention",
    )(colmat, cmp_rows, cmp_cols, rowmat, colmat, rowmat, colmat, gates, cbias, sbias, wbias, ov_t, msel)


SB_QB = 512
SB_KT = 256
SB_ZERO_LOG = -160.0


def _sb_kernel(q_ref, k_ref, v_ref, tri_ref, o_ref):
    i = pl.program_id(1)
    H = SB_HEADS
    tri = tri_ref[...]
    assert SB_QB == 2 * SB_KT
    below_diag = (lax.broadcasted_iota(jnp.int32, (SB_KT, SB_KT), 1)
                  < lax.broadcasted_iota(jnp.int32, (SB_KT, SB_KT), 0))

    def step(J, rows, st, masked):
        out = []
        for h in range(H):
            carry, acc = st[h]
            k_t = jnp.concatenate([k_ref[h, (SB_KT // LANES) * J + c] for c in range(SB_KT // LANES)], axis=1)
            z = _dot(q_ref[h, rows, :], k_t)
            sp = jnp.maximum(z, 0.0) + jnp.log2(1.0 + jnp.exp2(-jnp.abs(z)))
            lf = jnp.where(below_diag, -sp, 0.0) if masked else -sp
            lf_hi = lf.astype(BF16)
            lf_lo = (lf - lf_hi.astype(F32)).astype(BF16)
            after = _dot(lf_hi, tri) + _dot(lf_lo, tri) + carry
            w = jnp.exp2(z - sp + after)
            if masked:
                w = jnp.where(below_diag, w, 0.0)
            v = v_ref[h, pl.ds(pl.multiple_of(J * SB_KT, SB_KT), SB_KT), :]
            acc = acc + _dot(w.astype(BF16), v)
            carry = carry + jnp.sum(lf, axis=1, keepdims=True)
            out.append((carry, acc))
        return tuple(out)

    lo, hi = slice(0, SB_KT), slice(SB_KT, SB_QB)
    zero = tuple((jnp.zeros((SB_KT, 1), F32), jnp.zeros((SB_KT, HEAD_DIM), F32)) for _ in range(H))
    st_hi = step(2 * i, hi, step(2 * i + 1, hi, zero, True), False)
    st_lo = step(2 * i, lo, zero, True)

    def alive(st):
        worst = st[0][0]
        for h in range(1, H):
            worst = jnp.maximum(worst, st[h][0])
        return jnp.max(worst) > SB_ZERO_LOG

    def sweep(rows, st):
        def body(c):
            J, _, st = c
            st = step(J, rows, st, False)
            return J - 1, alive(st), st

        return lax.while_loop(lambda c: (c[0] >= 0) & c[1], body, (2 * i - 1, alive(st), st))[2]

    st_lo, st_hi = sweep(lo, st_lo), sweep(hi, st_hi)
    o_ref[0, lo, :] = jnp.concatenate([st_lo[h][1] for h in range(H)], axis=1).astype(o_ref.dtype)
    o_ref[0, hi, :] = jnp.concatenate([st_hi[h][1] for h in range(H)], axis=1).astype(o_ref.dtype)


def _sb_attention(colmat, rowmat, tri, B, S):
    nq = S // SB_QB
    nb = S // LANES
    H = SB_HEADS
    return pl.pallas_call(
        _sb_kernel,
        grid=(B, nq),
        in_specs=[
            pl.BlockSpec((H, SB_QB, HEAD_DIM), lambda b, i: (ROW_SQ // H, b * nq + i, 0)),
            pl.BlockSpec((H, nb, HEAD_DIM, LANES), lambda b, i: (COL_SK // H, b, 0, 0)),
            pl.BlockSpec((H, S, HEAD_DIM), lambda b, i: (ROW_SV // H, b, 0)),
            pl.BlockSpec((SB_KT, SB_KT), lambda b, i: (0, 0)),
        ],
        out_specs=pl.BlockSpec((1, SB_QB, H * HEAD_DIM), lambda b, i: (b, i, 0)),
        out_shape=jax.ShapeDtypeStruct((B, S, H * HEAD_DIM), BF16),
        compiler_params=_cparams(("parallel", "arbitrary")),
        name="sb_attention",
    )(rowmat, colmat, rowmat, tri)


SWA_SUB = 4


def _swa_kernel(q_ref, k_ref, v_ref, bias_ref, sink_ref, o_ref):
    R = SWA_REP
    sink = sink_ref[0]
    outs = []
    for u in range(SWA_SUB):
        i = pl.program_id(2) * SWA_SUB + u
        q_t = jnp.concatenate([q_ref[r, u] for r in range(R)], axis=1)
        J0 = jnp.maximum(i - 1, 0)
        k = k_ref[0, pl.ds(pl.multiple_of(J0 * Q_BLOCK, Q_BLOCK), 2 * Q_BLOCK), :]
        bias = bias_ref[0, pl.ds(pl.multiple_of((1 - (i - J0)) * Q_BLOCK, Q_BLOCK), 2 * Q_BLOCK), :]
        s = _dot(k, q_t) + bias
        m = jnp.maximum(jnp.max(s, axis=0, keepdims=True), sink)
        e = jnp.exp(s - m)
        den = jnp.sum(e, axis=0, keepdims=True) + jnp.exp(sink - m)
        v_t = jnp.concatenate([v_ref[0, J0], v_ref[0, J0 + 1]], axis=1)
        o_t = _dot(v_t, e.astype(BF16)) * (1.0 / den)
        outs.append(_pairs_to_rows(o_t).astype(o_ref.dtype))
    for u in range(SWA_SUB):
        o_ref[0, u * Q_BLOCK:(u + 1) * Q_BLOCK, :] = outs[u]


def _swa_attention(colmat, rowmat, bias, sinks, B, S):
    nb = S // Q_BLOCK
    ns = nb // SWA_SUB
    G, R = SWA_KV_HEADS, SWA_REP
    return pl.pallas_call(
        _swa_kernel,
        grid=(B, G, ns),
        in_specs=[
            pl.BlockSpec((R, SWA_SUB, HEAD_DIM, LANES), lambda b, g, i: (COL_WQ // R + g, b * ns + i, 0, 0)),
            pl.BlockSpec((1, S, HEAD_DIM), lambda b, g, i: (ROW_WK + g, b, 0)),
            pl.BlockSpec((1, nb, HEAD_DIM, LANES), lambda b, g, i: (COL_WV + g, b, 0, 0)),
            pl.BlockSpec((1, 3 * Q_BLOCK, R * Q_BLOCK), lambda b, g, i: (g, 0, 0)),
            pl.BlockSpec((1, 1, R * Q_BLOCK), lambda b, g, i: (g, 0, 0)),
        ],
        out_specs=pl.BlockSpec((1, SWA_SUB * Q_BLOCK, R * HEAD_DIM), lambda b, g, i: (b, i, g)),
        out_shape=jax.ShapeDtypeStruct((B, S, SWA_HEADS * HEAD_DIM), BF16),
        compiler_params=_cparams(("parallel", "parallel", "arbitrary")),
        name="swa_attention",
    )(colmat, rowmat, colmat, bias, sinks)


MERGE_TM = 512


def _merge_kernel(x_ref, nw_ref, wg_ref, ya_ref, yb_ref, yc_ref, ua_ref, ub_ref, uc_ref, wo_ref, o_ref):
    x = x_ref[...]
    h = _rms_rows(x, nw_ref[...]).astype(BF16)
    merged = None
    for br, (y_ref, u_ref) in enumerate(((ya_ref, ua_ref), (yb_ref, ub_ref), (yc_ref, uc_ref))):
        g = jax.nn.sigmoid(_dot(h, wg_ref[:, br * D_MODEL:(br + 1) * D_MODEL]))
        t = g * _dot(y_ref[...], u_ref[...])
        merged = t if merged is None else merged + t
    o_ref[...] = x + _dot(merged.astype(BF16), wo_ref[...])


def _merge(x2, nw, w_bgate, y_nsa, y_sb, y_swa, u_nsa, u_sb, u_swa, w_out, l):
    T = x2.shape[0]
    full = lambda w: _layer_spec(l, w.shape[1:])
    rows = lambda n: pl.BlockSpec((MERGE_TM, n), lambda i: (i, 0))
    return pl.pallas_call(
        _merge_kernel,
        grid=(T // MERGE_TM,),
        in_specs=[rows(D_MODEL), pl.BlockSpec((1, D_MODEL), lambda i: (0, 0)), full(w_bgate),
                  rows(y_nsa.shape[1]), rows(y_sb.shape[1]), rows(y_swa.shape[1]),
                  full(u_nsa), full(u_sb), full(u_swa), full(w_out)],
        out_specs=rows(D_MODEL),
        out_shape=jax.ShapeDtypeStruct((T, D_MODEL), F32),
        compiler_params=_cparams(("parallel",)),
        name="merge",
    )(x2, nw.reshape(1, D_MODEL), w_bgate, y_nsa, y_sb, y_swa, u_nsa, u_sb, u_swa, w_out)


def _t5_bucket(dist):
    max_exact = NUM_BUCKETS // 2
    d = jnp.maximum(dist, 0)
    df = jnp.maximum(d, 1).astype(F32)
    large = max_exact + (jnp.log(df / max_exact) / math.log(MAX_DISTANCE / max_exact)
                         * (NUM_BUCKETS - max_exact)).astype(jnp.int32)
    large = jnp.minimum(large, NUM_BUCKETS - 1)
    return jnp.where(d < max_exact, d, large)


def _toeplitz(ext, delta):
    H = ext.shape[0]
    n = Q_BLOCK
    c = delta + MAX_DISTANCE
    w = ext[:, c - (n - 1):c + n]
    flat = jnp.pad(jnp.broadcast_to(w[:, None, :], (H, n, 2 * n - 1)), ((0, 0), (0, 0), (0, 1))).reshape(H, 2 * n * n)
    return flat[:, n - 1:n - 1 + n * (2 * n - 1)].reshape(H, n, 2 * n - 1)[:, :, :n]


def _lanes(t, G, R):
    K, Q = t.shape[1:]
    return t.reshape(G, R, K, Q).transpose(0, 2, 1, 3).reshape(G, K, R * Q)


def _bias_tables(rel_bias, nb, qk_bound):
    n = Q_BLOCK
    G, R = NSA_KV_HEADS, NSA_REP
    by_dist = rel_bias[_t5_bucket(jnp.arange(MAX_DISTANCE)), :].T.astype(F32)
    far = rel_bias[NUM_BUCKETS - 1, :].astype(F32)
    neg = jnp.full((1, n), NEG_INF, F32)
    k = jnp.arange(n)[:, None]
    q = jnp.arange(n)[None, :]
    above = (k > q)[None]

    rel = (by_dist[:NSA_HEADS] - far[:NSA_HEADS, None]) * LOG2E
    m_g = qk_bound + jnp.maximum(jnp.max(rel.reshape(G, -1), axis=1), 0.0)
    m_h = jnp.repeat(m_g, R)[:, None, None]
    ext = jnp.concatenate([jnp.broadcast_to(neg, (NSA_HEADS, n)), rel, jnp.zeros((NSA_HEADS, n), F32)], axis=1)
    t0, t1 = _toeplitz(ext, 0), _toeplitz(ext, n)
    zeros = jnp.zeros_like(t0)
    masked = jnp.full_like(t0, NEG_INF)
    sbias = _lanes(jnp.concatenate([zeros, t1, t0, masked], axis=1), G, R)
    n_win = NSA_WINDOW // n
    edge = jnp.where(above, zeros, NEG_INF)
    tiles = [edge] + [zeros] * (n_win - 2) + [t1, t0]
    wbias = _lanes(jnp.concatenate([t - m_h for t in tiles] + [masked] * n_win, axis=1), G, R)
    off = 8 * (nb - 1)
    near = jnp.stack([ext[:, 97 - CMP_STRIDE * m:97 - CMP_STRIDE * m + n] for m in range(-9, 7)], axis=1)
    n_rows = 16 * nb - 8
    cb = jnp.concatenate([jnp.zeros((NSA_HEADS, off - 9, n), F32), near,
                          jnp.full((NSA_HEADS, n_rows - off - 7, n), NEG_INF, F32)], axis=1) - m_h
    cbias = _lanes(cb, G, R)
    msel = jnp.broadcast_to(-m_g[:, None, None], (G, 1, n))

    hs = slice(NSA_HEADS, NSA_HEADS + SWA_HEADS)
    ext_w = jnp.concatenate([jnp.broadcast_to(neg, (SWA_HEADS, n)), by_dist[hs],
                             jnp.broadcast_to(far[hs, None], (SWA_HEADS, n))], axis=1)
    s0 = _toeplitz(ext_w, 0)
    s1 = jnp.where(above, _toeplitz(ext_w, n), NEG_INF)
    swa_bias = _lanes(jnp.concatenate([s1, s0, jnp.full_like(s0, NEG_INF)], axis=1), SWA_KV_HEADS, SWA_REP)
    return cbias, sbias, wbias, msel, swa_bias


def _overlap_t(S):
    nc, n_slc = S // CMP_STRIDE, S // SLC_LEN
    n = np.arange(nc)[None, :]
    j = np.arange(n_slc)[:, None]
    ov = (n * CMP_STRIDE < (j + 1) * SLC_LEN) & (n * CMP_STRIDE + CMP_LEN - 1 >= j * SLC_LEN) & (n < nc - 1)
    return jnp.asarray(ov, BF16)


def _head_gains(heads, norm_of, scale_of):
    ones = jnp.ones((HEAD_DIM,), F32)
    gains = [norm_of[name].astype(F32) * scale_of.get(name, 1.0) if name in norm_of else ones for name, _ in heads]
    return jnp.concatenate(gains).reshape(1, -1)


def kernel(x, rel_bias, ffn1_norm, ffn1_w_gate, ffn1_w_up, ffn1_w_down, mix_norm, w_in, nsa_q_norm, nsa_k_norm, nsa_cmp_pos, nsa_cmp_k_w1, nsa_cmp_k_w2, nsa_cmp_v_w1, nsa_cmp_v_w2, swa_q_norm, swa_k_norm, swa_sinks, w_up_nsa, w_up_sb, w_up_swa, w_out, ffn2_norm, ffn2_w_gate, ffn2_w_up, ffn2_w_down):
    B, S, D = x.shape
    T = B * S
    nb = S // Q_BLOCK
    nc = S // CMP_STRIDE
    depth = w_in.shape[0]
    scale = HEAD_DIM ** -0.5
    scale_of = {"nq": scale * LOG2E, "sq": scale * LOG2E, "wq": scale}
    qk_bound = (BOUND_SLACK * LOG2E * HEAD_DIM ** 0.5
                * jnp.max(jnp.abs(nsa_q_norm.astype(F32))) * jnp.max(jnp.abs(nsa_k_norm.astype(F32))))
    cbias, sbias, wbias, msel, swa_bias = _bias_tables(rel_bias, nb, qk_bound)
    plan = _proj_plan(scale_of)
    ov_t = _overlap_t(S)
    tri = jnp.asarray(np.tril(np.ones((SB_KT, SB_KT), np.float32), -1), BF16)

    (ffn1_w_gate, ffn1_w_up, ffn1_w_down, ffn2_w_gate, ffn2_w_up, ffn2_w_down, w_up_nsa, w_up_sb, w_up_swa,
     w_out) = (w.astype(BF16) for w in (ffn1_w_gate, ffn1_w_up, ffn1_w_down, ffn2_w_gate, ffn2_w_up, ffn2_w_down,
                                        w_up_nsa, w_up_sb, w_up_swa, w_out))
    tail0 = OFF_NGATE + N_GATE
    w_head, w_tail = w_in[:, :, :OFF_NGATE].astype(BF16), w_in[:, :, tail0:].astype(BF16)
    w_bgate = w_tail[:, :, OFF_BGATE - tail0:]
    w_gate = jnp.pad(w_in[:, :, OFF_NGATE:tail0], ((0, 0), (0, 0), (0, LANES - N_GATE))).astype(BF16)
    runs = []
    for _, off in ROW_HEADS + COL_HEADS:
        if runs and runs[-1][1] == off:
            runs[-1][1] = off + HEAD_DIM
        else:
            runs.append([off, off + HEAD_DIM])
    pieces = [w_head[:, :, a:b] if a < OFF_NGATE else w_tail[:, :, a - tail0:b - tail0] for a, b in runs]
    w_proj = jnp.concatenate(pieces + [w_gate], axis=2)

    x2 = x.reshape(T, D)
    for l in range(depth):
        x2 = _ffn(x2, ffn1_norm[l], ffn1_w_gate, ffn1_w_up, ffn1_w_down, l)

        norm_of = {"nq": nsa_q_norm[l], "nks": nsa_k_norm[l], "nkw": nsa_k_norm[l],
                   "wq": swa_q_norm[l], "wk": swa_k_norm[l]}
        gain = jnp.concatenate([_head_gains(ROW_HEADS, norm_of, scale_of), _head_gains(COL_HEADS, norm_of, scale_of),
                                jnp.ones((1, LANES), F32)], axis=1)
        rowmat, colmat, gates = _proj(x2, mix_norm[l], w_proj, gain, plan, l)
        gates = gates[:N_GATE].reshape(N_BRANCHES, NSA_KV_HEADS, NSA_REP, T)

        rows4 = rowmat[ROW_KC:ROW_KC + 2 * NSA_KV_HEADS].reshape(2 * NSA_KV_HEADS, B, nc, CMP_STRIDE * HEAD_DIM)
        cmp_rows, cmp_cols = _compress(rows4, jnp.stack([nsa_cmp_k_w1[l], nsa_cmp_v_w1[l]]),
                                       jnp.stack([nsa_cmp_k_w2[l], nsa_cmp_v_w2[l]]),
                                       nsa_cmp_pos[l], nsa_k_norm[l])

        y_nsa = _nsa_attention(colmat, rowmat, cmp_rows, cmp_cols, gates, cbias, sbias, wbias, ov_t, msel, B, S)
        y_sb = _sb_attention(colmat, rowmat, tri, B, S)
        sinks = jnp.repeat(swa_sinks[l].astype(F32).reshape(SWA_KV_HEADS, 1, SWA_REP), Q_BLOCK, axis=2)
        y_swa = _swa_attention(colmat, rowmat, swa_bias, sinks, B, S)

        x2 = _merge(x2, mix_norm[l], w_bgate, y_nsa.reshape(T, -1), y_sb.reshape(T, -1),
                    y_swa.reshape(T, -1), w_up_nsa, w_up_sb, w_up_swa, w_out, l)
        x2 = _ffn(x2, ffn2_norm[l], ffn2_w_gate, ffn2_w_up, ffn2_w_down, l)
    return x2.reshape(B, S, D)
```

```python
import functools
import math

import numpy as np
import jax
import jax.numpy as jnp
from jax import lax
from jax.experimental import pallas as pl
from jax.experimental.pallas import tpu as pltpu

D_MODEL = 1024
HEAD_DIM = 64
Q_BLOCK = 128
NSA_HEADS = 8
NSA_KV_HEADS = 2
NSA_REP = NSA_HEADS // NSA_KV_HEADS
CMP_LEN = 32
CMP_STRIDE = 16
CMP_HIDDEN = 256
SLC_LEN = 64
SLC_TOPN = 16
NSA_WINDOW = 512
SB_HEADS = 4
SWA_HEADS = 4
SWA_KV_HEADS = 2
SWA_REP = SWA_HEADS // SWA_KV_HEADS
SWA_WINDOW = 128
NUM_BUCKETS = 32
MAX_DISTANCE = 128
D_FF = 2816
NORM_EPS = 1e-6
NEG_INF = -1e30
N_BRANCHES = 3
N_GATE = 3 * NSA_HEADS

LANES = 128
VMEM_LIMIT = 56 * 1024 * 1024
BF16 = jnp.bfloat16
F32 = jnp.float32

_sizes = (NSA_HEADS * HEAD_DIM,) + (NSA_KV_HEADS * HEAD_DIM,) * 6 + (N_GATE,) \
    + (SB_HEADS * HEAD_DIM,) * 3 + (SWA_HEADS * HEAD_DIM, SWA_KV_HEADS * HEAD_DIM,
                                    SWA_KV_HEADS * HEAD_DIM, N_BRANCHES * D_MODEL)
_offs = np.concatenate([[0], np.cumsum(_sizes)])
(OFF_NQ, OFF_NKC, OFF_NVC, OFF_NKS, OFF_NVS, OFF_NKW, OFF_NVW, OFF_NGATE,
 OFF_SQ, OFF_SK, OFF_SV, OFF_WQ, OFF_WK, OFF_WV, OFF_BGATE) = (int(o) for o in _offs[:-1])

ROW_HEADS = ([("nkc", OFF_NKC + 64 * g) for g in range(2)] + [("nvc", OFF_NVC + 64 * g) for g in range(2)]
             + [("nks", OFF_NKS + 64 * g) for g in range(2)] + [("nkw", OFF_NKW + 64 * g) for g in range(2)]
             + [("sq", OFF_SQ + 64 * h) for h in range(4)] + [("sv", OFF_SV + 64 * h) for h in range(4)]
             + [("wk", OFF_WK + 64 * g) for g in range(2)])
ROW_KC, ROW_VC, ROW_KS, ROW_KW, ROW_SQ, ROW_SV, ROW_WK = 0, 2, 4, 6, 8, 12, 16
COL_HEADS = ([("nq", OFF_NQ + 64 * h) for h in range(8)] + [("nvs", OFF_NVS + 64 * g) for g in range(2)]
             + [("nvw", OFF_NVW + 64 * g) for g in range(2)] + [("sk", OFF_SK + 64 * h) for h in range(4)]
             + [("wq", OFF_WQ + 64 * h) for h in range(4)] + [("wv", OFF_WV + 64 * g) for g in range(2)])
COL_NQ, COL_VS, COL_VW, COL_SK, COL_WQ, COL_WV = 0, 8, 10, 12, 16, 20


def _dot(a, b):
    return jnp.dot(a, b, preferred_element_type=F32)


def _rms_rows(x, w):
    ms = jnp.mean(x * x, axis=-1, keepdims=True)
    return x * lax.rsqrt(ms + NORM_EPS) * w


def _cparams(sem, flags=None):
    return pltpu.CompilerParams(dimension_semantics=sem, vmem_limit_bytes=VMEM_LIMIT, flags=flags)


FFN_TM = 512
MXU_TILE = 256
FFN_FC = MXU_TILE


def _ffn_kernel(x_ref, nw_ref, wg_ref, wu_ref, wd_ref, o_ref):
    x = x_ref[...]
    h = _rms_rows(x, nw_ref[...]).astype(BF16)
    acc = jnp.zeros(x.shape, F32)
    for c in range(D_FF // FFN_FC):
        cols = slice(c * FFN_FC, (c + 1) * FFN_FC)
        g = _dot(h, wg_ref[:, cols])
        u = _dot(h, wu_ref[:, cols])
        a = (g * jax.nn.sigmoid(g)) * u
        acc = acc + _dot(a.astype(BF16), wd_ref[cols, :])
    o_ref[...] = x + 0.5 * acc


def _layer_spec(l, shape, **kw):
    return pl.BlockSpec((None,) + tuple(shape), lambda i: (l, 0, 0), **kw)


def _ffn(x2, nw, wg, wu, wd, l):
    T = x2.shape[0]
    resident = lambda shape: _layer_spec(l, shape, pipeline_mode=pl.Buffered(1))
    return pl.pallas_call(
        _ffn_kernel,
        grid=(T // FFN_TM,),
        in_specs=[
            pl.BlockSpec((FFN_TM, D_MODEL), lambda i: (i, 0)),
            pl.BlockSpec((1, D_MODEL), lambda i: (0, 0)),
            resident((D_MODEL, D_FF)), resident((D_MODEL, D_FF)), resident((D_FF, D_MODEL)),
        ],
        out_specs=pl.BlockSpec((FFN_TM, D_MODEL), lambda i: (i, 0)),
        out_shape=jax.ShapeDtypeStruct((T, D_MODEL), F32),
        compiler_params=_cparams(("parallel",)),
        name="ffn",
    )(x2, nw.reshape(1, D_MODEL), wg, wu, wd)


PROJ_TM = 512


GATE_ROWS = 32

_NORMED = ("nq", "nks", "nkw", "wq", "wk")


def _proj_plan(scale_of):
    plan = []
    for kind, heads in (("row", ROW_HEADS), ("col", COL_HEADS)):
        for p in range(len(heads) // 2):
            name = heads[2 * p][0]
            assert heads[2 * p + 1][0] == name
            plan.append((kind, p, name in _NORMED, 1.0 if name in _NORMED else scale_of.get(name, 1.0)))
    plan.append(("gate", 0, False, 1.0))
    return tuple(plan)


def _proj_kernel(x_ref, nw_ref, w_ref, gain_ref, bd_ref, row_ref, col_ref, gate_ref, *, plan):
    h = _rms_rows(x_ref[...], nw_ref[...]).astype(BF16)
    pair = 2 * MXU_TILE
    for c0 in range(0, len(plan) * LANES, pair):
        width = min(pair, len(plan) * LANES - c0)
        zg = _dot(h, w_ref[:, c0:c0 + width])
        for u in range(width // LANES):
            t = c0 // LANES + u
            kind, p, normed, scale = plan[t]
            z = zg[:, u * LANES:(u + 1) * LANES]
            if normed:
                ms = _dot((z * z).astype(BF16), bd_ref[...])
                z = z * lax.rsqrt(ms + NORM_EPS) * gain_ref[:, t * LANES:(t + 1) * LANES]
            elif scale != 1.0:
                z = z * scale
            if kind == "row":
                row_ref[2 * p] = z[:, :HEAD_DIM].astype(row_ref.dtype)
                row_ref[2 * p + 1] = z[:, HEAD_DIM:].astype(row_ref.dtype)
            elif kind == "col":
                z_t = z.T
                for c in range(PROJ_TM // LANES):
                    col_ref[2 * p, c] = z_t[:HEAD_DIM, c * LANES:(c + 1) * LANES].astype(col_ref.dtype)
                    col_ref[2 * p + 1, c] = z_t[HEAD_DIM:, c * LANES:(c + 1) * LANES].astype(col_ref.dtype)
            else:
                gate_ref[...] = jax.nn.sigmoid(z).T[:GATE_ROWS]


def _proj(x2, nw, w, gain, plan, l):
    T = x2.shape[0]
    n_row, n_col = len(ROW_HEADS), len(COL_HEADS)
    bd = np.kron(np.eye(2, dtype=np.float32), np.full((HEAD_DIM, HEAD_DIM), 1.0 / HEAD_DIM, np.float32))
    return pl.pallas_call(
        functools.partial(_proj_kernel, plan=plan),
        grid=(T // PROJ_TM,),
        in_specs=[
            pl.BlockSpec((PROJ_TM, D_MODEL), lambda i: (i, 0)),
            pl.BlockSpec((1, D_MODEL), lambda i: (0, 0)),
            _layer_spec(l, w.shape[1:]),
            pl.BlockSpec(gain.shape, lambda i: (0, 0)),
            pl.BlockSpec((LANES, LANES), lambda i: (0, 0)),
        ],
        out_specs=[
            pl.BlockSpec((n_row, PROJ_TM, HEAD_DIM), lambda i: (0, i, 0)),
            pl.BlockSpec((n_col, PROJ_TM // LANES, HEAD_DIM, LANES), lambda i: (0, i, 0, 0)),
            pl.BlockSpec((GATE_ROWS, PROJ_TM), lambda i: (0, i)),
        ],
        out_shape=[jax.ShapeDtypeStruct((n_row, T, HEAD_DIM), BF16),
                   jax.ShapeDtypeStruct((n_col, T // LANES, HEAD_DIM, LANES), BF16),
                   jax.ShapeDtypeStruct((GATE_ROWS, T), F32)],
        compiler_params=_cparams(("parallel",)),
        name="proj",
    )(x2, nw.reshape(1, D_MODEL), w, gain, jnp.asarray(bd, BF16))


def _compress_kernel(r_ref, w1_ref, w2_ref, pos_ref, kn_ref, row_ref, col_ref):
    kind = pl.program_id(0) // NSA_KV_HEADS
    r = r_ref[0, 0]
    nc = r.shape[0]
    half = CMP_STRIDE * HEAD_DIM
    w1 = w1_ref[0]
    p_lo = _dot(r, w1[:half])
    p_hi = _dot(r, w1[half:])
    p_pos = _dot(jnp.broadcast_to(pos_ref[...], (8, CMP_LEN * HEAD_DIM)).astype(BF16), w1)[:1]
    hdn = p_lo + pltpu.roll(p_hi, nc - 1, 0) + p_pos
    hdn = hdn * jax.nn.sigmoid(hdn)
    out = _dot(hdn.astype(BF16), w2_ref[0])
    ms = jnp.sum(out * out, axis=-1, keepdims=True) * (1.0 / HEAD_DIM)
    normed = out * lax.rsqrt(ms + NORM_EPS) * kn_ref[...]
    out = jnp.where(kind == 0, normed, out)
    row_ref[0, 0] = out[:, :HEAD_DIM].astype(row_ref.dtype)
    col_ref[0, 0] = out.T[:HEAD_DIM].astype(col_ref.dtype)


def _compress(rows4, w1, w2, pos, k_norm):
    assert ROW_KC == 0 and ROW_VC == NSA_KV_HEADS
    _, B, NC, _ = rows4.shape
    kn = jnp.pad(k_norm.reshape(1, HEAD_DIM), ((0, 0), (0, LANES - HEAD_DIM)))
    w2p = jnp.pad(w2, ((0, 0), (0, 0), (0, LANES - HEAD_DIM))).astype(BF16)
    return pl.pallas_call(
        _compress_kernel,
        grid=(4, B),
        in_specs=[
            pl.BlockSpec((1, 1, NC, CMP_STRIDE * HEAD_DIM), lambda h, b: (h, b, 0, 0)),
            pl.BlockSpec((1, CMP_LEN * HEAD_DIM, CMP_HIDDEN), lambda h, b: (h // NSA_KV_HEADS, 0, 0)),
            pl.BlockSpec((1, CMP_HIDDEN, LANES), lambda h, b: (h // NSA_KV_HEADS, 0, 0)),
            pl.BlockSpec((1, CMP_LEN * HEAD_DIM), lambda h, b: (0, 0)),
            pl.BlockSpec((1, LANES), lambda h, b: (0, 0)),
        ],
        out_specs=[
            pl.BlockSpec((1, 1, NC, HEAD_DIM), lambda h, b: (h, b, 0, 0)),
            pl.BlockSpec((1, 1, HEAD_DIM, NC), lambda h, b: (h, b, 0, 0)),
        ],
        out_shape=[jax.ShapeDtypeStruct((4, B, NC, HEAD_DIM), BF16),
                   jax.ShapeDtypeStruct((4, B, HEAD_DIM, NC), BF16)],
        compiler_params=_cparams(("parallel", "parallel")),
        name="nsa_compress",
    )(rows4, w1.astype(BF16), w2p, pos.reshape(1, CMP_LEN * HEAD_DIM), kn)


def _softmax_tile_update(s, v_t, m, l, acc):
    m_new = jnp.maximum(m, jnp.max(s, axis=0, keepdims=True))
    alpha = jnp.exp2(m - m_new)
    p = jnp.exp2(s - m_new)
    l_new = alpha * l + jnp.sum(p, axis=0, keepdims=True)
    acc_new = alpha * acc + _dot(v_t, p.astype(BF16))
    return m_new, l_new, acc_new


def _pairs_to_rows(o_t):
    n = o_t.shape[1] // LANES
    outs = []
    for p in range(n // 2):
        a = o_t[:, (2 * p) * LANES:(2 * p + 1) * LANES]
        b = o_t[:, (2 * p + 1) * LANES:(2 * p + 2) * LANES]
        outs.append(jnp.concatenate([a, b], axis=0).T)
    return outs[0] if len(outs) == 1 else jnp.concatenate(outs, axis=1)


FAR_TILES = 4
NSA_SUB = 8
TAKEN = -2.0
NEG_ROWS = 8
N_FORCED = 3
LOG2E = math.log2(math.e)
NSA_TINY = 2.0 ** -100
BOUND_SLACK = 1.05


def _nsa_kernel(q_ref, kc_ref, vc_ref, ks_ref, vs_ref, kw_ref, vw_ref, gate_ref,
                cbias_ref, sbias_ref, wbias_ref, ov_ref, msel_ref, o_ref,
                neg_all, s_a, s_b, e_a, e_b, acc_ref, l_ref, *, nb):
    refs = (q_ref, kc_ref, vc_ref, ks_ref, vs_ref, kw_ref, vw_ref, gate_ref, cbias_ref, sbias_ref, wbias_ref,
            ov_ref, msel_ref)
    blocks = [_nsa_block(u, refs, neg_all.at[u], acc_ref, l_ref, nb) for u in range(NSA_SUB)]
    for phase in range(3):
        for blk in blocks:
            blk[phase]()
    _far_sweep(q_ref, ks_ref, vs_ref, neg_all, (s_a, s_b, e_a, e_b), acc_ref, l_ref, nb)
    fast = [blk[3]() for blk in blocks]
    l_min = fast[0][0]
    for l_u, _ in fast[1:]:
        l_min = jnp.minimum(l_min, l_u)
    outs = lax.cond(jnp.min(l_min) > NSA_TINY, lambda: tuple(o for _, o in fast),
                    lambda: tuple(blk[4]() for blk in blocks))
    for u, blk in enumerate(blocks):
        o_ref[0, u * Q_BLOCK:(u + 1) * Q_BLOCK, :] = blk[5](outs[u]).astype(o_ref.dtype)


def _far_sweep(q_ref, ks_ref, vs_ref, neg_all, stage_refs, acc_ref, l_ref, nb):
    R = NSA_REP
    n_slc = neg_all.shape[1] - NEG_ROWS
    i0 = pl.program_id(2) * NSA_SUB
    n_far = [jnp.maximum(i0 + u - 1, 0) for u in range(NSA_SUB)]
    offs = [0]
    for u in range(NSA_SUB):
        offs.append(offs[-1] + (n_far[u] + FAR_TILES - 1) // FAR_TILES)
    n_tot = offs[-1]

    def locate(c):
        u = sum(jnp.where(c >= offs[k], 1, 0) for k in range(1, NSA_SUB))
        start = sum(jnp.where(u == k, offs[k], 0) for k in range(NSA_SUB))
        far = sum(jnp.where(u == k, n_far[k], 0) for k in range(NSA_SUB))
        tile = FAR_TILES * (c - start)
        return u, jnp.clip(tile, 0, nb - FAR_TILES), far - tile

    def k_rows(J0):
        return ks_ref[0, pl.ds(pl.multiple_of(J0 * Q_BLOCK, Q_BLOCK), FAR_TILES * Q_BLOCK), :]

    def v_cols(J0):
        return jnp.concatenate([vs_ref[0, J0 + t] for t in range(FAR_TILES)], axis=1)

    def q_cols(u):
        return jnp.concatenate([q_ref[r, u] for r in range(R)], axis=1)

    def sel_mask(u, J0, n_valid):
        parts = []
        for t in range(2 * FAR_TILES):
            row = jnp.where(t // 2 < n_valid, 2 * J0 + t, n_slc)
            parts.append(jnp.broadcast_to(neg_all[u, pl.ds(row, 1), :], (SLC_LEN, Q_BLOCK)))
        return jnp.concatenate([jnp.concatenate(parts, axis=0)] * R, axis=1)

    def add_values(c, e_ref):
        u, tile, _ = locate(c)
        acc_ref[u] = acc_ref[u] + _dot(v_cols(tile), e_ref[...])

    def stage(c, s_in, s_out, e_in, e_out):
        u0, tile0, _ = locate(c - 1)
        u1, tile1, valid1 = locate(c)
        u2, tile2, _ = locate(c + 1)
        v_prev, acc_prev = v_cols(tile0), acc_ref[u0]
        mask, l_cur = sel_mask(u1, tile1, valid1), l_ref[u1]
        k_next, q_next = k_rows(tile2), q_cols(u2)
        pv = _dot(v_prev, e_in[...])
        s_out[...] = _dot(k_next, q_next)
        e = jnp.exp2(s_in[...] + mask)
        e_out[...] = e.astype(BF16)
        acc_ref[u0] = acc_prev + pv
        l_ref[u1] = l_cur + jnp.sum(e, axis=0, keepdims=True)

    s_a, s_b, e_a, e_b = stage_refs
    e_b[...] = jnp.zeros_like(e_b)
    u_first, tile_first, _ = locate(0)
    s_a[...] = _dot(k_rows(tile_first), q_cols(u_first))

    def pair(c):
        stage(c, s_a, s_b, e_b, e_a)
        stage(c + 1, s_b, s_a, e_a, e_b)

    def quad(c):
        pair(c)
        pair(c + 2)

    def oct_body(t, carry):
        quad(8 * t)
        quad(8 * t + 4)
        return carry

    n_octs = n_tot // 8
    lax.fori_loop(0, n_octs, oct_body, 0)
    done = 8 * n_octs

    @pl.when(n_tot - done >= 4)
    def _():
        quad(done)

    done = done + 4 * ((n_tot - done) // 4)

    @pl.when(n_tot - done >= 2)
    def _():
        pair(done)

    @pl.when(n_tot % 2 == 1)
    def _():
        stage(n_tot - 1, s_a, s_b, e_b, e_a)
        add_values(n_tot - 1, e_a)

    @pl.when((n_tot % 2 == 0) & (n_tot > 0))
    def _():
        add_values(n_tot - 1, e_b)


def _nsa_block(u, refs, neg_ref, acc_ref, l_ref, nb):
    (q_ref, kc_ref, vc_ref, ks_ref, vs_ref, kw_ref, vw_ref, gate_ref, cbias_ref, sbias_ref, wbias_ref,
     ov_ref, msel_ref) = refs
    i = pl.program_id(2) * NSA_SUB + u
    R = NSA_REP
    N = R * Q_BLOCK
    nc = kc_ref.shape[2]
    q_t = jnp.concatenate([q_ref[r, u] for r in range(R)], axis=1)
    n_win = NSA_WINDOW // Q_BLOCK

    def k_rows(ref, J0, n_tiles):
        return ref[0, pl.ds(pl.multiple_of(J0 * Q_BLOCK, Q_BLOCK), n_tiles * Q_BLOCK), :]

    def v_cols(ref, J0, n_tiles):
        return jnp.concatenate([ref[0, J0 + u] for u in range(n_tiles)], axis=1)

    def sel_mask(J0, n_tiles, n_valid=None):
        n_slc = neg_ref.shape[0] - NEG_ROWS
        parts = []
        for u in range(2 * n_tiles):
            row = 2 * J0 + u
            if n_valid is not None:
                row = jnp.where(u // 2 < n_valid, row, n_slc)
            parts.append(jnp.broadcast_to(neg_ref[pl.ds(row, 1), :], (SLC_LEN, Q_BLOCK)))
        return jnp.concatenate([jnp.concatenate(parts, axis=0)] * R, axis=1)

    def compressed(fast):
        cb = cbias_ref[0, pl.ds(pl.multiple_of(8 * (nb - 1) - 8 * i, 8), nc), :]
        s = _dot(kc_ref[0, 0], q_t) + cb
        if fast:
            e = jnp.exp2(s)
        else:
            e = jnp.where(s > 0.5 * NEG_INF, jnp.exp2(s - jnp.max(s, axis=0, keepdims=True)), 0.0)
        l = jnp.sum(e, axis=0, keepdims=True)
        p = e * jnp.where(l > 0.0, 1.0 / l, 0.0)
        return p, l, _dot(vc_ref[0, 0], p.astype(BF16))

    def select_blocks(p_c):
        p_sum = p_c[:, 0:Q_BLOCK]
        for r in range(1, R):
            p_sum = p_sum + p_c[:, r * Q_BLOCK:(r + 1) * Q_BLOCK]
        p_hi = p_sum.astype(BF16)
        p_lo = (p_sum - p_hi.astype(F32)).astype(BF16)
        imp = _dot(ov_ref[...], p_hi) + _dot(ov_ref[...], p_lo)
        n_slc = imp.shape[0]
        jj = lax.broadcasted_iota(jnp.int32, imp.shape, 0)
        qq = lax.broadcasted_iota(jnp.int32, imp.shape, 1)
        cur = 2 * i + jnp.where(qq >= SLC_LEN, 1, 0)
        forced = (jj == 0) | (jj == cur) | (jj == cur - 1)
        score = jnp.where(forced, TAKEN, jnp.where(jj <= cur, imp, -1.0))
        grp = 8
        j_rows = [jj[a:a + grp] for a in range(0, n_slc, grp)]
        for _ in range(min(SLC_TOPN, n_slc) - N_FORCED):
            cand = [(score[a:a + grp], j_rows[a // grp]) for a in range(0, n_slc, grp)]
            while len(cand) > 1:
                nxt = []
                for a in range(0, len(cand) - 1, 2):
                    (s0, j0), (s1, j1) = cand[a], cand[a + 1]
                    right = s1 > s0
                    nxt.append((jnp.maximum(s0, s1), jnp.where(right, j1, j0)))
                cand = nxt + cand[len(cand) - len(cand) % 2:]
            s8, j8 = cand[0]
            best = jnp.max(s8, axis=0, keepdims=True)
            first = jnp.min(jnp.where(s8 == best, j8, n_slc), axis=0, keepdims=True)
            score = jnp.where(jj == first, TAKEN, score)
        neg_ref[:n_slc] = jnp.where(score == TAKEN, msel_ref[0], NEG_INF)
        neg_ref[n_slc:] = jnp.full((NEG_ROWS, Q_BLOCK), NEG_INF, F32)

    def add_chunk(st, s, v_t):
        e = jnp.exp2(s)
        return st[0] + jnp.sum(e, axis=0, keepdims=True), st[1] + _dot(v_t, e.astype(BF16))

    zero = (jnp.zeros((1, N), F32), jnp.zeros((HEAD_DIM, N), F32))

    def selected_near():
        J0 = jnp.maximum(i - 1, 0)
        bias = sbias_ref[0, pl.ds(pl.multiple_of((2 - (i - J0)) * Q_BLOCK, Q_BLOCK), 2 * Q_BLOCK), :]
        return add_chunk(zero, _dot(k_rows(ks_ref, J0, 2), q_t) + bias + sel_mask(J0, 2), v_cols(vs_ref, J0, 2))

    def window_fast():
        J0 = jnp.maximum(i - n_win, 0)
        bias = wbias_ref[0, pl.ds(pl.multiple_of((n_win - (i - J0)) * Q_BLOCK, Q_BLOCK), (n_win + 1) * Q_BLOCK), :]
        return add_chunk(zero, _dot(k_rows(kw_ref, J0, n_win + 1), q_t) + bias, v_cols(vw_ref, J0, n_win + 1))

    def sweep_exact(k_ref, v_ref, bias_ref, far_dist, n_back, with_sel):
        def body(t, st):
            J = i - t
            off = pl.multiple_of((far_dist - jnp.minimum(t, far_dist)) * Q_BLOCK, Q_BLOCK)
            s = _dot(k_rows(k_ref, J, 1), q_t) + bias_ref[0, pl.ds(off, Q_BLOCK), :]
            if with_sel:
                s = s + sel_mask(J, 1)
            return _softmax_tile_update(s, v_ref[0, J], *st)

        init = (jnp.full((1, N), NEG_INF, F32),) + zero
        _, l, acc = lax.fori_loop(0, jnp.minimum(i, n_back) + 1, body, init)
        return l, acc

    vals = {}

    def load_phase():
        vals["cmp"] = compressed(True)
        vals["win"] = window_fast()

    def select_phase():
        select_blocks(vals["cmp"][0])

    def near_phase():
        l_near, acc_near = selected_near()
        l_ref[u] = jnp.broadcast_to(l_near, l_ref.shape[1:])
        acc_ref[u] = acc_near

    def exact_path():
        p_x, _, o_x = compressed(False)
        select_blocks(p_x)
        l_sx, a_sx = sweep_exact(ks_ref, vs_ref, sbias_ref, 2, nb, True)
        l_wx, a_wx = sweep_exact(kw_ref, vw_ref, wbias_ref, n_win, n_win, False)
        return o_x, a_sx * (1.0 / l_sx), a_wx * (1.0 / l_wx)

    def gate_row(br):
        return jnp.concatenate([gate_ref[br, 0, r:r + 1, u * Q_BLOCK:(u + 1) * Q_BLOCK] for r in range(R)], axis=1)

    def fast_result():
        _, l_c, o_c = vals["cmp"]
        l_w, a_w = vals["win"]
        l_s, a_s = l_ref[u, 0:1, :], acc_ref[u]
        lane_q = lax.broadcasted_iota(jnp.int32, (1, N), 1) % Q_BLOCK
        has_cmp = (i > 0) | (lane_q >= CMP_LEN - 1)
        l_min = jnp.minimum(jnp.minimum(l_s, l_w), jnp.where(has_cmp, l_c, 1.0))
        return l_min, (o_c, a_s * (1.0 / l_s), a_w * (1.0 / l_w))

    def combine(outs):
        o_c, o_s, o_w = outs
        o_t = gate_row(0) * o_c + gate_row(1) * o_s + gate_row(2) * o_w
        return _pairs_to_rows(o_t)

    return load_phase, select_phase, near_phase, fast_result, exact_path, combine


def _nsa_attention(colmat, rowmat, cmp_rows, cmp_cols, gates, cbias, sbias, wbias, ov_t, msel, B, S):
    nb = S // Q_BLOCK
    nc = S // CMP_STRIDE
    n_slc = S // SLC_LEN
    G, R = NSA_KV_HEADS, NSA_REP
    kern = functools.partial(_nsa_kernel, nb=nb)
    ns = nb // NSA_SUB
    return pl.pallas_call(
        kern,
        grid=(B, G, ns),
        in_specs=[
            pl.BlockSpec((R, NSA_SUB, HEAD_DIM, LANES), lambda b, g, i: (g, b * ns + i, 0, 0)),
            pl.BlockSpec((1, 1, nc, HEAD_DIM), lambda b, g, i: (g, b, 0, 0)),
            pl.BlockSpec((1, 1, HEAD_DIM, nc), lambda b, g, i: (NSA_KV_HEADS + g, b, 0, 0)),
            pl.BlockSpec((1, S, HEAD_DIM), lambda b, g, i: (ROW_KS + g, b, 0)),
            pl.BlockSpec((1, nb, HEAD_DIM, LANES), lambda b, g, i: (COL_VS + g, b, 0, 0)),
            pl.BlockSpec((1, S, HEAD_DIM), lambda b, g, i: (ROW_KW + g, b, 0)),
            pl.BlockSpec((1, nb, HEAD_DIM, LANES), lambda b, g, i: (COL_VW + g, b, 0, 0)),
            pl.BlockSpec((N_BRANCHES, 1, R, NSA_SUB * LANES), lambda b, g, i: (0, g, 0, b * ns + i)),
            pl.BlockSpec((1, cbias.shape[1], R * Q_BLOCK), lambda b, g, i: (g, 0, 0)),
            pl.BlockSpec((1, sbias.shape[1], R * Q_BLOCK), lambda b, g, i: (g, 0, 0)),
            pl.BlockSpec((1, wbias.shape[1], R * Q_BLOCK), lambda b, g, i: (g, 0, 0)),
            pl.BlockSpec((n_slc, nc), lambda b, g, i: (0, 0)),
            pl.BlockSpec((1, 1, Q_BLOCK), lambda b, g, i: (g, 0, 0)),
        ],
        out_specs=pl.BlockSpec((1, NSA_SUB * Q_BLOCK, R * HEAD_DIM), lambda b, g, i: (b, i, g)),
        out_shape=jax.ShapeDtypeStruct((B, S, NSA_HEADS * HEAD_DIM), BF16),
        scratch_shapes=[pltpu.VMEM((NSA_SUB, n_slc + NEG_ROWS, Q_BLOCK), F32)]
        + [pltpu.VMEM((FAR_TILES * Q_BLOCK, R * Q_BLOCK), F32)] * 2
        + [pltpu.VMEM((FAR_TILES * Q_BLOCK, R * Q_BLOCK), BF16)] * 2
        + [pltpu.VMEM((NSA_SUB, HEAD_DIM, R * Q_BLOCK), F32),
           pltpu.VMEM((NSA_SUB, 8, R * Q_BLOCK), F32)],
        compiler_params=_cparams(("parallel", "parallel", "arbitrary")),
        name="nsa_attention",
    )(colmat, cmp_rows, cmp_cols, rowmat, colmat, rowmat, colmat, gates, cbias, sbias, wbias, ov_t, msel)


SB_QB = 512
SB_KT = 256
SB_ZERO_LOG = -160.0


def _sb_kernel(q_ref, k_ref, v_ref, tri_ref, o_ref):
    i = pl.program_id(1)
    H = SB_HEADS
    tri = tri_ref[...]
    assert SB_QB == 2 * SB_KT
    below_diag = (lax.broadcasted_iota(jnp.int32, (SB_KT, SB_KT), 1)
                  < lax.broadcasted_iota(jnp.int32, (SB_KT, SB_KT), 0))

    def step(J, rows, st, masked):
        out = []
        for h in range(H):
            carry, acc = st[h]
            k_t = jnp.concatenate([k_ref[h, (SB_KT // LANES) * J + c] for c in range(SB_KT // LANES)], axis=1)
            z = _dot(q_ref[h, rows, :], k_t)
            sp = jnp.maximum(z, 0.0) + jnp.log2(1.0 + jnp.exp2(-jnp.abs(z)))
            lf = jnp.where(below_diag, -sp, 0.0) if masked else -sp
            lf_hi = lf.astype(BF16)
            lf_lo = (lf - lf_hi.astype(F32)).astype(BF16)
            after = _dot(lf_hi, tri) + _dot(lf_lo, tri) + carry
            w = jnp.exp2(z - sp + after)
            if masked:
                w = jnp.where(below_diag, w, 0.0)
            v = v_ref[h, pl.ds(pl.multiple_of(J * SB_KT, SB_KT), SB_KT), :]
            acc = acc + _dot(w.astype(BF16), v)
            carry = carry + jnp.sum(lf, axis=1, keepdims=True)
            out.append((carry, acc))
        return tuple(out)

    lo, hi = slice(0, SB_KT), slice(SB_KT, SB_QB)
    zero = tuple((jnp.zeros((SB_KT, 1), F32), jnp.zeros((SB_KT, HEAD_DIM), F32)) for _ in range(H))
    st_hi = step(2 * i, hi, step(2 * i + 1, hi, zero, True), False)
    st_lo = step(2 * i, lo, zero, True)

    def alive(st):
        worst = st[0][0]
        for h in range(1, H):
            worst = jnp.maximum(worst, st[h][0])
        return jnp.max(worst) > SB_ZERO_LOG

    def sweep(rows, st):
        def body(c):
            J, _, st = c
            st = step(J, rows, st, False)
            return J - 1, alive(st), st

        return lax.while_loop(lambda c: (c[0] >= 0) & c[1], body, (2 * i - 1, alive(st), st))[2]

    st_lo, st_hi = sweep(lo, st_lo), sweep(hi, st_hi)
    o_ref[0, lo, :] = jnp.concatenate([st_lo[h][1] for h in range(H)], axis=1).astype(o_ref.dtype)
    o_ref[0, hi, :] = jnp.concatenate([st_hi[h][1] for h in range(H)], axis=1).astype(o_ref.dtype)


def _sb_attention(colmat, rowmat, tri, B, S):
    nq = S // SB_QB
    nb = S // LANES
    H = SB_HEADS
    return pl.pallas_call(
        _sb_kernel,
        grid=(B, nq),
        in_specs=[
            pl.BlockSpec((H, SB_QB, HEAD_DIM), lambda b, i: (ROW_SQ // H, b * nq + i, 0)),
            pl.BlockSpec((H, nb, HEAD_DIM, LANES), lambda b, i: (COL_SK // H, b, 0, 0)),
            pl.BlockSpec((H, S, HEAD_DIM), lambda b, i: (ROW_SV // H, b, 0)),
            pl.BlockSpec((SB_KT, SB_KT), lambda b, i: (0, 0)),
        ],
        out_specs=pl.BlockSpec((1, SB_QB, H * HEAD_DIM), lambda b, i: (b, i, 0)),
        out_shape=jax.ShapeDtypeStruct((B, S, H * HEAD_DIM), BF16),
        compiler_params=_cparams(("parallel", "arbitrary")),
        name="sb_attention",
    )(rowmat, colmat, rowmat, tri)


SWA_SUB = 4


def _swa_kernel(q_ref, k_ref, v_ref, bias_ref, sink_ref, o_ref):
    R = SWA_REP
    sink = sink_ref[0]
    outs = []
    for u in range(SWA_SUB):
        i = pl.program_id(2) * SWA_SUB + u
        q_t = jnp.concatenate([q_ref[r, u] for r in range(R)], axis=1)
        J0 = jnp.maximum(i - 1, 0)
        k = k_ref[0, pl.ds(pl.multiple_of(J0 * Q_BLOCK, Q_BLOCK), 2 * Q_BLOCK), :]
        bias = bias_ref[0, pl.ds(pl.multiple_of((1 - (i - J0)) * Q_BLOCK, Q_BLOCK), 2 * Q_BLOCK), :]
        s = _dot(k, q_t) + bias
        m = jnp.maximum(jnp.max(s, axis=0, keepdims=True), sink)
        e = jnp.exp(s - m)
        den = jnp.sum(e, axis=0, keepdims=True) + jnp.exp(sink - m)
        v_t = jnp.concatenate([v_ref[0, J0], v_ref[0, J0 + 1]], axis=1)
        o_t = _dot(v_t, e.astype(BF16)) * (1.0 / den)
        outs.append(_pairs_to_rows(o_t).astype(o_ref.dtype))
    for u in range(SWA_SUB):
        o_ref[0, u * Q_BLOCK:(u + 1) * Q_BLOCK, :] = outs[u]


def _swa_attention(colmat, rowmat, bias, sinks, B, S):
    nb = S // Q_BLOCK
    ns = nb // SWA_SUB
    G, R = SWA_KV_HEADS, SWA_REP
    return pl.pallas_call(
        _swa_kernel,
        grid=(B, G, ns),
        in_specs=[
            pl.BlockSpec((R, SWA_SUB, HEAD_DIM, LANES), lambda b, g, i: (COL_WQ // R + g, b * ns + i, 0, 0)),
            pl.BlockSpec((1, S, HEAD_DIM), lambda b, g, i: (ROW_WK + g, b, 0)),
            pl.BlockSpec((1, nb, HEAD_DIM, LANES), lambda b, g, i: (COL_WV + g, b, 0, 0)),
            pl.BlockSpec((1, 3 * Q_BLOCK, R * Q_BLOCK), lambda b, g, i: (g, 0, 0)),
            pl.BlockSpec((1, 1, R * Q_BLOCK), lambda b, g, i: (g, 0, 0)),
        ],
        out_specs=pl.BlockSpec((1, SWA_SUB * Q_BLOCK, R * HEAD_DIM), lambda b, g, i: (b, i, g)),
        out_shape=jax.ShapeDtypeStruct((B, S, SWA_HEADS * HEAD_DIM), BF16),
        compiler_params=_cparams(("parallel", "parallel", "arbitrary")),
        name="swa_attention",
    )(colmat, rowmat, colmat, bias, sinks)


MERGE_TM = 512


def _merge_kernel(x_ref, nw_ref, wg_ref, ya_ref, yb_ref, yc_ref, ua_ref, ub_ref, uc_ref, wo_ref, o_ref):
    x = x_ref[...]
    h = _rms_rows(x, nw_ref[...]).astype(BF16)
    merged = None
    for br, (y_ref, u_ref) in enumerate(((ya_ref, ua_ref), (yb_ref, ub_ref), (yc_ref, uc_ref))):
        g = jax.nn.sigmoid(_dot(h, wg_ref[:, br * D_MODEL:(br + 1) * D_MODEL]))
        t = g * _dot(y_ref[...], u_ref[...])
        merged = t if merged is None else merged + t
    o_ref[...] = x + _dot(merged.astype(BF16), wo_ref[...])


def _merge(x2, nw, w_bgate, y_nsa, y_sb, y_swa, u_nsa, u_sb, u_swa, w_out, l):
    T = x2.shape[0]
    full = lambda w: _layer_spec(l, w.shape[1:])
    rows = lambda n: pl.BlockSpec((MERGE_TM, n), lambda i: (i, 0))
    return pl.pallas_call(
        _merge_kernel,
        grid=(T // MERGE_TM,),
        in_specs=[rows(D_MODEL), pl.BlockSpec((1, D_MODEL), lambda i: (0, 0)), full(w_bgate),
                  rows(y_nsa.shape[1]), rows(y_sb.shape[1]), rows(y_swa.shape[1]),
                  full(u_nsa), full(u_sb), full(u_swa), full(w_out)],
        out_specs=rows(D_MODEL),
        out_shape=jax.ShapeDtypeStruct((T, D_MODEL), F32),
        compiler_params=_cparams(("parallel",)),
        name="merge",
    )(x2, nw.reshape(1, D_MODEL), w_bgate, y_nsa, y_sb, y_swa, u_nsa, u_sb, u_swa, w_out)


def _t5_bucket(dist):
    max_exact = NUM_BUCKETS // 2
    d = jnp.maximum(dist, 0)
    df = jnp.maximum(d, 1).astype(F32)
    large = max_exact + (jnp.log(df / max_exact) / math.log(MAX_DISTANCE / max_exact)
                         * (NUM_BUCKETS - max_exact)).astype(jnp.int32)
    large = jnp.minimum(large, NUM_BUCKETS - 1)
    return jnp.where(d < max_exact, d, large)


def _toeplitz(ext, delta):
    H = ext.shape[0]
    n = Q_BLOCK
    c = delta + MAX_DISTANCE
    w = ext[:, c - (n - 1):c + n]
    flat = jnp.pad(jnp.broadcast_to(w[:, None, :], (H, n, 2 * n - 1)), ((0, 0), (0, 0), (0, 1))).reshape(H, 2 * n * n)
    return flat[:, n - 1:n - 1 + n * (2 * n - 1)].reshape(H, n, 2 * n - 1)[:, :, :n]


def _lanes(t, G, R):
    K, Q = t.shape[1:]
    return t.reshape(G, R, K, Q).transpose(0, 2, 1, 3).reshape(G, K, R * Q)


def _bias_tables(rel_bias, nb, qk_bound):
    n = Q_BLOCK
    G, R = NSA_KV_HEADS, NSA_REP
    by_dist = rel_bias[_t5_bucket(jnp.arange(MAX_DISTANCE)), :].T.astype(F32)
    far = rel_bias[NUM_BUCKETS - 1, :].astype(F32)
    neg = jnp.full((1, n), NEG_INF, F32)
    k = jnp.arange(n)[:, None]
    q = jnp.arange(n)[None, :]
    above = (k > q)[None]

    rel = (by_dist[:NSA_HEADS] - far[:NSA_HEADS, None]) * LOG2E
    m_g = qk_bound + jnp.maximum(jnp.max(rel.reshape(G, -1), axis=1), 0.0)
    m_h = jnp.repeat(m_g, R)[:, None, None]
    ext = jnp.concatenate([jnp.broadcast_to(neg, (NSA_HEADS, n)), rel, jnp.zeros((NSA_HEADS, n), F32)], axis=1)
    t0, t1 = _toeplitz(ext, 0), _toeplitz(ext, n)
    zeros = jnp.zeros_like(t0)
    masked = jnp.full_like(t0, NEG_INF)
    sbias = _lanes(jnp.concatenate([zeros, t1, t0, masked], axis=1), G, R)
    n_win = NSA_WINDOW // n
    edge = jnp.where(above, zeros, NEG_INF)
    tiles = [edge] + [zeros] * (n_win - 2) + [t1, t0]
    wbias = _lanes(jnp.concatenate([t - m_h for t in tiles] + [masked] * n_win, axis=1), G, R)
    off = 8 * (nb - 1)
    near = jnp.stack([ext[:, 97 - CMP_STRIDE * m:97 - CMP_STRIDE * m + n] for m in range(-9, 7)], axis=1)
    n_rows = 16 * nb - 8
    cb = jnp.concatenate([jnp.zeros((NSA_HEADS, off - 9, n), F32), near,
                          jnp.full((NSA_HEADS, n_rows - off - 7, n), NEG_INF, F32)], axis=1) - m_h
    cbias = _lanes(cb, G, R)
    msel = jnp.broadcast_to(-m_g[:, None, None], (G, 1, n))

    hs = slice(NSA_HEADS, NSA_HEADS + SWA_HEADS)
    ext_w = jnp.concatenate([jnp.broadcast_to(neg, (SWA_HEADS, n)), by_dist[hs],
                             jnp.broadcast_to(far[hs, None], (SWA_HEADS, n))], axis=1)
    s0 = _toeplitz(ext_w, 0)
    s1 = jnp.where(above, _toeplitz(ext_w, n), NEG_INF)
    swa_bias = _lanes(jnp.concatenate([s1, s0, jnp.full_like(s0, NEG_INF)], axis=1), SWA_KV_HEADS, SWA_REP)
    return cbias, sbias, wbias, msel, swa_bias


def _overlap_t(S):
    nc, n_slc = S // CMP_STRIDE, S // SLC_LEN
    n = np.arange(nc)[None, :]
    j = np.arange(n_slc)[:, None]
    ov = (n * CMP_STRIDE < (j + 1) * SLC_LEN) & (n * CMP_STRIDE + CMP_LEN - 1 >= j * SLC_LEN) & (n < nc - 1)
    return jnp.asarray(ov, BF16)


def _head_gains(heads, norm_of, scale_of):
    ones = jnp.ones((HEAD_DIM,), F32)
    gains = [norm_of[name].astype(F32) * scale_of.get(name, 1.0) if name in norm_of else ones for name, _ in heads]
    return jnp.concatenate(gains).reshape(1, -1)


def kernel(x, rel_bias, ffn1_norm, ffn1_w_gate, ffn1_w_up, ffn1_w_down, mix_norm, w_in, nsa_q_norm, nsa_k_norm, nsa_cmp_pos, nsa_cmp_k_w1, nsa_cmp_k_w2, nsa_cmp_v_w1, nsa_cmp_v_w2, swa_q_norm, swa_k_norm, swa_sinks, w_up_nsa, w_up_sb, w_up_swa, w_out, ffn2_norm, ffn2_w_gate, ffn2_w_up, ffn2_w_down):
    B, S, D = x.shape
    T = B * S
    nb = S // Q_BLOCK
    nc = S // CMP_STRIDE
    depth = w_in.shape[0]
    scale = HEAD_DIM ** -0.5
    scale_of = {"nq": scale * LOG2E, "sq": scale * LOG2E, "wq": scale}
    qk_bound = (BOUND_SLACK * LOG2E * HEAD_DIM ** 0.5
                * jnp.max(jnp.abs(nsa_q_norm.astype(F32))) * jnp.max(jnp.abs(nsa_k_norm.astype(F32))))
    cbias, sbias, wbias, msel, swa_bias = _bias_tables(rel_bias, nb, qk_bound)
    plan = _proj_plan(scale_of)
    ov_t = _overlap_t(S)
    tri = jnp.asarray(np.tril(np.ones((SB_KT, SB_KT), np.float32), -1), BF16)

    (ffn1_w_gate, ffn1_w_up, ffn1_w_down, ffn2_w_gate, ffn2_w_up, ffn2_w_down, w_up_nsa, w_up_sb, w_up_swa,
     w_out) = (w.astype(BF16) for w in (ffn1_w_gate, ffn1_w_up, ffn1_w_down, ffn2_w_gate, ffn2_w_up, ffn2_w_down,
                                        w_up_nsa, w_up_sb, w_up_swa, w_out))
    tail0 = OFF_NGATE + N_GATE
    w_head, w_tail = w_in[:, :, :OFF_NGATE].astype(BF16), w_in[:, :, tail0:].astype(BF16)
    w_bgate = w_tail[:, :, OFF_BGATE - tail0:]
    w_gate = jnp.pad(w_in[:, :, OFF_NGATE:tail0], ((0, 0), (0, 0), (0, LANES - N_GATE))).astype(BF16)
    runs = []
    for _, off in ROW_HEADS + COL_HEADS:
        if runs and runs[-1][1] == off:
            runs[-1][1] = off + HEAD_DIM
        else:
            runs.append([off, off + HEAD_DIM])
    pieces = [w_head[:, :, a:b] if a < OFF_NGATE else w_tail[:, :, a - tail0:b - tail0] for a, b in runs]
    w_proj = jnp.concatenate(pieces + [w_gate], axis=2)

    x2 = x.reshape(T, D)
    for l in range(depth):
        x2 = _ffn(x2, ffn1_norm[l], ffn1_w_gate, ffn1_w_up, ffn1_w_down, l)

        norm_of = {"nq": nsa_q_norm[l], "nks": nsa_k_norm[l], "nkw": nsa_k_norm[l],
                   "wq": swa_q_norm[l], "wk": swa_k_norm[l]}
        gain = jnp.concatenate([_head_gains(ROW_HEADS, norm_of, scale_of), _head_gains(COL_HEADS, norm_of, scale_of),
                                jnp.ones((1, LANES), F32)], axis=1)
        rowmat, colmat, gates = _proj(x2, mix_norm[l], w_proj, gain, plan, l)
        gates = gates[:N_GATE].reshape(N_BRANCHES, NSA_KV_HEADS, NSA_REP, T)

        rows4 = rowmat[ROW_KC:ROW_KC + 2 * NSA_KV_HEADS].reshape(2 * NSA_KV_HEADS, B, nc, CMP_STRIDE * HEAD_DIM)
        cmp_rows, cmp_cols = _compress(rows4, jnp.stack([nsa_cmp_k_w1[l], nsa_cmp_v_w1[l]]),
                                       jnp.stack([nsa_cmp_k_w2[l], nsa_cmp_v_w2[l]]),
                                       nsa_cmp_pos[l], nsa_k_norm[l])

        y_nsa = _nsa_attention(colmat, rowmat, cmp_rows, cmp_cols, gates, cbias, sbias, wbias, ov_t, msel, B, S)
        y_sb = _sb_attention(colmat, rowmat, tri, B, S)
        sinks = jnp.repeat(swa_sinks[l].astype(F32).reshape(SWA_KV_HEADS, 1, SWA_REP), Q_BLOCK, axis=2)
        y_swa = _swa_attention(colmat, rowmat, swa_bias, sinks, B, S)

        x2 = _merge(x2, mix_norm[l], w_bgate, y_nsa.reshape(T, -1), y_sb.reshape(T, -1),
                    y_swa.reshape(T, -1), w_up_nsa, w_up_sb, w_up_swa, w_out, l)
        x2 = _ffn(x2, ffn2_norm[l], ffn2_w_gate, ffn2_w_up, ffn2_w_down, l)
    return x2.reshape(B, S, D)
```

```python
import functools
import math

import numpy as np
import jax
import jax.numpy as jnp
from jax import lax
from jax.experimental import pallas as pl
from jax.experimental.pallas import tpu as pltpu

D_MODEL = 1024
HEAD_DIM = 64
Q_BLOCK = 128
NSA_HEADS = 8
NSA_KV_HEADS = 2
NSA_REP = NSA_HEADS // NSA_KV_HEADS
CMP_LEN = 32
CMP_STRIDE = 16
CMP_HIDDEN = 256
SLC_LEN = 64
SLC_TOPN = 16
NSA_WINDOW = 512
SB_HEADS = 4
SWA_HEADS = 4
SWA_KV_HEADS = 2
SWA_REP = SWA_HEADS // SWA_KV_HEADS
SWA_WINDOW = 128
NUM_BUCKETS = 32
MAX_DISTANCE = 128
D_FF = 2816
NORM_EPS = 1e-6
NEG_INF = -1e30
N_BRANCHES = 3
N_GATE = 3 * NSA_HEADS

LANES = 128
VMEM_LIMIT = 56 * 1024 * 1024
BF16 = jnp.bfloat16
F32 = jnp.float32

_sizes = (NSA_HEADS * HEAD_DIM,) + (NSA_KV_HEADS * HEAD_DIM,) * 6 + (N_GATE,) \
    + (SB_HEADS * HEAD_DIM,) * 3 + (SWA_HEADS * HEAD_DIM, SWA_KV_HEADS * HEAD_DIM,
                                    SWA_KV_HEADS * HEAD_DIM, N_BRANCHES * D_MODEL)
_offs = np.concatenate([[0], np.cumsum(_sizes)])
(OFF_NQ, OFF_NKC, OFF_NVC, OFF_NKS, OFF_NVS, OFF_NKW, OFF_NVW, OFF_NGATE,
 OFF_SQ, OFF_SK, OFF_SV, OFF_WQ, OFF_WK, OFF_WV, OFF_BGATE) = (int(o) for o in _offs[:-1])

ROW_HEADS = ([("nkc", OFF_NKC + 64 * g) for g in range(2)] + [("nvc", OFF_NVC + 64 * g) for g in range(2)]
             + [("nks", OFF_NKS + 64 * g) for g in range(2)] + [("nkw", OFF_NKW + 64 * g) for g in range(2)]
             + [("sq", OFF_SQ + 64 * h) for h in range(4)] + [("sv", OFF_SV + 64 * h) for h in range(4)]
             + [("wk", OFF_WK + 64 * g) for g in range(2)])
ROW_KC, ROW_VC, ROW_KS, ROW_KW, ROW_SQ, ROW_SV, ROW_WK = 0, 2, 4, 6, 8, 12, 16
COL_HEADS = ([("nq", OFF_NQ + 64 * h) for h in range(8)] + [("nvs", OFF_NVS + 64 * g) for g in range(2)]
             + [("nvw", OFF_NVW + 64 * g) for g in range(2)] + [("sk", OFF_SK + 64 * h) for h in range(4)]
             + [("wq", OFF_WQ + 64 * h) for h in range(4)] + [("wv", OFF_WV + 64 * g) for g in range(2)])
COL_NQ, COL_VS, COL_VW, COL_SK, COL_WQ, COL_WV = 0, 8, 10, 12, 16, 20


def _dot(a, b):
    return jnp.dot(a, b, preferred_element_type=F32)


def _rms_rows(x, w):
    ms = jnp.mean(x * x, axis=-1, keepdims=True)
    return x * lax.rsqrt(ms + NORM_EPS) * w


def _cparams(sem, flags=None):
    return pltpu.CompilerParams(dimension_semantics=sem, vmem_limit_bytes=VMEM_LIMIT, flags=flags)


FFN_TM = 1024
MXU_TILE = 256
FFN_FC = MXU_TILE


def _ffn_kernel(x_ref, nw_ref, wg_ref, wu_ref, wd_ref, o_ref):
    x = x_ref[...]
    h = _rms_rows(x, nw_ref[...]).astype(BF16)
    acc = jnp.zeros(x.shape, F32)
    for c in range(D_FF // FFN_FC):
        cols = slice(c * FFN_FC, (c + 1) * FFN_FC)
        g = _dot(h, wg_ref[:, cols])
        u = _dot(h, wu_ref[:, cols])
        a = (g * jax.nn.sigmoid(g)) * u
        acc = acc + _dot(a.astype(BF16), wd_ref[cols, :])
    o_ref[...] = x + 0.5 * acc


def _layer_spec(l, shape, **kw):
    return pl.BlockSpec((None,) + tuple(shape), lambda i: (l, 0, 0), **kw)


def _ffn(x2, nw, wg, wu, wd, l):
    T = x2.shape[0]
    resident = lambda shape: _layer_spec(l, shape, pipeline_mode=pl.Buffered(1))
    return pl.pallas_call(
        _ffn_kernel,
        grid=(T // FFN_TM,),
        in_specs=[
            pl.BlockSpec((FFN_TM, D_MODEL), lambda i: (i, 0)),
            pl.BlockSpec((1, D_MODEL), lambda i: (0, 0)),
            resident((D_MODEL, D_FF)), resident((D_MODEL, D_FF)), resident((D_FF, D_MODEL)),
        ],
        out_specs=pl.BlockSpec((FFN_TM, D_MODEL), lambda i: (i, 0)),
        out_shape=jax.ShapeDtypeStruct((T, D_MODEL), F32),
        compiler_params=_cparams(("parallel",)),
        name="ffn",
    )(x2, nw.reshape(1, D_MODEL), wg, wu, wd)


PROJ_TM = 512


GATE_ROWS = 32

_NORMED = ("nq", "nks", "nkw", "wq", "wk")


def _proj_plan(scale_of):
    plan = []
    for kind, heads in (("row", ROW_HEADS), ("col", COL_HEADS)):
        for p in range(len(heads) // 2):
            name = heads[2 * p][0]
            assert heads[2 * p + 1][0] == name
            plan.append((kind, p, name in _NORMED, 1.0 if name in _NORMED else scale_of.get(name, 1.0)))
    plan.append(("gate", 0, False, 1.0))
    return tuple(plan)


def _proj_kernel(x_ref, nw_ref, w_ref, gain_ref, bd_ref, row_ref, col_ref, gate_ref, *, plan):
    h = _rms_rows(x_ref[...], nw_ref[...]).astype(BF16)
    pair = 2 * MXU_TILE
    for c0 in range(0, len(plan) * LANES, pair):
        width = min(pair, len(plan) * LANES - c0)
        zg = _dot(h, w_ref[:, c0:c0 + width])
        for u in range(width // LANES):
            t = c0 // LANES + u
            kind, p, normed, scale = plan[t]
            z = zg[:, u * LANES:(u + 1) * LANES]
            if normed:
                ms = _dot((z * z).astype(BF16), bd_ref[...])
                z = z * lax.rsqrt(ms + NORM_EPS) * gain_ref[:, t * LANES:(t + 1) * LANES]
            elif scale != 1.0:
                z = z * scale
            if kind == "row":
                row_ref[2 * p] = z[:, :HEAD_DIM].astype(row_ref.dtype)
                row_ref[2 * p + 1] = z[:, HEAD_DIM:].astype(row_ref.dtype)
            elif kind == "col":
                z_t = z.T
                for c in range(PROJ_TM // LANES):
                    col_ref[2 * p, c] = z_t[:HEAD_DIM, c * LANES:(c + 1) * LANES].astype(col_ref.dtype)
                    col_ref[2 * p + 1, c] = z_t[HEAD_DIM:, c * LANES:(c + 1) * LANES].astype(col_ref.dtype)
            else:
                gate_ref[...] = jax.nn.sigmoid(z).T[:GATE_ROWS]


def _proj(x2, nw, w, gain, plan, l):
    T = x2.shape[0]
    n_row, n_col = len(ROW_HEADS), len(COL_HEADS)
    bd = np.kron(np.eye(2, dtype=np.float32), np.full((HEAD_DIM, HEAD_DIM), 1.0 / HEAD_DIM, np.float32))
    return pl.pallas_call(
        functools.partial(_proj_kernel, plan=plan),
        grid=(T // PROJ_TM,),
        in_specs=[
            pl.BlockSpec((PROJ_TM, D_MODEL), lambda i: (i, 0)),
            pl.BlockSpec((1, D_MODEL), lambda i: (0, 0)),
            _layer_spec(l, w.shape[1:]),
            pl.BlockSpec(gain.shape, lambda i: (0, 0)),
            pl.BlockSpec((LANES, LANES), lambda i: (0, 0)),
        ],
        out_specs=[
            pl.BlockSpec((n_row, PROJ_TM, HEAD_DIM), lambda i: (0, i, 0)),
            pl.BlockSpec((n_col, PROJ_TM // LANES, HEAD_DIM, LANES), lambda i: (0, i, 0, 0)),
            pl.BlockSpec((GATE_ROWS, PROJ_TM), lambda i: (0, i)),
        ],
        out_shape=[jax.ShapeDtypeStruct((n_row, T, HEAD_DIM), BF16),
                   jax.ShapeDtypeStruct((n_col, T // LANES, HEAD_DIM, LANES), BF16),
                   jax.ShapeDtypeStruct((GATE_ROWS, T), F32)],
        compiler_params=_cparams(("parallel",)),
        name="proj",
    )(x2, nw.reshape(1, D_MODEL), w, gain, jnp.asarray(bd, BF16))


def _compress_kernel(r_ref, w1_ref, w2_ref, pos_ref, kn_ref, row_ref, col_ref):
    kind = pl.program_id(0) // NSA_KV_HEADS
    r = r_ref[0, 0]
    nc = r.shape[0]
    half = CMP_STRIDE * HEAD_DIM
    w1 = w1_ref[0]
    p_lo = _dot(r, w1[:half])
    p_hi = _dot(r, w1[half:])
    p_pos = _dot(jnp.broadcast_to(pos_ref[...], (8, CMP_LEN * HEAD_DIM)).astype(BF16), w1)[:1]
    hdn = p_lo + pltpu.roll(p_hi, nc - 1, 0) + p_pos
    hdn = hdn * jax.nn.sigmoid(hdn)
    out = _dot(hdn.astype(BF16), w2_ref[0])
    ms = jnp.sum(out * out, axis=-1, keepdims=True) * (1.0 / HEAD_DIM)
    normed = out * lax.rsqrt(ms + NORM_EPS) * kn_ref[...]
    out = jnp.where(kind == 0, normed, out)
    row_ref[0, 0] = out[:, :HEAD_DIM].astype(row_ref.dtype)
    col_ref[0, 0] = out.T[:HEAD_DIM].astype(col_ref.dtype)


def _compress(rows4, w1, w2, pos, k_norm):
    assert ROW_KC == 0 and ROW_VC == NSA_KV_HEADS
    _, B, NC, _ = rows4.shape
    kn = jnp.pad(k_norm.reshape(1, HEAD_DIM), ((0, 0), (0, LANES - HEAD_DIM)))
    w2p = jnp.pad(w2, ((0, 0), (0, 0), (0, LANES - HEAD_DIM))).astype(BF16)
    return pl.pallas_call(
        _compress_kernel,
        grid=(4, B),
        in_specs=[
            pl.BlockSpec((1, 1, NC, CMP_STRIDE * HEAD_DIM), lambda h, b: (h, b, 0, 0)),
            pl.BlockSpec((1, CMP_LEN * HEAD_DIM, CMP_HIDDEN), lambda h, b: (h // NSA_KV_HEADS, 0, 0)),
            pl.BlockSpec((1, CMP_HIDDEN, LANES), lambda h, b: (h // NSA_KV_HEADS, 0, 0)),
            pl.BlockSpec((1, CMP_LEN * HEAD_DIM), lambda h, b: (0, 0)),
            pl.BlockSpec((1, LANES), lambda h, b: (0, 0)),
        ],
        out_specs=[
            pl.BlockSpec((1, 1, NC, HEAD_DIM), lambda h, b: (h, b, 0, 0)),
            pl.BlockSpec((1, 1, HEAD_DIM, NC), lambda h, b: (h, b, 0, 0)),
        ],
        out_shape=[jax.ShapeDtypeStruct((4, B, NC, HEAD_DIM), BF16),
                   jax.ShapeDtypeStruct((4, B, HEAD_DIM, NC), BF16)],
        compiler_params=_cparams(("parallel", "parallel")),
        name="nsa_compress",
    )(rows4, w1.astype(BF16), w2p, pos.reshape(1, CMP_LEN * HEAD_DIM), kn)


def _softmax_tile_update(s, v_t, m, l, acc):
    m_new = jnp.maximum(m, jnp.max(s, axis=0, keepdims=True))
    alpha = jnp.exp2(m - m_new)
    p = jnp.exp2(s - m_new)
    l_new = alpha * l + jnp.sum(p, axis=0, keepdims=True)
    acc_new = alpha * acc + _dot(v_t, p.astype(BF16))
    return m_new, l_new, acc_new


def _pairs_to_rows(o_t):
    n = o_t.shape[1] // LANES
    outs = []
    for p in range(n // 2):
        a = o_t[:, (2 * p) * LANES:(2 * p + 1) * LANES]
        b = o_t[:, (2 * p + 1) * LANES:(2 * p + 2) * LANES]
        outs.append(jnp.concatenate([a, b], axis=0).T)
    return outs[0] if len(outs) == 1 else jnp.concatenate(outs, axis=1)


FAR_TILES = 4
NSA_SUB = 8
TAKEN = -2.0
NEG_ROWS = 8
N_FORCED = 3
LOG2E = math.log2(math.e)
NSA_TINY = 2.0 ** -100
BOUND_SLACK = 1.05


def _nsa_kernel(q_ref, kc_ref, vc_ref, ks_ref, vs_ref, kw_ref, vw_ref, gate_ref,
                cbias_ref, sbias_ref, wbias_ref, ov_ref, msel_ref, o_ref,
                neg_all, s_a, s_b, e_a, e_b, acc_ref, l_ref, *, nb):
    refs = (q_ref, kc_ref, vc_ref, ks_ref, vs_ref, kw_ref, vw_ref, gate_ref, cbias_ref, sbias_ref, wbias_ref,
            ov_ref, msel_ref)
    blocks = [_nsa_block(u, refs, neg_all.at[u], acc_ref, l_ref, nb) for u in range(NSA_SUB)]
    for phase in range(3):
        for blk in blocks:
            blk[phase]()
    _far_sweep(q_ref, ks_ref, vs_ref, neg_all, (s_a, s_b, e_a, e_b), acc_ref, l_ref, nb)
    fast = [blk[3]() for blk in blocks]
    l_min = fast[0][0]
    for l_u, _ in fast[1:]:
        l_min = jnp.minimum(l_min, l_u)
    outs = lax.cond(jnp.min(l_min) > NSA_TINY, lambda: tuple(o for _, o in fast),
                    lambda: tuple(blk[4]() for blk in blocks))
    for u, blk in enumerate(blocks):
        o_ref[0, u * Q_BLOCK:(u + 1) * Q_BLOCK, :] = blk[5](outs[u]).astype(o_ref.dtype)


def _far_sweep(q_ref, ks_ref, vs_ref, neg_all, stage_refs, acc_ref, l_ref, nb):
    R = NSA_REP
    n_slc = neg_all.shape[1] - NEG_ROWS
    i0 = pl.program_id(2) * NSA_SUB
    n_far = [jnp.maximum(i0 + u - 1, 0) for u in range(NSA_SUB)]
    offs = [0]
    for u in range(NSA_SUB):
        offs.append(offs[-1] + (n_far[u] + FAR_TILES - 1) // FAR_TILES)
    n_tot = offs[-1]

    def locate(c):
        u = sum(jnp.where(c >= offs[k], 1, 0) for k in range(1, NSA_SUB))
        start = sum(jnp.where(u == k, offs[k], 0) for k in range(NSA_SUB))
        far = sum(jnp.where(u == k, n_far[k], 0) for k in range(NSA_SUB))
        tile = FAR_TILES * (c - start)
        return u, jnp.clip(tile, 0, nb - FAR_TILES), far - tile

    def k_rows(J0):
        return ks_ref[0, pl.ds(pl.multiple_of(J0 * Q_BLOCK, Q_BLOCK), FAR_TILES * Q_BLOCK), :]

    def v_cols(J0):
        return jnp.concatenate([vs_ref[0, J0 + t] for t in range(FAR_TILES)], axis=1)

    def q_cols(u):
        return jnp.concatenate([q_ref[r, u] for r in range(R)], axis=1)

    def sel_mask(u, J0, n_valid):
        parts = []
        for t in range(2 * FAR_TILES):
            row = jnp.where(t // 2 < n_valid, 2 * J0 + t, n_slc)
            parts.append(jnp.broadcast_to(neg_all[u, pl.ds(row, 1), :], (SLC_LEN, Q_BLOCK)))
        return jnp.concatenate([jnp.concatenate(parts, axis=0)] * R, axis=1)

    def add_values(c, e_ref):
        u, tile, _ = locate(c)
        acc_ref[u] = acc_ref[u] + _dot(v_cols(tile), e_ref[...])

    def stage(c, s_in, s_out, e_in, e_out):
        u0, tile0, _ = locate(c - 1)
        u1, tile1, valid1 = locate(c)
        u2, tile2, _ = locate(c + 1)
        v_prev, acc_prev = v_cols(tile0), acc_ref[u0]
        mask, l_cur = sel_mask(u1, tile1, valid1), l_ref[u1]
        k_next, q_next = k_rows(tile2), q_cols(u2)
        pv = _dot(v_prev, e_in[...])
        s_out[...] = _dot(k_next, q_next)
        e = jnp.exp2(s_in[...] + mask)
        e_out[...] = e.astype(BF16)
        acc_ref[u0] = acc_prev + pv
        l_ref[u1] = l_cur + jnp.sum(e, axis=0, keepdims=True)

    s_a, s_b, e_a, e_b = stage_refs
    e_b[...] = jnp.zeros_like(e_b)
    u_first, tile_first, _ = locate(0)
    s_a[...] = _dot(k_rows(tile_first), q_cols(u_first))

    def pair(c):
        stage(c, s_a, s_b, e_b, e_a)
        stage(c + 1, s_b, s_a, e_a, e_b)

    def quad(c):
        pair(c)
        pair(c + 2)

    def oct_body(t, carry):
        quad(8 * t)
        quad(8 * t + 4)
        return carry

    n_octs = n_tot // 8
    lax.fori_loop(0, n_octs, oct_body, 0)
    done = 8 * n_octs

    @pl.when(n_tot - done >= 4)
    def _():
        quad(done)

    done = done + 4 * ((n_tot - done) // 4)

    @pl.when(n_tot - done >= 2)
    def _():
        pair(done)

    @pl.when(n_tot % 2 == 1)
    def _():
        stage(n_tot - 1, s_a, s_b, e_b, e_a)
        add_values(n_tot - 1, e_a)

    @pl.when((n_tot % 2 == 0) & (n_tot > 0))
    def _():
        add_values(n_tot - 1, e_b)


def _nsa_block(u, refs, neg_ref, acc_ref, l_ref, nb):
    (q_ref, kc_ref, vc_ref, ks_ref, vs_ref, kw_ref, vw_ref, gate_ref, cbias_ref, sbias_ref, wbias_ref,
     ov_ref, msel_ref) = refs
    i = pl.program_id(2) * NSA_SUB + u
    R = NSA_REP
    N = R * Q_BLOCK
    nc = kc_ref.shape[2]
    q_t = jnp.concatenate([q_ref[r, u] for r in range(R)], axis=1)
    n_win = NSA_WINDOW // Q_BLOCK

    def k_rows(ref, J0, n_tiles):
        return ref[0, pl.ds(pl.multiple_of(J0 * Q_BLOCK, Q_BLOCK), n_tiles * Q_BLOCK), :]

    def v_cols(ref, J0, n_tiles):
        return jnp.concatenate([ref[0, J0 + u] for u in range(n_tiles)], axis=1)

    def sel_mask(J0, n_tiles, n_valid=None):
        n_slc = neg_ref.shape[0] - NEG_ROWS
        parts = []
        for u in range(2 * n_tiles):
            row = 2 * J0 + u
            if n_valid is not None:
                row = jnp.where(u // 2 < n_valid, row, n_slc)
            parts.append(jnp.broadcast_to(neg_ref[pl.ds(row, 1), :], (SLC_LEN, Q_BLOCK)))
        return jnp.concatenate([jnp.concatenate(parts, axis=0)] * R, axis=1)

    def compressed(fast):
        cb = cbias_ref[0, pl.ds(pl.multiple_of(8 * (nb - 1) - 8 * i, 8), nc), :]
        s = _dot(kc_ref[0, 0], q_t) + cb
        if fast:
            e = jnp.exp2(s)
        else:
            e = jnp.where(s > 0.5 * NEG_INF, jnp.exp2(s - jnp.max(s, axis=0, keepdims=True)), 0.0)
        l = jnp.sum(e, axis=0, keepdims=True)
        p = e * jnp.where(l > 0.0, 1.0 / l, 0.0)
        return p, l, _dot(vc_ref[0, 0], p.astype(BF16))

    def select_blocks(p_c):
        p_sum = p_c[:, 0:Q_BLOCK]
        for r in range(1, R):
            p_sum = p_sum + p_c[:, r * Q_BLOCK:(r + 1) * Q_BLOCK]
        p_hi = p_sum.astype(BF16)
        p_lo = (p_sum - p_hi.astype(F32)).astype(BF16)
        imp = _dot(ov_ref[...], p_hi) + _dot(ov_ref[...], p_lo)
        n_slc = imp.shape[0]
        jj = lax.broadcasted_iota(jnp.int32, imp.shape, 0)
        qq = lax.broadcasted_iota(jnp.int32, imp.shape, 1)
        cur = 2 * i + jnp.where(qq >= SLC_LEN, 1, 0)
        forced = (jj == 0) | (jj == cur) | (jj == cur - 1)
        score = jnp.where(forced, TAKEN, jnp.where(jj <= cur, imp, -1.0))
        grp = 8
        j_rows = [jj[a:a + grp] for a in range(0, n_slc, grp)]
        for _ in range(min(SLC_TOPN, n_slc) - N_FORCED):
            cand = [(score[a:a + grp], j_rows[a // grp]) for a in range(0, n_slc, grp)]
            while len(cand) > 1:
                nxt = []
                for a in range(0, len(cand) - 1, 2):
                    (s0, j0), (s1, j1) = cand[a], cand[a + 1]
                    right = s1 > s0
                    nxt.append((jnp.maximum(s0, s1), jnp.where(right, j1, j0)))
                cand = nxt + cand[len(cand) - len(cand) % 2:]
            s8, j8 = cand[0]
            best = jnp.max(s8, axis=0, keepdims=True)
            first = jnp.min(jnp.where(s8 == best, j8, n_slc), axis=0, keepdims=True)
            score = jnp.where(jj == first, TAKEN, score)
        neg_ref[:n_slc] = jnp.where(score == TAKEN, msel_ref[0], NEG_INF)
        neg_ref[n_slc:] = jnp.full((NEG_ROWS, Q_BLOCK), NEG_INF, F32)

    def add_chunk(st, s, v_t):
        e = jnp.exp2(s)
        return st[0] + jnp.sum(e, axis=0, keepdims=True), st[1] + _dot(v_t, e.astype(BF16))

    zero = (jnp.zeros((1, N), F32), jnp.zeros((HEAD_DIM, N), F32))

    def selected_near():
        J0 = jnp.maximum(i - 1, 0)
        bias = sbias_ref[0, pl.ds(pl.multiple_of((2 - (i - J0)) * Q_BLOCK, Q_BLOCK), 2 * Q_BLOCK), :]
        return add_chunk(zero, _dot(k_rows(ks_ref, J0, 2), q_t) + bias + sel_mask(J0, 2), v_cols(vs_ref, J0, 2))

    def window_fast():
        J0 = jnp.maximum(i - n_win, 0)
        bias = wbias_ref[0, pl.ds(pl.multiple_of((n_win - (i - J0)) * Q_BLOCK, Q_BLOCK), (n_win + 1) * Q_BLOCK), :]
        return add_chunk(zero, _dot(k_rows(kw_ref, J0, n_win + 1), q_t) + bias, v_cols(vw_ref, J0, n_win + 1))

    def sweep_exact(k_ref, v_ref, bias_ref, far_dist, n_back, with_sel):
        def body(t, st):
            J = i - t
            off = pl.multiple_of((far_dist - jnp.minimum(t, far_dist)) * Q_BLOCK, Q_BLOCK)
            s = _dot(k_rows(k_ref, J, 1), q_t) + bias_ref[0, pl.ds(off, Q_BLOCK), :]
            if with_sel:
                s = s + sel_mask(J, 1)
            return _softmax_tile_update(s, v_ref[0, J], *st)

        init = (jnp.full((1, N), NEG_INF, F32),) + zero
        _, l, acc = lax.fori_loop(0, jnp.minimum(i, n_back) + 1, body, init)
        return l, acc

    vals = {}

    def load_phase():
        vals["cmp"] = compressed(True)
        vals["win"] = window_fast()

    def select_phase():
        select_blocks(vals["cmp"][0])

    def near_phase():
        l_near, acc_near = selected_near()
        l_ref[u] = jnp.broadcast_to(l_near, l_ref.shape[1:])
        acc_ref[u] = acc_near

    def exact_path():
        p_x, _, o_x = compressed(False)
        select_blocks(p_x)
        l_sx, a_sx = sweep_exact(ks_ref, vs_ref, sbias_ref, 2, nb, True)
        l_wx, a_wx = sweep_exact(kw_ref, vw_ref, wbias_ref, n_win, n_win, False)
        return o_x, a_sx * (1.0 / l_sx), a_wx * (1.0 / l_wx)

    def gate_row(br):
        return jnp.concatenate([gate_ref[br, 0, r:r + 1, u * Q_BLOCK:(u + 1) * Q_BLOCK] for r in range(R)], axis=1)

    def fast_result():
        _, l_c, o_c = vals["cmp"]
        l_w, a_w = vals["win"]
        l_s, a_s = l_ref[u, 0:1, :], acc_ref[u]
        lane_q = lax.broadcasted_iota(jnp.int32, (1, N), 1) % Q_BLOCK
        has_cmp = (i > 0) | (lane_q >= CMP_LEN - 1)
        l_min = jnp.minimum(jnp.minimum(l_s, l_w), jnp.where(has_cmp, l_c, 1.0))
        return l_min, (o_c, a_s * (1.0 / l_s), a_w * (1.0 / l_w))

    def combine(outs):
        o_c, o_s, o_w = outs
        o_t = gate_row(0) * o_c + gate_row(1) * o_s + gate_row(2) * o_w
        return _pairs_to_rows(o_t)

    return load_phase, select_phase, near_phase, fast_result, exact_path, combine


def _nsa_attention(colmat, rowmat, cmp_rows, cmp_cols, gates, cbias, sbias, wbias, ov_t, msel, B, S):
    nb = S // Q_BLOCK
    nc = S // CMP_STRIDE
    n_slc = S // SLC_LEN
    G, R = NSA_KV_HEADS, NSA_REP
    kern = functools.partial(_nsa_kernel, nb=nb)
    ns = nb // NSA_SUB
    return pl.pallas_call(
        kern,
        grid=(B, G, ns),
        in_specs=[
            pl.BlockSpec((R, NSA_SUB, HEAD_DIM, LANES), lambda b, g, i: (g, b * ns + i, 0, 0)),
            pl.BlockSpec((1, 1, nc, HEAD_DIM), lambda b, g, i: (g, b, 0, 0)),
            pl.BlockSpec((1, 1, HEAD_DIM, nc), lambda b, g, i: (NSA_KV_HEADS + g, b, 0, 0)),
            pl.BlockSpec((1, S, HEAD_DIM), lambda b, g, i: (ROW_KS + g, b, 0)),
            pl.BlockSpec((1, nb, HEAD_DIM, LANES), lambda b, g, i: (COL_VS + g, b, 0, 0)),
            pl.BlockSpec((1, S, HEAD_DIM), lambda b, g, i: (ROW_KW + g, b, 0)),
            pl.BlockSpec((1, nb, HEAD_DIM, LANES), lambda b, g, i: (COL_VW + g, b, 0, 0)),
            pl.BlockSpec((N_BRANCHES, 1, R, NSA_SUB * LANES), lambda b, g, i: (0, g, 0, b * ns + i)),
            pl.BlockSpec((1, cbias.shape[1], R * Q_BLOCK), lambda b, g, i: (g, 0, 0)),
            pl.BlockSpec((1, sbias.shape[1], R * Q_BLOCK), lambda b, g, i: (g, 0, 0)),
            pl.BlockSpec((1, wbias.shape[1], R * Q_BLOCK), lambda b, g, i: (g, 0, 0)),
            pl.BlockSpec((n_slc, nc), lambda b, g, i: (0, 0)),
            pl.BlockSpec((1, 1, Q_BLOCK), lambda b, g, i: (g, 0, 0)),
        ],
        out_specs=pl.BlockSpec((1, NSA_SUB * Q_BLOCK, R * HEAD_DIM), lambda b, g, i: (b, i, g)),
        out_shape=jax.ShapeDtypeStruct((B, S, NSA_HEADS * HEAD_DIM), BF16),
        scratch_shapes=[pltpu.VMEM((NSA_SUB, n_slc + NEG_ROWS, Q_BLOCK), F32)]
        + [pltpu.VMEM((FAR_TILES * Q_BLOCK, R * Q_BLOCK), F32)] * 2
        + [pltpu.VMEM((FAR_TILES * Q_BLOCK, R * Q_BLOCK), BF16)] * 2
        + [pltpu.VMEM((NSA_SUB, HEAD_DIM, R * Q_BLOCK), F32),
           pltpu.VMEM((NSA_SUB, 8, R * Q_BLOCK), F32)],
        compiler_params=_cparams(("parallel", "parallel", "arbitrary")),
        name="nsa_attention",
    )(colmat, cmp_rows, cmp_cols, rowmat, colmat, rowmat, colmat, gates, cbias, sbias, wbias, ov_t, msel)


SB_QB = 512
SB_KT = 256
SB_ZERO_LOG = -160.0


def _sb_kernel(q_ref, k_ref, v_ref, tri_ref, o_ref):
    i = pl.program_id(1)
    H = SB_HEADS
    tri = tri_ref[...]
    assert SB_QB == 2 * SB_KT
    below_diag = (lax.broadcasted_iota(jnp.int32, (SB_KT, SB_KT), 1)
                  < lax.broadcasted_iota(jnp.int32, (SB_KT, SB_KT), 0))

    def step(J, rows, st, masked):
        out = []
        for h in range(H):
            carry, acc = st[h]
            k_t = jnp.concatenate([k_ref[h, (SB_KT // LANES) * J + c] for c in range(SB_KT // LANES)], axis=1)
            z = _dot(q_ref[h, rows, :], k_t)
            sp = jnp.maximum(z, 0.0) + jnp.log2(1.0 + jnp.exp2(-jnp.abs(z)))
            lf = jnp.where(below_diag, -sp, 0.0) if masked else -sp
            lf_hi = lf.astype(BF16)
            lf_lo = (lf - lf_hi.astype(F32)).astype(BF16)
            after = _dot(lf_hi, tri) + _dot(lf_lo, tri) + carry
            w = jnp.exp2(z - sp + after)
            if masked:
                w = jnp.where(below_diag, w, 0.0)
            v = v_ref[h, pl.ds(pl.multiple_of(J * SB_KT, SB_KT), SB_KT), :]
            acc = acc + _dot(w.astype(BF16), v)
            carry = carry + jnp.sum(lf, axis=1, keepdims=True)
            out.append((carry, acc))
        return tuple(out)

    lo, hi = slice(0, SB_KT), slice(SB_KT, SB_QB)
    zero = tuple((jnp.zeros((SB_KT, 1), F32), jnp.zeros((SB_KT, HEAD_DIM), F32)) for _ in range(H))
    st_hi = step(2 * i, hi, step(2 * i + 1, hi, zero, True), False)
    st_lo = step(2 * i, lo, zero, True)

    def alive(st):
        worst = st[0][0]
        for h in range(1, H):
            worst = jnp.maximum(worst, st[h][0])
        return jnp.max(worst) > SB_ZERO_LOG

    def sweep(rows, st):
        def body(c):
            J, _, st = c
            st = step(J, rows, st, False)
            return J - 1, alive(st), st

        return lax.while_loop(lambda c: (c[0] >= 0) & c[1], body, (2 * i - 1, alive(st), st))[2]

    st_lo, st_hi = sweep(lo, st_lo), sweep(hi, st_hi)
    o_ref[0, lo, :] = jnp.concatenate([st_lo[h][1] for h in range(H)], axis=1).astype(o_ref.dtype)
    o_ref[0, hi, :] = jnp.concatenate([st_hi[h][1] for h in range(H)], axis=1).astype(o_ref.dtype)


def _sb_attention(colmat, rowmat, tri, B, S):
    nq = S // SB_QB
    nb = S // LANES
    H = SB_HEADS
    return pl.pallas_call(
        _sb_kernel,
        grid=(B, nq),
        in_specs=[
            pl.BlockSpec((H, SB_QB, HEAD_DIM), lambda b, i: (ROW_SQ // H, b * nq + i, 0)),
            pl.BlockSpec((H, nb, HEAD_DIM, LANES), lambda b, i: (COL_SK // H, b, 0, 0)),
            pl.BlockSpec((H, S, HEAD_DIM), lambda b, i: (ROW_SV // H, b, 0)),
            pl.BlockSpec((SB_KT, SB_KT), lambda b, i: (0, 0)),
        ],
        out_specs=pl.BlockSpec((1, SB_QB, H * HEAD_DIM), lambda b, i: (b, i, 0)),
        out_shape=jax.ShapeDtypeStruct((B, S, H * HEAD_DIM), BF16),
        compiler_params=_cparams(("parallel", "arbitrary")),
        name="sb_attention",
    )(rowmat, colmat, rowmat, tri)


SWA_SUB = 8


def _swa_kernel(q_ref, k_ref, v_ref, bias_ref, sink_ref, o_ref):
    R = SWA_REP
    sink = sink_ref[0]
    outs = []
    for u in range(SWA_SUB):
        i = pl.program_id(2) * SWA_SUB + u
        q_t = jnp.concatenate([q_ref[r, u] for r in range(R)], axis=1)
        J0 = jnp.maximum(i - 1, 0)
        k = k_ref[0, pl.ds(pl.multiple_of(J0 * Q_BLOCK, Q_BLOCK), 2 * Q_BLOCK), :]
        bias = bias_ref[0, pl.ds(pl.multiple_of((1 - (i - J0)) * Q_BLOCK, Q_BLOCK), 2 * Q_BLOCK), :]
        s = _dot(k, q_t) + bias
        m = jnp.maximum(jnp.max(s, axis=0, keepdims=True), sink)
        e = jnp.exp(s - m)
        den = jnp.sum(e, axis=0, keepdims=True) + jnp.exp(sink - m)
        v_t = jnp.concatenate([v_ref[0, J0], v_ref[0, J0 + 1]], axis=1)
        o_t = _dot(v_t, e.astype(BF16)) * (1.0 / den)
        outs.append(_pairs_to_rows(o_t).astype(o_ref.dtype))
    for u in range(SWA_SUB):
        o_ref[0, u * Q_BLOCK:(u + 1) * Q_BLOCK, :] = outs[u]


def _swa_attention(colmat, rowmat, bias, sinks, B, S):
    nb = S // Q_BLOCK
    ns = nb // SWA_SUB
    G, R = SWA_KV_HEADS, SWA_REP
    return pl.pallas_call(
        _swa_kernel,
        grid=(B, G, ns),
        in_specs=[
            pl.BlockSpec((R, SWA_SUB, HEAD_DIM, LANES), lambda b, g, i: (COL_WQ // R + g, b * ns + i, 0, 0)),
            pl.BlockSpec((1, S, HEAD_DIM), lambda b, g, i: (ROW_WK + g, b, 0)),
            pl.BlockSpec((1, nb, HEAD_DIM, LANES), lambda b, g, i: (COL_WV + g, b, 0, 0)),
            pl.BlockSpec((1, 3 * Q_BLOCK, R * Q_BLOCK), lambda b, g, i: (g, 0, 0)),
            pl.BlockSpec((1, 1, R * Q_BLOCK), lambda b, g, i: (g, 0, 0)),
        ],
        out_specs=pl.BlockSpec((1, SWA_SUB * Q_BLOCK, R * HEAD_DIM), lambda b, g, i: (b, i, g)),
        out_shape=jax.ShapeDtypeStruct((B, S, SWA_HEADS * HEAD_DIM), BF16),
        compiler_params=_cparams(("parallel", "parallel", "arbitrary")),
        name="swa_attention",
    )(colmat, rowmat, colmat, bias, sinks)


MERGE_TM = 512


def _merge_kernel(x_ref, nw_ref, wg_ref, ya_ref, yb_ref, yc_ref, ua_ref, ub_ref, uc_ref, wo_ref, o_ref):
    x = x_ref[...]
    h = _rms_rows(x, nw_ref[...]).astype(BF16)
    merged = None
    for br, (y_ref, u_ref) in enumerate(((ya_ref, ua_ref), (yb_ref, ub_ref), (yc_ref, uc_ref))):
        g = jax.nn.sigmoid(_dot(h, wg_ref[:, br * D_MODEL:(br + 1) * D_MODEL]))
        t = g * _dot(y_ref[...], u_ref[...])
        merged = t if merged is None else merged + t
    o_ref[...] = x + _dot(merged.astype(BF16), wo_ref[...])


def _merge(x2, nw, w_bgate, y_nsa, y_sb, y_swa, u_nsa, u_sb, u_swa, w_out, l):
    T = x2.shape[0]
    full = lambda w: _layer_spec(l, w.shape[1:])
    rows = lambda n: pl.BlockSpec((MERGE_TM, n), lambda i: (i, 0))
    return pl.pallas_call(
        _merge_kernel,
        grid=(T // MERGE_TM,),
        in_specs=[rows(D_MODEL), pl.BlockSpec((1, D_MODEL), lambda i: (0, 0)), full(w_bgate),
                  rows(y_nsa.shape[1]), rows(y_sb.shape[1]), rows(y_swa.shape[1]),
                  full(u_nsa), full(u_sb), full(u_swa), full(w_out)],
        out_specs=rows(D_MODEL),
        out_shape=jax.ShapeDtypeStruct((T, D_MODEL), F32),
        compiler_params=_cparams(("parallel",)),
        name="merge",
    )(x2, nw.reshape(1, D_MODEL), w_bgate, y_nsa, y_sb, y_swa, u_nsa, u_sb, u_swa, w_out)


def _t5_bucket(dist):
    max_exact = NUM_BUCKETS // 2
    d = jnp.maximum(dist, 0)
    df = jnp.maximum(d, 1).astype(F32)
    large = max_exact + (jnp.log(df / max_exact) / math.log(MAX_DISTANCE / max_exact)
                         * (NUM_BUCKETS - max_exact)).astype(jnp.int32)
    large = jnp.minimum(large, NUM_BUCKETS - 1)
    return jnp.where(d < max_exact, d, large)


def _toeplitz(ext, delta):
    H = ext.shape[0]
    n = Q_BLOCK
    c = delta + MAX_DISTANCE
    w = ext[:, c - (n - 1):c + n]
    flat = jnp.pad(jnp.broadcast_to(w[:, None, :], (H, n, 2 * n - 1)), ((0, 0), (0, 0), (0, 1))).reshape(H, 2 * n * n)
    return flat[:, n - 1:n - 1 + n * (2 * n - 1)].reshape(H, n, 2 * n - 1)[:, :, :n]


def _lanes(t, G, R):
    K, Q = t.shape[1:]
    return t.reshape(G, R, K, Q).transpose(0, 2, 1, 3).reshape(G, K, R * Q)


def _bias_tables(rel_bias, nb, qk_bound):
    n = Q_BLOCK
    G, R = NSA_KV_HEADS, NSA_REP
    by_dist = rel_bias[_t5_bucket(jnp.arange(MAX_DISTANCE)), :].T.astype(F32)
    far = rel_bias[NUM_BUCKETS - 1, :].astype(F32)
    neg = jnp.full((1, n), NEG_INF, F32)
    k = jnp.arange(n)[:, None]
    q = jnp.arange(n)[None, :]
    above = (k > q)[None]

    rel = (by_dist[:NSA_HEADS] - far[:NSA_HEADS, None]) * LOG2E
    m_g = qk_bound + jnp.maximum(jnp.max(rel.reshape(G, -1), axis=1), 0.0)
    m_h = jnp.repeat(m_g, R)[:, None, None]
    ext = jnp.concatenate([jnp.broadcast_to(neg, (NSA_HEADS, n)), rel, jnp.zeros((NSA_HEADS, n), F32)], axis=1)
    t0, t1 = _toeplitz(ext, 0), _toeplitz(ext, n)
    zeros = jnp.zeros_like(t0)
    masked = jnp.full_like(t0, NEG_INF)
    sbias = _lanes(jnp.concatenate([zeros, t1, t0, masked], axis=1), G, R)
    n_win = NSA_WINDOW // n
    edge = jnp.where(above, zeros, NEG_INF)
    tiles = [edge] + [zeros] * (n_win - 2) + [t1, t0]
    wbias = _lanes(jnp.concatenate([t - m_h for t in tiles] + [masked] * n_win, axis=1), G, R)
    off = 8 * (nb - 1)
    near = jnp.stack([ext[:, 97 - CMP_STRIDE * m:97 - CMP_STRIDE * m + n] for m in range(-9, 7)], axis=1)
    n_rows = 16 * nb - 8
    cb = jnp.concatenate([jnp.zeros((NSA_HEADS, off - 9, n), F32), near,
                          jnp.full((NSA_HEADS, n_rows - off - 7, n), NEG_INF, F32)], axis=1) - m_h
    cbias = _lanes(cb, G, R)
    msel = jnp.broadcast_to(-m_g[:, None, None], (G, 1, n))

    hs = slice(NSA_HEADS, NSA_HEADS + SWA_HEADS)
    ext_w = jnp.concatenate([jnp.broadcast_to(neg, (SWA_HEADS, n)), by_dist[hs],
                             jnp.broadcast_to(far[hs, None], (SWA_HEADS, n))], axis=1)
    s0 = _toeplitz(ext_w, 0)
    s1 = jnp.where(above, _toeplitz(ext_w, n), NEG_INF)
    swa_bias = _lanes(jnp.concatenate([s1, s0, jnp.full_like(s0, NEG_INF)], axis=1), SWA_KV_HEADS, SWA_REP)
    return cbias, sbias, wbias, msel, swa_bias


def _overlap_t(S):
    nc, n_slc = S // CMP_STRIDE, S // SLC_LEN
    n = np.arange(nc)[None, :]
    j = np.arange(n_slc)[:, None]
    ov = (n * CMP_STRIDE < (j + 1) * SLC_LEN) & (n * CMP_STRIDE + CMP_LEN - 1 >= j * SLC_LEN) & (n < nc - 1)
    return jnp.asarray(ov, BF16)


def _head_gains(heads, norm_of, scale_of):
    ones = jnp.ones((HEAD_DIM,), F32)
    gains = [norm_of[name].astype(F32) * scale_of.get(name, 1.0) if name in norm_of else ones for name, _ in heads]
    return jnp.concatenate(gains).reshape(1, -1)


def kernel(x, rel_bias, ffn1_norm, ffn1_w_gate, ffn1_w_up, ffn1_w_down, mix_norm, w_in, nsa_q_norm, nsa_k_norm, nsa_cmp_pos, nsa_cmp_k_w1, nsa_cmp_k_w2, nsa_cmp_v_w1, nsa_cmp_v_w2, swa_q_norm, swa_k_norm, swa_sinks, w_up_nsa, w_up_sb, w_up_swa, w_out, ffn2_norm, ffn2_w_gate, ffn2_w_up, ffn2_w_down):
    B, S, D = x.shape
    T = B * S
    nb = S // Q_BLOCK
    nc = S // CMP_STRIDE
    depth = w_in.shape[0]
    scale = HEAD_DIM ** -0.5
    scale_of = {"nq": scale * LOG2E, "sq": scale * LOG2E, "wq": scale}
    qk_bound = (BOUND_SLACK * LOG2E * HEAD_DIM ** 0.5
                * jnp.max(jnp.abs(nsa_q_norm.astype(F32))) * jnp.max(jnp.abs(nsa_k_norm.astype(F32))))
    cbias, sbias, wbias, msel, swa_bias = _bias_tables(rel_bias, nb, qk_bound)
    plan = _proj_plan(scale_of)
    ov_t = _overlap_t(S)
    tri = jnp.asarray(np.tril(np.ones((SB_KT, SB_KT), np.float32), -1), BF16)

    (ffn1_w_gate, ffn1_w_up, ffn1_w_down, ffn2_w_gate, ffn2_w_up, ffn2_w_down, w_up_nsa, w_up_sb, w_up_swa,
     w_out) = (w.astype(BF16) for w in (ffn1_w_gate, ffn1_w_up, ffn1_w_down, ffn2_w_gate, ffn2_w_up, ffn2_w_down,
                                        w_up_nsa, w_up_sb, w_up_swa, w_out))
    tail0 = OFF_NGATE + N_GATE
    w_head, w_tail = w_in[:, :, :OFF_NGATE].astype(BF16), w_in[:, :, tail0:].astype(BF16)
    w_bgate = w_tail[:, :, OFF_BGATE - tail0:]
    w_gate = jnp.pad(w_in[:, :, OFF_NGATE:tail0], ((0, 0), (0, 0), (0, LANES - N_GATE))).astype(BF16)
    runs = []
    for _, off in ROW_HEADS + COL_HEADS:
        if runs and runs[-1][1] == off:
            runs[-1][1] = off + HEAD_DIM
        else:
            runs.append([off, off + HEAD_DIM])
    pieces = [w_head[:, :, a:b] if a < OFF_NGATE else w_tail[:, :, a - tail0:b - tail0] for a, b in runs]
    w_proj = jnp.concatenate(pieces + [w_gate], axis=2)

    x2 = x.reshape(T, D)
    for l in range(depth):
        x2 = _ffn(x2, ffn1_norm[l], ffn1_w_gate, ffn1_w_up, ffn1_w_down, l)

        norm_of = {"nq": nsa_q_norm[l], "nks": nsa_k_norm[l], "nkw": nsa_k_norm[l],
                   "wq": swa_q_norm[l], "wk": swa_k_norm[l]}
        gain = jnp.concatenate([_head_gains(ROW_HEADS, norm_of, scale_of), _head_gains(COL_HEADS, norm_of, scale_of),
                                jnp.ones((1, LANES), F32)], axis=1)
        rowmat, colmat, gates = _proj(x2, mix_norm[l], w_proj, gain, plan, l)
        gates = gates[:N_GATE].reshape(N_BRANCHES, NSA_KV_HEADS, NSA_REP, T)

        rows4 = rowmat[ROW_KC:ROW_KC + 2 * NSA_KV_HEADS].reshape(2 * NSA_KV_HEADS, B, nc, CMP_STRIDE * HEAD_DIM)
        cmp_rows, cmp_cols = _compress(rows4, jnp.stack([nsa_cmp_k_w1[l], nsa_cmp_v_w1[l]]),
                                       jnp.stack([nsa_cmp_k_w2[l], nsa_cmp_v_w2[l]]),
                                       nsa_cmp_pos[l], nsa_k_norm[l])

        y_nsa = _nsa_attention(colmat, rowmat, cmp_rows, cmp_cols, gates, cbias, sbias, wbias, ov_t, msel, B, S)
        y_sb = _sb_attention(colmat, rowmat, tri, B, S)
        sinks = jnp.repeat(swa_sinks[l].astype(F32).reshape(SWA_KV_HEADS, 1, SWA_REP), Q_BLOCK, axis=2)
        y_swa = _swa_attention(colmat, rowmat, swa_bias, sinks, B, S)

        x2 = _merge(x2, mix_norm[l], w_bgate, y_nsa.reshape(T, -1), y_sb.reshape(T, -1),
                    y_swa.reshape(T, -1), w_up_nsa, w_up_sb, w_up_swa, w_out, l)
        x2 = _ffn(x2, ffn2_norm[l], ffn2_w_gate, ffn2_w_up, ffn2_w_down, l)
    return x2.reshape(B, S, D)
```

```python
import functools
import math

import numpy as np
import jax
import jax.numpy as jnp
from jax import lax
from jax.experimental import pallas as pl
from jax.experimental.pallas import tpu as pltpu

D_MODEL = 1024
HEAD_DIM = 64
Q_BLOCK = 128
NSA_HEADS = 8
NSA_KV_HEADS = 2
NSA_REP = NSA_HEADS // NSA_KV_HEADS
CMP_LEN = 32
CMP_STRIDE = 16
CMP_HIDDEN = 256
SLC_LEN = 64
SLC_TOPN = 16
NSA_WINDOW = 512
SB_HEADS = 4
SWA_HEADS = 4
SWA_KV_HEADS = 2
SWA_REP = SWA_HEADS // SWA_KV_HEADS
SWA_WINDOW = 128
NUM_BUCKETS = 32
MAX_DISTANCE = 128
D_FF = 2816
NORM_EPS = 1e-6
NEG_INF = -1e30
N_BRANCHES = 3
N_GATE = 3 * NSA_HEADS

LANES = 128
VMEM_LIMIT = 56 * 1024 * 1024
BF16 = jnp.bfloat16
F32 = jnp.float32

_sizes = (NSA_HEADS * HEAD_DIM,) + (NSA_KV_HEADS * HEAD_DIM,) * 6 + (N_GATE,) \
    + (SB_HEADS * HEAD_DIM,) * 3 + (SWA_HEADS * HEAD_DIM, SWA_KV_HEADS * HEAD_DIM,
                                    SWA_KV_HEADS * HEAD_DIM, N_BRANCHES * D_MODEL)
_offs = np.concatenate([[0], np.cumsum(_sizes)])
(OFF_NQ, OFF_NKC, OFF_NVC, OFF_NKS, OFF_NVS, OFF_NKW, OFF_NVW, OFF_NGATE,
 OFF_SQ, OFF_SK, OFF_SV, OFF_WQ, OFF_WK, OFF_WV, OFF_BGATE) = (int(o) for o in _offs[:-1])

ROW_HEADS = ([("nkc", OFF_NKC + 64 * g) for g in range(2)] + [("nvc", OFF_NVC + 64 * g) for g in range(2)]
             + [("nks", OFF_NKS + 64 * g) for g in range(2)] + [("nkw", OFF_NKW + 64 * g) for g in range(2)]
             + [("sq", OFF_SQ + 64 * h) for h in range(4)] + [("sv", OFF_SV + 64 * h) for h in range(4)]
             + [("wk", OFF_WK + 64 * g) for g in range(2)])
ROW_KC, ROW_VC, ROW_KS, ROW_KW, ROW_SQ, ROW_SV, ROW_WK = 0, 2, 4, 6, 8, 12, 16
COL_HEADS = ([("nq", OFF_NQ + 64 * h) for h in range(8)] + [("nvs", OFF_NVS + 64 * g) for g in range(2)]
             + [("nvw", OFF_NVW + 64 * g) for g in range(2)] + [("sk", OFF_SK + 64 * h) for h in range(4)]
             + [("wq", OFF_WQ + 64 * h) for h in range(4)] + [("wv", OFF_WV + 64 * g) for g in range(2)])
COL_NQ, COL_VS, COL_VW, COL_SK, COL_WQ, COL_WV = 0, 8, 10, 12, 16, 20


def _dot(a, b):
    return jnp.dot(a, b, preferred_element_type=F32)


def _rms_rows(x, w):
    ms = jnp.mean(x * x, axis=-1, keepdims=True)
    return x * lax.rsqrt(ms + NORM_EPS) * w


def _cparams(sem, flags=None):
    return pltpu.CompilerParams(dimension_semantics=sem, vmem_limit_bytes=VMEM_LIMIT, flags=flags)


FFN_TM = 1024
MXU_TILE = 256
FFN_FC = MXU_TILE


def _ffn_kernel(x_ref, nw_ref, wg_ref, wu_ref, wd_ref, o_ref):
    x = x_ref[...]
    h = _rms_rows(x, nw_ref[...]).astype(BF16)
    acc = jnp.zeros(x.shape, F32)
    for c in range(D_FF // FFN_FC):
        cols = slice(c * FFN_FC, (c + 1) * FFN_FC)
        g = _dot(h, wg_ref[:, cols])
        u = _dot(h, wu_ref[:, cols])
        a = (g * jax.nn.sigmoid(g)) * u
        acc = acc + _dot(a.astype(BF16), wd_ref[cols, :])
    o_ref[...] = x + 0.5 * acc


def _layer_spec(l, shape, **kw):
    return pl.BlockSpec((None,) + tuple(shape), lambda i: (l, 0, 0), **kw)


def _ffn(x2, nw, wg, wu, wd, l):
    T = x2.shape[0]
    resident = lambda shape: _layer_spec(l, shape, pipeline_mode=pl.Buffered(1))
    return pl.pallas_call(
        _ffn_kernel,
        grid=(T // FFN_TM,),
        in_specs=[
            pl.BlockSpec((FFN_TM, D_MODEL), lambda i: (i, 0)),
            pl.BlockSpec((1, D_MODEL), lambda i: (0, 0)),
            resident((D_MODEL, D_FF)), resident((D_MODEL, D_FF)), resident((D_FF, D_MODEL)),
        ],
        out_specs=pl.BlockSpec((FFN_TM, D_MODEL), lambda i: (i, 0)),
        out_shape=jax.ShapeDtypeStruct((T, D_MODEL), F32),
        compiler_params=_cparams(("parallel",)),
        name="ffn",
    )(x2, nw.reshape(1, D_MODEL), wg, wu, wd)


PROJ_TM = 512


GATE_ROWS = 32

_NORMED = ("nq", "nks", "nkw", "wq", "wk")


def _proj_plan(scale_of):
    plan = []
    for kind, heads in (("row", ROW_HEADS), ("col", COL_HEADS)):
        for p in range(len(heads) // 2):
            name = heads[2 * p][0]
            assert heads[2 * p + 1][0] == name
            plan.append((kind, p, name in _NORMED, 1.0 if name in _NORMED else scale_of.get(name, 1.0)))
    plan.append(("gate", 0, False, 1.0))
    return tuple(plan)


def _proj_kernel(x_ref, nw_ref, w_ref, gain_ref, bd_ref, row_ref, col_ref, gate_ref, *, plan):
    h = _rms_rows(x_ref[...], nw_ref[...]).astype(BF16)
    pair = 2 * MXU_TILE
    for c0 in range(0, len(plan) * LANES, pair):
        width = min(pair, len(plan) * LANES - c0)
        zg = _dot(h, w_ref[:, c0:c0 + width])
        for u in range(width // LANES):
            t = c0 // LANES + u
            kind, p, normed, scale = plan[t]
            z = zg[:, u * LANES:(u + 1) * LANES]
            if normed:
                ms = _dot((z * z).astype(BF16), bd_ref[...])
                z = z * lax.rsqrt(ms + NORM_EPS) * gain_ref[:, t * LANES:(t + 1) * LANES]
            elif scale != 1.0:
                z = z * scale
            if kind == "row":
                row_ref[2 * p] = z[:, :HEAD_DIM].astype(row_ref.dtype)
                row_ref[2 * p + 1] = z[:, HEAD_DIM:].astype(row_ref.dtype)
            elif kind == "col":
                z_t = z.T
                for c in range(PROJ_TM // LANES):
                    col_ref[2 * p, c] = z_t[:HEAD_DIM, c * LANES:(c + 1) * LANES].astype(col_ref.dtype)
                    col_ref[2 * p + 1, c] = z_t[HEAD_DIM:, c * LANES:(c + 1) * LANES].astype(col_ref.dtype)
            else:
                gate_ref[...] = jax.nn.sigmoid(z).T[:GATE_ROWS]


def _proj(x2, nw, w, gain, plan, l):
    T = x2.shape[0]
    n_row, n_col = len(ROW_HEADS), len(COL_HEADS)
    bd = np.kron(np.eye(2, dtype=np.float32), np.full((HEAD_DIM, HEAD_DIM), 1.0 / HEAD_DIM, np.float32))
    return pl.pallas_call(
        functools.partial(_proj_kernel, plan=plan),
        grid=(T // PROJ_TM,),
        in_specs=[
            pl.BlockSpec((PROJ_TM, D_MODEL), lambda i: (i, 0)),
            pl.BlockSpec((1, D_MODEL), lambda i: (0, 0)),
            _layer_spec(l, w.shape[1:]),
            pl.BlockSpec(gain.shape, lambda i: (0, 0)),
            pl.BlockSpec((LANES, LANES), lambda i: (0, 0)),
        ],
        out_specs=[
            pl.BlockSpec((n_row, PROJ_TM, HEAD_DIM), lambda i: (0, i, 0)),
            pl.BlockSpec((n_col, PROJ_TM // LANES, HEAD_DIM, LANES), lambda i: (0, i, 0, 0)),
            pl.BlockSpec((GATE_ROWS, PROJ_TM), lambda i: (0, i)),
        ],
        out_shape=[jax.ShapeDtypeStruct((n_row, T, HEAD_DIM), BF16),
                   jax.ShapeDtypeStruct((n_col, T // LANES, HEAD_DIM, LANES), BF16),
                   jax.ShapeDtypeStruct((GATE_ROWS, T), F32)],
        compiler_params=_cparams(("parallel",)),
        name="proj",
    )(x2, nw.reshape(1, D_MODEL), w, gain, jnp.asarray(bd, BF16))


def _compress_kernel(r_ref, w1_ref, w2_ref, pos_ref, kn_ref, row_ref, col_ref):
    kind = pl.program_id(0) // NSA_KV_HEADS
    r = r_ref[0, 0]
    nc = r.shape[0]
    half = CMP_STRIDE * HEAD_DIM
    w1 = w1_ref[0]
    p_lo = _dot(r, w1[:half])
    p_hi = _dot(r, w1[half:])
    p_pos = _dot(jnp.broadcast_to(pos_ref[...], (8, CMP_LEN * HEAD_DIM)).astype(BF16), w1)[:1]
    hdn = p_lo + pltpu.roll(p_hi, nc - 1, 0) + p_pos
    hdn = hdn * jax.nn.sigmoid(hdn)
    out = _dot(hdn.astype(BF16), w2_ref[0])
    ms = jnp.sum(out * out, axis=-1, keepdims=True) * (1.0 / HEAD_DIM)
    normed = out * lax.rsqrt(ms + NORM_EPS) * kn_ref[...]
    out = jnp.where(kind == 0, normed, out)
    row_ref[0, 0] = out[:, :HEAD_DIM].astype(row_ref.dtype)
    col_ref[0, 0] = out.T[:HEAD_DIM].astype(col_ref.dtype)


def _compress(rows4, w1, w2, pos, k_norm):
    assert ROW_KC == 0 and ROW_VC == NSA_KV_HEADS
    _, B, NC, _ = rows4.shape
    kn = jnp.pad(k_norm.reshape(1, HEAD_DIM), ((0, 0), (0, LANES - HEAD_DIM)))
    w2p = jnp.pad(w2, ((0, 0), (0, 0), (0, LANES - HEAD_DIM))).astype(BF16)
    return pl.pallas_call(
        _compress_kernel,
        grid=(4, B),
        in_specs=[
            pl.BlockSpec((1, 1, NC, CMP_STRIDE * HEAD_DIM), lambda h, b: (h, b, 0, 0)),
            pl.BlockSpec((1, CMP_LEN * HEAD_DIM, CMP_HIDDEN), lambda h, b: (h // NSA_KV_HEADS, 0, 0)),
            pl.BlockSpec((1, CMP_HIDDEN, LANES), lambda h, b: (h // NSA_KV_HEADS, 0, 0)),
            pl.BlockSpec((1, CMP_LEN * HEAD_DIM), lambda h, b: (0, 0)),
            pl.BlockSpec((1, LANES), lambda h, b: (0, 0)),
        ],
        out_specs=[
            pl.BlockSpec((1, 1, NC, HEAD_DIM), lambda h, b: (h, b, 0, 0)),
            pl.BlockSpec((1, 1, HEAD_DIM, NC), lambda h, b: (h, b, 0, 0)),
        ],
        out_shape=[jax.ShapeDtypeStruct((4, B, NC, HEAD_DIM), BF16),
                   jax.ShapeDtypeStruct((4, B, HEAD_DIM, NC), BF16)],
        compiler_params=_cparams(("parallel", "parallel")),
        name="nsa_compress",
    )(rows4, w1.astype(BF16), w2p, pos.reshape(1, CMP_LEN * HEAD_DIM), kn)


def _softmax_tile_update(s, v_t, m, l, acc):
    m_new = jnp.maximum(m, jnp.max(s, axis=0, keepdims=True))
    alpha = jnp.exp2(m - m_new)
    p = jnp.exp2(s - m_new)
    l_new = alpha * l + jnp.sum(p, axis=0, keepdims=True)
    acc_new = alpha * acc + _dot(v_t, p.astype(BF16))
    return m_new, l_new, acc_new


def _pairs_to_rows(o_t):
    n = o_t.shape[1] // LANES
    outs = []
    for p in range(n // 2):
        a = o_t[:, (2 * p) * LANES:(2 * p + 1) * LANES]
        b = o_t[:, (2 * p + 1) * LANES:(2 * p + 2) * LANES]
        outs.append(jnp.concatenate([a, b], axis=0).T)
    return outs[0] if len(outs) == 1 else jnp.concatenate(outs, axis=1)


FAR_TILES = 4
NSA_SUB = 8
TAKEN = -2.0
NEG_ROWS = 8
N_FORCED = 3
LOG2E = math.log2(math.e)
NSA_TINY = 2.0 ** -100
BOUND_SLACK = 1.05


def _nsa_kernel(q_ref, kc_ref, vc_ref, ks_ref, vs_ref, kw_ref, vw_ref, gate_ref,
                cbias_ref, sbias_ref, wbias_ref, ov_ref, msel_ref, o_ref,
                neg_all, s_a, s_b, e_a, e_b, acc_ref, l_ref, *, nb):
    refs = (q_ref, kc_ref, vc_ref, ks_ref, vs_ref, kw_ref, vw_ref, gate_ref, cbias_ref, sbias_ref, wbias_ref,
            ov_ref, msel_ref)
    blocks = [_nsa_block(u, refs, neg_all.at[u], acc_ref, l_ref, nb) for u in range(NSA_SUB)]
    for phase in range(3):
        for blk in blocks:
            blk[phase]()
    _far_sweep(q_ref, ks_ref, vs_ref, neg_all, (s_a, s_b, e_a, e_b), acc_ref, l_ref, nb)
    fast = [blk[3]() for blk in blocks]
    l_min = fast[0][0]
    for l_u, _ in fast[1:]:
        l_min = jnp.minimum(l_min, l_u)
    outs = lax.cond(jnp.min(l_min) > NSA_TINY, lambda: tuple(o for _, o in fast),
                    lambda: tuple(blk[4]() for blk in blocks))
    for u, blk in enumerate(blocks):
        o_ref[0, u * Q_BLOCK:(u + 1) * Q_BLOCK, :] = blk[5](outs[u]).astype(o_ref.dtype)


def _far_sweep(q_ref, ks_ref, vs_ref, neg_all, stage_refs, acc_ref, l_ref, nb):
    R = NSA_REP
    n_slc = neg_all.shape[1] - NEG_ROWS
    i0 = pl.program_id(2) * NSA_SUB
    n_far = [jnp.maximum(i0 + u - 1, 0) for u in range(NSA_SUB)]
    offs = [0]
    for u in range(NSA_SUB):
        offs.append(offs[-1] + (n_far[u] + FAR_TILES - 1) // FAR_TILES)
    n_tot = offs[-1]

    def locate(c):
        u = sum(jnp.where(c >= offs[k], 1, 0) for k in range(1, NSA_SUB))
        start = sum(jnp.where(u == k, offs[k], 0) for k in range(NSA_SUB))
        far = sum(jnp.where(u == k, n_far[k], 0) for k in range(NSA_SUB))
        tile = FAR_TILES * (c - start)
        return u, jnp.clip(tile, 0, nb - FAR_TILES), far - tile

    def k_rows(J0):
        return ks_ref[0, pl.ds(pl.multiple_of(J0 * Q_BLOCK, Q_BLOCK), FAR_TILES * Q_BLOCK), :]

    def v_cols(J0):
        return jnp.concatenate([vs_ref[0, J0 + t] for t in range(FAR_TILES)], axis=1)

    def q_cols(u):
        return jnp.concatenate([q_ref[r, u] for r in range(R)], axis=1)

    def sel_mask(u, J0, n_valid):
        parts = []
        for t in range(2 * FAR_TILES):
            row = jnp.where(t // 2 < n_valid, 2 * J0 + t, n_slc)
            parts.append(jnp.broadcast_to(neg_all[u, pl.ds(row, 1), :], (SLC_LEN, Q_BLOCK)))
        return jnp.concatenate([jnp.concatenate(parts, axis=0)] * R, axis=1)

    def add_values(c, e_ref):
        u, tile, _ = locate(c)
        acc_ref[u] = acc_ref[u] + _dot(v_cols(tile), e_ref[...])

    def stage(c, s_in, s_out, e_in, e_out):
        u0, tile0, _ = locate(c - 1)
        u1, tile1, valid1 = locate(c)
        u2, tile2, _ = locate(c + 1)
        v_prev, acc_prev = v_cols(tile0), acc_ref[u0]
        mask, l_cur = sel_mask(u1, tile1, valid1), l_ref[u1]
        k_next, q_next = k_rows(tile2), q_cols(u2)
        pv = _dot(v_prev, e_in[...])
        s_out[...] = _dot(k_next, q_next)
        e = jnp.exp2(s_in[...] + mask)
        e_out[...] = e.astype(BF16)
        acc_ref[u0] = acc_prev + pv
        l_ref[u1] = l_cur + jnp.sum(e, axis=0, keepdims=True)

    s_a, s_b, e_a, e_b = stage_refs
    e_b[...] = jnp.zeros_like(e_b)
    u_first, tile_first, _ = locate(0)
    s_a[...] = _dot(k_rows(tile_first), q_cols(u_first))

    def pair(c):
        stage(c, s_a, s_b, e_b, e_a)
        stage(c + 1, s_b, s_a, e_a, e_b)

    def quad(c):
        pair(c)
        pair(c + 2)

    def oct_body(t, carry):
        quad(8 * t)
        quad(8 * t + 4)
        return carry

    n_octs = n_tot // 8
    lax.fori_loop(0, n_octs, oct_body, 0)
    done = 8 * n_octs

    @pl.when(n_tot - done >= 4)
    def _():
        quad(done)

    done = done + 4 * ((n_tot - done) // 4)

    @pl.when(n_tot - done >= 2)
    def _():
        pair(done)

    @pl.when(n_tot % 2 == 1)
    def _():
        stage(n_tot - 1, s_a, s_b, e_b, e_a)
        add_values(n_tot - 1, e_a)

    @pl.when((n_tot % 2 == 0) & (n_tot > 0))
    def _():
        add_values(n_tot - 1, e_b)


def _nsa_block(u, refs, neg_ref, acc_ref, l_ref, nb):
    (q_ref, kc_ref, vc_ref, ks_ref, vs_ref, kw_ref, vw_ref, gate_ref, cbias_ref, sbias_ref, wbias_ref,
     ov_ref, msel_ref) = refs
    i = pl.program_id(2) * NSA_SUB + u
    R = NSA_REP
    N = R * Q_BLOCK
    nc = kc_ref.shape[2]
    q_t = jnp.concatenate([q_ref[r, u] for r in range(R)], axis=1)
    n_win = NSA_WINDOW // Q_BLOCK

    def k_rows(ref, J0, n_tiles):
        return ref[0, pl.ds(pl.multiple_of(J0 * Q_BLOCK, Q_BLOCK), n_tiles * Q_BLOCK), :]

    def v_cols(ref, J0, n_tiles):
        return jnp.concatenate([ref[0, J0 + u] for u in range(n_tiles)], axis=1)

    def sel_mask(J0, n_tiles, n_valid=None):
        n_slc = neg_ref.shape[0] - NEG_ROWS
        parts = []
        for u in range(2 * n_tiles):
            row = 2 * J0 + u
            if n_valid is not None:
                row = jnp.where(u // 2 < n_valid, row, n_slc)
            parts.append(jnp.broadcast_to(neg_ref[pl.ds(row, 1), :], (SLC_LEN, Q_BLOCK)))
        return jnp.concatenate([jnp.concatenate(parts, axis=0)] * R, axis=1)

    def compressed(fast):
        cb = cbias_ref[0, pl.ds(pl.multiple_of(8 * (nb - 1) - 8 * i, 8), nc), :]
        s = _dot(kc_ref[0, 0], q_t) + cb
        if fast:
            e = jnp.exp2(s)
        else:
            e = jnp.where(s > 0.5 * NEG_INF, jnp.exp2(s - jnp.max(s, axis=0, keepdims=True)), 0.0)
        l = jnp.sum(e, axis=0, keepdims=True)
        p = e * jnp.where(l > 0.0, 1.0 / l, 0.0)
        return p, l, _dot(vc_ref[0, 0], p.astype(BF16))

    def select_blocks(p_c):
        p_sum = p_c[:, 0:Q_BLOCK]
        for r in range(1, R):
            p_sum = p_sum + p_c[:, r * Q_BLOCK:(r + 1) * Q_BLOCK]
        p_hi = p_sum.astype(BF16)
        p_lo = (p_sum - p_hi.astype(F32)).astype(BF16)
        imp = _dot(ov_ref[...], p_hi) + _dot(ov_ref[...], p_lo)
        n_slc = imp.shape[0]
        jj = lax.broadcasted_iota(jnp.int32, imp.shape, 0)
        qq = lax.broadcasted_iota(jnp.int32, imp.shape, 1)
        cur = 2 * i + jnp.where(qq >= SLC_LEN, 1, 0)
        forced = (jj == 0) | (jj == cur) | (jj == cur - 1)
        score = jnp.where(forced, TAKEN, jnp.where(jj <= cur, imp, -1.0))
        grp = 8
        j_rows = [jj[a:a + grp] for a in range(0, n_slc, grp)]
        for _ in range(min(SLC_TOPN, n_slc) - N_FORCED):
            cand = [(score[a:a + grp], j_rows[a // grp]) for a in range(0, n_slc, grp)]
            while len(cand) > 1:
                nxt = []
                for a in range(0, len(cand) - 1, 2):
                    (s0, j0), (s1, j1) = cand[a], cand[a + 1]
                    right = s1 > s0
                    nxt.append((jnp.maximum(s0, s1), jnp.where(right, j1, j0)))
                cand = nxt + cand[len(cand) - len(cand) % 2:]
            s8, j8 = cand[0]
            best = jnp.max(s8, axis=0, keepdims=True)
            first = jnp.min(jnp.where(s8 == best, j8, n_slc), axis=0, keepdims=True)
            score = jnp.where(jj == first, TAKEN, score)
        neg_ref[:n_slc] = jnp.where(score == TAKEN, msel_ref[0], NEG_INF)
        neg_ref[n_slc:] = jnp.full((NEG_ROWS, Q_BLOCK), NEG_INF, F32)

    def add_chunk(st, s, v_t):
        e = jnp.exp2(s)
        return st[0] + jnp.sum(e, axis=0, keepdims=True), st[1] + _dot(v_t, e.astype(BF16))

    zero = (jnp.zeros((1, N), F32), jnp.zeros((HEAD_DIM, N), F32))

    def selected_near():
        J0 = jnp.maximum(i - 1, 0)
        bias = sbias_ref[0, pl.ds(pl.multiple_of((2 - (i - J0)) * Q_BLOCK, Q_BLOCK), 2 * Q_BLOCK), :]
        return add_chunk(zero, _dot(k_rows(ks_ref, J0, 2), q_t) + bias + sel_mask(J0, 2), v_cols(vs_ref, J0, 2))

    def window_fast():
        J0 = jnp.maximum(i - n_win, 0)
        bias = wbias_ref[0, pl.ds(pl.multiple_of((n_win - (i - J0)) * Q_BLOCK, Q_BLOCK), (n_win + 1) * Q_BLOCK), :]
        return add_chunk(zero, _dot(k_rows(kw_ref, J0, n_win + 1), q_t) + bias, v_cols(vw_ref, J0, n_win + 1))

    def sweep_exact(k_ref, v_ref, bias_ref, far_dist, n_back, with_sel):
        def body(t, st):
            J = i - t
            off = pl.multiple_of((far_dist - jnp.minimum(t, far_dist)) * Q_BLOCK, Q_BLOCK)
            s = _dot(k_rows(k_ref, J, 1), q_t) + bias_ref[0, pl.ds(off, Q_BLOCK), :]
            if with_sel:
                s = s + sel_mask(J, 1)
            return _softmax_tile_update(s, v_ref[0, J], *st)

        init = (jnp.full((1, N), NEG_INF, F32),) + zero
        _, l, acc = lax.fori_loop(0, jnp.minimum(i, n_back) + 1, body, init)
        return l, acc

    vals = {}

    def load_phase():
        vals["cmp"] = compressed(True)
        vals["win"] = window_fast()

    def select_phase():
        select_blocks(vals["cmp"][0])

    def near_phase():
        l_near, acc_near = selected_near()
        l_ref[u] = jnp.broadcast_to(l_near, l_ref.shape[1:])
        acc_ref[u] = acc_near

    def exact_path():
        p_x, _, o_x = compressed(False)
        select_blocks(p_x)
        l_sx, a_sx = sweep_exact(ks_ref, vs_ref, sbias_ref, 2, nb, True)
        l_wx, a_wx = sweep_exact(kw_ref, vw_ref, wbias_ref, n_win, n_win, False)
        return o_x, a_sx * (1.0 / l_sx), a_wx * (1.0 / l_wx)

    def gate_row(br):
        return jnp.concatenate([gate_ref[br, 0, r:r + 1, u * Q_BLOCK:(u + 1) * Q_BLOCK] for r in range(R)], axis=1)

    def fast_result():
        _, l_c, o_c = vals["cmp"]
        l_w, a_w = vals["win"]
        l_s, a_s = l_ref[u, 0:1, :], acc_ref[u]
        lane_q = lax.broadcasted_iota(jnp.int32, (1, N), 1) % Q_BLOCK
        has_cmp = (i > 0) | (lane_q >= CMP_LEN - 1)
        l_min = jnp.minimum(jnp.minimum(l_s, l_w), jnp.where(has_cmp, l_c, 1.0))
        return l_min, (o_c, a_s * (1.0 / l_s), a_w * (1.0 / l_w))

    def combine(outs):
        o_c, o_s, o_w = outs
        o_t = gate_row(0) * o_c + gate_row(1) * o_s + gate_row(2) * o_w
        return _pairs_to_rows(o_t)

    return load_phase, select_phase, near_phase, fast_result, exact_path, combine


def _nsa_attention(colmat, rowmat, cmp_rows, cmp_cols, gates, cbias, sbias, wbias, ov_t, msel, B, S):
    nb = S // Q_BLOCK
    nc = S // CMP_STRIDE
    n_slc = S // SLC_LEN
    G, R = NSA_KV_HEADS, NSA_REP
    kern = functools.partial(_nsa_kernel, nb=nb)
    ns = nb // NSA_SUB
    return pl.pallas_call(
        kern,
        grid=(B, G, ns),
        in_specs=[
            pl.BlockSpec((R, NSA_SUB, HEAD_DIM, LANES), lambda b, g, i: (g, b * ns + i, 0, 0)),
            pl.BlockSpec((1, 1, nc, HEAD_DIM), lambda b, g, i: (g, b, 0, 0)),
            pl.BlockSpec((1, 1, HEAD_DIM, nc), lambda b, g, i: (NSA_KV_HEADS + g, b, 0, 0)),
            pl.BlockSpec((1, S, HEAD_DIM), lambda b, g, i: (ROW_KS + g, b, 0)),
            pl.BlockSpec((1, nb, HEAD_DIM, LANES), lambda b, g, i: (COL_VS + g, b, 0, 0)),
            pl.BlockSpec((1, S, HEAD_DIM), lambda b, g, i: (ROW_KW + g, b, 0)),
            pl.BlockSpec((1, nb, HEAD_DIM, LANES), lambda b, g, i: (COL_VW + g, b, 0, 0)),
            pl.BlockSpec((N_BRANCHES, 1, R, NSA_SUB * LANES), lambda b, g, i: (0, g, 0, b * ns + i)),
            pl.BlockSpec((1, cbias.shape[1], R * Q_BLOCK), lambda b, g, i: (g, 0, 0)),
            pl.BlockSpec((1, sbias.shape[1], R * Q_BLOCK), lambda b, g, i: (g, 0, 0)),
            pl.BlockSpec((1, wbias.shape[1], R * Q_BLOCK), lambda b, g, i: (g, 0, 0)),
            pl.BlockSpec((n_slc, nc), lambda b, g, i: (0, 0)),
            pl.BlockSpec((1, 1, Q_BLOCK), lambda b, g, i: (g, 0, 0)),
        ],
        out_specs=pl.BlockSpec((1, NSA_SUB * Q_BLOCK, R * HEAD_DIM), lambda b, g, i: (b, i, g)),
        out_shape=jax.ShapeDtypeStruct((B, S, NSA_HEADS * HEAD_DIM), BF16),
        scratch_shapes=[pltpu.VMEM((NSA_SUB, n_slc + NEG_ROWS, Q_BLOCK), F32)]
        + [pltpu.VMEM((FAR_TILES * Q_BLOCK, R * Q_BLOCK), F32)] * 2
        + [pltpu.VMEM((FAR_TILES * Q_BLOCK, R * Q_BLOCK), BF16)] * 2
        + [pltpu.VMEM((NSA_SUB, HEAD_DIM, R * Q_BLOCK), F32),
           pltpu.VMEM((NSA_SUB, 8, R * Q_BLOCK), F32)],
        compiler_params=_cparams(("parallel", "parallel", "arbitrary")),
        name="nsa_attention",
    )(colmat, cmp_rows, cmp_cols, rowmat, colmat, rowmat, colmat, gates, cbias, sbias, wbias, ov_t, msel)


SB_QB = 512
SB_KT = 256
SB_ZERO_LOG = -160.0


def _sb_kernel(q_ref, k_ref, v_ref, tri_ref, o_ref):
    i = pl.program_id(1)
    H = SB_HEADS
    tri = tri_ref[...]
    assert SB_QB == 2 * SB_KT
    below_diag = (lax.broadcasted_iota(jnp.int32, (SB_KT, SB_KT), 1)
                  < lax.broadcasted_iota(jnp.int32, (SB_KT, SB_KT), 0))

    def step(J, rows, st, masked):
        out = []
        for h in range(H):
            carry, acc = st[h]
            k_t = jnp.concatenate([k_ref[h, (SB_KT // LANES) * J + c] for c in range(SB_KT // LANES)], axis=1)
            z = _dot(q_ref[h, rows, :], k_t)
            sp = jnp.maximum(z, 0.0) + jnp.log2(1.0 + jnp.exp2(-jnp.abs(z)))
            lf = jnp.where(below_diag, -sp, 0.0) if masked else -sp
            lf_hi = lf.astype(BF16)
            lf_lo = (lf - lf_hi.astype(F32)).astype(BF16)
            after = _dot(lf_hi, tri) + _dot(lf_lo, tri) + carry
            w = jnp.exp2(z - sp + after)
            if masked:
                w = jnp.where(below_diag, w, 0.0)
            v = v_ref[h, pl.ds(pl.multiple_of(J * SB_KT, SB_KT), SB_KT), :]
            acc = acc + _dot(w.astype(BF16), v)
            carry = carry + jnp.sum(lf, axis=1, keepdims=True)
            out.append((carry, acc))
        return tuple(out)

    lo, hi = slice(0, SB_KT), slice(SB_KT, SB_QB)
    zero = tuple((jnp.zeros((SB_KT, 1), F32), jnp.zeros((SB_KT, HEAD_DIM), F32)) for _ in range(H))
    st_hi = step(2 * i, hi, step(2 * i + 1, hi, zero, True), False)
    st_lo = step(2 * i, lo, zero, True)

    def alive(st):
        worst = st[0][0]
        for h in range(1, H):
            worst = jnp.maximum(worst, st[h][0])
        return jnp.max(worst) > SB_ZERO_LOG

    def sweep(rows, st):
        def body(c):
            J, _, st = c
            st = step(J, rows, st, False)
            return J - 1, alive(st), st

        return lax.while_loop(lambda c: (c[0] >= 0) & c[1], body, (2 * i - 1, alive(st), st))[2]

    st_lo, st_hi = sweep(lo, st_lo), sweep(hi, st_hi)
    o_ref[0, lo, :] = jnp.concatenate([st_lo[h][1] for h in range(H)], axis=1).astype(o_ref.dtype)
    o_ref[0, hi, :] = jnp.concatenate([st_hi[h][1] for h in range(H)], axis=1).astype(o_ref.dtype)


def _sb_attention(colmat, rowmat, tri, B, S):
    nq = S // SB_QB
    nb = S // LANES
    H = SB_HEADS
    return pl.pallas_call(
        _sb_kernel,
        grid=(B, nq),
        in_specs=[
            pl.BlockSpec((H, SB_QB, HEAD_DIM), lambda b, i: (ROW_SQ // H, b * nq + i, 0)),
            pl.BlockSpec((H, nb, HEAD_DIM, LANES), lambda b, i: (COL_SK // H, b, 0, 0)),
            pl.BlockSpec((H, S, HEAD_DIM), lambda b, i: (ROW_SV // H, b, 0)),
            pl.BlockSpec((SB_KT, SB_KT), lambda b, i: (0, 0)),
        ],
        out_specs=pl.BlockSpec((1, SB_QB, H * HEAD_DIM), lambda b, i: (b, i, 0)),
        out_shape=jax.ShapeDtypeStruct((B, S, H * HEAD_DIM), BF16),
        compiler_params=_cparams(("parallel", "arbitrary")),
        name="sb_attention",
    )(rowmat, colmat, rowmat, tri)


SWA_SUB = 8


def _swa_kernel(q_ref, k_ref, v_ref, bias_ref, sink_ref, o_ref):
    R = SWA_REP
    sink = sink_ref[0]
    outs = []
    for u in range(SWA_SUB):
        i = pl.program_id(2) * SWA_SUB + u
        q_t = jnp.concatenate([q_ref[r, u] for r in range(R)], axis=1)
        J0 = jnp.maximum(i - 1, 0)
        k = k_ref[0, pl.ds(pl.multiple_of(J0 * Q_BLOCK, Q_BLOCK), 2 * Q_BLOCK), :]
        bias = bias_ref[0, pl.ds(pl.multiple_of((1 - (i - J0)) * Q_BLOCK, Q_BLOCK), 2 * Q_BLOCK), :]
        s = _dot(k, q_t) + bias
        m = jnp.maximum(jnp.max(s, axis=0, keepdims=True), sink)
        e = jnp.exp(s - m)
        den = jnp.sum(e, axis=0, keepdims=True) + jnp.exp(sink - m)
        v_t = jnp.concatenate([v_ref[0, J0], v_ref[0, J0 + 1]], axis=1)
        o_t = _dot(v_t, e.astype(BF16)) * (1.0 / den)
        outs.append(_pairs_to_rows(o_t).astype(o_ref.dtype))
    for u in range(SWA_SUB):
        o_ref[0, u * Q_BLOCK:(u + 1) * Q_BLOCK, :] = outs[u]


def _swa_attention(colmat, rowmat, bias, sinks, B, S):
    nb = S // Q_BLOCK
    ns = nb // SWA_SUB
    G, R = SWA_KV_HEADS, SWA_REP
    return pl.pallas_call(
        _swa_kernel,
        grid=(B, G, ns),
        in_specs=[
            pl.BlockSpec((R, SWA_SUB, HEAD_DIM, LANES), lambda b, g, i: (COL_WQ // R + g, b * ns + i, 0, 0)),
            pl.BlockSpec((1, S, HEAD_DIM), lambda b, g, i: (ROW_WK + g, b, 0)),
            pl.BlockSpec((1, nb, HEAD_DIM, LANES), lambda b, g, i: (COL_WV + g, b, 0, 0)),
            pl.BlockSpec((1, 3 * Q_BLOCK, R * Q_BLOCK), lambda b, g, i: (g, 0, 0)),
            pl.BlockSpec((1, 1, R * Q_BLOCK), lambda b, g, i: (g, 0, 0)),
        ],
        out_specs=pl.BlockSpec((1, SWA_SUB * Q_BLOCK, R * HEAD_DIM), lambda b, g, i: (b, i, g)),
        out_shape=jax.ShapeDtypeStruct((B, S, SWA_HEADS * HEAD_DIM), BF16),
        compiler_params=_cparams(("parallel", "parallel", "arbitrary")),
        name="swa_attention",
    )(colmat, rowmat, colmat, bias, sinks)


MERGE_TM = 1024


def _merge_kernel(x_ref, nw_ref, wg_ref, ya_ref, yb_ref, yc_ref, ua_ref, ub_ref, uc_ref, wo_ref, o_ref):
    x = x_ref[...]
    h = _rms_rows(x, nw_ref[...]).astype(BF16)
    merged = None
    for br, (y_ref, u_ref) in enumerate(((ya_ref, ua_ref), (yb_ref, ub_ref), (yc_ref, uc_ref))):
        g = jax.nn.sigmoid(_dot(h, wg_ref[:, br * D_MODEL:(br + 1) * D_MODEL]))
        t = g * _dot(y_ref[...], u_ref[...])
        merged = t if merged is None else merged + t
    o_ref[...] = x + _dot(merged.astype(BF16), wo_ref[...])


def _merge(x2, nw, w_bgate, y_nsa, y_sb, y_swa, u_nsa, u_sb, u_swa, w_out, l):
    T = x2.shape[0]
    full = lambda w: _layer_spec(l, w.shape[1:], pipeline_mode=pl.Buffered(1))
    rows = lambda n: pl.BlockSpec((MERGE_TM, n), lambda i: (i, 0))
    return pl.pallas_call(
        _merge_kernel,
        grid=(T // MERGE_TM,),
        in_specs=[rows(D_MODEL), pl.BlockSpec((1, D_MODEL), lambda i: (0, 0)), full(w_bgate),
                  rows(y_nsa.shape[1]), rows(y_sb.shape[1]), rows(y_swa.shape[1]),
                  full(u_nsa), full(u_sb), full(u_swa), full(w_out)],
        out_specs=rows(D_MODEL),
        out_shape=jax.ShapeDtypeStruct((T, D_MODEL), F32),
        compiler_params=_cparams(("parallel",)),
        name="merge",
    )(x2, nw.reshape(1, D_MODEL), w_bgate, y_nsa, y_sb, y_swa, u_nsa, u_sb, u_swa, w_out)


def _t5_bucket(dist):
    max_exact = NUM_BUCKETS // 2
    d = jnp.maximum(dist, 0)
    df = jnp.maximum(d, 1).astype(F32)
    large = max_exact + (jnp.log(df / max_exact) / math.log(MAX_DISTANCE / max_exact)
                         * (NUM_BUCKETS - max_exact)).astype(jnp.int32)
    large = jnp.minimum(large, NUM_BUCKETS - 1)
    return jnp.where(d < max_exact, d, large)


def _toeplitz(ext, delta):
    H = ext.shape[0]
    n = Q_BLOCK
    c = delta + MAX_DISTANCE
    w = ext[:, c - (n - 1):c + n]
    flat = jnp.pad(jnp.broadcast_to(w[:, None, :], (H, n, 2 * n - 1)), ((0, 0), (0, 0), (0, 1))).reshape(H, 2 * n * n)
    return flat[:, n - 1:n - 1 + n * (2 * n - 1)].reshape(H, n, 2 * n - 1)[:, :, :n]


def _lanes(t, G, R):
    K, Q = t.shape[1:]
    return t.reshape(G, R, K, Q).transpose(0, 2, 1, 3).reshape(G, K, R * Q)


def _bias_tables(rel_bias, nb, qk_bound):
    n = Q_BLOCK
    G, R = NSA_KV_HEADS, NSA_REP
    by_dist = rel_bias[_t5_bucket(jnp.arange(MAX_DISTANCE)), :].T.astype(F32)
    far = rel_bias[NUM_BUCKETS - 1, :].astype(F32)
    neg = jnp.full((1, n), NEG_INF, F32)
    k = jnp.arange(n)[:, None]
    q = jnp.arange(n)[None, :]
    above = (k > q)[None]

    rel = (by_dist[:NSA_HEADS] - far[:NSA_HEADS, None]) * LOG2E
    m_g = qk_bound + jnp.maximum(jnp.max(rel.reshape(G, -1), axis=1), 0.0)
    m_h = jnp.repeat(m_g, R)[:, None, None]
    ext = jnp.concatenate([jnp.broadcast_to(neg, (NSA_HEADS, n)), rel, jnp.zeros((NSA_HEADS, n), F32)], axis=1)
    t0, t1 = _toeplitz(ext, 0), _toeplitz(ext, n)
    zeros = jnp.zeros_like(t0)
    masked = jnp.full_like(t0, NEG_INF)
    sbias = _lanes(jnp.concatenate([zeros, t1, t0, masked], axis=1), G, R)
    n_win = NSA_WINDOW // n
    edge = jnp.where(above, zeros, NEG_INF)
    tiles = [edge] + [zeros] * (n_win - 2) + [t1, t0]
    wbias = _lanes(jnp.concatenate([t - m_h for t in tiles] + [masked] * n_win, axis=1), G, R)
    off = 8 * (nb - 1)
    near = jnp.stack([ext[:, 97 - CMP_STRIDE * m:97 - CMP_STRIDE * m + n] for m in range(-9, 7)], axis=1)
    n_rows = 16 * nb - 8
    cb = jnp.concatenate([jnp.zeros((NSA_HEADS, off - 9, n), F32), near,
                          jnp.full((NSA_HEADS, n_rows - off - 7, n), NEG_INF, F32)], axis=1) - m_h
    cbias = _lanes(cb, G, R)
    msel = jnp.broadcast_to(-m_g[:, None, None], (G, 1, n))

    hs = slice(NSA_HEADS, NSA_HEADS + SWA_HEADS)
    ext_w = jnp.concatenate([jnp.broadcast_to(neg, (SWA_HEADS, n)), by_dist[hs],
                             jnp.broadcast_to(far[hs, None], (SWA_HEADS, n))], axis=1)
    s0 = _toeplitz(ext_w, 0)
    s1 = jnp.where(above, _toeplitz(ext_w, n), NEG_INF)
    swa_bias = _lanes(jnp.concatenate([s1, s0, jnp.full_like(s0, NEG_INF)], axis=1), SWA_KV_HEADS, SWA_REP)
    return cbias, sbias, wbias, msel, swa_bias


def _overlap_t(S):
    nc, n_slc = S // CMP_STRIDE, S // SLC_LEN
    n = np.arange(nc)[None, :]
    j = np.arange(n_slc)[:, None]
    ov = (n * CMP_STRIDE < (j + 1) * SLC_LEN) & (n * CMP_STRIDE + CMP_LEN - 1 >= j * SLC_LEN) & (n < nc - 1)
    return jnp.asarray(ov, BF16)


def _head_gains(heads, norm_of, scale_of):
    ones = jnp.ones((HEAD_DIM,), F32)
    gains = [norm_of[name].astype(F32) * scale_of.get(name, 1.0) if name in norm_of else ones for name, _ in heads]
    return jnp.concatenate(gains).reshape(1, -1)


def kernel(x, rel_bias, ffn1_norm, ffn1_w_gate, ffn1_w_up, ffn1_w_down, mix_norm, w_in, nsa_q_norm, nsa_k_norm, nsa_cmp_pos, nsa_cmp_k_w1, nsa_cmp_k_w2, nsa_cmp_v_w1, nsa_cmp_v_w2, swa_q_norm, swa_k_norm, swa_sinks, w_up_nsa, w_up_sb, w_up_swa, w_out, ffn2_norm, ffn2_w_gate, ffn2_w_up, ffn2_w_down):
    B, S, D = x.shape
    T = B * S
    nb = S // Q_BLOCK
    nc = S // CMP_STRIDE
    depth = w_in.shape[0]
    scale = HEAD_DIM ** -0.5
    scale_of = {"nq": scale * LOG2E, "sq": scale * LOG2E, "wq": scale}
    qk_bound = (BOUND_SLACK * LOG2E * HEAD_DIM ** 0.5
                * jnp.max(jnp.abs(nsa_q_norm.astype(F32))) * jnp.max(jnp.abs(nsa_k_norm.astype(F32))))
    cbias, sbias, wbias, msel, swa_bias = _bias_tables(rel_bias, nb, qk_bound)
    plan = _proj_plan(scale_of)
    ov_t = _overlap_t(S)
    tri = jnp.asarray(np.tril(np.ones((SB_KT, SB_KT), np.float32), -1), BF16)

    (ffn1_w_gate, ffn1_w_up, ffn1_w_down, ffn2_w_gate, ffn2_w_up, ffn2_w_down, w_up_nsa, w_up_sb, w_up_swa,
     w_out) = (w.astype(BF16) for w in (ffn1_w_gate, ffn1_w_up, ffn1_w_down, ffn2_w_gate, ffn2_w_up, ffn2_w_down,
                                        w_up_nsa, w_up_sb, w_up_swa, w_out))
    tail0 = OFF_NGATE + N_GATE
    w_head, w_tail = w_in[:, :, :OFF_NGATE].astype(BF16), w_in[:, :, tail0:].astype(BF16)
    w_bgate = w_tail[:, :, OFF_BGATE - tail0:]
    w_gate = jnp.pad(w_in[:, :, OFF_NGATE:tail0], ((0, 0), (0, 0), (0, LANES - N_GATE))).astype(BF16)
    runs = []
    for _, off in ROW_HEADS + COL_HEADS:
        if runs and runs[-1][1] == off:
            runs[-1][1] = off + HEAD_DIM
        else:
            runs.append([off, off + HEAD_DIM])
    pieces = [w_head[:, :, a:b] if a < OFF_NGATE else w_tail[:, :, a - tail0:b - tail0] for a, b in runs]
    w_proj = jnp.concatenate(pieces + [w_gate], axis=2)

    x2 = x.reshape(T, D)
    for l in range(depth):
        x2 = _ffn(x2, ffn1_norm[l], ffn1_w_gate, ffn1_w_up, ffn1_w_down, l)

        norm_of = {"nq": nsa_q_norm[l], "nks": nsa_k_norm[l], "nkw": nsa_k_norm[l],
                   "wq": swa_q_norm[l], "wk": swa_k_norm[l]}
        gain = jnp.concatenate([_head_gains(ROW_HEADS, norm_of, scale_of), _head_gains(COL_HEADS, norm_of, scale_of),
                                jnp.ones((1, LANES), F32)], axis=1)
        rowmat, colmat, gates = _proj(x2, mix_norm[l], w_proj, gain, plan, l)
        gates = gates[:N_GATE].reshape(N_BRANCHES, NSA_KV_HEADS, NSA_REP, T)

        rows4 = rowmat[ROW_KC:ROW_KC + 2 * NSA_KV_HEADS].reshape(2 * NSA_KV_HEADS, B, nc, CMP_STRIDE * HEAD_DIM)
        cmp_rows, cmp_cols = _compress(rows4, jnp.stack([nsa_cmp_k_w1[l], nsa_cmp_v_w1[l]]),
                                       jnp.stack([nsa_cmp_k_w2[l], nsa_cmp_v_w2[l]]),
                                       nsa_cmp_pos[l], nsa_k_norm[l])

        y_nsa = _nsa_attention(colmat, rowmat, cmp_rows, cmp_cols, gates, cbias, sbias, wbias, ov_t, msel, B, S)
        y_sb = _sb_attention(colmat, rowmat, tri, B, S)
        sinks = jnp.repeat(swa_sinks[l].astype(F32).reshape(SWA_KV_HEADS, 1, SWA_REP), Q_BLOCK, axis=2)
        y_swa = _swa_attention(colmat, rowmat, swa_bias, sinks, B, S)

        x2 = _merge(x2, mix_norm[l], w_bgate, y_nsa.reshape(T, -1), y_sb.reshape(T, -1),
                    y_swa.reshape(T, -1), w_up_nsa, w_up_sb, w_up_swa, w_out, l)
        x2 = _ffn(x2, ffn2_norm[l], ffn2_w_gate, ffn2_w_up, ffn2_w_down, l)
    return x2.reshape(B, S, D)
```

```python
import functools
import math

import numpy as np
import jax
import jax.numpy as jnp
from jax import lax
from jax.experimental import pallas as pl
from jax.experimental.pallas import tpu as pltpu

D_MODEL = 1024
HEAD_DIM = 64
Q_BLOCK = 128
NSA_HEADS = 8
NSA_KV_HEADS = 2
NSA_REP = NSA_HEADS // NSA_KV_HEADS
CMP_LEN = 32
CMP_STRIDE = 16
CMP_HIDDEN = 256
SLC_LEN = 64
SLC_TOPN = 16
NSA_WINDOW = 512
SB_HEADS = 4
SWA_HEADS = 4
SWA_KV_HEADS = 2
SWA_REP = SWA_HEADS // SWA_KV_HEADS
SWA_WINDOW = 128
NUM_BUCKETS = 32
MAX_DISTANCE = 128
D_FF = 2816
NORM_EPS = 1e-6
NEG_INF = -1e30
N_BRANCHES = 3
N_GATE = 3 * NSA_HEADS

LANES = 128
VMEM_LIMIT = 56 * 1024 * 1024
BF16 = jnp.bfloat16
F32 = jnp.float32

_sizes = (NSA_HEADS * HEAD_DIM,) + (NSA_KV_HEADS * HEAD_DIM,) * 6 + (N_GATE,) \
    + (SB_HEADS * HEAD_DIM,) * 3 + (SWA_HEADS * HEAD_DIM, SWA_KV_HEADS * HEAD_DIM,
                                    SWA_KV_HEADS * HEAD_DIM, N_BRANCHES * D_MODEL)
_offs = np.concatenate([[0], np.cumsum(_sizes)])
(OFF_NQ, OFF_NKC, OFF_NVC, OFF_NKS, OFF_NVS, OFF_NKW, OFF_NVW, OFF_NGATE,
 OFF_SQ, OFF_SK, OFF_SV, OFF_WQ, OFF_WK, OFF_WV, OFF_BGATE) = (int(o) for o in _offs[:-1])

ROW_HEADS = ([("nkc", OFF_NKC + 64 * g) for g in range(2)] + [("nvc", OFF_NVC + 64 * g) for g in range(2)]
             + [("nks", OFF_NKS + 64 * g) for g in range(2)] + [("nkw", OFF_NKW + 64 * g) for g in range(2)]
             + [("sq", OFF_SQ + 64 * h) for h in range(4)] + [("sv", OFF_SV + 64 * h) for h in range(4)]
             + [("wk", OFF_WK + 64 * g) for g in range(2)])
ROW_KC, ROW_VC, ROW_KS, ROW_KW, ROW_SQ, ROW_SV, ROW_WK = 0, 2, 4, 6, 8, 12, 16
COL_HEADS = ([("nq", OFF_NQ + 64 * h) for h in range(8)] + [("nvs", OFF_NVS + 64 * g) for g in range(2)]
             + [("nvw", OFF_NVW + 64 * g) for g in range(2)] + [("sk", OFF_SK + 64 * h) for h in range(4)]
             + [("wq", OFF_WQ + 64 * h) for h in range(4)] + [("wv", OFF_WV + 64 * g) for g in range(2)])
COL_NQ, COL_VS, COL_VW, COL_SK, COL_WQ, COL_WV = 0, 8, 10, 12, 16, 20


def _dot(a, b):
    return jnp.dot(a, b, preferred_element_type=F32)


def _rms_rows(x, w):
    ms = jnp.mean(x * x, axis=-1, keepdims=True)
    return x * lax.rsqrt(ms + NORM_EPS) * w


def _cparams(sem):
    return pltpu.CompilerParams(dimension_semantics=sem, vmem_limit_bytes=VMEM_LIMIT)


FFN_TM = 1024
MXU_TILE = 256
FFN_FC = MXU_TILE


def _ffn_kernel(x_ref, nw_ref, wg_ref, wu_ref, wd_ref, o_ref):
    x = x_ref[...]
    h = _rms_rows(x, nw_ref[...]).astype(BF16)
    acc = jnp.zeros(x.shape, F32)
    for c in range(D_FF // FFN_FC):
        cols = slice(c * FFN_FC, (c + 1) * FFN_FC)
        g = _dot(h, wg_ref[:, cols])
        u = _dot(h, wu_ref[:, cols])
        a = (g * jax.nn.sigmoid(g)) * u
        acc = acc + _dot(a.astype(BF16), wd_ref[cols, :])
    o_ref[...] = x + 0.5 * acc


def _layer_spec(l, shape, **kw):
    return pl.BlockSpec((None,) + tuple(shape), lambda i: (l, 0, 0), **kw)


def _ffn(x2, nw, wg, wu, wd, l):
    T = x2.shape[0]
    resident = lambda shape: _layer_spec(l, shape, pipeline_mode=pl.Buffered(1))
    return pl.pallas_call(
        _ffn_kernel,
        grid=(T // FFN_TM,),
        in_specs=[
            pl.BlockSpec((FFN_TM, D_MODEL), lambda i: (i, 0)),
            pl.BlockSpec((1, D_MODEL), lambda i: (0, 0)),
            resident((D_MODEL, D_FF)), resident((D_MODEL, D_FF)), resident((D_FF, D_MODEL)),
        ],
        out_specs=pl.BlockSpec((FFN_TM, D_MODEL), lambda i: (i, 0)),
        out_shape=jax.ShapeDtypeStruct((T, D_MODEL), F32),
        compiler_params=_cparams(("parallel",)),
        name="ffn",
    )(x2, nw.reshape(1, D_MODEL), wg, wu, wd)


PROJ_TM = 512


GATE_ROWS = 32

_NORMED = ("nq", "nks", "nkw", "wq", "wk")


def _proj_plan(scale_of):
    plan = []
    for kind, heads in (("row", ROW_HEADS), ("col", COL_HEADS)):
        for p in range(len(heads) // 2):
            name = heads[2 * p][0]
            assert heads[2 * p + 1][0] == name
            plan.append((kind, p, name in _NORMED, 1.0 if name in _NORMED else scale_of.get(name, 1.0)))
    plan.append(("gate", 0, False, 1.0))
    return tuple(plan)


def _proj_kernel(x_ref, nw_ref, w_ref, gain_ref, bd_ref, row_ref, col_ref, gate_ref, *, plan):
    h = _rms_rows(x_ref[...], nw_ref[...]).astype(BF16)
    pair = 2 * MXU_TILE
    for c0 in range(0, len(plan) * LANES, pair):
        width = min(pair, len(plan) * LANES - c0)
        zg = _dot(h, w_ref[:, c0:c0 + width])
        for u in range(width // LANES):
            t = c0 // LANES + u
            kind, p, normed, scale = plan[t]
            z = zg[:, u * LANES:(u + 1) * LANES]
            if normed:
                ms = _dot((z * z).astype(BF16), bd_ref[...])
                z = z * lax.rsqrt(ms + NORM_EPS) * gain_ref[:, t * LANES:(t + 1) * LANES]
            elif scale != 1.0:
                z = z * scale
            if kind == "row":
                row_ref[2 * p] = z[:, :HEAD_DIM].astype(row_ref.dtype)
                row_ref[2 * p + 1] = z[:, HEAD_DIM:].astype(row_ref.dtype)
            elif kind == "col":
                z_t = z.T
                for c in range(PROJ_TM // LANES):
                    col_ref[2 * p, c] = z_t[:HEAD_DIM, c * LANES:(c + 1) * LANES].astype(col_ref.dtype)
                    col_ref[2 * p + 1, c] = z_t[HEAD_DIM:, c * LANES:(c + 1) * LANES].astype(col_ref.dtype)
            else:
                gate_ref[...] = jax.nn.sigmoid(z).T[:GATE_ROWS]


def _proj(x2, nw, w, gain, plan, l):
    T = x2.shape[0]
    n_row, n_col = len(ROW_HEADS), len(COL_HEADS)
    bd = np.kron(np.eye(2, dtype=np.float32), np.full((HEAD_DIM, HEAD_DIM), 1.0 / HEAD_DIM, np.float32))
    return pl.pallas_call(
        functools.partial(_proj_kernel, plan=plan),
        grid=(T // PROJ_TM,),
        in_specs=[
            pl.BlockSpec((PROJ_TM, D_MODEL), lambda i: (i, 0)),
            pl.BlockSpec((1, D_MODEL), lambda i: (0, 0)),
            _layer_spec(l, w.shape[1:]),
            pl.BlockSpec(gain.shape, lambda i: (0, 0)),
            pl.BlockSpec((LANES, LANES), lambda i: (0, 0)),
        ],
        out_specs=[
            pl.BlockSpec((n_row, PROJ_TM, HEAD_DIM), lambda i: (0, i, 0)),
            pl.BlockSpec((n_col, PROJ_TM // LANES, HEAD_DIM, LANES), lambda i: (0, i, 0, 0)),
            pl.BlockSpec((GATE_ROWS, PROJ_TM), lambda i: (0, i)),
        ],
        out_shape=[jax.ShapeDtypeStruct((n_row, T, HEAD_DIM), BF16),
                   jax.ShapeDtypeStruct((n_col, T // LANES, HEAD_DIM, LANES), BF16),
                   jax.ShapeDtypeStruct((GATE_ROWS, T), F32)],
        compiler_params=_cparams(("parallel",)),
        name="proj",
    )(x2, nw.reshape(1, D_MODEL), w, gain, jnp.asarray(bd, BF16))


def _compress_kernel(r_ref, w1_ref, w2_ref, pos_ref, kn_ref, row_ref, col_ref):
    kind = pl.program_id(0) // NSA_KV_HEADS
    r = r_ref[0, 0]
    nc = r.shape[0]
    half = CMP_STRIDE * HEAD_DIM
    w1 = w1_ref[0]
    p_lo = _dot(r, w1[:half])
    p_hi = _dot(r, w1[half:])
    p_pos = _dot(jnp.broadcast_to(pos_ref[...], (8, CMP_LEN * HEAD_DIM)).astype(BF16), w1)[:1]
    hdn = p_lo + pltpu.roll(p_hi, nc - 1, 0) + p_pos
    hdn = hdn * jax.nn.sigmoid(hdn)
    out = _dot(hdn.astype(BF16), w2_ref[0])
    ms = jnp.sum(out * out, axis=-1, keepdims=True) * (1.0 / HEAD_DIM)
    normed = out * lax.rsqrt(ms + NORM_EPS) * kn_ref[...]
    out = jnp.where(kind == 0, normed, out)
    row_ref[0, 0] = out[:, :HEAD_DIM].astype(row_ref.dtype)
    col_ref[0, 0] = out.T[:HEAD_DIM].astype(col_ref.dtype)


def _compress(rows4, w1, w2, pos, k_norm):
    assert ROW_KC == 0 and ROW_VC == NSA_KV_HEADS
    _, B, NC, _ = rows4.shape
    kn = jnp.pad(k_norm.reshape(1, HEAD_DIM), ((0, 0), (0, LANES - HEAD_DIM)))
    w2p = jnp.pad(w2, ((0, 0), (0, 0), (0, LANES - HEAD_DIM))).astype(BF16)
    return pl.pallas_call(
        _compress_kernel,
        grid=(4, B),
        in_specs=[
            pl.BlockSpec((1, 1, NC, CMP_STRIDE * HEAD_DIM), lambda h, b: (h, b, 0, 0)),
            pl.BlockSpec((1, CMP_LEN * HEAD_DIM, CMP_HIDDEN), lambda h, b: (h // NSA_KV_HEADS, 0, 0)),
            pl.BlockSpec((1, CMP_HIDDEN, LANES), lambda h, b: (h // NSA_KV_HEADS, 0, 0)),
            pl.BlockSpec((1, CMP_LEN * HEAD_DIM), lambda h, b: (0, 0)),
            pl.BlockSpec((1, LANES), lambda h, b: (0, 0)),
        ],
        out_specs=[
            pl.BlockSpec((1, 1, NC, HEAD_DIM), lambda h, b: (h, b, 0, 0)),
            pl.BlockSpec((1, 1, HEAD_DIM, NC), lambda h, b: (h, b, 0, 0)),
        ],
        out_shape=[jax.ShapeDtypeStruct((4, B, NC, HEAD_DIM), BF16),
                   jax.ShapeDtypeStruct((4, B, HEAD_DIM, NC), BF16)],
        compiler_params=_cparams(("parallel", "parallel")),
        name="nsa_compress",
    )(rows4, w1.astype(BF16), w2p, pos.reshape(1, CMP_LEN * HEAD_DIM), kn)


def _softmax_tile_update(s, v_t, m, l, acc):
    m_new = jnp.maximum(m, jnp.max(s, axis=0, keepdims=True))
    alpha = jnp.exp2(m - m_new)
    p = jnp.exp2(s - m_new)
    l_new = alpha * l + jnp.sum(p, axis=0, keepdims=True)
    acc_new = alpha * acc + _dot(v_t, p.astype(BF16))
    return m_new, l_new, acc_new


def _pairs_to_rows(o_t):
    n = o_t.shape[1] // LANES
    outs = []
    for p in range(n // 2):
        a = o_t[:, (2 * p) * LANES:(2 * p + 1) * LANES]
        b = o_t[:, (2 * p + 1) * LANES:(2 * p + 2) * LANES]
        outs.append(jnp.concatenate([a, b], axis=0).T)
    return outs[0] if len(outs) == 1 else jnp.concatenate(outs, axis=1)


FAR_TILES = 4
NSA_SUB = 8
TAKEN = -2.0
NEG_ROWS = 8
N_FORCED = 3
LOG2E = math.log2(math.e)
NSA_TINY = 2.0 ** -100
BOUND_SLACK = 1.05


def _nsa_kernel(q_ref, kc_ref, vc_ref, ks_ref, vs_ref, kw_ref, vw_ref, gate_ref,
                cbias_ref, sbias_ref, wbias_ref, ov_ref, msel_ref, o_ref,
                neg_all, s_a, s_b, e_a, e_b, acc_ref, l_ref, *, nb):
    refs = (q_ref, kc_ref, vc_ref, ks_ref, vs_ref, kw_ref, vw_ref, gate_ref, cbias_ref, sbias_ref, wbias_ref,
            ov_ref, msel_ref)
    blocks = [_nsa_block(u, refs, neg_all.at[u], acc_ref, l_ref, nb) for u in range(NSA_SUB)]
    for phase in range(3):
        for blk in blocks:
            blk[phase]()
    _far_sweep(q_ref, ks_ref, vs_ref, neg_all, (s_a, s_b, e_a, e_b), acc_ref, l_ref, nb)
    fast = [blk[3]() for blk in blocks]
    l_min = fast[0][0]
    for l_u, _ in fast[1:]:
        l_min = jnp.minimum(l_min, l_u)
    outs = lax.cond(jnp.min(l_min) > NSA_TINY, lambda: tuple(o for _, o in fast),
                    lambda: tuple(blk[4]() for blk in blocks))
    for u, blk in enumerate(blocks):
        o_ref[0, u * Q_BLOCK:(u + 1) * Q_BLOCK, :] = blk[5](outs[u]).astype(o_ref.dtype)


def _far_sweep(q_ref, ks_ref, vs_ref, neg_all, stage_refs, acc_ref, l_ref, nb):
    R = NSA_REP
    n_slc = neg_all.shape[1] - NEG_ROWS
    i0 = pl.program_id(2) * NSA_SUB
    n_far = [jnp.maximum(i0 + u - 1, 0) for u in range(NSA_SUB)]
    offs = [0]
    for u in range(NSA_SUB):
        offs.append(offs[-1] + (n_far[u] + FAR_TILES - 1) // FAR_TILES)
    n_tot = offs[-1]

    def locate(c):
        u = sum(jnp.where(c >= offs[k], 1, 0) for k in range(1, NSA_SUB))
        start = sum(jnp.where(u == k, offs[k], 0) for k in range(NSA_SUB))
        far = sum(jnp.where(u == k, n_far[k], 0) for k in range(NSA_SUB))
        tile = FAR_TILES * (c - start)
        return u, jnp.clip(tile, 0, nb - FAR_TILES), far - tile

    def k_rows(J0):
        return ks_ref[0, pl.ds(pl.multiple_of(J0 * Q_BLOCK, Q_BLOCK), FAR_TILES * Q_BLOCK), :]

    def v_cols(J0):
        return jnp.concatenate([vs_ref[0, J0 + t] for t in range(FAR_TILES)], axis=1)

    def q_cols(u):
        return jnp.concatenate([q_ref[r, u] for r in range(R)], axis=1)

    def sel_mask(u, J0, n_valid):
        parts = []
        for t in range(2 * FAR_TILES):
            row = jnp.where(t // 2 < n_valid, 2 * J0 + t, n_slc)
            parts.append(jnp.broadcast_to(neg_all[u, pl.ds(row, 1), :], (SLC_LEN, Q_BLOCK)))
        return jnp.concatenate([jnp.concatenate(parts, axis=0)] * R, axis=1)

    def add_values(c, e_ref):
        u, tile, _ = locate(c)
        acc_ref[u] = acc_ref[u] + _dot(v_cols(tile), e_ref[...])

    def stage(c, s_in, s_out, e_in, e_out):
        u0, tile0, _ = locate(c - 1)
        u1, tile1, valid1 = locate(c)
        u2, tile2, _ = locate(c + 1)
        v_prev, acc_prev = v_cols(tile0), acc_ref[u0]
        mask, l_cur = sel_mask(u1, tile1, valid1), l_ref[u1]
        k_next, q_next = k_rows(tile2), q_cols(u2)
        pv = _dot(v_prev, e_in[...])
        s_out[...] = _dot(k_next, q_next)
        e = jnp.exp2(s_in[...] + mask)
        e_out[...] = e.astype(BF16)
        acc_ref[u0] = acc_prev + pv
        l_ref[u1] = l_cur + jnp.sum(e, axis=0, keepdims=True)

    s_a, s_b, e_a, e_b = stage_refs
    e_b[...] = jnp.zeros_like(e_b)
    u_first, tile_first, _ = locate(0)
    s_a[...] = _dot(k_rows(tile_first), q_cols(u_first))

    def pair(c):
        stage(c, s_a, s_b, e_b, e_a)
        stage(c + 1, s_b, s_a, e_a, e_b)

    def quad(c):
        pair(c)
        pair(c + 2)

    def octet(c):
        quad(c)
        quad(c + 4)

    def long_body(t, carry):
        octet(16 * t)
        octet(16 * t + 8)
        return carry

    n_long = n_tot // 16
    lax.fori_loop(0, n_long, long_body, 0)
    done = 16 * n_long

    @pl.when(n_tot - done >= 8)
    def _():
        octet(done)

    done = done + 8 * ((n_tot - done) // 8)

    @pl.when(n_tot - done >= 4)
    def _():
        quad(done)

    done = done + 4 * ((n_tot - done) // 4)

    @pl.when(n_tot - done >= 2)
    def _():
        pair(done)

    @pl.when(n_tot % 2 == 1)
    def _():
        stage(n_tot - 1, s_a, s_b, e_b, e_a)
        add_values(n_tot - 1, e_a)

    @pl.when((n_tot % 2 == 0) & (n_tot > 0))
    def _():
        add_values(n_tot - 1, e_b)


def _nsa_block(u, refs, neg_ref, acc_ref, l_ref, nb):
    (q_ref, kc_ref, vc_ref, ks_ref, vs_ref, kw_ref, vw_ref, gate_ref, cbias_ref, sbias_ref, wbias_ref,
     ov_ref, msel_ref) = refs
    i = pl.program_id(2) * NSA_SUB + u
    R = NSA_REP
    N = R * Q_BLOCK
    nc = kc_ref.shape[2]
    q_t = jnp.concatenate([q_ref[r, u] for r in range(R)], axis=1)
    n_win = NSA_WINDOW // Q_BLOCK

    def k_rows(ref, J0, n_tiles):
        return ref[0, pl.ds(pl.multiple_of(J0 * Q_BLOCK, Q_BLOCK), n_tiles * Q_BLOCK), :]

    def v_cols(ref, J0, n_tiles):
        return jnp.concatenate([ref[0, J0 + u] for u in range(n_tiles)], axis=1)

    def sel_mask(J0, n_tiles, n_valid=None):
        n_slc = neg_ref.shape[0] - NEG_ROWS
        parts = []
        for u in range(2 * n_tiles):
            row = 2 * J0 + u
            if n_valid is not None:
                row = jnp.where(u // 2 < n_valid, row, n_slc)
            parts.append(jnp.broadcast_to(neg_ref[pl.ds(row, 1), :], (SLC_LEN, Q_BLOCK)))
        return jnp.concatenate([jnp.concatenate(parts, axis=0)] * R, axis=1)

    def compressed(fast):
        cb = cbias_ref[0, pl.ds(pl.multiple_of(8 * (nb - 1) - 8 * i, 8), nc), :]
        s = _dot(kc_ref[0, 0], q_t) + cb
        if fast:
            e = jnp.exp2(s)
        else:
            e = jnp.where(s > 0.5 * NEG_INF, jnp.exp2(s - jnp.max(s, axis=0, keepdims=True)), 0.0)
        l = jnp.sum(e, axis=0, keepdims=True)
        p = e * jnp.where(l > 0.0, 1.0 / l, 0.0)
        return p, l, _dot(vc_ref[0, 0], p.astype(BF16))

    def select_blocks(p_c):
        p_sum = p_c[:, 0:Q_BLOCK]
        for r in range(1, R):
            p_sum = p_sum + p_c[:, r * Q_BLOCK:(r + 1) * Q_BLOCK]
        p_hi = p_sum.astype(BF16)
        p_lo = (p_sum - p_hi.astype(F32)).astype(BF16)
        imp = _dot(ov_ref[...], p_hi) + _dot(ov_ref[...], p_lo)
        n_slc = imp.shape[0]
        jj = lax.broadcasted_iota(jnp.int32, imp.shape, 0)
        qq = lax.broadcasted_iota(jnp.int32, imp.shape, 1)
        cur = 2 * i + jnp.where(qq >= SLC_LEN, 1, 0)
        forced = (jj == 0) | (jj == cur) | (jj == cur - 1)
        score = jnp.where(forced, TAKEN, jnp.where(jj <= cur, imp, -1.0))
        grp = 8
        j_rows = [jj[a:a + grp] for a in range(0, n_slc, grp)]
        for _ in range(min(SLC_TOPN, n_slc) - N_FORCED):
            cand = [(score[a:a + grp], j_rows[a // grp]) for a in range(0, n_slc, grp)]
            while len(cand) > 1:
                nxt = []
                for a in range(0, len(cand) - 1, 2):
                    (s0, j0), (s1, j1) = cand[a], cand[a + 1]
                    right = s1 > s0
                    nxt.append((jnp.maximum(s0, s1), jnp.where(right, j1, j0)))
                cand = nxt + cand[len(cand) - len(cand) % 2:]
            s8, j8 = cand[0]
            best = jnp.max(s8, axis=0, keepdims=True)
            first = jnp.min(jnp.where(s8 == best, j8, n_slc), axis=0, keepdims=True)
            score = jnp.where(jj == first, TAKEN, score)
        neg_ref[:n_slc] = jnp.where(score == TAKEN, msel_ref[0], NEG_INF)
        neg_ref[n_slc:] = jnp.full((NEG_ROWS, Q_BLOCK), NEG_INF, F32)

    def add_chunk(st, s, v_t):
        e = jnp.exp2(s)
        return st[0] + jnp.sum(e, axis=0, keepdims=True), st[1] + _dot(v_t, e.astype(BF16))

    zero = (jnp.zeros((1, N), F32), jnp.zeros((HEAD_DIM, N), F32))

    def selected_near():
        J0 = jnp.maximum(i - 1, 0)
        bias = sbias_ref[0, pl.ds(pl.multiple_of((2 - (i - J0)) * Q_BLOCK, Q_BLOCK), 2 * Q_BLOCK), :]
        return add_chunk(zero, _dot(k_rows(ks_ref, J0, 2), q_t) + bias + sel_mask(J0, 2), v_cols(vs_ref, J0, 2))

    def window_fast():
        J0 = jnp.maximum(i - n_win, 0)
        bias = wbias_ref[0, pl.ds(pl.multiple_of((n_win - (i - J0)) * Q_BLOCK, Q_BLOCK), (n_win + 1) * Q_BLOCK), :]
        return add_chunk(zero, _dot(k_rows(kw_ref, J0, n_win + 1), q_t) + bias, v_cols(vw_ref, J0, n_win + 1))

    def sweep_exact(k_ref, v_ref, bias_ref, far_dist, n_back, with_sel):
        def body(t, st):
            J = i - t
            off = pl.multiple_of((far_dist - jnp.minimum(t, far_dist)) * Q_BLOCK, Q_BLOCK)
            s = _dot(k_rows(k_ref, J, 1), q_t) + bias_ref[0, pl.ds(off, Q_BLOCK), :]
            if with_sel:
                s = s + sel_mask(J, 1)
            return _softmax_tile_update(s, v_ref[0, J], *st)

        init = (jnp.full((1, N), NEG_INF, F32),) + zero
        _, l, acc = lax.fori_loop(0, jnp.minimum(i, n_back) + 1, body, init)
        return l, acc

    vals = {}

    def load_phase():
        vals["cmp"] = compressed(True)
        vals["win"] = window_fast()

    def select_phase():
        select_blocks(vals["cmp"][0])

    def near_phase():
        l_near, acc_near = selected_near()
        l_ref[u] = jnp.broadcast_to(l_near, l_ref.shape[1:])
        acc_ref[u] = acc_near

    def exact_path():
        p_x, _, o_x = compressed(False)
        select_blocks(p_x)
        l_sx, a_sx = sweep_exact(ks_ref, vs_ref, sbias_ref, 2, nb, True)
        l_wx, a_wx = sweep_exact(kw_ref, vw_ref, wbias_ref, n_win, n_win, False)
        return o_x, a_sx * (1.0 / l_sx), a_wx * (1.0 / l_wx)

    def gate_row(br):
        return jnp.concatenate([gate_ref[br, 0, r:r + 1, u * Q_BLOCK:(u + 1) * Q_BLOCK] for r in range(R)], axis=1)

    def fast_result():
        _, l_c, o_c = vals["cmp"]
        l_w, a_w = vals["win"]
        l_s, a_s = l_ref[u, 0:1, :], acc_ref[u]
        lane_q = lax.broadcasted_iota(jnp.int32, (1, N), 1) % Q_BLOCK
        has_cmp = (i > 0) | (lane_q >= CMP_LEN - 1)
        l_min = jnp.minimum(jnp.minimum(l_s, l_w), jnp.where(has_cmp, l_c, 1.0))
        return l_min, (o_c, a_s * (1.0 / l_s), a_w * (1.0 / l_w))

    def combine(outs):
        o_c, o_s, o_w = outs
        o_t = gate_row(0) * o_c + gate_row(1) * o_s + gate_row(2) * o_w
        return _pairs_to_rows(o_t)

    return load_phase, select_phase, near_phase, fast_result, exact_path, combine


def _nsa_attention(colmat, rowmat, cmp_rows, cmp_cols, gates, cbias, sbias, wbias, ov_t, msel, B, S):
    nb = S // Q_BLOCK
    nc = S // CMP_STRIDE
    n_slc = S // SLC_LEN
    G, R = NSA_KV_HEADS, NSA_REP
    kern = functools.partial(_nsa_kernel, nb=nb)
    ns = nb // NSA_SUB
    return pl.pallas_call(
        kern,
        grid=(B, G, ns),
        in_specs=[
            pl.BlockSpec((R, NSA_SUB, HEAD_DIM, LANES), lambda b, g, i: (g, b * ns + i, 0, 0)),
            pl.BlockSpec((1, 1, nc, HEAD_DIM), lambda b, g, i: (g, b, 0, 0)),
            pl.BlockSpec((1, 1, HEAD_DIM, nc), lambda b, g, i: (NSA_KV_HEADS + g, b, 0, 0)),
            pl.BlockSpec((1, S, HEAD_DIM), lambda b, g, i: (ROW_KS + g, b, 0)),
            pl.BlockSpec((1, nb, HEAD_DIM, LANES), lambda b, g, i: (COL_VS + g, b, 0, 0)),
            pl.BlockSpec((1, S, HEAD_DIM), lambda b, g, i: (ROW_KW + g, b, 0)),
            pl.BlockSpec((1, nb, HEAD_DIM, LANES), lambda b, g, i: (COL_VW + g, b, 0, 0)),
            pl.BlockSpec((N_BRANCHES, 1, R, NSA_SUB * LANES), lambda b, g, i: (0, g, 0, b * ns + i)),
            pl.BlockSpec((1, cbias.shape[1], R * Q_BLOCK), lambda b, g, i: (g, 0, 0)),
            pl.BlockSpec((1, sbias.shape[1], R * Q_BLOCK), lambda b, g, i: (g, 0, 0)),
            pl.BlockSpec((1, wbias.shape[1], R * Q_BLOCK), lambda b, g, i: (g, 0, 0)),
            pl.BlockSpec((n_slc, nc), lambda b, g, i: (0, 0)),
            pl.BlockSpec((1, 1, Q_BLOCK), lambda b, g, i: (g, 0, 0)),
        ],
        out_specs=pl.BlockSpec((1, NSA_SUB * Q_BLOCK, R * HEAD_DIM), lambda b, g, i: (b, i, g)),
        out_shape=jax.ShapeDtypeStruct((B, S, NSA_HEADS * HEAD_DIM), BF16),
        scratch_shapes=[pltpu.VMEM((NSA_SUB, n_slc + NEG_ROWS, Q_BLOCK), F32)]
        + [pltpu.VMEM((FAR_TILES * Q_BLOCK, R * Q_BLOCK), F32)] * 2
        + [pltpu.VMEM((FAR_TILES * Q_BLOCK, R * Q_BLOCK), BF16)] * 2
        + [pltpu.VMEM((NSA_SUB, HEAD_DIM, R * Q_BLOCK), F32),
           pltpu.VMEM((NSA_SUB, 8, R * Q_BLOCK), F32)],
        compiler_params=_cparams(("parallel", "parallel", "arbitrary")),
        name="nsa_attention",
    )(colmat, cmp_rows, cmp_cols, rowmat, colmat, rowmat, colmat, gates, cbias, sbias, wbias, ov_t, msel)


SB_QB = 512
SB_KT = 256
SB_ZERO_LOG = -160.0


def _sb_kernel(q_ref, k_ref, v_ref, tri_ref, o_ref):
    i = pl.program_id(1)
    H = SB_HEADS
    tri = tri_ref[...]
    assert SB_QB == 2 * SB_KT
    below_diag = (lax.broadcasted_iota(jnp.int32, (SB_KT, SB_KT), 1)
                  < lax.broadcasted_iota(jnp.int32, (SB_KT, SB_KT), 0))

    def step(J, rows, st, masked):
        out = []
        for h in range(H):
            carry, acc = st[h]
            k_t = jnp.concatenate([k_ref[h, (SB_KT // LANES) * J + c] for c in range(SB_KT // LANES)], axis=1)
            z = _dot(q_ref[h, rows, :], k_t)
            sp = jnp.maximum(z, 0.0) + jnp.log2(1.0 + jnp.exp2(-jnp.abs(z)))
            lf = jnp.where(below_diag, -sp, 0.0) if masked else -sp
            lf_hi = lf.astype(BF16)
            lf_lo = (lf - lf_hi.astype(F32)).astype(BF16)
            after = _dot(lf_hi, tri) + _dot(lf_lo, tri) + carry
            w = jnp.exp2(z - sp + after)
            if masked:
                w = jnp.where(below_diag, w, 0.0)
            v = v_ref[h, pl.ds(pl.multiple_of(J * SB_KT, SB_KT), SB_KT), :]
            acc = acc + _dot(w.astype(BF16), v)
            carry = carry + jnp.sum(lf, axis=1, keepdims=True)
            out.append((carry, acc))
        return tuple(out)

    lo, hi = slice(0, SB_KT), slice(SB_KT, SB_QB)
    zero = tuple((jnp.zeros((SB_KT, 1), F32), jnp.zeros((SB_KT, HEAD_DIM), F32)) for _ in range(H))
    st_hi = step(2 * i, hi, step(2 * i + 1, hi, zero, True), False)
    st_lo = step(2 * i, lo, zero, True)

    def alive(st):
        worst = st[0][0]
        for h in range(1, H):
            worst = jnp.maximum(worst, st[h][0])
        return jnp.max(worst) > SB_ZERO_LOG

    def sweep(rows, st):
        def body(c):
            J, _, st = c
            st = step(J, rows, st, False)
            return J - 1, alive(st), st

        return lax.while_loop(lambda c: (c[0] >= 0) & c[1], body, (2 * i - 1, alive(st), st))[2]

    st_lo, st_hi = sweep(lo, st_lo), sweep(hi, st_hi)
    o_ref[0, lo, :] = jnp.concatenate([st_lo[h][1] for h in range(H)], axis=1).astype(o_ref.dtype)
    o_ref[0, hi, :] = jnp.concatenate([st_hi[h][1] for h in range(H)], axis=1).astype(o_ref.dtype)


def _sb_attention(colmat, rowmat, tri, B, S):
    nq = S // SB_QB
    nb = S // LANES
    H = SB_HEADS
    return pl.pallas_call(
        _sb_kernel,
        grid=(B, nq),
        in_specs=[
            pl.BlockSpec((H, SB_QB, HEAD_DIM), lambda b, i: (ROW_SQ // H, b * nq + i, 0)),
            pl.BlockSpec((H, nb, HEAD_DIM, LANES), lambda b, i: (COL_SK // H, b, 0, 0)),
            pl.BlockSpec((H, S, HEAD_DIM), lambda b, i: (ROW_SV // H, b, 0)),
            pl.BlockSpec((SB_KT, SB_KT), lambda b, i: (0, 0)),
        ],
        out_specs=pl.BlockSpec((1, SB_QB, H * HEAD_DIM), lambda b, i: (b, i, 0)),
        out_shape=jax.ShapeDtypeStruct((B, S, H * HEAD_DIM), BF16),
        compiler_params=_cparams(("parallel", "arbitrary")),
        name="sb_attention",
    )(rowmat, colmat, rowmat, tri)


SWA_SUB = 8


def _swa_kernel(q_ref, k_ref, v_ref, bias_ref, sink_ref, o_ref):
    R = SWA_REP
    sink = sink_ref[0]
    outs = []
    for u in range(SWA_SUB):
        i = pl.program_id(2) * SWA_SUB + u
        q_t = jnp.concatenate([q_ref[r, u] for r in range(R)], axis=1)
        J0 = jnp.maximum(i - 1, 0)
        k = k_ref[0, pl.ds(pl.multiple_of(J0 * Q_BLOCK, Q_BLOCK), 2 * Q_BLOCK), :]
        bias = bias_ref[0, pl.ds(pl.multiple_of((1 - (i - J0)) * Q_BLOCK, Q_BLOCK), 2 * Q_BLOCK), :]
        s = _dot(k, q_t) + bias
        m = jnp.maximum(jnp.max(s, axis=0, keepdims=True), sink)
        e = jnp.exp(s - m)
        den = jnp.sum(e, axis=0, keepdims=True) + jnp.exp(sink - m)
        v_t = jnp.concatenate([v_ref[0, J0], v_ref[0, J0 + 1]], axis=1)
        o_t = _dot(v_t, e.astype(BF16)) * (1.0 / den)
        outs.append(_pairs_to_rows(o_t).astype(o_ref.dtype))
    for u in range(SWA_SUB):
        o_ref[0, u * Q_BLOCK:(u + 1) * Q_BLOCK, :] = outs[u]


def _swa_attention(colmat, rowmat, bias, sinks, B, S):
    nb = S // Q_BLOCK
    ns = nb // SWA_SUB
    G, R = SWA_KV_HEADS, SWA_REP
    return pl.pallas_call(
        _swa_kernel,
        grid=(B, G, ns),
        in_specs=[
            pl.BlockSpec((R, SWA_SUB, HEAD_DIM, LANES), lambda b, g, i: (COL_WQ // R + g, b * ns + i, 0, 0)),
            pl.BlockSpec((1, S, HEAD_DIM), lambda b, g, i: (ROW_WK + g, b, 0)),
            pl.BlockSpec((1, nb, HEAD_DIM, LANES), lambda b, g, i: (COL_WV + g, b, 0, 0)),
            pl.BlockSpec((1, 3 * Q_BLOCK, R * Q_BLOCK), lambda b, g, i: (g, 0, 0)),
            pl.BlockSpec((1, 1, R * Q_BLOCK), lambda b, g, i: (g, 0, 0)),
        ],
        out_specs=pl.BlockSpec((1, SWA_SUB * Q_BLOCK, R * HEAD_DIM), lambda b, g, i: (b, i, g)),
        out_shape=jax.ShapeDtypeStruct((B, S, SWA_HEADS * HEAD_DIM), BF16),
        compiler_params=_cparams(("parallel", "parallel", "arbitrary")),
        name="swa_attention",
    )(colmat, rowmat, colmat, bias, sinks)


MERGE_TM = 512


def _merge_kernel(x_ref, nw_ref, wg_ref, ya_ref, yb_ref, yc_ref, ua_ref, ub_ref, uc_ref, wo_ref, o_ref):
    x = x_ref[...]
    h = _rms_rows(x, nw_ref[...]).astype(BF16)
    merged = None
    for br, (y_ref, u_ref) in enumerate(((ya_ref, ua_ref), (yb_ref, ub_ref), (yc_ref, uc_ref))):
        g = jax.nn.sigmoid(_dot(h, wg_ref[:, br * D_MODEL:(br + 1) * D_MODEL]))
        t = g * _dot(y_ref[...], u_ref[...])
        merged = t if merged is None else merged + t
    o_ref[...] = x + _dot(merged.astype(BF16), wo_ref[...])


def _merge(x2, nw, w_bgate, y_nsa, y_sb, y_swa, u_nsa, u_sb, u_swa, w_out, l):
    T = x2.shape[0]
    full = lambda w: _layer_spec(l, w.shape[1:])
    rows = lambda n: pl.BlockSpec((MERGE_TM, n), lambda i: (i, 0))
    return pl.pallas_call(
        _merge_kernel,
        grid=(T // MERGE_TM,),
        in_specs=[rows(D_MODEL), pl.BlockSpec((1, D_MODEL), lambda i: (0, 0)), full(w_bgate),
                  rows(y_nsa.shape[1]), rows(y_sb.shape[1]), rows(y_swa.shape[1]),
                  full(u_nsa), full(u_sb), full(u_swa), full(w_out)],
        out_specs=rows(D_MODEL),
        out_shape=jax.ShapeDtypeStruct((T, D_MODEL), F32),
        compiler_params=_cparams(("parallel",)),
        name="merge",
    )(x2, nw.reshape(1, D_MODEL), w_bgate, y_nsa, y_sb, y_swa, u_nsa, u_sb, u_swa, w_out)


def _t5_bucket(dist):
    max_exact = NUM_BUCKETS // 2
    d = jnp.maximum(dist, 0)
    df = jnp.maximum(d, 1).astype(F32)
    large = max_exact + (jnp.log(df / max_exact) / math.log(MAX_DISTANCE / max_exact)
                         * (NUM_BUCKETS - max_exact)).astype(jnp.int32)
    large = jnp.minimum(large, NUM_BUCKETS - 1)
    return jnp.where(d < max_exact, d, large)


def _toeplitz(ext, delta):
    H = ext.shape[0]
    n = Q_BLOCK
    c = delta + MAX_DISTANCE
    w = ext[:, c - (n - 1):c + n]
    flat = jnp.pad(jnp.broadcast_to(w[:, None, :], (H, n, 2 * n - 1)), ((0, 0), (0, 0), (0, 1))).reshape(H, 2 * n * n)
    return flat[:, n - 1:n - 1 + n * (2 * n - 1)].reshape(H, n, 2 * n - 1)[:, :, :n]


def _lanes(t, G, R):
    K, Q = t.shape[1:]
    return t.reshape(G, R, K, Q).transpose(0, 2, 1, 3).reshape(G, K, R * Q)


def _bias_tables(rel_bias, nb, qk_bound):
    n = Q_BLOCK
    G, R = NSA_KV_HEADS, NSA_REP
    by_dist = rel_bias[_t5_bucket(jnp.arange(MAX_DISTANCE)), :].T.astype(F32)
    far = rel_bias[NUM_BUCKETS - 1, :].astype(F32)
    neg = jnp.full((1, n), NEG_INF, F32)
    k = jnp.arange(n)[:, None]
    q = jnp.arange(n)[None, :]
    above = (k > q)[None]

    rel = (by_dist[:NSA_HEADS] - far[:NSA_HEADS, None]) * LOG2E
    m_g = qk_bound + jnp.maximum(jnp.max(rel.reshape(G, -1), axis=1), 0.0)
    m_h = jnp.repeat(m_g, R)[:, None, None]
    ext = jnp.concatenate([jnp.broadcast_to(neg, (NSA_HEADS, n)), rel, jnp.zeros((NSA_HEADS, n), F32)], axis=1)
    t0, t1 = _toeplitz(ext, 0), _toeplitz(ext, n)
    zeros = jnp.zeros_like(t0)
    masked = jnp.full_like(t0, NEG_INF)
    sbias = _lanes(jnp.concatenate([zeros, t1, t0, masked], axis=1), G, R)
    n_win = NSA_WINDOW // n
    edge = jnp.where(above, zeros, NEG_INF)
    tiles = [edge] + [zeros] * (n_win - 2) + [t1, t0]
    wbias = _lanes(jnp.concatenate([t - m_h for t in tiles] + [masked] * n_win, axis=1), G, R)
    off = 8 * (nb - 1)
    near = jnp.stack([ext[:, 97 - CMP_STRIDE * m:97 - CMP_STRIDE * m + n] for m in range(-9, 7)], axis=1)
    n_rows = 16 * nb - 8
    cb = jnp.concatenate([jnp.zeros((NSA_HEADS, off - 9, n), F32), near,
                          jnp.full((NSA_HEADS, n_rows - off - 7, n), NEG_INF, F32)], axis=1) - m_h
    cbias = _lanes(cb, G, R)
    msel = jnp.broadcast_to(-m_g[:, None, None], (G, 1, n))

    hs = slice(NSA_HEADS, NSA_HEADS + SWA_HEADS)
    ext_w = jnp.concatenate([jnp.broadcast_to(neg, (SWA_HEADS, n)), by_dist[hs],
                             jnp.broadcast_to(far[hs, None], (SWA_HEADS, n))], axis=1)
    s0 = _toeplitz(ext_w, 0)
    s1 = jnp.where(above, _toeplitz(ext_w, n), NEG_INF)
    swa_bias = _lanes(jnp.concatenate([s1, s0, jnp.full_like(s0, NEG_INF)], axis=1), SWA_KV_HEADS, SWA_REP)
    return cbias, sbias, wbias, msel, swa_bias


def _overlap_t(S):
    nc, n_slc = S // CMP_STRIDE, S // SLC_LEN
    n = np.arange(nc)[None, :]
    j = np.arange(n_slc)[:, None]
    ov = (n * CMP_STRIDE < (j + 1) * SLC_LEN) & (n * CMP_STRIDE + CMP_LEN - 1 >= j * SLC_LEN) & (n < nc - 1)
    return jnp.asarray(ov, BF16)


def _head_gains(heads, norm_of, scale_of):
    ones = jnp.ones((HEAD_DIM,), F32)
    gains = [norm_of[name].astype(F32) * scale_of.get(name, 1.0) if name in norm_of else ones for name, _ in heads]
    return jnp.concatenate(gains).reshape(1, -1)


def kernel(x, rel_bias, ffn1_norm, ffn1_w_gate, ffn1_w_up, ffn1_w_down, mix_norm, w_in, nsa_q_norm, nsa_k_norm, nsa_cmp_pos, nsa_cmp_k_w1, nsa_cmp_k_w2, nsa_cmp_v_w1, nsa_cmp_v_w2, swa_q_norm, swa_k_norm, swa_sinks, w_up_nsa, w_up_sb, w_up_swa, w_out, ffn2_norm, ffn2_w_gate, ffn2_w_up, ffn2_w_down):
    B, S, D = x.shape
    T = B * S
    nb = S // Q_BLOCK
    nc = S // CMP_STRIDE
    depth = w_in.shape[0]
    scale = HEAD_DIM ** -0.5
    scale_of = {"nq": scale * LOG2E, "sq": scale * LOG2E, "wq": scale}
    qk_bound = (BOUND_SLACK * LOG2E * HEAD_DIM ** 0.5
                * jnp.max(jnp.abs(nsa_q_norm.astype(F32))) * jnp.max(jnp.abs(nsa_k_norm.astype(F32))))
    cbias, sbias, wbias, msel, swa_bias = _bias_tables(rel_bias, nb, qk_bound)
    plan = _proj_plan(scale_of)
    ov_t = _overlap_t(S)
    tri = jnp.asarray(np.tril(np.ones((SB_KT, SB_KT), np.float32), -1), BF16)

    (ffn1_w_gate, ffn1_w_up, ffn1_w_down, ffn2_w_gate, ffn2_w_up, ffn2_w_down, w_up_nsa, w_up_sb, w_up_swa,
     w_out) = (w.astype(BF16) for w in (ffn1_w_gate, ffn1_w_up, ffn1_w_down, ffn2_w_gate, ffn2_w_up, ffn2_w_down,
                                        w_up_nsa, w_up_sb, w_up_swa, w_out))
    tail0 = OFF_NGATE + N_GATE
    w_head, w_tail = w_in[:, :, :OFF_NGATE].astype(BF16), w_in[:, :, tail0:].astype(BF16)
    w_bgate = w_tail[:, :, OFF_BGATE - tail0:]
    w_gate = jnp.pad(w_in[:, :, OFF_NGATE:tail0], ((0, 0), (0, 0), (0, LANES - N_GATE))).astype(BF16)
    runs = []
    for _, off in ROW_HEADS + COL_HEADS:
        if runs and runs[-1][1] == off:
            runs[-1][1] = off + HEAD_DIM
        else:
            runs.append([off, off + HEAD_DIM])
    pieces = [w_head[:, :, a:b] if a < OFF_NGATE else w_tail[:, :, a - tail0:b - tail0] for a, b in runs]
    w_proj = jnp.concatenate(pieces + [w_gate], axis=2)

    x2 = x.reshape(T, D)
    for l in range(depth):
        x2 = _ffn(x2, ffn1_norm[l], ffn1_w_gate, ffn1_w_up, ffn1_w_down, l)

        norm_of = {"nq": nsa_q_norm[l], "nks": nsa_k_norm[l], "nkw": nsa_k_norm[l],
                   "wq": swa_q_norm[l], "wk": swa_k_norm[l]}
        gain = jnp.concatenate([_head_gains(ROW_HEADS, norm_of, scale_of), _head_gains(COL_HEADS, norm_of, scale_of),
                                jnp.ones((1, LANES), F32)], axis=1)
        rowmat, colmat, gates = _proj(x2, mix_norm[l], w_proj, gain, plan, l)
        gates = gates[:N_GATE].reshape(N_BRANCHES, NSA_KV_HEADS, NSA_REP, T)

        rows4 = rowmat[ROW_KC:ROW_KC + 2 * NSA_KV_HEADS].reshape(2 * NSA_KV_HEADS, B, nc, CMP_STRIDE * HEAD_DIM)
        cmp_rows, cmp_cols = _compress(rows4, jnp.stack([nsa_cmp_k_w1[l], nsa_cmp_v_w1[l]]),
                                       jnp.stack([nsa_cmp_k_w2[l], nsa_cmp_v_w2[l]]),
                                       nsa_cmp_pos[l], nsa_k_norm[l])

        y_nsa = _nsa_attention(colmat, rowmat, cmp_rows, cmp_cols, gates, cbias, sbias, wbias, ov_t, msel, B, S)
        y_sb = _sb_attention(colmat, rowmat, tri, B, S)
        sinks = jnp.repeat(swa_sinks[l].astype(F32).reshape(SWA_KV_HEADS, 1, SWA_REP), Q_BLOCK, axis=2)
        y_swa = _swa_attention(colmat, rowmat, swa_bias, sinks, B, S)

        x2 = _merge(x2, mix_norm[l], w_bgate, y_nsa.reshape(T, -1), y_sb.reshape(T, -1),
                    y_swa.reshape(T, -1), w_up_nsa, w_up_sb, w_up_swa, w_out, l)
        x2 = _ffn(x2, ffn2_norm[l], ffn2_w_gate, ffn2_w_up, ffn2_w_down, l)
    return x2.reshape(B, S, D)
```

```python
import functools
import math

import numpy as np
import jax
import jax.numpy as jnp
from jax import lax
from jax.experimental import pallas as pl
from jax.experimental.pallas import tpu as pltpu

D_MODEL = 1024
HEAD_DIM = 64
Q_BLOCK = 128
NSA_HEADS = 8
NSA_KV_HEADS = 2
NSA_REP = NSA_HEADS // NSA_KV_HEADS
CMP_LEN = 32
CMP_STRIDE = 16
CMP_HIDDEN = 256
SLC_LEN = 64
SLC_TOPN = 16
NSA_WINDOW = 512
SB_HEADS = 4
SWA_HEADS = 4
SWA_KV_HEADS = 2
SWA_REP = SWA_HEADS // SWA_KV_HEADS
SWA_WINDOW = 128
NUM_BUCKETS = 32
MAX_DISTANCE = 128
D_FF = 2816
NORM_EPS = 1e-6
NEG_INF = -1e30
N_BRANCHES = 3
N_GATE = 3 * NSA_HEADS

LANES = 128
VMEM_LIMIT = 56 * 1024 * 1024
BF16 = jnp.bfloat16
F32 = jnp.float32

_sizes = (NSA_HEADS * HEAD_DIM,) + (NSA_KV_HEADS * HEAD_DIM,) * 6 + (N_GATE,) \
    + (SB_HEADS * HEAD_DIM,) * 3 + (SWA_HEADS * HEAD_DIM, SWA_KV_HEADS * HEAD_DIM,
                                    SWA_KV_HEADS * HEAD_DIM, N_BRANCHES * D_MODEL)
_offs = np.concatenate([[0], np.cumsum(_sizes)])
(OFF_NQ, OFF_NKC, OFF_NVC, OFF_NKS, OFF_NVS, OFF_NKW, OFF_NVW, OFF_NGATE,
 OFF_SQ, OFF_SK, OFF_SV, OFF_WQ, OFF_WK, OFF_WV, OFF_BGATE) = (int(o) for o in _offs[:-1])

ROW_HEADS = ([("nkc", OFF_NKC + 64 * g) for g in range(2)] + [("nvc", OFF_NVC + 64 * g) for g in range(2)]
             + [("nks", OFF_NKS + 64 * g) for g in range(2)] + [("nkw", OFF_NKW + 64 * g) for g in range(2)]
             + [("sq", OFF_SQ + 64 * h) for h in range(4)] + [("sv", OFF_SV + 64 * h) for h in range(4)]
             + [("wk", OFF_WK + 64 * g) for g in range(2)])
ROW_KC, ROW_VC, ROW_KS, ROW_KW, ROW_SQ, ROW_SV, ROW_WK = 0, 2, 4, 6, 8, 12, 16
COL_HEADS = ([("nq", OFF_NQ + 64 * h) for h in range(8)] + [("nvs", OFF_NVS + 64 * g) for g in range(2)]
             + [("nvw", OFF_NVW + 64 * g) for g in range(2)] + [("sk", OFF_SK + 64 * h) for h in range(4)]
             + [("wq", OFF_WQ + 64 * h) for h in range(4)] + [("wv", OFF_WV + 64 * g) for g in range(2)])
COL_NQ, COL_VS, COL_VW, COL_SK, COL_WQ, COL_WV = 0, 8, 10, 12, 16, 20


def _dot(a, b):
    return jnp.dot(a, b, preferred_element_type=F32)


def _rms_rows(x, w):
    ms = jnp.mean(x * x, axis=-1, keepdims=True)
    return x * lax.rsqrt(ms + NORM_EPS) * w


def _cparams(sem):
    return pltpu.CompilerParams(dimension_semantics=sem, vmem_limit_bytes=VMEM_LIMIT)


FFN_TM = 1024
MXU_TILE = 256
FFN_FC = MXU_TILE


def _ffn_kernel(x_ref, nw_ref, wg_ref, wu_ref, wd_ref, o_ref):
    x = x_ref[...]
    h = _rms_rows(x, nw_ref[...]).astype(BF16)
    acc = jnp.zeros(x.shape, F32)
    for c in range(D_FF // FFN_FC):
        cols = slice(c * FFN_FC, (c + 1) * FFN_FC)
        g = _dot(h, wg_ref[:, cols])
        u = _dot(h, wu_ref[:, cols])
        a = (g * jax.nn.sigmoid(g)) * u
        acc = acc + _dot(a.astype(BF16), wd_ref[cols, :])
    o_ref[...] = x + 0.5 * acc


def _layer_spec(l, shape, **kw):
    return pl.BlockSpec((None,) + tuple(shape), lambda i: (l, 0, 0), **kw)


def _ffn(x2, nw, wg, wu, wd, l):
    T = x2.shape[0]
    resident = lambda shape: _layer_spec(l, shape, pipeline_mode=pl.Buffered(1))
    return pl.pallas_call(
        _ffn_kernel,
        grid=(T // FFN_TM,),
        in_specs=[
            pl.BlockSpec((FFN_TM, D_MODEL), lambda i: (i, 0)),
            pl.BlockSpec((1, D_MODEL), lambda i: (0, 0)),
            resident((D_MODEL, D_FF)), resident((D_MODEL, D_FF)), resident((D_FF, D_MODEL)),
        ],
        out_specs=pl.BlockSpec((FFN_TM, D_MODEL), lambda i: (i, 0)),
        out_shape=jax.ShapeDtypeStruct((T, D_MODEL), F32),
        compiler_params=_cparams(("parallel",)),
        name="ffn",
    )(x2, nw.reshape(1, D_MODEL), wg, wu, wd)


PROJ_TM = 512


GATE_ROWS = 32

_NORMED = ("nq", "nks", "nkw", "wq", "wk")


def _proj_plan(scale_of):
    plan = []
    for kind, heads in (("row", ROW_HEADS), ("col", COL_HEADS)):
        for p in range(len(heads) // 2):
            name = heads[2 * p][0]
            assert heads[2 * p + 1][0] == name
            plan.append((kind, p, name in _NORMED, 1.0 if name in _NORMED else scale_of.get(name, 1.0)))
    plan.append(("gate", 0, False, 1.0))
    return tuple(plan)


def _proj_kernel(x_ref, nw_ref, w_ref, gain_ref, bd_ref, row_ref, col_ref, gate_ref, *, plan):
    h = _rms_rows(x_ref[...], nw_ref[...]).astype(BF16)
    pair = 2 * MXU_TILE
    for c0 in range(0, len(plan) * LANES, pair):
        width = min(pair, len(plan) * LANES - c0)
        zg = _dot(h, w_ref[:, c0:c0 + width])
        for u in range(width // LANES):
            t = c0 // LANES + u
            kind, p, normed, scale = plan[t]
            z = zg[:, u * LANES:(u + 1) * LANES]
            if normed:
                ms = _dot((z * z).astype(BF16), bd_ref[...])
                z = z * lax.rsqrt(ms + NORM_EPS) * gain_ref[:, t * LANES:(t + 1) * LANES]
            elif scale != 1.0:
                z = z * scale
            if kind == "row":
                row_ref[2 * p] = z[:, :HEAD_DIM].astype(row_ref.dtype)
                row_ref[2 * p + 1] = z[:, HEAD_DIM:].astype(row_ref.dtype)
            elif kind == "col":
                z_t = z.T
                for c in range(PROJ_TM // LANES):
                    col_ref[2 * p, c] = z_t[:HEAD_DIM, c * LANES:(c + 1) * LANES].astype(col_ref.dtype)
                    col_ref[2 * p + 1, c] = z_t[HEAD_DIM:, c * LANES:(c + 1) * LANES].astype(col_ref.dtype)
            else:
                gate_ref[...] = jax.nn.sigmoid(z).T[:GATE_ROWS]


def _proj(x2, nw, w, gain, plan, l):
    T = x2.shape[0]
    n_row, n_col = len(ROW_HEADS), len(COL_HEADS)
    bd = np.kron(np.eye(2, dtype=np.float32), np.full((HEAD_DIM, HEAD_DIM), 1.0 / HEAD_DIM, np.float32))
    return pl.pallas_call(
        functools.partial(_proj_kernel, plan=plan),
        grid=(T // PROJ_TM,),
        in_specs=[
            pl.BlockSpec((PROJ_TM, D_MODEL), lambda i: (i, 0)),
            pl.BlockSpec((1, D_MODEL), lambda i: (0, 0)),
            _layer_spec(l, w.shape[1:]),
            pl.BlockSpec(gain.shape, lambda i: (0, 0)),
            pl.BlockSpec((LANES, LANES), lambda i: (0, 0)),
        ],
        out_specs=[
            pl.BlockSpec((n_row, PROJ_TM, HEAD_DIM), lambda i: (0, i, 0)),
            pl.BlockSpec((n_col, PROJ_TM // LANES, HEAD_DIM, LANES), lambda i: (0, i, 0, 0)),
            pl.BlockSpec((GATE_ROWS, PROJ_TM), lambda i: (0, i)),
        ],
        out_shape=[jax.ShapeDtypeStruct((n_row, T, HEAD_DIM), BF16),
                   jax.ShapeDtypeStruct((n_col, T // LANES, HEAD_DIM, LANES), BF16),
                   jax.ShapeDtypeStruct((GATE_ROWS, T), F32)],
        compiler_params=_cparams(("parallel",)),
        name="proj",
    )(x2, nw.reshape(1, D_MODEL), w, gain, jnp.asarray(bd, BF16))


def _compress_kernel(r_ref, w1_ref, w2_ref, pos_ref, kn_ref, row_ref, col_ref):
    kind = pl.program_id(0) // NSA_KV_HEADS
    r = r_ref[0, 0]
    nc = r.shape[0]
    half = CMP_STRIDE * HEAD_DIM
    w1 = w1_ref[0]
    p_lo = _dot(r, w1[:half])
    p_hi = _dot(r, w1[half:])
    p_pos = _dot(jnp.broadcast_to(pos_ref[...], (8, CMP_LEN * HEAD_DIM)).astype(BF16), w1)[:1]
    hdn = p_lo + pltpu.roll(p_hi, nc - 1, 0) + p_pos
    hdn = hdn * jax.nn.sigmoid(hdn)
    out = _dot(hdn.astype(BF16), w2_ref[0])
    ms = jnp.sum(out * out, axis=-1, keepdims=True) * (1.0 / HEAD_DIM)
    normed = out * lax.rsqrt(ms + NORM_EPS) * kn_ref[...]
    out = jnp.where(kind == 0, normed, out)
    row_ref[0, 0] = out[:, :HEAD_DIM].astype(row_ref.dtype)
    col_ref[0, 0] = out.T[:HEAD_DIM].astype(col_ref.dtype)


def _compress(rows4, w1, w2, pos, k_norm):
    assert ROW_KC == 0 and ROW_VC == NSA_KV_HEADS
    _, B, NC, _ = rows4.shape
    kn = jnp.pad(k_norm.reshape(1, HEAD_DIM), ((0, 0), (0, LANES - HEAD_DIM)))
    w2p = jnp.pad(w2, ((0, 0), (0, 0), (0, LANES - HEAD_DIM))).astype(BF16)
    return pl.pallas_call(
        _compress_kernel,
        grid=(4, B),
        in_specs=[
            pl.BlockSpec((1, 1, NC, CMP_STRIDE * HEAD_DIM), lambda h, b: (h, b, 0, 0)),
            pl.BlockSpec((1, CMP_LEN * HEAD_DIM, CMP_HIDDEN), lambda h, b: (h // NSA_KV_HEADS, 0, 0)),
            pl.BlockSpec((1, CMP_HIDDEN, LANES), lambda h, b: (h // NSA_KV_HEADS, 0, 0)),
            pl.BlockSpec((1, CMP_LEN * HEAD_DIM), lambda h, b: (0, 0)),
            pl.BlockSpec((1, LANES), lambda h, b: (0, 0)),
        ],
        out_specs=[
            pl.BlockSpec((1, 1, NC, HEAD_DIM), lambda h, b: (h, b, 0, 0)),
            pl.BlockSpec((1, 1, HEAD_DIM, NC), lambda h, b: (h, b, 0, 0)),
        ],
        out_shape=[jax.ShapeDtypeStruct((4, B, NC, HEAD_DIM), BF16),
                   jax.ShapeDtypeStruct((4, B, HEAD_DIM, NC), BF16)],
        compiler_params=_cparams(("parallel", "parallel")),
        name="nsa_compress",
    )(rows4, w1.astype(BF16), w2p, pos.reshape(1, CMP_LEN * HEAD_DIM), kn)


def _softmax_tile_update(s, v_t, m, l, acc):
    m_new = jnp.maximum(m, jnp.max(s, axis=0, keepdims=True))
    alpha = jnp.exp2(m - m_new)
    p = jnp.exp2(s - m_new)
    l_new = alpha * l + jnp.sum(p, axis=0, keepdims=True)
    acc_new = alpha * acc + _dot(v_t, p.astype(BF16))
    return m_new, l_new, acc_new


def _pairs_to_rows(o_t):
    n = o_t.shape[1] // LANES
    outs = []
    for p in range(n // 2):
        a = o_t[:, (2 * p) * LANES:(2 * p + 1) * LANES]
        b = o_t[:, (2 * p + 1) * LANES:(2 * p + 2) * LANES]
        outs.append(jnp.concatenate([a, b], axis=0).T)
    return outs[0] if len(outs) == 1 else jnp.concatenate(outs, axis=1)


FAR_TILES = 4
NSA_SUB = 16
TAKEN = -2.0
NEG_ROWS = 8
N_FORCED = 3
LOG2E = math.log2(math.e)
NSA_TINY = 2.0 ** -100
BOUND_SLACK = 1.05


def _nsa_kernel(q_ref, kc_ref, vc_ref, ks_ref, vs_ref, kw_ref, vw_ref, gate_ref,
                cbias_ref, sbias_ref, wbias_ref, ov_ref, msel_ref, o_ref,
                neg_all, s_a, s_b, e_a, e_b, acc_ref, l_ref, *, nb):
    refs = (q_ref, kc_ref, vc_ref, ks_ref, vs_ref, kw_ref, vw_ref, gate_ref, cbias_ref, sbias_ref, wbias_ref,
            ov_ref, msel_ref)
    blocks = [_nsa_block(u, refs, neg_all.at[u], acc_ref, l_ref, nb) for u in range(NSA_SUB)]
    for phase in range(3):
        for blk in blocks:
            blk[phase]()
    _far_sweep(q_ref, ks_ref, vs_ref, neg_all, (s_a, s_b, e_a, e_b), acc_ref, l_ref, nb)
    fast = [blk[3]() for blk in blocks]
    l_min = fast[0][0]
    for l_u, _ in fast[1:]:
        l_min = jnp.minimum(l_min, l_u)
    outs = lax.cond(jnp.min(l_min) > NSA_TINY, lambda: tuple(o for _, o in fast),
                    lambda: tuple(blk[4]() for blk in blocks))
    for u, blk in enumerate(blocks):
        o_ref[0, u * Q_BLOCK:(u + 1) * Q_BLOCK, :] = blk[5](outs[u]).astype(o_ref.dtype)


def _far_sweep(q_ref, ks_ref, vs_ref, neg_all, stage_refs, acc_ref, l_ref, nb):
    R = NSA_REP
    n_slc = neg_all.shape[1] - NEG_ROWS
    i0 = pl.program_id(2) * NSA_SUB
    n_far = [jnp.maximum(i0 + u - 1, 0) for u in range(NSA_SUB)]
    offs = [0]
    for u in range(NSA_SUB):
        offs.append(offs[-1] + (n_far[u] + FAR_TILES - 1) // FAR_TILES)
    n_tot = offs[-1]

    def locate(c):
        u = sum(jnp.where(c >= offs[k], 1, 0) for k in range(1, NSA_SUB))
        start = sum(jnp.where(u == k, offs[k], 0) for k in range(NSA_SUB))
        far = sum(jnp.where(u == k, n_far[k], 0) for k in range(NSA_SUB))
        tile = FAR_TILES * (c - start)
        return u, jnp.clip(tile, 0, nb - FAR_TILES), far - tile

    def k_rows(J0):
        return ks_ref[0, pl.ds(pl.multiple_of(J0 * Q_BLOCK, Q_BLOCK), FAR_TILES * Q_BLOCK), :]

    def v_cols(J0):
        return jnp.concatenate([vs_ref[0, J0 + t] for t in range(FAR_TILES)], axis=1)

    def q_cols(u):
        return jnp.concatenate([q_ref[r, u] for r in range(R)], axis=1)

    def sel_mask(u, J0, n_valid):
        parts = []
        for t in range(2 * FAR_TILES):
            row = jnp.where(t // 2 < n_valid, 2 * J0 + t, n_slc)
            parts.append(jnp.broadcast_to(neg_all[u, pl.ds(row, 1), :], (SLC_LEN, Q_BLOCK)))
        return jnp.concatenate([jnp.concatenate(parts, axis=0)] * R, axis=1)

    def add_values(c, e_ref):
        u, tile, _ = locate(c)
        acc_ref[u] = acc_ref[u] + _dot(v_cols(tile), e_ref[...])

    def stage(c, s_in, s_out, e_in, e_out):
        u0, tile0, _ = locate(c - 1)
        u1, tile1, valid1 = locate(c)
        u2, tile2, _ = locate(c + 1)
        v_prev, acc_prev = v_cols(tile0), acc_ref[u0]
        mask, l_cur = sel_mask(u1, tile1, valid1), l_ref[u1]
        k_next, q_next = k_rows(tile2), q_cols(u2)
        pv = _dot(v_prev, e_in[...])
        s_out[...] = _dot(k_next, q_next)
        e = jnp.exp2(s_in[...] + mask)
        e_out[...] = e.astype(BF16)
        acc_ref[u0] = acc_prev + pv
        l_ref[u1] = l_cur + jnp.sum(e, axis=0, keepdims=True)

    s_a, s_b, e_a, e_b = stage_refs
    e_b[...] = jnp.zeros_like(e_b)
    u_first, tile_first, _ = locate(0)
    s_a[...] = _dot(k_rows(tile_first), q_cols(u_first))

    def pair(c):
        stage(c, s_a, s_b, e_b, e_a)
        stage(c + 1, s_b, s_a, e_a, e_b)

    def quad(c):
        pair(c)
        pair(c + 2)

    def octet(c):
        quad(c)
        quad(c + 4)

    def long_body(t, carry):
        octet(16 * t)
        octet(16 * t + 8)
        return carry

    n_long = n_tot // 16
    lax.fori_loop(0, n_long, long_body, 0)
    done = 16 * n_long

    @pl.when(n_tot - done >= 8)
    def _():
        octet(done)

    done = done + 8 * ((n_tot - done) // 8)

    @pl.when(n_tot - done >= 4)
    def _():
        quad(done)

    done = done + 4 * ((n_tot - done) // 4)

    @pl.when(n_tot - done >= 2)
    def _():
        pair(done)

    @pl.when(n_tot % 2 == 1)
    def _():
        stage(n_tot - 1, s_a, s_b, e_b, e_a)
        add_values(n_tot - 1, e_a)

    @pl.when((n_tot % 2 == 0) & (n_tot > 0))
    def _():
        add_values(n_tot - 1, e_b)


def _nsa_block(u, refs, neg_ref, acc_ref, l_ref, nb):
    (q_ref, kc_ref, vc_ref, ks_ref, vs_ref, kw_ref, vw_ref, gate_ref, cbias_ref, sbias_ref, wbias_ref,
     ov_ref, msel_ref) = refs
    i = pl.program_id(2) * NSA_SUB + u
    R = NSA_REP
    N = R * Q_BLOCK
    nc = kc_ref.shape[2]
    q_t = jnp.concatenate([q_ref[r, u] for r in range(R)], axis=1)
    n_win = NSA_WINDOW // Q_BLOCK

    def k_rows(ref, J0, n_tiles):
        return ref[0, pl.ds(pl.multiple_of(J0 * Q_BLOCK, Q_BLOCK), n_tiles * Q_BLOCK), :]

    def v_cols(ref, J0, n_tiles):
        return jnp.concatenate([ref[0, J0 + u] for u in range(n_tiles)], axis=1)

    def sel_mask(J0, n_tiles, n_valid=None):
        n_slc = neg_ref.shape[0] - NEG_ROWS
        parts = []
        for u in range(2 * n_tiles):
            row = 2 * J0 + u
            if n_valid is not None:
                row = jnp.where(u // 2 < n_valid, row, n_slc)
            parts.append(jnp.broadcast_to(neg_ref[pl.ds(row, 1), :], (SLC_LEN, Q_BLOCK)))
        return jnp.concatenate([jnp.concatenate(parts, axis=0)] * R, axis=1)

    def compressed(fast):
        cb = cbias_ref[0, pl.ds(pl.multiple_of(8 * (nb - 1) - 8 * i, 8), nc), :]
        s = _dot(kc_ref[0, 0], q_t) + cb
        if fast:
            e = jnp.exp2(s)
        else:
            e = jnp.where(s > 0.5 * NEG_INF, jnp.exp2(s - jnp.max(s, axis=0, keepdims=True)), 0.0)
        l = jnp.sum(e, axis=0, keepdims=True)
        p = e * jnp.where(l > 0.0, 1.0 / l, 0.0)
        return p, l, _dot(vc_ref[0, 0], p.astype(BF16))

    def select_blocks(p_c):
        p_sum = p_c[:, 0:Q_BLOCK]
        for r in range(1, R):
            p_sum = p_sum + p_c[:, r * Q_BLOCK:(r + 1) * Q_BLOCK]
        p_hi = p_sum.astype(BF16)
        p_lo = (p_sum - p_hi.astype(F32)).astype(BF16)
        imp = _dot(ov_ref[...], p_hi) + _dot(ov_ref[...], p_lo)
        n_slc = imp.shape[0]
        jj = lax.broadcasted_iota(jnp.int32, imp.shape, 0)
        qq = lax.broadcasted_iota(jnp.int32, imp.shape, 1)
        cur = 2 * i + jnp.where(qq >= SLC_LEN, 1, 0)
        forced = (jj == 0) | (jj == cur) | (jj == cur - 1)
        score = jnp.where(forced, TAKEN, jnp.where(jj <= cur, imp, -1.0))
        grp = 8
        j_rows = [jj[a:a + grp] for a in range(0, n_slc, grp)]
        for _ in range(min(SLC_TOPN, n_slc) - N_FORCED):
            cand = [(score[a:a + grp], j_rows[a // grp]) for a in range(0, n_slc, grp)]
            while len(cand) > 1:
                nxt = []
                for a in range(0, len(cand) - 1, 2):
                    (s0, j0), (s1, j1) = cand[a], cand[a + 1]
                    right = s1 > s0
                    nxt.append((jnp.maximum(s0, s1), jnp.where(right, j1, j0)))
                cand = nxt + cand[len(cand) - len(cand) % 2:]
            s8, j8 = cand[0]
            best = jnp.max(s8, axis=0, keepdims=True)
            first = jnp.min(jnp.where(s8 == best, j8, n_slc), axis=0, keepdims=True)
            score = jnp.where(jj == first, TAKEN, score)
        neg_ref[:n_slc] = jnp.where(score == TAKEN, msel_ref[0], NEG_INF)
        neg_ref[n_slc:] = jnp.full((NEG_ROWS, Q_BLOCK), NEG_INF, F32)

    def add_chunk(st, s, v_t):
        e = jnp.exp2(s)
        return st[0] + jnp.sum(e, axis=0, keepdims=True), st[1] + _dot(v_t, e.astype(BF16))

    zero = (jnp.zeros((1, N), F32), jnp.zeros((HEAD_DIM, N), F32))

    def selected_near():
        J0 = jnp.maximum(i - 1, 0)
        bias = sbias_ref[0, pl.ds(pl.multiple_of((2 - (i - J0)) * Q_BLOCK, Q_BLOCK), 2 * Q_BLOCK), :]
        return add_chunk(zero, _dot(k_rows(ks_ref, J0, 2), q_t) + bias + sel_mask(J0, 2), v_cols(vs_ref, J0, 2))

    def window_fast():
        J0 = jnp.maximum(i - n_win, 0)
        bias = wbias_ref[0, pl.ds(pl.multiple_of((n_win - (i - J0)) * Q_BLOCK, Q_BLOCK), (n_win + 1) * Q_BLOCK), :]
        return add_chunk(zero, _dot(k_rows(kw_ref, J0, n_win + 1), q_t) + bias, v_cols(vw_ref, J0, n_win + 1))

    def sweep_exact(k_ref, v_ref, bias_ref, far_dist, n_back, with_sel):
        def body(t, st):
            J = i - t
            off = pl.multiple_of((far_dist - jnp.minimum(t, far_dist)) * Q_BLOCK, Q_BLOCK)
            s = _dot(k_rows(k_ref, J, 1), q_t) + bias_ref[0, pl.ds(off, Q_BLOCK), :]
            if with_sel:
                s = s + sel_mask(J, 1)
            return _softmax_tile_update(s, v_ref[0, J], *st)

        init = (jnp.full((1, N), NEG_INF, F32),) + zero
        _, l, acc = lax.fori_loop(0, jnp.minimum(i, n_back) + 1, body, init)
        return l, acc

    vals = {}

    def load_phase():
        vals["cmp"] = compressed(True)
        vals["win"] = window_fast()

    def select_phase():
        select_blocks(vals["cmp"][0])

    def near_phase():
        l_near, acc_near = selected_near()
        l_ref[u] = jnp.broadcast_to(l_near, l_ref.shape[1:])
        acc_ref[u] = acc_near

    def exact_path():
        p_x, _, o_x = compressed(False)
        select_blocks(p_x)
        l_sx, a_sx = sweep_exact(ks_ref, vs_ref, sbias_ref, 2, nb, True)
        l_wx, a_wx = sweep_exact(kw_ref, vw_ref, wbias_ref, n_win, n_win, False)
        return o_x, a_sx * (1.0 / l_sx), a_wx * (1.0 / l_wx)

    def gate_row(br):
        return jnp.concatenate([gate_ref[br, 0, r:r + 1, u * Q_BLOCK:(u + 1) * Q_BLOCK] for r in range(R)], axis=1)

    def fast_result():
        _, l_c, o_c = vals["cmp"]
        l_w, a_w = vals["win"]
        l_s, a_s = l_ref[u, 0:1, :], acc_ref[u]
        lane_q = lax.broadcasted_iota(jnp.int32, (1, N), 1) % Q_BLOCK
        has_cmp = (i > 0) | (lane_q >= CMP_LEN - 1)
        l_min = jnp.minimum(jnp.minimum(l_s, l_w), jnp.where(has_cmp, l_c, 1.0))
        return l_min, (o_c, a_s * (1.0 / l_s), a_w * (1.0 / l_w))

    def combine(outs):
        o_c, o_s, o_w = outs
        o_t = gate_row(0) * o_c + gate_row(1) * o_s + gate_row(2) * o_w
        return _pairs_to_rows(o_t)

    return load_phase, select_phase, near_phase, fast_result, exact_path, combine


def _nsa_attention(colmat, rowmat, cmp_rows, cmp_cols, gates, cbias, sbias, wbias, ov_t, msel, B, S):
    nb = S // Q_BLOCK
    nc = S // CMP_STRIDE
    n_slc = S // SLC_LEN
    G, R = NSA_KV_HEADS, NSA_REP
    kern = functools.partial(_nsa_kernel, nb=nb)
    ns = nb // NSA_SUB
    return pl.pallas_call(
        kern,
        grid=(B, G, ns),
        in_specs=[
            pl.BlockSpec((R, NSA_SUB, HEAD_DIM, LANES), lambda b, g, i: (g, b * ns + i, 0, 0)),
            pl.BlockSpec((1, 1, nc, HEAD_DIM), lambda b, g, i: (g, b, 0, 0)),
            pl.BlockSpec((1, 1, HEAD_DIM, nc), lambda b, g, i: (NSA_KV_HEADS + g, b, 0, 0)),
            pl.BlockSpec((1, S, HEAD_DIM), lambda b, g, i: (ROW_KS + g, b, 0)),
            pl.BlockSpec((1, nb, HEAD_DIM, LANES), lambda b, g, i: (COL_VS + g, b, 0, 0)),
            pl.BlockSpec((1, S, HEAD_DIM), lambda b, g, i: (ROW_KW + g, b, 0)),
            pl.BlockSpec((1, nb, HEAD_DIM, LANES), lambda b, g, i: (COL_VW + g, b, 0, 0)),
            pl.BlockSpec((N_BRANCHES, 1, R, NSA_SUB * LANES), lambda b, g, i: (0, g, 0, b * ns + i)),
            pl.BlockSpec((1, cbias.shape[1], R * Q_BLOCK), lambda b, g, i: (g, 0, 0)),
            pl.BlockSpec((1, sbias.shape[1], R * Q_BLOCK), lambda b, g, i: (g, 0, 0)),
            pl.BlockSpec((1, wbias.shape[1], R * Q_BLOCK), lambda b, g, i: (g, 0, 0)),
            pl.BlockSpec((n_slc, nc), lambda b, g, i: (0, 0)),
            pl.BlockSpec((1, 1, Q_BLOCK), lambda b, g, i: (g, 0, 0)),
        ],
        out_specs=pl.BlockSpec((1, NSA_SUB * Q_BLOCK, R * HEAD_DIM), lambda b, g, i: (b, i, g)),
        out_shape=jax.ShapeDtypeStruct((B, S, NSA_HEADS * HEAD_DIM), BF16),
        scratch_shapes=[pltpu.VMEM((NSA_SUB, n_slc + NEG_ROWS, Q_BLOCK), F32)]
        + [pltpu.VMEM((FAR_TILES * Q_BLOCK, R * Q_BLOCK), F32)] * 2
        + [pltpu.VMEM((FAR_TILES * Q_BLOCK, R * Q_BLOCK), BF16)] * 2
        + [pltpu.VMEM((NSA_SUB, HEAD_DIM, R * Q_BLOCK), F32),
           pltpu.VMEM((NSA_SUB, 8, R * Q_BLOCK), F32)],
        compiler_params=_cparams(("parallel", "parallel", "arbitrary")),
        name="nsa_attention",
    )(colmat, cmp_rows, cmp_cols, rowmat, colmat, rowmat, colmat, gates, cbias, sbias, wbias, ov_t, msel)


SB_QB = 512
SB_KT = 256
SB_ZERO_LOG = -160.0


def _sb_kernel(q_ref, k_ref, v_ref, tri_ref, o_ref):
    i = pl.program_id(1)
    H = SB_HEADS
    tri = tri_ref[...]
    assert SB_QB == 2 * SB_KT
    below_diag = (lax.broadcasted_iota(jnp.int32, (SB_KT, SB_KT), 1)
                  < lax.broadcasted_iota(jnp.int32, (SB_KT, SB_KT), 0))

    def step(J, rows, st, masked):
        out = []
        for h in range(H):
            carry, acc = st[h]
            k_t = jnp.concatenate([k_ref[h, (SB_KT // LANES) * J + c] for c in range(SB_KT // LANES)], axis=1)
            z = _dot(q_ref[h, rows, :], k_t)
            sp = jnp.maximum(z, 0.0) + jnp.log2(1.0 + jnp.exp2(-jnp.abs(z)))
            lf = jnp.where(below_diag, -sp, 0.0) if masked else -sp
            lf_hi = lf.astype(BF16)
            lf_lo = (lf - lf_hi.astype(F32)).astype(BF16)
            after = _dot(lf_hi, tri) + _dot(lf_lo, tri) + carry
            w = jnp.exp2(z - sp + after)
            if masked:
                w = jnp.where(below_diag, w, 0.0)
            v = v_ref[h, pl.ds(pl.multiple_of(J * SB_KT, SB_KT), SB_KT), :]
            acc = acc + _dot(w.astype(BF16), v)
            carry = carry + jnp.sum(lf, axis=1, keepdims=True)
            out.append((carry, acc))
        return tuple(out)

    lo, hi = slice(0, SB_KT), slice(SB_KT, SB_QB)
    zero = tuple((jnp.zeros((SB_KT, 1), F32), jnp.zeros((SB_KT, HEAD_DIM), F32)) for _ in range(H))
    st_hi = step(2 * i, hi, step(2 * i + 1, hi, zero, True), False)
    st_lo = step(2 * i, lo, zero, True)

    def alive(st):
        worst = st[0][0]
        for h in range(1, H):
            worst = jnp.maximum(worst, st[h][0])
        return jnp.max(worst) > SB_ZERO_LOG

    def sweep(rows, st):
        def body(c):
            J, _, st = c
            st = step(J, rows, st, False)
            return J - 1, alive(st), st

        return lax.while_loop(lambda c: (c[0] >= 0) & c[1], body, (2 * i - 1, alive(st), st))[2]

    st_lo, st_hi = sweep(lo, st_lo), sweep(hi, st_hi)
    o_ref[0, lo, :] = jnp.concatenate([st_lo[h][1] for h in range(H)], axis=1).astype(o_ref.dtype)
    o_ref[0, hi, :] = jnp.concatenate([st_hi[h][1] for h in range(H)], axis=1).astype(o_ref.dtype)


def _sb_attention(colmat, rowmat, tri, B, S):
    nq = S // SB_QB
    nb = S // LANES
    H = SB_HEADS
    return pl.pallas_call(
        _sb_kernel,
        grid=(B, nq),
        in_specs=[
            pl.BlockSpec((H, SB_QB, HEAD_DIM), lambda b, i: (ROW_SQ // H, b * nq + i, 0)),
            pl.BlockSpec((H, nb, HEAD_DIM, LANES), lambda b, i: (COL_SK // H, b, 0, 0)),
            pl.BlockSpec((H, S, HEAD_DIM), lambda b, i: (ROW_SV // H, b, 0)),
            pl.BlockSpec((SB_KT, SB_KT), lambda b, i: (0, 0)),
        ],
        out_specs=pl.BlockSpec((1, SB_QB, H * HEAD_DIM), lambda b, i: (b, i, 0)),
        out_shape=jax.ShapeDtypeStruct((B, S, H * HEAD_DIM), BF16),
        compiler_params=_cparams(("parallel", "arbitrary")),
        name="sb_attention",
    )(rowmat, colmat, rowmat, tri)


SWA_SUB = 8


def _swa_kernel(q_ref, k_ref, v_ref, bias_ref, sink_ref, o_ref):
    R = SWA_REP
    sink = sink_ref[0]
    outs = []
    for u in range(SWA_SUB):
        i = pl.program_id(2) * SWA_SUB + u
        q_t = jnp.concatenate([q_ref[r, u] for r in range(R)], axis=1)
        J0 = jnp.maximum(i - 1, 0)
        k = k_ref[0, pl.ds(pl.multiple_of(J0 * Q_BLOCK, Q_BLOCK), 2 * Q_BLOCK), :]
        bias = bias_ref[0, pl.ds(pl.multiple_of((1 - (i - J0)) * Q_BLOCK, Q_BLOCK), 2 * Q_BLOCK), :]
        s = _dot(k, q_t) + bias
        m = jnp.maximum(jnp.max(s, axis=0, keepdims=True), sink)
        e = jnp.exp(s - m)
        den = jnp.sum(e, axis=0, keepdims=True) + jnp.exp(sink - m)
        v_t = jnp.concatenate([v_ref[0, J0], v_ref[0, J0 + 1]], axis=1)
        o_t = _dot(v_t, e.astype(BF16)) * (1.0 / den)
        outs.append(_pairs_to_rows(o_t).astype(o_ref.dtype))
    for u in range(SWA_SUB):
        o_ref[0, u * Q_BLOCK:(u + 1) * Q_BLOCK, :] = outs[u]


def _swa_attention(colmat, rowmat, bias, sinks, B, S):
    nb = S // Q_BLOCK
    ns = nb // SWA_SUB
    G, R = SWA_KV_HEADS, SWA_REP
    return pl.pallas_call(
        _swa_kernel,
        grid=(B, G, ns),
        in_specs=[
            pl.BlockSpec((R, SWA_SUB, HEAD_DIM, LANES), lambda b, g, i: (COL_WQ // R + g, b * ns + i, 0, 0)),
            pl.BlockSpec((1, S, HEAD_DIM), lambda b, g, i: (ROW_WK + g, b, 0)),
            pl.BlockSpec((1, nb, HEAD_DIM, LANES), lambda b, g, i: (COL_WV + g, b, 0, 0)),
            pl.BlockSpec((1, 3 * Q_BLOCK, R * Q_BLOCK), lambda b, g, i: (g, 0, 0)),
            pl.BlockSpec((1, 1, R * Q_BLOCK), lambda b, g, i: (g, 0, 0)),
        ],
        out_specs=pl.BlockSpec((1, SWA_SUB * Q_BLOCK, R * HEAD_DIM), lambda b, g, i: (b, i, g)),
        out_shape=jax.ShapeDtypeStruct((B, S, SWA_HEADS * HEAD_DIM), BF16),
        compiler_params=_cparams(("parallel", "parallel", "arbitrary")),
        name="swa_attention",
    )(colmat, rowmat, colmat, bias, sinks)


MERGE_TM = 512


def _merge_kernel(x_ref, nw_ref, wg_ref, ya_ref, yb_ref, yc_ref, ua_ref, ub_ref, uc_ref, wo_ref, o_ref):
    x = x_ref[...]
    h = _rms_rows(x, nw_ref[...]).astype(BF16)
    merged = None
    for br, (y_ref, u_ref) in enumerate(((ya_ref, ua_ref), (yb_ref, ub_ref), (yc_ref, uc_ref))):
        g = jax.nn.sigmoid(_dot(h, wg_ref[:, br * D_MODEL:(br + 1) * D_MODEL]))
        t = g * _dot(y_ref[...], u_ref[...])
        merged = t if merged is None else merged + t
    o_ref[...] = x + _dot(merged.astype(BF16), wo_ref[...])


def _merge(x2, nw, w_bgate, y_nsa, y_sb, y_swa, u_nsa, u_sb, u_swa, w_out, l):
    T = x2.shape[0]
    full = lambda w: _layer_spec(l, w.shape[1:])
    rows = lambda n: pl.BlockSpec((MERGE_TM, n), lambda i: (i, 0))
    return pl.pallas_call(
        _merge_kernel,
        grid=(T // MERGE_TM,),
        in_specs=[rows(D_MODEL), pl.BlockSpec((1, D_MODEL), lambda i: (0, 0)), full(w_bgate),
                  rows(y_nsa.shape[1]), rows(y_sb.shape[1]), rows(y_swa.shape[1]),
                  full(u_nsa), full(u_sb), full(u_swa), full(w_out)],
        out_specs=rows(D_MODEL),
        out_shape=jax.ShapeDtypeStruct((T, D_MODEL), F32),
        compiler_params=_cparams(("parallel",)),
        name="merge",
    )(x2, nw.reshape(1, D_MODEL), w_bgate, y_nsa, y_sb, y_swa, u_nsa, u_sb, u_swa, w_out)


def _t5_bucket(dist):
    max_exact = NUM_BUCKETS // 2
    d = jnp.maximum(dist, 0)
    df = jnp.maximum(d, 1).astype(F32)
    large = max_exact + (jnp.log(df / max_exact) / math.log(MAX_DISTANCE / max_exact)
                         * (NUM_BUCKETS - max_exact)).astype(jnp.int32)
    large = jnp.minimum(large, NUM_BUCKETS - 1)
    return jnp.where(d < max_exact, d, large)


def _toeplitz(ext, delta):
    H = ext.shape[0]
    n = Q_BLOCK
    c = delta + MAX_DISTANCE
    w = ext[:, c - (n - 1):c + n]
    flat = jnp.pad(jnp.broadcast_to(w[:, None, :], (H, n, 2 * n - 1)), ((0, 0), (0, 0), (0, 1))).reshape(H, 2 * n * n)
    return flat[:, n - 1:n - 1 + n * (2 * n - 1)].reshape(H, n, 2 * n - 1)[:, :, :n]


def _lanes(t, G, R):
    K, Q = t.shape[1:]
    return t.reshape(G, R, K, Q).transpose(0, 2, 1, 3).reshape(G, K, R * Q)


def _bias_tables(rel_bias, nb, qk_bound):
    n = Q_BLOCK
    G, R = NSA_KV_HEADS, NSA_REP
    by_dist = rel_bias[_t5_bucket(jnp.arange(MAX_DISTANCE)), :].T.astype(F32)
    far = rel_bias[NUM_BUCKETS - 1, :].astype(F32)
    neg = jnp.full((1, n), NEG_INF, F32)
    k = jnp.arange(n)[:, None]
    q = jnp.arange(n)[None, :]
    above = (k > q)[None]

    rel = (by_dist[:NSA_HEADS] - far[:NSA_HEADS, None]) * LOG2E
    m_g = qk_bound + jnp.maximum(jnp.max(rel.reshape(G, -1), axis=1), 0.0)
    m_h = jnp.repeat(m_g, R)[:, None, None]
    ext = jnp.concatenate([jnp.broadcast_to(neg, (NSA_HEADS, n)), rel, jnp.zeros((NSA_HEADS, n), F32)], axis=1)
    t0, t1 = _toeplitz(ext, 0), _toeplitz(ext, n)
    zeros = jnp.zeros_like(t0)
    masked = jnp.full_like(t0, NEG_INF)
    sbias = _lanes(jnp.concatenate([zeros, t1, t0, masked], axis=1), G, R)
    n_win = NSA_WINDOW // n
    edge = jnp.where(above, zeros, NEG_INF)
    tiles = [edge] + [zeros] * (n_win - 2) + [t1, t0]
    wbias = _lanes(jnp.concatenate([t - m_h for t in tiles] + [masked] * n_win, axis=1), G, R)
    off = 8 * (nb - 1)
    near = jnp.stack([ext[:, 97 - CMP_STRIDE * m:97 - CMP_STRIDE * m + n] for m in range(-9, 7)], axis=1)
    n_rows = 16 * nb - 8
    cb = jnp.concatenate([jnp.zeros((NSA_HEADS, off - 9, n), F32), near,
                          jnp.full((NSA_HEADS, n_rows - off - 7, n), NEG_INF, F32)], axis=1) - m_h
    cbias = _lanes(cb, G, R)
    msel = jnp.broadcast_to(-m_g[:, None, None], (G, 1, n))

    hs = slice(NSA_HEADS, NSA_HEADS + SWA_HEADS)
    ext_w = jnp.concatenate([jnp.broadcast_to(neg, (SWA_HEADS, n)), by_dist[hs],
                             jnp.broadcast_to(far[hs, None], (SWA_HEADS, n))], axis=1)
    s0 = _toeplitz(ext_w, 0)
    s1 = jnp.where(above, _toeplitz(ext_w, n), NEG_INF)
    swa_bias = _lanes(jnp.concatenate([s1, s0, jnp.full_like(s0, NEG_INF)], axis=1), SWA_KV_HEADS, SWA_REP)
    return cbias, sbias, wbias, msel, swa_bias


def _overlap_t(S):
    nc, n_slc = S // CMP_STRIDE, S // SLC_LEN
    n = np.arange(nc)[None, :]
    j = np.arange(n_slc)[:, None]
    ov = (n * CMP_STRIDE < (j + 1) * SLC_LEN) & (n * CMP_STRIDE + CMP_LEN - 1 >= j * SLC_LEN) & (n < nc - 1)
    return jnp.asarray(ov, BF16)


def _head_gains(heads, norm_of, scale_of):
    ones = jnp.ones((HEAD_DIM,), F32)
    gains = [norm_of[name].astype(F32) * scale_of.get(name, 1.0) if name in norm_of else ones for name, _ in heads]
    return jnp.concatenate(gains).reshape(1, -1)


def kernel(x, rel_bias, ffn1_norm, ffn1_w_gate, ffn1_w_up, ffn1_w_down, mix_norm, w_in, nsa_q_norm, nsa_k_norm, nsa_cmp_pos, nsa_cmp_k_w1, nsa_cmp_k_w2, nsa_cmp_v_w1, nsa_cmp_v_w2, swa_q_norm, swa_k_norm, swa_sinks, w_up_nsa, w_up_sb, w_up_swa, w_out, ffn2_norm, ffn2_w_gate, ffn2_w_up, ffn2_w_down):
    B, S, D = x.shape
    T = B * S
    nb = S // Q_BLOCK
    nc = S // CMP_STRIDE
    depth = w_in.shape[0]
    scale = HEAD_DIM ** -0.5
    scale_of = {"nq": scale * LOG2E, "sq": scale * LOG2E, "wq": scale}
    qk_bound = (BOUND_SLACK * LOG2E * HEAD_DIM ** 0.5
                * jnp.max(jnp.abs(nsa_q_norm.astype(F32))) * jnp.max(jnp.abs(nsa_k_norm.astype(F32))))
    cbias, sbias, wbias, msel, swa_bias = _bias_tables(rel_bias, nb, qk_bound)
    plan = _proj_plan(scale_of)
    ov_t = _overlap_t(S)
    tri = jnp.asarray(np.tril(np.ones((SB_KT, SB_KT), np.float32), -1), BF16)

    (ffn1_w_gate, ffn1_w_up, ffn1_w_down, ffn2_w_gate, ffn2_w_up, ffn2_w_down, w_up_nsa, w_up_sb, w_up_swa,
     w_out) = (w.astype(BF16) for w in (ffn1_w_gate, ffn1_w_up, ffn1_w_down, ffn2_w_gate, ffn2_w_up, ffn2_w_down,
                                        w_up_nsa, w_up_sb, w_up_swa, w_out))
    tail0 = OFF_NGATE + N_GATE
    w_head, w_tail = w_in[:, :, :OFF_NGATE].astype(BF16), w_in[:, :, tail0:].astype(BF16)
    w_bgate = w_tail[:, :, OFF_BGATE - tail0:]
    w_gate = jnp.pad(w_in[:, :, OFF_NGATE:tail0], ((0, 0), (0, 0), (0, LANES - N_GATE))).astype(BF16)
    runs = []
    for _, off in ROW_HEADS + COL_HEADS:
        if runs and runs[-1][1] == off:
            runs[-1][1] = off + HEAD_DIM
        else:
            runs.append([off, off + HEAD_DIM])
    pieces = [w_head[:, :, a:b] if a < OFF_NGATE else w_tail[:, :, a - tail0:b - tail0] for a, b in runs]
    w_proj = jnp.concatenate(pieces + [w_gate], axis=2)

    x2 = x.reshape(T, D)
    for l in range(depth):
        x2 = _ffn(x2, ffn1_norm[l], ffn1_w_gate, ffn1_w_up, ffn1_w_down, l)

        norm_of = {"nq": nsa_q_norm[l], "nks": nsa_k_norm[l], "nkw": nsa_k_norm[l],
                   "wq": swa_q_norm[l], "wk": swa_k_norm[l]}
        gain = jnp.concatenate([_head_gains(ROW_HEADS, norm_of, scale_of), _head_gains(COL_HEADS, norm_of, scale_of),
                                jnp.ones((1, LANES), F32)], axis=1)
        rowmat, colmat, gates = _proj(x2, mix_norm[l], w_proj, gain, plan, l)
        gates = gates[:N_GATE].reshape(N_BRANCHES, NSA_KV_HEADS, NSA_REP, T)

        rows4 = rowmat[ROW_KC:ROW_KC + 2 * NSA_KV_HEADS].reshape(2 * NSA_KV_HEADS, B, nc, CMP_STRIDE * HEAD_DIM)
        cmp_rows, cmp_cols = _compress(rows4, jnp.stack([nsa_cmp_k_w1[l], nsa_cmp_v_w1[l]]),
                                       jnp.stack([nsa_cmp_k_w2[l], nsa_cmp_v_w2[l]]),
                                       nsa_cmp_pos[l], nsa_k_norm[l])

        y_nsa = _nsa_attention(colmat, rowmat, cmp_rows, cmp_cols, gates, cbias, sbias, wbias, ov_t, msel, B, S)
        y_sb = _sb_attention(colmat, rowmat, tri, B, S)
        sinks = jnp.repeat(swa_sinks[l].astype(F32).reshape(SWA_KV_HEADS, 1, SWA_REP), Q_BLOCK, axis=2)
        y_swa = _swa_attention(colmat, rowmat, swa_bias, sinks, B, S)

        x2 = _merge(x2, mix_norm[l], w_bgate, y_nsa.reshape(T, -1), y_sb.reshape(T, -1),
                    y_swa.reshape(T, -1), w_up_nsa, w_up_sb, w_up_swa, w_out, l)
        x2 = _ffn(x2, ffn2_norm[l], ffn2_w_gate, ffn2_w_up, ffn2_w_down, l)
    return x2.reshape(B, S, D)
```

```python
import functools
import math

import numpy as np
import jax
import jax.numpy as jnp
from jax import lax
from jax.experimental import pallas as pl
from jax.experimental.pallas import tpu as pltpu

D_MODEL = 1024
HEAD_DIM = 64
Q_BLOCK = 128
NSA_HEADS = 8
NSA_KV_HEADS = 2
NSA_REP = NSA_HEADS // NSA_KV_HEADS
CMP_LEN = 32
CMP_STRIDE = 16
CMP_HIDDEN = 256
SLC_LEN = 64
SLC_TOPN = 16
NSA_WINDOW = 512
SB_HEADS = 4
SWA_HEADS = 4
SWA_KV_HEADS = 2
SWA_REP = SWA_HEADS // SWA_KV_HEADS
SWA_WINDOW = 128
NUM_BUCKETS = 32
MAX_DISTANCE = 128
D_FF = 2816
NORM_EPS = 1e-6
NEG_INF = -1e30
N_BRANCHES = 3
N_GATE = 3 * NSA_HEADS

LANES = 128
VMEM_LIMIT = 56 * 1024 * 1024
BF16 = jnp.bfloat16
F32 = jnp.float32

_sizes = (NSA_HEADS * HEAD_DIM,) + (NSA_KV_HEADS * HEAD_DIM,) * 6 + (N_GATE,) \
    + (SB_HEADS * HEAD_DIM,) * 3 + (SWA_HEADS * HEAD_DIM, SWA_KV_HEADS * HEAD_DIM,
                                    SWA_KV_HEADS * HEAD_DIM, N_BRANCHES * D_MODEL)
_offs = np.concatenate([[0], np.cumsum(_sizes)])
(OFF_NQ, OFF_NKC, OFF_NVC, OFF_NKS, OFF_NVS, OFF_NKW, OFF_NVW, OFF_NGATE,
 OFF_SQ, OFF_SK, OFF_SV, OFF_WQ, OFF_WK, OFF_WV, OFF_BGATE) = (int(o) for o in _offs[:-1])

ROW_HEADS = ([("nkc", OFF_NKC + 64 * g) for g in range(2)] + [("nvc", OFF_NVC + 64 * g) for g in range(2)]
             + [("nks", OFF_NKS + 64 * g) for g in range(2)] + [("nkw", OFF_NKW + 64 * g) for g in range(2)]
             + [("sq", OFF_SQ + 64 * h) for h in range(4)] + [("sv", OFF_SV + 64 * h) for h in range(4)]
             + [("wk", OFF_WK + 64 * g) for g in range(2)])
ROW_KC, ROW_VC, ROW_KS, ROW_KW, ROW_SQ, ROW_SV, ROW_WK = 0, 2, 4, 6, 8, 12, 16
COL_HEADS = ([("nq", OFF_NQ + 64 * h) for h in range(8)] + [("nvs", OFF_NVS + 64 * g) for g in range(2)]
             + [("nvw", OFF_NVW + 64 * g) for g in range(2)] + [("sk", OFF_SK + 64 * h) for h in range(4)]
             + [("wq", OFF_WQ + 64 * h) for h in range(4)] + [("wv", OFF_WV + 64 * g) for g in range(2)])
COL_NQ, COL_VS, COL_VW, COL_SK, COL_WQ, COL_WV = 0, 8, 10, 12, 16, 20


def _dot(a, b):
    return jnp.dot(a, b, preferred_element_type=F32)


def _rms_rows(x, w):
    ms = jnp.mean(x * x, axis=-1, keepdims=True)
    return x * lax.rsqrt(ms + NORM_EPS) * w


def _cparams(sem):
    return pltpu.CompilerParams(dimension_semantics=sem, vmem_limit_bytes=VMEM_LIMIT)


FFN_TM = 1024
MXU_TILE = 256
FFN_FC = MXU_TILE


def _ffn_kernel(x_ref, nw_ref, wg_ref, wu_ref, wd_ref, o_ref):
    x = x_ref[...]
    h = _rms_rows(x, nw_ref[...]).astype(BF16)
    acc = jnp.zeros(x.shape, F32)
    for c in range(D_FF // FFN_FC):
        cols = slice(c * FFN_FC, (c + 1) * FFN_FC)
        g = _dot(h, wg_ref[:, cols])
        u = _dot(h, wu_ref[:, cols])
        a = (g * jax.nn.sigmoid(g)) * u
        acc = acc + _dot(a.astype(BF16), wd_ref[cols, :])
    o_ref[...] = x + 0.5 * acc


def _layer_spec(l, shape, **kw):
    return pl.BlockSpec((None,) + tuple(shape), lambda i: (l, 0, 0), **kw)


def _ffn(x2, nw, wg, wu, wd, l):
    T = x2.shape[0]
    resident = lambda shape: _layer_spec(l, shape, pipeline_mode=pl.Buffered(1))
    return pl.pallas_call(
        _ffn_kernel,
        grid=(T // FFN_TM,),
        in_specs=[
            pl.BlockSpec((FFN_TM, D_MODEL), lambda i: (i, 0)),
            pl.BlockSpec((1, D_MODEL), lambda i: (0, 0)),
            resident((D_MODEL, D_FF)), resident((D_MODEL, D_FF)), resident((D_FF, D_MODEL)),
        ],
        out_specs=pl.BlockSpec((FFN_TM, D_MODEL), lambda i: (i, 0)),
        out_shape=jax.ShapeDtypeStruct((T, D_MODEL), F32),
        compiler_params=_cparams(("parallel",)),
        name="ffn",
    )(x2, nw.reshape(1, D_MODEL), wg, wu, wd)


PROJ_TM = 512


GATE_ROWS = 32

_NORMED = ("nq", "nks", "nkw", "wq", "wk")


def _proj_plan(scale_of):
    plan = []
    for kind, heads in (("row", ROW_HEADS), ("col", COL_HEADS)):
        for p in range(len(heads) // 2):
            name = heads[2 * p][0]
            assert heads[2 * p + 1][0] == name
            plan.append((kind, p, name in _NORMED, 1.0 if name in _NORMED else scale_of.get(name, 1.0)))
    plan.append(("gate", 0, False, 1.0))
    return tuple(plan)


def _proj_kernel(x_ref, nw_ref, w_ref, gain_ref, bd_ref, row_ref, col_ref, gate_ref, *, plan):
    h = _rms_rows(x_ref[...], nw_ref[...]).astype(BF16)
    pair = 2 * MXU_TILE
    for c0 in range(0, len(plan) * LANES, pair):
        width = min(pair, len(plan) * LANES - c0)
        zg = _dot(h, w_ref[:, c0:c0 + width])
        for u in range(width // LANES):
            t = c0 // LANES + u
            kind, p, normed, scale = plan[t]
            z = zg[:, u * LANES:(u + 1) * LANES]
            if normed:
                ms = _dot((z * z).astype(BF16), bd_ref[...])
                z = z * lax.rsqrt(ms + NORM_EPS) * gain_ref[:, t * LANES:(t + 1) * LANES]
            elif scale != 1.0:
                z = z * scale
            if kind == "row":
                row_ref[2 * p] = z[:, :HEAD_DIM].astype(row_ref.dtype)
                row_ref[2 * p + 1] = z[:, HEAD_DIM:].astype(row_ref.dtype)
            elif kind == "col":
                z_t = z.T
                for c in range(PROJ_TM // LANES):
                    col_ref[2 * p, c] = z_t[:HEAD_DIM, c * LANES:(c + 1) * LANES].astype(col_ref.dtype)
                    col_ref[2 * p + 1, c] = z_t[HEAD_DIM:, c * LANES:(c + 1) * LANES].astype(col_ref.dtype)
            else:
                gate_ref[...] = jax.nn.sigmoid(z).T[:GATE_ROWS]


def _proj(x2, nw, w, gain, plan, l):
    T = x2.shape[0]
    n_row, n_col = len(ROW_HEADS), len(COL_HEADS)
    bd = np.kron(np.eye(2, dtype=np.float32), np.full((HEAD_DIM, HEAD_DIM), 1.0 / HEAD_DIM, np.float32))
    return pl.pallas_call(
        functools.partial(_proj_kernel, plan=plan),
        grid=(T // PROJ_TM,),
        in_specs=[
            pl.BlockSpec((PROJ_TM, D_MODEL), lambda i: (i, 0)),
            pl.BlockSpec((1, D_MODEL), lambda i: (0, 0)),
            _layer_spec(l, w.shape[1:]),
            pl.BlockSpec(gain.shape, lambda i: (0, 0)),
            pl.BlockSpec((LANES, LANES), lambda i: (0, 0)),
        ],
        out_specs=[
            pl.BlockSpec((n_row, PROJ_TM, HEAD_DIM), lambda i: (0, i, 0)),
            pl.BlockSpec((n_col, PROJ_TM // LANES, HEAD_DIM, LANES), lambda i: (0, i, 0, 0)),
            pl.BlockSpec((GATE_ROWS, PROJ_TM), lambda i: (0, i)),
        ],
        out_shape=[jax.ShapeDtypeStruct((n_row, T, HEAD_DIM), BF16),
                   jax.ShapeDtypeStruct((n_col, T // LANES, HEAD_DIM, LANES), BF16),
                   jax.ShapeDtypeStruct((GATE_ROWS, T), F32)],
        compiler_params=_cparams(("parallel",)),
        name="proj",
    )(x2, nw.reshape(1, D_MODEL), w, gain, jnp.asarray(bd, BF16))


def _compress_kernel(r_ref, w1_ref, w2_ref, pos_ref, kn_ref, row_ref, col_ref):
    kind = pl.program_id(0) // NSA_KV_HEADS
    r = r_ref[0, 0]
    nc = r.shape[0]
    half = CMP_STRIDE * HEAD_DIM
    w1 = w1_ref[0]
    p_lo = _dot(r, w1[:half])
    p_hi = _dot(r, w1[half:])
    p_pos = _dot(jnp.broadcast_to(pos_ref[...], (8, CMP_LEN * HEAD_DIM)).astype(BF16), w1)[:1]
    hdn = p_lo + pltpu.roll(p_hi, nc - 1, 0) + p_pos
    hdn = hdn * jax.nn.sigmoid(hdn)
    out = _dot(hdn.astype(BF16), w2_ref[0])
    ms = jnp.sum(out * out, axis=-1, keepdims=True) * (1.0 / HEAD_DIM)
    normed = out * lax.rsqrt(ms + NORM_EPS) * kn_ref[...]
    out = jnp.where(kind == 0, normed, out)
    row_ref[0, 0] = out[:, :HEAD_DIM].astype(row_ref.dtype)
    col_ref[0, 0] = out.T[:HEAD_DIM].astype(col_ref.dtype)


def _compress(rows4, w1, w2, pos, k_norm):
    assert ROW_KC == 0 and ROW_VC == NSA_KV_HEADS
    _, B, NC, _ = rows4.shape
    kn = jnp.pad(k_norm.reshape(1, HEAD_DIM), ((0, 0), (0, LANES - HEAD_DIM)))
    w2p = jnp.pad(w2, ((0, 0), (0, 0), (0, LANES - HEAD_DIM))).astype(BF16)
    return pl.pallas_call(
        _compress_kernel,
        grid=(4, B),
        in_specs=[
            pl.BlockSpec((1, 1, NC, CMP_STRIDE * HEAD_DIM), lambda h, b: (h, b, 0, 0)),
            pl.BlockSpec((1, CMP_LEN * HEAD_DIM, CMP_HIDDEN), lambda h, b: (h // NSA_KV_HEADS, 0, 0)),
            pl.BlockSpec((1, CMP_HIDDEN, LANES), lambda h, b: (h // NSA_KV_HEADS, 0, 0)),
            pl.BlockSpec((1, CMP_LEN * HEAD_DIM), lambda h, b: (0, 0)),
            pl.BlockSpec((1, LANES), lambda h, b: (0, 0)),
        ],
        out_specs=[
            pl.BlockSpec((1, 1, NC, HEAD_DIM), lambda h, b: (h, b, 0, 0)),
            pl.BlockSpec((1, 1, HEAD_DIM, NC), lambda h, b: (h, b, 0, 0)),
        ],
        out_shape=[jax.ShapeDtypeStruct((4, B, NC, HEAD_DIM), BF16),
                   jax.ShapeDtypeStruct((4, B, HEAD_DIM, NC), BF16)],
        compiler_params=_cparams(("parallel", "parallel")),
        name="nsa_compress",
    )(rows4, w1.astype(BF16), w2p, pos.reshape(1, CMP_LEN * HEAD_DIM), kn)


def _softmax_tile_update(s, v_t, m, l, acc):
    m_new = jnp.maximum(m, jnp.max(s, axis=0, keepdims=True))
    alpha = jnp.exp2(m - m_new)
    p = jnp.exp2(s - m_new)
    l_new = alpha * l + jnp.sum(p, axis=0, keepdims=True)
    acc_new = alpha * acc + _dot(v_t, p.astype(BF16))
    return m_new, l_new, acc_new


def _pairs_to_rows(o_t):
    n = o_t.shape[1] // LANES
    outs = []
    for p in range(n // 2):
        a = o_t[:, (2 * p) * LANES:(2 * p + 1) * LANES]
        b = o_t[:, (2 * p + 1) * LANES:(2 * p + 2) * LANES]
        outs.append(jnp.concatenate([a, b], axis=0).T)
    return outs[0] if len(outs) == 1 else jnp.concatenate(outs, axis=1)


FAR_TILES = 4
NSA_SUB = 8
TAKEN = -2.0
NEG_ROWS = 8
N_FORCED = 3
LOG2E = math.log2(math.e)
NSA_TINY = 2.0 ** -100
BOUND_SLACK = 1.05


def _nsa_kernel(q_ref, kc_ref, vc_ref, ks_ref, vs_ref, kw_ref, vw_ref, gate_ref,
                cbias_ref, sbias_ref, wbias_ref, ov_ref, msel_ref, o_ref,
                neg_all, s_a, s_b, e_a, e_b, acc_ref, l_ref, *, nb):
    refs = (q_ref, kc_ref, vc_ref, ks_ref, vs_ref, kw_ref, vw_ref, gate_ref, cbias_ref, sbias_ref, wbias_ref,
            ov_ref, msel_ref)
    blocks = [_nsa_block(u, refs, neg_all.at[u], acc_ref, l_ref, nb) for u in range(NSA_SUB)]
    for phase in range(3):
        for blk in blocks:
            blk[phase]()
    _far_sweep(q_ref, ks_ref, vs_ref, neg_all, (s_a, s_b, e_a, e_b), acc_ref, l_ref, nb)
    fast = [blk[3]() for blk in blocks]
    l_min = fast[0][0]
    for l_u, _ in fast[1:]:
        l_min = jnp.minimum(l_min, l_u)
    outs = lax.cond(jnp.min(l_min) > NSA_TINY, lambda: tuple(o for _, o in fast),
                    lambda: tuple(blk[4]() for blk in blocks))
    for u, blk in enumerate(blocks):
        o_ref[0, u * Q_BLOCK:(u + 1) * Q_BLOCK, :] = blk[5](outs[u]).astype(o_ref.dtype)


def _far_sweep(q_ref, ks_ref, vs_ref, neg_all, stage_refs, acc_ref, l_ref, nb):
    R = NSA_REP
    n_slc = neg_all.shape[1] - NEG_ROWS
    i0 = pl.program_id(2) * NSA_SUB
    n_far = [jnp.maximum(i0 + u - 1, 0) for u in range(NSA_SUB)]
    offs = [0]
    for u in range(NSA_SUB):
        offs.append(offs[-1] + (n_far[u] + FAR_TILES - 1) // FAR_TILES)
    n_tot = offs[-1]

    def locate(c):
        u = sum(jnp.where(c >= offs[k], 1, 0) for k in range(1, NSA_SUB))
        start = sum(jnp.where(u == k, offs[k], 0) for k in range(NSA_SUB))
        far = sum(jnp.where(u == k, n_far[k], 0) for k in range(NSA_SUB))
        tile = FAR_TILES * (c - start)
        return u, jnp.clip(tile, 0, nb - FAR_TILES), far - tile

    def k_rows(J0):
        return ks_ref[0, pl.ds(pl.multiple_of(J0 * Q_BLOCK, Q_BLOCK), FAR_TILES * Q_BLOCK), :]

    def v_cols(J0):
        return jnp.concatenate([vs_ref[0, J0 + t] for t in range(FAR_TILES)], axis=1)

    def q_cols(u):
        return jnp.concatenate([q_ref[r, u] for r in range(R)], axis=1)

    def sel_mask(u, J0, n_valid):
        parts = []
        for t in range(2 * FAR_TILES):
            row = jnp.where(t // 2 < n_valid, 2 * J0 + t, n_slc)
            parts.append(jnp.broadcast_to(neg_all[u, pl.ds(row, 1), :], (SLC_LEN, Q_BLOCK)))
        return jnp.concatenate([jnp.concatenate(parts, axis=0)] * R, axis=1)

    def add_values(c, e_ref):
        u, tile, _ = locate(c)
        acc_ref[u] = acc_ref[u] + _dot(v_cols(tile), e_ref[...])

    def stage(c, s_in, s_out, e_in, e_out):
        u0, tile0, _ = locate(c - 1)
        u1, tile1, valid1 = locate(c)
        u2, tile2, _ = locate(c + 1)
        v_prev, acc_prev = v_cols(tile0), acc_ref[u0]
        mask, l_cur = sel_mask(u1, tile1, valid1), l_ref[u1]
        k_next, q_next = k_rows(tile2), q_cols(u2)
        pv = _dot(v_prev, e_in[...])
        s_out[...] = _dot(k_next, q_next)
        e = jnp.exp2(s_in[...] + mask)
        e_out[...] = e.astype(BF16)
        acc_ref[u0] = acc_prev + pv
        l_ref[u1] = l_cur + jnp.sum(e, axis=0, keepdims=True)

    s_a, s_b, e_a, e_b = stage_refs
    e_b[...] = jnp.zeros_like(e_b)
    u_first, tile_first, _ = locate(0)
    s_a[...] = _dot(k_rows(tile_first), q_cols(u_first))

    def pair(c):
        stage(c, s_a, s_b, e_b, e_a)
        stage(c + 1, s_b, s_a, e_a, e_b)

    def quad(c):
        pair(c)
        pair(c + 2)

    def octet(c):
        quad(c)
        quad(c + 4)

    def long_body(t, carry):
        octet(16 * t)
        octet(16 * t + 8)
        return carry

    n_long = n_tot // 16
    lax.fori_loop(0, n_long, long_body, 0)
    done = 16 * n_long

    @pl.when(n_tot - done >= 8)
    def _():
        octet(done)

    done = done + 8 * ((n_tot - done) // 8)

    @pl.when(n_tot - done >= 4)
    def _():
        quad(done)

    done = done + 4 * ((n_tot - done) // 4)

    @pl.when(n_tot - done >= 2)
    def _():
        pair(done)

    @pl.when(n_tot % 2 == 1)
    def _():
        stage(n_tot - 1, s_a, s_b, e_b, e_a)
        add_values(n_tot - 1, e_a)

    @pl.when((n_tot % 2 == 0) & (n_tot > 0))
    def _():
        add_values(n_tot - 1, e_b)


def _nsa_block(u, refs, neg_ref, acc_ref, l_ref, nb):
    (q_ref, kc_ref, vc_ref, ks_ref, vs_ref, kw_ref, vw_ref, gate_ref, cbias_ref, sbias_ref, wbias_ref,
     ov_ref, msel_ref) = refs
    i = pl.program_id(2) * NSA_SUB + u
    R = NSA_REP
    N = R * Q_BLOCK
    nc = kc_ref.shape[2]
    q_t = jnp.concatenate([q_ref[r, u] for r in range(R)], axis=1)
    n_win = NSA_WINDOW // Q_BLOCK

    def k_rows(ref, J0, n_tiles):
        return ref[0, pl.ds(pl.multiple_of(J0 * Q_BLOCK, Q_BLOCK), n_tiles * Q_BLOCK), :]

    def v_cols(ref, J0, n_tiles):
        return jnp.concatenate([ref[0, J0 + u] for u in range(n_tiles)], axis=1)

    def sel_mask(J0, n_tiles, n_valid=None):
        n_slc = neg_ref.shape[0] - NEG_ROWS
        parts = []
        for u in range(2 * n_tiles):
            row = 2 * J0 + u
            if n_valid is not None:
                row = jnp.where(u // 2 < n_valid, row, n_slc)
            parts.append(jnp.broadcast_to(neg_ref[pl.ds(row, 1), :], (SLC_LEN, Q_BLOCK)))
        return jnp.concatenate([jnp.concatenate(parts, axis=0)] * R, axis=1)

    def compressed(fast):
        cb = cbias_ref[0, pl.ds(pl.multiple_of(8 * (nb - 1) - 8 * i, 8), nc), :]
        s = _dot(kc_ref[0, 0], q_t) + cb
        if fast:
            e = jnp.exp2(s)
        else:
            e = jnp.where(s > 0.5 * NEG_INF, jnp.exp2(s - jnp.max(s, axis=0, keepdims=True)), 0.0)
        l = jnp.sum(e, axis=0, keepdims=True)
        p = e * jnp.where(l > 0.0, 1.0 / l, 0.0)
        return p, l, _dot(vc_ref[0, 0], p.astype(BF16))

    def select_blocks(p_c):
        p_sum = p_c[:, 0:Q_BLOCK]
        for r in range(1, R):
            p_sum = p_sum + p_c[:, r * Q_BLOCK:(r + 1) * Q_BLOCK]
        p_hi = p_sum.astype(BF16)
        p_lo = (p_sum - p_hi.astype(F32)).astype(BF16)
        imp = _dot(ov_ref[...], p_hi) + _dot(ov_ref[...], p_lo)
        n_slc = imp.shape[0]
        jj = lax.broadcasted_iota(jnp.int32, imp.shape, 0)
        qq = lax.broadcasted_iota(jnp.int32, imp.shape, 1)
        cur = 2 * i + jnp.where(qq >= SLC_LEN, 1, 0)
        forced = (jj == 0) | (jj == cur) | (jj == cur - 1)
        score = jnp.where(forced, TAKEN, jnp.where(jj <= cur, imp, -1.0))
        grp = 8
        j_rows = [jj[a:a + grp] for a in range(0, n_slc, grp)]
        for _ in range(min(SLC_TOPN, n_slc) - N_FORCED):
            cand = [(score[a:a + grp], j_rows[a // grp]) for a in range(0, n_slc, grp)]
            while len(cand) > 1:
                nxt = []
                for a in range(0, len(cand) - 1, 2):
                    (s0, j0), (s1, j1) = cand[a], cand[a + 1]
                    right = s1 > s0
                    nxt.append((jnp.maximum(s0, s1), jnp.where(right, j1, j0)))
                cand = nxt + cand[len(cand) - len(cand) % 2:]
            s8, j8 = cand[0]
            best = jnp.max(s8, axis=0, keepdims=True)
            first = jnp.min(jnp.where(s8 == best, j8, n_slc), axis=0, keepdims=True)
            score = jnp.where(jj == first, TAKEN, score)
        neg_ref[:n_slc] = jnp.where(score == TAKEN, msel_ref[0], NEG_INF)
        neg_ref[n_slc:] = jnp.full((NEG_ROWS, Q_BLOCK), NEG_INF, F32)

    def add_chunk(st, s, v_t):
        e = jnp.exp2(s)
        return st[0] + jnp.sum(e, axis=0, keepdims=True), st[1] + _dot(v_t, e.astype(BF16))

    zero = (jnp.zeros((1, N), F32), jnp.zeros((HEAD_DIM, N), F32))

    def selected_near():
        J0 = jnp.maximum(i - 1, 0)
        bias = sbias_ref[0, pl.ds(pl.multiple_of((2 - (i - J0)) * Q_BLOCK, Q_BLOCK), 2 * Q_BLOCK), :]
        return add_chunk(zero, _dot(k_rows(ks_ref, J0, 2), q_t) + bias + sel_mask(J0, 2), v_cols(vs_ref, J0, 2))

    def window_fast():
        J0 = jnp.maximum(i - n_win, 0)
        bias = wbias_ref[0, pl.ds(pl.multiple_of((n_win - (i - J0)) * Q_BLOCK, Q_BLOCK), (n_win + 1) * Q_BLOCK), :]
        return add_chunk(zero, _dot(k_rows(kw_ref, J0, n_win + 1), q_t) + bias, v_cols(vw_ref, J0, n_win + 1))

    def sweep_exact(k_ref, v_ref, bias_ref, far_dist, n_back, with_sel):
        def body(t, st):
            J = i - t
            off = pl.multiple_of((far_dist - jnp.minimum(t, far_dist)) * Q_BLOCK, Q_BLOCK)
            s = _dot(k_rows(k_ref, J, 1), q_t) + bias_ref[0, pl.ds(off, Q_BLOCK), :]
            if with_sel:
                s = s + sel_mask(J, 1)
            return _softmax_tile_update(s, v_ref[0, J], *st)

        init = (jnp.full((1, N), NEG_INF, F32),) + zero
        _, l, acc = lax.fori_loop(0, jnp.minimum(i, n_back) + 1, body, init)
        return l, acc

    vals = {}

    def load_phase():
        vals["cmp"] = compressed(True)
        vals["win"] = window_fast()

    def select_phase():
        select_blocks(vals["cmp"][0])

    def near_phase():
        l_near, acc_near = selected_near()
        l_ref[u] = jnp.broadcast_to(l_near, l_ref.shape[1:])
        acc_ref[u] = acc_near

    def exact_path():
        p_x, _, o_x = compressed(False)
        select_blocks(p_x)
        l_sx, a_sx = sweep_exact(ks_ref, vs_ref, sbias_ref, 2, nb, True)
        l_wx, a_wx = sweep_exact(kw_ref, vw_ref, wbias_ref, n_win, n_win, False)
        return o_x, a_sx * (1.0 / l_sx), a_wx * (1.0 / l_wx)

    def gate_row(br):
        return jnp.concatenate([gate_ref[br, 0, r:r + 1, u * Q_BLOCK:(u + 1) * Q_BLOCK] for r in range(R)], axis=1)

    def fast_result():
        _, l_c, o_c = vals["cmp"]
        l_w, a_w = vals["win"]
        l_s, a_s = l_ref[u, 0:1, :], acc_ref[u]
        lane_q = lax.broadcasted_iota(jnp.int32, (1, N), 1) % Q_BLOCK
        has_cmp = (i > 0) | (lane_q >= CMP_LEN - 1)
        l_min = jnp.minimum(jnp.minimum(l_s, l_w), jnp.where(has_cmp, l_c, 1.0))
        return l_min, (o_c, a_s * (1.0 / l_s), a_w * (1.0 / l_w))

    def combine(outs):
        o_c, o_s, o_w = outs
        o_t = gate_row(0) * o_c + gate_row(1) * o_s + gate_row(2) * o_w
        return _pairs_to_rows(o_t)

    return load_phase, select_phase, near_phase, fast_result, exact_path, combine


def _nsa_attention(colmat, rowmat, cmp_rows, cmp_cols, gates, cbias, sbias, wbias, ov_t, msel, B, S):
    nb = S // Q_BLOCK
    nc = S // CMP_STRIDE
    n_slc = S // SLC_LEN
    G, R = NSA_KV_HEADS, NSA_REP
    kern = functools.partial(_nsa_kernel, nb=nb)
    ns = nb // NSA_SUB
    return pl.pallas_call(
        kern,
        grid=(B, G, ns),
        in_specs=[
            pl.BlockSpec((R, NSA_SUB, HEAD_DIM, LANES), lambda b, g, i: (g, b * ns + i, 0, 0)),
            pl.BlockSpec((1, 1, nc, HEAD_DIM), lambda b, g, i: (g, b, 0, 0)),
            pl.BlockSpec((1, 1, HEAD_DIM, nc), lambda b, g, i: (NSA_KV_HEADS + g, b, 0, 0)),
            pl.BlockSpec((1, S, HEAD_DIM), lambda b, g, i: (ROW_KS + g, b, 0)),
            pl.BlockSpec((1, nb, HEAD_DIM, LANES), lambda b, g, i: (COL_VS + g, b, 0, 0)),
            pl.BlockSpec((1, S, HEAD_DIM), lambda b, g, i: (ROW_KW + g, b, 0)),
            pl.BlockSpec((1, nb, HEAD_DIM, LANES), lambda b, g, i: (COL_VW + g, b, 0, 0)),
            pl.BlockSpec((N_BRANCHES, 1, R, NSA_SUB * LANES), lambda b, g, i: (0, g, 0, b * ns + i)),
            pl.BlockSpec((1, cbias.shape[1], R * Q_BLOCK), lambda b, g, i: (g, 0, 0)),
            pl.BlockSpec((1, sbias.shape[1], R * Q_BLOCK), lambda b, g, i: (g, 0, 0)),
            pl.BlockSpec((1, wbias.shape[1], R * Q_BLOCK), lambda b, g, i: (g, 0, 0)),
            pl.BlockSpec((n_slc, nc), lambda b, g, i: (0, 0)),
            pl.BlockSpec((1, 1, Q_BLOCK), lambda b, g, i: (g, 0, 0)),
        ],
        out_specs=pl.BlockSpec((1, NSA_SUB * Q_BLOCK, R * HEAD_DIM), lambda b, g, i: (b, i, g)),
        out_shape=jax.ShapeDtypeStruct((B, S, NSA_HEADS * HEAD_DIM), BF16),
        scratch_shapes=[pltpu.VMEM((NSA_SUB, n_slc + NEG_ROWS, Q_BLOCK), F32)]
        + [pltpu.VMEM((FAR_TILES * Q_BLOCK, R * Q_BLOCK), F32)] * 2
        + [pltpu.VMEM((FAR_TILES * Q_BLOCK, R * Q_BLOCK), BF16)] * 2
        + [pltpu.VMEM((NSA_SUB, HEAD_DIM, R * Q_BLOCK), F32),
           pltpu.VMEM((NSA_SUB, 8, R * Q_BLOCK), F32)],
        compiler_params=_cparams(("parallel", "parallel", "arbitrary")),
        name="nsa_attention",
    )(colmat, cmp_rows, cmp_cols, rowmat, colmat, rowmat, colmat, gates, cbias, sbias, wbias, ov_t, msel)


SB_QB = 512
SB_KT = 256
SB_ZERO_LOG = -160.0


def _sb_kernel(q_ref, k_ref, v_ref, tri_ref, o_ref):
    i = pl.program_id(1)
    H = SB_HEADS
    tri = tri_ref[...]
    assert SB_QB == 2 * SB_KT
    below_diag = (lax.broadcasted_iota(jnp.int32, (SB_KT, SB_KT), 1)
                  < lax.broadcasted_iota(jnp.int32, (SB_KT, SB_KT), 0))

    def step(J, rows, st, masked):
        out = []
        for h in range(H):
            carry, acc = st[h]
            k_t = jnp.concatenate([k_ref[h, (SB_KT // LANES) * J + c] for c in range(SB_KT // LANES)], axis=1)
            z = _dot(q_ref[h, rows, :], k_t)
            sp = jnp.maximum(z, 0.0) + jnp.log2(1.0 + jnp.exp2(-jnp.abs(z)))
            lf = jnp.where(below_diag, -sp, 0.0) if masked else -sp
            lf_hi = lf.astype(BF16)
            lf_lo = (lf - lf_hi.astype(F32)).astype(BF16)
            after = _dot(lf_hi, tri) + _dot(lf_lo, tri) + carry
            w = jnp.exp2(z - sp + after)
            if masked:
                w = jnp.where(below_diag, w, 0.0)
            v = v_ref[h, pl.ds(pl.multiple_of(J * SB_KT, SB_KT), SB_KT), :]
            acc = acc + _dot(w.astype(BF16), v)
            carry = carry + jnp.sum(lf, axis=1, keepdims=True)
            out.append((carry, acc))
        return tuple(out)

    lo, hi = slice(0, SB_KT), slice(SB_KT, SB_QB)
    zero = tuple((jnp.zeros((SB_KT, 1), F32), jnp.zeros((SB_KT, HEAD_DIM), F32)) for _ in range(H))
    st_hi = step(2 * i, hi, step(2 * i + 1, hi, zero, True), False)
    st_lo = step(2 * i, lo, zero, True)

    def alive(st):
        worst = st[0][0]
        for h in range(1, H):
            worst = jnp.maximum(worst, st[h][0])
        return jnp.max(worst) > SB_ZERO_LOG

    def sweep(rows, st):
        def body(c):
            J, _, st = c
            st = step(J, rows, st, False)
            return J - 1, alive(st), st

        return lax.while_loop(lambda c: (c[0] >= 0) & c[1], body, (2 * i - 1, alive(st), st))[2]

    st_lo, st_hi = sweep(lo, st_lo), sweep(hi, st_hi)
    o_ref[0, lo, :] = jnp.concatenate([st_lo[h][1] for h in range(H)], axis=1).astype(o_ref.dtype)
    o_ref[0, hi, :] = jnp.concatenate([st_hi[h][1] for h in range(H)], axis=1).astype(o_ref.dtype)


def _sb_attention(colmat, rowmat, tri, B, S):
    nq = S // SB_QB
    nb = S // LANES
    H = SB_HEADS
    return pl.pallas_call(
        _sb_kernel,
        grid=(B, nq),
        in_specs=[
            pl.BlockSpec((H, SB_QB, HEAD_DIM), lambda b, i: (ROW_SQ // H, b * nq + i, 0)),
            pl.BlockSpec((H, nb, HEAD_DIM, LANES), lambda b, i: (COL_SK // H, b, 0, 0)),
            pl.BlockSpec((H, S, HEAD_DIM), lambda b, i: (ROW_SV // H, b, 0)),
            pl.BlockSpec((SB_KT, SB_KT), lambda b, i: (0, 0)),
        ],
        out_specs=pl.BlockSpec((1, SB_QB, H * HEAD_DIM), lambda b, i: (b, i, 0)),
        out_shape=jax.ShapeDtypeStruct((B, S, H * HEAD_DIM), BF16),
        compiler_params=_cparams(("parallel", "arbitrary")),
        name="sb_attention",
    )(rowmat, colmat, rowmat, tri)


SWA_SUB = 8


def _swa_kernel(q_ref, k_ref, v_ref, bias_ref, sink_ref, o_ref):
    R = SWA_REP
    sink = sink_ref[0]
    outs = []
    for u in range(SWA_SUB):
        i = pl.program_id(2) * SWA_SUB + u
        q_t = jnp.concatenate([q_ref[r, u] for r in range(R)], axis=1)
        J0 = jnp.maximum(i - 1, 0)
        k = k_ref[0, pl.ds(pl.multiple_of(J0 * Q_BLOCK, Q_BLOCK), 2 * Q_BLOCK), :]
        bias = bias_ref[0, pl.ds(pl.multiple_of((1 - (i - J0)) * Q_BLOCK, Q_BLOCK), 2 * Q_BLOCK), :]
        s = _dot(k, q_t) + bias
        m = jnp.maximum(jnp.max(s, axis=0, keepdims=True), sink)
        e = jnp.exp(s - m)
        den = jnp.sum(e, axis=0, keepdims=True) + jnp.exp(sink - m)
        v_t = jnp.concatenate([v_ref[0, J0], v_ref[0, J0 + 1]], axis=1)
        o_t = _dot(v_t, e.astype(BF16)) * (1.0 / den)
        outs.append(_pairs_to_rows(o_t).astype(o_ref.dtype))
    for u in range(SWA_SUB):
        o_ref[0, u * Q_BLOCK:(u + 1) * Q_BLOCK, :] = outs[u]


def _swa_attention(colmat, rowmat, bias, sinks, B, S):
    nb = S // Q_BLOCK
    ns = nb // SWA_SUB
    G, R = SWA_KV_HEADS, SWA_REP
    return pl.pallas_call(
        _swa_kernel,
        grid=(B, G, ns),
        in_specs=[
            pl.BlockSpec((R, SWA_SUB, HEAD_DIM, LANES), lambda b, g, i: (COL_WQ // R + g, b * ns + i, 0, 0)),
            pl.BlockSpec((1, S, HEAD_DIM), lambda b, g, i: (ROW_WK + g, b, 0)),
            pl.BlockSpec((1, nb, HEAD_DIM, LANES), lambda b, g, i: (COL_WV + g, b, 0, 0)),
            pl.BlockSpec((1, 3 * Q_BLOCK, R * Q_BLOCK), lambda b, g, i: (g, 0, 0)),
            pl.BlockSpec((1, 1, R * Q_BLOCK), lambda b, g, i: (g, 0, 0)),
        ],
        out_specs=pl.BlockSpec((1, SWA_SUB * Q_BLOCK, R * HEAD_DIM), lambda b, g, i: (b, i, g)),
        out_shape=jax.ShapeDtypeStruct((B, S, SWA_HEADS * HEAD_DIM), BF16),
        compiler_params=_cparams(("parallel", "parallel", "arbitrary")),
        name="swa_attention",
    )(colmat, rowmat, colmat, bias, sinks)


MERGE_TM = 512


def _merge_kernel(x_ref, nw_ref, wg_ref, ya_ref, yb_ref, yc_ref, ua_ref, ub_ref, uc_ref, wo_ref, o_ref):
    x = x_ref[...]
    h = _rms_rows(x, nw_ref[...]).astype(BF16)
    merged = None
    for br, (y_ref, u_ref) in enumerate(((ya_ref, ua_ref), (yb_ref, ub_ref), (yc_ref, uc_ref))):
        g = jax.nn.sigmoid(_dot(h, wg_ref[:, br * D_MODEL:(br + 1) * D_MODEL]))
        t = g * _dot(y_ref[...], u_ref[...])
        merged = t if merged is None else merged + t
    o_ref[...] = x + _dot(merged.astype(BF16), wo_ref[...])


def _merge(x2, nw, w_bgate, y_nsa, y_sb, y_swa, u_nsa, u_sb, u_swa, w_out, l):
    T = x2.shape[0]
    full = lambda w: _layer_spec(l, w.shape[1:])
    rows = lambda n: pl.BlockSpec((MERGE_TM, n), lambda i: (i, 0))
    return pl.pallas_call(
        _merge_kernel,
        grid=(T // MERGE_TM,),
        in_specs=[rows(D_MODEL), pl.BlockSpec((1, D_MODEL), lambda i: (0, 0)), full(w_bgate),
                  rows(y_nsa.shape[1]), rows(y_sb.shape[1]), rows(y_swa.shape[1]),
                  full(u_nsa), full(u_sb), full(u_swa), full(w_out)],
        out_specs=rows(D_MODEL),
        out_shape=jax.ShapeDtypeStruct((T, D_MODEL), F32),
        compiler_params=_cparams(("parallel",)),
        name="merge",
    )(x2, nw.reshape(1, D_MODEL), w_bgate, y_nsa, y_sb, y_swa, u_nsa, u_sb, u_swa, w_out)


def _t5_bucket(dist):
    max_exact = NUM_BUCKETS // 2
    d = jnp.maximum(dist, 0)
    df = jnp.maximum(d, 1).astype(F32)
    large = max_exact + (jnp.log(df / max_exact) / math.log(MAX_DISTANCE / max_exact)
                         * (NUM_BUCKETS - max_exact)).astype(jnp.int32)
    large = jnp.minimum(large, NUM_BUCKETS - 1)
    return jnp.where(d < max_exact, d, large)


def _toeplitz(ext, delta):
    H = ext.shape[0]
    n = Q_BLOCK
    c = delta + MAX_DISTANCE
    w = ext[:, c - (n - 1):c + n]
    flat = jnp.pad(jnp.broadcast_to(w[:, None, :], (H, n, 2 * n - 1)), ((0, 0), (0, 0), (0, 1))).reshape(H, 2 * n * n)
    return flat[:, n - 1:n - 1 + n * (2 * n - 1)].reshape(H, n, 2 * n - 1)[:, :, :n]


def _lanes(t, G, R):
    K, Q = t.shape[1:]
    return t.reshape(G, R, K, Q).transpose(0, 2, 1, 3).reshape(G, K, R * Q)


def _bias_tables(rel_bias, nb, qk_bound):
    n = Q_BLOCK
    G, R = NSA_KV_HEADS, NSA_REP
    by_dist = rel_bias[_t5_bucket(jnp.arange(MAX_DISTANCE)), :].T.astype(F32)
    far = rel_bias[NUM_BUCKETS - 1, :].astype(F32)
    neg = jnp.full((1, n), NEG_INF, F32)
    k = jnp.arange(n)[:, None]
    q = jnp.arange(n)[None, :]
    above = (k > q)[None]

    rel = (by_dist[:NSA_HEADS] - far[:NSA_HEADS, None]) * LOG2E
    m_g = qk_bound + jnp.maximum(jnp.max(rel.reshape(G, -1), axis=1), 0.0)
    m_h = jnp.repeat(m_g, R)[:, None, None]
    ext = jnp.concatenate([jnp.broadcast_to(neg, (NSA_HEADS, n)), rel, jnp.zeros((NSA_HEADS, n), F32)], axis=1)
    t0, t1 = _toeplitz(ext, 0), _toeplitz(ext, n)
    zeros = jnp.zeros_like(t0)
    masked = jnp.full_like(t0, NEG_INF)
    sbias = _lanes(jnp.concatenate([zeros, t1, t0, masked], axis=1), G, R)
    n_win = NSA_WINDOW // n
    edge = jnp.where(above, zeros, NEG_INF)
    tiles = [edge] + [zeros] * (n_win - 2) + [t1, t0]
    wbias = _lanes(jnp.concatenate([t - m_h for t in tiles] + [masked] * n_win, axis=1), G, R)
    off = 8 * (nb - 1)
    near = jnp.stack([ext[:, 97 - CMP_STRIDE * m:97 - CMP_STRIDE * m + n] for m in range(-9, 7)], axis=1)
    n_rows = 16 * nb - 8
    cb = jnp.concatenate([jnp.zeros((NSA_HEADS, off - 9, n), F32), near,
                          jnp.full((NSA_HEADS, n_rows - off - 7, n), NEG_INF, F32)], axis=1) - m_h
    cbias = _lanes(cb, G, R)
    msel = jnp.broadcast_to(-m_g[:, None, None], (G, 1, n))

    hs = slice(NSA_HEADS, NSA_HEADS + SWA_HEADS)
    ext_w = jnp.concatenate([jnp.broadcast_to(neg, (SWA_HEADS, n)), by_dist[hs],
                             jnp.broadcast_to(far[hs, None], (SWA_HEADS, n))], axis=1)
    s0 = _toeplitz(ext_w, 0)
    s1 = jnp.where(above, _toeplitz(ext_w, n), NEG_INF)
    swa_bias = _lanes(jnp.concatenate([s1, s0, jnp.full_like(s0, NEG_INF)], axis=1), SWA_KV_HEADS, SWA_REP)
    return cbias, sbias, wbias, msel, swa_bias


def _overlap_t(S):
    nc, n_slc = S // CMP_STRIDE, S // SLC_LEN
    n = np.arange(nc)[None, :]
    j = np.arange(n_slc)[:, None]
    ov = (n * CMP_STRIDE < (j + 1) * SLC_LEN) & (n * CMP_STRIDE + CMP_LEN - 1 >= j * SLC_LEN) & (n < nc - 1)
    return jnp.asarray(ov, BF16)


def _head_gains(heads, norm_of, scale_of):
    ones = jnp.ones((HEAD_DIM,), F32)
    gains = [norm_of[name].astype(F32) * scale_of.get(name, 1.0) if name in norm_of else ones for name, _ in heads]
    return jnp.concatenate(gains).reshape(1, -1)


def kernel(x, rel_bias, ffn1_norm, ffn1_w_gate, ffn1_w_up, ffn1_w_down, mix_norm, w_in, nsa_q_norm, nsa_k_norm, nsa_cmp_pos, nsa_cmp_k_w1, nsa_cmp_k_w2, nsa_cmp_v_w1, nsa_cmp_v_w2, swa_q_norm, swa_k_norm, swa_sinks, w_up_nsa, w_up_sb, w_up_swa, w_out, ffn2_norm, ffn2_w_gate, ffn2_w_up, ffn2_w_down):
    B, S, D = x.shape
    T = B * S
    nb = S // Q_BLOCK
    nc = S // CMP_STRIDE
    depth = w_in.shape[0]
    assert D == D_MODEL and w_in.shape[2] == OFF_BGATE + N_BRANCHES * D_MODEL
    assert S % (NSA_SUB * Q_BLOCK) == 0 and S % (SWA_SUB * Q_BLOCK) == 0 and S % SB_QB == 0
    assert nb >= max(NSA_WINDOW // Q_BLOCK + 1, FAR_TILES) and S // SLC_LEN >= SLC_TOPN
    assert T % FFN_TM == 0 and T % PROJ_TM == 0 and T % MERGE_TM == 0 and S % PROJ_TM == 0
    scale = HEAD_DIM ** -0.5
    scale_of = {"nq": scale * LOG2E, "sq": scale * LOG2E, "wq": scale}
    qk_bound = (BOUND_SLACK * LOG2E * HEAD_DIM ** 0.5
                * jnp.max(jnp.abs(nsa_q_norm.astype(F32))) * jnp.max(jnp.abs(nsa_k_norm.astype(F32))))
    cbias, sbias, wbias, msel, swa_bias = _bias_tables(rel_bias, nb, qk_bound)
    plan = _proj_plan(scale_of)
    ov_t = _overlap_t(S)
    tri = jnp.asarray(np.tril(np.ones((SB_KT, SB_KT), np.float32), -1), BF16)

    (ffn1_w_gate, ffn1_w_up, ffn1_w_down, ffn2_w_gate, ffn2_w_up, ffn2_w_down, w_up_nsa, w_up_sb, w_up_swa,
     w_out) = (w.astype(BF16) for w in (ffn1_w_gate, ffn1_w_up, ffn1_w_down, ffn2_w_gate, ffn2_w_up, ffn2_w_down,
                                        w_up_nsa, w_up_sb, w_up_swa, w_out))
    tail0 = OFF_NGATE + N_GATE
    w_head, w_tail = w_in[:, :, :OFF_NGATE].astype(BF16), w_in[:, :, tail0:].astype(BF16)
    w_bgate = w_tail[:, :, OFF_BGATE - tail0:]
    w_gate = jnp.pad(w_in[:, :, OFF_NGATE:tail0], ((0, 0), (0, 0), (0, LANES - N_GATE))).astype(BF16)
    runs = []
    for _, off in ROW_HEADS + COL_HEADS:
        if runs and runs[-1][1] == off:
            runs[-1][1] = off + HEAD_DIM
        else:
            runs.append([off, off + HEAD_DIM])
    pieces = [w_head[:, :, a:b] if a < OFF_NGATE else w_tail[:, :, a - tail0:b - tail0] for a, b in runs]
    w_proj = jnp.concatenate(pieces + [w_gate], axis=2)

    x2 = x.reshape(T, D)
    for l in range(depth):
        x2 = _ffn(x2, ffn1_norm[l], ffn1_w_gate, ffn1_w_up, ffn1_w_down, l)

        norm_of = {"nq": nsa_q_norm[l], "nks": nsa_k_norm[l], "nkw": nsa_k_norm[l],
                   "wq": swa_q_norm[l], "wk": swa_k_norm[l]}
        gain = jnp.concatenate([_head_gains(ROW_HEADS, norm_of, scale_of), _head_gains(COL_HEADS, norm_of, scale_of),
                                jnp.ones((1, LANES), F32)], axis=1)
        rowmat, colmat, gates = _proj(x2, mix_norm[l], w_proj, gain, plan, l)
        gates = gates[:N_GATE].reshape(N_BRANCHES, NSA_KV_HEADS, NSA_REP, T)

        rows4 = rowmat[ROW_KC:ROW_KC + 2 * NSA_KV_HEADS].reshape(2 * NSA_KV_HEADS, B, nc, CMP_STRIDE * HEAD_DIM)
        cmp_rows, cmp_cols = _compress(rows4, jnp.stack([nsa_cmp_k_w1[l], nsa_cmp_v_w1[l]]),
                                       jnp.stack([nsa_cmp_k_w2[l], nsa_cmp_v_w2[l]]),
                                       nsa_cmp_pos[l], nsa_k_norm[l])

        y_nsa = _nsa_attention(colmat, rowmat, cmp_rows, cmp_cols, gates, cbias, sbias, wbias, ov_t, msel, B, S)
        y_sb = _sb_attention(colmat, rowmat, tri, B, S)
        sinks = jnp.repeat(swa_sinks[l].astype(F32).reshape(SWA_KV_HEADS, 1, SWA_REP), Q_BLOCK, axis=2)
        y_swa = _swa_attention(colmat, rowmat, swa_bias, sinks, B, S)

        x2 = _merge(x2, mix_norm[l], w_bgate, y_nsa.reshape(T, -1), y_sb.reshape(T, -1),
                    y_swa.reshape(T, -1), w_up_nsa, w_up_sb, w_up_swa, w_out, l)
        x2 = _ffn(x2, ffn2_norm[l], ffn2_w_gate, ffn2_w_up, ffn2_w_down, l)
    return x2.reshape(B, S, D)
```

```python
import functools
import math

import numpy as np
import jax
import jax.numpy as jnp
from jax import lax
from jax.experimental import pallas as pl
from jax.experimental.pallas import tpu as pltpu

D_MODEL = 1024
HEAD_DIM = 64
Q_BLOCK = 128
NSA_HEADS = 8
NSA_KV_HEADS = 2
NSA_REP = NSA_HEADS // NSA_KV_HEADS
CMP_LEN = 32
CMP_STRIDE = 16
CMP_HIDDEN = 256
SLC_LEN = 64
SLC_TOPN = 16
NSA_WINDOW = 512
SB_HEADS = 4
SWA_HEADS = 4
SWA_KV_HEADS = 2
SWA_REP = SWA_HEADS // SWA_KV_HEADS
SWA_WINDOW = 128
NUM_BUCKETS = 32
MAX_DISTANCE = 128
D_FF = 2816
NORM_EPS = 1e-6
NEG_INF = -1e30
N_BRANCHES = 3
N_GATE = 3 * NSA_HEADS

LANES = 128
VMEM_LIMIT = 56 * 1024 * 1024
BF16 = jnp.bfloat16
F32 = jnp.float32

_sizes = (NSA_HEADS * HEAD_DIM,) + (NSA_KV_HEADS * HEAD_DIM,) * 6 + (N_GATE,) \
    + (SB_HEADS * HEAD_DIM,) * 3 + (SWA_HEADS * HEAD_DIM, SWA_KV_HEADS * HEAD_DIM,
                                    SWA_KV_HEADS * HEAD_DIM, N_BRANCHES * D_MODEL)
_offs = np.concatenate([[0], np.cumsum(_sizes)])
(OFF_NQ, OFF_NKC, OFF_NVC, OFF_NKS, OFF_NVS, OFF_NKW, OFF_NVW, OFF_NGATE,
 OFF_SQ, OFF_SK, OFF_SV, OFF_WQ, OFF_WK, OFF_WV, OFF_BGATE) = (int(o) for o in _offs[:-1])

ROW_HEADS = ([("nkc", OFF_NKC + 64 * g) for g in range(2)] + [("nvc", OFF_NVC + 64 * g) for g in range(2)]
             + [("nks", OFF_NKS + 64 * g) for g in range(2)] + [("nkw", OFF_NKW + 64 * g) for g in range(2)]
             + [("sq", OFF_SQ + 64 * h) for h in range(4)] + [("sv", OFF_SV + 64 * h) for h in range(4)]
             + [("wk", OFF_WK + 64 * g) for g in range(2)])
ROW_KC, ROW_VC, ROW_KS, ROW_KW, ROW_SQ, ROW_SV, ROW_WK = 0, 2, 4, 6, 8, 12, 16
COL_HEADS = ([("nq", OFF_NQ + 64 * h) for h in range(8)] + [("nvs", OFF_NVS + 64 * g) for g in range(2)]
             + [("nvw", OFF_NVW + 64 * g) for g in range(2)] + [("sk", OFF_SK + 64 * h) for h in range(4)]
             + [("wq", OFF_WQ + 64 * h) for h in range(4)] + [("wv", OFF_WV + 64 * g) for g in range(2)])
COL_NQ, COL_VS, COL_VW, COL_SK, COL_WQ, COL_WV = 0, 8, 10, 12, 16, 20


def _dot(a, b):
    return jnp.dot(a, b, preferred_element_type=F32)


def _rms_rows(x, w):
    ms = jnp.mean(x * x, axis=-1, keepdims=True)
    return x * lax.rsqrt(ms + NORM_EPS) * w


def _cparams(sem):
    return pltpu.CompilerParams(dimension_semantics=sem, vmem_limit_bytes=VMEM_LIMIT)


FFN_TM = 1024
MXU_TILE = 256
FFN_FC = MXU_TILE


def _ffn_rows(x, nw_ref, wg_ref, wu_ref, wd_ref):
    h = _rms_rows(x, nw_ref[...]).astype(BF16)
    acc = jnp.zeros(x.shape, F32)
    for c in range(D_FF // FFN_FC):
        cols = slice(c * FFN_FC, (c + 1) * FFN_FC)
        g = _dot(h, wg_ref[:, cols])
        u = _dot(h, wu_ref[:, cols])
        a = (g * jax.nn.sigmoid(g)) * u
        acc = acc + _dot(a.astype(BF16), wd_ref[cols, :])
    return x + 0.5 * acc


def _ffn_kernel(x_ref, nw_ref, wg_ref, wu_ref, wd_ref, o_ref):
    o_ref[...] = _ffn_rows(x_ref[...], nw_ref, wg_ref, wu_ref, wd_ref)


def _layer_spec(l, shape, **kw):
    return pl.BlockSpec((None,) + tuple(shape), lambda i: (l, 0, 0), **kw)


def _ffn(x2, nw, wg, wu, wd, l):
    T = x2.shape[0]
    resident = lambda shape: _layer_spec(l, shape, pipeline_mode=pl.Buffered(1))
    return pl.pallas_call(
        _ffn_kernel,
        grid=(T // FFN_TM,),
        in_specs=[
            pl.BlockSpec((FFN_TM, D_MODEL), lambda i: (i, 0)),
            pl.BlockSpec((1, D_MODEL), lambda i: (0, 0)),
            resident((D_MODEL, D_FF)), resident((D_MODEL, D_FF)), resident((D_FF, D_MODEL)),
        ],
        out_specs=pl.BlockSpec((FFN_TM, D_MODEL), lambda i: (i, 0)),
        out_shape=jax.ShapeDtypeStruct((T, D_MODEL), F32),
        compiler_params=_cparams(("parallel",)),
        name="ffn",
    )(x2, nw.reshape(1, D_MODEL), wg, wu, wd)


PROJ_TM = 512


GATE_ROWS = 32

_NORMED = ("nq", "nks", "nkw", "wq", "wk")


def _proj_plan(scale_of):
    plan = []
    for kind, heads in (("row", ROW_HEADS), ("col", COL_HEADS)):
        for p in range(len(heads) // 2):
            name = heads[2 * p][0]
            assert heads[2 * p + 1][0] == name
            plan.append((kind, p, name in _NORMED, 1.0 if name in _NORMED else scale_of.get(name, 1.0)))
    plan.append(("gate", 0, False, 1.0))
    return tuple(plan)


def _proj_kernel(x_ref, nw_ref, w_ref, gain_ref, bd_ref, row_ref, col_ref, gate_ref, *, plan):
    h = _rms_rows(x_ref[...], nw_ref[...]).astype(BF16)
    pair = 2 * MXU_TILE
    for c0 in range(0, len(plan) * LANES, pair):
        width = min(pair, len(plan) * LANES - c0)
        zg = _dot(h, w_ref[:, c0:c0 + width])
        for u in range(width // LANES):
            t = c0 // LANES + u
            kind, p, normed, scale = plan[t]
            z = zg[:, u * LANES:(u + 1) * LANES]
            if normed:
                ms = _dot((z * z).astype(BF16), bd_ref[...])
                z = z * lax.rsqrt(ms + NORM_EPS) * gain_ref[:, t * LANES:(t + 1) * LANES]
            elif scale != 1.0:
                z = z * scale
            if kind == "row":
                row_ref[2 * p] = z[:, :HEAD_DIM].astype(row_ref.dtype)
                row_ref[2 * p + 1] = z[:, HEAD_DIM:].astype(row_ref.dtype)
            elif kind == "col":
                z_t = z.T
                for c in range(PROJ_TM // LANES):
                    col_ref[2 * p, c] = z_t[:HEAD_DIM, c * LANES:(c + 1) * LANES].astype(col_ref.dtype)
                    col_ref[2 * p + 1, c] = z_t[HEAD_DIM:, c * LANES:(c + 1) * LANES].astype(col_ref.dtype)
            else:
                gate_ref[...] = jax.nn.sigmoid(z).T[:GATE_ROWS]


def _proj(x2, nw, w, gain, plan, l):
    T = x2.shape[0]
    n_row, n_col = len(ROW_HEADS), len(COL_HEADS)
    bd = np.kron(np.eye(2, dtype=np.float32), np.full((HEAD_DIM, HEAD_DIM), 1.0 / HEAD_DIM, np.float32))
    return pl.pallas_call(
        functools.partial(_proj_kernel, plan=plan),
        grid=(T // PROJ_TM,),
        in_specs=[
            pl.BlockSpec((PROJ_TM, D_MODEL), lambda i: (i, 0)),
            pl.BlockSpec((1, D_MODEL), lambda i: (0, 0)),
            _layer_spec(l, w.shape[1:]),
            pl.BlockSpec(gain.shape, lambda i: (0, 0)),
            pl.BlockSpec((LANES, LANES), lambda i: (0, 0)),
        ],
        out_specs=[
            pl.BlockSpec((n_row, PROJ_TM, HEAD_DIM), lambda i: (0, i, 0)),
            pl.BlockSpec((n_col, PROJ_TM // LANES, HEAD_DIM, LANES), lambda i: (0, i, 0, 0)),
            pl.BlockSpec((GATE_ROWS, PROJ_TM), lambda i: (0, i)),
        ],
        out_shape=[jax.ShapeDtypeStruct((n_row, T, HEAD_DIM), BF16),
                   jax.ShapeDtypeStruct((n_col, T // LANES, HEAD_DIM, LANES), BF16),
                   jax.ShapeDtypeStruct((GATE_ROWS, T), F32)],
        compiler_params=_cparams(("parallel",)),
        name="proj",
    )(x2, nw.reshape(1, D_MODEL), w, gain, jnp.asarray(bd, BF16))


def _compress_kernel(r_ref, w1_ref, w2_ref, pos_ref, kn_ref, row_ref, col_ref):
    kind = pl.program_id(0) // NSA_KV_HEADS
    r = r_ref[0, 0]
    nc = r.shape[0]
    half = CMP_STRIDE * HEAD_DIM
    w1 = w1_ref[0]
    p_lo = _dot(r, w1[:half])
    p_hi = _dot(r, w1[half:])
    p_pos = _dot(jnp.broadcast_to(pos_ref[...], (8, CMP_LEN * HEAD_DIM)).astype(BF16), w1)[:1]
    hdn = p_lo + pltpu.roll(p_hi, nc - 1, 0) + p_pos
    hdn = hdn * jax.nn.sigmoid(hdn)
    out = _dot(hdn.astype(BF16), w2_ref[0])
    ms = jnp.sum(out * out, axis=-1, keepdims=True) * (1.0 / HEAD_DIM)
    normed = out * lax.rsqrt(ms + NORM_EPS) * kn_ref[...]
    out = jnp.where(kind == 0, normed, out)
    row_ref[0, 0] = out[:, :HEAD_DIM].astype(row_ref.dtype)
    col_ref[0, 0] = out.T[:HEAD_DIM].astype(col_ref.dtype)


def _compress(rows4, w1, w2, pos, k_norm):
    assert ROW_KC == 0 and ROW_VC == NSA_KV_HEADS
    _, B, NC, _ = rows4.shape
    kn = jnp.pad(k_norm.reshape(1, HEAD_DIM), ((0, 0), (0, LANES - HEAD_DIM)))
    w2p = jnp.pad(w2, ((0, 0), (0, 0), (0, LANES - HEAD_DIM))).astype(BF16)
    return pl.pallas_call(
        _compress_kernel,
        grid=(4, B),
        in_specs=[
            pl.BlockSpec((1, 1, NC, CMP_STRIDE * HEAD_DIM), lambda h, b: (h, b, 0, 0)),
            pl.BlockSpec((1, CMP_LEN * HEAD_DIM, CMP_HIDDEN), lambda h, b: (h // NSA_KV_HEADS, 0, 0)),
            pl.BlockSpec((1, CMP_HIDDEN, LANES), lambda h, b: (h // NSA_KV_HEADS, 0, 0)),
            pl.BlockSpec((1, CMP_LEN * HEAD_DIM), lambda h, b: (0, 0)),
            pl.BlockSpec((1, LANES), lambda h, b: (0, 0)),
        ],
        out_specs=[
            pl.BlockSpec((1, 1, NC, HEAD_DIM), lambda h, b: (h, b, 0, 0)),
            pl.BlockSpec((1, 1, HEAD_DIM, NC), lambda h, b: (h, b, 0, 0)),
        ],
        out_shape=[jax.ShapeDtypeStruct((4, B, NC, HEAD_DIM), BF16),
                   jax.ShapeDtypeStruct((4, B, HEAD_DIM, NC), BF16)],
        compiler_params=_cparams(("parallel", "parallel")),
        name="nsa_compress",
    )(rows4, w1.astype(BF16), w2p, pos.reshape(1, CMP_LEN * HEAD_DIM), kn)


def _softmax_tile_update(s, v_t, m, l, acc):
    m_new = jnp.maximum(m, jnp.max(s, axis=0, keepdims=True))
    alpha = jnp.exp2(m - m_new)
    p = jnp.exp2(s - m_new)
    l_new = alpha * l + jnp.sum(p, axis=0, keepdims=True)
    acc_new = alpha * acc + _dot(v_t, p.astype(BF16))
    return m_new, l_new, acc_new


def _pairs_to_rows(o_t):
    n = o_t.shape[1] // LANES
    outs = []
    for p in range(n // 2):
        a = o_t[:, (2 * p) * LANES:(2 * p + 1) * LANES]
        b = o_t[:, (2 * p + 1) * LANES:(2 * p + 2) * LANES]
        outs.append(jnp.concatenate([a, b], axis=0).T)
    return outs[0] if len(outs) == 1 else jnp.concatenate(outs, axis=1)


FAR_TILES = 4
NSA_SUB = 8
TAKEN = -2.0
NEG_ROWS = 8
N_FORCED = 3
LOG2E = math.log2(math.e)
NSA_TINY = 2.0 ** -100
BOUND_SLACK = 1.05


def _nsa_kernel(q_ref, kc_ref, vc_ref, ks_ref, vs_ref, kw_ref, vw_ref, gate_ref,
                cbias_ref, sbias_ref, wbias_ref, ov_ref, msel_ref, o_ref,
                neg_all, s_a, s_b, e_a, e_b, acc_ref, l_ref, *, nb):
    refs = (q_ref, kc_ref, vc_ref, ks_ref, vs_ref, kw_ref, vw_ref, gate_ref, cbias_ref, sbias_ref, wbias_ref,
            ov_ref, msel_ref)
    blocks = [_nsa_block(u, refs, neg_all.at[u], acc_ref, l_ref, nb) for u in range(NSA_SUB)]
    for phase in range(3):
        for blk in blocks:
            blk[phase]()
    _far_sweep(q_ref, ks_ref, vs_ref, neg_all, (s_a, s_b, e_a, e_b), acc_ref, l_ref, nb)
    fast = [blk[3]() for blk in blocks]
    l_min = fast[0][0]
    for l_u, _ in fast[1:]:
        l_min = jnp.minimum(l_min, l_u)
    outs = lax.cond(jnp.min(l_min) > NSA_TINY, lambda: tuple(o for _, o in fast),
                    lambda: tuple(blk[4]() for blk in blocks))
    for u, blk in enumerate(blocks):
        o_ref[0, u * Q_BLOCK:(u + 1) * Q_BLOCK, :] = blk[5](outs[u]).astype(o_ref.dtype)


def _far_sweep(q_ref, ks_ref, vs_ref, neg_all, stage_refs, acc_ref, l_ref, nb):
    R = NSA_REP
    n_slc = neg_all.shape[1] - NEG_ROWS
    i0 = pl.program_id(2) * NSA_SUB
    n_far = [jnp.maximum(i0 + u - 1, 0) for u in range(NSA_SUB)]
    offs = [0]
    for u in range(NSA_SUB):
        offs.append(offs[-1] + (n_far[u] + FAR_TILES - 1) // FAR_TILES)
    n_tot = offs[-1]

    def locate(c):
        u = sum(jnp.where(c >= offs[k], 1, 0) for k in range(1, NSA_SUB))
        start = sum(jnp.where(u == k, offs[k], 0) for k in range(NSA_SUB))
        far = sum(jnp.where(u == k, n_far[k], 0) for k in range(NSA_SUB))
        tile = FAR_TILES * (c - start)
        return u, jnp.clip(tile, 0, nb - FAR_TILES), far - tile

    def k_rows(J0):
        return ks_ref[0, pl.ds(pl.multiple_of(J0 * Q_BLOCK, Q_BLOCK), FAR_TILES * Q_BLOCK), :]

    def v_cols(J0):
        return jnp.concatenate([vs_ref[0, J0 + t] for t in range(FAR_TILES)], axis=1)

    def q_cols(u):
        return jnp.concatenate([q_ref[r, u] for r in range(R)], axis=1)

    def sel_mask(u, J0, n_valid):
        parts = []
        for t in range(2 * FAR_TILES):
            row = jnp.where(t // 2 < n_valid, 2 * J0 + t, n_slc)
            parts.append(jnp.broadcast_to(neg_all[u, pl.ds(row, 1), :], (SLC_LEN, Q_BLOCK)))
        return jnp.concatenate([jnp.concatenate(parts, axis=0)] * R, axis=1)

    def add_values(c, e_ref):
        u, tile, _ = locate(c)
        acc_ref[u] = acc_ref[u] + _dot(v_cols(tile), e_ref[...])

    def stage(c, s_in, s_out, e_in, e_out):
        u0, tile0, _ = locate(c - 1)
        u1, tile1, valid1 = locate(c)
        u2, tile2, _ = locate(c + 1)
        v_prev, acc_prev = v_cols(tile0), acc_ref[u0]
        mask, l_cur = sel_mask(u1, tile1, valid1), l_ref[u1]
        k_next, q_next = k_rows(tile2), q_cols(u2)
        pv = _dot(v_prev, e_in[...])
        s_out[...] = _dot(k_next, q_next)
        e = jnp.exp2(s_in[...] + mask)
        e_out[...] = e.astype(BF16)
        acc_ref[u0] = acc_prev + pv
        l_ref[u1] = l_cur + jnp.sum(e, axis=0, keepdims=True)

    s_a, s_b, e_a, e_b = stage_refs
    e_b[...] = jnp.zeros_like(e_b)
    u_first, tile_first, _ = locate(0)
    s_a[...] = _dot(k_rows(tile_first), q_cols(u_first))

    def pair(c):
        stage(c, s_a, s_b, e_b, e_a)
        stage(c + 1, s_b, s_a, e_a, e_b)

    def quad(c):
        pair(c)
        pair(c + 2)

    def octet(c):
        quad(c)
        quad(c + 4)

    def long_body(t, carry):
        octet(16 * t)
        octet(16 * t + 8)
        return carry

    n_long = n_tot // 16
    lax.fori_loop(0, n_long, long_body, 0)
    done = 16 * n_long

    @pl.when(n_tot - done >= 8)
    def _():
        octet(done)

    done = done + 8 * ((n_tot - done) // 8)

    @pl.when(n_tot - done >= 4)
    def _():
        quad(done)

    done = done + 4 * ((n_tot - done) // 4)

    @pl.when(n_tot - done >= 2)
    def _():
        pair(done)

    @pl.when(n_tot % 2 == 1)
    def _():
        stage(n_tot - 1, s_a, s_b, e_b, e_a)
        add_values(n_tot - 1, e_a)

    @pl.when((n_tot % 2 == 0) & (n_tot > 0))
    def _():
        add_values(n_tot - 1, e_b)


def _nsa_block(u, refs, neg_ref, acc_ref, l_ref, nb):
    (q_ref, kc_ref, vc_ref, ks_ref, vs_ref, kw_ref, vw_ref, gate_ref, cbias_ref, sbias_ref, wbias_ref,
     ov_ref, msel_ref) = refs
    i = pl.program_id(2) * NSA_SUB + u
    R = NSA_REP
    N = R * Q_BLOCK
    nc = kc_ref.shape[2]
    q_t = jnp.concatenate([q_ref[r, u] for r in range(R)], axis=1)
    n_win = NSA_WINDOW // Q_BLOCK

    def k_rows(ref, J0, n_tiles):
        return ref[0, pl.ds(pl.multiple_of(J0 * Q_BLOCK, Q_BLOCK), n_tiles * Q_BLOCK), :]

    def v_cols(ref, J0, n_tiles):
        return jnp.concatenate([ref[0, J0 + u] for u in range(n_tiles)], axis=1)

    def sel_mask(J0, n_tiles, n_valid=None):
        n_slc = neg_ref.shape[0] - NEG_ROWS
        parts = []
        for u in range(2 * n_tiles):
            row = 2 * J0 + u
            if n_valid is not None:
                row = jnp.where(u // 2 < n_valid, row, n_slc)
            parts.append(jnp.broadcast_to(neg_ref[pl.ds(row, 1), :], (SLC_LEN, Q_BLOCK)))
        return jnp.concatenate([jnp.concatenate(parts, axis=0)] * R, axis=1)

    def compressed(fast):
        cb = cbias_ref[0, pl.ds(pl.multiple_of(8 * (nb - 1) - 8 * i, 8), nc), :]
        s = _dot(kc_ref[0, 0], q_t) + cb
        if fast:
            e = jnp.exp2(s)
        else:
            e = jnp.where(s > 0.5 * NEG_INF, jnp.exp2(s - jnp.max(s, axis=0, keepdims=True)), 0.0)
        l = jnp.sum(e, axis=0, keepdims=True)
        p = e * jnp.where(l > 0.0, 1.0 / l, 0.0)
        return p, l, _dot(vc_ref[0, 0], p.astype(BF16))

    def select_blocks(p_c):
        p_sum = p_c[:, 0:Q_BLOCK]
        for r in range(1, R):
            p_sum = p_sum + p_c[:, r * Q_BLOCK:(r + 1) * Q_BLOCK]
        p_hi = p_sum.astype(BF16)
        p_lo = (p_sum - p_hi.astype(F32)).astype(BF16)
        imp = _dot(ov_ref[...], p_hi) + _dot(ov_ref[...], p_lo)
        n_slc = imp.shape[0]
        jj = lax.broadcasted_iota(jnp.int32, imp.shape, 0)
        qq = lax.broadcasted_iota(jnp.int32, imp.shape, 1)
        cur = 2 * i + jnp.where(qq >= SLC_LEN, 1, 0)
        forced = (jj == 0) | (jj == cur) | (jj == cur - 1)
        score = jnp.where(forced, TAKEN, jnp.where(jj <= cur, imp, -1.0))
        grp = 8
        j_rows = [jj[a:a + grp] for a in range(0, n_slc, grp)]
        for _ in range(min(SLC_TOPN, n_slc) - N_FORCED):
            cand = [(score[a:a + grp], j_rows[a // grp]) for a in range(0, n_slc, grp)]
            while len(cand) > 1:
                nxt = []
                for a in range(0, len(cand) - 1, 2):
                    (s0, j0), (s1, j1) = cand[a], cand[a + 1]
                    right = s1 > s0
                    nxt.append((jnp.maximum(s0, s1), jnp.where(right, j1, j0)))
                cand = nxt + cand[len(cand) - len(cand) % 2:]
            s8, j8 = cand[0]
            best = jnp.max(s8, axis=0, keepdims=True)
            first = jnp.min(jnp.where(s8 == best, j8, n_slc), axis=0, keepdims=True)
            score = jnp.where(jj == first, TAKEN, score)
        neg_ref[:n_slc] = jnp.where(score == TAKEN, msel_ref[0], NEG_INF)
        neg_ref[n_slc:] = jnp.full((NEG_ROWS, Q_BLOCK), NEG_INF, F32)

    def add_chunk(st, s, v_t):
        e = jnp.exp2(s)
        return st[0] + jnp.sum(e, axis=0, keepdims=True), st[1] + _dot(v_t, e.astype(BF16))

    zero = (jnp.zeros((1, N), F32), jnp.zeros((HEAD_DIM, N), F32))

    def selected_near():
        J0 = jnp.maximum(i - 1, 0)
        bias = sbias_ref[0, pl.ds(pl.multiple_of((2 - (i - J0)) * Q_BLOCK, Q_BLOCK), 2 * Q_BLOCK), :]
        return add_chunk(zero, _dot(k_rows(ks_ref, J0, 2), q_t) + bias + sel_mask(J0, 2), v_cols(vs_ref, J0, 2))

    def window_fast():
        J0 = jnp.maximum(i - n_win, 0)
        bias = wbias_ref[0, pl.ds(pl.multiple_of((n_win - (i - J0)) * Q_BLOCK, Q_BLOCK), (n_win + 1) * Q_BLOCK), :]
        return add_chunk(zero, _dot(k_rows(kw_ref, J0, n_win + 1), q_t) + bias, v_cols(vw_ref, J0, n_win + 1))

    def sweep_exact(k_ref, v_ref, bias_ref, far_dist, n_back, with_sel):
        def body(t, st):
            J = i - t
            off = pl.multiple_of((far_dist - jnp.minimum(t, far_dist)) * Q_BLOCK, Q_BLOCK)
            s = _dot(k_rows(k_ref, J, 1), q_t) + bias_ref[0, pl.ds(off, Q_BLOCK), :]
            if with_sel:
                s = s + sel_mask(J, 1)
            return _softmax_tile_update(s, v_ref[0, J], *st)

        init = (jnp.full((1, N), NEG_INF, F32),) + zero
        _, l, acc = lax.fori_loop(0, jnp.minimum(i, n_back) + 1, body, init)
        return l, acc

    vals = {}

    def load_phase():
        vals["cmp"] = compressed(True)
        vals["win"] = window_fast()

    def select_phase():
        select_blocks(vals["cmp"][0])

    def near_phase():
        l_near, acc_near = selected_near()
        l_ref[u] = jnp.broadcast_to(l_near, l_ref.shape[1:])
        acc_ref[u] = acc_near

    def exact_path():
        p_x, _, o_x = compressed(False)
        select_blocks(p_x)
        l_sx, a_sx = sweep_exact(ks_ref, vs_ref, sbias_ref, 2, nb, True)
        l_wx, a_wx = sweep_exact(kw_ref, vw_ref, wbias_ref, n_win, n_win, False)
        return o_x, a_sx * (1.0 / l_sx), a_wx * (1.0 / l_wx)

    def gate_row(br):
        return jnp.concatenate([gate_ref[br, 0, r:r + 1, u * Q_BLOCK:(u + 1) * Q_BLOCK] for r in range(R)], axis=1)

    def fast_result():
        _, l_c, o_c = vals["cmp"]
        l_w, a_w = vals["win"]
        l_s, a_s = l_ref[u, 0:1, :], acc_ref[u]
        lane_q = lax.broadcasted_iota(jnp.int32, (1, N), 1) % Q_BLOCK
        has_cmp = (i > 0) | (lane_q >= CMP_LEN - 1)
        l_min = jnp.minimum(jnp.minimum(l_s, l_w), jnp.where(has_cmp, l_c, 1.0))
        return l_min, (o_c, a_s * (1.0 / l_s), a_w * (1.0 / l_w))

    def combine(outs):
        o_c, o_s, o_w = outs
        o_t = gate_row(0) * o_c + gate_row(1) * o_s + gate_row(2) * o_w
        return _pairs_to_rows(o_t)

    return load_phase, select_phase, near_phase, fast_result, exact_path, combine


def _nsa_attention(colmat, rowmat, cmp_rows, cmp_cols, gates, cbias, sbias, wbias, ov_t, msel, B, S):
    nb = S // Q_BLOCK
    nc = S // CMP_STRIDE
    n_slc = S // SLC_LEN
    G, R = NSA_KV_HEADS, NSA_REP
    kern = functools.partial(_nsa_kernel, nb=nb)
    ns = nb // NSA_SUB
    return pl.pallas_call(
        kern,
        grid=(B, G, ns),
        in_specs=[
            pl.BlockSpec((R, NSA_SUB, HEAD_DIM, LANES), lambda b, g, i: (g, b * ns + i, 0, 0)),
            pl.BlockSpec((1, 1, nc, HEAD_DIM), lambda b, g, i: (g, b, 0, 0)),
            pl.BlockSpec((1, 1, HEAD_DIM, nc), lambda b, g, i: (NSA_KV_HEADS + g, b, 0, 0)),
            pl.BlockSpec((1, S, HEAD_DIM), lambda b, g, i: (ROW_KS + g, b, 0)),
            pl.BlockSpec((1, nb, HEAD_DIM, LANES), lambda b, g, i: (COL_VS + g, b, 0, 0)),
            pl.BlockSpec((1, S, HEAD_DIM), lambda b, g, i: (ROW_KW + g, b, 0)),
            pl.BlockSpec((1, nb, HEAD_DIM, LANES), lambda b, g, i: (COL_VW + g, b, 0, 0)),
            pl.BlockSpec((N_BRANCHES, 1, R, NSA_SUB * LANES), lambda b, g, i: (0, g, 0, b * ns + i)),
            pl.BlockSpec((1, cbias.shape[1], R * Q_BLOCK), lambda b, g, i: (g, 0, 0)),
            pl.BlockSpec((1, sbias.shape[1], R * Q_BLOCK), lambda b, g, i: (g, 0, 0)),
            pl.BlockSpec((1, wbias.shape[1], R * Q_BLOCK), lambda b, g, i: (g, 0, 0)),
            pl.BlockSpec((n_slc, nc), lambda b, g, i: (0, 0)),
            pl.BlockSpec((1, 1, Q_BLOCK), lambda b, g, i: (g, 0, 0)),
        ],
        out_specs=pl.BlockSpec((1, NSA_SUB * Q_BLOCK, R * HEAD_DIM), lambda b, g, i: (b, i, g)),
        out_shape=jax.ShapeDtypeStruct((B, S, NSA_HEADS * HEAD_DIM), BF16),
        scratch_shapes=[pltpu.VMEM((NSA_SUB, n_slc + NEG_ROWS, Q_BLOCK), F32)]
        + [pltpu.VMEM((FAR_TILES * Q_BLOCK, R * Q_BLOCK), F32)] * 2
        + [pltpu.VMEM((FAR_TILES * Q_BLOCK, R * Q_BLOCK), BF16)] * 2
        + [pltpu.VMEM((NSA_SUB, HEAD_DIM, R * Q_BLOCK), F32),
           pltpu.VMEM((NSA_SUB, 8, R * Q_BLOCK), F32)],
        compiler_params=_cparams(("parallel", "parallel", "arbitrary")),
        name="nsa_attention",
    )(colmat, cmp_rows, cmp_cols, rowmat, colmat, rowmat, colmat, gates, cbias, sbias, wbias, ov_t, msel)


SB_QB = 512
SB_KT = 256
SB_ZERO_LOG = -160.0


def _sb_kernel(q_ref, k_ref, v_ref, tri_ref, o_ref):
    i = pl.program_id(1)
    H = SB_HEADS
    tri = tri_ref[...]
    assert SB_QB == 2 * SB_KT
    below_diag = (lax.broadcasted_iota(jnp.int32, (SB_KT, SB_KT), 1)
                  < lax.broadcasted_iota(jnp.int32, (SB_KT, SB_KT), 0))

    def step(J, rows, st, masked):
        out = []
        for h in range(H):
            carry, acc = st[h]
            k_t = jnp.concatenate([k_ref[h, (SB_KT // LANES) * J + c] for c in range(SB_KT // LANES)], axis=1)
            z = _dot(q_ref[h, rows, :], k_t)
            sp = jnp.maximum(z, 0.0) + jnp.log2(1.0 + jnp.exp2(-jnp.abs(z)))
            lf = jnp.where(below_diag, -sp, 0.0) if masked else -sp
            lf_hi = lf.astype(BF16)
            lf_lo = (lf - lf_hi.astype(F32)).astype(BF16)
            after = _dot(lf_hi, tri) + _dot(lf_lo, tri) + carry
            w = jnp.exp2(z - sp + after)
            if masked:
                w = jnp.where(below_diag, w, 0.0)
            v = v_ref[h, pl.ds(pl.multiple_of(J * SB_KT, SB_KT), SB_KT), :]
            acc = acc + _dot(w.astype(BF16), v)
            carry = carry + jnp.sum(lf, axis=1, keepdims=True)
            out.append((carry, acc))
        return tuple(out)

    lo, hi = slice(0, SB_KT), slice(SB_KT, SB_QB)
    zero = tuple((jnp.zeros((SB_KT, 1), F32), jnp.zeros((SB_KT, HEAD_DIM), F32)) for _ in range(H))
    st_hi = step(2 * i, hi, step(2 * i + 1, hi, zero, True), False)
    st_lo = step(2 * i, lo, zero, True)

    def alive(st):
        worst = st[0][0]
        for h in range(1, H):
            worst = jnp.maximum(worst, st[h][0])
        return jnp.max(worst) > SB_ZERO_LOG

    def sweep(rows, st):
        def body(c):
            J, _, st = c
            st = step(J, rows, st, False)
            return J - 1, alive(st), st

        return lax.while_loop(lambda c: (c[0] >= 0) & c[1], body, (2 * i - 1, alive(st), st))[2]

    st_lo, st_hi = sweep(lo, st_lo), sweep(hi, st_hi)
    o_ref[0, lo, :] = jnp.concatenate([st_lo[h][1] for h in range(H)], axis=1).astype(o_ref.dtype)
    o_ref[0, hi, :] = jnp.concatenate([st_hi[h][1] for h in range(H)], axis=1).astype(o_ref.dtype)


def _sb_attention(colmat, rowmat, tri, B, S):
    nq = S // SB_QB
    nb = S // LANES
    H = SB_HEADS
    return pl.pallas_call(
        _sb_kernel,
        grid=(B, nq),
        in_specs=[
            pl.BlockSpec((H, SB_QB, HEAD_DIM), lambda b, i: (ROW_SQ // H, b * nq + i, 0)),
            pl.BlockSpec((H, nb, HEAD_DIM, LANES), lambda b, i: (COL_SK // H, b, 0, 0)),
            pl.BlockSpec((H, S, HEAD_DIM), lambda b, i: (ROW_SV // H, b, 0)),
            pl.BlockSpec((SB_KT, SB_KT), lambda b, i: (0, 0)),
        ],
        out_specs=pl.BlockSpec((1, SB_QB, H * HEAD_DIM), lambda b, i: (b, i, 0)),
        out_shape=jax.ShapeDtypeStruct((B, S, H * HEAD_DIM), BF16),
        compiler_params=_cparams(("parallel", "arbitrary")),
        name="sb_attention",
    )(rowmat, colmat, rowmat, tri)


SWA_SUB = 8


def _swa_kernel(q_ref, k_ref, v_ref, bias_ref, sink_ref, o_ref):
    R = SWA_REP
    sink = sink_ref[0]
    outs = []
    for u in range(SWA_SUB):
        i = pl.program_id(2) * SWA_SUB + u
        q_t = jnp.concatenate([q_ref[r, u] for r in range(R)], axis=1)
        J0 = jnp.maximum(i - 1, 0)
        k = k_ref[0, pl.ds(pl.multiple_of(J0 * Q_BLOCK, Q_BLOCK), 2 * Q_BLOCK), :]
        bias = bias_ref[0, pl.ds(pl.multiple_of((1 - (i - J0)) * Q_BLOCK, Q_BLOCK), 2 * Q_BLOCK), :]
        s = _dot(k, q_t) + bias
        m = jnp.maximum(jnp.max(s, axis=0, keepdims=True), sink)
        e = jnp.exp(s - m)
        den = jnp.sum(e, axis=0, keepdims=True) + jnp.exp(sink - m)
        v_t = jnp.concatenate([v_ref[0, J0], v_ref[0, J0 + 1]], axis=1)
        o_t = _dot(v_t, e.astype(BF16)) * (1.0 / den)
        outs.append(_pairs_to_rows(o_t).astype(o_ref.dtype))
    for u in range(SWA_SUB):
        o_ref[0, u * Q_BLOCK:(u + 1) * Q_BLOCK, :] = outs[u]


def _swa_attention(colmat, rowmat, bias, sinks, B, S):
    nb = S // Q_BLOCK
    ns = nb // SWA_SUB
    G, R = SWA_KV_HEADS, SWA_REP
    return pl.pallas_call(
        _swa_kernel,
        grid=(B, G, ns),
        in_specs=[
            pl.BlockSpec((R, SWA_SUB, HEAD_DIM, LANES), lambda b, g, i: (COL_WQ // R + g, b * ns + i, 0, 0)),
            pl.BlockSpec((1, S, HEAD_DIM), lambda b, g, i: (ROW_WK + g, b, 0)),
            pl.BlockSpec((1, nb, HEAD_DIM, LANES), lambda b, g, i: (COL_WV + g, b, 0, 0)),
            pl.BlockSpec((1, 3 * Q_BLOCK, R * Q_BLOCK), lambda b, g, i: (g, 0, 0)),
            pl.BlockSpec((1, 1, R * Q_BLOCK), lambda b, g, i: (g, 0, 0)),
        ],
        out_specs=pl.BlockSpec((1, SWA_SUB * Q_BLOCK, R * HEAD_DIM), lambda b, g, i: (b, i, g)),
        out_shape=jax.ShapeDtypeStruct((B, S, SWA_HEADS * HEAD_DIM), BF16),
        compiler_params=_cparams(("parallel", "parallel", "arbitrary")),
        name="swa_attention",
    )(colmat, rowmat, colmat, bias, sinks)


MERGE_TM = 512


def _merge_kernel(x_ref, nw_ref, wg_ref, ya_ref, yb_ref, yc_ref, ua_ref, ub_ref, uc_ref, wo_ref,
                  fnw_ref, fwg_ref, fwu_ref, fwd_ref, o_ref):
    x = x_ref[...]
    h = _rms_rows(x, nw_ref[...]).astype(BF16)
    merged = None
    for br, (y_ref, u_ref) in enumerate(((ya_ref, ua_ref), (yb_ref, ub_ref), (yc_ref, uc_ref))):
        g = jax.nn.sigmoid(_dot(h, wg_ref[:, br * D_MODEL:(br + 1) * D_MODEL]))
        t = g * _dot(y_ref[...], u_ref[...])
        merged = t if merged is None else merged + t
    x = x + _dot(merged.astype(BF16), wo_ref[...])
    o_ref[...] = _ffn_rows(x, fnw_ref, fwg_ref, fwu_ref, fwd_ref)


def _merge_ffn(x2, nw, w_bgate, y_nsa, y_sb, y_swa, u_nsa, u_sb, u_swa, w_out, ffn_nw, ffn_wg, ffn_wu, ffn_wd, l):
    T = x2.shape[0]
    full = lambda w: _layer_spec(l, w.shape[1:], pipeline_mode=pl.Buffered(1))
    vec = pl.BlockSpec((1, D_MODEL), lambda i: (0, 0))
    rows = lambda n: pl.BlockSpec((MERGE_TM, n), lambda i: (i, 0))
    return pl.pallas_call(
        _merge_kernel,
        grid=(T // MERGE_TM,),
        in_specs=[rows(D_MODEL), vec, full(w_bgate),
                  rows(y_nsa.shape[1]), rows(y_sb.shape[1]), rows(y_swa.shape[1]),
                  full(u_nsa), full(u_sb), full(u_swa), full(w_out),
                  vec, full(ffn_wg), full(ffn_wu), full(ffn_wd)],
        out_specs=rows(D_MODEL),
        out_shape=jax.ShapeDtypeStruct((T, D_MODEL), F32),
        compiler_params=_cparams(("parallel",)),
        name="merge_ffn",
    )(x2, nw.reshape(1, D_MODEL), w_bgate, y_nsa, y_sb, y_swa, u_nsa, u_sb, u_swa, w_out,
      ffn_nw.reshape(1, D_MODEL), ffn_wg, ffn_wu, ffn_wd)


def _t5_bucket(dist):
    max_exact = NUM_BUCKETS // 2
    d = jnp.maximum(dist, 0)
    df = jnp.maximum(d, 1).astype(F32)
    large = max_exact + (jnp.log(df / max_exact) / math.log(MAX_DISTANCE / max_exact)
                         * (NUM_BUCKETS - max_exact)).astype(jnp.int32)
    large = jnp.minimum(large, NUM_BUCKETS - 1)
    return jnp.where(d < max_exact, d, large)


def _toeplitz(ext, delta):
    H = ext.shape[0]
    n = Q_BLOCK
    c = delta + MAX_DISTANCE
    w = ext[:, c - (n - 1):c + n]
    flat = jnp.pad(jnp.broadcast_to(w[:, None, :], (H, n, 2 * n - 1)), ((0, 0), (0, 0), (0, 1))).reshape(H, 2 * n * n)
    return flat[:, n - 1:n - 1 + n * (2 * n - 1)].reshape(H, n, 2 * n - 1)[:, :, :n]


def _lanes(t, G, R):
    K, Q = t.shape[1:]
    return t.reshape(G, R, K, Q).transpose(0, 2, 1, 3).reshape(G, K, R * Q)


def _bias_tables(rel_bias, nb, qk_bound):
    n = Q_BLOCK
    G, R = NSA_KV_HEADS, NSA_REP
    by_dist = rel_bias[_t5_bucket(jnp.arange(MAX_DISTANCE)), :].T.astype(F32)
    far = rel_bias[NUM_BUCKETS - 1, :].astype(F32)
    neg = jnp.full((1, n), NEG_INF, F32)
    k = jnp.arange(n)[:, None]
    q = jnp.arange(n)[None, :]
    above = (k > q)[None]

    rel = (by_dist[:NSA_HEADS] - far[:NSA_HEADS, None]) * LOG2E
    m_g = qk_bound + jnp.maximum(jnp.max(rel.reshape(G, -1), axis=1), 0.0)
    m_h = jnp.repeat(m_g, R)[:, None, None]
    ext = jnp.concatenate([jnp.broadcast_to(neg, (NSA_HEADS, n)), rel, jnp.zeros((NSA_HEADS, n), F32)], axis=1)
    t0, t1 = _toeplitz(ext, 0), _toeplitz(ext, n)
    zeros = jnp.zeros_like(t0)
    masked = jnp.full_like(t0, NEG_INF)
    sbias = _lanes(jnp.concatenate([zeros, t1, t0, masked], axis=1), G, R)
    n_win = NSA_WINDOW // n
    edge = jnp.where(above, zeros, NEG_INF)
    tiles = [edge] + [zeros] * (n_win - 2) + [t1, t0]
    wbias = _lanes(jnp.concatenate([t - m_h for t in tiles] + [masked] * n_win, axis=1), G, R)
    off = 8 * (nb - 1)
    near = jnp.stack([ext[:, 97 - CMP_STRIDE * m:97 - CMP_STRIDE * m + n] for m in range(-9, 7)], axis=1)
    n_rows = 16 * nb - 8
    cb = jnp.concatenate([jnp.zeros((NSA_HEADS, off - 9, n), F32), near,
                          jnp.full((NSA_HEADS, n_rows - off - 7, n), NEG_INF, F32)], axis=1) - m_h
    cbias = _lanes(cb, G, R)
    msel = jnp.broadcast_to(-m_g[:, None, None], (G, 1, n))

    hs = slice(NSA_HEADS, NSA_HEADS + SWA_HEADS)
    ext_w = jnp.concatenate([jnp.broadcast_to(neg, (SWA_HEADS, n)), by_dist[hs],
                             jnp.broadcast_to(far[hs, None], (SWA_HEADS, n))], axis=1)
    s0 = _toeplitz(ext_w, 0)
    s1 = jnp.where(above, _toeplitz(ext_w, n), NEG_INF)
    swa_bias = _lanes(jnp.concatenate([s1, s0, jnp.full_like(s0, NEG_INF)], axis=1), SWA_KV_HEADS, SWA_REP)
    return cbias, sbias, wbias, msel, swa_bias


def _overlap_t(S):
    nc, n_slc = S // CMP_STRIDE, S // SLC_LEN
    n = np.arange(nc)[None, :]
    j = np.arange(n_slc)[:, None]
    ov = (n * CMP_STRIDE < (j + 1) * SLC_LEN) & (n * CMP_STRIDE + CMP_LEN - 1 >= j * SLC_LEN) & (n < nc - 1)
    return jnp.asarray(ov, BF16)


def _head_gains(heads, norm_of, scale_of):
    ones = jnp.ones((HEAD_DIM,), F32)
    gains = [norm_of[name].astype(F32) * scale_of.get(name, 1.0) if name in norm_of else ones for name, _ in heads]
    return jnp.concatenate(gains).reshape(1, -1)


def kernel(x, rel_bias, ffn1_norm, ffn1_w_gate, ffn1_w_up, ffn1_w_down, mix_norm, w_in, nsa_q_norm, nsa_k_norm, nsa_cmp_pos, nsa_cmp_k_w1, nsa_cmp_k_w2, nsa_cmp_v_w1, nsa_cmp_v_w2, swa_q_norm, swa_k_norm, swa_sinks, w_up_nsa, w_up_sb, w_up_swa, w_out, ffn2_norm, ffn2_w_gate, ffn2_w_up, ffn2_w_down):
    B, S, D = x.shape
    T = B * S
    nb = S // Q_BLOCK
    nc = S // CMP_STRIDE
    depth = w_in.shape[0]
    scale = HEAD_DIM ** -0.5
    scale_of = {"nq": scale * LOG2E, "sq": scale * LOG2E, "wq": scale}
    qk_bound = (BOUND_SLACK * LOG2E * HEAD_DIM ** 0.5
                * jnp.max(jnp.abs(nsa_q_norm.astype(F32))) * jnp.max(jnp.abs(nsa_k_norm.astype(F32))))
    cbias, sbias, wbias, msel, swa_bias = _bias_tables(rel_bias, nb, qk_bound)
    plan = _proj_plan(scale_of)
    ov_t = _overlap_t(S)
    tri = jnp.asarray(np.tril(np.ones((SB_KT, SB_KT), np.float32), -1), BF16)

    (ffn1_w_gate, ffn1_w_up, ffn1_w_down, ffn2_w_gate, ffn2_w_up, ffn2_w_down, w_up_nsa, w_up_sb, w_up_swa,
     w_out) = (w.astype(BF16) for w in (ffn1_w_gate, ffn1_w_up, ffn1_w_down, ffn2_w_gate, ffn2_w_up, ffn2_w_down,
                                        w_up_nsa, w_up_sb, w_up_swa, w_out))
    tail0 = OFF_NGATE + N_GATE
    w_head, w_tail = w_in[:, :, :OFF_NGATE].astype(BF16), w_in[:, :, tail0:].astype(BF16)
    w_bgate = w_tail[:, :, OFF_BGATE - tail0:]
    w_gate = jnp.pad(w_in[:, :, OFF_NGATE:tail0], ((0, 0), (0, 0), (0, LANES - N_GATE))).astype(BF16)
    runs = []
    for _, off in ROW_HEADS + COL_HEADS:
        if runs and runs[-1][1] == off:
            runs[-1][1] = off + HEAD_DIM
        else:
            runs.append([off, off + HEAD_DIM])
    pieces = [w_head[:, :, a:b] if a < OFF_NGATE else w_tail[:, :, a - tail0:b - tail0] for a, b in runs]
    w_proj = jnp.concatenate(pieces + [w_gate], axis=2)

    x2 = x.reshape(T, D)
    for l in range(depth):
        x2 = _ffn(x2, ffn1_norm[l], ffn1_w_gate, ffn1_w_up, ffn1_w_down, l)

        norm_of = {"nq": nsa_q_norm[l], "nks": nsa_k_norm[l], "nkw": nsa_k_norm[l],
                   "wq": swa_q_norm[l], "wk": swa_k_norm[l]}
        gain = jnp.concatenate([_head_gains(ROW_HEADS, norm_of, scale_of), _head_gains(COL_HEADS, norm_of, scale_of),
                                jnp.ones((1, LANES), F32)], axis=1)
        rowmat, colmat, gates = _proj(x2, mix_norm[l], w_proj, gain, plan, l)
        gates = gates[:N_GATE].reshape(N_BRANCHES, NSA_KV_HEADS, NSA_REP, T)

        rows4 = rowmat[ROW_KC:ROW_KC + 2 * NSA_KV_HEADS].reshape(2 * NSA_KV_HEADS, B, nc, CMP_STRIDE * HEAD_DIM)
        cmp_rows, cmp_cols = _compress(rows4, jnp.stack([nsa_cmp_k_w1[l], nsa_cmp_v_w1[l]]),
                                       jnp.stack([nsa_cmp_k_w2[l], nsa_cmp_v_w2[l]]),
                                       nsa_cmp_pos[l], nsa_k_norm[l])

        y_nsa = _nsa_attention(colmat, rowmat, cmp_rows, cmp_cols, gates, cbias, sbias, wbias, ov_t, msel, B, S)
        y_sb = _sb_attention(colmat, rowmat, tri, B, S)
        sinks = jnp.repeat(swa_sinks[l].astype(F32).reshape(SWA_KV_HEADS, 1, SWA_REP), Q_BLOCK, axis=2)
        y_swa = _swa_attention(colmat, rowmat, swa_bias, sinks, B, S)

        x2 = _merge_ffn(x2, mix_norm[l], w_bgate, y_nsa.reshape(T, -1), y_sb.reshape(T, -1),
                        y_swa.reshape(T, -1), w_up_nsa, w_up_sb, w_up_swa, w_out,
                        ffn2_norm[l], ffn2_w_gate, ffn2_w_up, ffn2_w_down, l)
    return x2.reshape(B, S, D)
```

```python
import functools
import math

import numpy as np
import jax
import jax.numpy as jnp
from jax import lax
from jax.experimental import pallas as pl
from jax.experimental.pallas import tpu as pltpu

D_MODEL = 1024
HEAD_DIM = 64
Q_BLOCK = 128
NSA_HEADS = 8
NSA_KV_HEADS = 2
NSA_REP = NSA_HEADS // NSA_KV_HEADS
CMP_LEN = 32
CMP_STRIDE = 16
CMP_HIDDEN = 256
SLC_LEN = 64
SLC_TOPN = 16
NSA_WINDOW = 512
SB_HEADS = 4
SWA_HEADS = 4
SWA_KV_HEADS = 2
SWA_REP = SWA_HEADS // SWA_KV_HEADS
SWA_WINDOW = 128
NUM_BUCKETS = 32
MAX_DISTANCE = 128
D_FF = 2816
NORM_EPS = 1e-6
NEG_INF = -1e30
N_BRANCHES = 3
N_GATE = 3 * NSA_HEADS

LANES = 128
VMEM_LIMIT = 56 * 1024 * 1024
BF16 = jnp.bfloat16
F32 = jnp.float32

_sizes = (NSA_HEADS * HEAD_DIM,) + (NSA_KV_HEADS * HEAD_DIM,) * 6 + (N_GATE,) \
    + (SB_HEADS * HEAD_DIM,) * 3 + (SWA_HEADS * HEAD_DIM, SWA_KV_HEADS * HEAD_DIM,
                                    SWA_KV_HEADS * HEAD_DIM, N_BRANCHES * D_MODEL)
_offs = np.concatenate([[0], np.cumsum(_sizes)])
(OFF_NQ, OFF_NKC, OFF_NVC, OFF_NKS, OFF_NVS, OFF_NKW, OFF_NVW, OFF_NGATE,
 OFF_SQ, OFF_SK, OFF_SV, OFF_WQ, OFF_WK, OFF_WV, OFF_BGATE) = (int(o) for o in _offs[:-1])

ROW_HEADS = ([("nkc", OFF_NKC + 64 * g) for g in range(2)] + [("nvc", OFF_NVC + 64 * g) for g in range(2)]
             + [("nks", OFF_NKS + 64 * g) for g in range(2)] + [("nkw", OFF_NKW + 64 * g) for g in range(2)]
             + [("sq", OFF_SQ + 64 * h) for h in range(4)] + [("sv", OFF_SV + 64 * h) for h in range(4)]
             + [("wk", OFF_WK + 64 * g) for g in range(2)])
ROW_KC, ROW_VC, ROW_KS, ROW_KW, ROW_SQ, ROW_SV, ROW_WK = 0, 2, 4, 6, 8, 12, 16
COL_HEADS = ([("nq", OFF_NQ + 64 * h) for h in range(8)] + [("nvs", OFF_NVS + 64 * g) for g in range(2)]
             + [("nvw", OFF_NVW + 64 * g) for g in range(2)] + [("sk", OFF_SK + 64 * h) for h in range(4)]
             + [("wq", OFF_WQ + 64 * h) for h in range(4)] + [("wv", OFF_WV + 64 * g) for g in range(2)])
COL_NQ, COL_VS, COL_VW, COL_SK, COL_WQ, COL_WV = 0, 8, 10, 12, 16, 20


def _dot(a, b):
    return jnp.dot(a, b, preferred_element_type=F32)


def _rms_rows(x, w):
    ms = jnp.mean(x * x, axis=-1, keepdims=True)
    return x * lax.rsqrt(ms + NORM_EPS) * w


def _cparams(sem):
    return pltpu.CompilerParams(dimension_semantics=sem, vmem_limit_bytes=VMEM_LIMIT)


MXU_TILE = 256
FFN_FC = MXU_TILE


def _ffn_rows(x, nw_ref, wg_ref, wu_ref, wd_ref):
    h = _rms_rows(x, nw_ref[...]).astype(BF16)
    acc = jnp.zeros(x.shape, F32)
    for c in range(D_FF // FFN_FC):
        cols = slice(c * FFN_FC, (c + 1) * FFN_FC)
        g = _dot(h, wg_ref[:, cols])
        u = _dot(h, wu_ref[:, cols])
        a = (g * jax.nn.sigmoid(g)) * u
        acc = acc + _dot(a.astype(BF16), wd_ref[cols, :])
    return x + 0.5 * acc


def _layer_spec(l, shape, **kw):
    return pl.BlockSpec((None,) + tuple(shape), lambda i: (l, 0, 0), **kw)


PROJ_TM = 512


GATE_ROWS = 32

_NORMED = ("nq", "nks", "nkw", "wq", "wk")


def _proj_plan(scale_of):
    plan = []
    for kind, heads in (("row", ROW_HEADS), ("col", COL_HEADS)):
        for p in range(len(heads) // 2):
            name = heads[2 * p][0]
            assert heads[2 * p + 1][0] == name
            plan.append((kind, p, name in _NORMED, 1.0 if name in _NORMED else scale_of.get(name, 1.0)))
    plan.append(("gate", 0, False, 1.0))
    return tuple(plan)


def _proj_kernel(x_ref, fnw_ref, fwg_ref, fwu_ref, fwd_ref, nw_ref, w_ref, gain_ref, bd_ref,
                 xo_ref, row_ref, col_ref, gate_ref, *, plan):
    x = _ffn_rows(x_ref[...], fnw_ref, fwg_ref, fwu_ref, fwd_ref)
    xo_ref[...] = x
    h = _rms_rows(x, nw_ref[...]).astype(BF16)
    pair = 2 * MXU_TILE
    for c0 in range(0, len(plan) * LANES, pair):
        width = min(pair, len(plan) * LANES - c0)
        zg = _dot(h, w_ref[:, c0:c0 + width])
        for u in range(width // LANES):
            t = c0 // LANES + u
            kind, p, normed, scale = plan[t]
            z = zg[:, u * LANES:(u + 1) * LANES]
            if normed:
                ms = _dot((z * z).astype(BF16), bd_ref[...])
                z = z * lax.rsqrt(ms + NORM_EPS) * gain_ref[:, t * LANES:(t + 1) * LANES]
            elif scale != 1.0:
                z = z * scale
            if kind == "row":
                row_ref[2 * p] = z[:, :HEAD_DIM].astype(row_ref.dtype)
                row_ref[2 * p + 1] = z[:, HEAD_DIM:].astype(row_ref.dtype)
            elif kind == "col":
                z_t = z.T
                for c in range(PROJ_TM // LANES):
                    col_ref[2 * p, c] = z_t[:HEAD_DIM, c * LANES:(c + 1) * LANES].astype(col_ref.dtype)
                    col_ref[2 * p + 1, c] = z_t[HEAD_DIM:, c * LANES:(c + 1) * LANES].astype(col_ref.dtype)
            else:
                gate_ref[...] = jax.nn.sigmoid(z).T[:GATE_ROWS]


def _ffn_proj(x2, ffn_nw, ffn_wg, ffn_wu, ffn_wd, nw, w, gain, plan, l):
    T = x2.shape[0]
    n_row, n_col = len(ROW_HEADS), len(COL_HEADS)
    bd = np.kron(np.eye(2, dtype=np.float32), np.full((HEAD_DIM, HEAD_DIM), 1.0 / HEAD_DIM, np.float32))
    full = lambda a: _layer_spec(l, a.shape[1:], pipeline_mode=pl.Buffered(1))
    vec = pl.BlockSpec((1, D_MODEL), lambda i: (0, 0))
    rows = pl.BlockSpec((PROJ_TM, D_MODEL), lambda i: (i, 0))
    return pl.pallas_call(
        functools.partial(_proj_kernel, plan=plan),
        grid=(T // PROJ_TM,),
        in_specs=[
            rows, vec, full(ffn_wg), full(ffn_wu), full(ffn_wd), vec, full(w),
            pl.BlockSpec(gain.shape, lambda i: (0, 0)),
            pl.BlockSpec((LANES, LANES), lambda i: (0, 0)),
        ],
        out_specs=[
            rows,
            pl.BlockSpec((n_row, PROJ_TM, HEAD_DIM), lambda i: (0, i, 0)),
            pl.BlockSpec((n_col, PROJ_TM // LANES, HEAD_DIM, LANES), lambda i: (0, i, 0, 0)),
            pl.BlockSpec((GATE_ROWS, PROJ_TM), lambda i: (0, i)),
        ],
        out_shape=[jax.ShapeDtypeStruct((T, D_MODEL), F32),
                   jax.ShapeDtypeStruct((n_row, T, HEAD_DIM), BF16),
                   jax.ShapeDtypeStruct((n_col, T // LANES, HEAD_DIM, LANES), BF16),
                   jax.ShapeDtypeStruct((GATE_ROWS, T), F32)],
        compiler_params=_cparams(("parallel",)),
        name="ffn_proj",
    )(x2, ffn_nw.reshape(1, D_MODEL), ffn_wg, ffn_wu, ffn_wd, nw.reshape(1, D_MODEL), w, gain, jnp.asarray(bd, BF16))


def _compress_kernel(r_ref, w1_ref, w2_ref, pos_ref, kn_ref, row_ref, col_ref):
    kind = pl.program_id(0) // NSA_KV_HEADS
    r = r_ref[0, 0]
    nc = r.shape[0]
    half = CMP_STRIDE * HEAD_DIM
    w1 = w1_ref[0]
    p_lo = _dot(r, w1[:half])
    p_hi = _dot(r, w1[half:])
    p_pos = _dot(jnp.broadcast_to(pos_ref[...], (8, CMP_LEN * HEAD_DIM)).astype(BF16), w1)[:1]
    hdn = p_lo + pltpu.roll(p_hi, nc - 1, 0) + p_pos
    hdn = hdn * jax.nn.sigmoid(hdn)
    out = _dot(hdn.astype(BF16), w2_ref[0])
    ms = jnp.sum(out * out, axis=-1, keepdims=True) * (1.0 / HEAD_DIM)
    normed = out * lax.rsqrt(ms + NORM_EPS) * kn_ref[...]
    out = jnp.where(kind == 0, normed, out)
    row_ref[0, 0] = out[:, :HEAD_DIM].astype(row_ref.dtype)
    col_ref[0, 0] = out.T[:HEAD_DIM].astype(col_ref.dtype)


def _compress(rows4, w1, w2, pos, k_norm):
    assert ROW_KC == 0 and ROW_VC == NSA_KV_HEADS
    _, B, NC, _ = rows4.shape
    kn = jnp.pad(k_norm.reshape(1, HEAD_DIM), ((0, 0), (0, LANES - HEAD_DIM)))
    w2p = jnp.pad(w2, ((0, 0), (0, 0), (0, LANES - HEAD_DIM))).astype(BF16)
    return pl.pallas_call(
        _compress_kernel,
        grid=(4, B),
        in_specs=[
            pl.BlockSpec((1, 1, NC, CMP_STRIDE * HEAD_DIM), lambda h, b: (h, b, 0, 0)),
            pl.BlockSpec((1, CMP_LEN * HEAD_DIM, CMP_HIDDEN), lambda h, b: (h // NSA_KV_HEADS, 0, 0)),
            pl.BlockSpec((1, CMP_HIDDEN, LANES), lambda h, b: (h // NSA_KV_HEADS, 0, 0)),
            pl.BlockSpec((1, CMP_LEN * HEAD_DIM), lambda h, b: (0, 0)),
            pl.BlockSpec((1, LANES), lambda h, b: (0, 0)),
        ],
        out_specs=[
            pl.BlockSpec((1, 1, NC, HEAD_DIM), lambda h, b: (h, b, 0, 0)),
            pl.BlockSpec((1, 1, HEAD_DIM, NC), lambda h, b: (h, b, 0, 0)),
        ],
        out_shape=[jax.ShapeDtypeStruct((4, B, NC, HEAD_DIM), BF16),
                   jax.ShapeDtypeStruct((4, B, HEAD_DIM, NC), BF16)],
        compiler_params=_cparams(("parallel", "parallel")),
        name="nsa_compress",
    )(rows4, w1.astype(BF16), w2p, pos.reshape(1, CMP_LEN * HEAD_DIM), kn)


def _softmax_tile_update(s, v_t, m, l, acc):
    m_new = jnp.maximum(m, jnp.max(s, axis=0, keepdims=True))
    alpha = jnp.exp2(m - m_new)
    p = jnp.exp2(s - m_new)
    l_new = alpha * l + jnp.sum(p, axis=0, keepdims=True)
    acc_new = alpha * acc + _dot(v_t, p.astype(BF16))
    return m_new, l_new, acc_new


def _pairs_to_rows(o_t):
    n = o_t.shape[1] // LANES
    outs = []
    for p in range(n // 2):
        a = o_t[:, (2 * p) * LANES:(2 * p + 1) * LANES]
        b = o_t[:, (2 * p + 1) * LANES:(2 * p + 2) * LANES]
        outs.append(jnp.concatenate([a, b], axis=0).T)
    return outs[0] if len(outs) == 1 else jnp.concatenate(outs, axis=1)


FAR_TILES = 4
NSA_SUB = 8
TAKEN = -2.0
NEG_ROWS = 8
N_FORCED = 3
LOG2E = math.log2(math.e)
NSA_TINY = 2.0 ** -100
BOUND_SLACK = 1.05


def _nsa_kernel(q_ref, kc_ref, vc_ref, ks_ref, vs_ref, kw_ref, vw_ref, gate_ref,
                cbias_ref, sbias_ref, wbias_ref, ov_ref, msel_ref, o_ref,
                neg_all, s_a, s_b, e_a, e_b, acc_ref, l_ref, *, nb):
    refs = (q_ref, kc_ref, vc_ref, ks_ref, vs_ref, kw_ref, vw_ref, gate_ref, cbias_ref, sbias_ref, wbias_ref,
            ov_ref, msel_ref)
    blocks = [_nsa_block(u, refs, neg_all.at[u], acc_ref, l_ref, nb) for u in range(NSA_SUB)]
    for phase in range(3):
        for blk in blocks:
            blk[phase]()
    _far_sweep(q_ref, ks_ref, vs_ref, neg_all, (s_a, s_b, e_a, e_b), acc_ref, l_ref, nb)
    fast = [blk[3]() for blk in blocks]
    l_min = fast[0][0]
    for l_u, _ in fast[1:]:
        l_min = jnp.minimum(l_min, l_u)
    outs = lax.cond(jnp.min(l_min) > NSA_TINY, lambda: tuple(o for _, o in fast),
                    lambda: tuple(blk[4]() for blk in blocks))
    for u, blk in enumerate(blocks):
        o_ref[0, u * Q_BLOCK:(u + 1) * Q_BLOCK, :] = blk[5](outs[u]).astype(o_ref.dtype)


def _far_sweep(q_ref, ks_ref, vs_ref, neg_all, stage_refs, acc_ref, l_ref, nb):
    R = NSA_REP
    n_slc = neg_all.shape[1] - NEG_ROWS
    i0 = pl.program_id(2) * NSA_SUB
    n_far = [jnp.maximum(i0 + u - 1, 0) for u in range(NSA_SUB)]
    offs = [0]
    for u in range(NSA_SUB):
        offs.append(offs[-1] + (n_far[u] + FAR_TILES - 1) // FAR_TILES)
    n_tot = offs[-1]

    def locate(c):
        u = sum(jnp.where(c >= offs[k], 1, 0) for k in range(1, NSA_SUB))
        start = sum(jnp.where(u == k, offs[k], 0) for k in range(NSA_SUB))
        far = sum(jnp.where(u == k, n_far[k], 0) for k in range(NSA_SUB))
        tile = FAR_TILES * (c - start)
        return u, jnp.clip(tile, 0, nb - FAR_TILES), far - tile

    def k_rows(J0):
        return ks_ref[0, pl.ds(pl.multiple_of(J0 * Q_BLOCK, Q_BLOCK), FAR_TILES * Q_BLOCK), :]

    def v_cols(J0):
        return jnp.concatenate([vs_ref[0, J0 + t] for t in range(FAR_TILES)], axis=1)

    def q_cols(u):
        return jnp.concatenate([q_ref[r, u] for r in range(R)], axis=1)

    def sel_mask(u, J0, n_valid):
        parts = []
        for t in range(2 * FAR_TILES):
            row = jnp.where(t // 2 < n_valid, 2 * J0 + t, n_slc)
            parts.append(jnp.broadcast_to(neg_all[u, pl.ds(row, 1), :], (SLC_LEN, Q_BLOCK)))
        return jnp.concatenate([jnp.concatenate(parts, axis=0)] * R, axis=1)

    def add_values(c, e_ref):
        u, tile, _ = locate(c)
        acc_ref[u] = acc_ref[u] + _dot(v_cols(tile), e_ref[...])

    def stage(c, s_in, s_out, e_in, e_out):
        u0, tile0, _ = locate(c - 1)
        u1, tile1, valid1 = locate(c)
        u2, tile2, _ = locate(c + 1)
        v_prev, acc_prev = v_cols(tile0), acc_ref[u0]
        mask, l_cur = sel_mask(u1, tile1, valid1), l_ref[u1]
        k_next, q_next = k_rows(tile2), q_cols(u2)
        pv = _dot(v_prev, e_in[...])
        s_out[...] = _dot(k_next, q_next)
        e = jnp.exp2(s_in[...] + mask)
        e_out[...] = e.astype(BF16)
        acc_ref[u0] = acc_prev + pv
        l_ref[u1] = l_cur + jnp.sum(e, axis=0, keepdims=True)

    s_a, s_b, e_a, e_b = stage_refs
    e_b[...] = jnp.zeros_like(e_b)
    u_first, tile_first, _ = locate(0)
    s_a[...] = _dot(k_rows(tile_first), q_cols(u_first))

    def pair(c):
        stage(c, s_a, s_b, e_b, e_a)
        stage(c + 1, s_b, s_a, e_a, e_b)

    def quad(c):
        pair(c)
        pair(c + 2)

    def octet(c):
        quad(c)
        quad(c + 4)

    def long_body(t, carry):
        octet(16 * t)
        octet(16 * t + 8)
        return carry

    n_long = n_tot // 16
    lax.fori_loop(0, n_long, long_body, 0)
    done = 16 * n_long

    @pl.when(n_tot - done >= 8)
    def _():
        octet(done)

    done = done + 8 * ((n_tot - done) // 8)

    @pl.when(n_tot - done >= 4)
    def _():
        quad(done)

    done = done + 4 * ((n_tot - done) // 4)

    @pl.when(n_tot - done >= 2)
    def _():
        pair(done)

    @pl.when(n_tot % 2 == 1)
    def _():
        stage(n_tot - 1, s_a, s_b, e_b, e_a)
        add_values(n_tot - 1, e_a)

    @pl.when((n_tot % 2 == 0) & (n_tot > 0))
    def _():
        add_values(n_tot - 1, e_b)


def _nsa_block(u, refs, neg_ref, acc_ref, l_ref, nb):
    (q_ref, kc_ref, vc_ref, ks_ref, vs_ref, kw_ref, vw_ref, gate_ref, cbias_ref, sbias_ref, wbias_ref,
     ov_ref, msel_ref) = refs
    i = pl.program_id(2) * NSA_SUB + u
    R = NSA_REP
    N = R * Q_BLOCK
    nc = kc_ref.shape[2]
    q_t = jnp.concatenate([q_ref[r, u] for r in range(R)], axis=1)
    n_win = NSA_WINDOW // Q_BLOCK

    def k_rows(ref, J0, n_tiles):
        return ref[0, pl.ds(pl.multiple_of(J0 * Q_BLOCK, Q_BLOCK), n_tiles * Q_BLOCK), :]

    def v_cols(ref, J0, n_tiles):
        return jnp.concatenate([ref[0, J0 + u] for u in range(n_tiles)], axis=1)

    def sel_mask(J0, n_tiles, n_valid=None):
        n_slc = neg_ref.shape[0] - NEG_ROWS
        parts = []
        for u in range(2 * n_tiles):
            row = 2 * J0 + u
            if n_valid is not None:
                row = jnp.where(u // 2 < n_valid, row, n_slc)
            parts.append(jnp.broadcast_to(neg_ref[pl.ds(row, 1), :], (SLC_LEN, Q_BLOCK)))
        return jnp.concatenate([jnp.concatenate(parts, axis=0)] * R, axis=1)

    def compressed(fast):
        cb = cbias_ref[0, pl.ds(pl.multiple_of(8 * (nb - 1) - 8 * i, 8), nc), :]
        s = _dot(kc_ref[0, 0], q_t) + cb
        if fast:
            e = jnp.exp2(s)
        else:
            e = jnp.where(s > 0.5 * NEG_INF, jnp.exp2(s - jnp.max(s, axis=0, keepdims=True)), 0.0)
        l = jnp.sum(e, axis=0, keepdims=True)
        p = e * jnp.where(l > 0.0, 1.0 / l, 0.0)
        return p, l, _dot(vc_ref[0, 0], p.astype(BF16))

    def select_blocks(p_c):
        p_sum = p_c[:, 0:Q_BLOCK]
        for r in range(1, R):
            p_sum = p_sum + p_c[:, r * Q_BLOCK:(r + 1) * Q_BLOCK]
        p_hi = p_sum.astype(BF16)
        p_lo = (p_sum - p_hi.astype(F32)).astype(BF16)
        imp = _dot(ov_ref[...], p_hi) + _dot(ov_ref[...], p_lo)
        n_slc = imp.shape[0]
        jj = lax.broadcasted_iota(jnp.int32, imp.shape, 0)
        qq = lax.broadcasted_iota(jnp.int32, imp.shape, 1)
        cur = 2 * i + jnp.where(qq >= SLC_LEN, 1, 0)
        forced = (jj == 0) | (jj == cur) | (jj == cur - 1)
        score = jnp.where(forced, TAKEN, jnp.where(jj <= cur, imp, -1.0))
        grp = 8
        j_rows = [jj[a:a + grp] for a in range(0, n_slc, grp)]
        for _ in range(min(SLC_TOPN, n_slc) - N_FORCED):
            cand = [(score[a:a + grp], j_rows[a // grp]) for a in range(0, n_slc, grp)]
            while len(cand) > 1:
                nxt = []
                for a in range(0, len(cand) - 1, 2):
                    (s0, j0), (s1, j1) = cand[a], cand[a + 1]
                    right = s1 > s0
                    nxt.append((jnp.maximum(s0, s1), jnp.where(right, j1, j0)))
                cand = nxt + cand[len(cand) - len(cand) % 2:]
            s8, j8 = cand[0]
            best = jnp.max(s8, axis=0, keepdims=True)
            first = jnp.min(jnp.where(s8 == best, j8, n_slc), axis=0, keepdims=True)
            score = jnp.where(jj == first, TAKEN, score)
        neg_ref[:n_slc] = jnp.where(score == TAKEN, msel_ref[0], NEG_INF)
        neg_ref[n_slc:] = jnp.full((NEG_ROWS, Q_BLOCK), NEG_INF, F32)

    def add_chunk(st, s, v_t):
        e = jnp.exp2(s)
        return st[0] + jnp.sum(e, axis=0, keepdims=True), st[1] + _dot(v_t, e.astype(BF16))

    zero = (jnp.zeros((1, N), F32), jnp.zeros((HEAD_DIM, N), F32))

    def selected_near():
        J0 = jnp.maximum(i - 1, 0)
        bias = sbias_ref[0, pl.ds(pl.multiple_of((2 - (i - J0)) * Q_BLOCK, Q_BLOCK), 2 * Q_BLOCK), :]
        return add_chunk(zero, _dot(k_rows(ks_ref, J0, 2), q_t) + bias + sel_mask(J0, 2), v_cols(vs_ref, J0, 2))

    def window_fast():
        J0 = jnp.maximum(i - n_win, 0)
        bias = wbias_ref[0, pl.ds(pl.multiple_of((n_win - (i - J0)) * Q_BLOCK, Q_BLOCK), (n_win + 1) * Q_BLOCK), :]
        return add_chunk(zero, _dot(k_rows(kw_ref, J0, n_win + 1), q_t) + bias, v_cols(vw_ref, J0, n_win + 1))

    def sweep_exact(k_ref, v_ref, bias_ref, far_dist, n_back, with_sel):
        def body(t, st):
            J = i - t
            off = pl.multiple_of((far_dist - jnp.minimum(t, far_dist)) * Q_BLOCK, Q_BLOCK)
            s = _dot(k_rows(k_ref, J, 1), q_t) + bias_ref[0, pl.ds(off, Q_BLOCK), :]
            if with_sel:
                s = s + sel_mask(J, 1)
            return _softmax_tile_update(s, v_ref[0, J], *st)

        init = (jnp.full((1, N), NEG_INF, F32),) + zero
        _, l, acc = lax.fori_loop(0, jnp.minimum(i, n_back) + 1, body, init)
        return l, acc

    vals = {}

    def load_phase():
        vals["cmp"] = compressed(True)
        vals["win"] = window_fast()

    def select_phase():
        select_blocks(vals["cmp"][0])

    def near_phase():
        l_near, acc_near = selected_near()
        l_ref[u] = jnp.broadcast_to(l_near, l_ref.shape[1:])
        acc_ref[u] = acc_near

    def exact_path():
        p_x, _, o_x = compressed(False)
        select_blocks(p_x)
        l_sx, a_sx = sweep_exact(ks_ref, vs_ref, sbias_ref, 2, nb, True)
        l_wx, a_wx = sweep_exact(kw_ref, vw_ref, wbias_ref, n_win, n_win, False)
        return o_x, a_sx * (1.0 / l_sx), a_wx * (1.0 / l_wx)

    def gate_row(br):
        return jnp.concatenate([gate_ref[br, 0, r:r + 1, u * Q_BLOCK:(u + 1) * Q_BLOCK] for r in range(R)], axis=1)

    def fast_result():
        _, l_c, o_c = vals["cmp"]
        l_w, a_w = vals["win"]
        l_s, a_s = l_ref[u, 0:1, :], acc_ref[u]
        lane_q = lax.broadcasted_iota(jnp.int32, (1, N), 1) % Q_BLOCK
        has_cmp = (i > 0) | (lane_q >= CMP_LEN - 1)
        l_min = jnp.minimum(jnp.minimum(l_s, l_w), jnp.where(has_cmp, l_c, 1.0))
        return l_min, (o_c, a_s * (1.0 / l_s), a_w * (1.0 / l_w))

    def combine(outs):
        o_c, o_s, o_w = outs
        o_t = gate_row(0) * o_c + gate_row(1) * o_s + gate_row(2) * o_w
        return _pairs_to_rows(o_t)

    return load_phase, select_phase, near_phase, fast_result, exact_path, combine


def _nsa_attention(colmat, rowmat, cmp_rows, cmp_cols, gates, cbias, sbias, wbias, ov_t, msel, B, S):
    nb = S // Q_BLOCK
    nc = S // CMP_STRIDE
    n_slc = S // SLC_LEN
    G, R = NSA_KV_HEADS, NSA_REP
    kern = functools.partial(_nsa_kernel, nb=nb)
    ns = nb // NSA_SUB
    return pl.pallas_call(
        kern,
        grid=(B, G, ns),
        in_specs=[
            pl.BlockSpec((R, NSA_SUB, HEAD_DIM, LANES), lambda b, g, i: (g, b * ns + i, 0, 0)),
            pl.BlockSpec((1, 1, nc, HEAD_DIM), lambda b, g, i: (g, b, 0, 0)),
            pl.BlockSpec((1, 1, HEAD_DIM, nc), lambda b, g, i: (NSA_KV_HEADS + g, b, 0, 0)),
            pl.BlockSpec((1, S, HEAD_DIM), lambda b, g, i: (ROW_KS + g, b, 0)),
            pl.BlockSpec((1, nb, HEAD_DIM, LANES), lambda b, g, i: (COL_VS + g, b, 0, 0)),
            pl.BlockSpec((1, S, HEAD_DIM), lambda b, g, i: (ROW_KW + g, b, 0)),
            pl.BlockSpec((1, nb, HEAD_DIM, LANES), lambda b, g, i: (COL_VW + g, b, 0, 0)),
            pl.BlockSpec((N_BRANCHES, 1, R, NSA_SUB * LANES), lambda b, g, i: (0, g, 0, b * ns + i)),
            pl.BlockSpec((1, cbias.shape[1], R * Q_BLOCK), lambda b, g, i: (g, 0, 0)),
            pl.BlockSpec((1, sbias.shape[1], R * Q_BLOCK), lambda b, g, i: (g, 0, 0)),
            pl.BlockSpec((1, wbias.shape[1], R * Q_BLOCK), lambda b, g, i: (g, 0, 0)),
            pl.BlockSpec((n_slc, nc), lambda b, g, i: (0, 0)),
            pl.BlockSpec((1, 1, Q_BLOCK), lambda b, g, i: (g, 0, 0)),
        ],
        out_specs=pl.BlockSpec((1, NSA_SUB * Q_BLOCK, R * HEAD_DIM), lambda b, g, i: (b, i, g)),
        out_shape=jax.ShapeDtypeStruct((B, S, NSA_HEADS * HEAD_DIM), BF16),
        scratch_shapes=[pltpu.VMEM((NSA_SUB, n_slc + NEG_ROWS, Q_BLOCK), F32)]
        + [pltpu.VMEM((FAR_TILES * Q_BLOCK, R * Q_BLOCK), F32)] * 2
        + [pltpu.VMEM((FAR_TILES * Q_BLOCK, R * Q_BLOCK), BF16)] * 2
        + [pltpu.VMEM((NSA_SUB, HEAD_DIM, R * Q_BLOCK), F32),
           pltpu.VMEM((NSA_SUB, 8, R * Q_BLOCK), F32)],
        compiler_params=_cparams(("parallel", "parallel", "arbitrary")),
        name="nsa_attention",
    )(colmat, cmp_rows, cmp_cols, rowmat, colmat, rowmat, colmat, gates, cbias, sbias, wbias, ov_t, msel)


SB_QB = 512
SB_KT = 256
SB_ZERO_LOG = -160.0


def _sb_kernel(q_ref, k_ref, v_ref, tri_ref, o_ref):
    i = pl.program_id(1)
    H = SB_HEADS
    tri = tri_ref[...]
    assert SB_QB == 2 * SB_KT
    below_diag = (lax.broadcasted_iota(jnp.int32, (SB_KT, SB_KT), 1)
                  < lax.broadcasted_iota(jnp.int32, (SB_KT, SB_KT), 0))

    def step(J, rows, st, masked):
        out = []
        for h in range(H):
            carry, acc = st[h]
            k_t = jnp.concatenate([k_ref[h, (SB_KT // LANES) * J + c] for c in range(SB_KT // LANES)], axis=1)
            z = _dot(q_ref[h, rows, :], k_t)
            sp = jnp.maximum(z, 0.0) + jnp.log2(1.0 + jnp.exp2(-jnp.abs(z)))
            lf = jnp.where(below_diag, -sp, 0.0) if masked else -sp
            lf_hi = lf.astype(BF16)
            lf_lo = (lf - lf_hi.astype(F32)).astype(BF16)
            after = _dot(lf_hi, tri) + _dot(lf_lo, tri) + carry
            w = jnp.exp2(z - sp + after)
            if masked:
                w = jnp.where(below_diag, w, 0.0)
            v = v_ref[h, pl.ds(pl.multiple_of(J * SB_KT, SB_KT), SB_KT), :]
            acc = acc + _dot(w.astype(BF16), v)
            carry = carry + jnp.sum(lf, axis=1, keepdims=True)
            out.append((carry, acc))
        return tuple(out)

    lo, hi = slice(0, SB_KT), slice(SB_KT, SB_QB)
    zero = tuple((jnp.zeros((SB_KT, 1), F32), jnp.zeros((SB_KT, HEAD_DIM), F32)) for _ in range(H))
    st_hi = step(2 * i, hi, step(2 * i + 1, hi, zero, True), False)
    st_lo = step(2 * i, lo, zero, True)

    def alive(st):
        worst = st[0][0]
        for h in range(1, H):
            worst = jnp.maximum(worst, st[h][0])
        return jnp.max(worst) > SB_ZERO_LOG

    def sweep(rows, st):
        def body(c):
            J, _, st = c
            st = step(J, rows, st, False)
            return J - 1, alive(st), st

        return lax.while_loop(lambda c: (c[0] >= 0) & c[1], body, (2 * i - 1, alive(st), st))[2]

    st_lo, st_hi = sweep(lo, st_lo), sweep(hi, st_hi)
    o_ref[0, lo, :] = jnp.concatenate([st_lo[h][1] for h in range(H)], axis=1).astype(o_ref.dtype)
    o_ref[0, hi, :] = jnp.concatenate([st_hi[h][1] for h in range(H)], axis=1).astype(o_ref.dtype)


def _sb_attention(colmat, rowmat, tri, B, S):
    nq = S // SB_QB
    nb = S // LANES
    H = SB_HEADS
    return pl.pallas_call(
        _sb_kernel,
        grid=(B, nq),
        in_specs=[
            pl.BlockSpec((H, SB_QB, HEAD_DIM), lambda b, i: (ROW_SQ // H, b * nq + i, 0)),
            pl.BlockSpec((H, nb, HEAD_DIM, LANES), lambda b, i: (COL_SK // H, b, 0, 0)),
            pl.BlockSpec((H, S, HEAD_DIM), lambda b, i: (ROW_SV // H, b, 0)),
            pl.BlockSpec((SB_KT, SB_KT), lambda b, i: (0, 0)),
        ],
        out_specs=pl.BlockSpec((1, SB_QB, H * HEAD_DIM), lambda b, i: (b, i, 0)),
        out_shape=jax.ShapeDtypeStruct((B, S, H * HEAD_DIM), BF16),
        compiler_params=_cparams(("parallel", "arbitrary")),
        name="sb_attention",
    )(rowmat, colmat, rowmat, tri)


SWA_SUB = 8


def _swa_kernel(q_ref, k_ref, v_ref, bias_ref, sink_ref, o_ref):
    R = SWA_REP
    sink = sink_ref[0]
    outs = []
    for u in range(SWA_SUB):
        i = pl.program_id(2) * SWA_SUB + u
        q_t = jnp.concatenate([q_ref[r, u] for r in range(R)], axis=1)
        J0 = jnp.maximum(i - 1, 0)
        k = k_ref[0, pl.ds(pl.multiple_of(J0 * Q_BLOCK, Q_BLOCK), 2 * Q_BLOCK), :]
        bias = bias_ref[0, pl.ds(pl.multiple_of((1 - (i - J0)) * Q_BLOCK, Q_BLOCK), 2 * Q_BLOCK), :]
        s = _dot(k, q_t) + bias
        m = jnp.maximum(jnp.max(s, axis=0, keepdims=True), sink)
        e = jnp.exp(s - m)
        den = jnp.sum(e, axis=0, keepdims=True) + jnp.exp(sink - m)
        v_t = jnp.concatenate([v_ref[0, J0], v_ref[0, J0 + 1]], axis=1)
        o_t = _dot(v_t, e.astype(BF16)) * (1.0 / den)
        outs.append(_pairs_to_rows(o_t).astype(o_ref.dtype))
    for u in range(SWA_SUB):
        o_ref[0, u * Q_BLOCK:(u + 1) * Q_BLOCK, :] = outs[u]


def _swa_attention(colmat, rowmat, bias, sinks, B, S):
    nb = S // Q_BLOCK
    ns = nb // SWA_SUB
    G, R = SWA_KV_HEADS, SWA_REP
    return pl.pallas_call(
        _swa_kernel,
        grid=(B, G, ns),
        in_specs=[
            pl.BlockSpec((R, SWA_SUB, HEAD_DIM, LANES), lambda b, g, i: (COL_WQ // R + g, b * ns + i, 0, 0)),
            pl.BlockSpec((1, S, HEAD_DIM), lambda b, g, i: (ROW_WK + g, b, 0)),
            pl.BlockSpec((1, nb, HEAD_DIM, LANES), lambda b, g, i: (COL_WV + g, b, 0, 0)),
            pl.BlockSpec((1, 3 * Q_BLOCK, R * Q_BLOCK), lambda b, g, i: (g, 0, 0)),
            pl.BlockSpec((1, 1, R * Q_BLOCK), lambda b, g, i: (g, 0, 0)),
        ],
        out_specs=pl.BlockSpec((1, SWA_SUB * Q_BLOCK, R * HEAD_DIM), lambda b, g, i: (b, i, g)),
        out_shape=jax.ShapeDtypeStruct((B, S, SWA_HEADS * HEAD_DIM), BF16),
        compiler_params=_cparams(("parallel", "parallel", "arbitrary")),
        name="swa_attention",
    )(colmat, rowmat, colmat, bias, sinks)


MERGE_TM = 512


def _merge_kernel(x_ref, nw_ref, wg_ref, ya_ref, yb_ref, yc_ref, ua_ref, ub_ref, uc_ref, wo_ref,
                  fnw_ref, fwg_ref, fwu_ref, fwd_ref, o_ref):
    x = x_ref[...]
    h = _rms_rows(x, nw_ref[...]).astype(BF16)
    merged = None
    for br, (y_ref, u_ref) in enumerate(((ya_ref, ua_ref), (yb_ref, ub_ref), (yc_ref, uc_ref))):
        g = jax.nn.sigmoid(_dot(h, wg_ref[:, br * D_MODEL:(br + 1) * D_MODEL]))
        t = g * _dot(y_ref[...], u_ref[...])
        merged = t if merged is None else merged + t
    x = x + _dot(merged.astype(BF16), wo_ref[...])
    o_ref[...] = _ffn_rows(x, fnw_ref, fwg_ref, fwu_ref, fwd_ref)


def _merge_ffn(x2, nw, w_bgate, y_nsa, y_sb, y_swa, u_nsa, u_sb, u_swa, w_out, ffn_nw, ffn_wg, ffn_wu, ffn_wd, l):
    T = x2.shape[0]
    full = lambda w: _layer_spec(l, w.shape[1:], pipeline_mode=pl.Buffered(1))
    vec = pl.BlockSpec((1, D_MODEL), lambda i: (0, 0))
    rows = lambda n: pl.BlockSpec((MERGE_TM, n), lambda i: (i, 0))
    return pl.pallas_call(
        _merge_kernel,
        grid=(T // MERGE_TM,),
        in_specs=[rows(D_MODEL), vec, full(w_bgate),
                  rows(y_nsa.shape[1]), rows(y_sb.shape[1]), rows(y_swa.shape[1]),
                  full(u_nsa), full(u_sb), full(u_swa), full(w_out),
                  vec, full(ffn_wg), full(ffn_wu), full(ffn_wd)],
        out_specs=rows(D_MODEL),
        out_shape=jax.ShapeDtypeStruct((T, D_MODEL), F32),
        compiler_params=_cparams(("parallel",)),
        name="merge_ffn",
    )(x2, nw.reshape(1, D_MODEL), w_bgate, y_nsa, y_sb, y_swa, u_nsa, u_sb, u_swa, w_out,
      ffn_nw.reshape(1, D_MODEL), ffn_wg, ffn_wu, ffn_wd)


def _t5_bucket(dist):
    max_exact = NUM_BUCKETS // 2
    d = jnp.maximum(dist, 0)
    df = jnp.maximum(d, 1).astype(F32)
    large = max_exact + (jnp.log(df / max_exact) / math.log(MAX_DISTANCE / max_exact)
                         * (NUM_BUCKETS - max_exact)).astype(jnp.int32)
    large = jnp.minimum(large, NUM_BUCKETS - 1)
    return jnp.where(d < max_exact, d, large)


def _toeplitz(ext, delta):
    H = ext.shape[0]
    n = Q_BLOCK
    c = delta + MAX_DISTANCE
    w = ext[:, c - (n - 1):c + n]
    flat = jnp.pad(jnp.broadcast_to(w[:, None, :], (H, n, 2 * n - 1)), ((0, 0), (0, 0), (0, 1))).reshape(H, 2 * n * n)
    return flat[:, n - 1:n - 1 + n * (2 * n - 1)].reshape(H, n, 2 * n - 1)[:, :, :n]


def _lanes(t, G, R):
    K, Q = t.shape[1:]
    return t.reshape(G, R, K, Q).transpose(0, 2, 1, 3).reshape(G, K, R * Q)


def _bias_tables(rel_bias, nb, qk_bound):
    n = Q_BLOCK
    G, R = NSA_KV_HEADS, NSA_REP
    by_dist = rel_bias[_t5_bucket(jnp.arange(MAX_DISTANCE)), :].T.astype(F32)
    far = rel_bias[NUM_BUCKETS - 1, :].astype(F32)
    neg = jnp.full((1, n), NEG_INF, F32)
    k = jnp.arange(n)[:, None]
    q = jnp.arange(n)[None, :]
    above = (k > q)[None]

    rel = (by_dist[:NSA_HEADS] - far[:NSA_HEADS, None]) * LOG2E
    m_g = qk_bound + jnp.maximum(jnp.max(rel.reshape(G, -1), axis=1), 0.0)
    m_h = jnp.repeat(m_g, R)[:, None, None]
    ext = jnp.concatenate([jnp.broadcast_to(neg, (NSA_HEADS, n)), rel, jnp.zeros((NSA_HEADS, n), F32)], axis=1)
    t0, t1 = _toeplitz(ext, 0), _toeplitz(ext, n)
    zeros = jnp.zeros_like(t0)
    masked = jnp.full_like(t0, NEG_INF)
    sbias = _lanes(jnp.concatenate([zeros, t1, t0, masked], axis=1), G, R)
    n_win = NSA_WINDOW // n
    edge = jnp.where(above, zeros, NEG_INF)
    tiles = [edge] + [zeros] * (n_win - 2) + [t1, t0]
    wbias = _lanes(jnp.concatenate([t - m_h for t in tiles] + [masked] * n_win, axis=1), G, R)
    off = 8 * (nb - 1)
    near = jnp.stack([ext[:, 97 - CMP_STRIDE * m:97 - CMP_STRIDE * m + n] for m in range(-9, 7)], axis=1)
    n_rows = 16 * nb - 8
    cb = jnp.concatenate([jnp.zeros((NSA_HEADS, off - 9, n), F32), near,
                          jnp.full((NSA_HEADS, n_rows - off - 7, n), NEG_INF, F32)], axis=1) - m_h
    cbias = _lanes(cb, G, R)
    msel = jnp.broadcast_to(-m_g[:, None, None], (G, 1, n))

    hs = slice(NSA_HEADS, NSA_HEADS + SWA_HEADS)
    ext_w = jnp.concatenate([jnp.broadcast_to(neg, (SWA_HEADS, n)), by_dist[hs],
                             jnp.broadcast_to(far[hs, None], (SWA_HEADS, n))], axis=1)
    s0 = _toeplitz(ext_w, 0)
    s1 = jnp.where(above, _toeplitz(ext_w, n), NEG_INF)
    swa_bias = _lanes(jnp.concatenate([s1, s0, jnp.full_like(s0, NEG_INF)], axis=1), SWA_KV_HEADS, SWA_REP)
    return cbias, sbias, wbias, msel, swa_bias


def _overlap_t(S):
    nc, n_slc = S // CMP_STRIDE, S // SLC_LEN
    n = np.arange(nc)[None, :]
    j = np.arange(n_slc)[:, None]
    ov = (n * CMP_STRIDE < (j + 1) * SLC_LEN) & (n * CMP_STRIDE + CMP_LEN - 1 >= j * SLC_LEN) & (n < nc - 1)
    return jnp.asarray(ov, BF16)


def _head_gains(heads, norm_of, scale_of):
    ones = jnp.ones((HEAD_DIM,), F32)
    gains = [norm_of[name].astype(F32) * scale_of.get(name, 1.0) if name in norm_of else ones for name, _ in heads]
    return jnp.concatenate(gains).reshape(1, -1)


def kernel(x, rel_bias, ffn1_norm, ffn1_w_gate, ffn1_w_up, ffn1_w_down, mix_norm, w_in, nsa_q_norm, nsa_k_norm, nsa_cmp_pos, nsa_cmp_k_w1, nsa_cmp_k_w2, nsa_cmp_v_w1, nsa_cmp_v_w2, swa_q_norm, swa_k_norm, swa_sinks, w_up_nsa, w_up_sb, w_up_swa, w_out, ffn2_norm, ffn2_w_gate, ffn2_w_up, ffn2_w_down):
    B, S, D = x.shape
    T = B * S
    nb = S // Q_BLOCK
    nc = S // CMP_STRIDE
    depth = w_in.shape[0]
    scale = HEAD_DIM ** -0.5
    scale_of = {"nq": scale * LOG2E, "sq": scale * LOG2E, "wq": scale}
    qk_bound = (BOUND_SLACK * LOG2E * HEAD_DIM ** 0.5
                * jnp.max(jnp.abs(nsa_q_norm.astype(F32))) * jnp.max(jnp.abs(nsa_k_norm.astype(F32))))
    cbias, sbias, wbias, msel, swa_bias = _bias_tables(rel_bias, nb, qk_bound)
    plan = _proj_plan(scale_of)
    ov_t = _overlap_t(S)
    tri = jnp.asarray(np.tril(np.ones((SB_KT, SB_KT), np.float32), -1), BF16)

    (ffn1_w_gate, ffn1_w_up, ffn1_w_down, ffn2_w_gate, ffn2_w_up, ffn2_w_down, w_up_nsa, w_up_sb, w_up_swa,
     w_out) = (w.astype(BF16) for w in (ffn1_w_gate, ffn1_w_up, ffn1_w_down, ffn2_w_gate, ffn2_w_up, ffn2_w_down,
                                        w_up_nsa, w_up_sb, w_up_swa, w_out))
    tail0 = OFF_NGATE + N_GATE
    w_head, w_tail = w_in[:, :, :OFF_NGATE].astype(BF16), w_in[:, :, tail0:].astype(BF16)
    w_bgate = w_tail[:, :, OFF_BGATE - tail0:]
    w_gate = jnp.pad(w_in[:, :, OFF_NGATE:tail0], ((0, 0), (0, 0), (0, LANES - N_GATE))).astype(BF16)
    runs = []
    for _, off in ROW_HEADS + COL_HEADS:
        if runs and runs[-1][1] == off:
            runs[-1][1] = off + HEAD_DIM
        else:
            runs.append([off, off + HEAD_DIM])
    pieces = [w_head[:, :, a:b] if a < OFF_NGATE else w_tail[:, :, a - tail0:b - tail0] for a, b in runs]
    w_proj = jnp.concatenate(pieces + [w_gate], axis=2)

    x2 = x.reshape(T, D)
    for l in range(depth):
        norm_of = {"nq": nsa_q_norm[l], "nks": nsa_k_norm[l], "nkw": nsa_k_norm[l],
                   "wq": swa_q_norm[l], "wk": swa_k_norm[l]}
        gain = jnp.concatenate([_head_gains(ROW_HEADS, norm_of, scale_of), _head_gains(COL_HEADS, norm_of, scale_of),
                                jnp.ones((1, LANES), F32)], axis=1)
        x2, rowmat, colmat, gates = _ffn_proj(x2, ffn1_norm[l], ffn1_w_gate, ffn1_w_up, ffn1_w_down,
                                              mix_norm[l], w_proj, gain, plan, l)
        gates = gates[:N_GATE].reshape(N_BRANCHES, NSA_KV_HEADS, NSA_REP, T)

        rows4 = rowmat[ROW_KC:ROW_KC + 2 * NSA_KV_HEADS].reshape(2 * NSA_KV_HEADS, B, nc, CMP_STRIDE * HEAD_DIM)
        cmp_rows, cmp_cols = _compress(rows4, jnp.stack([nsa_cmp_k_w1[l], nsa_cmp_v_w1[l]]),
                                       jnp.stack([nsa_cmp_k_w2[l], nsa_cmp_v_w2[l]]),
                                       nsa_cmp_pos[l], nsa_k_norm[l])

        y_nsa = _nsa_attention(colmat, rowmat, cmp_rows, cmp_cols, gates, cbias, sbias, wbias, ov_t, msel, B, S)
        y_sb = _sb_attention(colmat, rowmat, tri, B, S)
        sinks = jnp.repeat(swa_sinks[l].astype(F32).reshape(SWA_KV_HEADS, 1, SWA_REP), Q_BLOCK, axis=2)
        y_swa = _swa_attention(colmat, rowmat, swa_bias, sinks, B, S)

        x2 = _merge_ffn(x2, mix_norm[l], w_bgate, y_nsa.reshape(T, -1), y_sb.reshape(T, -1),
                        y_swa.reshape(T, -1), w_up_nsa, w_up_sb, w_up_swa, w_out,
                        ffn2_norm[l], ffn2_w_gate, ffn2_w_up, ffn2_w_down, l)
    return x2.reshape(B, S, D)
```
